```python
import math
import jax, jax.numpy as jnp
from jax import lax
import numpy as np

D_MODEL = 1024
BATCH = 8
SEQ = 4096
DEPTH = 4

ATTN_GROUPS = ((128, 1), (512, 4), (2048, 16))
N_ATTN_GROUPS = len(ATTN_GROUPS)
HEADS_PER_GROUP = 6
ATTN_HEADS = N_ATTN_GROUPS * HEADS_PER_GROUP
HEAD_DIM = 64
ATTN_WIDTH = ATTN_HEADS * HEAD_DIM
ATTN_OUT_WIDTH = HEADS_PER_GROUP * HEAD_DIM
REL_BUCKETS = 32
REL_MAX_DISTANCE = 2048
POOL_WINDOWS = (2, 4, 8, 16)
POOL_WIDTH = D_MODEL
POOL_GROUP = POOL_WIDTH // len(POOL_WINDOWS)
SSD_INNER = D_MODEL
SSD_HEAD_DIM = 64
SSD_HEADS = SSD_INNER // SSD_HEAD_DIM
SSD_GROUPS = 2
SSD_STATE = 128
SSD_CONV = 4
SSD_CHUNK = 128
SSD_XBC = SSD_INNER + 2 * SSD_GROUPS * SSD_STATE
D_FF = 2816
FFN_CONV = 3
N_BRANCH = 3
NORM_EPS = 1e-6
IN_SPLIT_SIZES = (ATTN_WIDTH, ATTN_WIDTH, ATTN_WIDTH, POOL_WIDTH, SSD_INNER, SSD_XBC, SSD_HEADS, N_BRANCH * D_MODEL)
IN_WIDTH = sum(IN_SPLIT_SIZES)

kernel_name = "hybrid_gated_dilattn_pool_ssd_trunk"


def rmsnorm(x, g):
    xf = x.astype(jnp.float32)
    y = xf * lax.rsqrt(jnp.mean(xf * xf, axis=-1, keepdims=True) + NORM_EPS)
    return (y * g.astype(jnp.float32)).astype(x.dtype)


def causal_dwconv(x, w, b):
    K = w.shape[0]
    s = x.shape[1]
    xp = jnp.pad(x, ((0, 0), (K - 1, 0), (0, 0)))
    y = xp[:, 0:s] * w[0]
    for k in range(1, K):
        y = y + xp[:, k:k + s] * w[k]
    return y + b


def t5_bucket(dist):
    max_exact = REL_BUCKETS // 2
    is_small = dist < max_exact
    nf = jnp.maximum(dist, 1).astype(jnp.float32)
    large = max_exact + (jnp.log(nf / max_exact) / math.log(REL_MAX_DISTANCE / max_exact)
                         * (REL_BUCKETS - max_exact)).astype(jnp.int32)
    large = jnp.minimum(large, REL_BUCKETS - 1)
    return jnp.where(is_small, dist, large)


def dilated_group_attention(q, k, v, bias_table, dilation, steps):
    b, s, h, dh = q.shape
    L = s // dilation
    W = steps
    nb = -(-L // W)
    Lp = nb * W
    bd = b * dilation

    def to_res(t):
        return t.reshape(b, L, dilation, h, dh).transpose(0, 2, 3, 1, 4).reshape(bd, h, L, dh)

    qb = jnp.pad(to_res(q), ((0, 0), (0, 0), (0, Lp - L), (0, 0))).reshape(bd, h, nb, W, dh)
    kr = jnp.pad(to_res(k), ((0, 0), (0, 0), (W, Lp - L), (0, 0))).reshape(bd, h, nb + 1, W, dh)
    vr = jnp.pad(to_res(v), ((0, 0), (0, 0), (W, Lp - L), (0, 0))).reshape(bd, h, nb + 1, W, dh)
    kb = jnp.concatenate([kr[:, :, :-1], kr[:, :, 1:]], axis=3)
    vb = jnp.concatenate([vr[:, :, :-1], vr[:, :, 1:]], axis=3)

    qi = jnp.arange(W)[:, None]
    kk = jnp.arange(2 * W)[None, :]
    rel = qi + W - kk
    band = (rel >= 0) & (rel <= W)
    kabs = jnp.arange(nb)[:, None, None] * W + kk[None] - W
    valid = band[None] & (kabs >= 0)
    bucket = t5_bucket(jnp.clip(rel, 0, None) * dilation)
    bias = bias_table[bucket].transpose(2, 0, 1).astype(jnp.float32)

    scale = 1.0 / math.sqrt(dh)
    scores = jnp.einsum("bhnqd,bhnkd->bhnqk", qb, kb).astype(jnp.float32) * scale + bias[:, None]
    scores = jnp.where(valid[None, None], scores, -jnp.inf)
    m = jnp.max(scores, axis=-1, keepdims=True)
    p = jnp.exp(scores - m)
    l = jnp.sum(p, axis=-1)
    o = jnp.einsum("bhnqk,bhnkd->bhnqd", p, vb.astype(jnp.float32)) / l[..., None]
    lse = m[..., 0] + jnp.log(l)

    o = o.reshape(bd, h, Lp, dh)[:, :, :L].reshape(b, dilation, h, L, dh)
    o = o.transpose(0, 3, 1, 2, 4).reshape(b, s, h, dh)
    lse = lse.reshape(bd, h, Lp)[:, :, :L].reshape(b, dilation, h, L)
    lse = lse.transpose(0, 3, 1, 2).reshape(b, s, h)
    return o, lse


def dilated_attention_mixer(q, k, v, rel_bias):
    b, s = q.shape[:2]
    outs, lses = [], []
    for gi, (window, dilation) in enumerate(ATTN_GROUPS):
        hs = slice(gi * HEADS_PER_GROUP, (gi + 1) * HEADS_PER_GROUP)
        o, lse = dilated_group_attention(q[:, :, hs], k[:, :, hs], v[:, :, hs],
                                         rel_bias[:, hs], dilation, window // dilation)
        outs.append(o)
        lses.append(lse)
    o = jnp.stack(outs, axis=0)
    alpha = jax.nn.softmax(jnp.stack(lses, axis=0), axis=0)
    o = jnp.sum(alpha[..., None] * o, axis=0)
    return o.reshape(b, s, ATTN_OUT_WIDTH).astype(q.dtype)


def pool_mixer(u, w_grp, scale):
    b, s, _ = u.shape
    uf = u.astype(jnp.float32)
    cs = jnp.cumsum(uf, axis=1)
    pos = (jnp.arange(s) + 1)[None, :, None]
    outs = []
    for gi, w in enumerate(POOL_WINDOWS):
        sl = slice(gi * POOL_GROUP, (gi + 1) * POOL_GROUP)
        c = cs[..., sl]
        shifted = jnp.pad(c, ((0, 0), (w, 0), (0, 0)))[:, :s]
        cnt = jnp.minimum(pos, w).astype(jnp.float32)
        outs.append((c - shifted) / cnt - uf[..., sl])
    d = jnp.stack(outs, axis=2).astype(u.dtype)
    y = jnp.einsum("bsgc,gcd->bsgd", d, w_grp).reshape(b, s, POOL_WIDTH)
    return y * scale


def ssd_scan(x, dt, A, Bm, Cm):
    b, s, h, p = x.shape
    g, n = Bm.shape[2:]
    e = h // g
    l = SSD_CHUNK
    c = s // l
    xc = (x * dt[..., None]).reshape(b, c, l, g, e, p)
    a = (dt * A).reshape(b, c, l, h).transpose(0, 3, 1, 2)
    a_cs = jnp.cumsum(a, axis=-1)
    Bc = Bm.reshape(b, c, l, g, n)
    Cc = Cm.reshape(b, c, l, g, n)
    causal = jnp.tril(jnp.ones((l, l), dtype=bool))
    seg = a_cs[..., :, None] - a_cs[..., None, :]
    Lmat = jnp.exp(jnp.where(causal, seg, -jnp.inf)).reshape(b, g, e, c, l, l)
    cb = jnp.einsum("bclgn,bcsgn->bcgls", Cc, Bc)
    y_diag = jnp.einsum("bcgls,bgecls,bcsgep->bclgep", cb, Lmat, xc)
    decay = jnp.exp(a_cs[..., -1:] - a_cs).reshape(b, g, e, c, l)
    states = jnp.einsum("bclgn,bgecl,bclgep->bcgepn", Bc, decay, xc)
    chunk_decay = jnp.exp(a_cs[..., -1]).reshape(b, g, e, c)

    def step(carry, inp):
        st, dec = inp
        return carry * dec[..., None, None] + st, carry

    init = jnp.zeros((b, g, e, p, n), dtype=x.dtype)
    _, prev = lax.scan(step, init, (states.transpose(1, 0, 2, 3, 4, 5), chunk_decay.transpose(3, 0, 1, 2)))
    prev = prev.transpose(1, 0, 2, 3, 4, 5)
    out_decay = jnp.exp(a_cs).reshape(b, g, e, c, l)
    y_off = jnp.einsum("bclgn,bcgepn,bgecl->bclgep", Cc, prev, out_decay)
    return (y_diag + y_off).reshape(b, s, h, p)


def ssd_mixer(z, xbc, dt_raw, conv_w, conv_b, dt_bias, a_log, d_skip, norm_w):
    b, s, _ = z.shape
    xbc = jax.nn.silu(causal_dwconv(xbc, conv_w, conv_b))
    xs = xbc[..., :SSD_INNER].reshape(b, s, SSD_HEADS, SSD_HEAD_DIM).astype(jnp.float32)
    Bm = xbc[..., SSD_INNER:SSD_INNER + SSD_GROUPS * SSD_STATE].reshape(b, s, SSD_GROUPS, SSD_STATE).astype(jnp.float32)
    Cm = xbc[..., SSD_INNER + SSD_GROUPS * SSD_STATE:].reshape(b, s, SSD_GROUPS, SSD_STATE).astype(jnp.float32)
    dt = jax.nn.softplus(dt_raw.astype(jnp.float32) + dt_bias.astype(jnp.float32))
    A = -jnp.exp(a_log.astype(jnp.float32))
    y = ssd_scan(xs, dt, A, Bm, Cm) + d_skip.astype(jnp.float32)[:, None] * xs
    y = y.reshape(b, s, SSD_INNER) * jax.nn.silu(z.astype(jnp.float32))
    yg = y.reshape(b, s, SSD_GROUPS, SSD_INNER // SSD_GROUPS)
    yg = yg * lax.rsqrt(jnp.mean(yg * yg, axis=-1, keepdims=True) + NORM_EPS)
    y = yg.reshape(b, s, SSD_INNER) * norm_w.astype(jnp.float32)
    return y.astype(z.dtype)


def conv_ffn(u, w_up, conv_w, conv_b, w_down):
    h = causal_dwconv(u @ w_up, conv_w, conv_b)
    a, v = h[..., :D_FF], h[..., D_FF:]
    return (jax.nn.silu(a) * v) @ w_down


def _fwd_setup_inputs(seed: int = 0) -> dict:
    key = jax.random.key(seed)
    ks = jax.random.split(key, 24)
    f32 = jnp.float32

    def nrm(k, shape, scale):
        return jax.random.normal(k, shape, dtype=f32) * scale

    dt0 = jnp.exp(jax.random.uniform(ks[10], (DEPTH, SSD_HEADS), dtype=f32)
                  * (math.log(0.1) - math.log(0.001)) + math.log(0.001))
    return {
        "x": nrm(ks[0], (BATCH, SEQ, D_MODEL), 1.0),
        "rel_bias": nrm(ks[1], (REL_BUCKETS, ATTN_HEADS), 0.1),
        "ln1_g": 1.0 + nrm(ks[2], (DEPTH, D_MODEL), 0.02),
        "w_in": nrm(ks[3], (DEPTH, D_MODEL, IN_WIDTH), D_MODEL ** -0.5),
        "b_gate": nrm(ks[4], (DEPTH, N_BRANCH * D_MODEL), 0.02),
        "w_a": nrm(ks[5], (DEPTH, ATTN_OUT_WIDTH, D_MODEL), ATTN_OUT_WIDTH ** -0.5),
        "pool_w": nrm(ks[6], (DEPTH, len(POOL_WINDOWS), POOL_GROUP, POOL_GROUP), POOL_GROUP ** -0.5),
        "pool_scale": 1.0 + nrm(ks[7], (DEPTH, POOL_WIDTH), 0.1),
        "w_b": nrm(ks[8], (DEPTH, POOL_WIDTH, D_MODEL), POOL_WIDTH ** -0.5),
        "ssd_conv_w": nrm(ks[9], (DEPTH, SSD_CONV, SSD_XBC), SSD_CONV ** -0.5),
        "ssd_conv_b": nrm(ks[11], (DEPTH, SSD_XBC), 0.02),
        "ssd_dt_bias": dt0 + jnp.log(-jnp.expm1(-dt0)),
        "ssd_a_log": jnp.log(jax.random.uniform(ks[12], (DEPTH, SSD_HEADS), dtype=f32, minval=1.0, maxval=16.0)),
        "ssd_d": 1.0 + nrm(ks[13], (DEPTH, SSD_HEADS), 0.1),
        "ssd_norm_w": 1.0 + nrm(ks[14], (DEPTH, SSD_INNER), 0.02),
        "w_c": nrm(ks[15], (DEPTH, SSD_INNER, D_MODEL), SSD_INNER ** -0.5),
        "w_o": nrm(ks[16], (DEPTH, D_MODEL, D_MODEL), D_MODEL ** -0.5),
        "ln2_g": 1.0 + nrm(ks[17], (DEPTH, D_MODEL), 0.02),
        "ffn_w_up": nrm(ks[18], (DEPTH, D_MODEL, 2 * D_FF), D_MODEL ** -0.5),
        "ffn_conv_w": nrm(ks[19], (DEPTH, FFN_CONV, 2 * D_FF), FFN_CONV ** -0.5),
        "ffn_conv_b": nrm(ks[20], (DEPTH, 2 * D_FF), 0.02),
        "ffn_w_down": nrm(ks[21], (DEPTH, D_FF, D_MODEL), D_FF ** -0.5),
        "final_g": 1.0 + nrm(ks[22], (D_MODEL,), 0.02),
    }


def _fwd_reference(x, rel_bias, ln1_g, w_in, b_gate, w_a, pool_w, pool_scale, w_b,
              ssd_conv_w, ssd_conv_b, ssd_dt_bias, ssd_a_log, ssd_d, ssd_norm_w, w_c,
              w_o, ln2_g, ffn_w_up, ffn_conv_w, ffn_conv_b, ffn_w_down, final_g):
    b, s, _ = x.shape
    split_idx = []
    acc = 0
    for sz in IN_SPLIT_SIZES[:-1]:
        acc += sz
        split_idx.append(acc)
    for i in range(DEPTH):
        u = rmsnorm(x, ln1_g[i])
        proj = u @ w_in[i]
        q, k, v, pool_in, z, xbc, dt_raw, gate_pre = jnp.split(proj, split_idx, axis=-1)
        q = q.reshape(b, s, ATTN_HEADS, HEAD_DIM)
        k = k.reshape(b, s, ATTN_HEADS, HEAD_DIM)
        v = v.reshape(b, s, ATTN_HEADS, HEAD_DIM)
        y_a = dilated_attention_mixer(q, k, v, rel_bias) @ w_a[i]
        y_b = pool_mixer(pool_in, pool_w[i], pool_scale[i]) @ w_b[i]
        y_c = ssd_mixer(z, xbc, dt_raw, ssd_conv_w[i], ssd_conv_b[i], ssd_dt_bias[i],
                        ssd_a_log[i], ssd_d[i], ssd_norm_w[i]) @ w_c[i]
        gates = jax.nn.sigmoid(gate_pre + b_gate[i]).reshape(b, s, N_BRANCH, D_MODEL)
        merged = gates[:, :, 0] * y_a + gates[:, :, 1] * y_b + gates[:, :, 2] * y_c
        x = x + merged @ w_o[i]
        x = x + conv_ffn(rmsnorm(x, ln2_g[i]), ffn_w_up[i], ffn_conv_w[i], ffn_conv_b[i], ffn_w_down[i])
    return rmsnorm(x, final_g)


import jax as _jax
import jax.numpy as _jnp

TWIN_FORMAT = 'train_step'
FWD_PARAMS = ['x', 'rel_bias', 'ln1_g', 'w_in', 'b_gate', 'w_a', 'pool_w', 'pool_scale', 'w_b', 'ssd_conv_w', 'ssd_conv_b', 'ssd_dt_bias', 'ssd_a_log', 'ssd_d', 'ssd_norm_w', 'w_c', 'w_o', 'ln2_g', 'ffn_w_up', 'ffn_conv_w', 'ffn_conv_b', 'ffn_w_down', 'final_g']
TWIN_WEIGHTS = ['rel_bias', 'ln1_g', 'w_in', 'b_gate', 'w_a', 'pool_w', 'pool_scale', 'w_b', 'ssd_conv_w', 'ssd_conv_b', 'ssd_dt_bias', 'ssd_a_log', 'ssd_d', 'ssd_norm_w', 'w_c', 'w_o', 'ln2_g', 'ffn_w_up', 'ffn_conv_w', 'ffn_conv_b', 'ffn_w_down', 'final_g']
TWIN_DIFF_INPUT = 'x'
TWIN_INPUTS = ['x', 'rel_bias', 'ln1_g', 'w_in', 'b_gate', 'w_a', 'pool_w', 'pool_scale', 'w_b', 'ssd_conv_w', 'ssd_conv_b', 'ssd_dt_bias', 'ssd_a_log', 'ssd_d', 'ssd_norm_w', 'w_c', 'w_o', 'ln2_g', 'ffn_w_up', 'ffn_conv_w', 'ffn_conv_b', 'ffn_w_down', 'final_g', 'loss_target', 'm_rel_bias', 'm_ln1_g', 'm_w_in', 'm_b_gate', 'm_w_a', 'm_pool_w', 'm_pool_scale', 'm_w_b', 'm_ssd_conv_w', 'm_ssd_conv_b', 'm_ssd_dt_bias', 'm_ssd_a_log', 'm_ssd_d', 'm_ssd_norm_w', 'm_w_c', 'm_w_o', 'm_ln2_g', 'm_ffn_w_up', 'm_ffn_conv_w', 'm_ffn_conv_b', 'm_ffn_w_down', 'm_final_g', 'v_rel_bias', 'v_ln1_g', 'v_w_in', 'v_b_gate', 'v_w_a', 'v_pool_w', 'v_pool_scale', 'v_w_b', 'v_ssd_conv_w', 'v_ssd_conv_b', 'v_ssd_dt_bias', 'v_ssd_a_log', 'v_ssd_d', 'v_ssd_norm_w', 'v_w_c', 'v_w_o', 'v_ln2_g', 'v_ffn_w_up', 'v_ffn_conv_w', 'v_ffn_conv_b', 'v_ffn_w_down', 'v_final_g']
TWIN_OUTPUTS = ['loss', 'grad_x', 'grad_rel_bias', 'grad_ln1_g', 'grad_w_in', 'grad_b_gate', 'grad_w_a', 'grad_pool_w', 'grad_pool_scale', 'grad_w_b', 'grad_ssd_conv_w', 'grad_ssd_conv_b', 'grad_ssd_dt_bias', 'grad_ssd_a_log', 'grad_ssd_d', 'grad_ssd_norm_w', 'grad_w_c', 'grad_w_o', 'grad_ln2_g', 'grad_ffn_w_up', 'grad_ffn_conv_w', 'grad_ffn_conv_b', 'grad_ffn_w_down', 'grad_final_g', 'delta_rel_bias', 'delta_ln1_g', 'delta_w_in', 'delta_b_gate', 'delta_w_a', 'delta_pool_w', 'delta_pool_scale', 'delta_w_b', 'delta_ssd_conv_w', 'delta_ssd_conv_b', 'delta_ssd_dt_bias', 'delta_ssd_a_log', 'delta_ssd_d', 'delta_ssd_norm_w', 'delta_w_c', 'delta_w_o', 'delta_ln2_g', 'delta_ffn_w_up', 'delta_ffn_conv_w', 'delta_ffn_conv_b', 'delta_ffn_w_down', 'delta_final_g', 'new_m_rel_bias', 'new_m_ln1_g', 'new_m_w_in', 'new_m_b_gate', 'new_m_w_a', 'new_m_pool_w', 'new_m_pool_scale', 'new_m_w_b', 'new_m_ssd_conv_w', 'new_m_ssd_conv_b', 'new_m_ssd_dt_bias', 'new_m_ssd_a_log', 'new_m_ssd_d', 'new_m_ssd_norm_w', 'new_m_w_c', 'new_m_w_o', 'new_m_ln2_g', 'new_m_ffn_w_up', 'new_m_ffn_conv_w', 'new_m_ffn_conv_b', 'new_m_ffn_w_down', 'new_m_final_g', 'new_v_rel_bias', 'new_v_ln1_g', 'new_v_w_in', 'new_v_b_gate', 'new_v_w_a', 'new_v_pool_w', 'new_v_pool_scale', 'new_v_w_b', 'new_v_ssd_conv_w', 'new_v_ssd_conv_b', 'new_v_ssd_dt_bias', 'new_v_ssd_a_log', 'new_v_ssd_d', 'new_v_ssd_norm_w', 'new_v_w_c', 'new_v_w_o', 'new_v_ln2_g', 'new_v_ffn_w_up', 'new_v_ffn_conv_w', 'new_v_ffn_conv_b', 'new_v_ffn_w_down', 'new_v_final_g']
TWIN_LEAF_KINDS = {'loss': 'loss', 'grad_x': 'grad_x', 'grad_rel_bias': 'grad_w', 'grad_ln1_g': 'grad_w', 'grad_w_in': 'grad_w', 'grad_b_gate': 'grad_w', 'grad_w_a': 'grad_w', 'grad_pool_w': 'grad_w', 'grad_pool_scale': 'grad_w', 'grad_w_b': 'grad_w', 'grad_ssd_conv_w': 'grad_w', 'grad_ssd_conv_b': 'grad_w', 'grad_ssd_dt_bias': 'grad_w', 'grad_ssd_a_log': 'grad_w', 'grad_ssd_d': 'grad_w', 'grad_ssd_norm_w': 'grad_w', 'grad_w_c': 'grad_w', 'grad_w_o': 'grad_w', 'grad_ln2_g': 'grad_w', 'grad_ffn_w_up': 'grad_w', 'grad_ffn_conv_w': 'grad_w', 'grad_ffn_conv_b': 'grad_w', 'grad_ffn_w_down': 'grad_w', 'grad_final_g': 'grad_w', 'delta_rel_bias': 'delta_w', 'delta_ln1_g': 'delta_w', 'delta_w_in': 'delta_w', 'delta_b_gate': 'delta_w', 'delta_w_a': 'delta_w', 'delta_pool_w': 'delta_w', 'delta_pool_scale': 'delta_w', 'delta_w_b': 'delta_w', 'delta_ssd_conv_w': 'delta_w', 'delta_ssd_conv_b': 'delta_w', 'delta_ssd_dt_bias': 'delta_w', 'delta_ssd_a_log': 'delta_w', 'delta_ssd_d': 'delta_w', 'delta_ssd_norm_w': 'delta_w', 'delta_w_c': 'delta_w', 'delta_w_o': 'delta_w', 'delta_ln2_g': 'delta_w', 'delta_ffn_w_up': 'delta_w', 'delta_ffn_conv_w': 'delta_w', 'delta_ffn_conv_b': 'delta_w', 'delta_ffn_w_down': 'delta_w', 'delta_final_g': 'delta_w', 'new_m_rel_bias': 'new_m', 'new_m_ln1_g': 'new_m', 'new_m_w_in': 'new_m', 'new_m_b_gate': 'new_m', 'new_m_w_a': 'new_m', 'new_m_pool_w': 'new_m', 'new_m_pool_scale': 'new_m', 'new_m_w_b': 'new_m', 'new_m_ssd_conv_w': 'new_m', 'new_m_ssd_conv_b': 'new_m', 'new_m_ssd_dt_bias': 'new_m', 'new_m_ssd_a_log': 'new_m', 'new_m_ssd_d': 'new_m', 'new_m_ssd_norm_w': 'new_m', 'new_m_w_c': 'new_m', 'new_m_w_o': 'new_m', 'new_m_ln2_g': 'new_m', 'new_m_ffn_w_up': 'new_m', 'new_m_ffn_conv_w': 'new_m', 'new_m_ffn_conv_b': 'new_m', 'new_m_ffn_w_down': 'new_m', 'new_m_final_g': 'new_m', 'new_v_rel_bias': 'new_v', 'new_v_ln1_g': 'new_v', 'new_v_w_in': 'new_v', 'new_v_b_gate': 'new_v', 'new_v_w_a': 'new_v', 'new_v_pool_w': 'new_v', 'new_v_pool_scale': 'new_v', 'new_v_w_b': 'new_v', 'new_v_ssd_conv_w': 'new_v', 'new_v_ssd_conv_b': 'new_v', 'new_v_ssd_dt_bias': 'new_v', 'new_v_ssd_a_log': 'new_v', 'new_v_ssd_d': 'new_v', 'new_v_ssd_norm_w': 'new_v', 'new_v_w_c': 'new_v', 'new_v_w_o': 'new_v', 'new_v_ln2_g': 'new_v', 'new_v_ffn_w_up': 'new_v', 'new_v_ffn_conv_w': 'new_v', 'new_v_ffn_conv_b': 'new_v', 'new_v_ffn_w_down': 'new_v', 'new_v_final_g': 'new_v'}


def _forward(args):
    return _fwd_reference(*[args[k] for k in FWD_PARAMS])


def _output_shape():
    out = _jax.eval_shape(lambda: _forward(_fwd_setup_inputs(0)))
    return out.shape, out.dtype

N_MICROBATCH = 1
ADAM_LR = 0.001
ADAM_B1 = 0.9
ADAM_B2 = 0.999
ADAM_EPS = 1e-08
ADAM_WD = 0.01
ADAM_STEP = 10
PER_EXAMPLE_BATCH_AXIS = {'x': 0, 'loss_target': 0}
SHARED_INPUTS = []
_WEIGHT_DTYPES = {'rel_bias': _jnp.float32, 'ln1_g': _jnp.float32, 'w_in': _jnp.float32, 'b_gate': _jnp.float32, 'w_a': _jnp.float32, 'pool_w': _jnp.float32, 'pool_scale': _jnp.float32, 'w_b': _jnp.float32, 'ssd_conv_w': _jnp.float32, 'ssd_conv_b': _jnp.float32, 'ssd_dt_bias': _jnp.float32, 'ssd_a_log': _jnp.float32, 'ssd_d': _jnp.float32, 'ssd_norm_w': _jnp.float32, 'w_c': _jnp.float32, 'w_o': _jnp.float32, 'ln2_g': _jnp.float32, 'ffn_w_up': _jnp.float32, 'ffn_conv_w': _jnp.float32, 'ffn_conv_b': _jnp.float32, 'ffn_w_down': _jnp.float32, 'final_g': _jnp.float32}
MOMENT_SCALE = {'rel_bias': 3.611257e-02, 'ln1_g': 1.548339e-01, 'w_in': 4.919013e-02, 'b_gate': 2.535523e-02, 'w_a': 1.677669e-02, 'pool_w': 7.375222e-02, 'pool_scale': 7.223993e-02, 'w_b': 7.365205e-02, 'ssd_conv_w': 7.392070e-02, 'ssd_conv_b': 9.476564e-02, 'ssd_dt_bias': 1.670681e-01, 'ssd_a_log': 2.072830e-01, 'ssd_d': 4.890996e-01, 'ssd_norm_w': 8.676132e-02, 'w_c': 8.363199e-02, 'w_o': 1.121843e-01, 'ln2_g': 1.217992e-01, 'ffn_w_up': 4.913624e-02, 'ffn_conv_w': 4.923282e-02, 'ffn_conv_b': 4.846094e-02, 'ffn_w_down': 8.077995e-02, 'final_g': 3.199267e+01}


def _to_microbatches(a, axis):
    t = _jnp.moveaxis(a, axis, 0)
    t = t.reshape((N_MICROBATCH, t.shape[0] // N_MICROBATCH) + t.shape[1:])
    return _jnp.moveaxis(t, 1, axis + 1)


def setup_inputs(seed: int = 0) -> dict:
    inp = _fwd_setup_inputs(seed)
    key = _jax.random.fold_in(_jax.random.key(seed), 7919)
    shape, _ = _output_shape()
    out = dict(inp)
    out["loss_target"] = _jax.random.normal(_jax.random.fold_in(key, 0), shape, _jnp.float32)
    for i, name in enumerate(TWIN_WEIGHTS):
        w = inp[name].astype(_jnp.float32)
        if MOMENT_SCALE is None:
            s = _jnp.sqrt(_jnp.mean(_jnp.square(w)) + 1e-30)
        else:
            s = MOMENT_SCALE[name]
        km, kv = _jax.random.split(_jax.random.fold_in(key, i + 1))
        out[name] = w
        out["m_" + name] = s * _jax.random.normal(km, w.shape, _jnp.float32)
        out["v_" + name] = (s * s) * _jax.random.uniform(kv, w.shape, _jnp.float32, 0.5, 1.5)
    if N_MICROBATCH > 1:
        for name, axis in PER_EXAMPLE_BATCH_AXIS.items():
            out[name] = _to_microbatches(out[name], axis)
    return {'x': out['x'], 'rel_bias': out['rel_bias'], 'ln1_g': out['ln1_g'], 'w_in': out['w_in'], 'b_gate': out['b_gate'], 'w_a': out['w_a'], 'pool_w': out['pool_w'], 'pool_scale': out['pool_scale'], 'w_b': out['w_b'], 'ssd_conv_w': out['ssd_conv_w'], 'ssd_conv_b': out['ssd_conv_b'], 'ssd_dt_bias': out['ssd_dt_bias'], 'ssd_a_log': out['ssd_a_log'], 'ssd_d': out['ssd_d'], 'ssd_norm_w': out['ssd_norm_w'], 'w_c': out['w_c'], 'w_o': out['w_o'], 'ln2_g': out['ln2_g'], 'ffn_w_up': out['ffn_w_up'], 'ffn_conv_w': out['ffn_conv_w'], 'ffn_conv_b': out['ffn_conv_b'], 'ffn_w_down': out['ffn_w_down'], 'final_g': out['final_g'], 'loss_target': out['loss_target'], 'm_rel_bias': out['m_rel_bias'], 'm_ln1_g': out['m_ln1_g'], 'm_w_in': out['m_w_in'], 'm_b_gate': out['m_b_gate'], 'm_w_a': out['m_w_a'], 'm_pool_w': out['m_pool_w'], 'm_pool_scale': out['m_pool_scale'], 'm_w_b': out['m_w_b'], 'm_ssd_conv_w': out['m_ssd_conv_w'], 'm_ssd_conv_b': out['m_ssd_conv_b'], 'm_ssd_dt_bias': out['m_ssd_dt_bias'], 'm_ssd_a_log': out['m_ssd_a_log'], 'm_ssd_d': out['m_ssd_d'], 'm_ssd_norm_w': out['m_ssd_norm_w'], 'm_w_c': out['m_w_c'], 'm_w_o': out['m_w_o'], 'm_ln2_g': out['m_ln2_g'], 'm_ffn_w_up': out['m_ffn_w_up'], 'm_ffn_conv_w': out['m_ffn_conv_w'], 'm_ffn_conv_b': out['m_ffn_conv_b'], 'm_ffn_w_down': out['m_ffn_w_down'], 'm_final_g': out['m_final_g'], 'v_rel_bias': out['v_rel_bias'], 'v_ln1_g': out['v_ln1_g'], 'v_w_in': out['v_w_in'], 'v_b_gate': out['v_b_gate'], 'v_w_a': out['v_w_a'], 'v_pool_w': out['v_pool_w'], 'v_pool_scale': out['v_pool_scale'], 'v_w_b': out['v_w_b'], 'v_ssd_conv_w': out['v_ssd_conv_w'], 'v_ssd_conv_b': out['v_ssd_conv_b'], 'v_ssd_dt_bias': out['v_ssd_dt_bias'], 'v_ssd_a_log': out['v_ssd_a_log'], 'v_ssd_d': out['v_ssd_d'], 'v_ssd_norm_w': out['v_ssd_norm_w'], 'v_w_c': out['v_w_c'], 'v_w_o': out['v_w_o'], 'v_ln2_g': out['v_ln2_g'], 'v_ffn_w_up': out['v_ffn_w_up'], 'v_ffn_conv_w': out['v_ffn_conv_w'], 'v_ffn_conv_b': out['v_ffn_conv_b'], 'v_ffn_w_down': out['v_ffn_w_down'], 'v_final_g': out['v_final_g']}


def _loss(weights, diff, rest, loss_target):
    with _jax.named_scope("forward"):
        args = {**rest, TWIN_DIFF_INPUT: diff, **{k: w.astype(_WEIGHT_DTYPES[k]) for k, w in weights.items()}}
        y = _forward(args)
    with _jax.named_scope("loss_head"):
        err = _jnp.square(y.astype(_jnp.float32) - loss_target)
        return 0.5 * _jnp.sum(_jnp.mean(err, axis=-1)) if err.ndim else 0.5 * err


def _adamw(w, g, m, v):
    m = ADAM_B1 * m + (1.0 - ADAM_B1) * g
    v = ADAM_B2 * v + (1.0 - ADAM_B2) * _jnp.square(g)
    m_hat = m / (1.0 - ADAM_B1 ** ADAM_STEP)
    v_hat = v / (1.0 - ADAM_B2 ** ADAM_STEP)
    delta = -ADAM_LR * (m_hat / (_jnp.sqrt(v_hat) + ADAM_EPS) + ADAM_WD * w)
    return delta, m, v


def reference(x, rel_bias, ln1_g, w_in, b_gate, w_a, pool_w, pool_scale, w_b, ssd_conv_w, ssd_conv_b, ssd_dt_bias, ssd_a_log, ssd_d, ssd_norm_w, w_c, w_o, ln2_g, ffn_w_up, ffn_conv_w, ffn_conv_b, ffn_w_down, final_g, loss_target, m_rel_bias, m_ln1_g, m_w_in, m_b_gate, m_w_a, m_pool_w, m_pool_scale, m_w_b, m_ssd_conv_w, m_ssd_conv_b, m_ssd_dt_bias, m_ssd_a_log, m_ssd_d, m_ssd_norm_w, m_w_c, m_w_o, m_ln2_g, m_ffn_w_up, m_ffn_conv_w, m_ffn_conv_b, m_ffn_w_down, m_final_g, v_rel_bias, v_ln1_g, v_w_in, v_b_gate, v_w_a, v_pool_w, v_pool_scale, v_w_b, v_ssd_conv_w, v_ssd_conv_b, v_ssd_dt_bias, v_ssd_a_log, v_ssd_d, v_ssd_norm_w, v_w_c, v_w_o, v_ln2_g, v_ffn_w_up, v_ffn_conv_w, v_ffn_conv_b, v_ffn_w_down, v_final_g):
    given = dict(x=x, rel_bias=rel_bias, ln1_g=ln1_g, w_in=w_in, b_gate=b_gate, w_a=w_a, pool_w=pool_w, pool_scale=pool_scale, w_b=w_b, ssd_conv_w=ssd_conv_w, ssd_conv_b=ssd_conv_b, ssd_dt_bias=ssd_dt_bias, ssd_a_log=ssd_a_log, ssd_d=ssd_d, ssd_norm_w=ssd_norm_w, w_c=w_c, w_o=w_o, ln2_g=ln2_g, ffn_w_up=ffn_w_up, ffn_conv_w=ffn_conv_w, ffn_conv_b=ffn_conv_b, ffn_w_down=ffn_w_down, final_g=final_g, loss_target=loss_target, m_rel_bias=m_rel_bias, m_ln1_g=m_ln1_g, m_w_in=m_w_in, m_b_gate=m_b_gate, m_w_a=m_w_a, m_pool_w=m_pool_w, m_pool_scale=m_pool_scale, m_w_b=m_w_b, m_ssd_conv_w=m_ssd_conv_w, m_ssd_conv_b=m_ssd_conv_b, m_ssd_dt_bias=m_ssd_dt_bias, m_ssd_a_log=m_ssd_a_log, m_ssd_d=m_ssd_d, m_ssd_norm_w=m_ssd_norm_w, m_w_c=m_w_c, m_w_o=m_w_o, m_ln2_g=m_ln2_g, m_ffn_w_up=m_ffn_w_up, m_ffn_conv_w=m_ffn_conv_w, m_ffn_conv_b=m_ffn_conv_b, m_ffn_w_down=m_ffn_w_down, m_final_g=m_final_g, v_rel_bias=v_rel_bias, v_ln1_g=v_ln1_g, v_w_in=v_w_in, v_b_gate=v_b_gate, v_w_a=v_w_a, v_pool_w=v_pool_w, v_pool_scale=v_pool_scale, v_w_b=v_w_b, v_ssd_conv_w=v_ssd_conv_w, v_ssd_conv_b=v_ssd_conv_b, v_ssd_dt_bias=v_ssd_dt_bias, v_ssd_a_log=v_ssd_a_log, v_ssd_d=v_ssd_d, v_ssd_norm_w=v_ssd_norm_w, v_w_c=v_w_c, v_w_o=v_w_o, v_ln2_g=v_ln2_g, v_ffn_w_up=v_ffn_w_up, v_ffn_conv_w=v_ffn_conv_w, v_ffn_conv_b=v_ffn_conv_b, v_ffn_w_down=v_ffn_w_down, v_final_g=v_final_g)
    weights = {n: given[n] for n in TWIN_WEIGHTS}
    shared = {n: given[n] for n in SHARED_INPUTS}
    per_example = {n: given[n] for n in ['x']}
    grad_fn = _jax.value_and_grad(_loss, argnums=(0, 1))

    def one_microbatch(ex, loss_target):
        ex = dict(ex)
        diff = ex.pop(TWIN_DIFF_INPUT)
        return grad_fn(weights, diff, {**shared, **ex}, loss_target)

    if N_MICROBATCH == 1:
        loss, (grad_w, grad_x) = one_microbatch(per_example, given["loss_target"])
    else:
        def body(carry, xs):
            loss_sum, grad_sum = carry
            l_k, (gw_k, gx_k) = one_microbatch(xs[0], xs[1])
            with _jax.named_scope("update"):
                return (loss_sum + l_k, _jax.tree.map(_jnp.add, grad_sum, gw_k)), gx_k

        init = (_jnp.zeros((), _jnp.float32), _jax.tree.map(_jnp.zeros_like, weights))
        (loss, grad_w), grad_x = _jax.lax.scan(body, init, (per_example, given["loss_target"]))
    with _jax.named_scope("update"):
        delta_w, new_m, new_v = {}, {}, {}
        for n in TWIN_WEIGHTS:
            delta_w[n], new_m[n], new_v[n] = _adamw(weights[n], grad_w[n], given["m_" + n], given["v_" + n])
    return (loss, grad_x, *[grad_w[n] for n in TWIN_WEIGHTS], *[delta_w[n] for n in TWIN_WEIGHTS],
            *[new_m[n] for n in TWIN_WEIGHTS], *[new_v[n] for n in TWIN_WEIGHTS])
```

```python
import functools
import math

import numpy as np
import jax
import jax.numpy as jnp
from jax import lax
from jax.experimental import pallas as pl
from jax.experimental.pallas import tpu as pltpu

F32 = jnp.float32
BF16 = jnp.bfloat16

N_DEV = 8
MESH_AXES = ("x", "y", "c")
S = 4096
D = 1024
DEPTH = 4
HEAD_DIM = 64
ATT_W = 1152
ATT_GW = 384
ATT_GROUPS = ((128, 1), (512, 4), (2048, 16))
ATT_BLOCK = 128
REL_BUCKETS = 32
REL_MAX_DISTANCE = 2048
POOL_WINDOWS = (2, 4, 8, 16)
SSD_HEADS = 16
SSD_CHUNK = 128
SSD_XBC = 1536
D_FF = 2816
IN_WIDTH = 10128
EPS = 1e-6
NEG = -1e30

OFF_GATE, OFF_POOL, OFF_Z, OFF_XBC, OFF_DT, OFF_Q, OFF_K, OFF_V = 0, 3072, 4096, 5120, 6656, 6912, 8064, 9216
NP = 10368
DT_PAD = 128

ADAM_LR, ADAM_B1, ADAM_B2, ADAM_EPS, ADAM_WD, ADAM_STEP = 0.001, 0.9, 0.999, 1e-08, 0.01, 10

VMEM_LIMIT = 52 * 1024 * 1024


def _cparams(sem=None):
    return pltpu.CompilerParams(dimension_semantics=sem, vmem_limit_bytes=VMEM_LIMIT)


def _dot(a, b, ca, cb):
    return lax.dot_general(a.astype(BF16), b.astype(BF16), (((ca,), (cb,)), ((), ())), preferred_element_type=F32)


@jax.custom_vjp
def _mm(a, b):
    return _dot(a, b, 1, 0)


def _mm_fwd(a, b):
    return _mm(a, b), (a, b)


def _mm_bwd(res, g):
    a, b = res
    return _dot(g, b, 1, 1).astype(a.dtype), _dot(a, g, 0, 0).astype(b.dtype)


_mm.defvjp(_mm_fwd, _mm_bwd)


@jax.custom_vjp
def _mm_nt(a, b):
    return _dot(a, b, 1, 1)


def _mm_nt_fwd(a, b):
    return _mm_nt(a, b), (a, b)


def _mm_nt_bwd(res, g):
    a, b = res
    return _dot(g, b, 1, 0).astype(a.dtype), _dot(g, a, 0, 0).astype(b.dtype)


_mm_nt.defvjp(_mm_nt_fwd, _mm_nt_bwd)


@jax.custom_vjp
def _mm_tn(a, b):
    return _dot(a, b, 0, 0)


def _mm_tn_fwd(a, b):
    return _mm_tn(a, b), (a, b)


def _mm_tn_bwd(res, g):
    a, b = res
    return _dot(b, g, 1, 1).astype(a.dtype), _dot(a, g, 1, 0).astype(b.dtype)


_mm_tn.defvjp(_mm_tn_fwd, _mm_tn_bwd)


def _shift_impl(x, j):
    n = x.shape[0]
    if j == 0:
        return x
    r = pltpu.roll(x, j % n, axis=0)
    t = lax.broadcasted_iota(jnp.int32, x.shape, 0)
    mask = (t >= j) if j > 0 else (t < n + j)
    return jnp.where(mask, r, 0.0)


@functools.partial(jax.custom_vjp, nondiff_argnums=(1,))
def _shift(x, j):
    return _shift_impl(x, j)


_shift.defvjp(lambda x, j: (_shift_impl(x, j), None), lambda j, _, g: (_shift_impl(g, -j),))


def _tri(lower):
    r = lax.broadcasted_iota(jnp.int32, (SSD_CHUNK, SSD_CHUNK), 0)
    c = lax.broadcasted_iota(jnp.int32, (SSD_CHUNK, SSD_CHUNK), 1)
    return (r >= c) if lower else (r <= c)


def _dot_hi(a, b):
    return lax.dot_general(a, b, (((1,), (0,)), ((), ())), precision=lax.Precision.HIGHEST,
                           preferred_element_type=F32)


@jax.custom_vjp
def _cumsum_rows(a):
    return _dot_hi(_tri(True).astype(F32), a)


_cumsum_rows.defvjp(lambda a: (_cumsum_rows(a), None), lambda _, g: (_dot_hi(_tri(False).astype(F32), g),))


@jax.custom_vjp
def _softplus(x):
    return jnp.maximum(x, 0.0) + jnp.log(1.0 + jnp.exp(-jnp.abs(x)))


_softplus.defvjp(lambda x: (_softplus(x), x), lambda x, g: (g * jax.nn.sigmoid(x),))


def _silu(x):
    return x * jax.nn.sigmoid(x)


def _rows(arr, cw, off=0, tm=None, lead=None):
    return (arr, cw, off, lead)


def _tiled(name, fn, grid, tm, rows, params, outs, accs=()):
    ncol, nrow = grid
    in_specs, operands = [], []
    for arr, cw, off, lead in rows:
        if lead is None:
            in_specs.append(pl.BlockSpec((tm, cw), functools.partial(lambda j, i, off: (i, off + j), off=off)))
        else:
            in_specs.append(pl.BlockSpec((None, tm, cw),
                                         functools.partial(lambda j, i, off, lead: (lead, i, off + j), off=off, lead=lead)))
        operands.append(arr)
    for arr, bs, im in params:
        in_specs.append(pl.BlockSpec(bs, im))
        operands.append(arr)
    out_specs, out_shape = [], []
    for n_rows, cw, dt in outs:
        out_specs.append(pl.BlockSpec((tm, cw), lambda j, i: (i, j)))
        out_shape.append(jax.ShapeDtypeStruct((n_rows, ncol * cw), dt))
    for shape, bs, im in accs:
        out_specs.append(pl.BlockSpec(bs, im))
        out_shape.append(jax.ShapeDtypeStruct(shape, F32))
    n_in, n_out = len(operands), len(outs)

    def body(*refs):
        vals = [r[...] for r in refs[:n_in]]
        o_vals, a_vals = fn(*vals)
        for r, v in zip(refs[n_in:n_in + n_out], o_vals):
            r[...] = v.astype(r.dtype)
        i = pl.program_id(1)
        for r, v in zip(refs[n_in + n_out:], a_vals):
            @pl.when(i == 0)
            def _(r=r, v=v):
                r[...] = v

            @pl.when(i > 0)
            def _(r=r, v=v):
                r[...] += v

    res = pl.pallas_call(body, grid=grid, in_specs=in_specs, out_specs=out_specs, out_shape=out_shape, name=name,
                         compiler_params=_cparams(("arbitrary", "arbitrary")))(*operands)
    return list(res)


def _with_vjp(fn, n_prim, want_out, want_acc):
    def f(*args):
        prim, g = args[:n_prim], args[n_prim:]
        outs, vjp = jax.vjp(lambda *a: fn(*a), *prim)
        d = vjp(tuple(gi.astype(o.dtype) for gi, o in zip(g, outs)))
        return tuple(d[k] for k in want_out), tuple(d[k] for k in want_acc)
    return f


def _p_row(arr, cw, off=0):
    return (arr, (1, cw), functools.partial(lambda j, i, off: (0, off + j), off=off))


def _a_row(n, cw):
    return ((1, n), (1, cw), lambda j, i: (0, j))


def _pick(n, cap, mult):
    best = None
    for t in range(mult, min(n, cap) + 1, mult):
        if n % t == 0:
            best = t
    return best if best is not None else n


def _matmul(name, a, b, mode, add=None, out_dtype=F32):
    if mode == "nn":
        (M, K), N = a.shape, b.shape[1]
    elif mode == "nt":
        (M, K), N = a.shape, b.shape[0]
    else:
        (K, M), N = a.shape, b.shape[1]
    tm = _pick(M, 512, 128 if mode == "tn" else 8)
    tn = _pick(N, 1536, 128)
    tk = K if K <= 1536 else _pick(K, 1152, 128)
    nk = K // tk
    if mode == "nn":
        a_spec = pl.BlockSpec((tm, tk), lambda i, j, k: (i, k))
        b_spec = pl.BlockSpec((tk, tn), lambda i, j, k: (k, j))
        ca, cb = 1, 0
    elif mode == "nt":
        a_spec = pl.BlockSpec((tm, tk), lambda i, j, k: (i, k))
        b_spec = pl.BlockSpec((tn, tk), lambda i, j, k: (j, k))
        ca, cb = 1, 1
    else:
        a_spec = pl.BlockSpec((tk, tm), lambda i, j, k: (k, i))
        b_spec = pl.BlockSpec((tk, tn), lambda i, j, k: (k, j))
        ca, cb = 0, 0
    in_specs, operands = [a_spec, b_spec], [a, b]
    if add is not None:
        in_specs.append(pl.BlockSpec((tm, tn), lambda i, j, k: (i, j)))
        operands.append(add)

    def body(*refs):
        a_ref, b_ref = refs[0], refs[1]
        o_ref, acc_ref = refs[-2], refs[-1]
        k = pl.program_id(2)

        @pl.when(k == 0)
        def _():
            acc_ref[...] = jnp.zeros_like(acc_ref)

        acc_ref[...] += _dot(a_ref[...], b_ref[...], ca, cb)

        @pl.when(k == nk - 1)
        def _():
            r = acc_ref[...]
            if add is not None:
                r = r + refs[2][...]
            o_ref[...] = r.astype(o_ref.dtype)

    return pl.pallas_call(
        body, grid=(M // tm, N // tn, nk), in_specs=in_specs,
        out_specs=pl.BlockSpec((tm, tn), lambda i, j, k: (i, j)),
        out_shape=jax.ShapeDtypeStruct((M, N), out_dtype),
        scratch_shapes=[pltpu.VMEM((tm, tn), F32)], name=name,
        compiler_params=_cparams(("parallel", "parallel", "arbitrary")))(*operands)


def _rmsnorm_fn(x, g):
    x = x.astype(F32)
    return (x * lax.rsqrt(jnp.mean(x * x, axis=-1, keepdims=True) + EPS) * g,)


def _gate_merge_fn(g0, g1, g2, ya, yb, yc, b0, b1, b2):
    return (jax.nn.sigmoid(g0 + b0) * ya + jax.nn.sigmoid(g1 + b1) * yb + jax.nn.sigmoid(g2 + b2) * yc,)


def _gated_norm_fn(y, z, w):
    t = y * _silu(z)
    return (t * lax.rsqrt(jnp.mean(t * t, axis=-1, keepdims=True) + EPS) * w,)


def _att_merge_fn(o0, o1, o2, l0, l1, l2):
    m = lax.stop_gradient(jnp.maximum(jnp.maximum(l0, l1), l2))
    e0, e1, e2 = jnp.exp(l0 - m), jnp.exp(l1 - m), jnp.exp(l2 - m)
    return ((e0 * o0 + e1 * o1 + e2 * o2) / (e0 + e1 + e2),)


def _loss_fn(x, tgt, g):
    (y,) = _rmsnorm_fn(x, g)
    err = y - tgt
    return 0.5 * jnp.sum(jnp.mean(err * err, axis=-1, keepdims=True), axis=0, keepdims=True)


def _pool_fn(x, wg, scale):
    g = pl.program_id(0)
    s2 = x + _shift(x, 1)
    s4 = s2 + _shift(s2, 2)
    s8 = s4 + _shift(s4, 4)
    s16 = s8 + _shift(s8, 8)
    win = ((g == 0).astype(F32) * s2 + (g == 1).astype(F32) * s4 + (g == 2).astype(F32) * s8
           + (g == 3).astype(F32) * s16)
    t = lax.broadcasted_iota(jnp.int32, (x.shape[0], 1), 0) + 1
    cnt = jnp.minimum(t, jnp.left_shift(2, g)).astype(F32)
    d = win / cnt - x
    return (_mm(d, wg) * scale,)


def _dwconv(x, taps, b):
    k = len(taps)
    y = taps[k - 1] * x + b
    for i in range(k - 1):
        y = y + taps[i] * _shift(x, k - 1 - i)
    return y


def _ssd_conv_fn(x, w0, w1, w2, w3, b):
    return (_silu(_dwconv(x, (w0, w1, w2, w3), b)),)


def _ffn_act_fn(xa, xv, a0, a1, a2, ab, v0, v1, v2, vb):
    return (_silu(_dwconv(xa, (a0, a1, a2), ab)) * _dwconv(xv, (v0, v1, v2), vb),)


def _att_block(q, kp, kc, vp, vc, bpa, bpb, bca, bcb, prev_ok):
    qi = lax.broadcasted_iota(jnp.int32, (ATT_BLOCK, ATT_BLOCK), 0)
    kj = lax.broadcasted_iota(jnp.int32, (ATT_BLOCK, ATT_BLOCK), 1)
    lane = lax.broadcasted_iota(jnp.int32, (1, 2 * HEAD_DIM), 1)
    mask_c = kj <= qi
    mask_p = jnp.logical_and(kj >= qi, prev_ok)
    q = q.astype(F32)
    o = None
    lse = None
    for hh, (bp, bc) in enumerate(((bpa, bca), (bpb, bcb))):
        hm = (lane // HEAD_DIM == hh).astype(F32)
        qh = q * (hm * (1.0 / math.sqrt(HEAD_DIM)))
        sp = jnp.where(mask_p, _mm_nt(qh, kp) + bp, NEG)
        sc = jnp.where(mask_c, _mm_nt(qh, kc) + bc, NEG)
        m = lax.stop_gradient(jnp.maximum(jnp.max(sp, axis=1, keepdims=True), jnp.max(sc, axis=1, keepdims=True)))
        pp = jnp.exp(sp - m)
        pc = jnp.exp(sc - m)
        l = jnp.sum(pp, axis=1, keepdims=True) + jnp.sum(pc, axis=1, keepdims=True)
        oh = (_mm(pp, vp) + _mm(pc, vc)) / l * hm
        lh = (m + jnp.log(l)) * hm
        o = oh if o is None else o + oh
        lse = lh if lse is None else lse + lh
    return o, lse


def _att_specs(nb, ncolblk, clamp):
    def cur(p, r, j):
        return (jnp.minimum(j, nb - 1) if clamp else j, r * 3 + p)

    def prev(p, r, j):
        jj = jnp.minimum(j, nb - 1) if clamp else j
        return (jnp.maximum(jj - 1, 0), r * 3 + p)

    def done(p, r, j):
        return (jnp.maximum(j - 1, 0), r * 3 + p)

    blk = (ATT_BLOCK, 2 * HEAD_DIM)
    return pl.BlockSpec(blk, cur), pl.BlockSpec(blk, prev), pl.BlockSpec(blk, done)


def _bias_specs():
    return [pl.BlockSpec((None, ATT_BLOCK, ATT_BLOCK), functools.partial(lambda p, r, j, hh: (2 * p + hh, 0, 0), hh=hh))
            for hh in (0, 1)]


def _bias_grad_spec():
    return pl.BlockSpec((None, ATT_BLOCK, ATT_BLOCK), lambda p, r, j: (p, 0, 0))


def _att_fwd(name, q, k, v, bias_p, bias_c, dil):
    L = S // dil
    nb = L // ATT_BLOCK
    cur, prev, _ = _att_specs(nb, dil * 3, False)
    bsp = _bias_specs()

    def body(q_ref, kp_ref, kc_ref, vp_ref, vc_ref, bpa, bpb, bca, bcb, o_ref, l_ref):
        prev_ok = pl.program_id(2) > 0
        o, lse = _att_block(q_ref[...], kp_ref[...], kc_ref[...], vp_ref[...], vc_ref[...],
                            bpa[...], bpb[...], bca[...], bcb[...], prev_ok)
        o_ref[...] = o
        l_ref[...] = lse

    shp = jax.ShapeDtypeStruct((L, dil * ATT_GW), F32)
    return pl.pallas_call(
        body, grid=(3, dil, nb), in_specs=[cur, prev, cur, prev, cur, bsp[0], bsp[1], bsp[0], bsp[1]],
        out_specs=[cur, cur], out_shape=[shp, shp], name=name,
        compiler_params=_cparams(("arbitrary",) * 3))(q, k, k, v, v, bias_p, bias_p, bias_c, bias_c)


def _att_bwd(name, q, k, v, bias_p, bias_c, do, dl, dil):
    L = S // dil
    nb = L // ATT_BLOCK
    cur, prev, done = _att_specs(nb, dil * 3, True)
    bsp = _bias_specs()
    gsp = _bias_grad_spec()

    def body(q_ref, kp_ref, kc_ref, vp_ref, vc_ref, bpa, bpb, bca, bcb, do_ref, dl_ref,
             dq_ref, dk_ref, dv_ref, gpa, gpb, gca, gcb, ck, cv):
        r, j = pl.program_id(1), pl.program_id(2)

        @pl.when(jnp.logical_and(r == 0, j == 0))
        def _():
            for g in (gpa, gpb, gca, gcb):
                g[...] = jnp.zeros_like(g)

        @pl.when(j == 0)
        def _():
            ck[...] = jnp.zeros_like(ck)
            cv[...] = jnp.zeros_like(cv)

        @pl.when(j < nb)
        def _():
            prev_ok = j > 0
            prim = (q_ref[...], kp_ref[...], kc_ref[...], vp_ref[...], vc_ref[...],
                    bpa[...], bpb[...], bca[...], bcb[...])
            _, vjp = jax.vjp(lambda *a: _att_block(*a, prev_ok), *prim)
            dq, dkp, dkc, dvp, dvc, dpa, dpb, dca, dcb = vjp((do_ref[...], dl_ref[...]))
            dq_ref[...] = dq.astype(F32)
            gpa[...] += dpa
            gpb[...] += dpb
            gca[...] += dca
            gcb[...] += dcb
            dk_ref[...] = ck[...] + dkp.astype(F32)
            dv_ref[...] = cv[...] + dvp.astype(F32)
            ck[...] = dkc.astype(F32)
            cv[...] = dvc.astype(F32)

        @pl.when(j == nb)
        def _():
            dk_ref[...] = ck[...]
            dv_ref[...] = cv[...]

    shp = jax.ShapeDtypeStruct((L, dil * ATT_GW), F32)
    gshp = jax.ShapeDtypeStruct((3, ATT_BLOCK, ATT_BLOCK), F32)
    res = pl.pallas_call(
        body, grid=(3, dil, nb + 1),
        in_specs=[cur, prev, cur, prev, cur, bsp[0], bsp[1], bsp[0], bsp[1], cur, cur],
        out_specs=[cur, done, done, gsp, gsp, gsp, gsp],
        out_shape=[shp, shp, shp, gshp, gshp, gshp, gshp],
        scratch_shapes=[pltpu.VMEM((ATT_BLOCK, 2 * HEAD_DIM), F32)] * 2, name=name,
        compiler_params=_cparams(("arbitrary",) * 3))(q, k, k, v, v, bias_p, bias_p, bias_c, bias_c, do, dl)
    dq, dk, dv, gpa, gpb, gca, gcb = res
    heads = lambda a, b: jnp.stack([a, b], axis=1).reshape(6, ATT_BLOCK, ATT_BLOCK)
    return dq, dk, dv, heads(gpa, gpb), heads(gca, gcb)


N_PAIR = SSD_HEADS // 2


def _ssd_chunk(xs, bs, cs_in, dt_raw, hs, a_row, dtb_row, ds):
    lane = lax.broadcasted_iota(jnp.int32, (1, 128), 1)
    row = lax.broadcasted_iota(jnp.int32, (128, 1), 0)
    tril = _tri(True)
    dt = _softplus(dt_raw + dtb_row)
    acs = _cumsum_rows(dt * a_row)
    acs_t = acs.T
    gmat = [_mm_nt(cs_in[g], bs[g]) for g in range(2)]
    lo = lane < HEAD_DIM
    lo_r = row < HEAD_DIM
    last = (row == SSD_CHUNK - 1).astype(F32)
    ys, hn = [], []
    for p in range(N_PAIR):
        g = p // (N_PAIR // 2)
        col, dtc, mm, clast = [], [], [], []
        for hh in range(2):
            h = 2 * p + hh
            oh = (lane == h).astype(F32)
            c_col = jnp.sum(acs * oh, axis=1, keepdims=True)
            c_row = jnp.sum(acs_t * (row == h).astype(F32), axis=0, keepdims=True)
            col.append(c_col)
            dtc.append(jnp.sum(dt * oh, axis=1, keepdims=True))
            clast.append(jnp.sum(c_col * last, axis=0, keepdims=True))
            mm.append(gmat[g] * jnp.exp(jnp.where(tril, c_col - c_row, NEG)))
        x = xs[p]
        xd = x * jnp.where(lo, dtc[0], dtc[1])
        y = jnp.where(lo, _mm(mm[0], xd), _mm(mm[1], xd))
        y = y + jnp.where(lo, jnp.exp(col[0]), jnp.exp(col[1])) * _mm_nt(cs_in[g], hs[p])
        ys.append(y + ds[p] * x)
        dec = jnp.where(lo, jnp.exp(clast[0] - col[0]), jnp.exp(clast[1] - col[1]))
        hn.append(hs[p] * jnp.where(lo_r, jnp.exp(clast[0]), jnp.exp(clast[1])) + _mm_tn(xd * dec, bs[g]))
    return tuple(ys), tuple(hn)


def _ssd_load(xbc_ref, dt_ref, a_ref, dtb_ref, d_ref):
    xs = tuple(xbc_ref[:, 128 * p:128 * (p + 1)] for p in range(N_PAIR))
    bs = tuple(xbc_ref[:, D + 128 * g:D + 128 * (g + 1)] for g in range(2))
    cs = tuple(xbc_ref[:, D + 256 + 128 * g:D + 256 + 128 * (g + 1)] for g in range(2))
    ds = tuple(d_ref[:, 128 * p:128 * (p + 1)] for p in range(N_PAIR))
    return xs, bs, cs, dt_ref[...], a_ref[...], dtb_ref[...], ds


def _ssd_fwd(name, xbc_c, proj, a_row, dtb_row, d_exp):
    nc = S // SSD_CHUNK
    prow = lambda n: pl.BlockSpec((1, n), lambda c: (0, 0))

    def body(xbc_ref, dt_ref, a_ref, dtb_ref, d_ref, y_ref, st_ref, h_ref):
        @pl.when(pl.program_id(0) == 0)
        def _():
            h_ref[...] = jnp.zeros_like(h_ref)

        xs, bs, cs, dt_raw, a, dtb, ds = _ssd_load(xbc_ref, dt_ref, a_ref, dtb_ref, d_ref)
        hs = tuple(h_ref[p] for p in range(N_PAIR))
        ys, hn = _ssd_chunk(xs, bs, cs, dt_raw, hs, a, dtb, ds)
        for p in range(N_PAIR):
            y_ref[:, 128 * p:128 * (p + 1)] = ys[p]
            st_ref[p] = hs[p]
            h_ref[p] = hn[p]

    return pl.pallas_call(
        body, grid=(nc,),
        in_specs=[pl.BlockSpec((SSD_CHUNK, SSD_XBC), lambda c: (c, 0)),
                  pl.BlockSpec((SSD_CHUNK, DT_PAD), lambda c: (c, OFF_DT // DT_PAD)),
                  prow(128), prow(128), prow(D)],
        out_specs=[pl.BlockSpec((SSD_CHUNK, D), lambda c: (c, 0)),
                   pl.BlockSpec((None, N_PAIR, 128, 128), lambda c: (c, 0, 0, 0))],
        out_shape=[jax.ShapeDtypeStruct((S, D), F32), jax.ShapeDtypeStruct((nc, N_PAIR, 128, 128), F32)],
        scratch_shapes=[pltpu.VMEM((N_PAIR, 128, 128), F32)], name=name,
        compiler_params=_cparams(("arbitrary",)))(xbc_c, proj, a_row, dtb_row, d_exp)


def _ssd_bwd(name, xbc_c, proj, states, dy, a_row, dtb_row, d_exp):
    nc = S // SSD_CHUNK
    prow = lambda n: pl.BlockSpec((1, n), lambda i: (0, 0))
    rc = lambda i: nc - 1 - i

    def body(xbc_ref, dt_ref, st_ref, dy_ref, a_ref, dtb_ref, d_ref, dxbc_ref, ddt_ref, da_ref, ddtb_ref, dd_ref, e_ref):
        i = pl.program_id(0)

        @pl.when(i == 0)
        def _():
            e_ref[...] = jnp.zeros_like(e_ref)
            da_ref[...] = jnp.zeros_like(da_ref)
            ddtb_ref[...] = jnp.zeros_like(ddtb_ref)
            dd_ref[...] = jnp.zeros_like(dd_ref)

        xs, bs, cs, dt_raw, a, dtb, ds = _ssd_load(xbc_ref, dt_ref, a_ref, dtb_ref, d_ref)
        hs = tuple(st_ref[p] for p in range(N_PAIR))
        _, vjp = jax.vjp(_ssd_chunk, xs, bs, cs, dt_raw, hs, a, dtb, ds)
        dys = tuple(dy_ref[:, 128 * p:128 * (p + 1)] for p in range(N_PAIR))
        es = tuple(e_ref[p] for p in range(N_PAIR))
        dxs, dbs, dcs, ddt, dhs, da, ddtb, dds = vjp((dys, es))
        for p in range(N_PAIR):
            dxbc_ref[:, 128 * p:128 * (p + 1)] = dxs[p]
            e_ref[p] = dhs[p]
            dd_ref[:, 128 * p:128 * (p + 1)] += dds[p]
        for g in range(2):
            dxbc_ref[:, D + 128 * g:D + 128 * (g + 1)] = dbs[g]
            dxbc_ref[:, D + 256 + 128 * g:D + 256 + 128 * (g + 1)] = dcs[g]
        ddt_ref[...] = ddt
        da_ref[...] += da
        ddtb_ref[...] += ddtb

    return pl.pallas_call(
        body, grid=(nc,),
        in_specs=[pl.BlockSpec((SSD_CHUNK, SSD_XBC), lambda i: (rc(i), 0)),
                  pl.BlockSpec((SSD_CHUNK, DT_PAD), lambda i: (rc(i), OFF_DT // DT_PAD)),
                  pl.BlockSpec((None, N_PAIR, 128, 128), lambda i: (rc(i), 0, 0, 0)),
                  pl.BlockSpec((SSD_CHUNK, D), lambda i: (rc(i), 0)),
                  prow(128), prow(128), prow(D)],
        out_specs=[pl.BlockSpec((SSD_CHUNK, SSD_XBC), lambda i: (rc(i), 0)),
                   pl.BlockSpec((SSD_CHUNK, DT_PAD), lambda i: (rc(i), 0)),
                   prow(128), prow(128), prow(D)],
        out_shape=[jax.ShapeDtypeStruct((S, SSD_XBC), F32), jax.ShapeDtypeStruct((S, DT_PAD), F32),
                   jax.ShapeDtypeStruct((1, 128), F32), jax.ShapeDtypeStruct((1, 128), F32),
                   jax.ShapeDtypeStruct((1, D), F32)],
        scratch_shapes=[pltpu.VMEM((N_PAIR, 128, 128), F32)], name=name,
        compiler_params=_cparams(("arbitrary",)))(xbc_c, proj, states, dy, a_row, dtb_row, d_exp)


def _exchange(name, arrays, scatter):
    n = len(arrays)
    flips = [(dx, dy, dc) for dx in (0, 1) for dy in (0, 1) for dc in (0, 1) if dx or dy or dc]

    def body(*refs):
        ins, outs = refs[:n], refs[n:2 * n]
        send_sems, recv_sems, loc_sems = refs[2 * n:]
        x, y, c = lax.axis_index("x"), lax.axis_index("y"), lax.axis_index("c")
        me = 4 * x + 2 * y + c
        peers = []
        for dx, dy, dc in flips:
            px, py, pc = (1 - x if dx else x), (1 - y if dy else y), (1 - c if dc else c)
            peers.append(((px, py, pc), 4 * px + 2 * py + pc))

        def remote(k, j, landed_from):
            dev, pid = peers[j]
            src = ins[k].at[pid] if scatter else ins[k]
            return pltpu.make_async_remote_copy(
                src_ref=src, dst_ref=outs[k].at[landed_from], send_sem=send_sems.at[k, j], recv_sem=recv_sems.at[k, j],
                device_id=dev, device_id_type=pl.DeviceIdType.MESH)

        local = [pltpu.make_async_copy(ins[k].at[me] if scatter else ins[k], outs[k].at[me], loc_sems.at[k])
                 for k in range(n)]
        for cp in local:
            cp.start()
        for k in range(n):
            for j in range(len(flips)):
                remote(k, j, me).start()
        for cp in local:
            cp.wait()
        for k in range(n):
            for j in range(len(flips)):
                remote(k, j, me).wait_send()
                remote(k, j, peers[j][1]).wait_recv()

    hbm = pl.BlockSpec(memory_space=pltpu.HBM)
    out_shape = [jax.ShapeDtypeStruct(a.shape if scatter else (N_DEV,) + a.shape, a.dtype) for a in arrays]
    res = pl.pallas_call(
        body, in_specs=[hbm] * n, out_specs=[hbm] * n, out_shape=out_shape, name=name,
        scratch_shapes=[pltpu.SemaphoreType.DMA((n, len(flips))), pltpu.SemaphoreType.DMA((n, len(flips))),
                        pltpu.SemaphoreType.DMA((n,))])(*arrays)
    return list(res)


def _adamw_fn(*vals):
    slots, (w, m, v) = vals[:N_DEV], vals[N_DEV:]
    g = slots[0].astype(F32)
    for s in slots[1:]:
        g = g + s.astype(F32)
    m2 = ADAM_B1 * m + (1.0 - ADAM_B1) * g
    v2 = ADAM_B2 * v + (1.0 - ADAM_B2) * (g * g)
    m_hat = m2 / (1.0 - ADAM_B1 ** ADAM_STEP)
    v_hat = v2 / (1.0 - ADAM_B2 ** ADAM_STEP)
    delta = -ADAM_LR * (m_hat / (jnp.sqrt(v_hat) + ADAM_EPS) + ADAM_WD * w)
    return (g, delta, m2, v2), ()


def _adamw(name, slots, w, m, v):
    R, C = w.shape
    tm = R if R <= 128 else _pick(R, 128 if C > D else 256, 8)
    rows = [_rows(slots, C, lead=s) for s in range(N_DEV)] + [_rows(a, C) for a in (w, m, v)]
    return _tiled(name, _adamw_fn, (1, R // tm), tm, rows, [], [(R, C, F32)] * 4)


def _bucket_onehots():
    out = []
    qi = jnp.arange(ATT_BLOCK)[:, None]
    kj = jnp.arange(ATT_BLOCK)[None, :]
    max_exact = REL_BUCKETS // 2
    for _, dil in ATT_GROUPS:
        parts = []
        for rel in (qi + ATT_BLOCK - kj, qi - kj):
            dist = jnp.clip(rel, 0, None) * dil
            nf = jnp.maximum(dist, 1).astype(F32)
            large = max_exact + (jnp.log(nf / max_exact) / math.log(REL_MAX_DISTANCE / max_exact)
                                 * (REL_BUCKETS - max_exact)).astype(jnp.int32)
            large = jnp.minimum(large, REL_BUCKETS - 1)
            bucket = jnp.where(dist < max_exact, dist, large)
            parts.append((bucket[:, :, None] == jnp.arange(REL_BUCKETS)[None, None, :]).astype(F32))
        out.append(jnp.stack(parts))
    return out


SHARDED = {"w_in": 2, "w_a": 2, "pool_w": 2, "w_b": 1, "ssd_conv_w": 2, "w_c": 1, "w_o": 1,
           "ffn_w_up": 2, "ffn_conv_w": 2, "ffn_w_down": 1}
MATMUL_WEIGHTS = ("w_in", "w_a", "pool_w", "w_b", "w_c", "w_o", "ffn_w_up", "ffn_w_down")
REPLICATED = ("rel_bias", "ln1_g", "b_gate", "pool_scale", "ssd_conv_b", "ssd_dt_bias", "ssd_a_log", "ssd_d",
              "ssd_norm_w", "ln2_g", "ffn_conv_b", "final_g")
WEIGHTS = ("rel_bias", "ln1_g", "w_in", "b_gate", "w_a", "pool_w", "pool_scale", "w_b", "ssd_conv_w", "ssd_conv_b",
           "ssd_dt_bias", "ssd_a_log", "ssd_d", "ssd_norm_w", "w_c", "w_o", "ln2_g", "ffn_w_up", "ffn_conv_w",
           "ffn_conv_b", "ffn_w_down", "final_g")


def _assemble(gathered, axis):
    t = jnp.moveaxis(gathered, 0, axis)
    shp = list(t.shape)
    shp[axis:axis + 2] = [shp[axis] * shp[axis + 1]]
    return t.reshape(shp)


def _split(full, axis):
    shp = list(full.shape)
    shp[axis:axis + 1] = [N_DEV, shp[axis] // N_DEV]
    return jnp.moveaxis(full.reshape(shp), axis, 0)


def _pad_w_in(w):
    z = jnp.zeros((D, OFF_Q - OFF_DT - SSD_HEADS), w.dtype)
    return jnp.concatenate([w[:, 7056:10128], w[:, 3456:4480], w[:, 4480:5504], w[:, 5504:7040], w[:, 7040:7056], z,
                            w[:, 0:3456]], axis=1)


def _unpad_w_in(g):
    return jnp.concatenate([g[:, OFF_Q:NP], g[:, OFF_POOL:OFF_Z], g[:, OFF_Z:OFF_XBC], g[:, OFF_XBC:OFF_DT],
                            g[:, OFF_DT:OFF_DT + SSD_HEADS], g[:, 0:OFF_POOL]], axis=1)


def _to_residue(t, dil):
    return t.reshape(S // dil, dil * ATT_GW)


def _row(v, n=None):
    v = v.reshape(1, -1)
    if n is not None and v.shape[1] < n:
        v = jnp.pad(v, ((0, 0), (0, n - v.shape[1])))
    return v


RT = 256


def _row_call(name, fn, cw, ncol, rows, params, outs, accs=()):
    return _tiled(name, fn, (ncol, S // RT), RT, rows, params, [(S, cw, dt) for dt in outs], accs)


def _col_call(name, fn, tc, ncol, rows, params, outs, accs=()):
    return _tiled(name, fn, (ncol, 1), S, rows, params, [(S, tc, dt) for dt in outs], accs)


def _fwd_only(fn):
    return lambda *a: (fn(*a), ())


def _layer_fwd(i, x, W, P, bias_tabs):
    sv = {"x": x}
    (u,) = _row_call(f"ln1_f{i}", _fwd_only(_rmsnorm_fn), D, 1, [_rows(x, D)], [_p_row(P["ln1_g"], D)], [BF16])
    proj = _matmul(f"inproj_f{i}", u, W["w_in"], "nn")
    sv["u"], sv["proj"] = u, proj

    os_, ls_, res = [], [], []
    for gi, (_, dil) in enumerate(ATT_GROUPS):
        q, k, v = (_to_residue(proj[:, off + gi * ATT_GW: off + (gi + 1) * ATT_GW].astype(BF16), dil)
                   for off in (OFF_Q, OFF_K, OFF_V))
        o, lse = _att_fwd(f"att_f{i}_{gi}", q, k, v, bias_tabs[gi][0], bias_tabs[gi][1], dil)
        res.append((q, k, v))
        os_.append(o.reshape(S, ATT_GW))
        ls_.append(lse.reshape(S, ATT_GW))
    sv["att_res"], sv["att_o"], sv["att_l"] = res, os_, ls_
    (att,) = _row_call(f"attmerge_f{i}", _fwd_only(_att_merge_fn), 128, 3,
                       [_rows(t, 128) for t in os_ + ls_], [], [BF16])
    y_a = _matmul(f"wa_f{i}", att, W["w_a"], "nn")
    sv["att"], sv["y_a"] = att, y_a

    pool_params = [(W["pool_w"], (None, 256, 256), lambda j, i_: (j, 0, 0)), _p_row(P["pool_scale"], 256)]
    (yb_pre,) = _col_call(f"pool_f{i}", _fwd_only(_pool_fn), 256, 4, [_rows(proj, 256, OFF_POOL // 256)],
                          pool_params, [BF16])
    y_b = _matmul(f"wb_f{i}", yb_pre, W["w_b"], "nn")
    sv["yb_pre"], sv["y_b"] = yb_pre, y_b

    conv_params = [_p_row(P["ssd_conv_w"][k], 128) for k in range(4)] + [_p_row(P["ssd_conv_b"], 128)]
    (xbc_c,) = _col_call(f"ssdconv_f{i}", _fwd_only(_ssd_conv_fn), 128, SSD_XBC // 128,
                         [_rows(proj, 128, OFF_XBC // 128)], conv_params, [F32])
    y_ssd, states = _ssd_fwd(f"ssd_f{i}", xbc_c, proj, P["a_row"], P["dtb_row"], P["d_exp"])
    (yc_pre,) = _row_call(f"ssdnorm_f{i}", _fwd_only(_gated_norm_fn), 512, 2,
                          [_rows(y_ssd, 512), _rows(proj, 512, OFF_Z // 512)], [_p_row(P["ssd_norm_w"], 512)], [BF16])
    y_c = _matmul(f"wc_f{i}", yc_pre, W["w_c"], "nn")
    sv["xbc_c"], sv["states"], sv["y_ssd"], sv["yc_pre"], sv["y_c"] = xbc_c, states, y_ssd, yc_pre, y_c

    gate_rows = [_rows(proj, D, k) for k in range(3)] + [_rows(t, D) for t in (y_a, y_b, y_c)]
    gate_params = [_p_row(P["b_gate"], D, k) for k in range(3)]
    (merged,) = _row_call(f"gate_f{i}", _fwd_only(_gate_merge_fn), D, 1, gate_rows, gate_params, [BF16])
    x1 = _matmul(f"wo_f{i}", merged, W["w_o"], "nn", add=x)
    sv["merged"], sv["x1"] = merged, x1

    (u2,) = _row_call(f"ln2_f{i}", _fwd_only(_rmsnorm_fn), D, 1, [_rows(x1, D)], [_p_row(P["ln2_g"], D)], [BF16])
    up = _matmul(f"up_f{i}", u2, W["ffn_w_up"], "nn")
    (act,) = _col_call(f"ffnact_f{i}", _fwd_only(_ffn_act_fn), 128, D_FF // 128,
                       [_rows(up, 128), _rows(up, 128, D_FF // 128)], _ffn_params(P), [BF16])
    x2 = _matmul(f"down_f{i}", act, W["ffn_w_down"], "nn", add=x1)
    sv["u2"], sv["up"], sv["act"] = u2, up, act
    return x2, sv


def _ffn_params(P):
    nb = D_FF // 128
    return ([_p_row(P["ffn_conv_w"][k], 128) for k in range(3)] + [_p_row(P["ffn_conv_b"], 128)]
            + [_p_row(P["ffn_conv_w"][k], 128, nb) for k in range(3)] + [_p_row(P["ffn_conv_b"], 128, nb)])


def _layer_bwd(i, dx2, sv, W, P, bias_tabs, onehots):
    G = {}
    x, proj, x1 = sv["x"], sv["proj"], sv["x1"]

    dact = _matmul(f"down_bx{i}", dx2, W["ffn_w_down"], "nt", out_dtype=BF16)
    G["ffn_w_down"] = _matmul(f"down_bw{i}", sv["act"], dx2, "tn")
    nb = D_FF // 128
    up = sv["up"]
    f = _with_vjp(_ffn_act_fn, 10, (0, 1), tuple(range(2, 10)))
    accs = [_a_row(D_FF, 128)] * 8
    dua, duv, a0, a1, a2, ab, v0, v1, v2, vb = _col_call(
        f"ffnact_b{i}", f, 128, nb, [_rows(up, 128), _rows(up, 128, nb)], _ffn_params(P) + [_rows_as_param(dact, 128)],
        [BF16, BF16], accs)
    G["ffn_conv_w"] = jnp.concatenate([jnp.concatenate([a0, a1, a2], 0), jnp.concatenate([v0, v1, v2], 0)], axis=1)
    G["ffn_conv_b"] = jnp.concatenate([ab, vb], axis=1)[0]
    dup = jnp.concatenate([dua, duv], axis=1)
    du2 = _matmul(f"up_bx{i}", dup, W["ffn_w_up"], "nt")
    G["ffn_w_up"] = _matmul(f"up_bw{i}", sv["u2"], dup, "tn")

    def norm_bwd(x_, g_, du_, dres):
        (dxn,), (dg,) = _with_vjp(_rmsnorm_fn, 2, (0,), (1,))(x_, g_, du_)
        return (dxn + dres,), (dg,)

    (dx1,), (G["ln2_g"],) = _split_res(_row_call(
        f"ln2_b{i}", lambda x_, du_, dres, g_: norm_bwd(x_, g_, du_, dres), D, 1,
        [_rows(x1, D), _rows(du2, D), _rows(dx2, D)], [_p_row(P["ln2_g"], D)], [F32], [_a_row(D, D)]), 1)

    dmerged = _matmul(f"wo_bx{i}", dx1, W["w_o"], "nt")
    G["w_o"] = _matmul(f"wo_bw{i}", sv["merged"], dx1, "tn")
    gate_rows = [_rows(proj, D, k) for k in range(3)] + [_rows(sv[t], D) for t in ("y_a", "y_b", "y_c")]
    gate_params = [_p_row(P["b_gate"], D, k) for k in range(3)]

    def gate_bwd(g0, g1, g2, ya, yb, yc, dm, b0, b1, b2):
        return _with_vjp(_gate_merge_fn, 9, (0, 1, 2, 3, 4, 5), (6, 7, 8))(g0, g1, g2, ya, yb, yc, b0, b1, b2, dm)

    dg0, dg1, dg2, dya, dyb, dyc, db0, db1, db2 = _row_call(
        f"gate_b{i}", gate_bwd, D, 1, gate_rows + [_rows(dmerged, D)], gate_params,
        [BF16, BF16, BF16, BF16, BF16, BF16], [_a_row(D, D)] * 3)
    G["b_gate"] = jnp.concatenate([db0, db1, db2], axis=1)[0]

    dyc_pre = _matmul(f"wc_bx{i}", dyc, W["w_c"], "nt")
    G["w_c"] = _matmul(f"wc_bw{i}", sv["yc_pre"], dyc, "tn")

    def gnorm_bwd(y_, z_, dy_, w_):
        return _with_vjp(_gated_norm_fn, 3, (0, 1), (2,))(y_, z_, w_, dy_)

    dy_ssd, dz, dnw = _row_call(
        f"ssdnorm_b{i}", gnorm_bwd, 512, 2,
        [_rows(sv["y_ssd"], 512), _rows(proj, 512, OFF_Z // 512), _rows(dyc_pre, 512)],
        [_p_row(P["ssd_norm_w"], 512)], [F32, BF16], [_a_row(D, 512)])
    G["ssd_norm_w"] = dnw[0]
    dxbc_c, ddt, da_row, ddtb_row, dd_exp = _ssd_bwd(f"ssd_b{i}", sv["xbc_c"], proj, sv["states"], dy_ssd,
                                                     P["a_row"], P["dtb_row"], P["d_exp"])
    a_vec = P["a_row"][0, :SSD_HEADS]
    G["ssd_a_log"] = da_row[0, :SSD_HEADS] * a_vec
    G["ssd_dt_bias"] = ddtb_row[0, :SSD_HEADS]
    G["ssd_d"] = dd_exp.reshape(SSD_HEADS, HEAD_DIM).sum(axis=1)
    conv_params = [_p_row(P["ssd_conv_w"][k], 128) for k in range(4)] + [_p_row(P["ssd_conv_b"], 128)]

    def conv_bwd(x_, dy_, w0, w1, w2, w3, b_):
        return _with_vjp(_ssd_conv_fn, 6, (0,), (1, 2, 3, 4, 5))(x_, w0, w1, w2, w3, b_, dy_)

    dxbc, c0, c1, c2, c3, cb = _col_call(
        f"ssdconv_b{i}", conv_bwd, 128, SSD_XBC // 128, [_rows(proj, 128, OFF_XBC // 128), _rows(dxbc_c, 128)],
        conv_params, [BF16], [_a_row(SSD_XBC, 128)] * 5)
    G["ssd_conv_w"] = jnp.concatenate([c0, c1, c2, c3], axis=0)
    G["ssd_conv_b"] = cb[0]

    dyb_pre = _matmul(f"wb_bx{i}", dyb, W["w_b"], "nt")
    G["w_b"] = _matmul(f"wb_bw{i}", sv["yb_pre"], dyb, "tn")
    pool_params = [(W["pool_w"], (None, 256, 256), lambda j, i_: (j, 0, 0)), _p_row(P["pool_scale"], 256)]

    def pool_bwd(x_, dy_, wg, sc):
        return _with_vjp(_pool_fn, 3, (0,), (1, 2))(x_, wg.astype(F32), sc, dy_)

    dpool, dwg, dsc = _col_call(
        f"pool_b{i}", pool_bwd, 256, 4, [_rows(proj, 256, OFF_POOL // 256), _rows(dyb_pre, 256)], pool_params, [BF16],
        [((4, 256, 256), (None, 256, 256), lambda j, i_: (j, 0, 0)), _a_row(D, 256)])
    G["pool_w"] = dwg
    G["pool_scale"] = dsc[0]

    datt = _matmul(f"wa_bx{i}", dya, W["w_a"], "nt")
    G["w_a"] = _matmul(f"wa_bw{i}", sv["att"], dya, "tn")

    def merge_bwd(o0, o1, o2, l0, l1, l2, da_):
        return _with_vjp(_att_merge_fn, 6, (0, 1, 2, 3, 4, 5), ())(o0, o1, o2, l0, l1, l2, da_)

    dol = _row_call(f"attmerge_b{i}", merge_bwd, 128, 3,
                    [_rows(t, 128) for t in sv["att_o"] + sv["att_l"]] + [_rows(datt, 128)], [], [F32] * 6)
    dqs, dks, dvs = [], [], []
    g_rel = jnp.zeros((REL_BUCKETS, 18), F32)
    for gi, (_, dil) in enumerate(ATT_GROUPS):
        q, k, v = sv["att_res"][gi]
        dq, dk, dv, gbp, gbc = _att_bwd(f"att_b{i}_{gi}", q, k, v, bias_tabs[gi][0], bias_tabs[gi][1],
                                        _to_residue(dol[gi], dil), _to_residue(dol[3 + gi], dil), dil)
        dqs.append(dq.reshape(S, ATT_GW))
        dks.append(dk.reshape(S, ATT_GW))
        dvs.append(dv.reshape(S, ATT_GW))
        oh = onehots[gi]
        gt = (jnp.einsum("hqk,qkb->bh", gbp, oh[0], precision=lax.Precision.HIGHEST)
              + jnp.einsum("hqk,qkb->bh", gbc, oh[1], precision=lax.Precision.HIGHEST))
        g_rel = g_rel.at[:, gi * 6:(gi + 1) * 6].add(gt)
    G["rel_bias"] = g_rel

    dproj = jnp.concatenate(
        [dg0, dg1, dg2, dpool, dz, dxbc, ddt.astype(BF16), jnp.zeros((S, OFF_Q - OFF_DT - DT_PAD), BF16)]
        + [t.astype(BF16) for t in dqs + dks + dvs], axis=1)
    du = _matmul(f"inproj_bx{i}", dproj, W["w_in"], "nt")
    G["w_in"] = _unpad_w_in(_matmul(f"inproj_bw{i}", sv["u"], dproj, "tn"))
    (dx,), (G["ln1_g"],) = _split_res(_row_call(
        f"ln1_b{i}", lambda x_, du_, dres, g_: norm_bwd(x_, g_, du_, dres), D, 1,
        [_rows(x, D), _rows(du, D), _rows(dx1, D)], [_p_row(P["ln1_g"], D)], [F32], [_a_row(D, D)]), 1)
    G["ln1_g"] = G["ln1_g"][0]
    G["ln2_g"] = G["ln2_g"][0]
    return dx, G


def _rows_as_param(arr, cw):
    return (arr, (arr.shape[0], cw), lambda j, i: (0, j))


def _split_res(res, n_out):
    return tuple(res[:n_out]), tuple(res[n_out:])


def kernel(x, rel_bias, ln1_g, w_in, b_gate, w_a, pool_w, pool_scale, w_b, ssd_conv_w, ssd_conv_b, ssd_dt_bias, ssd_a_log, ssd_d, ssd_norm_w, w_c, w_o, ln2_g, ffn_w_up, ffn_conv_w, ffn_conv_b, ffn_w_down, final_g, loss_target, m_rel_bias, m_ln1_g, m_w_in, m_b_gate, m_w_a, m_pool_w, m_pool_scale, m_w_b, m_ssd_conv_w, m_ssd_conv_b, m_ssd_dt_bias, m_ssd_a_log, m_ssd_d, m_ssd_norm_w, m_w_c, m_w_o, m_ln2_g, m_ffn_w_up, m_ffn_conv_w, m_ffn_conv_b, m_ffn_w_down, m_final_g, v_rel_bias, v_ln1_g, v_w_in, v_b_gate, v_w_a, v_pool_w, v_pool_scale, v_w_b, v_ssd_conv_w, v_ssd_conv_b, v_ssd_dt_bias, v_ssd_a_log, v_ssd_d, v_ssd_norm_w, v_w_c, v_w_o, v_ln2_g, v_ffn_w_up, v_ffn_conv_w, v_ffn_conv_b, v_ffn_w_down, v_final_g):
    args = locals()
    wts = {n: args[n] for n in WEIGHTS}
    mom = {n: args["m_" + n] for n in WEIGHTS}
    var = {n: args["v_" + n] for n in WEIGHTS}
    names = list(SHARDED)

    shards = [wts[n].astype(BF16) if n in MATMUL_WEIGHTS else wts[n] for n in names]
    gathered = dict(zip(names, _exchange("gather_weights", shards, scatter=False)))
    full = {n: _assemble(gathered[n], SHARDED[n]) for n in names}

    onehots = _bucket_onehots()
    bias_tabs = []
    for gi in range(3):
        tab = rel_bias[:, gi * 6:(gi + 1) * 6]
        b = jnp.einsum("pqkb,bh->phqk", onehots[gi], tab, precision=lax.Precision.HIGHEST)
        bias_tabs.append((b[0], b[1]))

    Ws, Ps = [], []
    for i in range(DEPTH):
        W = {n: full[n][i] for n in MATMUL_WEIGHTS}
        W["w_in"] = _pad_w_in(W["w_in"])
        a_vec = -jnp.exp(ssd_a_log[i])
        P = {"ln1_g": _row(ln1_g[i]), "ln2_g": _row(ln2_g[i]), "b_gate": _row(b_gate[i]),
             "pool_scale": _row(pool_scale[i]), "ssd_conv_b": _row(ssd_conv_b[i]),
             "ssd_conv_w": [_row(full["ssd_conv_w"][i, k]) for k in range(4)],
             "ssd_norm_w": _row(ssd_norm_w[i]), "ffn_conv_b": _row(ffn_conv_b[i]),
             "ffn_conv_w": [_row(full["ffn_conv_w"][i, k]) for k in range(3)],
             "a_row": _row(a_vec, 128), "dtb_row": _row(ssd_dt_bias[i], 128),
             "d_exp": _row(jnp.repeat(ssd_d[i], HEAD_DIM))}
        Ws.append(W)
        Ps.append(P)

    h = x.reshape(S, D)
    saved = []
    for i in range(DEPTH):
        h, sv = _layer_fwd(i, h, Ws[i], Ps[i], bias_tabs)
        saved.append(sv)

    def loss_bwd(x_, t_, g_):
        lval, vjp = jax.vjp(_loss_fn, x_, t_, g_)
        dx_, _, dg_ = vjp(jnp.ones_like(lval))
        return (dx_,), (dg_, jnp.broadcast_to(lval, (1, 128)))

    dh, g_final, loss_part = _row_call("loss", loss_bwd, D, 1, [_rows(h, D), _rows(loss_target.reshape(S, D), D)],
                                       [_p_row(_row(final_g), D)], [F32], [_a_row(D, D), _a_row(128, 128)])
    loss = lax.psum(loss_part[0, 0], MESH_AXES)

    grads = {n: [None] * DEPTH for n in WEIGHTS if n not in ("rel_bias", "final_g")}
    g_rel = jnp.zeros((REL_BUCKETS, 18), F32)
    for i in reversed(range(DEPTH)):
        dh, G = _layer_bwd(i, dh, saved[i], Ws[i], Ps[i], bias_tabs, onehots)
        g_rel = g_rel + G.pop("rel_bias")
        for n, g in G.items():
            grads[n][i] = g
    grad_x = dh.reshape(1, S, D)
    local = {n: jnp.stack(g) for n, g in grads.items()}
    local["rel_bias"] = g_rel
    local["final_g"] = g_final[0]

    parts = [_split(local[n], SHARDED[n]) for n in names]
    slots = dict(zip(names, _exchange("scatter_grads", parts, scatter=True)))
    out = {}
    for n in names:
        shp = wts[n].shape
        r = int(np.prod(shp[:-1]))
        res = _adamw("adamw_" + n, slots[n].reshape(N_DEV, r, shp[-1]), wts[n].reshape(r, shp[-1]),
                     mom[n].reshape(r, shp[-1]), var[n].reshape(r, shp[-1]))
        out[n] = [t.reshape(shp) for t in res]

    def pack(d):
        flat = jnp.concatenate([d[n].reshape(-1).astype(F32) for n in REPLICATED])
        rows = -(-flat.shape[0] // (8 * 128)) * 8
        return jnp.pad(flat, (0, rows * 128 - flat.shape[0])).reshape(rows, 128)

    (rep_slots,) = _exchange("gather_small_grads", [pack(local)], scatter=False)
    rep = _adamw("adamw_small", rep_slots, pack(wts), pack(mom), pack(var))
    off = 0
    for n in REPLICATED:
        sz = int(np.prod(wts[n].shape))
        out[n] = [t.reshape(-1)[off:off + sz].reshape(wts[n].shape) for t in rep]
        off += sz

    return (loss, grad_x, *[out[n][0] for n in WEIGHTS], *[out[n][1] for n in WEIGHTS],
            *[out[n][2] for n in WEIGHTS], *[out[n][3] for n in WEIGHTS])
```

```python
import functools
import math

import numpy as np
import jax
import jax.numpy as jnp
from jax import lax
from jax.experimental import pallas as pl
from jax.experimental.pallas import tpu as pltpu

F32 = jnp.float32
BF16 = jnp.bfloat16

N_DEV = 8
MESH_AXES = ("x", "y", "c")
S = 4096
D = 1024
DEPTH = 4
HEAD_DIM = 64
ATT_W = 1152
ATT_GW = 384
ATT_GROUPS = ((128, 1), (512, 4), (2048, 16))
ATT_BLOCK = 128
REL_BUCKETS = 32
REL_MAX_DISTANCE = 2048
POOL_WINDOWS = (2, 4, 8, 16)
SSD_HEADS = 16
SSD_CHUNK = 128
SSD_XBC = 1536
D_FF = 2816
IN_WIDTH = 10128
EPS = 1e-6
NEG = -1e30

OFF_GATE, OFF_POOL, OFF_Z, OFF_XBC, OFF_DT, OFF_Q, OFF_K, OFF_V = 0, 3072, 4096, 5120, 6656, 6912, 8064, 9216
NP = 10368
DT_PAD = 128

ADAM_LR, ADAM_B1, ADAM_B2, ADAM_EPS, ADAM_WD, ADAM_STEP = 0.001, 0.9, 0.999, 1e-08, 0.01, 10

VMEM_LIMIT = 52 * 1024 * 1024


def _cparams(sem=None):
    return pltpu.CompilerParams(dimension_semantics=sem, vmem_limit_bytes=VMEM_LIMIT)


def _dot(a, b, ca, cb):
    return lax.dot_general(a.astype(BF16), b.astype(BF16), (((ca,), (cb,)), ((), ())), preferred_element_type=F32)


@jax.custom_vjp
def _mm(a, b):
    return _dot(a, b, 1, 0)


def _mm_fwd(a, b):
    return _mm(a, b), (a, b)


def _mm_bwd(res, g):
    a, b = res
    return _dot(g, b, 1, 1).astype(a.dtype), _dot(a, g, 0, 0).astype(b.dtype)


_mm.defvjp(_mm_fwd, _mm_bwd)


@jax.custom_vjp
def _mm_nt(a, b):
    return _dot(a, b, 1, 1)


def _mm_nt_fwd(a, b):
    return _mm_nt(a, b), (a, b)


def _mm_nt_bwd(res, g):
    a, b = res
    return _dot(g, b, 1, 0).astype(a.dtype), _dot(g, a, 0, 0).astype(b.dtype)


_mm_nt.defvjp(_mm_nt_fwd, _mm_nt_bwd)


@jax.custom_vjp
def _mm_tn(a, b):
    return _dot(a, b, 0, 0)


def _mm_tn_fwd(a, b):
    return _mm_tn(a, b), (a, b)


def _mm_tn_bwd(res, g):
    a, b = res
    return _dot(b, g, 1, 1).astype(a.dtype), _dot(a, g, 1, 0).astype(b.dtype)


_mm_tn.defvjp(_mm_tn_fwd, _mm_tn_bwd)


def _shift_impl(x, j):
    n = x.shape[0]
    if j == 0:
        return x
    r = pltpu.roll(x, j % n, axis=0)
    t = lax.broadcasted_iota(jnp.int32, x.shape, 0)
    mask = (t >= j) if j > 0 else (t < n + j)
    return jnp.where(mask, r, 0.0)


@functools.partial(jax.custom_vjp, nondiff_argnums=(1,))
def _shift(x, j):
    return _shift_impl(x, j)


_shift.defvjp(lambda x, j: (_shift_impl(x, j), None), lambda j, _, g: (_shift_impl(g, -j),))


def _tri(lower):
    r = lax.broadcasted_iota(jnp.int32, (SSD_CHUNK, SSD_CHUNK), 0)
    c = lax.broadcasted_iota(jnp.int32, (SSD_CHUNK, SSD_CHUNK), 1)
    return (r >= c) if lower else (r <= c)


def _dot_hi(a, b):
    return lax.dot_general(a, b, (((1,), (0,)), ((), ())), precision=lax.Precision.HIGHEST,
                           preferred_element_type=F32)


@jax.custom_vjp
def _cumsum_rows(a):
    return _dot_hi(_tri(True).astype(F32), a)


_cumsum_rows.defvjp(lambda a: (_cumsum_rows(a), None), lambda _, g: (_dot_hi(_tri(False).astype(F32), g),))


@jax.custom_vjp
def _softplus(x):
    return jnp.maximum(x, 0.0) + jnp.log(1.0 + jnp.exp(-jnp.abs(x)))


_softplus.defvjp(lambda x: (_softplus(x), x), lambda x, g: (g * jax.nn.sigmoid(x),))


def _silu(x):
    return x * jax.nn.sigmoid(x)


def _rows(arr, cw, off=0, lead=None, roff=0):
    return (arr, cw, off, lead, roff)


def _tiled(name, fn, grid, tm, rows, params, outs, accs=(), out_roff=0, prev_outs=None):
    ncol, nrow = grid
    in_specs, operands = [], []
    for arr, cw, off, lead, roff in rows:
        if lead is None:
            in_specs.append(pl.BlockSpec((tm, cw), functools.partial(lambda j, i, off, roff: (roff + i, off + j),
                                                                     off=off, roff=roff)))
        else:
            in_specs.append(pl.BlockSpec((None, tm, cw), functools.partial(
                lambda j, i, off, lead, roff: (lead, roff + i, off + j), off=off, lead=lead, roff=roff)))
        operands.append(arr)
    for arr, bs, im in params:
        in_specs.append(pl.BlockSpec(bs, im))
        operands.append(arr)
    out_specs, out_shape = [], []
    for n_rows, cw, dt in outs:
        out_specs.append(pl.BlockSpec((tm, cw), functools.partial(lambda j, i, r: (r + i, j), r=out_roff)))
        out_shape.append(jax.ShapeDtypeStruct((n_rows, ncol * cw), dt))
    for shape, bs, im in accs:
        out_specs.append(pl.BlockSpec(bs, im))
        out_shape.append(jax.ShapeDtypeStruct(shape, F32))
    n_in, n_out = len(operands), len(outs)
    aliases = {}
    if prev_outs is not None:
        for k, p in enumerate(prev_outs):
            aliases[len(operands)] = k
            in_specs.append(pl.BlockSpec(memory_space=pl.ANY))
            operands.append(p)

    n_all = len(operands)

    def body(*refs):
        vals = [r[...] for r in refs[:n_in]]
        o_vals, a_vals = fn(*vals)
        for r, v in zip(refs[n_all:n_all + n_out], o_vals):
            r[...] = v.astype(r.dtype)
        i = pl.program_id(1)
        for r, v in zip(refs[n_all + n_out:], a_vals):
            @pl.when(i == 0)
            def _(r=r, v=v):
                r[...] = v.astype(r.dtype)

            @pl.when(i > 0)
            def _(r=r, v=v):
                r[...] += v.astype(r.dtype)

    res = pl.pallas_call(body, grid=grid, in_specs=in_specs, out_specs=out_specs, out_shape=out_shape, name=name,
                         input_output_aliases=aliases, compiler_params=_cparams(("arbitrary", "arbitrary")))(*operands)
    return list(res)


def _with_vjp(fn, n_prim, want_out, want_acc):
    def f(*args):
        prim, g = args[:n_prim], args[n_prim:]
        outs, vjp = jax.vjp(lambda *a: fn(*a), *prim)
        d = vjp(tuple(gi.astype(o.dtype) for gi, o in zip(g, outs)))
        return tuple(d[k] for k in want_out), tuple(d[k] for k in want_acc)
    return f


def _p_row(arr, cw, off=0):
    return (arr, (1, cw), functools.partial(lambda j, i, off: (0, off + j), off=off))


def _a_row(n, cw):
    return ((1, n), (1, cw), lambda j, i: (0, j))


def _pick(n, cap, mult):
    best = None
    for t in range(mult, min(n, cap) + 1, mult):
        if n % t == 0:
            best = t
    return best if best is not None else n


def _matmul(name, a, b, mode, add=None, out_dtype=F32):
    if mode == "nn":
        (M, K), N = a.shape, b.shape[1]
    elif mode == "nt":
        (M, K), N = a.shape, b.shape[0]
    else:
        (K, M), N = a.shape, b.shape[1]
    tm = _pick(M, 512, 128 if mode == "tn" else 8)
    tn = _pick(N, 1536, 128)
    tk = K if K <= 1536 else _pick(K, 1152, 128)
    nk = K // tk
    if mode == "nn":
        a_spec = pl.BlockSpec((tm, tk), lambda i, j, k: (i, k))
        b_spec = pl.BlockSpec((tk, tn), lambda i, j, k: (k, j))
        ca, cb = 1, 0
    elif mode == "nt":
        a_spec = pl.BlockSpec((tm, tk), lambda i, j, k: (i, k))
        b_spec = pl.BlockSpec((tn, tk), lambda i, j, k: (j, k))
        ca, cb = 1, 1
    else:
        a_spec = pl.BlockSpec((tk, tm), lambda i, j, k: (k, i))
        b_spec = pl.BlockSpec((tk, tn), lambda i, j, k: (k, j))
        ca, cb = 0, 0
    in_specs, operands = [a_spec, b_spec], [a, b]
    if add is not None:
        in_specs.append(pl.BlockSpec((tm, tn), lambda i, j, k: (i, j)))
        operands.append(add)

    def body(*refs):
        a_ref, b_ref = refs[0], refs[1]
        o_ref, acc_ref = refs[-2], refs[-1]
        k = pl.program_id(2)

        @pl.when(k == 0)
        def _():
            acc_ref[...] = jnp.zeros_like(acc_ref)

        acc_ref[...] += _dot(a_ref[...], b_ref[...], ca, cb)

        @pl.when(k == nk - 1)
        def _():
            r = acc_ref[...]
            if add is not None:
                r = r + refs[2][...]
            o_ref[...] = r.astype(o_ref.dtype)

    return pl.pallas_call(
        body, grid=(M // tm, N // tn, nk), in_specs=in_specs,
        out_specs=pl.BlockSpec((tm, tn), lambda i, j, k: (i, j)),
        out_shape=jax.ShapeDtypeStruct((M, N), out_dtype),
        scratch_shapes=[pltpu.VMEM((tm, tn), F32)], name=name,
        compiler_params=_cparams(("parallel", "parallel", "arbitrary")))(*operands)


def _seg_copies(segs, c):
    out = []
    for lo, hi, dst in segs:
        n = lo
        while n < hi:
            p = n // c
            w = min(hi, (p + 1) * c) - n
            out.append((p, n - p * c, w, dst + n - lo))
            n += w
    return out


def _col_assemble(name, blocks, copies, zeros, n_out):
    _, R, c = blocks.shape
    tm = R if R <= 128 else 128

    def body(b_ref, o_ref):
        for p, s, w, d in copies:
            o_ref[:, d:d + w] = b_ref[p, :, s:s + w]
        for lo, hi in zeros:
            o_ref[:, lo:hi] = jnp.zeros((tm, hi - lo), o_ref.dtype)

    return pl.pallas_call(
        body, grid=(R // tm,), in_specs=[pl.BlockSpec((N_DEV, tm, c), lambda i: (0, i, 0))],
        out_specs=pl.BlockSpec((tm, n_out), lambda i: (i, 0)),
        out_shape=jax.ShapeDtypeStruct((R, n_out), blocks.dtype), name=name, compiler_params=_cparams(("parallel",)))(blocks)


def _col_split(name, full, copies, c, dtype):
    R, n = full.shape
    tm = R if R <= 128 else 128

    def body(f_ref, o_ref):
        for p, s, w, d in copies:
            o_ref[p, :, s:s + w] = f_ref[:, d:d + w].astype(dtype)

    return pl.pallas_call(
        body, grid=(R // tm,), in_specs=[pl.BlockSpec((tm, n), lambda i: (i, 0))],
        out_specs=pl.BlockSpec((N_DEV, tm, c), lambda i: (0, i, 0)),
        out_shape=jax.ShapeDtypeStruct((N_DEV, R, c), dtype), name=name, compiler_params=_cparams(("parallel",)))(full)


def _rmsnorm_fn(x, g):
    x = x.astype(F32)
    return (x * lax.rsqrt(jnp.mean(x * x, axis=-1, keepdims=True) + EPS) * g,)


def _gate_merge_fn(g0, g1, g2, ya, yb, yc, b0, b1, b2):
    return (jax.nn.sigmoid(g0 + b0) * ya + jax.nn.sigmoid(g1 + b1) * yb + jax.nn.sigmoid(g2 + b2) * yc,)


def _gated_norm_fn(y, z, w):
    t = y * _silu(z)
    return (t * lax.rsqrt(jnp.mean(t * t, axis=-1, keepdims=True) + EPS) * w,)


def _att_merge_fn(o0, o1, o2, l0, l1, l2):
    m = lax.stop_gradient(jnp.maximum(jnp.maximum(l0, l1), l2))
    e0, e1, e2 = jnp.exp(l0 - m), jnp.exp(l1 - m), jnp.exp(l2 - m)
    return ((e0 * o0 + e1 * o1 + e2 * o2) / (e0 + e1 + e2),)


def _loss_fn(x, tgt, g):
    (y,) = _rmsnorm_fn(x, g)
    err = y - tgt
    return 0.5 * jnp.sum(jnp.mean(err * err, axis=-1, keepdims=True), axis=0, keepdims=True)


def _pool_fn(x, wg, scale):
    g = pl.program_id(0)
    s2 = x + _shift(x, 1)
    s4 = s2 + _shift(s2, 2)
    s8 = s4 + _shift(s4, 4)
    s16 = s8 + _shift(s8, 8)
    win = ((g == 0).astype(F32) * s2 + (g == 1).astype(F32) * s4 + (g == 2).astype(F32) * s8
           + (g == 3).astype(F32) * s16)
    t = lax.broadcasted_iota(jnp.int32, (x.shape[0], 1), 0) + 1
    cnt = jnp.minimum(t, jnp.left_shift(2, g)).astype(F32)
    d = win / cnt - x
    return (_mm(d, wg.reshape(256, 256)) * scale,)


def _dwconv(x, taps, b):
    k = len(taps)
    y = taps[k - 1] * x + b
    for i in range(k - 1):
        y = y + taps[i] * _shift(x, k - 1 - i)
    return y


def _ssd_conv_fn(x, w0, w1, w2, w3, b):
    return (_silu(_dwconv(x, (w0, w1, w2, w3), b)),)


def _ffn_act_fn(xa, xv, a0, a1, a2, ab, v0, v1, v2, vb):
    return (_silu(_dwconv(xa, (a0, a1, a2), ab)) * _dwconv(xv, (v0, v1, v2), vb),)


def _att_block(q, kp, kc, vp, vc, bpa, bpb, bca, bcb, prev_ok):
    qi = lax.broadcasted_iota(jnp.int32, (ATT_BLOCK, ATT_BLOCK), 0)
    kj = lax.broadcasted_iota(jnp.int32, (ATT_BLOCK, ATT_BLOCK), 1)
    lane = lax.broadcasted_iota(jnp.int32, (1, 2 * HEAD_DIM), 1)
    mask_c = kj <= qi
    mask_p = jnp.logical_and(kj >= qi, prev_ok)
    q = q.astype(F32)
    o = None
    lse = None
    for hh, (bp, bc) in enumerate(((bpa, bca), (bpb, bcb))):
        hm = (lane // HEAD_DIM == hh).astype(F32)
        qh = q * (hm * (1.0 / math.sqrt(HEAD_DIM)))
        sp = jnp.where(mask_p, _mm_nt(qh, kp) + bp, NEG)
        sc = jnp.where(mask_c, _mm_nt(qh, kc) + bc, NEG)
        m = lax.stop_gradient(jnp.maximum(jnp.max(sp, axis=1, keepdims=True), jnp.max(sc, axis=1, keepdims=True)))
        pp = jnp.exp(sp - m)
        pc = jnp.exp(sc - m)
        l = jnp.sum(pp, axis=1, keepdims=True) + jnp.sum(pc, axis=1, keepdims=True)
        oh = (_mm(pp, vp) + _mm(pc, vc)) / l * hm
        lh = (m + jnp.log(l)) * hm
        o = oh if o is None else o + oh
        lse = lh if lse is None else lse + lh
    return o, lse


def _att_specs(nb, ncolblk, clamp):
    def cur(p, r, j):
        return (jnp.minimum(j, nb - 1) if clamp else j, r * 3 + p)

    def prev(p, r, j):
        jj = jnp.minimum(j, nb - 1) if clamp else j
        return (jnp.maximum(jj - 1, 0), r * 3 + p)

    def done(p, r, j):
        return (jnp.maximum(j - 1, 0), r * 3 + p)

    blk = (ATT_BLOCK, 2 * HEAD_DIM)
    return pl.BlockSpec(blk, cur), pl.BlockSpec(blk, prev), pl.BlockSpec(blk, done)


def _bias_specs():
    return [pl.BlockSpec((None, ATT_BLOCK, ATT_BLOCK), functools.partial(lambda p, r, j, hh: (2 * p + hh, 0, 0), hh=hh))
            for hh in (0, 1)]


def _bias_grad_spec():
    return pl.BlockSpec((None, ATT_BLOCK, ATT_BLOCK), lambda p, r, j: (p, 0, 0))


def _att_fwd(name, q, k, v, bias_p, bias_c, dil):
    L = S // dil
    nb = L // ATT_BLOCK
    cur, prev, _ = _att_specs(nb, dil * 3, False)
    bsp = _bias_specs()

    def body(q_ref, kp_ref, kc_ref, vp_ref, vc_ref, bpa, bpb, bca, bcb, o_ref, l_ref):
        prev_ok = pl.program_id(2) > 0
        o, lse = _att_block(q_ref[...], kp_ref[...], kc_ref[...], vp_ref[...], vc_ref[...],
                            bpa[...], bpb[...], bca[...], bcb[...], prev_ok)
        o_ref[...] = o
        l_ref[...] = lse

    shp = jax.ShapeDtypeStruct((L, dil * ATT_GW), F32)
    return pl.pallas_call(
        body, grid=(3, dil, nb), in_specs=[cur, prev, cur, prev, cur, bsp[0], bsp[1], bsp[0], bsp[1]],
        out_specs=[cur, cur], out_shape=[shp, shp], name=name,
        compiler_params=_cparams(("arbitrary",) * 3))(q, k, k, v, v, bias_p, bias_p, bias_c, bias_c)


def _att_bwd(name, q, k, v, bias_p, bias_c, do, dl, dil):
    L = S // dil
    nb = L // ATT_BLOCK
    cur, prev, done = _att_specs(nb, dil * 3, True)
    bsp = _bias_specs()
    gsp = _bias_grad_spec()

    def body(q_ref, kp_ref, kc_ref, vp_ref, vc_ref, bpa, bpb, bca, bcb, do_ref, dl_ref,
             dq_ref, dk_ref, dv_ref, gpa, gpb, gca, gcb, ck, cv):
        r, j = pl.program_id(1), pl.program_id(2)

        @pl.when(jnp.logical_and(r == 0, j == 0))
        def _():
            for g in (gpa, gpb, gca, gcb):
                g[...] = jnp.zeros_like(g)

        @pl.when(j == 0)
        def _():
            ck[...] = jnp.zeros_like(ck)
            cv[...] = jnp.zeros_like(cv)

        @pl.when(j < nb)
        def _():
            prev_ok = j > 0
            prim = (q_ref[...], kp_ref[...], kc_ref[...], vp_ref[...], vc_ref[...],
                    bpa[...], bpb[...], bca[...], bcb[...])
            _, vjp = jax.vjp(lambda *a: _att_block(*a, prev_ok), *prim)
            dq, dkp, dkc, dvp, dvc, dpa, dpb, dca, dcb = vjp((do_ref[...], dl_ref[...]))
            dq_ref[...] = dq.astype(F32)
            gpa[...] += dpa
            gpb[...] += dpb
            gca[...] += dca
            gcb[...] += dcb
            dk_ref[...] = ck[...] + dkp.astype(F32)
            dv_ref[...] = cv[...] + dvp.astype(F32)
            ck[...] = dkc.astype(F32)
            cv[...] = dvc.astype(F32)

        @pl.when(j == nb)
        def _():
            dk_ref[...] = ck[...]
            dv_ref[...] = cv[...]

    shp = jax.ShapeDtypeStruct((L, dil * ATT_GW), F32)
    gshp = jax.ShapeDtypeStruct((3, ATT_BLOCK, ATT_BLOCK), F32)
    res = pl.pallas_call(
        body, grid=(3, dil, nb + 1),
        in_specs=[cur, prev, cur, prev, cur, bsp[0], bsp[1], bsp[0], bsp[1], cur, cur],
        out_specs=[cur, done, done, gsp, gsp, gsp, gsp],
        out_shape=[shp, shp, shp, gshp, gshp, gshp, gshp],
        scratch_shapes=[pltpu.VMEM((ATT_BLOCK, 2 * HEAD_DIM), F32)] * 2, name=name,
        compiler_params=_cparams(("arbitrary",) * 3))(q, k, k, v, v, bias_p, bias_p, bias_c, bias_c, do, dl)
    dq, dk, dv, gpa, gpb, gca, gcb = res
    heads = lambda a, b: jnp.stack([a, b], axis=1).reshape(6, ATT_BLOCK, ATT_BLOCK)
    return dq, dk, dv, heads(gpa, gpb), heads(gca, gcb)


N_PAIR = SSD_HEADS // 2


def _ssd_chunk(xs, bs, cs_in, dt_raw, hs, a_row, dtb_row, ds):
    lane = lax.broadcasted_iota(jnp.int32, (1, 128), 1)
    row = lax.broadcasted_iota(jnp.int32, (128, 1), 0)
    tril = _tri(True)
    dt = _softplus(dt_raw + dtb_row)
    acs = _cumsum_rows(dt * a_row)
    acs_t = acs.T
    gmat = [_mm_nt(cs_in[g], bs[g]) for g in range(2)]
    lo = lane < HEAD_DIM
    lo_r = row < HEAD_DIM
    last = (row == SSD_CHUNK - 1).astype(F32)
    ys, hn = [], []
    for p in range(N_PAIR):
        g = p // (N_PAIR // 2)
        col, dtc, mm, clast = [], [], [], []
        for hh in range(2):
            h = 2 * p + hh
            oh = (lane == h).astype(F32)
            c_col = jnp.sum(acs * oh, axis=1, keepdims=True)
            c_row = jnp.sum(acs_t * (row == h).astype(F32), axis=0, keepdims=True)
            col.append(c_col)
            dtc.append(jnp.sum(dt * oh, axis=1, keepdims=True))
            clast.append(jnp.sum(c_col * last, axis=0, keepdims=True))
            mm.append(gmat[g] * jnp.exp(jnp.where(tril, c_col - c_row, NEG)))
        x = xs[p]
        xd = x * jnp.where(lo, dtc[0], dtc[1])
        y = jnp.where(lo, _mm(mm[0], xd), _mm(mm[1], xd))
        y = y + jnp.where(lo, jnp.exp(col[0]), jnp.exp(col[1])) * _mm_nt(cs_in[g], hs[p])
        ys.append(y + ds[p] * x)
        dec = jnp.where(lo, jnp.exp(clast[0] - col[0]), jnp.exp(clast[1] - col[1]))
        hn.append(hs[p] * jnp.where(lo_r, jnp.exp(clast[0]), jnp.exp(clast[1])) + _mm_tn(xd * dec, bs[g]))
    return tuple(ys), tuple(hn)


def _ssd_load(xbc_ref, dt_ref, a_ref, dtb_ref, d_ref):
    xs = tuple(xbc_ref[:, 128 * p:128 * (p + 1)] for p in range(N_PAIR))
    bs = tuple(xbc_ref[:, D + 128 * g:D + 128 * (g + 1)] for g in range(2))
    cs = tuple(xbc_ref[:, D + 256 + 128 * g:D + 256 + 128 * (g + 1)] for g in range(2))
    ds = tuple(d_ref[:, 128 * p:128 * (p + 1)] for p in range(N_PAIR))
    return xs, bs, cs, dt_ref[...], a_ref[...], dtb_ref[...], ds


def _ssd_fwd(name, xbc_c, proj, a_row, dtb_row, d_exp):
    nc = S // SSD_CHUNK
    prow = lambda n: pl.BlockSpec((1, n), lambda c: (0, 0))

    def body(xbc_ref, dt_ref, a_ref, dtb_ref, d_ref, y_ref, st_ref, h_ref):
        @pl.when(pl.program_id(0) == 0)
        def _():
            h_ref[...] = jnp.zeros_like(h_ref)

        xs, bs, cs, dt_raw, a, dtb, ds = _ssd_load(xbc_ref, dt_ref, a_ref, dtb_ref, d_ref)
        hs = tuple(h_ref[p] for p in range(N_PAIR))
        ys, hn = _ssd_chunk(xs, bs, cs, dt_raw, hs, a, dtb, ds)
        for p in range(N_PAIR):
            y_ref[:, 128 * p:128 * (p + 1)] = ys[p]
            st_ref[p] = hs[p]
            h_ref[p] = hn[p]

    return pl.pallas_call(
        body, grid=(nc,),
        in_specs=[pl.BlockSpec((SSD_CHUNK, SSD_XBC), lambda c: (c, 0)),
                  pl.BlockSpec((SSD_CHUNK, DT_PAD), lambda c: (c, OFF_DT // DT_PAD)),
                  prow(128), prow(128), prow(D)],
        out_specs=[pl.BlockSpec((SSD_CHUNK, D), lambda c: (c, 0)),
                   pl.BlockSpec((None, N_PAIR, 128, 128), lambda c: (c, 0, 0, 0))],
        out_shape=[jax.ShapeDtypeStruct((S, D), F32), jax.ShapeDtypeStruct((nc, N_PAIR, 128, 128), F32)],
        scratch_shapes=[pltpu.VMEM((N_PAIR, 128, 128), F32)], name=name,
        compiler_params=_cparams(("arbitrary",)))(xbc_c, proj, a_row, dtb_row, d_exp)


def _ssd_bwd(name, xbc_c, proj, states, dy, a_row, dtb_row, d_exp):
    nc = S // SSD_CHUNK
    prow = lambda n: pl.BlockSpec((1, n), lambda i: (0, 0))
    rc = lambda i: nc - 1 - i

    def body(xbc_ref, dt_ref, st_ref, dy_ref, a_ref, dtb_ref, d_ref, dxbc_ref, ddt_ref, da_ref, ddtb_ref, dd_ref, e_ref):
        i = pl.program_id(0)

        @pl.when(i == 0)
        def _():
            e_ref[...] = jnp.zeros_like(e_ref)
            da_ref[...] = jnp.zeros_like(da_ref)
            ddtb_ref[...] = jnp.zeros_like(ddtb_ref)
            dd_ref[...] = jnp.zeros_like(dd_ref)

        xs, bs, cs, dt_raw, a, dtb, ds = _ssd_load(xbc_ref, dt_ref, a_ref, dtb_ref, d_ref)
        hs = tuple(st_ref[p] for p in range(N_PAIR))
        _, vjp = jax.vjp(_ssd_chunk, xs, bs, cs, dt_raw, hs, a, dtb, ds)
        dys = tuple(dy_ref[:, 128 * p:128 * (p + 1)] for p in range(N_PAIR))
        es = tuple(e_ref[p] for p in range(N_PAIR))
        dxs, dbs, dcs, ddt, dhs, da, ddtb, dds = vjp((dys, es))
        for p in range(N_PAIR):
            dxbc_ref[:, 128 * p:128 * (p + 1)] = dxs[p]
            e_ref[p] = dhs[p]
            dd_ref[:, 128 * p:128 * (p + 1)] += dds[p]
        for g in range(2):
            dxbc_ref[:, D + 128 * g:D + 128 * (g + 1)] = dbs[g]
            dxbc_ref[:, D + 256 + 128 * g:D + 256 + 128 * (g + 1)] = dcs[g]
        ddt_ref[...] = ddt
        da_ref[...] += da
        ddtb_ref[...] += ddtb

    return pl.pallas_call(
        body, grid=(nc,),
        in_specs=[pl.BlockSpec((SSD_CHUNK, SSD_XBC), lambda i: (rc(i), 0)),
                  pl.BlockSpec((SSD_CHUNK, DT_PAD), lambda i: (rc(i), OFF_DT // DT_PAD)),
                  pl.BlockSpec((None, N_PAIR, 128, 128), lambda i: (rc(i), 0, 0, 0)),
                  pl.BlockSpec((SSD_CHUNK, D), lambda i: (rc(i), 0)),
                  prow(128), prow(128), prow(D)],
        out_specs=[pl.BlockSpec((SSD_CHUNK, SSD_XBC), lambda i: (rc(i), 0)),
                   pl.BlockSpec((SSD_CHUNK, DT_PAD), lambda i: (rc(i), 0)),
                   prow(128), prow(128), prow(D)],
        out_shape=[jax.ShapeDtypeStruct((S, SSD_XBC), F32), jax.ShapeDtypeStruct((S, DT_PAD), F32),
                   jax.ShapeDtypeStruct((1, 128), F32), jax.ShapeDtypeStruct((1, 128), F32),
                   jax.ShapeDtypeStruct((1, D), F32)],
        scratch_shapes=[pltpu.VMEM((N_PAIR, 128, 128), F32)], name=name,
        compiler_params=_cparams(("arbitrary",)))(xbc_c, proj, states, dy, a_row, dtb_row, d_exp)


def _exchange(name, arrays, scatter):
    n = len(arrays)
    flips = [(dx, dy, dc) for dx in (0, 1) for dy in (0, 1) for dc in (0, 1) if dx or dy or dc]

    def body(*refs):
        ins, outs = refs[:n], refs[n:2 * n]
        send_sems, recv_sems, loc_sems = refs[2 * n:]
        x, y, c = lax.axis_index("x"), lax.axis_index("y"), lax.axis_index("c")
        me = 4 * x + 2 * y + c
        peers = []
        for dx, dy, dc in flips:
            px, py, pc = (1 - x if dx else x), (1 - y if dy else y), (1 - c if dc else c)
            peers.append(((px, py, pc), 4 * px + 2 * py + pc))

        def remote(k, j, landed_from):
            dev, pid = peers[j]
            src = ins[k].at[pid] if scatter else ins[k]
            return pltpu.make_async_remote_copy(
                src_ref=src, dst_ref=outs[k].at[landed_from], send_sem=send_sems.at[k, j], recv_sem=recv_sems.at[k, j],
                device_id=dev, device_id_type=pl.DeviceIdType.MESH)

        local = [pltpu.make_async_copy(ins[k].at[me] if scatter else ins[k], outs[k].at[me], loc_sems.at[k])
                 for k in range(n)]
        for cp in local:
            cp.start()
        for k in range(n):
            for j in range(len(flips)):
                remote(k, j, me).start()
        for cp in local:
            cp.wait()
        for k in range(n):
            for j in range(len(flips)):
                remote(k, j, me).wait_send()
                remote(k, j, peers[j][1]).wait_recv()

    hbm = pl.BlockSpec(memory_space=pltpu.HBM)
    out_shape = [jax.ShapeDtypeStruct(a.shape if scatter else (N_DEV,) + a.shape, a.dtype) for a in arrays]
    res = pl.pallas_call(
        body, in_specs=[hbm] * n, out_specs=[hbm] * n, out_shape=out_shape, name=name,
        scratch_shapes=[pltpu.SemaphoreType.DMA((n, len(flips))), pltpu.SemaphoreType.DMA((n, len(flips))),
                        pltpu.SemaphoreType.DMA((n,))])(*arrays)
    return list(res)


def _peer_copies(ins, lands, send_sems, recv_sems, loc_sems, scatter):
    n = len(ins)
    flips = [(dx, dy, dc) for dx in (0, 1) for dy in (0, 1) for dc in (0, 1) if dx or dy or dc]
    x, y, c = lax.axis_index("x"), lax.axis_index("y"), lax.axis_index("c")
    me = 4 * x + 2 * y + c
    peers = []
    for dx, dy, dc in flips:
        px, py, pc = (1 - x if dx else x), (1 - y if dy else y), (1 - c if dc else c)
        peers.append(((px, py, pc), 4 * px + 2 * py + pc))

    def remote(k, j, slot):
        dev, pid = peers[j]
        return pltpu.make_async_remote_copy(
            src_ref=ins[k].at[pid] if scatter else ins[k], dst_ref=lands[k].at[slot],
            send_sem=send_sems.at[k * N_FLIP + j], recv_sem=recv_sems.at[k * N_FLIP + j],
            device_id=dev, device_id_type=pl.DeviceIdType.MESH)

    local = [pltpu.make_async_copy(ins[k].at[me] if scatter else ins[k], lands[k].at[me], loc_sems.at[k])
             for k in range(n)]
    pairs = [(k, j) for k in range(n) for j in range(len(flips))]
    sent = lambda k, j: remote(k, j, me)
    landed = lambda k, j: remote(k, j, peers[j][1])
    return local, pairs, sent, landed


_HBM = pl.BlockSpec(memory_space=pltpu.HBM)
_SEM = pl.BlockSpec(memory_space=pltpu.SEMAPHORE)
N_FLIP = N_DEV - 1


def _exchange_start(name, arrays, scatter, after):
    n = len(arrays)
    arrays = [pltpu.with_memory_space_constraint(a, pltpu.HBM) for a in arrays]
    lands = [pltpu.with_memory_space_constraint(
        lax.empty(a.shape if scatter else (N_DEV,) + a.shape, a.dtype), pltpu.HBM) for a in arrays]

    def body(*refs):
        ins, lnd = refs[:n], refs[n:2 * n]
        send_sems, recv_sems, loc_sems = refs[2 * n + 1:2 * n + 4]
        token = refs[-1]
        local, pairs, sent, _ = _peer_copies(ins, lnd, send_sems, recv_sems, loc_sems, scatter)
        for cp in local:
            cp.start()
        for k, j in pairs:
            sent(k, j).start()
        token[...] = jnp.zeros_like(token)

    res = pl.pallas_call(
        body, name=name,
        in_specs=[_HBM] * (2 * n) + [pl.BlockSpec(memory_space=pl.ANY)],
        out_specs=[_SEM, _SEM, _SEM] + [_HBM] * (2 * n) + [pl.BlockSpec(memory_space=pltpu.VMEM)],
        out_shape=[pltpu.SemaphoreType.DMA((n * N_FLIP,)), pltpu.SemaphoreType.DMA((n * N_FLIP,)), pltpu.SemaphoreType.DMA((n,))]
        + [pltpu.HBM(a.shape, a.dtype) for a in arrays] + [pltpu.HBM(a.shape, a.dtype) for a in lands]
        + [jax.ShapeDtypeStruct((8, 128), F32)],
        input_output_aliases={k: 3 + k for k in range(2 * n)},
        compiler_params=pltpu.CompilerParams(has_side_effects=pltpu.SideEffectType.DATAFLOW_SIDE_EFFECTING),
    )(*arrays, *lands, after)
    return (res[:3], res[3:3 + n], res[3 + n:3 + 2 * n], scatter), res[-1]


def _exchange_wait(name, state, after):
    sems, ins_thru, lands_thru, scatter = state
    n = len(ins_thru)

    def body(*refs):
        ins, lnd = refs[:n], refs[n:2 * n]
        send_sems, recv_sems, loc_sems = refs[2 * n:2 * n + 3]
        local, pairs, sent, landed = _peer_copies(ins, lnd, send_sems, recv_sems, loc_sems, scatter)
        for cp in local:
            cp.wait()
        for k, j in pairs:
            sent(k, j).wait_send()
            landed(k, j).wait_recv()

    res = pl.pallas_call(
        body, name=name,
        in_specs=[_HBM] * (2 * n) + [_SEM, _SEM, _SEM] + [pl.BlockSpec(memory_space=pl.ANY)],
        out_specs=[_HBM] * (2 * n),
        out_shape=[pltpu.HBM(a.shape, a.dtype) for a in ins_thru] + [pltpu.HBM(a.shape, a.dtype) for a in lands_thru],
        input_output_aliases={k: k for k in range(2 * n)},
        compiler_params=pltpu.CompilerParams(has_side_effects=pltpu.SideEffectType.DATAFLOW_SIDE_EFFECTING),
    )(*ins_thru, *lands_thru, *sems, after)
    return list(res[n:])


def _adamw_fn(*vals):
    slots, (w, m, v) = vals[:N_DEV], vals[N_DEV:]
    g = slots[0].astype(F32)
    for s in slots[1:]:
        g = g + s.astype(F32)
    m2 = ADAM_B1 * m + (1.0 - ADAM_B1) * g
    v2 = ADAM_B2 * v + (1.0 - ADAM_B2) * (g * g)
    m_hat = m2 / (1.0 - ADAM_B1 ** ADAM_STEP)
    v_hat = v2 / (1.0 - ADAM_B2 ** ADAM_STEP)
    delta = -ADAM_LR * (m_hat / (jnp.sqrt(v_hat) + ADAM_EPS) + ADAM_WD * w)
    return (g, delta, m2, v2), ()


def _adamw(name, slots, w, m, v, first_row=0, prev=None):
    R, C = slots.shape[1:]
    tm = R if R <= 128 else _pick(R, 128 if C > D else 256, 8)
    rows = ([_rows(slots, C, lead=s) for s in range(N_DEV)]
            + [_rows(a, C, roff=first_row // tm) for a in (w, m, v)])
    return _tiled(name, _adamw_fn, (1, R // tm), tm, rows, [], [(w.shape[0], C, F32)] * 4,
                  out_roff=first_row // tm, prev_outs=prev)


def _bucket_onehots():
    out = []
    qi = jnp.arange(ATT_BLOCK)[:, None]
    kj = jnp.arange(ATT_BLOCK)[None, :]
    max_exact = REL_BUCKETS // 2
    for _, dil in ATT_GROUPS:
        parts = []
        for rel in (qi + ATT_BLOCK - kj, qi - kj):
            dist = jnp.clip(rel, 0, None) * dil
            nf = jnp.maximum(dist, 1).astype(F32)
            large = max_exact + (jnp.log(nf / max_exact) / math.log(REL_MAX_DISTANCE / max_exact)
                                 * (REL_BUCKETS - max_exact)).astype(jnp.int32)
            large = jnp.minimum(large, REL_BUCKETS - 1)
            bucket = jnp.where(dist < max_exact, dist, large)
            parts.append((bucket[:, :, None] == jnp.arange(REL_BUCKETS)[None, None, :]).astype(F32))
        out.append(jnp.stack(parts))
    return out


SHARDED = ("w_in", "w_a", "pool_w", "w_b", "ssd_conv_w", "w_c", "w_o", "ffn_w_up", "ffn_conv_w", "ffn_w_down")
MATMUL_WEIGHTS = ("w_in", "w_a", "pool_w", "w_b", "w_c", "w_o", "ffn_w_up", "ffn_w_down")
ROW_SHARDED = ("w_b", "w_c", "w_o", "ffn_w_down")
W_IN_SEGS = ((0, 1152, OFF_Q), (1152, 2304, OFF_K), (2304, 3456, OFF_V), (3456, 4480, OFF_POOL), (4480, 5504, OFF_Z),
             (5504, 7040, OFF_XBC), (7040, 7056, OFF_DT), (7056, IN_WIDTH, OFF_GATE))
COL_SHARDED = {
    "w_in": (IN_WIDTH // N_DEV, W_IN_SEGS, ((OFF_DT + SSD_HEADS, OFF_Q),), NP),
    "w_a": (D // N_DEV, ((0, D, 0),), (), D),
    "ffn_w_up": (2 * D_FF // N_DEV, ((0, 2 * D_FF, 0),), (), 2 * D_FF),
    "ssd_conv_w": (SSD_XBC // N_DEV, ((0, SSD_XBC, 0),), (), SSD_XBC),
    "ffn_conv_w": (2 * D_FF // N_DEV, ((0, 2 * D_FF, 0),), (), 2 * D_FF),
}
REPLICATED = ("rel_bias", "ln1_g", "b_gate", "pool_scale", "ssd_conv_b", "ssd_dt_bias", "ssd_a_log", "ssd_d",
              "ssd_norm_w", "ln2_g", "ffn_conv_b", "final_g")
WEIGHTS = ("rel_bias", "ln1_g", "w_in", "b_gate", "w_a", "pool_w", "pool_scale", "w_b", "ssd_conv_w", "ssd_conv_b",
           "ssd_dt_bias", "ssd_a_log", "ssd_d", "ssd_norm_w", "w_c", "w_o", "ln2_g", "ffn_w_up", "ffn_conv_w",
           "ffn_conv_b", "ffn_w_down", "final_g")


def _local_weight(name, n, blocks):
    if n in COL_SHARDED:
        c, segs, zeros, width = COL_SHARDED[n]
        return _col_assemble(name, blocks, _seg_copies(segs, c), zeros, width)
    if n in ROW_SHARDED:
        return blocks.reshape(-1, blocks.shape[-1])
    return blocks


def _device_blocks(name, n, g):
    if n in COL_SHARDED:
        c, segs, _, _ = COL_SHARDED[n]
        return _col_split(name, g, _seg_copies(segs, c), c, BF16)
    if n in ROW_SHARDED:
        return g.reshape(N_DEV, g.shape[0] // N_DEV, g.shape[1]).astype(BF16)
    return g.astype(BF16)


def _to_residue(t, dil):
    return t.reshape(S // dil, dil * ATT_GW)


def _row(v, n=None):
    v = v.reshape(1, -1)
    if n is not None and v.shape[1] < n:
        v = jnp.pad(v, ((0, 0), (0, n - v.shape[1])))
    return v


RT = 256


def _row_call(name, fn, cw, ncol, rows, params, outs, accs=()):
    return _tiled(name, fn, (ncol, S // RT), RT, rows, params, [(S, cw, dt) for dt in outs], accs)


def _col_call(name, fn, tc, ncol, rows, params, outs, accs=()):
    return _tiled(name, fn, (ncol, 1), S, rows, params, [(S, tc, dt) for dt in outs], accs)


def _fwd_only(fn):
    return lambda *a: (fn(*a), ())


def _layer_fwd(i, x, W, P, bias_tabs):
    sv = {"x": x}
    (u,) = _row_call(f"ln1_f{i}", _fwd_only(_rmsnorm_fn), D, 1, [_rows(x, D)], [_p_row(P["ln1_g"], D)], [BF16])
    proj = _matmul(f"inproj_f{i}", u, W["w_in"], "nn")
    sv["u"], sv["proj"] = u, proj

    os_, ls_, res = [], [], []
    for gi, (_, dil) in enumerate(ATT_GROUPS):
        q, k, v = (_to_residue(proj[:, off + gi * ATT_GW: off + (gi + 1) * ATT_GW].astype(BF16), dil)
                   for off in (OFF_Q, OFF_K, OFF_V))
        o, lse = _att_fwd(f"att_f{i}_{gi}", q, k, v, bias_tabs[gi][0], bias_tabs[gi][1], dil)
        res.append((q, k, v))
        os_.append(o.reshape(S, ATT_GW))
        ls_.append(lse.reshape(S, ATT_GW))
    sv["att_res"], sv["att_o"], sv["att_l"] = res, os_, ls_
    (att,) = _row_call(f"attmerge_f{i}", _fwd_only(_att_merge_fn), 128, 3,
                       [_rows(t, 128) for t in os_ + ls_], [], [BF16])
    y_a = _matmul(f"wa_f{i}", att, W["w_a"], "nn")
    sv["att"], sv["y_a"] = att, y_a

    pool_params = [(W["pool_w"], (N_DEV, None, 32, 256), lambda j, i_: (0, j, 0, 0)), _p_row(P["pool_scale"], 256)]
    (yb_pre,) = _col_call(f"pool_f{i}", _fwd_only(_pool_fn), 256, 4, [_rows(proj, 256, OFF_POOL // 256)],
                          pool_params, [BF16])
    y_b = _matmul(f"wb_f{i}", yb_pre, W["w_b"], "nn")
    sv["yb_pre"], sv["y_b"] = yb_pre, y_b

    conv_params = [_p_row(P["ssd_conv_w"][k], 128) for k in range(4)] + [_p_row(P["ssd_conv_b"], 128)]
    (xbc_c,) = _col_call(f"ssdconv_f{i}", _fwd_only(_ssd_conv_fn), 128, SSD_XBC // 128,
                         [_rows(proj, 128, OFF_XBC // 128)], conv_params, [F32])
    y_ssd, states = _ssd_fwd(f"ssd_f{i}", xbc_c, proj, P["a_row"], P["dtb_row"], P["d_exp"])
    (yc_pre,) = _row_call(f"ssdnorm_f{i}", _fwd_only(_gated_norm_fn), 512, 2,
                          [_rows(y_ssd, 512), _rows(proj, 512, OFF_Z // 512)], [_p_row(P["ssd_norm_w"], 512)], [BF16])
    y_c = _matmul(f"wc_f{i}", yc_pre, W["w_c"], "nn")
    sv["xbc_c"], sv["states"], sv["y_ssd"], sv["yc_pre"], sv["y_c"] = xbc_c, states, y_ssd, yc_pre, y_c

    gate_rows = [_rows(proj, D, k) for k in range(3)] + [_rows(t, D) for t in (y_a, y_b, y_c)]
    gate_params = [_p_row(P["b_gate"], D, k) for k in range(3)]
    (merged,) = _row_call(f"gate_f{i}", _fwd_only(_gate_merge_fn), D, 1, gate_rows, gate_params, [BF16])
    x1 = _matmul(f"wo_f{i}", merged, W["w_o"], "nn", add=x)
    sv["merged"], sv["x1"] = merged, x1

    (u2,) = _row_call(f"ln2_f{i}", _fwd_only(_rmsnorm_fn), D, 1, [_rows(x1, D)], [_p_row(P["ln2_g"], D)], [BF16])
    up = _matmul(f"up_f{i}", u2, W["ffn_w_up"], "nn")
    (act,) = _col_call(f"ffnact_f{i}", _fwd_only(_ffn_act_fn), 128, D_FF // 128,
                       [_rows(up, 128), _rows(up, 128, D_FF // 128)], _ffn_params(P), [BF16])
    x2 = _matmul(f"down_f{i}", act, W["ffn_w_down"], "nn", add=x1)
    sv["u2"], sv["up"], sv["act"] = u2, up, act
    return x2, sv


def _ffn_params(P):
    nb = D_FF // 128
    return ([_p_row(P["ffn_conv_w"][k], 128) for k in range(3)] + [_p_row(P["ffn_conv_b"], 128)]
            + [_p_row(P["ffn_conv_w"][k], 128, nb) for k in range(3)] + [_p_row(P["ffn_conv_b"], 128, nb)])


def _layer_bwd(i, dx2, sv, W, P, bias_tabs, onehots, on_sharded_grads):
    G = {}
    x, proj, x1 = sv["x"], sv["proj"], sv["x1"]

    dact = _matmul(f"down_bx{i}", dx2, W["ffn_w_down"], "nt", out_dtype=BF16)
    G["ffn_w_down"] = _matmul(f"down_bw{i}", sv["act"], dx2, "tn", out_dtype=BF16)
    nb = D_FF // 128
    up = sv["up"]
    f = _with_vjp(_ffn_act_fn, 10, (0, 1), tuple(range(2, 10)))
    accs = [_a_row(D_FF, 128)] * 8
    dua, duv, a0, a1, a2, ab, v0, v1, v2, vb = _col_call(
        f"ffnact_b{i}", f, 128, nb, [_rows(up, 128), _rows(up, 128, nb)], _ffn_params(P) + [_rows_as_param(dact, 128)],
        [BF16, BF16], accs)
    G["ffn_conv_w"] = jnp.concatenate([jnp.concatenate([a0, a1, a2], 0), jnp.concatenate([v0, v1, v2], 0)], axis=1)
    G["ffn_conv_b"] = jnp.concatenate([ab, vb], axis=1)[0]
    dup = jnp.concatenate([dua, duv], axis=1)
    du2 = _matmul(f"up_bx{i}", dup, W["ffn_w_up"], "nt")
    G["ffn_w_up"] = _matmul(f"up_bw{i}", sv["u2"], dup, "tn", out_dtype=BF16)

    def norm_bwd(x_, g_, du_, dres):
        (dxn,), (dg,) = _with_vjp(_rmsnorm_fn, 2, (0,), (1,))(x_, g_, du_)
        return (dxn + dres,), (dg,)

    (dx1,), (G["ln2_g"],) = _split_res(_row_call(
        f"ln2_b{i}", lambda x_, du_, dres, g_: norm_bwd(x_, g_, du_, dres), D, 1,
        [_rows(x1, D), _rows(du2, D), _rows(dx2, D)], [_p_row(P["ln2_g"], D)], [F32], [_a_row(D, D)]), 1)

    dmerged = _matmul(f"wo_bx{i}", dx1, W["w_o"], "nt")
    G["w_o"] = _matmul(f"wo_bw{i}", sv["merged"], dx1, "tn", out_dtype=BF16)
    gate_rows = [_rows(proj, D, k) for k in range(3)] + [_rows(sv[t], D) for t in ("y_a", "y_b", "y_c")]
    gate_params = [_p_row(P["b_gate"], D, k) for k in range(3)]

    def gate_bwd(g0, g1, g2, ya, yb, yc, dm, b0, b1, b2):
        return _with_vjp(_gate_merge_fn, 9, (0, 1, 2, 3, 4, 5), (6, 7, 8))(g0, g1, g2, ya, yb, yc, b0, b1, b2, dm)

    dg0, dg1, dg2, dya, dyb, dyc, db0, db1, db2 = _row_call(
        f"gate_b{i}", gate_bwd, D, 1, gate_rows + [_rows(dmerged, D)], gate_params,
        [BF16, BF16, BF16, BF16, BF16, BF16], [_a_row(D, D)] * 3)
    G["b_gate"] = jnp.concatenate([db0, db1, db2], axis=1)[0]

    dyc_pre = _matmul(f"wc_bx{i}", dyc, W["w_c"], "nt")
    G["w_c"] = _matmul(f"wc_bw{i}", sv["yc_pre"], dyc, "tn", out_dtype=BF16)

    def gnorm_bwd(y_, z_, dy_, w_):
        return _with_vjp(_gated_norm_fn, 3, (0, 1), (2,))(y_, z_, w_, dy_)

    dy_ssd, dz, dnw = _row_call(
        f"ssdnorm_b{i}", gnorm_bwd, 512, 2,
        [_rows(sv["y_ssd"], 512), _rows(proj, 512, OFF_Z // 512), _rows(dyc_pre, 512)],
        [_p_row(P["ssd_norm_w"], 512)], [F32, BF16], [_a_row(D, 512)])
    G["ssd_norm_w"] = dnw[0]
    dxbc_c, ddt, da_row, ddtb_row, dd_exp = _ssd_bwd(f"ssd_b{i}", sv["xbc_c"], proj, sv["states"], dy_ssd,
                                                     P["a_row"], P["dtb_row"], P["d_exp"])
    a_vec = P["a_row"][0, :SSD_HEADS]
    G["ssd_a_log"] = da_row[0, :SSD_HEADS] * a_vec
    G["ssd_dt_bias"] = ddtb_row[0, :SSD_HEADS]
    G["ssd_d"] = dd_exp.reshape(SSD_HEADS, HEAD_DIM).sum(axis=1)
    conv_params = [_p_row(P["ssd_conv_w"][k], 128) for k in range(4)] + [_p_row(P["ssd_conv_b"], 128)]

    def conv_bwd(x_, dy_, w0, w1, w2, w3, b_):
        return _with_vjp(_ssd_conv_fn, 6, (0,), (1, 2, 3, 4, 5))(x_, w0, w1, w2, w3, b_, dy_)

    dxbc, c0, c1, c2, c3, cb = _col_call(
        f"ssdconv_b{i}", conv_bwd, 128, SSD_XBC // 128, [_rows(proj, 128, OFF_XBC // 128), _rows(dxbc_c, 128)],
        conv_params, [BF16], [_a_row(SSD_XBC, 128)] * 5)
    G["ssd_conv_w"] = jnp.concatenate([c0, c1, c2, c3], axis=0)
    G["ssd_conv_b"] = cb[0]

    dyb_pre = _matmul(f"wb_bx{i}", dyb, W["w_b"], "nt")
    G["w_b"] = _matmul(f"wb_bw{i}", sv["yb_pre"], dyb, "tn", out_dtype=BF16)
    pool_params = [(W["pool_w"], (N_DEV, None, 32, 256), lambda j, i_: (0, j, 0, 0)), _p_row(P["pool_scale"], 256)]

    def pool_bwd(x_, dy_, wg, sc):
        return _with_vjp(_pool_fn, 3, (0,), (1, 2))(x_, wg.astype(F32), sc, dy_)

    dpool, dwg, dsc = _col_call(
        f"pool_b{i}", pool_bwd, 256, 4, [_rows(proj, 256, OFF_POOL // 256), _rows(dyb_pre, 256)], pool_params, [BF16],
        [((N_DEV, 4, 32, 256), (N_DEV, None, 32, 256), lambda j, i_: (0, j, 0, 0)), _a_row(D, 256)])
    G["pool_w"] = dwg
    G["pool_scale"] = dsc[0]

    datt = _matmul(f"wa_bx{i}", dya, W["w_a"], "nt")
    G["w_a"] = _matmul(f"wa_bw{i}", sv["att"], dya, "tn", out_dtype=BF16)

    def merge_bwd(o0, o1, o2, l0, l1, l2, da_):
        return _with_vjp(_att_merge_fn, 6, (0, 1, 2, 3, 4, 5), ())(o0, o1, o2, l0, l1, l2, da_)

    dol = _row_call(f"attmerge_b{i}", merge_bwd, 128, 3,
                    [_rows(t, 128) for t in sv["att_o"] + sv["att_l"]] + [_rows(datt, 128)], [], [F32] * 6)
    dqs, dks, dvs = [], [], []
    g_rel = jnp.zeros((REL_BUCKETS, 18), F32)
    for gi, (_, dil) in enumerate(ATT_GROUPS):
        q, k, v = sv["att_res"][gi]
        dq, dk, dv, gbp, gbc = _att_bwd(f"att_b{i}_{gi}", q, k, v, bias_tabs[gi][0], bias_tabs[gi][1],
                                        _to_residue(dol[gi], dil), _to_residue(dol[3 + gi], dil), dil)
        dqs.append(dq.reshape(S, ATT_GW))
        dks.append(dk.reshape(S, ATT_GW))
        dvs.append(dv.reshape(S, ATT_GW))
        oh = onehots[gi]
        gt = (jnp.einsum("hqk,qkb->bh", gbp, oh[0], precision=lax.Precision.HIGHEST)
              + jnp.einsum("hqk,qkb->bh", gbc, oh[1], precision=lax.Precision.HIGHEST))
        g_rel = g_rel.at[:, gi * 6:(gi + 1) * 6].add(gt)
    G["rel_bias"] = g_rel

    dproj = jnp.concatenate(
        [dg0, dg1, dg2, dpool, dz, dxbc, ddt.astype(BF16), jnp.zeros((S, OFF_Q - OFF_DT - DT_PAD), BF16)]
        + [t.astype(BF16) for t in dqs + dks + dvs], axis=1)
    du = _matmul(f"inproj_bx{i}", dproj, W["w_in"], "nt")
    G["w_in"] = _matmul(f"inproj_bw{i}", sv["u"], dproj, "tn", out_dtype=BF16)
    ln1_g = P["ln1_g"] + on_sharded_grads(G)
    (dx,), (G["ln1_g"],) = _split_res(_row_call(
        f"ln1_b{i}", lambda x_, du_, dres, g_: norm_bwd(x_, g_, du_, dres), D, 1,
        [_rows(x, D), _rows(du, D), _rows(dx1, D)], [_p_row(ln1_g, D)], [F32], [_a_row(D, D)]), 1)
    G["ln1_g"] = G["ln1_g"][0]
    G["ln2_g"] = G["ln2_g"][0]
    return dx, G


def _rows_as_param(arr, cw):
    return (arr, (arr.shape[0], cw), lambda j, i: (0, j))


def _split_res(res, n_out):
    return tuple(res[:n_out]), tuple(res[n_out:])


def kernel(x, rel_bias, ln1_g, w_in, b_gate, w_a, pool_w, pool_scale, w_b, ssd_conv_w, ssd_conv_b, ssd_dt_bias, ssd_a_log, ssd_d, ssd_norm_w, w_c, w_o, ln2_g, ffn_w_up, ffn_conv_w, ffn_conv_b, ffn_w_down, final_g, loss_target, m_rel_bias, m_ln1_g, m_w_in, m_b_gate, m_w_a, m_pool_w, m_pool_scale, m_w_b, m_ssd_conv_w, m_ssd_conv_b, m_ssd_dt_bias, m_ssd_a_log, m_ssd_d, m_ssd_norm_w, m_w_c, m_w_o, m_ln2_g, m_ffn_w_up, m_ffn_conv_w, m_ffn_conv_b, m_ffn_w_down, m_final_g, v_rel_bias, v_ln1_g, v_w_in, v_b_gate, v_w_a, v_pool_w, v_pool_scale, v_w_b, v_ssd_conv_w, v_ssd_conv_b, v_ssd_dt_bias, v_ssd_a_log, v_ssd_d, v_ssd_norm_w, v_w_c, v_w_o, v_ln2_g, v_ffn_w_up, v_ffn_conv_w, v_ffn_conv_b, v_ffn_w_down, v_final_g):
    args = locals()
    wts = {n: args[n] for n in WEIGHTS}
    mom = {n: args["m_" + n] for n in WEIGHTS}
    var = {n: args["v_" + n] for n in WEIGHTS}
    names = list(SHARDED)

    onehots = _bucket_onehots()
    bias_tabs = []
    for gi in range(3):
        tab = rel_bias[:, gi * 6:(gi + 1) * 6]
        b = jnp.einsum("pqkb,bh->phqk", onehots[gi], tab, precision=lax.Precision.HIGHEST)
        bias_tabs.append((b[0], b[1]))

    def gather_start(i, after):
        shards = [wts[n][i].astype(BF16) if n in MATMUL_WEIGHTS else wts[n][i] for n in names]
        return _exchange_start(f"gather_start{i}", shards, False, after)

    def layer_params(i, landed):
        full = {n: _local_weight(f"local_{n}{i}", n, g) for n, g in zip(names, landed)}
        W = {n: full[n] for n in MATMUL_WEIGHTS}
        a_vec = -jnp.exp(ssd_a_log[i])
        P = {"ln1_g": _row(ln1_g[i]), "ln2_g": _row(ln2_g[i]), "b_gate": _row(b_gate[i]),
             "pool_scale": _row(pool_scale[i]), "ssd_conv_b": _row(ssd_conv_b[i]),
             "ssd_conv_w": [_row(full["ssd_conv_w"][k]) for k in range(4)],
             "ssd_norm_w": _row(ssd_norm_w[i]), "ffn_conv_b": _row(ffn_conv_b[i]),
             "ffn_conv_w": [_row(full["ffn_conv_w"][k]) for k in range(3)],
             "a_row": _row(a_vec, 128), "dtb_row": _row(ssd_dt_bias[i], 128),
             "d_exp": _row(jnp.repeat(ssd_d[i], HEAD_DIM))}
        return W, P

    h = x.reshape(S, D)
    saved, Ws, Ps = [], [], []
    state, _ = gather_start(0, h)
    landed = _exchange_wait("gather_wait0", state, h)
    for i in range(DEPTH):
        W, P = layer_params(i, landed)
        Ws.append(W)
        Ps.append(P)
        if i + 1 < DEPTH:
            state, token = gather_start(i + 1, landed[0])
            P = dict(P, ln1_g=P["ln1_g"] + token[0, 0])
        h, sv = _layer_fwd(i, h, W, P, bias_tabs)
        saved.append(sv)
        if i + 1 < DEPTH:
            landed = _exchange_wait(f"gather_wait{i + 1}", state, h)

    def loss_bwd(x_, t_, g_):
        lval, vjp = jax.vjp(_loss_fn, x_, t_, g_)
        dx_, _, dg_ = vjp(jnp.ones_like(lval))
        return (dx_,), (dg_, jnp.broadcast_to(lval, (1, 128)))

    dh, g_final, loss_part = _row_call("loss", loss_bwd, D, 1, [_rows(h, D), _rows(loss_target.reshape(S, D), D)],
                                       [_p_row(_row(final_g), D)], [F32], [_a_row(D, D), _a_row(128, 128)])
    loss = lax.psum(loss_part[0, 0], MESH_AXES)

    grads = {n: [None] * DEPTH for n in WEIGHTS if n not in ("rel_bias", "final_g")}
    g_rel = jnp.zeros((REL_BUCKETS, 18), F32)
    slots = [None] * DEPTH
    pending = None
    for i in reversed(range(DEPTH)):
        started = {}

        def on_sharded_grads(G, i=i, started=started):
            parts = [_device_blocks(f"blocks_{n}{i}", n, G[n]) for n in names]
            started["state"], token = _exchange_start(f"scatter_start{i}", parts, True, G["b_gate"])
            return token[0, 0]

        dh, G = _layer_bwd(i, dh, saved[i], Ws[i], Ps[i], bias_tabs, onehots, on_sharded_grads)
        if pending is not None:
            j, st = pending
            slots[j] = _exchange_wait(f"scatter_wait{j}", st, dh)
        pending = (i, started["state"])
        g_rel = g_rel + G.pop("rel_bias")
        for n, g in G.items():
            grads[n][i] = g
    slots[0] = _exchange_wait("scatter_wait0", pending[1], dh)
    grad_x = dh.reshape(1, S, D)
    local = {n: jnp.stack(grads[n]) for n in grads if n not in SHARDED}
    local["rel_bias"] = g_rel
    local["final_g"] = g_final[0]

    out = {}
    for k, n in enumerate(names):
        shp = wts[n].shape
        r, c = int(np.prod(shp[:-1])), shp[-1]
        w2, m2, v2 = wts[n].reshape(r, c), mom[n].reshape(r, c), var[n].reshape(r, c)
        if n in MATMUL_WEIGHTS:
            res = None
            for i in range(DEPTH):
                res = _adamw(f"adamw_{n}{i}", slots[i][k].reshape(N_DEV, r // DEPTH, c), w2, m2, v2,
                             first_row=i * (r // DEPTH), prev=res)
        else:
            stacked = jnp.stack([slots[i][k] for i in range(DEPTH)], axis=1)
            res = _adamw("adamw_" + n, stacked.reshape(N_DEV, r, c), w2, m2, v2)
        out[n] = [t.reshape(shp) for t in res]

    def pack(d):
        flat = jnp.concatenate([d[n].reshape(-1).astype(F32) for n in REPLICATED])
        rows = -(-flat.shape[0] // (8 * 128)) * 8
        return jnp.pad(flat, (0, rows * 128 - flat.shape[0])).reshape(rows, 128)

    (rep_slots,) = _exchange("gather_small_grads", [pack(local)], scatter=False)
    rep = _adamw("adamw_small", rep_slots, pack(wts), pack(mom), pack(var))
    off = 0
    for n in REPLICATED:
        sz = int(np.prod(wts[n].shape))
        out[n] = [t.reshape(-1)[off:off + sz].reshape(wts[n].shape) for t in rep]
        off += sz

    return (loss, grad_x, *[out[n][0] for n in WEIGHTS], *[out[n][1] for n in WEIGHTS],
            *[out[n][2] for n in WEIGHTS], *[out[n][3] for n in WEIGHTS])
```

```python
import functools
import math

import numpy as np
import jax
import jax.numpy as jnp
from jax import lax
from jax.experimental import pallas as pl
from jax.experimental.pallas import tpu as pltpu

F32 = jnp.float32
BF16 = jnp.bfloat16

N_DEV = 8
MESH_AXES = ("x", "y", "c")
S = 4096
D = 1024
DEPTH = 4
HEAD_DIM = 64
ATT_W = 1152
ATT_GW = 384
ATT_GROUPS = ((128, 1), (512, 4), (2048, 16))
ATT_BLOCK = 128
REL_BUCKETS = 32
REL_MAX_DISTANCE = 2048
POOL_WINDOWS = (2, 4, 8, 16)
SSD_HEADS = 16
SSD_CHUNK = 128
SSD_XBC = 1536
D_FF = 2816
IN_WIDTH = 10128
EPS = 1e-6
NEG = -1e30

OFF_GATE, OFF_POOL, OFF_Z, OFF_XBC, OFF_DT, OFF_Q, OFF_K, OFF_V = 0, 3072, 4096, 5120, 6656, 6912, 8064, 9216
NP = 10368
DT_PAD = 128

ADAM_LR, ADAM_B1, ADAM_B2, ADAM_EPS, ADAM_WD, ADAM_STEP = 0.001, 0.9, 0.999, 1e-08, 0.01, 10

VMEM_LIMIT = 52 * 1024 * 1024


def _cparams(sem=None):
    return pltpu.CompilerParams(dimension_semantics=sem, vmem_limit_bytes=VMEM_LIMIT)


def _dot(a, b, ca, cb):
    return lax.dot_general(a.astype(BF16), b.astype(BF16), (((ca,), (cb,)), ((), ())), preferred_element_type=F32)


@jax.custom_vjp
def _mm(a, b):
    return _dot(a, b, 1, 0)


def _mm_fwd(a, b):
    return _mm(a, b), (a, b)


def _mm_bwd(res, g):
    a, b = res
    return _dot(g, b, 1, 1).astype(a.dtype), _dot(a, g, 0, 0).astype(b.dtype)


_mm.defvjp(_mm_fwd, _mm_bwd)


@jax.custom_vjp
def _mm_nt(a, b):
    return _dot(a, b, 1, 1)


def _mm_nt_fwd(a, b):
    return _mm_nt(a, b), (a, b)


def _mm_nt_bwd(res, g):
    a, b = res
    return _dot(g, b, 1, 0).astype(a.dtype), _dot(g, a, 0, 0).astype(b.dtype)


_mm_nt.defvjp(_mm_nt_fwd, _mm_nt_bwd)


@jax.custom_vjp
def _mm_tn(a, b):
    return _dot(a, b, 0, 0)


def _mm_tn_fwd(a, b):
    return _mm_tn(a, b), (a, b)


def _mm_tn_bwd(res, g):
    a, b = res
    return _dot(b, g, 1, 1).astype(a.dtype), _dot(a, g, 1, 0).astype(b.dtype)


_mm_tn.defvjp(_mm_tn_fwd, _mm_tn_bwd)


def _shift_impl(x, j):
    n = x.shape[0]
    if j == 0:
        return x
    r = pltpu.roll(x, j % n, axis=0)
    t = lax.broadcasted_iota(jnp.int32, x.shape, 0)
    mask = (t >= j) if j > 0 else (t < n + j)
    return jnp.where(mask, r, 0.0)


@functools.partial(jax.custom_vjp, nondiff_argnums=(1,))
def _shift(x, j):
    return _shift_impl(x, j)


_shift.defvjp(lambda x, j: (_shift_impl(x, j), None), lambda j, _, g: (_shift_impl(g, -j),))


def _tri(lower):
    r = lax.broadcasted_iota(jnp.int32, (SSD_CHUNK, SSD_CHUNK), 0)
    c = lax.broadcasted_iota(jnp.int32, (SSD_CHUNK, SSD_CHUNK), 1)
    return (r >= c) if lower else (r <= c)


def _dot_hi(a, b):
    return lax.dot_general(a, b, (((1,), (0,)), ((), ())), precision=lax.Precision.HIGHEST,
                           preferred_element_type=F32)


@jax.custom_vjp
def _cumsum_rows(a):
    return _dot_hi(_tri(True).astype(F32), a)


_cumsum_rows.defvjp(lambda a: (_cumsum_rows(a), None), lambda _, g: (_dot_hi(_tri(False).astype(F32), g),))


@jax.custom_vjp
def _softplus(x):
    return jnp.maximum(x, 0.0) + jnp.log(1.0 + jnp.exp(-jnp.abs(x)))


_softplus.defvjp(lambda x: (_softplus(x), x), lambda x, g: (g * jax.nn.sigmoid(x),))


def _silu(x):
    return x * jax.nn.sigmoid(x)


def _rows(arr, cw, off=0, lead=None, roff=0):
    return (arr, cw, off, lead, roff)


def _tiled(name, fn, grid, tm, rows, params, outs, accs=(), out_roff=0, prev_outs=None):
    ncol, nrow = grid
    in_specs, operands = [], []
    for arr, cw, off, lead, roff in rows:
        if lead is None:
            in_specs.append(pl.BlockSpec((tm, cw), functools.partial(lambda j, i, off, roff: (roff + i, off + j),
                                                                     off=off, roff=roff)))
        else:
            in_specs.append(pl.BlockSpec((None, tm, cw), functools.partial(
                lambda j, i, off, lead, roff: (lead, roff + i, off + j), off=off, lead=lead, roff=roff)))
        operands.append(arr)
    for arr, bs, im in params:
        in_specs.append(pl.BlockSpec(bs, im))
        operands.append(arr)
    out_specs, out_shape = [], []
    for n_rows, cw, dt in outs:
        out_specs.append(pl.BlockSpec((tm, cw), functools.partial(lambda j, i, r: (r + i, j), r=out_roff)))
        out_shape.append(jax.ShapeDtypeStruct((n_rows, ncol * cw), dt))
    for shape, bs, im in accs:
        out_specs.append(pl.BlockSpec(bs, im))
        out_shape.append(jax.ShapeDtypeStruct(shape, F32))
    n_in, n_out = len(operands), len(outs)
    aliases = {}
    if prev_outs is not None:
        for k, p in enumerate(prev_outs):
            aliases[len(operands)] = k
            in_specs.append(pl.BlockSpec(memory_space=pl.ANY))
            operands.append(p)

    n_all = len(operands)

    def body(*refs):
        vals = [r[...] for r in refs[:n_in]]
        o_vals, a_vals = fn(*vals)
        for r, v in zip(refs[n_all:n_all + n_out], o_vals):
            r[...] = v.astype(r.dtype)
        i = pl.program_id(1)
        for r, v in zip(refs[n_all + n_out:], a_vals):
            @pl.when(i == 0)
            def _(r=r, v=v):
                r[...] = v.astype(r.dtype)

            @pl.when(i > 0)
            def _(r=r, v=v):
                r[...] += v.astype(r.dtype)

    res = pl.pallas_call(body, grid=grid, in_specs=in_specs, out_specs=out_specs, out_shape=out_shape, name=name,
                         input_output_aliases=aliases, compiler_params=_cparams(("arbitrary", "arbitrary")))(*operands)
    return list(res)


def _with_vjp(fn, n_prim, want_out, want_acc):
    def f(*args):
        prim, g = args[:n_prim], args[n_prim:]
        outs, vjp = jax.vjp(lambda *a: fn(*a), *prim)
        d = vjp(tuple(gi.astype(o.dtype) for gi, o in zip(g, outs)))
        return tuple(d[k] for k in want_out), tuple(d[k] for k in want_acc)
    return f


def _p_row(arr, cw, off=0):
    return (arr, (1, cw), functools.partial(lambda j, i, off: (0, off + j), off=off))


def _a_row(n, cw):
    return ((1, n), (1, cw), lambda j, i: (0, j))


def _pick(n, cap, mult):
    best = None
    for t in range(mult, min(n, cap) + 1, mult):
        if n % t == 0:
            best = t
    return best if best is not None else n


def _matmul(name, a, b, mode, add=None, out_dtype=F32):
    if mode == "nn":
        (M, K), N = a.shape, b.shape[1]
    elif mode == "nt":
        (M, K), N = a.shape, b.shape[0]
    else:
        (K, M), N = a.shape, b.shape[1]
    tm = _pick(M, 512, 128 if mode == "tn" else 8)
    tn = _pick(N, 1536, 128)
    k_cap = 2048 if mode == "tn" else 3456
    tk = K if K <= k_cap else _pick(K, k_cap, 128)
    nk = K // tk
    if mode == "nn":
        a_spec = pl.BlockSpec((tm, tk), lambda i, j, k: (i, k))
        b_spec = pl.BlockSpec((tk, tn), lambda i, j, k: (k, j))
        ca, cb = 1, 0
    elif mode == "nt":
        a_spec = pl.BlockSpec((tm, tk), lambda i, j, k: (i, k))
        b_spec = pl.BlockSpec((tn, tk), lambda i, j, k: (j, k))
        ca, cb = 1, 1
    else:
        a_spec = pl.BlockSpec((tk, tm), lambda i, j, k: (k, i))
        b_spec = pl.BlockSpec((tk, tn), lambda i, j, k: (k, j))
        ca, cb = 0, 0
    in_specs, operands = [a_spec, b_spec], [a, b]
    if add is not None:
        in_specs.append(pl.BlockSpec((tm, tn), lambda i, j, k: (i, j)))
        operands.append(add)

    def finish(r, refs, o_ref):
        if add is not None:
            r = r + refs[2][...]
        o_ref[...] = r.astype(o_ref.dtype)

    def body_single(*refs):
        finish(_dot(refs[0][...], refs[1][...], ca, cb), refs, refs[-1])

    def body_multi(*refs):
        o_ref, acc_ref = refs[-2], refs[-1]
        k = pl.program_id(2)
        d = _dot(refs[0][...], refs[1][...], ca, cb)

        @pl.when(k == 0)
        def _():
            acc_ref[...] = d

        @pl.when(jnp.logical_and(k > 0, k < nk - 1))
        def _():
            acc_ref[...] += d

        @pl.when(k == nk - 1)
        def _():
            finish(acc_ref[...] + d, refs, o_ref)

    return pl.pallas_call(
        body_single if nk == 1 else body_multi, grid=(M // tm, N // tn, nk), in_specs=in_specs,
        out_specs=pl.BlockSpec((tm, tn), lambda i, j, k: (i, j)),
        out_shape=jax.ShapeDtypeStruct((M, N), out_dtype),
        scratch_shapes=[] if nk == 1 else [pltpu.VMEM((tm, tn), F32)], name=name,
        compiler_params=_cparams(("parallel", "parallel", "arbitrary")))(*operands)


def _seg_copies(segs, c):
    out = []
    for lo, hi, dst in segs:
        n = lo
        while n < hi:
            p = n // c
            w = min(hi, (p + 1) * c) - n
            out.append((p, n - p * c, w, dst + n - lo))
            n += w
    return out


def _col_assemble(name, blocks, copies, zeros, n_out):
    _, R, c = blocks.shape
    tm = R if R <= 128 else 128

    def body(b_ref, o_ref):
        for p, s, w, d in copies:
            o_ref[:, d:d + w] = b_ref[p, :, s:s + w]
        for lo, hi in zeros:
            o_ref[:, lo:hi] = jnp.zeros((tm, hi - lo), o_ref.dtype)

    return pl.pallas_call(
        body, grid=(R // tm,), in_specs=[pl.BlockSpec((N_DEV, tm, c), lambda i: (0, i, 0))],
        out_specs=pl.BlockSpec((tm, n_out), lambda i: (i, 0)),
        out_shape=jax.ShapeDtypeStruct((R, n_out), blocks.dtype), name=name, compiler_params=_cparams(("parallel",)))(blocks)


def _col_split(name, full, copies, c, dtype):
    R, n = full.shape
    tm = R if R <= 128 else 128

    def body(f_ref, o_ref):
        for p, s, w, d in copies:
            o_ref[p, :, s:s + w] = f_ref[:, d:d + w].astype(dtype)

    return pl.pallas_call(
        body, grid=(R // tm,), in_specs=[pl.BlockSpec((tm, n), lambda i: (i, 0))],
        out_specs=pl.BlockSpec((N_DEV, tm, c), lambda i: (0, i, 0)),
        out_shape=jax.ShapeDtypeStruct((N_DEV, R, c), dtype), name=name, compiler_params=_cparams(("parallel",)))(full)


def _rmsnorm_fn(x, g):
    x = x.astype(F32)
    return (x * lax.rsqrt(jnp.mean(x * x, axis=-1, keepdims=True) + EPS) * g,)


def _gate_merge_fn(g0, g1, g2, ya, yb, yc, b0, b1, b2):
    return (jax.nn.sigmoid(g0 + b0) * ya + jax.nn.sigmoid(g1 + b1) * yb + jax.nn.sigmoid(g2 + b2) * yc,)


def _gated_norm_fn(y, z, w):
    t = y * _silu(z)
    return (t * lax.rsqrt(jnp.mean(t * t, axis=-1, keepdims=True) + EPS) * w,)


def _att_merge_fn(o0, o1, o2, l0, l1, l2):
    m = lax.stop_gradient(jnp.maximum(jnp.maximum(l0, l1), l2))
    e0, e1, e2 = jnp.exp(l0 - m), jnp.exp(l1 - m), jnp.exp(l2 - m)
    return ((e0 * o0 + e1 * o1 + e2 * o2) / (e0 + e1 + e2),)


def _loss_fn(x, tgt, g):
    (y,) = _rmsnorm_fn(x, g)
    err = y - tgt
    return 0.5 * jnp.sum(jnp.mean(err * err, axis=-1, keepdims=True), axis=0, keepdims=True)


def _pool_fn(x, wg, scale):
    g = pl.program_id(0)
    s2 = x + _shift(x, 1)
    s4 = s2 + _shift(s2, 2)
    s8 = s4 + _shift(s4, 4)
    s16 = s8 + _shift(s8, 8)
    win = ((g == 0).astype(F32) * s2 + (g == 1).astype(F32) * s4 + (g == 2).astype(F32) * s8
           + (g == 3).astype(F32) * s16)
    t = lax.broadcasted_iota(jnp.int32, (x.shape[0], 1), 0) + 1
    cnt = jnp.minimum(t, jnp.left_shift(2, g)).astype(F32)
    d = win / cnt - x
    return (_mm(d, wg.reshape(256, 256)) * scale,)


def _dwconv(x, taps, b):
    k = len(taps)
    y = taps[k - 1] * x + b
    for i in range(k - 1):
        y = y + taps[i] * _shift(x, k - 1 - i)
    return y


def _ssd_conv_fn(x, w0, w1, w2, w3, b):
    return (_silu(_dwconv(x, (w0, w1, w2, w3), b)),)


def _ffn_act_fn(xa, xv, a0, a1, a2, ab, v0, v1, v2, vb):
    return (_silu(_dwconv(xa, (a0, a1, a2), ab)) * _dwconv(xv, (v0, v1, v2), vb),)


def _att_block(q, kp, kc, vp, vc, bpa, bpb, bca, bcb, prev_ok):
    qi = lax.broadcasted_iota(jnp.int32, (ATT_BLOCK, ATT_BLOCK), 0)
    kj = lax.broadcasted_iota(jnp.int32, (ATT_BLOCK, ATT_BLOCK), 1)
    lane = lax.broadcasted_iota(jnp.int32, (1, 2 * HEAD_DIM), 1)
    mask_c = kj <= qi
    mask_p = jnp.logical_and(kj >= qi, prev_ok)
    q = q.astype(F32)
    o = None
    lse = None
    for hh, (bp, bc) in enumerate(((bpa, bca), (bpb, bcb))):
        hm = (lane // HEAD_DIM == hh).astype(F32)
        qh = q * (hm * (1.0 / math.sqrt(HEAD_DIM)))
        sp = jnp.where(mask_p, _mm_nt(qh, kp) + bp, NEG)
        sc = jnp.where(mask_c, _mm_nt(qh, kc) + bc, NEG)
        m = lax.stop_gradient(jnp.maximum(jnp.max(sp, axis=1, keepdims=True), jnp.max(sc, axis=1, keepdims=True)))
        pp = jnp.exp(sp - m)
        pc = jnp.exp(sc - m)
        l = jnp.sum(pp, axis=1, keepdims=True) + jnp.sum(pc, axis=1, keepdims=True)
        oh = (_mm(pp, vp) + _mm(pc, vc)) / l * hm
        lh = (m + jnp.log(l)) * hm
        o = oh if o is None else o + oh
        lse = lh if lse is None else lse + lh
    return o, lse


def _att_specs(nb, ncolblk, clamp):
    def cur(p, r, j):
        return (jnp.minimum(j, nb - 1) if clamp else j, r * 3 + p)

    def prev(p, r, j):
        jj = jnp.minimum(j, nb - 1) if clamp else j
        return (jnp.maximum(jj - 1, 0), r * 3 + p)

    def done(p, r, j):
        return (jnp.maximum(j - 1, 0), r * 3 + p)

    blk = (ATT_BLOCK, 2 * HEAD_DIM)
    return pl.BlockSpec(blk, cur), pl.BlockSpec(blk, prev), pl.BlockSpec(blk, done)


def _bias_specs():
    return [pl.BlockSpec((None, ATT_BLOCK, ATT_BLOCK), functools.partial(lambda p, r, j, hh: (2 * p + hh, 0, 0), hh=hh))
            for hh in (0, 1)]


def _bias_grad_spec():
    return pl.BlockSpec((None, ATT_BLOCK, ATT_BLOCK), lambda p, r, j: (p, 0, 0))


def _att_fwd(name, q, k, v, bias_p, bias_c, dil):
    L = S // dil
    nb = L // ATT_BLOCK
    cur, prev, _ = _att_specs(nb, dil * 3, False)
    bsp = _bias_specs()

    def body(q_ref, kp_ref, kc_ref, vp_ref, vc_ref, bpa, bpb, bca, bcb, o_ref, l_ref):
        prev_ok = pl.program_id(2) > 0
        o, lse = _att_block(q_ref[...], kp_ref[...], kc_ref[...], vp_ref[...], vc_ref[...],
                            bpa[...], bpb[...], bca[...], bcb[...], prev_ok)
        o_ref[...] = o
        l_ref[...] = lse

    shp = jax.ShapeDtypeStruct((L, dil * ATT_GW), F32)
    return pl.pallas_call(
        body, grid=(3, dil, nb), in_specs=[cur, prev, cur, prev, cur, bsp[0], bsp[1], bsp[0], bsp[1]],
        out_specs=[cur, cur], out_shape=[shp, shp], name=name,
        compiler_params=_cparams(("arbitrary",) * 3))(q, k, k, v, v, bias_p, bias_p, bias_c, bias_c)


def _att_bwd(name, q, k, v, bias_p, bias_c, do, dl, dil):
    L = S // dil
    nb = L // ATT_BLOCK
    cur, prev, done = _att_specs(nb, dil * 3, True)
    bsp = _bias_specs()
    gsp = _bias_grad_spec()

    def body(q_ref, kp_ref, kc_ref, vp_ref, vc_ref, bpa, bpb, bca, bcb, do_ref, dl_ref,
             dq_ref, dk_ref, dv_ref, gpa, gpb, gca, gcb, ck, cv):
        r, j = pl.program_id(1), pl.program_id(2)

        @pl.when(jnp.logical_and(r == 0, j == 0))
        def _():
            for g in (gpa, gpb, gca, gcb):
                g[...] = jnp.zeros_like(g)

        @pl.when(j == 0)
        def _():
            ck[...] = jnp.zeros_like(ck)
            cv[...] = jnp.zeros_like(cv)

        @pl.when(j < nb)
        def _():
            prev_ok = j > 0
            prim = (q_ref[...], kp_ref[...], kc_ref[...], vp_ref[...], vc_ref[...],
                    bpa[...], bpb[...], bca[...], bcb[...])
            _, vjp = jax.vjp(lambda *a: _att_block(*a, prev_ok), *prim)
            dq, dkp, dkc, dvp, dvc, dpa, dpb, dca, dcb = vjp((do_ref[...], dl_ref[...]))
            dq_ref[...] = dq.astype(F32)
            gpa[...] += dpa
            gpb[...] += dpb
            gca[...] += dca
            gcb[...] += dcb
            dk_ref[...] = ck[...] + dkp.astype(F32)
            dv_ref[...] = cv[...] + dvp.astype(F32)
            ck[...] = dkc.astype(F32)
            cv[...] = dvc.astype(F32)

        @pl.when(j == nb)
        def _():
            dk_ref[...] = ck[...]
            dv_ref[...] = cv[...]

    shp = jax.ShapeDtypeStruct((L, dil * ATT_GW), F32)
    gshp = jax.ShapeDtypeStruct((3, ATT_BLOCK, ATT_BLOCK), F32)
    res = pl.pallas_call(
        body, grid=(3, dil, nb + 1),
        in_specs=[cur, prev, cur, prev, cur, bsp[0], bsp[1], bsp[0], bsp[1], cur, cur],
        out_specs=[cur, done, done, gsp, gsp, gsp, gsp],
        out_shape=[shp, shp, shp, gshp, gshp, gshp, gshp],
        scratch_shapes=[pltpu.VMEM((ATT_BLOCK, 2 * HEAD_DIM), F32)] * 2, name=name,
        compiler_params=_cparams(("arbitrary",) * 3))(q, k, k, v, v, bias_p, bias_p, bias_c, bias_c, do, dl)
    dq, dk, dv, gpa, gpb, gca, gcb = res
    heads = lambda a, b: jnp.stack([a, b], axis=1).reshape(6, ATT_BLOCK, ATT_BLOCK)
    return dq, dk, dv, heads(gpa, gpb), heads(gca, gcb)


N_PAIR = SSD_HEADS // 2


def _ssd_chunk(xs, bs, cs_in, dt_raw, hs, a_row, dtb_row, ds):
    lane = lax.broadcasted_iota(jnp.int32, (1, 128), 1)
    row = lax.broadcasted_iota(jnp.int32, (128, 1), 0)
    tril = _tri(True)
    dt = _softplus(dt_raw + dtb_row)
    acs = _cumsum_rows(dt * a_row)
    acs_t = acs.T
    gmat = [_mm_nt(cs_in[g], bs[g]) for g in range(2)]
    lo = lane < HEAD_DIM
    lo_r = row < HEAD_DIM
    last = (row == SSD_CHUNK - 1).astype(F32)
    ys, hn = [], []
    for p in range(N_PAIR):
        g = p // (N_PAIR // 2)
        col, dtc, mm, clast = [], [], [], []
        for hh in range(2):
            h = 2 * p + hh
            oh = (lane == h).astype(F32)
            c_col = jnp.sum(acs * oh, axis=1, keepdims=True)
            c_row = jnp.sum(acs_t * (row == h).astype(F32), axis=0, keepdims=True)
            col.append(c_col)
            dtc.append(jnp.sum(dt * oh, axis=1, keepdims=True))
            clast.append(jnp.sum(c_col * last, axis=0, keepdims=True))
            mm.append(gmat[g] * jnp.exp(jnp.where(tril, c_col - c_row, NEG)))
        x = xs[p]
        xd = x * jnp.where(lo, dtc[0], dtc[1])
        y = jnp.where(lo, _mm(mm[0], xd), _mm(mm[1], xd))
        y = y + jnp.where(lo, jnp.exp(col[0]), jnp.exp(col[1])) * _mm_nt(cs_in[g], hs[p])
        ys.append(y + ds[p] * x)
        dec = jnp.where(lo, jnp.exp(clast[0] - col[0]), jnp.exp(clast[1] - col[1]))
        hn.append(hs[p] * jnp.where(lo_r, jnp.exp(clast[0]), jnp.exp(clast[1])) + _mm_tn(xd * dec, bs[g]))
    return tuple(ys), tuple(hn)


def _ssd_load(xbc_ref, dt_ref, a_ref, dtb_ref, d_ref):
    xs = tuple(xbc_ref[:, 128 * p:128 * (p + 1)] for p in range(N_PAIR))
    bs = tuple(xbc_ref[:, D + 128 * g:D + 128 * (g + 1)] for g in range(2))
    cs = tuple(xbc_ref[:, D + 256 + 128 * g:D + 256 + 128 * (g + 1)] for g in range(2))
    ds = tuple(d_ref[:, 128 * p:128 * (p + 1)] for p in range(N_PAIR))
    return xs, bs, cs, dt_ref[...], a_ref[...], dtb_ref[...], ds


def _ssd_fwd(name, xbc_c, proj, a_row, dtb_row, d_exp):
    nc = S // SSD_CHUNK
    prow = lambda n: pl.BlockSpec((1, n), lambda c: (0, 0))

    def body(xbc_ref, dt_ref, a_ref, dtb_ref, d_ref, y_ref, st_ref, h_ref):
        @pl.when(pl.program_id(0) == 0)
        def _():
            h_ref[...] = jnp.zeros_like(h_ref)

        xs, bs, cs, dt_raw, a, dtb, ds = _ssd_load(xbc_ref, dt_ref, a_ref, dtb_ref, d_ref)
        hs = tuple(h_ref[p] for p in range(N_PAIR))
        ys, hn = _ssd_chunk(xs, bs, cs, dt_raw, hs, a, dtb, ds)
        for p in range(N_PAIR):
            y_ref[:, 128 * p:128 * (p + 1)] = ys[p]
            st_ref[p] = hs[p]
            h_ref[p] = hn[p]

    return pl.pallas_call(
        body, grid=(nc,),
        in_specs=[pl.BlockSpec((SSD_CHUNK, SSD_XBC), lambda c: (c, 0)),
                  pl.BlockSpec((SSD_CHUNK, DT_PAD), lambda c: (c, OFF_DT // DT_PAD)),
                  prow(128), prow(128), prow(D)],
        out_specs=[pl.BlockSpec((SSD_CHUNK, D), lambda c: (c, 0)),
                   pl.BlockSpec((None, N_PAIR, 128, 128), lambda c: (c, 0, 0, 0))],
        out_shape=[jax.ShapeDtypeStruct((S, D), F32), jax.ShapeDtypeStruct((nc, N_PAIR, 128, 128), F32)],
        scratch_shapes=[pltpu.VMEM((N_PAIR, 128, 128), F32)], name=name,
        compiler_params=_cparams(("arbitrary",)))(xbc_c, proj, a_row, dtb_row, d_exp)


def _ssd_bwd(name, xbc_c, proj, states, dy, a_row, dtb_row, d_exp):
    nc = S // SSD_CHUNK
    prow = lambda n: pl.BlockSpec((1, n), lambda i: (0, 0))
    rc = lambda i: nc - 1 - i

    def body(xbc_ref, dt_ref, st_ref, dy_ref, a_ref, dtb_ref, d_ref, dxbc_ref, ddt_ref, da_ref, ddtb_ref, dd_ref, e_ref):
        i = pl.program_id(0)

        @pl.when(i == 0)
        def _():
            e_ref[...] = jnp.zeros_like(e_ref)
            da_ref[...] = jnp.zeros_like(da_ref)
            ddtb_ref[...] = jnp.zeros_like(ddtb_ref)
            dd_ref[...] = jnp.zeros_like(dd_ref)

        xs, bs, cs, dt_raw, a, dtb, ds = _ssd_load(xbc_ref, dt_ref, a_ref, dtb_ref, d_ref)
        hs = tuple(st_ref[p] for p in range(N_PAIR))
        _, vjp = jax.vjp(_ssd_chunk, xs, bs, cs, dt_raw, hs, a, dtb, ds)
        dys = tuple(dy_ref[:, 128 * p:128 * (p + 1)] for p in range(N_PAIR))
        es = tuple(e_ref[p] for p in range(N_PAIR))
        dxs, dbs, dcs, ddt, dhs, da, ddtb, dds = vjp((dys, es))
        for p in range(N_PAIR):
            dxbc_ref[:, 128 * p:128 * (p + 1)] = dxs[p]
            e_ref[p] = dhs[p]
            dd_ref[:, 128 * p:128 * (p + 1)] += dds[p]
        for g in range(2):
            dxbc_ref[:, D + 128 * g:D + 128 * (g + 1)] = dbs[g]
            dxbc_ref[:, D + 256 + 128 * g:D + 256 + 128 * (g + 1)] = dcs[g]
        ddt_ref[...] = ddt
        da_ref[...] += da
        ddtb_ref[...] += ddtb

    return pl.pallas_call(
        body, grid=(nc,),
        in_specs=[pl.BlockSpec((SSD_CHUNK, SSD_XBC), lambda i: (rc(i), 0)),
                  pl.BlockSpec((SSD_CHUNK, DT_PAD), lambda i: (rc(i), OFF_DT // DT_PAD)),
                  pl.BlockSpec((None, N_PAIR, 128, 128), lambda i: (rc(i), 0, 0, 0)),
                  pl.BlockSpec((SSD_CHUNK, D), lambda i: (rc(i), 0)),
                  prow(128), prow(128), prow(D)],
        out_specs=[pl.BlockSpec((SSD_CHUNK, SSD_XBC), lambda i: (rc(i), 0)),
                   pl.BlockSpec((SSD_CHUNK, DT_PAD), lambda i: (rc(i), 0)),
                   prow(128), prow(128), prow(D)],
        out_shape=[jax.ShapeDtypeStruct((S, SSD_XBC), F32), jax.ShapeDtypeStruct((S, DT_PAD), F32),
                   jax.ShapeDtypeStruct((1, 128), F32), jax.ShapeDtypeStruct((1, 128), F32),
                   jax.ShapeDtypeStruct((1, D), F32)],
        scratch_shapes=[pltpu.VMEM((N_PAIR, 128, 128), F32)], name=name,
        compiler_params=_cparams(("arbitrary",)))(xbc_c, proj, states, dy, a_row, dtb_row, d_exp)


def _exchange(name, arrays, scatter):
    n = len(arrays)
    flips = [(dx, dy, dc) for dx in (0, 1) for dy in (0, 1) for dc in (0, 1) if dx or dy or dc]

    def body(*refs):
        ins, outs = refs[:n], refs[n:2 * n]
        send_sems, recv_sems, loc_sems = refs[2 * n:]
        x, y, c = lax.axis_index("x"), lax.axis_index("y"), lax.axis_index("c")
        me = 4 * x + 2 * y + c
        peers = []
        for dx, dy, dc in flips:
            px, py, pc = (1 - x if dx else x), (1 - y if dy else y), (1 - c if dc else c)
            peers.append(((px, py, pc), 4 * px + 2 * py + pc))

        def remote(k, j, landed_from):
            dev, pid = peers[j]
            src = ins[k].at[pid] if scatter else ins[k]
            return pltpu.make_async_remote_copy(
                src_ref=src, dst_ref=outs[k].at[landed_from], send_sem=send_sems.at[k, j], recv_sem=recv_sems.at[k, j],
                device_id=dev, device_id_type=pl.DeviceIdType.MESH)

        local = [pltpu.make_async_copy(ins[k].at[me] if scatter else ins[k], outs[k].at[me], loc_sems.at[k])
                 for k in range(n)]
        for cp in local:
            cp.start()
        for k in range(n):
            for j in range(len(flips)):
                remote(k, j, me).start()
        for cp in local:
            cp.wait()
        for k in range(n):
            for j in range(len(flips)):
                remote(k, j, me).wait_send()
                remote(k, j, peers[j][1]).wait_recv()

    hbm = pl.BlockSpec(memory_space=pltpu.HBM)
    out_shape = [jax.ShapeDtypeStruct(a.shape if scatter else (N_DEV,) + a.shape, a.dtype) for a in arrays]
    res = pl.pallas_call(
        body, in_specs=[hbm] * n, out_specs=[hbm] * n, out_shape=out_shape, name=name,
        scratch_shapes=[pltpu.SemaphoreType.DMA((n, len(flips))), pltpu.SemaphoreType.DMA((n, len(flips))),
                        pltpu.SemaphoreType.DMA((n,))])(*arrays)
    return list(res)


def _peer_copies(ins, lands, send_sems, recv_sems, loc_sems, scatter):
    n = len(ins)
    flips = [(dx, dy, dc) for dx in (0, 1) for dy in (0, 1) for dc in (0, 1) if dx or dy or dc]
    x, y, c = lax.axis_index("x"), lax.axis_index("y"), lax.axis_index("c")
    me = 4 * x + 2 * y + c
    peers = []
    for dx, dy, dc in flips:
        px, py, pc = (1 - x if dx else x), (1 - y if dy else y), (1 - c if dc else c)
        peers.append(((px, py, pc), 4 * px + 2 * py + pc))

    def remote(k, j, slot):
        dev, pid = peers[j]
        return pltpu.make_async_remote_copy(
            src_ref=ins[k].at[pid] if scatter else ins[k], dst_ref=lands[k].at[slot],
            send_sem=send_sems.at[k * N_FLIP + j], recv_sem=recv_sems.at[k * N_FLIP + j],
            device_id=dev, device_id_type=pl.DeviceIdType.MESH)

    local = [pltpu.make_async_copy(ins[k].at[me] if scatter else ins[k], lands[k].at[me], loc_sems.at[k])
             for k in range(n)]
    pairs = [(k, j) for k in range(n) for j in range(len(flips))]
    sent = lambda k, j: remote(k, j, me)
    landed = lambda k, j: remote(k, j, peers[j][1])
    return local, pairs, sent, landed


_HBM = pl.BlockSpec(memory_space=pltpu.HBM)
_SEM = pl.BlockSpec(memory_space=pltpu.SEMAPHORE)
N_FLIP = N_DEV - 1


def _exchange_start(name, arrays, scatter, after):
    n = len(arrays)
    arrays = [pltpu.with_memory_space_constraint(a, pltpu.HBM) for a in arrays]
    lands = [pltpu.with_memory_space_constraint(
        lax.empty(a.shape if scatter else (N_DEV,) + a.shape, a.dtype), pltpu.HBM) for a in arrays]

    def body(*refs):
        ins, lnd = refs[:n], refs[n:2 * n]
        send_sems, recv_sems, loc_sems = refs[2 * n + 1:2 * n + 4]
        token = refs[-1]
        local, pairs, sent, _ = _peer_copies(ins, lnd, send_sems, recv_sems, loc_sems, scatter)
        for cp in local:
            cp.start()
        for k, j in pairs:
            sent(k, j).start()
        token[...] = jnp.zeros_like(token)

    res = pl.pallas_call(
        body, name=name,
        in_specs=[_HBM] * (2 * n) + [pl.BlockSpec(memory_space=pl.ANY)],
        out_specs=[_SEM, _SEM, _SEM] + [_HBM] * (2 * n) + [pl.BlockSpec(memory_space=pltpu.VMEM)],
        out_shape=[pltpu.SemaphoreType.DMA((n * N_FLIP,)), pltpu.SemaphoreType.DMA((n * N_FLIP,)), pltpu.SemaphoreType.DMA((n,))]
        + [pltpu.HBM(a.shape, a.dtype) for a in arrays] + [pltpu.HBM(a.shape, a.dtype) for a in lands]
        + [jax.ShapeDtypeStruct((8, 128), F32)],
        input_output_aliases={k: 3 + k for k in range(2 * n)},
        compiler_params=pltpu.CompilerParams(has_side_effects=pltpu.SideEffectType.DATAFLOW_SIDE_EFFECTING),
    )(*arrays, *lands, after)
    return (res[:3], res[3:3 + n], res[3 + n:3 + 2 * n], scatter), res[-1]


def _exchange_wait(name, state, after):
    sems, ins_thru, lands_thru, scatter = state
    n = len(ins_thru)

    def body(*refs):
        ins, lnd = refs[:n], refs[n:2 * n]
        send_sems, recv_sems, loc_sems = refs[2 * n:2 * n + 3]
        local, pairs, sent, landed = _peer_copies(ins, lnd, send_sems, recv_sems, loc_sems, scatter)
        for cp in local:
            cp.wait()
        for k, j in pairs:
            sent(k, j).wait_send()
            landed(k, j).wait_recv()

    res = pl.pallas_call(
        body, name=name,
        in_specs=[_HBM] * (2 * n) + [_SEM, _SEM, _SEM] + [pl.BlockSpec(memory_space=pl.ANY)],
        out_specs=[_HBM] * (2 * n),
        out_shape=[pltpu.HBM(a.shape, a.dtype) for a in ins_thru] + [pltpu.HBM(a.shape, a.dtype) for a in lands_thru],
        input_output_aliases={k: k for k in range(2 * n)},
        compiler_params=pltpu.CompilerParams(has_side_effects=pltpu.SideEffectType.DATAFLOW_SIDE_EFFECTING),
    )(*ins_thru, *lands_thru, *sems, after)
    return list(res[n:])


def _adamw_fn(*vals):
    slots, (w, m, v) = vals[:N_DEV], vals[N_DEV:]
    g = slots[0].astype(F32)
    for s in slots[1:]:
        g = g + s.astype(F32)
    m2 = ADAM_B1 * m + (1.0 - ADAM_B1) * g
    v2 = ADAM_B2 * v + (1.0 - ADAM_B2) * (g * g)
    m_hat = m2 / (1.0 - ADAM_B1 ** ADAM_STEP)
    v_hat = v2 / (1.0 - ADAM_B2 ** ADAM_STEP)
    delta = -ADAM_LR * (m_hat / (jnp.sqrt(v_hat) + ADAM_EPS) + ADAM_WD * w)
    return (g, delta, m2, v2), ()


def _adamw(name, slots, w, m, v, first_row=0, prev=None):
    R, C = slots.shape[1:]
    tm = R if R <= 128 else _pick(R, 128 if C > D else 256, 8)
    rows = ([_rows(slots, C, lead=s) for s in range(N_DEV)]
            + [_rows(a, C, roff=first_row // tm) for a in (w, m, v)])
    return _tiled(name, _adamw_fn, (1, R // tm), tm, rows, [], [(w.shape[0], C, F32)] * 4,
                  out_roff=first_row // tm, prev_outs=prev)


def _bucket_onehots():
    out = []
    qi = jnp.arange(ATT_BLOCK)[:, None]
    kj = jnp.arange(ATT_BLOCK)[None, :]
    max_exact = REL_BUCKETS // 2
    for _, dil in ATT_GROUPS:
        parts = []
        for rel in (qi + ATT_BLOCK - kj, qi - kj):
            dist = jnp.clip(rel, 0, None) * dil
            nf = jnp.maximum(dist, 1).astype(F32)
            large = max_exact + (jnp.log(nf / max_exact) / math.log(REL_MAX_DISTANCE / max_exact)
                                 * (REL_BUCKETS - max_exact)).astype(jnp.int32)
            large = jnp.minimum(large, REL_BUCKETS - 1)
            bucket = jnp.where(dist < max_exact, dist, large)
            parts.append((bucket[:, :, None] == jnp.arange(REL_BUCKETS)[None, None, :]).astype(F32))
        out.append(jnp.stack(parts))
    return out


SHARDED = ("w_in", "w_a", "pool_w", "w_b", "ssd_conv_w", "w_c", "w_o", "ffn_w_up", "ffn_conv_w", "ffn_w_down")
MATMUL_WEIGHTS = ("w_in", "w_a", "pool_w", "w_b", "w_c", "w_o", "ffn_w_up", "ffn_w_down")
ROW_SHARDED = ("w_b", "w_c", "w_o", "ffn_w_down")
W_IN_SEGS = ((0, 1152, OFF_Q), (1152, 2304, OFF_K), (2304, 3456, OFF_V), (3456, 4480, OFF_POOL), (4480, 5504, OFF_Z),
             (5504, 7040, OFF_XBC), (7040, 7056, OFF_DT), (7056, IN_WIDTH, OFF_GATE))
COL_SHARDED = {
    "w_in": (IN_WIDTH // N_DEV, W_IN_SEGS, ((OFF_DT + SSD_HEADS, OFF_Q),), NP),
    "w_a": (D // N_DEV, ((0, D, 0),), (), D),
    "ffn_w_up": (2 * D_FF // N_DEV, ((0, 2 * D_FF, 0),), (), 2 * D_FF),
    "ssd_conv_w": (SSD_XBC // N_DEV, ((0, SSD_XBC, 0),), (), SSD_XBC),
    "ffn_conv_w": (2 * D_FF // N_DEV, ((0, 2 * D_FF, 0),), (), 2 * D_FF),
}
REPLICATED = ("rel_bias", "ln1_g", "b_gate", "pool_scale", "ssd_conv_b", "ssd_dt_bias", "ssd_a_log", "ssd_d",
              "ssd_norm_w", "ln2_g", "ffn_conv_b", "final_g")
WEIGHTS = ("rel_bias", "ln1_g", "w_in", "b_gate", "w_a", "pool_w", "pool_scale", "w_b", "ssd_conv_w", "ssd_conv_b",
           "ssd_dt_bias", "ssd_a_log", "ssd_d", "ssd_norm_w", "w_c", "w_o", "ln2_g", "ffn_w_up", "ffn_conv_w",
           "ffn_conv_b", "ffn_w_down", "final_g")


def _local_weight(name, n, blocks):
    if n in COL_SHARDED:
        c, segs, zeros, width = COL_SHARDED[n]
        return _col_assemble(name, blocks, _seg_copies(segs, c), zeros, width)
    if n in ROW_SHARDED:
        return blocks.reshape(-1, blocks.shape[-1])
    return blocks


def _device_blocks(name, n, g):
    if n in COL_SHARDED:
        c, segs, _, _ = COL_SHARDED[n]
        return _col_split(name, g, _seg_copies(segs, c), c, BF16)
    if n in ROW_SHARDED:
        return g.reshape(N_DEV, g.shape[0] // N_DEV, g.shape[1]).astype(BF16)
    return g.astype(BF16)


def _to_residue(t, dil):
    return t.reshape(S // dil, dil * ATT_GW)


def _row(v, n=None):
    v = v.reshape(1, -1)
    if n is not None and v.shape[1] < n:
        v = jnp.pad(v, ((0, 0), (0, n - v.shape[1])))
    return v


RT = 256


def _row_call(name, fn, cw, ncol, rows, params, outs, accs=()):
    return _tiled(name, fn, (ncol, S // RT), RT, rows, params, [(S, cw, dt) for dt in outs], accs)


def _col_call(name, fn, tc, ncol, rows, params, outs, accs=()):
    return _tiled(name, fn, (ncol, 1), S, rows, params, [(S, tc, dt) for dt in outs], accs)


def _fwd_only(fn):
    return lambda *a: (fn(*a), ())


def _layer_fwd(i, x, W, P, bias_tabs, late=None):
    sv = {"x": x}
    (u,) = _row_call(f"ln1_f{i}", _fwd_only(_rmsnorm_fn), D, 1, [_rows(x, D)], [_p_row(P["ln1_g"], D)], [BF16])
    proj = _matmul(f"inproj_f{i}", u, W["w_in"], "nn")
    sv["u"], sv["proj"] = u, proj

    os_, ls_, res = [], [], []
    for gi, (_, dil) in enumerate(ATT_GROUPS):
        q, k, v = (_to_residue(proj[:, off + gi * ATT_GW: off + (gi + 1) * ATT_GW].astype(BF16), dil)
                   for off in (OFF_Q, OFF_K, OFF_V))
        o, lse = _att_fwd(f"att_f{i}_{gi}", q, k, v, bias_tabs[gi][0], bias_tabs[gi][1], dil)
        res.append((q, k, v))
        os_.append(o.reshape(S, ATT_GW))
        ls_.append(lse.reshape(S, ATT_GW))
    sv["att_res"], sv["att_o"], sv["att_l"] = res, os_, ls_
    (att,) = _row_call(f"attmerge_f{i}", _fwd_only(_att_merge_fn), 128, 3,
                       [_rows(t, 128) for t in os_ + ls_], [], [BF16])
    if late is not None:
        W2, P2 = late(att)
        W.update(W2)
        P.update(P2)
    y_a = _matmul(f"wa_f{i}", att, W["w_a"], "nn")
    sv["att"], sv["y_a"] = att, y_a

    pool_params = [(W["pool_w"], (N_DEV, None, 32, 256), lambda j, i_: (0, j, 0, 0)), _p_row(P["pool_scale"], 256)]
    (yb_pre,) = _col_call(f"pool_f{i}", _fwd_only(_pool_fn), 256, 4, [_rows(proj, 256, OFF_POOL // 256)],
                          pool_params, [BF16])
    y_b = _matmul(f"wb_f{i}", yb_pre, W["w_b"], "nn")
    sv["yb_pre"], sv["y_b"] = yb_pre, y_b

    conv_params = [_p_row(P["ssd_conv_w"][k], 128) for k in range(4)] + [_p_row(P["ssd_conv_b"], 128)]
    (xbc_c,) = _col_call(f"ssdconv_f{i}", _fwd_only(_ssd_conv_fn), 128, SSD_XBC // 128,
                         [_rows(proj, 128, OFF_XBC // 128)], conv_params, [F32])
    y_ssd, states = _ssd_fwd(f"ssd_f{i}", xbc_c, proj, P["a_row"], P["dtb_row"], P["d_exp"])
    (yc_pre,) = _row_call(f"ssdnorm_f{i}", _fwd_only(_gated_norm_fn), 512, 2,
                          [_rows(y_ssd, 512), _rows(proj, 512, OFF_Z // 512)], [_p_row(P["ssd_norm_w"], 512)], [BF16])
    y_c = _matmul(f"wc_f{i}", yc_pre, W["w_c"], "nn")
    sv["xbc_c"], sv["states"], sv["y_ssd"], sv["yc_pre"], sv["y_c"] = xbc_c, states, y_ssd, yc_pre, y_c

    gate_rows = [_rows(proj, D, k) for k in range(3)] + [_rows(t, D) for t in (y_a, y_b, y_c)]
    gate_params = [_p_row(P["b_gate"], D, k) for k in range(3)]
    (merged,) = _row_call(f"gate_f{i}", _fwd_only(_gate_merge_fn), D, 1, gate_rows, gate_params, [BF16])
    x1 = _matmul(f"wo_f{i}", merged, W["w_o"], "nn", add=x)
    sv["merged"], sv["x1"] = merged, x1

    (u2,) = _row_call(f"ln2_f{i}", _fwd_only(_rmsnorm_fn), D, 1, [_rows(x1, D)], [_p_row(P["ln2_g"], D)], [BF16])
    up = _matmul(f"up_f{i}", u2, W["ffn_w_up"], "nn")
    (act,) = _col_call(f"ffnact_f{i}", _fwd_only(_ffn_act_fn), 128, D_FF // 128,
                       [_rows(up, 128), _rows(up, 128, D_FF // 128)], _ffn_params(P), [BF16])
    x2 = _matmul(f"down_f{i}", act, W["ffn_w_down"], "nn", add=x1)
    sv["u2"], sv["up"], sv["act"] = u2, up, act
    return x2, sv


def _ffn_params(P):
    nb = D_FF // 128
    return ([_p_row(P["ffn_conv_w"][k], 128) for k in range(3)] + [_p_row(P["ffn_conv_b"], 128)]
            + [_p_row(P["ffn_conv_w"][k], 128, nb) for k in range(3)] + [_p_row(P["ffn_conv_b"], 128, nb)])


def _layer_bwd(i, dx2, sv, W, P, bias_tabs, onehots, on_sharded_grads):
    G = {}
    x, proj, x1 = sv["x"], sv["proj"], sv["x1"]

    dact = _matmul(f"down_bx{i}", dx2, W["ffn_w_down"], "nt", out_dtype=BF16)
    G["ffn_w_down"] = _matmul(f"down_bw{i}", sv["act"], dx2, "tn", out_dtype=BF16)
    nb = D_FF // 128
    up = sv["up"]
    f = _with_vjp(_ffn_act_fn, 10, (0, 1), tuple(range(2, 10)))
    accs = [_a_row(D_FF, 128)] * 8
    dua, duv, a0, a1, a2, ab, v0, v1, v2, vb = _col_call(
        f"ffnact_b{i}", f, 128, nb, [_rows(up, 128), _rows(up, 128, nb)], _ffn_params(P) + [_rows_as_param(dact, 128)],
        [BF16, BF16], accs)
    G["ffn_conv_w"] = jnp.concatenate([jnp.concatenate([a0, a1, a2], 0), jnp.concatenate([v0, v1, v2], 0)], axis=1)
    G["ffn_conv_b"] = jnp.concatenate([ab, vb], axis=1)[0]
    dup = jnp.concatenate([dua, duv], axis=1)
    du2 = _matmul(f"up_bx{i}", dup, W["ffn_w_up"], "nt")
    G["ffn_w_up"] = _matmul(f"up_bw{i}", sv["u2"], dup, "tn", out_dtype=BF16)

    def norm_bwd(x_, g_, du_, dres):
        (dxn,), (dg,) = _with_vjp(_rmsnorm_fn, 2, (0,), (1,))(x_, g_, du_)
        return (dxn + dres,), (dg,)

    (dx1,), (G["ln2_g"],) = _split_res(_row_call(
        f"ln2_b{i}", lambda x_, du_, dres, g_: norm_bwd(x_, g_, du_, dres), D, 1,
        [_rows(x1, D), _rows(du2, D), _rows(dx2, D)], [_p_row(P["ln2_g"], D)], [F32], [_a_row(D, D)]), 1)

    dmerged = _matmul(f"wo_bx{i}", dx1, W["w_o"], "nt")
    G["w_o"] = _matmul(f"wo_bw{i}", sv["merged"], dx1, "tn", out_dtype=BF16)
    gate_rows = [_rows(proj, D, k) for k in range(3)] + [_rows(sv[t], D) for t in ("y_a", "y_b", "y_c")]
    gate_params = [_p_row(P["b_gate"], D, k) for k in range(3)]

    def gate_bwd(g0, g1, g2, ya, yb, yc, dm, b0, b1, b2):
        return _with_vjp(_gate_merge_fn, 9, (0, 1, 2, 3, 4, 5), (6, 7, 8))(g0, g1, g2, ya, yb, yc, b0, b1, b2, dm)

    dg0, dg1, dg2, dya, dyb, dyc, db0, db1, db2 = _row_call(
        f"gate_b{i}", gate_bwd, D, 1, gate_rows + [_rows(dmerged, D)], gate_params,
        [BF16, BF16, BF16, BF16, BF16, BF16], [_a_row(D, D)] * 3)
    G["b_gate"] = jnp.concatenate([db0, db1, db2], axis=1)[0]

    dyc_pre = _matmul(f"wc_bx{i}", dyc, W["w_c"], "nt")
    G["w_c"] = _matmul(f"wc_bw{i}", sv["yc_pre"], dyc, "tn", out_dtype=BF16)

    def gnorm_bwd(y_, z_, dy_, w_):
        return _with_vjp(_gated_norm_fn, 3, (0, 1), (2,))(y_, z_, w_, dy_)

    dy_ssd, dz, dnw = _row_call(
        f"ssdnorm_b{i}", gnorm_bwd, 512, 2,
        [_rows(sv["y_ssd"], 512), _rows(proj, 512, OFF_Z // 512), _rows(dyc_pre, 512)],
        [_p_row(P["ssd_norm_w"], 512)], [F32, BF16], [_a_row(D, 512)])
    G["ssd_norm_w"] = dnw[0]
    dxbc_c, ddt, da_row, ddtb_row, dd_exp = _ssd_bwd(f"ssd_b{i}", sv["xbc_c"], proj, sv["states"], dy_ssd,
                                                     P["a_row"], P["dtb_row"], P["d_exp"])
    a_vec = P["a_row"][0, :SSD_HEADS]
    G["ssd_a_log"] = da_row[0, :SSD_HEADS] * a_vec
    G["ssd_dt_bias"] = ddtb_row[0, :SSD_HEADS]
    G["ssd_d"] = dd_exp.reshape(SSD_HEADS, HEAD_DIM).sum(axis=1)
    conv_params = [_p_row(P["ssd_conv_w"][k], 128) for k in range(4)] + [_p_row(P["ssd_conv_b"], 128)]

    def conv_bwd(x_, dy_, w0, w1, w2, w3, b_):
        return _with_vjp(_ssd_conv_fn, 6, (0,), (1, 2, 3, 4, 5))(x_, w0, w1, w2, w3, b_, dy_)

    dxbc, c0, c1, c2, c3, cb = _col_call(
        f"ssdconv_b{i}", conv_bwd, 128, SSD_XBC // 128, [_rows(proj, 128, OFF_XBC // 128), _rows(dxbc_c, 128)],
        conv_params, [BF16], [_a_row(SSD_XBC, 128)] * 5)
    G["ssd_conv_w"] = jnp.concatenate([c0, c1, c2, c3], axis=0)
    G["ssd_conv_b"] = cb[0]

    dyb_pre = _matmul(f"wb_bx{i}", dyb, W["w_b"], "nt")
    G["w_b"] = _matmul(f"wb_bw{i}", sv["yb_pre"], dyb, "tn", out_dtype=BF16)
    pool_params = [(W["pool_w"], (N_DEV, None, 32, 256), lambda j, i_: (0, j, 0, 0)), _p_row(P["pool_scale"], 256)]

    def pool_bwd(x_, dy_, wg, sc):
        return _with_vjp(_pool_fn, 3, (0,), (1, 2))(x_, wg.astype(F32), sc, dy_)

    dpool, dwg, dsc = _col_call(
        f"pool_b{i}", pool_bwd, 256, 4, [_rows(proj, 256, OFF_POOL // 256), _rows(dyb_pre, 256)], pool_params, [BF16],
        [((N_DEV, 4, 32, 256), (N_DEV, None, 32, 256), lambda j, i_: (0, j, 0, 0)), _a_row(D, 256)])
    G["pool_w"] = dwg
    G["pool_scale"] = dsc[0]

    datt = _matmul(f"wa_bx{i}", dya, W["w_a"], "nt")
    G["w_a"] = _matmul(f"wa_bw{i}", sv["att"], dya, "tn", out_dtype=BF16)

    def merge_bwd(o0, o1, o2, l0, l1, l2, da_):
        return _with_vjp(_att_merge_fn, 6, (0, 1, 2, 3, 4, 5), ())(o0, o1, o2, l0, l1, l2, da_)

    dol = _row_call(f"attmerge_b{i}", merge_bwd, 128, 3,
                    [_rows(t, 128) for t in sv["att_o"] + sv["att_l"]] + [_rows(datt, 128)], [], [F32] * 6)
    dqs, dks, dvs = [], [], []
    g_rel = jnp.zeros((REL_BUCKETS, 18), F32)
    for gi, (_, dil) in enumerate(ATT_GROUPS):
        q, k, v = sv["att_res"][gi]
        dq, dk, dv, gbp, gbc = _att_bwd(f"att_b{i}_{gi}", q, k, v, bias_tabs[gi][0], bias_tabs[gi][1],
                                        _to_residue(dol[gi], dil), _to_residue(dol[3 + gi], dil), dil)
        dqs.append(dq.reshape(S, ATT_GW))
        dks.append(dk.reshape(S, ATT_GW))
        dvs.append(dv.reshape(S, ATT_GW))
        oh = onehots[gi]
        gt = (jnp.einsum("hqk,qkb->bh", gbp, oh[0], precision=lax.Precision.HIGHEST)
              + jnp.einsum("hqk,qkb->bh", gbc, oh[1], precision=lax.Precision.HIGHEST))
        g_rel = g_rel.at[:, gi * 6:(gi + 1) * 6].add(gt)
    G["rel_bias"] = g_rel

    dproj = jnp.concatenate(
        [dg0, dg1, dg2, dpool, dz, dxbc, ddt.astype(BF16), jnp.zeros((S, OFF_Q - OFF_DT - DT_PAD), BF16)]
        + [t.astype(BF16) for t in dqs + dks + dvs], axis=1)
    du = _matmul(f"inproj_bx{i}", dproj, W["w_in"], "nt")
    G["w_in"] = _matmul(f"inproj_bw{i}", sv["u"], dproj, "tn", out_dtype=BF16)
    ln1_g = P["ln1_g"] + on_sharded_grads(G)
    (dx,), (G["ln1_g"],) = _split_res(_row_call(
        f"ln1_b{i}", lambda x_, du_, dres, g_: norm_bwd(x_, g_, du_, dres), D, 1,
        [_rows(x, D), _rows(du, D), _rows(dx1, D)], [_p_row(ln1_g, D)], [F32], [_a_row(D, D)]), 1)
    G["ln1_g"] = G["ln1_g"][0]
    G["ln2_g"] = G["ln2_g"][0]
    return dx, G


def _rows_as_param(arr, cw):
    return (arr, (arr.shape[0], cw), lambda j, i: (0, j))


def _split_res(res, n_out):
    return tuple(res[:n_out]), tuple(res[n_out:])


def kernel(x, rel_bias, ln1_g, w_in, b_gate, w_a, pool_w, pool_scale, w_b, ssd_conv_w, ssd_conv_b, ssd_dt_bias, ssd_a_log, ssd_d, ssd_norm_w, w_c, w_o, ln2_g, ffn_w_up, ffn_conv_w, ffn_conv_b, ffn_w_down, final_g, loss_target, m_rel_bias, m_ln1_g, m_w_in, m_b_gate, m_w_a, m_pool_w, m_pool_scale, m_w_b, m_ssd_conv_w, m_ssd_conv_b, m_ssd_dt_bias, m_ssd_a_log, m_ssd_d, m_ssd_norm_w, m_w_c, m_w_o, m_ln2_g, m_ffn_w_up, m_ffn_conv_w, m_ffn_conv_b, m_ffn_w_down, m_final_g, v_rel_bias, v_ln1_g, v_w_in, v_b_gate, v_w_a, v_pool_w, v_pool_scale, v_w_b, v_ssd_conv_w, v_ssd_conv_b, v_ssd_dt_bias, v_ssd_a_log, v_ssd_d, v_ssd_norm_w, v_w_c, v_w_o, v_ln2_g, v_ffn_w_up, v_ffn_conv_w, v_ffn_conv_b, v_ffn_w_down, v_final_g):
    args = locals()
    wts = {n: args[n] for n in WEIGHTS}
    mom = {n: args["m_" + n] for n in WEIGHTS}
    var = {n: args["v_" + n] for n in WEIGHTS}
    names = list(SHARDED)

    onehots = _bucket_onehots()
    bias_tabs = []
    for gi in range(3):
        tab = rel_bias[:, gi * 6:(gi + 1) * 6]
        b = jnp.einsum("pqkb,bh->phqk", onehots[gi], tab, precision=lax.Precision.HIGHEST)
        bias_tabs.append((b[0], b[1]))

    def gather_start(tag, i, which, after):
        shards = [wts[n][i].astype(BF16) if n in MATMUL_WEIGHTS else wts[n][i] for n in which]
        return _exchange_start(f"gather_start{tag}", shards, False, after)

    def layer_params(i, which, landed):
        full = {n: _local_weight(f"local_{n}{i}", n, g) for n, g in zip(which, landed)}
        W = {n: full[n] for n in which if n in MATMUL_WEIGHTS}
        P = {}
        if "ssd_conv_w" in full:
            P["ssd_conv_w"] = [_row(full["ssd_conv_w"][k]) for k in range(4)]
            P["ffn_conv_w"] = [_row(full["ffn_conv_w"][k]) for k in range(3)]
        return W, P

    def replicated_params(i):
        return {"ln1_g": _row(ln1_g[i]), "ln2_g": _row(ln2_g[i]), "b_gate": _row(b_gate[i]),
                "pool_scale": _row(pool_scale[i]), "ssd_conv_b": _row(ssd_conv_b[i]),
                "ssd_norm_w": _row(ssd_norm_w[i]), "ffn_conv_b": _row(ffn_conv_b[i]),
                "a_row": _row(-jnp.exp(ssd_a_log[i]), 128), "dtb_row": _row(ssd_dt_bias[i], 128),
                "d_exp": _row(jnp.repeat(ssd_d[i], HEAD_DIM))}

    h = x.reshape(S, D)
    saved, Ws, Ps = [], [], []
    first, rest = ["w_in"], [n for n in names if n != "w_in"]
    state, _ = gather_start("0a", 0, first, h)
    landed_first = _exchange_wait("gather_wait0a", state, h)
    state_rest, token = gather_start("0b", 0, rest, landed_first[0])
    nxt = {}

    def late0(att):
        landed_rest = _exchange_wait("gather_wait0b", state_rest, att)
        W2, P2 = layer_params(0, rest, landed_rest)
        nxt["state"], tok = gather_start("1", 1, names, landed_rest[0])
        P2["pool_scale"] = _row(pool_scale[0]) + tok[0, 0]
        return W2, P2

    for i in range(DEPTH):
        P = replicated_params(i)
        if i == 0:
            W, P1 = layer_params(0, first, landed_first)
        else:
            W, P1 = layer_params(i, names, landed)
            if i + 1 < DEPTH:
                nxt["state"], token = gather_start(str(i + 1), i + 1, names, landed[0])
        P.update(P1)
        if i + 1 < DEPTH:
            P["ln1_g"] = P["ln1_g"] + token[0, 0]
        h, sv = _layer_fwd(i, h, W, P, bias_tabs, late0 if i == 0 else None)
        Ws.append(W)
        Ps.append(dict(P, ln1_g=_row(ln1_g[i]), pool_scale=_row(pool_scale[i])))
        saved.append(sv)
        if i + 1 < DEPTH:
            landed = _exchange_wait(f"gather_wait{i + 1}", nxt["state"], h)

    def loss_bwd(x_, t_, g_):
        lval, vjp = jax.vjp(_loss_fn, x_, t_, g_)
        dx_, _, dg_ = vjp(jnp.ones_like(lval))
        return (dx_,), (dg_, jnp.broadcast_to(lval, (1, 128)))

    dh, g_final, loss_part = _row_call("loss", loss_bwd, D, 1, [_rows(h, D), _rows(loss_target.reshape(S, D), D)],
                                       [_p_row(_row(final_g), D)], [F32], [_a_row(D, D), _a_row(128, 128)])
    loss = lax.psum(loss_part[0, 0], MESH_AXES)

    grads = {n: [None] * DEPTH for n in WEIGHTS if n not in ("rel_bias", "final_g")}
    g_rel = jnp.zeros((REL_BUCKETS, 18), F32)
    slots = [None] * DEPTH
    pending = None
    for i in reversed(range(DEPTH)):
        started = {}

        def on_sharded_grads(G, i=i, started=started):
            parts = [_device_blocks(f"blocks_{n}{i}", n, G[n]) for n in names]
            started["state"], token = _exchange_start(f"scatter_start{i}", parts, True, G["b_gate"])
            return token[0, 0]

        dh, G = _layer_bwd(i, dh, saved[i], Ws[i], Ps[i], bias_tabs, onehots, on_sharded_grads)
        if pending is not None:
            j, st = pending
            slots[j] = _exchange_wait(f"scatter_wait{j}", st, dh)
        pending = (i, started["state"])
        g_rel = g_rel + G.pop("rel_bias")
        for n, g in G.items():
            grads[n][i] = g
    grad_x = dh.reshape(1, S, D)
    local = {n: jnp.stack(grads[n]) for n in grads if n not in SHARDED}
    local["rel_bias"] = g_rel
    local["final_g"] = g_final[0]
    out = {}

    def pack(d):
        flat = jnp.concatenate([d[n].reshape(-1).astype(F32) for n in REPLICATED])
        rows = -(-flat.shape[0] // (8 * 128)) * 8
        return jnp.pad(flat, (0, rows * 128 - flat.shape[0])).reshape(rows, 128)

    (rep_slots,) = _exchange("gather_small_grads", [pack(local)], scatter=False)
    rep = _adamw("adamw_small", rep_slots, pack(wts), pack(mom), pack(var))
    off = 0
    for n in REPLICATED:
        sz = int(np.prod(wts[n].shape))
        out[n] = [t.reshape(-1)[off:off + sz].reshape(wts[n].shape) for t in rep]
        off += sz

    def flat2(n):
        shp = wts[n].shape
        r, c = int(np.prod(shp[:-1])), shp[-1]
        return r, c, wts[n].reshape(r, c), mom[n].reshape(r, c), var[n].reshape(r, c)

    chain = {}
    done = rep[0][0, 0]
    for k, n in enumerate(names):
        if n in MATMUL_WEIGHTS:
            r, c, w2, m2, v2 = flat2(n)
            res = None
            for i in (3, 2, 1):
                res = _adamw(f"adamw_{n}{i}", slots[i][k].reshape(N_DEV, r // DEPTH, c), w2, m2, v2,
                             first_row=i * (r // DEPTH), prev=res)
            chain[n] = res
            done = done + res[0][-1, 0]
    slots[0] = _exchange_wait("scatter_wait0", pending[1], done.reshape(1, 1))
    for k, n in enumerate(names):
        r, c, w2, m2, v2 = flat2(n)
        if n in MATMUL_WEIGHTS:
            res = _adamw(f"adamw_{n}0", slots[0][k].reshape(N_DEV, r // DEPTH, c), w2, m2, v2, first_row=0, prev=chain[n])
        else:
            stacked = jnp.stack([slots[i][k] for i in range(DEPTH)], axis=1)
            res = _adamw("adamw_" + n, stacked.reshape(N_DEV, r, c), w2, m2, v2)
        out[n] = [t.reshape(wts[n].shape) for t in res]

    return (loss, grad_x, *[out[n][0] for n in WEIGHTS], *[out[n][1] for n in WEIGHTS],
            *[out[n][2] for n in WEIGHTS], *[out[n][3] for n in WEIGHTS])
```

```python
import functools
import math

import numpy as np
import jax
import jax.numpy as jnp
from jax import lax
from jax.experimental import pallas as pl
from jax.experimental.pallas import tpu as pltpu

F32 = jnp.float32
BF16 = jnp.bfloat16

N_DEV = 8
MESH_AXES = ("x", "y", "c")
S = 4096
D = 1024
DEPTH = 4
HEAD_DIM = 64
ATT_W = 1152
ATT_GW = 384
ATT_GROUPS = ((128, 1), (512, 4), (2048, 16))
ATT_BLOCK = 128
REL_BUCKETS = 32
REL_MAX_DISTANCE = 2048
POOL_WINDOWS = (2, 4, 8, 16)
SSD_HEADS = 16
SSD_CHUNK = 128
SSD_XBC = 1536
D_FF = 2816
IN_WIDTH = 10128
EPS = 1e-6
NEG = -1e30

OFF_GATE, OFF_POOL, OFF_Z, OFF_XBC, OFF_DT, OFF_Q, OFF_K, OFF_V = 0, 3072, 4096, 5120, 6656, 6912, 8064, 9216
NP = 10368
DT_PAD = 128

ADAM_LR, ADAM_B1, ADAM_B2, ADAM_EPS, ADAM_WD, ADAM_STEP = 0.001, 0.9, 0.999, 1e-08, 0.01, 10

VMEM_LIMIT = 52 * 1024 * 1024


def _cparams(sem=None):
    return pltpu.CompilerParams(dimension_semantics=sem, vmem_limit_bytes=VMEM_LIMIT)


def _dot(a, b, ca, cb):
    return lax.dot_general(a.astype(BF16), b.astype(BF16), (((ca,), (cb,)), ((), ())), preferred_element_type=F32)


@jax.custom_vjp
def _mm(a, b):
    return _dot(a, b, 1, 0)


def _mm_fwd(a, b):
    return _mm(a, b), (a, b)


def _mm_bwd(res, g):
    a, b = res
    return _dot(g, b, 1, 1).astype(a.dtype), _dot(a, g, 0, 0).astype(b.dtype)


_mm.defvjp(_mm_fwd, _mm_bwd)


@jax.custom_vjp
def _mm_nt(a, b):
    return _dot(a, b, 1, 1)


def _mm_nt_fwd(a, b):
    return _mm_nt(a, b), (a, b)


def _mm_nt_bwd(res, g):
    a, b = res
    return _dot(g, b, 1, 0).astype(a.dtype), _dot(g, a, 0, 0).astype(b.dtype)


_mm_nt.defvjp(_mm_nt_fwd, _mm_nt_bwd)


@jax.custom_vjp
def _mm_tn(a, b):
    return _dot(a, b, 0, 0)


def _mm_tn_fwd(a, b):
    return _mm_tn(a, b), (a, b)


def _mm_tn_bwd(res, g):
    a, b = res
    return _dot(b, g, 1, 1).astype(a.dtype), _dot(a, g, 1, 0).astype(b.dtype)


_mm_tn.defvjp(_mm_tn_fwd, _mm_tn_bwd)


def _shift_impl(x, j):
    n = x.shape[0]
    if j == 0:
        return x
    r = pltpu.roll(x, j % n, axis=0)
    t = lax.broadcasted_iota(jnp.int32, x.shape, 0)
    mask = (t >= j) if j > 0 else (t < n + j)
    return jnp.where(mask, r, 0.0)


@functools.partial(jax.custom_vjp, nondiff_argnums=(1,))
def _shift(x, j):
    return _shift_impl(x, j)


_shift.defvjp(lambda x, j: (_shift_impl(x, j), None), lambda j, _, g: (_shift_impl(g, -j),))


def _tri(lower):
    r = lax.broadcasted_iota(jnp.int32, (SSD_CHUNK, SSD_CHUNK), 0)
    c = lax.broadcasted_iota(jnp.int32, (SSD_CHUNK, SSD_CHUNK), 1)
    return (r >= c) if lower else (r <= c)


def _dot_hi(a, b):
    return lax.dot_general(a, b, (((1,), (0,)), ((), ())), precision=lax.Precision.HIGHEST,
                           preferred_element_type=F32)


@jax.custom_vjp
def _cumsum_rows(a):
    return _dot_hi(_tri(True).astype(F32), a)


_cumsum_rows.defvjp(lambda a: (_cumsum_rows(a), None), lambda _, g: (_dot_hi(_tri(False).astype(F32), g),))


@jax.custom_vjp
def _softplus(x):
    return jnp.maximum(x, 0.0) + jnp.log(1.0 + jnp.exp(-jnp.abs(x)))


_softplus.defvjp(lambda x: (_softplus(x), x), lambda x, g: (g * jax.nn.sigmoid(x),))


def _silu(x):
    return x * jax.nn.sigmoid(x)


def _rows(arr, cw, off=0, lead=None, roff=0):
    return (arr, cw, off, lead, roff)


def _tiled(name, fn, grid, tm, rows, params, outs, accs=(), out_roff=0, prev_outs=None):
    ncol, nrow = grid
    in_specs, operands = [], []
    for arr, cw, off, lead, roff in rows:
        if lead is None:
            in_specs.append(pl.BlockSpec((tm, cw), functools.partial(lambda j, i, off, roff: (roff + i, off + j),
                                                                     off=off, roff=roff)))
        else:
            in_specs.append(pl.BlockSpec((None, tm, cw), functools.partial(
                lambda j, i, off, lead, roff: (lead, roff + i, off + j), off=off, lead=lead, roff=roff)))
        operands.append(arr)
    for arr, bs, im in params:
        in_specs.append(pl.BlockSpec(bs, im))
        operands.append(arr)
    out_specs, out_shape = [], []
    for n_rows, cw, dt in outs:
        out_specs.append(pl.BlockSpec((tm, cw), functools.partial(lambda j, i, r: (r + i, j), r=out_roff)))
        out_shape.append(jax.ShapeDtypeStruct((n_rows, ncol * cw), dt))
    for shape, bs, im in accs:
        out_specs.append(pl.BlockSpec(bs, im))
        out_shape.append(jax.ShapeDtypeStruct(shape, F32))
    n_in, n_out = len(operands), len(outs)
    aliases = {}
    if prev_outs is not None:
        for k, p in enumerate(prev_outs):
            aliases[len(operands)] = k
            in_specs.append(pl.BlockSpec(memory_space=pl.ANY))
            operands.append(p)

    n_all = len(operands)

    def body(*refs):
        vals = [r[...] for r in refs[:n_in]]
        o_vals, a_vals = fn(*vals)
        for r, v in zip(refs[n_all:n_all + n_out], o_vals):
            r[...] = v.astype(r.dtype)
        i = pl.program_id(1)
        for r, v in zip(refs[n_all + n_out:], a_vals):
            @pl.when(i == 0)
            def _(r=r, v=v):
                r[...] = v.astype(r.dtype)

            @pl.when(i > 0)
            def _(r=r, v=v):
                r[...] += v.astype(r.dtype)

    res = pl.pallas_call(body, grid=grid, in_specs=in_specs, out_specs=out_specs, out_shape=out_shape, name=name,
                         input_output_aliases=aliases, compiler_params=_cparams(("arbitrary", "arbitrary")))(*operands)
    return list(res)


def _with_vjp(fn, n_prim, want_out, want_acc):
    def f(*args):
        prim, g = args[:n_prim], args[n_prim:]
        outs, vjp = jax.vjp(lambda *a: fn(*a), *prim)
        d = vjp(tuple(gi.astype(o.dtype) for gi, o in zip(g, outs)))
        return tuple(d[k] for k in want_out), tuple(d[k] for k in want_acc)
    return f


def _p_row(arr, cw, off=0):
    return (arr, (1, cw), functools.partial(lambda j, i, off: (0, off + j), off=off))


def _a_row(n, cw):
    return ((1, n), (1, cw), lambda j, i: (0, j))


def _pick(n, cap, mult):
    best = None
    for t in range(mult, min(n, cap) + 1, mult):
        if n % t == 0:
            best = t
    return best if best is not None else n


def _matmul(name, a, b, mode, add=None, out_dtype=F32):
    if mode == "nn":
        (M, K), N = a.shape, b.shape[1]
    elif mode == "nt":
        (M, K), N = a.shape, b.shape[0]
    else:
        (K, M), N = a.shape, b.shape[1]
    tm = _pick(M, 1408, 128) if mode == "tn" else _pick(M, 512, 8)
    tn = _pick(N, 1536, 128)
    k_cap = 2048 if mode == "tn" else 3456
    tk = K if K <= k_cap else _pick(K, k_cap, 128)
    nk = K // tk
    a_bytes, b_bytes = a.size * a.dtype.itemsize, b.size * b.dtype.itemsize
    swap = nk == 1 and a_bytes * (N // tn) + b_bytes < b_bytes * (M // tm) + a_bytes
    ij = (lambda g0, g1: (g1, g0)) if swap else (lambda g0, g1: (g0, g1))

    def spec(block, index):
        return pl.BlockSpec(block, lambda g0, g1, k: index(*ij(g0, g1), k))

    if mode == "nn":
        a_spec = spec((tm, tk), lambda i, j, k: (i, k))
        b_spec = spec((tk, tn), lambda i, j, k: (k, j))
        ca, cb = 1, 0
    elif mode == "nt":
        a_spec = spec((tm, tk), lambda i, j, k: (i, k))
        b_spec = spec((tn, tk), lambda i, j, k: (j, k))
        ca, cb = 1, 1
    else:
        a_spec = spec((tk, tm), lambda i, j, k: (k, i))
        b_spec = spec((tk, tn), lambda i, j, k: (k, j))
        ca, cb = 0, 0
    in_specs, operands = [a_spec, b_spec], [a, b]
    if add is not None:
        in_specs.append(spec((tm, tn), lambda i, j, k: (i, j)))
        operands.append(add)

    def finish(r, refs, o_ref):
        if add is not None:
            r = r + refs[2][...]
        o_ref[...] = r.astype(o_ref.dtype)

    def body_single(*refs):
        finish(_dot(refs[0][...], refs[1][...], ca, cb), refs, refs[-1])

    def body_multi(*refs):
        o_ref, acc_ref = refs[-2], refs[-1]
        k = pl.program_id(2)
        d = _dot(refs[0][...], refs[1][...], ca, cb)

        @pl.when(k == 0)
        def _():
            acc_ref[...] = d

        @pl.when(jnp.logical_and(k > 0, k < nk - 1))
        def _():
            acc_ref[...] += d

        @pl.when(k == nk - 1)
        def _():
            finish(acc_ref[...] + d, refs, o_ref)

    grid = (N // tn, M // tm, nk) if swap else (M // tm, N // tn, nk)
    return pl.pallas_call(
        body_single if nk == 1 else body_multi, grid=grid, in_specs=in_specs,
        out_specs=spec((tm, tn), lambda i, j, k: (i, j)),
        out_shape=jax.ShapeDtypeStruct((M, N), out_dtype),
        scratch_shapes=[] if nk == 1 else [pltpu.VMEM((tm, tn), F32)], name=name,
        compiler_params=_cparams(("parallel", "parallel", "arbitrary")))(*operands)


def _seg_copies(segs, c):
    out = []
    for lo, hi, dst in segs:
        n = lo
        while n < hi:
            p = n // c
            w = min(hi, (p + 1) * c) - n
            out.append((p, n - p * c, w, dst + n - lo))
            n += w
    return out


def _col_assemble(name, blocks, copies, zeros, n_out):
    _, R, c = blocks.shape
    tm = R if R <= 128 else 128

    def body(b_ref, o_ref):
        for p, s, w, d in copies:
            o_ref[:, d:d + w] = b_ref[p, :, s:s + w]
        for lo, hi in zeros:
            o_ref[:, lo:hi] = jnp.zeros((tm, hi - lo), o_ref.dtype)

    return pl.pallas_call(
        body, grid=(R // tm,), in_specs=[pl.BlockSpec((N_DEV, tm, c), lambda i: (0, i, 0))],
        out_specs=pl.BlockSpec((tm, n_out), lambda i: (i, 0)),
        out_shape=jax.ShapeDtypeStruct((R, n_out), blocks.dtype), name=name, compiler_params=_cparams(("parallel",)))(blocks)


def _col_split(name, full, copies, c, dtype):
    R, n = full.shape
    tm = R if R <= 128 else 128

    def body(f_ref, o_ref):
        for p, s, w, d in copies:
            o_ref[p, :, s:s + w] = f_ref[:, d:d + w].astype(dtype)

    return pl.pallas_call(
        body, grid=(R // tm,), in_specs=[pl.BlockSpec((tm, n), lambda i: (i, 0))],
        out_specs=pl.BlockSpec((N_DEV, tm, c), lambda i: (0, i, 0)),
        out_shape=jax.ShapeDtypeStruct((N_DEV, R, c), dtype), name=name, compiler_params=_cparams(("parallel",)))(full)


def _rmsnorm_fn(x, g):
    x = x.astype(F32)
    return (x * lax.rsqrt(jnp.mean(x * x, axis=-1, keepdims=True) + EPS) * g,)


def _gate_merge_fn(g0, g1, g2, ya, yb, yc, b0, b1, b2):
    return (jax.nn.sigmoid(g0 + b0) * ya + jax.nn.sigmoid(g1 + b1) * yb + jax.nn.sigmoid(g2 + b2) * yc,)


def _gated_norm_fn(y, z, w):
    t = y * _silu(z)
    return (t * lax.rsqrt(jnp.mean(t * t, axis=-1, keepdims=True) + EPS) * w,)


def _att_merge_fn(o0, o1, o2, l0, l1, l2):
    m = lax.stop_gradient(jnp.maximum(jnp.maximum(l0, l1), l2))
    e0, e1, e2 = jnp.exp(l0 - m), jnp.exp(l1 - m), jnp.exp(l2 - m)
    return ((e0 * o0 + e1 * o1 + e2 * o2) / (e0 + e1 + e2),)


def _loss_fn(x, tgt, g):
    (y,) = _rmsnorm_fn(x, g)
    err = y - tgt
    return 0.5 * jnp.sum(jnp.mean(err * err, axis=-1, keepdims=True), axis=0, keepdims=True)


def _pool_fn(x, wg, scale):
    g = pl.program_id(0)
    s2 = x + _shift(x, 1)
    s4 = s2 + _shift(s2, 2)
    s8 = s4 + _shift(s4, 4)
    s16 = s8 + _shift(s8, 8)
    win = ((g == 0).astype(F32) * s2 + (g == 1).astype(F32) * s4 + (g == 2).astype(F32) * s8
           + (g == 3).astype(F32) * s16)
    t = lax.broadcasted_iota(jnp.int32, (x.shape[0], 1), 0) + 1
    cnt = jnp.minimum(t, jnp.left_shift(2, g)).astype(F32)
    d = win / cnt - x
    return (_mm(d, wg.reshape(256, 256)) * scale,)


def _dwconv(x, taps, b):
    k = len(taps)
    y = taps[k - 1] * x + b
    for i in range(k - 1):
        y = y + taps[i] * _shift(x, k - 1 - i)
    return y


def _ssd_conv_fn(x, w0, w1, w2, w3, b):
    return (_silu(_dwconv(x, (w0, w1, w2, w3), b)),)


def _ffn_act_fn(xa, xv, a0, a1, a2, ab, v0, v1, v2, vb):
    return (_silu(_dwconv(xa, (a0, a1, a2), ab)) * _dwconv(xv, (v0, v1, v2), vb),)


def _att_block(q, kp, kc, vp, vc, bpa, bpb, bca, bcb, prev_ok):
    qi = lax.broadcasted_iota(jnp.int32, (ATT_BLOCK, ATT_BLOCK), 0)
    kj = lax.broadcasted_iota(jnp.int32, (ATT_BLOCK, ATT_BLOCK), 1)
    lane = lax.broadcasted_iota(jnp.int32, (1, 2 * HEAD_DIM), 1)
    mask_c = kj <= qi
    mask_p = jnp.logical_and(kj >= qi, prev_ok)
    q = q.astype(F32)
    o = None
    lse = None
    for hh, (bp, bc) in enumerate(((bpa, bca), (bpb, bcb))):
        hm = (lane // HEAD_DIM == hh).astype(F32)
        qh = q * (hm * (1.0 / math.sqrt(HEAD_DIM)))
        sp = jnp.where(mask_p, _mm_nt(qh, kp) + bp, NEG)
        sc = jnp.where(mask_c, _mm_nt(qh, kc) + bc, NEG)
        m = lax.stop_gradient(jnp.maximum(jnp.max(sp, axis=1, keepdims=True), jnp.max(sc, axis=1, keepdims=True)))
        pp = jnp.exp(sp - m)
        pc = jnp.exp(sc - m)
        l = jnp.sum(pp, axis=1, keepdims=True) + jnp.sum(pc, axis=1, keepdims=True)
        oh = (_mm(pp, vp) + _mm(pc, vc)) / l * hm
        lh = (m + jnp.log(l)) * hm
        o = oh if o is None else o + oh
        lse = lh if lse is None else lse + lh
    return o, lse


def _att_slab(dil):
    nbk = 4 if dil == 1 else 1
    t = ATT_BLOCK * dil * nbk
    return nbk, t, S // t


def _att_in_specs(gi, t):
    def spec(off, prev):
        col = off // 128 + gi * 3

        def index(p, j, col=col, prev=prev):
            jj = jnp.minimum(j, S // t - 1)
            return (jnp.maximum(jj - 1, 0) if prev else jj, col + p)
        return pl.BlockSpec((t, 2 * HEAD_DIM), index)
    return [spec(OFF_Q, False), spec(OFF_K, False), spec(OFF_K, True), spec(OFF_V, False), spec(OFF_V, True)]


def _bias_specs():
    return [pl.BlockSpec((None, ATT_BLOCK, ATT_BLOCK), functools.partial(lambda p, j, hh: (2 * p + hh, 0, 0), hh=hh))
            for hh in (0, 1)]


def _att_units(dil, nbk, body):
    def per_residue(r, carry):
        for b in range(nbk):
            rows = pl.ds(b * ATT_BLOCK * dil + r, ATT_BLOCK, stride=dil)
            prev = pl.ds(((b - 1) % nbk) * ATT_BLOCK * dil + r, ATT_BLOCK, stride=dil)
            body(b, rows, prev, b > 0)
        return carry
    if dil == 1:
        per_residue(0, 0)
    else:
        lax.fori_loop(0, dil, per_residue, 0)


def _att_fwd(name, proj, gi, bias_p, bias_c):
    dil = ATT_GROUPS[gi][1]
    nbk, t, ns = _att_slab(dil)
    bsp = _bias_specs()
    out_spec = pl.BlockSpec((t, 2 * HEAD_DIM), lambda p, j: (j, p))

    def body(q_ref, kc_ref, kp_ref, vc_ref, vp_ref, bpa, bpb, bca, bcb, o_ref, l_ref):
        first = pl.program_id(1) == 0
        biases = (bpa[...], bpb[...], bca[...], bcb[...])

        def unit(b, rows, prev, in_slab):
            kp = kc_ref[prev, :] if in_slab else kp_ref[prev, :]
            vp = vc_ref[prev, :] if in_slab else vp_ref[prev, :]
            prev_ok = True if in_slab else jnp.logical_not(first)
            o, lse = _att_block(q_ref[rows, :], kp, kc_ref[rows, :], vp, vc_ref[rows, :], *biases, prev_ok)
            o_ref[rows, :] = o
            l_ref[rows, :] = lse

        _att_units(dil, nbk, unit)

    shp = jax.ShapeDtypeStruct((S, ATT_GW), F32)
    return pl.pallas_call(
        body, grid=(3, ns), in_specs=_att_in_specs(gi, t) + [bsp[0], bsp[1], bsp[0], bsp[1]],
        out_specs=[out_spec, out_spec], out_shape=[shp, shp], name=name,
        compiler_params=_cparams(("arbitrary",) * 2))(proj, proj, proj, proj, proj, bias_p, bias_p, bias_c, bias_c)


def _att_bwd(name, proj, gi, bias_p, bias_c, do, dl):
    dil = ATT_GROUPS[gi][1]
    nbk, t, ns = _att_slab(dil)
    bsp = _bias_specs()
    blk = (t, 2 * HEAD_DIM)
    cur = pl.BlockSpec(blk, lambda p, j: (jnp.minimum(j, ns - 1), p))
    done = pl.BlockSpec(blk, lambda p, j: (jnp.maximum(j - 1, 0), p))
    gsp = pl.BlockSpec((None, ATT_BLOCK, ATT_BLOCK), lambda p, j: (p, 0, 0))

    def body(q_ref, kc_ref, kp_ref, vc_ref, vp_ref, bpa, bpb, bca, bcb, do_ref, dl_ref,
             dq_ref, dk_ref, dv_ref, gpa, gpb, gca, gcb, acck, accv):
        j = pl.program_id(1)
        mine, other = acck.at[j % 2], acck.at[1 - j % 2]
        mine_v, other_v = accv.at[j % 2], accv.at[1 - j % 2]

        @pl.when(j == 0)
        def _():
            for g in (gpa, gpb, gca, gcb):
                g[...] = jnp.zeros_like(g)
            other[...] = jnp.zeros_like(other)
            other_v[...] = jnp.zeros_like(other_v)

        @pl.when(j < ns)
        def _():
            mine[...] = jnp.zeros_like(mine)
            mine_v[...] = jnp.zeros_like(mine_v)
            biases = (bpa[...], bpb[...], bca[...], bcb[...])

            def unit(b, rows, prev, in_slab):
                kp = kc_ref[prev, :] if in_slab else kp_ref[prev, :]
                vp = vc_ref[prev, :] if in_slab else vp_ref[prev, :]
                prev_ok = True if in_slab else j > 0
                prim = (q_ref[rows, :], kp, kc_ref[rows, :], vp, vc_ref[rows, :]) + biases
                _, vjp = jax.vjp(lambda *a: _att_block(*a, prev_ok), *prim)
                dq, dkp, dkc, dvp, dvc, dpa, dpb, dca, dcb = vjp((do_ref[rows, :], dl_ref[rows, :]))
                dq_ref[rows, :] = dq
                mine[rows, :] += dkc
                mine_v[rows, :] += dvc
                tgt, tgt_v = (mine, mine_v) if in_slab else (other, other_v)
                tgt[prev, :] += dkp
                tgt_v[prev, :] += dvp
                gpa[...] += dpa
                gpb[...] += dpb
                gca[...] += dca
                gcb[...] += dcb

            _att_units(dil, nbk, unit)

        dk_ref[...] = other[...]
        dv_ref[...] = other_v[...]

    shp = jax.ShapeDtypeStruct((S, ATT_GW), F32)
    gshp = jax.ShapeDtypeStruct((3, ATT_BLOCK, ATT_BLOCK), F32)
    res = pl.pallas_call(
        body, grid=(3, ns + 1),
        in_specs=_att_in_specs(gi, t) + [bsp[0], bsp[1], bsp[0], bsp[1], cur, cur],
        out_specs=[cur, done, done, gsp, gsp, gsp, gsp],
        out_shape=[shp, shp, shp, gshp, gshp, gshp, gshp],
        scratch_shapes=[pltpu.VMEM((2,) + blk, F32)] * 2, name=name,
        compiler_params=_cparams(("arbitrary",) * 2))(proj, proj, proj, proj, proj, bias_p, bias_p, bias_c, bias_c, do, dl)
    dq, dk, dv, gpa, gpb, gca, gcb = res
    heads = lambda a, b: jnp.stack([a, b], axis=1).reshape(6, ATT_BLOCK, ATT_BLOCK)
    return dq, dk, dv, heads(gpa, gpb), heads(gca, gcb)


N_PAIR = SSD_HEADS // 2


def _ssd_chunk(xs, bs, cs_in, dt_raw, hs, a_row, dtb_row, ds):
    lane = lax.broadcasted_iota(jnp.int32, (1, 128), 1)
    row = lax.broadcasted_iota(jnp.int32, (128, 1), 0)
    tril = _tri(True)
    dt = _softplus(dt_raw + dtb_row)
    acs = _cumsum_rows(dt * a_row)
    acs_t = acs.T
    gmat = [_mm_nt(cs_in[g], bs[g]) for g in range(2)]
    lo = lane < HEAD_DIM
    lo_r = row < HEAD_DIM
    last = (row == SSD_CHUNK - 1).astype(F32)
    ys, hn = [], []
    for p in range(N_PAIR):
        g = p // (N_PAIR // 2)
        col, dtc, mm, clast = [], [], [], []
        for hh in range(2):
            h = 2 * p + hh
            oh = (lane == h).astype(F32)
            c_col = jnp.sum(acs * oh, axis=1, keepdims=True)
            c_row = jnp.sum(acs_t * (row == h).astype(F32), axis=0, keepdims=True)
            col.append(c_col)
            dtc.append(jnp.sum(dt * oh, axis=1, keepdims=True))
            clast.append(jnp.sum(c_col * last, axis=0, keepdims=True))
            mm.append(gmat[g] * jnp.exp(jnp.where(tril, c_col - c_row, NEG)))
        x = xs[p]
        xd = x * jnp.where(lo, dtc[0], dtc[1])
        y = jnp.where(lo, _mm(mm[0], xd), _mm(mm[1], xd))
        y = y + jnp.where(lo, jnp.exp(col[0]), jnp.exp(col[1])) * _mm_nt(cs_in[g], hs[p])
        ys.append(y + ds[p] * x)
        dec = jnp.where(lo, jnp.exp(clast[0] - col[0]), jnp.exp(clast[1] - col[1]))
        hn.append(hs[p] * jnp.where(lo_r, jnp.exp(clast[0]), jnp.exp(clast[1])) + _mm_tn(xd * dec, bs[g]))
    return tuple(ys), tuple(hn)


def _ssd_load(xbc_ref, dt_ref, a_ref, dtb_ref, d_ref):
    xs = tuple(xbc_ref[:, 128 * p:128 * (p + 1)] for p in range(N_PAIR))
    bs = tuple(xbc_ref[:, D + 128 * g:D + 128 * (g + 1)] for g in range(2))
    cs = tuple(xbc_ref[:, D + 256 + 128 * g:D + 256 + 128 * (g + 1)] for g in range(2))
    ds = tuple(d_ref[:, 128 * p:128 * (p + 1)] for p in range(N_PAIR))
    return xs, bs, cs, dt_ref[...], a_ref[...], dtb_ref[...], ds


def _ssd_fwd(name, xbc_c, proj, a_row, dtb_row, d_exp):
    nc = S // SSD_CHUNK
    prow = lambda n: pl.BlockSpec((1, n), lambda c: (0, 0))

    def body(xbc_ref, dt_ref, a_ref, dtb_ref, d_ref, y_ref, st_ref, h_ref):
        @pl.when(pl.program_id(0) == 0)
        def _():
            h_ref[...] = jnp.zeros_like(h_ref)

        xs, bs, cs, dt_raw, a, dtb, ds = _ssd_load(xbc_ref, dt_ref, a_ref, dtb_ref, d_ref)
        hs = tuple(h_ref[p] for p in range(N_PAIR))
        ys, hn = _ssd_chunk(xs, bs, cs, dt_raw, hs, a, dtb, ds)
        for p in range(N_PAIR):
            y_ref[:, 128 * p:128 * (p + 1)] = ys[p]
            st_ref[p] = hs[p]
            h_ref[p] = hn[p]

    return pl.pallas_call(
        body, grid=(nc,),
        in_specs=[pl.BlockSpec((SSD_CHUNK, SSD_XBC), lambda c: (c, 0)),
                  pl.BlockSpec((SSD_CHUNK, DT_PAD), lambda c: (c, OFF_DT // DT_PAD)),
                  prow(128), prow(128), prow(D)],
        out_specs=[pl.BlockSpec((SSD_CHUNK, D), lambda c: (c, 0)),
                   pl.BlockSpec((None, N_PAIR, 128, 128), lambda c: (c, 0, 0, 0))],
        out_shape=[jax.ShapeDtypeStruct((S, D), F32), jax.ShapeDtypeStruct((nc, N_PAIR, 128, 128), F32)],
        scratch_shapes=[pltpu.VMEM((N_PAIR, 128, 128), F32)], name=name,
        compiler_params=_cparams(("arbitrary",)))(xbc_c, proj, a_row, dtb_row, d_exp)


def _ssd_bwd(name, xbc_c, proj, states, dy, a_row, dtb_row, d_exp):
    nc = S // SSD_CHUNK
    prow = lambda n: pl.BlockSpec((1, n), lambda i: (0, 0))
    rc = lambda i: nc - 1 - i

    def body(xbc_ref, dt_ref, st_ref, dy_ref, a_ref, dtb_ref, d_ref, dxbc_ref, ddt_ref, da_ref, ddtb_ref, dd_ref, e_ref):
        i = pl.program_id(0)

        @pl.when(i == 0)
        def _():
            e_ref[...] = jnp.zeros_like(e_ref)
            da_ref[...] = jnp.zeros_like(da_ref)
            ddtb_ref[...] = jnp.zeros_like(ddtb_ref)
            dd_ref[...] = jnp.zeros_like(dd_ref)

        xs, bs, cs, dt_raw, a, dtb, ds = _ssd_load(xbc_ref, dt_ref, a_ref, dtb_ref, d_ref)
        hs = tuple(st_ref[p] for p in range(N_PAIR))
        _, vjp = jax.vjp(_ssd_chunk, xs, bs, cs, dt_raw, hs, a, dtb, ds)
        dys = tuple(dy_ref[:, 128 * p:128 * (p + 1)] for p in range(N_PAIR))
        es = tuple(e_ref[p] for p in range(N_PAIR))
        dxs, dbs, dcs, ddt, dhs, da, ddtb, dds = vjp((dys, es))
        for p in range(N_PAIR):
            dxbc_ref[:, 128 * p:128 * (p + 1)] = dxs[p]
            e_ref[p] = dhs[p]
            dd_ref[:, 128 * p:128 * (p + 1)] += dds[p]
        for g in range(2):
            dxbc_ref[:, D + 128 * g:D + 128 * (g + 1)] = dbs[g]
            dxbc_ref[:, D + 256 + 128 * g:D + 256 + 128 * (g + 1)] = dcs[g]
        ddt_ref[...] = ddt
        da_ref[...] += da
        ddtb_ref[...] += ddtb

    return pl.pallas_call(
        body, grid=(nc,),
        in_specs=[pl.BlockSpec((SSD_CHUNK, SSD_XBC), lambda i: (rc(i), 0)),
                  pl.BlockSpec((SSD_CHUNK, DT_PAD), lambda i: (rc(i), OFF_DT // DT_PAD)),
                  pl.BlockSpec((None, N_PAIR, 128, 128), lambda i: (rc(i), 0, 0, 0)),
                  pl.BlockSpec((SSD_CHUNK, D), lambda i: (rc(i), 0)),
                  prow(128), prow(128), prow(D)],
        out_specs=[pl.BlockSpec((SSD_CHUNK, SSD_XBC), lambda i: (rc(i), 0)),
                   pl.BlockSpec((SSD_CHUNK, DT_PAD), lambda i: (rc(i), 0)),
                   prow(128), prow(128), prow(D)],
        out_shape=[jax.ShapeDtypeStruct((S, SSD_XBC), F32), jax.ShapeDtypeStruct((S, DT_PAD), F32),
                   jax.ShapeDtypeStruct((1, 128), F32), jax.ShapeDtypeStruct((1, 128), F32),
                   jax.ShapeDtypeStruct((1, D), F32)],
        scratch_shapes=[pltpu.VMEM((N_PAIR, 128, 128), F32)], name=name,
        compiler_params=_cparams(("arbitrary",)))(xbc_c, proj, states, dy, a_row, dtb_row, d_exp)


def _exchange(name, arrays, scatter):
    n = len(arrays)
    flips = [(dx, dy, dc) for dx in (0, 1) for dy in (0, 1) for dc in (0, 1) if dx or dy or dc]

    def body(*refs):
        ins, outs = refs[:n], refs[n:2 * n]
        send_sems, recv_sems, loc_sems = refs[2 * n:]
        x, y, c = lax.axis_index("x"), lax.axis_index("y"), lax.axis_index("c")
        me = 4 * x + 2 * y + c
        peers = []
        for dx, dy, dc in flips:
            px, py, pc = (1 - x if dx else x), (1 - y if dy else y), (1 - c if dc else c)
            peers.append(((px, py, pc), 4 * px + 2 * py + pc))

        def remote(k, j, landed_from):
            dev, pid = peers[j]
            src = ins[k].at[pid] if scatter else ins[k]
            return pltpu.make_async_remote_copy(
                src_ref=src, dst_ref=outs[k].at[landed_from], send_sem=send_sems.at[k, j], recv_sem=recv_sems.at[k, j],
                device_id=dev, device_id_type=pl.DeviceIdType.MESH)

        local = [pltpu.make_async_copy(ins[k].at[me] if scatter else ins[k], outs[k].at[me], loc_sems.at[k])
                 for k in range(n)]
        for cp in local:
            cp.start()
        for k in range(n):
            for j in range(len(flips)):
                remote(k, j, me).start()
        for cp in local:
            cp.wait()
        for k in range(n):
            for j in range(len(flips)):
                remote(k, j, me).wait_send()
                remote(k, j, peers[j][1]).wait_recv()

    hbm = pl.BlockSpec(memory_space=pltpu.HBM)
    out_shape = [jax.ShapeDtypeStruct(a.shape if scatter else (N_DEV,) + a.shape, a.dtype) for a in arrays]
    res = pl.pallas_call(
        body, in_specs=[hbm] * n, out_specs=[hbm] * n, out_shape=out_shape, name=name,
        scratch_shapes=[pltpu.SemaphoreType.DMA((n, len(flips))), pltpu.SemaphoreType.DMA((n, len(flips))),
                        pltpu.SemaphoreType.DMA((n,))])(*arrays)
    return list(res)


def _peer_copies(ins, lands, send_sems, recv_sems, loc_sems, scatter):
    n = len(ins)
    flips = [(dx, dy, dc) for dx in (0, 1) for dy in (0, 1) for dc in (0, 1) if dx or dy or dc]
    x, y, c = lax.axis_index("x"), lax.axis_index("y"), lax.axis_index("c")
    me = 4 * x + 2 * y + c
    peers = []
    for dx, dy, dc in flips:
        px, py, pc = (1 - x if dx else x), (1 - y if dy else y), (1 - c if dc else c)
        peers.append(((px, py, pc), 4 * px + 2 * py + pc))

    def remote(k, j, slot):
        dev, pid = peers[j]
        return pltpu.make_async_remote_copy(
            src_ref=ins[k].at[pid] if scatter else ins[k], dst_ref=lands[k].at[slot],
            send_sem=send_sems.at[k * N_FLIP + j], recv_sem=recv_sems.at[k * N_FLIP + j],
            device_id=dev, device_id_type=pl.DeviceIdType.MESH)

    local = [pltpu.make_async_copy(ins[k].at[me] if scatter else ins[k], lands[k].at[me], loc_sems.at[k])
             for k in range(n)]
    pairs = [(k, j) for k in range(n) for j in range(len(flips))]
    sent = lambda k, j: remote(k, j, me)
    landed = lambda k, j: remote(k, j, peers[j][1])
    return local, pairs, sent, landed


_HBM = pl.BlockSpec(memory_space=pltpu.HBM)
_SEM = pl.BlockSpec(memory_space=pltpu.SEMAPHORE)
N_FLIP = N_DEV - 1


def _exchange_start(name, arrays, scatter, after):
    n = len(arrays)
    arrays = [pltpu.with_memory_space_constraint(a, pltpu.HBM) for a in arrays]
    lands = [pltpu.with_memory_space_constraint(
        lax.empty(a.shape if scatter else (N_DEV,) + a.shape, a.dtype), pltpu.HBM) for a in arrays]

    def body(*refs):
        ins, lnd = refs[:n], refs[n:2 * n]
        send_sems, recv_sems, loc_sems = refs[2 * n + 1:2 * n + 4]
        token = refs[-1]
        local, pairs, sent, _ = _peer_copies(ins, lnd, send_sems, recv_sems, loc_sems, scatter)
        for cp in local:
            cp.start()
        for k, j in pairs:
            sent(k, j).start()
        token[...] = jnp.zeros_like(token)

    res = pl.pallas_call(
        body, name=name,
        in_specs=[_HBM] * (2 * n) + [pl.BlockSpec(memory_space=pl.ANY)],
        out_specs=[_SEM, _SEM, _SEM] + [_HBM] * (2 * n) + [pl.BlockSpec(memory_space=pltpu.VMEM)],
        out_shape=[pltpu.SemaphoreType.DMA((n * N_FLIP,)), pltpu.SemaphoreType.DMA((n * N_FLIP,)), pltpu.SemaphoreType.DMA((n,))]
        + [pltpu.HBM(a.shape, a.dtype) for a in arrays] + [pltpu.HBM(a.shape, a.dtype) for a in lands]
        + [jax.ShapeDtypeStruct((8, 128), F32)],
        input_output_aliases={k: 3 + k for k in range(2 * n)},
        compiler_params=pltpu.CompilerParams(has_side_effects=pltpu.SideEffectType.DATAFLOW_SIDE_EFFECTING),
    )(*arrays, *lands, after)
    return (res[:3], res[3:3 + n], res[3 + n:3 + 2 * n], scatter), res[-1]


def _exchange_wait(name, state, after):
    sems, ins_thru, lands_thru, scatter = state
    n = len(ins_thru)

    def body(*refs):
        ins, lnd = refs[:n], refs[n:2 * n]
        send_sems, recv_sems, loc_sems = refs[2 * n:2 * n + 3]
        local, pairs, sent, landed = _peer_copies(ins, lnd, send_sems, recv_sems, loc_sems, scatter)
        for cp in local:
            cp.wait()
        for k, j in pairs:
            sent(k, j).wait_send()
            landed(k, j).wait_recv()

    res = pl.pallas_call(
        body, name=name,
        in_specs=[_HBM] * (2 * n) + [_SEM, _SEM, _SEM] + [pl.BlockSpec(memory_space=pl.ANY)],
        out_specs=[_HBM] * (2 * n),
        out_shape=[pltpu.HBM(a.shape, a.dtype) for a in ins_thru] + [pltpu.HBM(a.shape, a.dtype) for a in lands_thru],
        input_output_aliases={k: k for k in range(2 * n)},
        compiler_params=pltpu.CompilerParams(has_side_effects=pltpu.SideEffectType.DATAFLOW_SIDE_EFFECTING),
    )(*ins_thru, *lands_thru, *sems, after)
    return list(res[n:])


def _adamw_fn(*vals):
    slots, (w, m, v) = vals[:N_DEV], vals[N_DEV:]
    g = slots[0].astype(F32)
    for s in slots[1:]:
        g = g + s.astype(F32)
    m2 = ADAM_B1 * m + (1.0 - ADAM_B1) * g
    v2 = ADAM_B2 * v + (1.0 - ADAM_B2) * (g * g)
    m_hat = m2 / (1.0 - ADAM_B1 ** ADAM_STEP)
    v_hat = v2 / (1.0 - ADAM_B2 ** ADAM_STEP)
    delta = -ADAM_LR * (m_hat / (jnp.sqrt(v_hat) + ADAM_EPS) + ADAM_WD * w)
    return (g, delta, m2, v2), ()


def _adamw(name, slots, w, m, v, first_row=0, prev=None):
    R, C = slots.shape[1:]
    tm = R if R <= 128 else _pick(R, 128 if C > D else 256, 8)
    rows = ([_rows(slots, C, lead=s) for s in range(N_DEV)]
            + [_rows(a, C, roff=first_row // tm) for a in (w, m, v)])
    return _tiled(name, _adamw_fn, (1, R // tm), tm, rows, [], [(w.shape[0], C, F32)] * 4,
                  out_roff=first_row // tm, prev_outs=prev)


def _bucket_onehots():
    out = []
    qi = jnp.arange(ATT_BLOCK)[:, None]
    kj = jnp.arange(ATT_BLOCK)[None, :]
    max_exact = REL_BUCKETS // 2
    for _, dil in ATT_GROUPS:
        parts = []
        for rel in (qi + ATT_BLOCK - kj, qi - kj):
            dist = jnp.clip(rel, 0, None) * dil
            nf = jnp.maximum(dist, 1).astype(F32)
            large = max_exact + (jnp.log(nf / max_exact) / math.log(REL_MAX_DISTANCE / max_exact)
                                 * (REL_BUCKETS - max_exact)).astype(jnp.int32)
            large = jnp.minimum(large, REL_BUCKETS - 1)
            bucket = jnp.where(dist < max_exact, dist, large)
            parts.append((bucket[:, :, None] == jnp.arange(REL_BUCKETS)[None, None, :]).astype(F32))
        out.append(jnp.stack(parts))
    return out


SHARDED = ("w_in", "w_a", "pool_w", "w_b", "ssd_conv_w", "w_c", "w_o", "ffn_w_up", "ffn_conv_w", "ffn_w_down")
MATMUL_WEIGHTS = ("w_in", "w_a", "pool_w", "w_b", "w_c", "w_o", "ffn_w_up", "ffn_w_down")
ROW_SHARDED = ("w_b", "w_c", "w_o", "ffn_w_down")
W_IN_SEGS = ((0, 1152, OFF_Q), (1152, 2304, OFF_K), (2304, 3456, OFF_V), (3456, 4480, OFF_POOL), (4480, 5504, OFF_Z),
             (5504, 7040, OFF_XBC), (7040, 7056, OFF_DT), (7056, IN_WIDTH, OFF_GATE))
COL_SHARDED = {
    "w_in": (IN_WIDTH // N_DEV, W_IN_SEGS, ((OFF_DT + SSD_HEADS, OFF_Q),), NP),
    "w_a": (D // N_DEV, ((0, D, 0),), (), D),
    "ffn_w_up": (2 * D_FF // N_DEV, ((0, 2 * D_FF, 0),), (), 2 * D_FF),
    "ssd_conv_w": (SSD_XBC // N_DEV, ((0, SSD_XBC, 0),), (), SSD_XBC),
    "ffn_conv_w": (2 * D_FF // N_DEV, ((0, 2 * D_FF, 0),), (), 2 * D_FF),
}
REPLICATED = ("rel_bias", "ln1_g", "b_gate", "pool_scale", "ssd_conv_b", "ssd_dt_bias", "ssd_a_log", "ssd_d",
              "ssd_norm_w", "ln2_g", "ffn_conv_b", "final_g")
WEIGHTS = ("rel_bias", "ln1_g", "w_in", "b_gate", "w_a", "pool_w", "pool_scale", "w_b", "ssd_conv_w", "ssd_conv_b",
           "ssd_dt_bias", "ssd_a_log", "ssd_d", "ssd_norm_w", "w_c", "w_o", "ln2_g", "ffn_w_up", "ffn_conv_w",
           "ffn_conv_b", "ffn_w_down", "final_g")


def _local_weight(name, n, blocks):
    if n in COL_SHARDED:
        c, segs, zeros, width = COL_SHARDED[n]
        return _col_assemble(name, blocks, _seg_copies(segs, c), zeros, width)
    if n in ROW_SHARDED:
        return blocks.reshape(-1, blocks.shape[-1])
    return blocks


def _device_blocks(name, n, g):
    if n in COL_SHARDED:
        c, segs, _, _ = COL_SHARDED[n]
        return _col_split(name, g, _seg_copies(segs, c), c, BF16)
    if n in ROW_SHARDED:
        return g.reshape(N_DEV, g.shape[0] // N_DEV, g.shape[1]).astype(BF16)
    return g.astype(BF16)


def _row(v, n=None):
    v = v.reshape(1, -1)
    if n is not None and v.shape[1] < n:
        v = jnp.pad(v, ((0, 0), (0, n - v.shape[1])))
    return v


RT = 256


def _row_call(name, fn, cw, ncol, rows, params, outs, accs=()):
    return _tiled(name, fn, (ncol, S // RT), RT, rows, params, [(S, cw, dt) for dt in outs], accs)


def _col_call(name, fn, tc, ncol, rows, params, outs, accs=()):
    return _tiled(name, fn, (ncol, 1), S, rows, params, [(S, tc, dt) for dt in outs], accs)


def _fwd_only(fn):
    return lambda *a: (fn(*a), ())


def _layer_fwd(i, x, W, P, bias_tabs, late=None):
    sv = {"x": x}
    (u,) = _row_call(f"ln1_f{i}", _fwd_only(_rmsnorm_fn), D, 1, [_rows(x, D)], [_p_row(P["ln1_g"], D)], [BF16])
    proj = _matmul(f"inproj_f{i}", u, W["w_in"], "nn")
    sv["u"], sv["proj"] = u, proj

    os_, ls_ = [], []
    for gi in range(len(ATT_GROUPS)):
        o, lse = _att_fwd(f"att_f{i}_{gi}", proj, gi, bias_tabs[gi][0], bias_tabs[gi][1])
        os_.append(o)
        ls_.append(lse)
    sv["att_o"], sv["att_l"] = os_, ls_
    (att,) = _row_call(f"attmerge_f{i}", _fwd_only(_att_merge_fn), 128, 3,
                       [_rows(t, 128) for t in os_ + ls_], [], [BF16])
    if late is not None:
        W2, P2 = late(att)
        W.update(W2)
        P.update(P2)
    y_a = _matmul(f"wa_f{i}", att, W["w_a"], "nn")
    sv["att"], sv["y_a"] = att, y_a

    pool_params = [(W["pool_w"], (N_DEV, None, 32, 256), lambda j, i_: (0, j, 0, 0)), _p_row(P["pool_scale"], 256)]
    (yb_pre,) = _col_call(f"pool_f{i}", _fwd_only(_pool_fn), 256, 4, [_rows(proj, 256, OFF_POOL // 256)],
                          pool_params, [BF16])
    y_b = _matmul(f"wb_f{i}", yb_pre, W["w_b"], "nn")
    sv["yb_pre"], sv["y_b"] = yb_pre, y_b

    conv_params = [_p_row(P["ssd_conv_w"][k], 128) for k in range(4)] + [_p_row(P["ssd_conv_b"], 128)]
    (xbc_c,) = _col_call(f"ssdconv_f{i}", _fwd_only(_ssd_conv_fn), 128, SSD_XBC // 128,
                         [_rows(proj, 128, OFF_XBC // 128)], conv_params, [F32])
    y_ssd, states = _ssd_fwd(f"ssd_f{i}", xbc_c, proj, P["a_row"], P["dtb_row"], P["d_exp"])
    (yc_pre,) = _row_call(f"ssdnorm_f{i}", _fwd_only(_gated_norm_fn), 512, 2,
                          [_rows(y_ssd, 512), _rows(proj, 512, OFF_Z // 512)], [_p_row(P["ssd_norm_w"], 512)], [BF16])
    y_c = _matmul(f"wc_f{i}", yc_pre, W["w_c"], "nn")
    sv["xbc_c"], sv["states"], sv["y_ssd"], sv["yc_pre"], sv["y_c"] = xbc_c, states, y_ssd, yc_pre, y_c

    gate_rows = [_rows(proj, D, k) for k in range(3)] + [_rows(t, D) for t in (y_a, y_b, y_c)]
    gate_params = [_p_row(P["b_gate"], D, k) for k in range(3)]
    (merged,) = _row_call(f"gate_f{i}", _fwd_only(_gate_merge_fn), D, 1, gate_rows, gate_params, [BF16])
    x1 = _matmul(f"wo_f{i}", merged, W["w_o"], "nn", add=x)
    sv["merged"], sv["x1"] = merged, x1

    (u2,) = _row_call(f"ln2_f{i}", _fwd_only(_rmsnorm_fn), D, 1, [_rows(x1, D)], [_p_row(P["ln2_g"], D)], [BF16])
    up = _matmul(f"up_f{i}", u2, W["ffn_w_up"], "nn")
    (act,) = _col_call(f"ffnact_f{i}", _fwd_only(_ffn_act_fn), 128, D_FF // 128,
                       [_rows(up, 128), _rows(up, 128, D_FF // 128)], _ffn_params(P), [BF16])
    x2 = _matmul(f"down_f{i}", act, W["ffn_w_down"], "nn", add=x1)
    sv["u2"], sv["up"], sv["act"] = u2, up, act
    return x2, sv


def _ffn_params(P):
    nb = D_FF // 128
    return ([_p_row(P["ffn_conv_w"][k], 128) for k in range(3)] + [_p_row(P["ffn_conv_b"], 128)]
            + [_p_row(P["ffn_conv_w"][k], 128, nb) for k in range(3)] + [_p_row(P["ffn_conv_b"], 128, nb)])


def _layer_bwd(i, dx2, sv, W, P, bias_tabs, onehots, on_sharded_grads):
    G = {}
    x, proj, x1 = sv["x"], sv["proj"], sv["x1"]

    dact = _matmul(f"down_bx{i}", dx2, W["ffn_w_down"], "nt", out_dtype=BF16)
    G["ffn_w_down"] = _matmul(f"down_bw{i}", sv["act"], dx2, "tn", out_dtype=BF16)
    nb = D_FF // 128
    up = sv["up"]
    f = _with_vjp(_ffn_act_fn, 10, (0, 1), tuple(range(2, 10)))
    accs = [_a_row(D_FF, 128)] * 8
    dua, duv, a0, a1, a2, ab, v0, v1, v2, vb = _col_call(
        f"ffnact_b{i}", f, 128, nb, [_rows(up, 128), _rows(up, 128, nb)], _ffn_params(P) + [_rows_as_param(dact, 128)],
        [BF16, BF16], accs)
    G["ffn_conv_w"] = jnp.concatenate([jnp.concatenate([a0, a1, a2], 0), jnp.concatenate([v0, v1, v2], 0)], axis=1)
    G["ffn_conv_b"] = jnp.concatenate([ab, vb], axis=1)[0]
    dup = jnp.concatenate([dua, duv], axis=1)
    du2 = _matmul(f"up_bx{i}", dup, W["ffn_w_up"], "nt")
    G["ffn_w_up"] = _matmul(f"up_bw{i}", sv["u2"], dup, "tn", out_dtype=BF16)

    def norm_bwd(x_, g_, du_, dres):
        (dxn,), (dg,) = _with_vjp(_rmsnorm_fn, 2, (0,), (1,))(x_, g_, du_)
        return (dxn + dres,), (dg,)

    (dx1,), (G["ln2_g"],) = _split_res(_row_call(
        f"ln2_b{i}", lambda x_, du_, dres, g_: norm_bwd(x_, g_, du_, dres), D, 1,
        [_rows(x1, D), _rows(du2, D), _rows(dx2, D)], [_p_row(P["ln2_g"], D)], [F32], [_a_row(D, D)]), 1)

    dmerged = _matmul(f"wo_bx{i}", dx1, W["w_o"], "nt")
    G["w_o"] = _matmul(f"wo_bw{i}", sv["merged"], dx1, "tn", out_dtype=BF16)
    gate_rows = [_rows(proj, D, k) for k in range(3)] + [_rows(sv[t], D) for t in ("y_a", "y_b", "y_c")]
    gate_params = [_p_row(P["b_gate"], D, k) for k in range(3)]

    def gate_bwd(g0, g1, g2, ya, yb, yc, dm, b0, b1, b2):
        return _with_vjp(_gate_merge_fn, 9, (0, 1, 2, 3, 4, 5), (6, 7, 8))(g0, g1, g2, ya, yb, yc, b0, b1, b2, dm)

    dg0, dg1, dg2, dya, dyb, dyc, db0, db1, db2 = _row_call(
        f"gate_b{i}", gate_bwd, D, 1, gate_rows + [_rows(dmerged, D)], gate_params,
        [BF16, BF16, BF16, BF16, BF16, BF16], [_a_row(D, D)] * 3)
    G["b_gate"] = jnp.concatenate([db0, db1, db2], axis=1)[0]

    dyc_pre = _matmul(f"wc_bx{i}", dyc, W["w_c"], "nt")
    G["w_c"] = _matmul(f"wc_bw{i}", sv["yc_pre"], dyc, "tn", out_dtype=BF16)

    def gnorm_bwd(y_, z_, dy_, w_):
        return _with_vjp(_gated_norm_fn, 3, (0, 1), (2,))(y_, z_, w_, dy_)

    dy_ssd, dz, dnw = _row_call(
        f"ssdnorm_b{i}", gnorm_bwd, 512, 2,
        [_rows(sv["y_ssd"], 512), _rows(proj, 512, OFF_Z // 512), _rows(dyc_pre, 512)],
        [_p_row(P["ssd_norm_w"], 512)], [F32, BF16], [_a_row(D, 512)])
    G["ssd_norm_w"] = dnw[0]
    dxbc_c, ddt, da_row, ddtb_row, dd_exp = _ssd_bwd(f"ssd_b{i}", sv["xbc_c"], proj, sv["states"], dy_ssd,
                                                     P["a_row"], P["dtb_row"], P["d_exp"])
    a_vec = P["a_row"][0, :SSD_HEADS]
    G["ssd_a_log"] = da_row[0, :SSD_HEADS] * a_vec
    G["ssd_dt_bias"] = ddtb_row[0, :SSD_HEADS]
    G["ssd_d"] = dd_exp.reshape(SSD_HEADS, HEAD_DIM).sum(axis=1)
    conv_params = [_p_row(P["ssd_conv_w"][k], 128) for k in range(4)] + [_p_row(P["ssd_conv_b"], 128)]

    def conv_bwd(x_, dy_, w0, w1, w2, w3, b_):
        return _with_vjp(_ssd_conv_fn, 6, (0,), (1, 2, 3, 4, 5))(x_, w0, w1, w2, w3, b_, dy_)

    dxbc, c0, c1, c2, c3, cb = _col_call(
        f"ssdconv_b{i}", conv_bwd, 128, SSD_XBC // 128, [_rows(proj, 128, OFF_XBC // 128), _rows(dxbc_c, 128)],
        conv_params, [BF16], [_a_row(SSD_XBC, 128)] * 5)
    G["ssd_conv_w"] = jnp.concatenate([c0, c1, c2, c3], axis=0)
    G["ssd_conv_b"] = cb[0]

    dyb_pre = _matmul(f"wb_bx{i}", dyb, W["w_b"], "nt")
    G["w_b"] = _matmul(f"wb_bw{i}", sv["yb_pre"], dyb, "tn", out_dtype=BF16)
    pool_params = [(W["pool_w"], (N_DEV, None, 32, 256), lambda j, i_: (0, j, 0, 0)), _p_row(P["pool_scale"], 256)]

    def pool_bwd(x_, dy_, wg, sc):
        return _with_vjp(_pool_fn, 3, (0,), (1, 2))(x_, wg.astype(F32), sc, dy_)

    dpool, dwg, dsc = _col_call(
        f"pool_b{i}", pool_bwd, 256, 4, [_rows(proj, 256, OFF_POOL // 256), _rows(dyb_pre, 256)], pool_params, [BF16],
        [((N_DEV, 4, 32, 256), (N_DEV, None, 32, 256), lambda j, i_: (0, j, 0, 0)), _a_row(D, 256)])
    G["pool_w"] = dwg
    G["pool_scale"] = dsc[0]

    datt = _matmul(f"wa_bx{i}", dya, W["w_a"], "nt")
    G["w_a"] = _matmul(f"wa_bw{i}", sv["att"], dya, "tn", out_dtype=BF16)

    def merge_bwd(o0, o1, o2, l0, l1, l2, da_):
        return _with_vjp(_att_merge_fn, 6, (0, 1, 2, 3, 4, 5), ())(o0, o1, o2, l0, l1, l2, da_)

    dol = _row_call(f"attmerge_b{i}", merge_bwd, 128, 3,
                    [_rows(t, 128) for t in sv["att_o"] + sv["att_l"]] + [_rows(datt, 128)], [], [F32] * 6)
    dqs, dks, dvs = [], [], []
    g_rel = jnp.zeros((REL_BUCKETS, 18), F32)
    for gi in range(len(ATT_GROUPS)):
        dq, dk, dv, gbp, gbc = _att_bwd(f"att_b{i}_{gi}", proj, gi, bias_tabs[gi][0], bias_tabs[gi][1],
                                        dol[gi], dol[3 + gi])
        dqs.append(dq)
        dks.append(dk)
        dvs.append(dv)
        oh = onehots[gi]
        gt = (jnp.einsum("hqk,qkb->bh", gbp, oh[0], precision=lax.Precision.HIGHEST)
              + jnp.einsum("hqk,qkb->bh", gbc, oh[1], precision=lax.Precision.HIGHEST))
        g_rel = g_rel.at[:, gi * 6:(gi + 1) * 6].add(gt)
    G["rel_bias"] = g_rel

    dproj = jnp.concatenate(
        [dg0, dg1, dg2, dpool, dz, dxbc, ddt.astype(BF16), jnp.zeros((S, OFF_Q - OFF_DT - DT_PAD), BF16)]
        + [t.astype(BF16) for t in dqs + dks + dvs], axis=1)
    du = _matmul(f"inproj_bx{i}", dproj, W["w_in"], "nt")
    G["w_in"] = _matmul(f"inproj_bw{i}", sv["u"], dproj, "tn", out_dtype=BF16)
    ln1_g = P["ln1_g"] + on_sharded_grads(G)
    (dx,), (G["ln1_g"],) = _split_res(_row_call(
        f"ln1_b{i}", lambda x_, du_, dres, g_: norm_bwd(x_, g_, du_, dres), D, 1,
        [_rows(x, D), _rows(du, D), _rows(dx1, D)], [_p_row(ln1_g, D)], [F32], [_a_row(D, D)]), 1)
    G["ln1_g"] = G["ln1_g"][0]
    G["ln2_g"] = G["ln2_g"][0]
    return dx, G


def _rows_as_param(arr, cw):
    return (arr, (arr.shape[0], cw), lambda j, i: (0, j))


def _split_res(res, n_out):
    return tuple(res[:n_out]), tuple(res[n_out:])


def kernel(x, rel_bias, ln1_g, w_in, b_gate, w_a, pool_w, pool_scale, w_b, ssd_conv_w, ssd_conv_b, ssd_dt_bias, ssd_a_log, ssd_d, ssd_norm_w, w_c, w_o, ln2_g, ffn_w_up, ffn_conv_w, ffn_conv_b, ffn_w_down, final_g, loss_target, m_rel_bias, m_ln1_g, m_w_in, m_b_gate, m_w_a, m_pool_w, m_pool_scale, m_w_b, m_ssd_conv_w, m_ssd_conv_b, m_ssd_dt_bias, m_ssd_a_log, m_ssd_d, m_ssd_norm_w, m_w_c, m_w_o, m_ln2_g, m_ffn_w_up, m_ffn_conv_w, m_ffn_conv_b, m_ffn_w_down, m_final_g, v_rel_bias, v_ln1_g, v_w_in, v_b_gate, v_w_a, v_pool_w, v_pool_scale, v_w_b, v_ssd_conv_w, v_ssd_conv_b, v_ssd_dt_bias, v_ssd_a_log, v_ssd_d, v_ssd_norm_w, v_w_c, v_w_o, v_ln2_g, v_ffn_w_up, v_ffn_conv_w, v_ffn_conv_b, v_ffn_w_down, v_final_g):
    args = locals()
    wts = {n: args[n] for n in WEIGHTS}
    mom = {n: args["m_" + n] for n in WEIGHTS}
    var = {n: args["v_" + n] for n in WEIGHTS}
    names = list(SHARDED)

    onehots = _bucket_onehots()
    bias_tabs = []
    for gi in range(3):
        tab = rel_bias[:, gi * 6:(gi + 1) * 6]
        b = jnp.einsum("pqkb,bh->phqk", onehots[gi], tab, precision=lax.Precision.HIGHEST)
        bias_tabs.append((b[0], b[1]))

    def gather_start(tag, i, which, after):
        shards = [wts[n][i].astype(BF16) if n in MATMUL_WEIGHTS else wts[n][i] for n in which]
        return _exchange_start(f"gather_start{tag}", shards, False, after)

    def layer_params(i, which, landed):
        full = {n: _local_weight(f"local_{n}{i}", n, g) for n, g in zip(which, landed)}
        W = {n: full[n] for n in which if n in MATMUL_WEIGHTS}
        P = {}
        if "ssd_conv_w" in full:
            P["ssd_conv_w"] = [_row(full["ssd_conv_w"][k]) for k in range(4)]
            P["ffn_conv_w"] = [_row(full["ffn_conv_w"][k]) for k in range(3)]
        return W, P

    def replicated_params(i):
        return {"ln1_g": _row(ln1_g[i]), "ln2_g": _row(ln2_g[i]), "b_gate": _row(b_gate[i]),
                "pool_scale": _row(pool_scale[i]), "ssd_conv_b": _row(ssd_conv_b[i]),
                "ssd_norm_w": _row(ssd_norm_w[i]), "ffn_conv_b": _row(ffn_conv_b[i]),
                "a_row": _row(-jnp.exp(ssd_a_log[i]), 128), "dtb_row": _row(ssd_dt_bias[i], 128),
                "d_exp": _row(jnp.repeat(ssd_d[i], HEAD_DIM))}

    h = x.reshape(S, D)
    saved, Ws, Ps = [], [], []
    first, rest = ["w_in"], [n for n in names if n != "w_in"]
    state, _ = gather_start("0a", 0, first, h)
    landed_first = _exchange_wait("gather_wait0a", state, h)
    state_rest, token = gather_start("0b", 0, rest, landed_first[0])
    nxt = {}

    def late0(att):
        landed_rest = _exchange_wait("gather_wait0b", state_rest, att)
        W2, P2 = layer_params(0, rest, landed_rest)
        nxt["state"], tok = gather_start("1", 1, names, landed_rest[0])
        P2["pool_scale"] = _row(pool_scale[0]) + tok[0, 0]
        return W2, P2

    for i in range(DEPTH):
        P = replicated_params(i)
        if i == 0:
            W, P1 = layer_params(0, first, landed_first)
        else:
            W, P1 = layer_params(i, names, landed)
            if i + 1 < DEPTH:
                nxt["state"], token = gather_start(str(i + 1), i + 1, names, landed[0])
        P.update(P1)
        if i + 1 < DEPTH:
            P["ln1_g"] = P["ln1_g"] + token[0, 0]
        h, sv = _layer_fwd(i, h, W, P, bias_tabs, late0 if i == 0 else None)
        Ws.append(W)
        Ps.append(dict(P, ln1_g=_row(ln1_g[i]), pool_scale=_row(pool_scale[i])))
        saved.append(sv)
        if i + 1 < DEPTH:
            landed = _exchange_wait(f"gather_wait{i + 1}", nxt["state"], h)

    def loss_bwd(x_, t_, g_):
        lval, vjp = jax.vjp(_loss_fn, x_, t_, g_)
        dx_, _, dg_ = vjp(jnp.ones_like(lval))
        return (dx_,), (dg_, jnp.broadcast_to(lval, (1, 128)))

    dh, g_final, loss_part = _row_call("loss", loss_bwd, D, 1, [_rows(h, D), _rows(loss_target.reshape(S, D), D)],
                                       [_p_row(_row(final_g), D)], [F32], [_a_row(D, D), _a_row(128, 128)])
    loss = lax.psum(loss_part[0, 0], MESH_AXES)

    grads = {n: [None] * DEPTH for n in WEIGHTS if n not in ("rel_bias", "final_g")}
    g_rel = jnp.zeros((REL_BUCKETS, 18), F32)
    slots = [None] * DEPTH
    pending = None
    for i in reversed(range(DEPTH)):
        started = {}

        def on_sharded_grads(G, i=i, started=started):
            parts = [_device_blocks(f"blocks_{n}{i}", n, G[n]) for n in names]
            started["state"], token = _exchange_start(f"scatter_start{i}", parts, True, G["b_gate"])
            return token[0, 0]

        dh, G = _layer_bwd(i, dh, saved[i], Ws[i], Ps[i], bias_tabs, onehots, on_sharded_grads)
        if pending is not None:
            j, st = pending
            slots[j] = _exchange_wait(f"scatter_wait{j}", st, dh)
        pending = (i, started["state"])
        g_rel = g_rel + G.pop("rel_bias")
        for n, g in G.items():
            grads[n][i] = g
    grad_x = dh.reshape(1, S, D)
    local = {n: jnp.stack(grads[n]) for n in grads if n not in SHARDED}
    local["rel_bias"] = g_rel
    local["final_g"] = g_final[0]
    out = {}

    def pack(d):
        flat = jnp.concatenate([d[n].reshape(-1).astype(F32) for n in REPLICATED])
        rows = -(-flat.shape[0] // (8 * 128)) * 8
        return jnp.pad(flat, (0, rows * 128 - flat.shape[0])).reshape(rows, 128)

    (rep_slots,) = _exchange("gather_small_grads", [pack(local)], scatter=False)
    rep = _adamw("adamw_small", rep_slots, pack(wts), pack(mom), pack(var))
    off = 0
    for n in REPLICATED:
        sz = int(np.prod(wts[n].shape))
        out[n] = [t.reshape(-1)[off:off + sz].reshape(wts[n].shape) for t in rep]
        off += sz

    def flat2(n):
        shp = wts[n].shape
        r, c = int(np.prod(shp[:-1])), shp[-1]
        return r, c, wts[n].reshape(r, c), mom[n].reshape(r, c), var[n].reshape(r, c)

    chain = {}
    done = rep[0][0, 0]
    for k, n in enumerate(names):
        if n in MATMUL_WEIGHTS:
            r, c, w2, m2, v2 = flat2(n)
            res = None
            for i in (3, 2, 1):
                res = _adamw(f"adamw_{n}{i}", slots[i][k].reshape(N_DEV, r // DEPTH, c), w2, m2, v2,
                             first_row=i * (r // DEPTH), prev=res)
            chain[n] = res
            done = done + res[0][-1, 0]
    slots[0] = _exchange_wait("scatter_wait0", pending[1], done.reshape(1, 1))
    for k, n in enumerate(names):
        r, c, w2, m2, v2 = flat2(n)
        if n in MATMUL_WEIGHTS:
            res = _adamw(f"adamw_{n}0", slots[0][k].reshape(N_DEV, r // DEPTH, c), w2, m2, v2, first_row=0, prev=chain[n])
        else:
            stacked = jnp.stack([slots[i][k] for i in range(DEPTH)], axis=1)
            res = _adamw("adamw_" + n, stacked.reshape(N_DEV, r, c), w2, m2, v2)
        out[n] = [t.reshape(wts[n].shape) for t in res]

    return (loss, grad_x, *[out[n][0] for n in WEIGHTS], *[out[n][1] for n in WEIGHTS],
            *[out[n][2] for n in WEIGHTS], *[out[n][3] for n in WEIGHTS])
```

```python
import functools
import math

import numpy as np
import jax
import jax.numpy as jnp
from jax import lax
from jax.experimental import pallas as pl
from jax.experimental.pallas import tpu as pltpu

F32 = jnp.float32
BF16 = jnp.bfloat16

N_DEV = 8
MESH_AXES = ("x", "y", "c")
S = 4096
D = 1024
DEPTH = 4
HEAD_DIM = 64
ATT_W = 1152
ATT_GW = 384
ATT_GROUPS = ((128, 1), (512, 4), (2048, 16))
ATT_BLOCK = 128
REL_BUCKETS = 32
REL_MAX_DISTANCE = 2048
POOL_WINDOWS = (2, 4, 8, 16)
SSD_HEADS = 16
SSD_CHUNK = 128
SSD_XBC = 1536
D_FF = 2816
IN_WIDTH = 10128
EPS = 1e-6
NEG = -1e30

OFF_GATE, OFF_POOL, OFF_Z, OFF_XBC, OFF_DT, OFF_Q, OFF_K, OFF_V = 0, 3072, 4096, 5120, 6656, 6912, 8064, 9216
NP = 10368
DT_PAD = 128

ADAM_LR, ADAM_B1, ADAM_B2, ADAM_EPS, ADAM_WD, ADAM_STEP = 0.001, 0.9, 0.999, 1e-08, 0.01, 10

VMEM_LIMIT = 52 * 1024 * 1024


def _cparams(sem=None):
    return pltpu.CompilerParams(dimension_semantics=sem, vmem_limit_bytes=VMEM_LIMIT)


def _dot(a, b, ca, cb):
    return lax.dot_general(a.astype(BF16), b.astype(BF16), (((ca,), (cb,)), ((), ())), preferred_element_type=F32)


@jax.custom_vjp
def _mm(a, b):
    return _dot(a, b, 1, 0)


def _mm_fwd(a, b):
    return _mm(a, b), (a, b)


def _mm_bwd(res, g):
    a, b = res
    return _dot(g, b, 1, 1).astype(a.dtype), _dot(a, g, 0, 0).astype(b.dtype)


_mm.defvjp(_mm_fwd, _mm_bwd)


@jax.custom_vjp
def _mm_nt(a, b):
    return _dot(a, b, 1, 1)


def _mm_nt_fwd(a, b):
    return _mm_nt(a, b), (a, b)


def _mm_nt_bwd(res, g):
    a, b = res
    return _dot(g, b, 1, 0).astype(a.dtype), _dot(g, a, 0, 0).astype(b.dtype)


_mm_nt.defvjp(_mm_nt_fwd, _mm_nt_bwd)


@jax.custom_vjp
def _mm_tn(a, b):
    return _dot(a, b, 0, 0)


def _mm_tn_fwd(a, b):
    return _mm_tn(a, b), (a, b)


def _mm_tn_bwd(res, g):
    a, b = res
    return _dot(b, g, 1, 1).astype(a.dtype), _dot(a, g, 1, 0).astype(b.dtype)


_mm_tn.defvjp(_mm_tn_fwd, _mm_tn_bwd)


def _shift_impl(x, j):
    n = x.shape[0]
    if j == 0:
        return x
    r = pltpu.roll(x, j % n, axis=0)
    t = lax.broadcasted_iota(jnp.int32, x.shape, 0)
    mask = (t >= j) if j > 0 else (t < n + j)
    return jnp.where(mask, r, 0.0)


@functools.partial(jax.custom_vjp, nondiff_argnums=(1,))
def _shift(x, j):
    return _shift_impl(x, j)


_shift.defvjp(lambda x, j: (_shift_impl(x, j), None), lambda j, _, g: (_shift_impl(g, -j),))


def _tri(lower):
    r = lax.broadcasted_iota(jnp.int32, (SSD_CHUNK, SSD_CHUNK), 0)
    c = lax.broadcasted_iota(jnp.int32, (SSD_CHUNK, SSD_CHUNK), 1)
    return (r >= c) if lower else (r <= c)


def _dot_hi(a, b):
    return lax.dot_general(a, b, (((1,), (0,)), ((), ())), precision=lax.Precision.HIGHEST,
                           preferred_element_type=F32)


@jax.custom_vjp
def _cumsum_rows(a):
    return _dot_hi(_tri(True).astype(F32), a)


_cumsum_rows.defvjp(lambda a: (_cumsum_rows(a), None), lambda _, g: (_dot_hi(_tri(False).astype(F32), g),))


@jax.custom_vjp
def _softplus(x):
    return jnp.maximum(x, 0.0) + jnp.log(1.0 + jnp.exp(-jnp.abs(x)))


_softplus.defvjp(lambda x: (_softplus(x), x), lambda x, g: (g * jax.nn.sigmoid(x),))


def _silu(x):
    return x * jax.nn.sigmoid(x)


def _rows(arr, cw, off=0, lead=None, roff=0):
    return (arr, cw, off, lead, roff)


def _tiled(name, fn, grid, tm, rows, params, outs, accs=(), out_roff=0, prev_outs=None, into=None):
    into = into or {}
    ncol, nrow = grid
    in_specs, operands = [], []
    for arr, cw, off, lead, roff in rows:
        if lead is None:
            in_specs.append(pl.BlockSpec((tm, cw), functools.partial(lambda j, i, off, roff: (roff + i, off + j),
                                                                     off=off, roff=roff)))
        else:
            in_specs.append(pl.BlockSpec((None, tm, cw), functools.partial(
                lambda j, i, off, lead, roff: (lead, roff + i, off + j), off=off, lead=lead, roff=roff)))
        operands.append(arr)
    for arr, bs, im in params:
        in_specs.append(pl.BlockSpec(bs, im))
        operands.append(arr)
    out_specs, out_shape = [], []
    for k, (n_rows, cw, dt) in enumerate(outs):
        _, coff, total = into.get(k, (None, 0, ncol * cw))
        out_specs.append(pl.BlockSpec((tm, cw), functools.partial(lambda j, i, r, c: (r + i, c + j), r=out_roff, c=coff)))
        out_shape.append(jax.ShapeDtypeStruct((n_rows, total), dt))
    for shape, bs, im in accs:
        out_specs.append(pl.BlockSpec(bs, im))
        out_shape.append(jax.ShapeDtypeStruct(shape, F32))
    n_in, n_out = len(operands), len(outs)
    aliases = {}
    earlier = dict(enumerate(prev_outs)) if prev_outs is not None else {}
    earlier.update({k: v[0] for k, v in into.items() if v[0] is not None})
    for k, p in sorted(earlier.items()):
        aliases[len(operands)] = k
        in_specs.append(pl.BlockSpec(memory_space=pl.ANY))
        operands.append(p)

    n_all = len(operands)

    def body(*refs):
        vals = [r[...] for r in refs[:n_in]]
        o_vals, a_vals = fn(*vals)
        for r, v in zip(refs[n_all:n_all + n_out], o_vals):
            r[...] = v.astype(r.dtype)
        i = pl.program_id(1)
        for r, v in zip(refs[n_all + n_out:], a_vals):
            @pl.when(i == 0)
            def _(r=r, v=v):
                r[...] = v.astype(r.dtype)

            @pl.when(i > 0)
            def _(r=r, v=v):
                r[...] += v.astype(r.dtype)

    res = pl.pallas_call(body, grid=grid, in_specs=in_specs, out_specs=out_specs, out_shape=out_shape, name=name,
                         input_output_aliases=aliases, compiler_params=_cparams(("arbitrary", "arbitrary")))(*operands)
    return list(res)


def _with_vjp(fn, n_prim, want_out, want_acc):
    def f(*args):
        prim, g = args[:n_prim], args[n_prim:]
        outs, vjp = jax.vjp(lambda *a: fn(*a), *prim)
        d = vjp(tuple(gi.astype(o.dtype) for gi, o in zip(g, outs)))
        return tuple(d[k] for k in want_out), tuple(d[k] for k in want_acc)
    return f


def _p_row(arr, cw, off=0):
    return (arr, (1, cw), functools.partial(lambda j, i, off: (0, off + j), off=off))


def _a_row(n, cw):
    return ((1, n), (1, cw), lambda j, i: (0, j))


def _pick(n, cap, mult):
    best = None
    for t in range(mult, min(n, cap) + 1, mult):
        if n % t == 0:
            best = t
    return best if best is not None else n


def _matmul(name, a, b, mode, add=None, out_dtype=F32):
    if mode == "nn":
        (M, K), N = a.shape, b.shape[1]
    elif mode == "nt":
        (M, K), N = a.shape, b.shape[0]
    else:
        (K, M), N = a.shape, b.shape[1]
    tn = _pick(N, 1536, 128)
    k_cap = 2048 if mode == "tn" else 3456
    tk = K if K <= k_cap else _pick(K, k_cap, 128)
    nk = K // tk
    tm = _pick(M, 1408, 128) if mode == "tn" else _pick(M, 1024 if nk > 1 else 512, 8)
    a_bytes, b_bytes = a.size * a.dtype.itemsize, b.size * b.dtype.itemsize
    swap = nk == 1 and a_bytes * (N // tn) + b_bytes < b_bytes * (M // tm) + a_bytes
    ij = (lambda g0, g1: (g1, g0)) if swap else (lambda g0, g1: (g0, g1))

    def spec(block, index):
        return pl.BlockSpec(block, lambda g0, g1, k: index(*ij(g0, g1), k))

    if mode == "nn":
        a_spec = spec((tm, tk), lambda i, j, k: (i, k))
        b_spec = spec((tk, tn), lambda i, j, k: (k, j))
        ca, cb = 1, 0
    elif mode == "nt":
        a_spec = spec((tm, tk), lambda i, j, k: (i, k))
        b_spec = spec((tn, tk), lambda i, j, k: (j, k))
        ca, cb = 1, 1
    else:
        a_spec = spec((tk, tm), lambda i, j, k: (k, i))
        b_spec = spec((tk, tn), lambda i, j, k: (k, j))
        ca, cb = 0, 0
    in_specs, operands = [a_spec, b_spec], [a, b]
    if add is not None:
        in_specs.append(spec((tm, tn), lambda i, j, k: (i, j)))
        operands.append(add)

    def finish(r, refs, o_ref):
        if add is not None:
            r = r + refs[2][...]
        o_ref[...] = r.astype(o_ref.dtype)

    def body_single(*refs):
        finish(_dot(refs[0][...], refs[1][...], ca, cb), refs, refs[-1])

    def body_multi(*refs):
        o_ref, acc_ref = refs[-2], refs[-1]
        k = pl.program_id(2)
        d = _dot(refs[0][...], refs[1][...], ca, cb)

        @pl.when(k == 0)
        def _():
            acc_ref[...] = d

        @pl.when(jnp.logical_and(k > 0, k < nk - 1))
        def _():
            acc_ref[...] += d

        @pl.when(k == nk - 1)
        def _():
            finish(acc_ref[...] + d, refs, o_ref)

    grid = (N // tn, M // tm, nk) if swap else (M // tm, N // tn, nk)
    return pl.pallas_call(
        body_single if nk == 1 else body_multi, grid=grid, in_specs=in_specs,
        out_specs=spec((tm, tn), lambda i, j, k: (i, j)),
        out_shape=jax.ShapeDtypeStruct((M, N), out_dtype),
        scratch_shapes=[] if nk == 1 else [pltpu.VMEM((tm, tn), F32)], name=name,
        compiler_params=_cparams(("parallel", "parallel", "arbitrary")))(*operands)


def _seg_copies(segs, c):
    out = []
    for lo, hi, dst in segs:
        n = lo
        while n < hi:
            p = n // c
            w = min(hi, (p + 1) * c) - n
            out.append((p, n - p * c, w, dst + n - lo))
            n += w
    return out


def _col_assemble(name, blocks, copies, zeros, n_out):
    _, R, c = blocks.shape
    tm = R if R <= 128 else 128

    def body(b_ref, o_ref):
        for p, s, w, d in copies:
            o_ref[:, d:d + w] = b_ref[p, :, s:s + w]
        for lo, hi in zeros:
            o_ref[:, lo:hi] = jnp.zeros((tm, hi - lo), o_ref.dtype)

    return pl.pallas_call(
        body, grid=(R // tm,), in_specs=[pl.BlockSpec((N_DEV, tm, c), lambda i: (0, i, 0))],
        out_specs=pl.BlockSpec((tm, n_out), lambda i: (i, 0)),
        out_shape=jax.ShapeDtypeStruct((R, n_out), blocks.dtype), name=name, compiler_params=_cparams(("parallel",)))(blocks)


def _col_split(name, full, copies, c, dtype):
    R, n = full.shape
    tm = R if R <= 128 else 128

    def body(f_ref, o_ref):
        for p, s, w, d in copies:
            o_ref[p, :, s:s + w] = f_ref[:, d:d + w].astype(dtype)

    return pl.pallas_call(
        body, grid=(R // tm,), in_specs=[pl.BlockSpec((tm, n), lambda i: (i, 0))],
        out_specs=pl.BlockSpec((N_DEV, tm, c), lambda i: (0, i, 0)),
        out_shape=jax.ShapeDtypeStruct((N_DEV, R, c), dtype), name=name, compiler_params=_cparams(("parallel",)))(full)


def _rmsnorm_fn(x, g):
    x = x.astype(F32)
    return (x * lax.rsqrt(jnp.mean(x * x, axis=-1, keepdims=True) + EPS) * g,)


def _gate_merge_fn(g0, g1, g2, ya, yb, yc, b0, b1, b2):
    return (jax.nn.sigmoid(g0 + b0) * ya + jax.nn.sigmoid(g1 + b1) * yb + jax.nn.sigmoid(g2 + b2) * yc,)


def _gated_norm_fn(y, z, w):
    t = y * _silu(z)
    return (t * lax.rsqrt(jnp.mean(t * t, axis=-1, keepdims=True) + EPS) * w,)


def _att_merge_fn(o0, o1, o2, l0, l1, l2):
    m = lax.stop_gradient(jnp.maximum(jnp.maximum(l0, l1), l2))
    e0, e1, e2 = jnp.exp(l0 - m), jnp.exp(l1 - m), jnp.exp(l2 - m)
    return ((e0 * o0 + e1 * o1 + e2 * o2) / (e0 + e1 + e2),)


def _loss_fn(x, tgt, g):
    (y,) = _rmsnorm_fn(x, g)
    err = y - tgt
    return 0.5 * jnp.sum(jnp.mean(err * err, axis=-1, keepdims=True), axis=0, keepdims=True)


def _pool_fn(x, wg, scale):
    g = pl.program_id(0)
    s2 = x + _shift(x, 1)
    s4 = s2 + _shift(s2, 2)
    s8 = s4 + _shift(s4, 4)
    s16 = s8 + _shift(s8, 8)
    win = ((g == 0).astype(F32) * s2 + (g == 1).astype(F32) * s4 + (g == 2).astype(F32) * s8
           + (g == 3).astype(F32) * s16)
    t = lax.broadcasted_iota(jnp.int32, (x.shape[0], 1), 0) + 1
    cnt = jnp.minimum(t, jnp.left_shift(2, g)).astype(F32)
    d = win / cnt - x
    return (_mm(d, wg.reshape(256, 256)) * scale,)


def _dwconv(x, taps, b):
    k = len(taps)
    y = taps[k - 1] * x + b
    for i in range(k - 1):
        y = y + taps[i] * _shift(x, k - 1 - i)
    return y


def _ssd_conv_fn(x, w0, w1, w2, w3, b):
    return (_silu(_dwconv(x, (w0, w1, w2, w3), b)),)


def _ffn_act_fn(xa, xv, a0, a1, a2, ab, v0, v1, v2, vb):
    return (_silu(_dwconv(xa, (a0, a1, a2), ab)) * _dwconv(xv, (v0, v1, v2), vb),)


@jax.custom_vjp
def _halves(x):
    return x[:ATT_BLOCK], x[ATT_BLOCK:]


_halves.defvjp(lambda x: (_halves(x), None), lambda _, g: (jnp.concatenate([g[0], g[1]], axis=0),))


def _att_block(q, kp, kc, vp, vc, bpa, bpb, bca, bcb, prev_ok):
    n = ATT_BLOCK
    lane = lax.broadcasted_iota(jnp.int32, (1, 2 * HEAD_DIM), 1)
    ma = (lane < HEAD_DIM).astype(F32)
    mb = 1.0 - ma
    q = q.astype(F32) * (1.0 / math.sqrt(HEAD_DIM))
    q2 = jnp.concatenate([q * ma, q * mb], axis=0)
    qi = lax.broadcasted_iota(jnp.int32, (2 * n, n), 0) & (n - 1)
    kj = lax.broadcasted_iota(jnp.int32, (2 * n, n), 1)
    sp = jnp.where(jnp.logical_and(kj >= qi, prev_ok), _mm_nt(q2, kp) + jnp.concatenate([bpa, bpb], axis=0), NEG)
    sc = jnp.where(kj <= qi, _mm_nt(q2, kc) + jnp.concatenate([bca, bcb], axis=0), NEG)
    m = lax.stop_gradient(jnp.maximum(jnp.max(sp, axis=1, keepdims=True), jnp.max(sc, axis=1, keepdims=True)))
    pp = jnp.exp(sp - m)
    pc = jnp.exp(sc - m)
    l = jnp.sum(pp, axis=1, keepdims=True) + jnp.sum(pc, axis=1, keepdims=True)
    oa, ob = _halves((_mm(pp, vp) + _mm(pc, vc)) / l)
    la, lb = _halves((m + jnp.log(l)) * jnp.ones((1, 2 * HEAD_DIM), F32))
    return oa * ma + ob * mb, la * ma + lb * mb


def _att_slab(dil):
    nbk = 4 if dil == 1 else 1
    t = ATT_BLOCK * dil * nbk
    return nbk, t, S // t


def _att_in_specs(gi, t):
    def spec(off, prev):
        col = off // 128 + gi * 3

        def index(p, j, col=col, prev=prev):
            jj = jnp.minimum(j, S // t - 1)
            return (jnp.maximum(jj - 1, 0) if prev else jj, col + p)
        return pl.BlockSpec((t, 2 * HEAD_DIM), index)
    return [spec(OFF_Q, False), spec(OFF_K, False), spec(OFF_K, True), spec(OFF_V, False), spec(OFF_V, True)]


def _bias_specs():
    return [pl.BlockSpec((None, ATT_BLOCK, ATT_BLOCK), functools.partial(lambda p, j, hh: (2 * p + hh, 0, 0), hh=hh))
            for hh in (0, 1)]


def _att_units(dil, nbk, body):
    def per_residue(r, carry):
        for b in range(nbk):
            rows = pl.ds(b * ATT_BLOCK * dil + r, ATT_BLOCK, stride=dil)
            prev = pl.ds(((b - 1) % nbk) * ATT_BLOCK * dil + r, ATT_BLOCK, stride=dil)
            body(b, rows, prev, b > 0)
        return carry
    if dil == 1:
        per_residue(0, 0)
    else:
        lax.fori_loop(0, dil, per_residue, 0)


def _att_fwd(name, proj, gi, bias_p, bias_c):
    dil = ATT_GROUPS[gi][1]
    nbk, t, ns = _att_slab(dil)
    bsp = _bias_specs()
    out_spec = pl.BlockSpec((t, 2 * HEAD_DIM), lambda p, j: (j, p))

    def body(q_ref, kc_ref, kp_ref, vc_ref, vp_ref, bpa, bpb, bca, bcb, o_ref, l_ref):
        first = pl.program_id(1) == 0
        biases = (bpa[...], bpb[...], bca[...], bcb[...])

        def unit(b, rows, prev, in_slab):
            kp = kc_ref[prev, :] if in_slab else kp_ref[prev, :]
            vp = vc_ref[prev, :] if in_slab else vp_ref[prev, :]
            prev_ok = True if in_slab else jnp.logical_not(first)
            o, lse = _att_block(q_ref[rows, :], kp, kc_ref[rows, :], vp, vc_ref[rows, :], *biases, prev_ok)
            o_ref[rows, :] = o
            l_ref[rows, :] = lse

        _att_units(dil, nbk, unit)

    shp = jax.ShapeDtypeStruct((S, ATT_GW), F32)
    return pl.pallas_call(
        body, grid=(3, ns), in_specs=_att_in_specs(gi, t) + [bsp[0], bsp[1], bsp[0], bsp[1]],
        out_specs=[out_spec, out_spec], out_shape=[shp, shp], name=name,
        compiler_params=_cparams(("arbitrary",) * 2))(proj, proj, proj, proj, proj, bias_p, bias_p, bias_c, bias_c)


def _att_bwd(name, proj, gi, bias_p, bias_c, do, dl):
    dil = ATT_GROUPS[gi][1]
    nbk, t, ns = _att_slab(dil)
    bsp = _bias_specs()
    blk = (t, 2 * HEAD_DIM)
    cur = pl.BlockSpec(blk, lambda p, j: (jnp.minimum(j, ns - 1), p))
    done = pl.BlockSpec(blk, lambda p, j: (jnp.maximum(j - 1, 0), p))
    gsp = pl.BlockSpec((None, ATT_BLOCK, ATT_BLOCK), lambda p, j: (p, 0, 0))

    def body(q_ref, kc_ref, kp_ref, vc_ref, vp_ref, bpa, bpb, bca, bcb, do_ref, dl_ref,
             dq_ref, dk_ref, dv_ref, gpa, gpb, gca, gcb, acck, accv):
        j = pl.program_id(1)
        mine, other = acck.at[j % 2], acck.at[1 - j % 2]
        mine_v, other_v = accv.at[j % 2], accv.at[1 - j % 2]

        @pl.when(j == 0)
        def _():
            for g in (gpa, gpb, gca, gcb):
                g[...] = jnp.zeros_like(g)
            other[...] = jnp.zeros_like(other)
            other_v[...] = jnp.zeros_like(other_v)

        @pl.when(j < ns)
        def _():
            mine[...] = jnp.zeros_like(mine)
            mine_v[...] = jnp.zeros_like(mine_v)
            biases = (bpa[...], bpb[...], bca[...], bcb[...])

            def unit(b, rows, prev, in_slab):
                kp = kc_ref[prev, :] if in_slab else kp_ref[prev, :]
                vp = vc_ref[prev, :] if in_slab else vp_ref[prev, :]
                prev_ok = True if in_slab else j > 0
                prim = (q_ref[rows, :], kp, kc_ref[rows, :], vp, vc_ref[rows, :]) + biases
                _, vjp = jax.vjp(lambda *a: _att_block(*a, prev_ok), *prim)
                dq, dkp, dkc, dvp, dvc, dpa, dpb, dca, dcb = vjp((do_ref[rows, :], dl_ref[rows, :]))
                dq_ref[rows, :] = dq
                mine[rows, :] += dkc
                mine_v[rows, :] += dvc
                tgt, tgt_v = (mine, mine_v) if in_slab else (other, other_v)
                tgt[prev, :] += dkp
                tgt_v[prev, :] += dvp
                gpa[...] += dpa
                gpb[...] += dpb
                gca[...] += dca
                gcb[...] += dcb

            _att_units(dil, nbk, unit)

        dk_ref[...] = other[...]
        dv_ref[...] = other_v[...]

    shp = jax.ShapeDtypeStruct((S, ATT_GW), F32)
    gshp = jax.ShapeDtypeStruct((3, ATT_BLOCK, ATT_BLOCK), F32)
    res = pl.pallas_call(
        body, grid=(3, ns + 1),
        in_specs=_att_in_specs(gi, t) + [bsp[0], bsp[1], bsp[0], bsp[1], cur, cur],
        out_specs=[cur, done, done, gsp, gsp, gsp, gsp],
        out_shape=[shp, shp, shp, gshp, gshp, gshp, gshp],
        scratch_shapes=[pltpu.VMEM((2,) + blk, F32)] * 2, name=name,
        compiler_params=_cparams(("arbitrary",) * 2))(proj, proj, proj, proj, proj, bias_p, bias_p, bias_c, bias_c, do, dl)
    dq, dk, dv, gpa, gpb, gca, gcb = res
    heads = lambda a, b: jnp.stack([a, b], axis=1).reshape(6, ATT_BLOCK, ATT_BLOCK)
    return dq, dk, dv, heads(gpa, gpb), heads(gca, gcb)


N_PAIR = SSD_HEADS // 2


def _ssd_chunk(xs, bs, cs_in, dt_raw, hs, a_row, dtb_row, ds):
    lane = lax.broadcasted_iota(jnp.int32, (1, 128), 1)
    row = lax.broadcasted_iota(jnp.int32, (128, 1), 0)
    tril = _tri(True)
    dt = _softplus(dt_raw + dtb_row)
    acs = _cumsum_rows(dt * a_row)
    acs_t = acs.T
    gmat = [_mm_nt(cs_in[g], bs[g]) for g in range(2)]
    lo = lane < HEAD_DIM
    lo_r = row < HEAD_DIM
    last = (row == SSD_CHUNK - 1).astype(F32)
    ys, hn = [], []
    for p in range(N_PAIR):
        g = p // (N_PAIR // 2)
        col, dtc, mm, clast = [], [], [], []
        for hh in range(2):
            h = 2 * p + hh
            oh = (lane == h).astype(F32)
            c_col = jnp.sum(acs * oh, axis=1, keepdims=True)
            c_row = jnp.sum(acs_t * (row == h).astype(F32), axis=0, keepdims=True)
            col.append(c_col)
            dtc.append(jnp.sum(dt * oh, axis=1, keepdims=True))
            clast.append(jnp.sum(c_col * last, axis=0, keepdims=True))
            mm.append(gmat[g] * jnp.exp(jnp.where(tril, c_col - c_row, NEG)))
        x = xs[p]
        xd = x * jnp.where(lo, dtc[0], dtc[1])
        y = jnp.where(lo, _mm(mm[0], xd), _mm(mm[1], xd))
        y = y + jnp.where(lo, jnp.exp(col[0]), jnp.exp(col[1])) * _mm_nt(cs_in[g], hs[p])
        ys.append(y + ds[p] * x)
        dec = jnp.where(lo, jnp.exp(clast[0] - col[0]), jnp.exp(clast[1] - col[1]))
        hn.append(hs[p] * jnp.where(lo_r, jnp.exp(clast[0]), jnp.exp(clast[1])) + _mm_tn(xd * dec, bs[g]))
    return tuple(ys), tuple(hn)


def _ssd_load(xbc_ref, dt_ref, a_ref, dtb_ref, d_ref):
    xs = tuple(xbc_ref[:, 128 * p:128 * (p + 1)] for p in range(N_PAIR))
    bs = tuple(xbc_ref[:, D + 128 * g:D + 128 * (g + 1)] for g in range(2))
    cs = tuple(xbc_ref[:, D + 256 + 128 * g:D + 256 + 128 * (g + 1)] for g in range(2))
    ds = tuple(d_ref[:, 128 * p:128 * (p + 1)] for p in range(N_PAIR))
    return xs, bs, cs, dt_ref[...], a_ref[...], dtb_ref[...], ds


def _ssd_fwd(name, xbc_c, proj, a_row, dtb_row, d_exp):
    nc = S // SSD_CHUNK
    prow = lambda n: pl.BlockSpec((1, n), lambda c: (0, 0))

    def body(xbc_ref, dt_ref, a_ref, dtb_ref, d_ref, y_ref, st_ref, h_ref):
        @pl.when(pl.program_id(0) == 0)
        def _():
            h_ref[...] = jnp.zeros_like(h_ref)

        xs, bs, cs, dt_raw, a, dtb, ds = _ssd_load(xbc_ref, dt_ref, a_ref, dtb_ref, d_ref)
        hs = tuple(h_ref[p] for p in range(N_PAIR))
        ys, hn = _ssd_chunk(xs, bs, cs, dt_raw, hs, a, dtb, ds)
        for p in range(N_PAIR):
            y_ref[:, 128 * p:128 * (p + 1)] = ys[p]
            st_ref[p] = hs[p]
            h_ref[p] = hn[p]

    return pl.pallas_call(
        body, grid=(nc,),
        in_specs=[pl.BlockSpec((SSD_CHUNK, SSD_XBC), lambda c: (c, 0)),
                  pl.BlockSpec((SSD_CHUNK, DT_PAD), lambda c: (c, OFF_DT // DT_PAD)),
                  prow(128), prow(128), prow(D)],
        out_specs=[pl.BlockSpec((SSD_CHUNK, D), lambda c: (c, 0)),
                   pl.BlockSpec((None, N_PAIR, 128, 128), lambda c: (c, 0, 0, 0))],
        out_shape=[jax.ShapeDtypeStruct((S, D), F32), jax.ShapeDtypeStruct((nc, N_PAIR, 128, 128), F32)],
        scratch_shapes=[pltpu.VMEM((N_PAIR, 128, 128), F32)], name=name,
        compiler_params=_cparams(("arbitrary",)))(xbc_c, proj, a_row, dtb_row, d_exp)


def _ssd_bwd(name, xbc_c, proj, states, dy, a_row, dtb_row, d_exp, dproj):
    nc = S // SSD_CHUNK
    prow = lambda n: pl.BlockSpec((1, n), lambda i: (0, 0))
    rc = lambda i: nc - 1 - i

    def body(xbc_ref, dt_ref, st_ref, dy_ref, a_ref, dtb_ref, d_ref, _, dxbc_ref, ddt_ref, da_ref, ddtb_ref, dd_ref, e_ref):
        i = pl.program_id(0)

        @pl.when(i == 0)
        def _():
            e_ref[...] = jnp.zeros_like(e_ref)
            da_ref[...] = jnp.zeros_like(da_ref)
            ddtb_ref[...] = jnp.zeros_like(ddtb_ref)
            dd_ref[...] = jnp.zeros_like(dd_ref)

        xs, bs, cs, dt_raw, a, dtb, ds = _ssd_load(xbc_ref, dt_ref, a_ref, dtb_ref, d_ref)
        hs = tuple(st_ref[p] for p in range(N_PAIR))
        _, vjp = jax.vjp(_ssd_chunk, xs, bs, cs, dt_raw, hs, a, dtb, ds)
        dys = tuple(dy_ref[:, 128 * p:128 * (p + 1)] for p in range(N_PAIR))
        es = tuple(e_ref[p] for p in range(N_PAIR))
        dxs, dbs, dcs, ddt, dhs, da, ddtb, dds = vjp((dys, es))
        for p in range(N_PAIR):
            dxbc_ref[:, 128 * p:128 * (p + 1)] = dxs[p]
            e_ref[p] = dhs[p]
            dd_ref[:, 128 * p:128 * (p + 1)] += dds[p]
        for g in range(2):
            dxbc_ref[:, D + 128 * g:D + 128 * (g + 1)] = dbs[g]
            dxbc_ref[:, D + 256 + 128 * g:D + 256 + 128 * (g + 1)] = dcs[g]
        ddt_ref[:, :DT_PAD] = ddt.astype(BF16)
        ddt_ref[:, DT_PAD:] = jnp.zeros((SSD_CHUNK, OFF_Q - OFF_DT - DT_PAD), BF16)
        da_ref[...] += da
        ddtb_ref[...] += ddtb

    dt_w = OFF_Q - OFF_DT
    return pl.pallas_call(
        body, grid=(nc,),
        in_specs=[pl.BlockSpec((SSD_CHUNK, SSD_XBC), lambda i: (rc(i), 0)),
                  pl.BlockSpec((SSD_CHUNK, DT_PAD), lambda i: (rc(i), OFF_DT // DT_PAD)),
                  pl.BlockSpec((None, N_PAIR, 128, 128), lambda i: (rc(i), 0, 0, 0)),
                  pl.BlockSpec((SSD_CHUNK, D), lambda i: (rc(i), 0)),
                  prow(128), prow(128), prow(D), pl.BlockSpec(memory_space=pl.ANY)],
        out_specs=[pl.BlockSpec((SSD_CHUNK, SSD_XBC), lambda i: (rc(i), 0)),
                   pl.BlockSpec((SSD_CHUNK, dt_w), lambda i: (rc(i), OFF_DT // dt_w)),
                   prow(128), prow(128), prow(D)],
        out_shape=[jax.ShapeDtypeStruct((S, SSD_XBC), F32), jax.ShapeDtypeStruct((S, NP), BF16),
                   jax.ShapeDtypeStruct((1, 128), F32), jax.ShapeDtypeStruct((1, 128), F32),
                   jax.ShapeDtypeStruct((1, D), F32)],
        input_output_aliases={7: 1},
        scratch_shapes=[pltpu.VMEM((N_PAIR, 128, 128), F32)], name=name,
        compiler_params=_cparams(("arbitrary",)))(xbc_c, proj, states, dy, a_row, dtb_row, d_exp, dproj)


def _exchange(name, arrays, scatter):
    n = len(arrays)
    flips = [(dx, dy, dc) for dx in (0, 1) for dy in (0, 1) for dc in (0, 1) if dx or dy or dc]

    def body(*refs):
        ins, outs = refs[:n], refs[n:2 * n]
        send_sems, recv_sems, loc_sems = refs[2 * n:]
        x, y, c = lax.axis_index("x"), lax.axis_index("y"), lax.axis_index("c")
        me = 4 * x + 2 * y + c
        peers = []
        for dx, dy, dc in flips:
            px, py, pc = (1 - x if dx else x), (1 - y if dy else y), (1 - c if dc else c)
            peers.append(((px, py, pc), 4 * px + 2 * py + pc))

        def remote(k, j, landed_from):
            dev, pid = peers[j]
            src = ins[k].at[pid] if scatter else ins[k]
            return pltpu.make_async_remote_copy(
                src_ref=src, dst_ref=outs[k].at[landed_from], send_sem=send_sems.at[k, j], recv_sem=recv_sems.at[k, j],
                device_id=dev, device_id_type=pl.DeviceIdType.MESH)

        local = [pltpu.make_async_copy(ins[k].at[me] if scatter else ins[k], outs[k].at[me], loc_sems.at[k])
                 for k in range(n)]
        for cp in local:
            cp.start()
        for k in range(n):
            for j in range(len(flips)):
                remote(k, j, me).start()
        for cp in local:
            cp.wait()
        for k in range(n):
            for j in range(len(flips)):
                remote(k, j, me).wait_send()
                remote(k, j, peers[j][1]).wait_recv()

    hbm = pl.BlockSpec(memory_space=pltpu.HBM)
    out_shape = [jax.ShapeDtypeStruct(a.shape if scatter else (N_DEV,) + a.shape, a.dtype) for a in arrays]
    res = pl.pallas_call(
        body, in_specs=[hbm] * n, out_specs=[hbm] * n, out_shape=out_shape, name=name,
        scratch_shapes=[pltpu.SemaphoreType.DMA((n, len(flips))), pltpu.SemaphoreType.DMA((n, len(flips))),
                        pltpu.SemaphoreType.DMA((n,))])(*arrays)
    return list(res)


def _peer_copies(ins, lands, send_sems, recv_sems, loc_sems, scatter):
    n = len(ins)
    flips = [(dx, dy, dc) for dx in (0, 1) for dy in (0, 1) for dc in (0, 1) if dx or dy or dc]
    x, y, c = lax.axis_index("x"), lax.axis_index("y"), lax.axis_index("c")
    me = 4 * x + 2 * y + c
    peers = []
    for dx, dy, dc in flips:
        px, py, pc = (1 - x if dx else x), (1 - y if dy else y), (1 - c if dc else c)
        peers.append(((px, py, pc), 4 * px + 2 * py + pc))

    def remote(k, j, slot):
        dev, pid = peers[j]
        return pltpu.make_async_remote_copy(
            src_ref=ins[k].at[pid] if scatter else ins[k], dst_ref=lands[k].at[slot],
            send_sem=send_sems.at[k * N_FLIP + j], recv_sem=recv_sems.at[k * N_FLIP + j],
            device_id=dev, device_id_type=pl.DeviceIdType.MESH)

    local = [pltpu.make_async_copy(ins[k].at[me] if scatter else ins[k], lands[k].at[me], loc_sems.at[k])
             for k in range(n)]
    pairs = [(k, j) for k in range(n) for j in range(len(flips))]
    sent = lambda k, j: remote(k, j, me)
    landed = lambda k, j: remote(k, j, peers[j][1])
    return local, pairs, sent, landed


_HBM = pl.BlockSpec(memory_space=pltpu.HBM)
_SEM = pl.BlockSpec(memory_space=pltpu.SEMAPHORE)
N_FLIP = N_DEV - 1


def _exchange_start(name, arrays, scatter, after):
    n = len(arrays)
    arrays = [pltpu.with_memory_space_constraint(a, pltpu.HBM) for a in arrays]
    lands = [pltpu.with_memory_space_constraint(
        lax.empty(a.shape if scatter else (N_DEV,) + a.shape, a.dtype), pltpu.HBM) for a in arrays]

    def body(*refs):
        ins, lnd = refs[:n], refs[n:2 * n]
        send_sems, recv_sems, loc_sems = refs[2 * n + 1:2 * n + 4]
        token = refs[-1]
        local, pairs, sent, _ = _peer_copies(ins, lnd, send_sems, recv_sems, loc_sems, scatter)
        for cp in local:
            cp.start()
        for k, j in pairs:
            sent(k, j).start()
        token[...] = jnp.zeros_like(token)

    res = pl.pallas_call(
        body, name=name,
        in_specs=[_HBM] * (2 * n) + [pl.BlockSpec(memory_space=pl.ANY)],
        out_specs=[_SEM, _SEM, _SEM] + [_HBM] * (2 * n) + [pl.BlockSpec(memory_space=pltpu.VMEM)],
        out_shape=[pltpu.SemaphoreType.DMA((n * N_FLIP,)), pltpu.SemaphoreType.DMA((n * N_FLIP,)), pltpu.SemaphoreType.DMA((n,))]
        + [pltpu.HBM(a.shape, a.dtype) for a in arrays] + [pltpu.HBM(a.shape, a.dtype) for a in lands]
        + [jax.ShapeDtypeStruct((8, 128), F32)],
        input_output_aliases={k: 3 + k for k in range(2 * n)},
        compiler_params=pltpu.CompilerParams(has_side_effects=pltpu.SideEffectType.DATAFLOW_SIDE_EFFECTING),
    )(*arrays, *lands, after)
    return (res[:3], res[3:3 + n], res[3 + n:3 + 2 * n], scatter), res[-1]


def _exchange_wait(name, state, after):
    sems, ins_thru, lands_thru, scatter = state
    n = len(ins_thru)

    def body(*refs):
        ins, lnd = refs[:n], refs[n:2 * n]
        send_sems, recv_sems, loc_sems = refs[2 * n:2 * n + 3]
        local, pairs, sent, landed = _peer_copies(ins, lnd, send_sems, recv_sems, loc_sems, scatter)
        for cp in local:
            cp.wait()
        for k, j in pairs:
            sent(k, j).wait_send()
            landed(k, j).wait_recv()

    res = pl.pallas_call(
        body, name=name,
        in_specs=[_HBM] * (2 * n) + [_SEM, _SEM, _SEM] + [pl.BlockSpec(memory_space=pl.ANY)],
        out_specs=[_HBM] * (2 * n),
        out_shape=[pltpu.HBM(a.shape, a.dtype) for a in ins_thru] + [pltpu.HBM(a.shape, a.dtype) for a in lands_thru],
        input_output_aliases={k: k for k in range(2 * n)},
        compiler_params=pltpu.CompilerParams(has_side_effects=pltpu.SideEffectType.DATAFLOW_SIDE_EFFECTING),
    )(*ins_thru, *lands_thru, *sems, after)
    return list(res[n:])


def _adamw_fn(*vals):
    slots, (w, m, v) = vals[:N_DEV], vals[N_DEV:]
    g = slots[0].astype(F32)
    for s in slots[1:]:
        g = g + s.astype(F32)
    m2 = ADAM_B1 * m + (1.0 - ADAM_B1) * g
    v2 = ADAM_B2 * v + (1.0 - ADAM_B2) * (g * g)
    m_hat = m2 / (1.0 - ADAM_B1 ** ADAM_STEP)
    v_hat = v2 / (1.0 - ADAM_B2 ** ADAM_STEP)
    delta = -ADAM_LR * (m_hat / (jnp.sqrt(v_hat) + ADAM_EPS) + ADAM_WD * w)
    return (g, delta, m2, v2), ()


def _adamw(name, slots, w, m, v, first_row=0, prev=None):
    R, C = slots.shape[1:]
    tm = R if R <= 128 else _pick(R, 128 if C > D else 256, 8)
    rows = ([_rows(slots, C, lead=s) for s in range(N_DEV)]
            + [_rows(a, C, roff=first_row // tm) for a in (w, m, v)])
    return _tiled(name, _adamw_fn, (1, R // tm), tm, rows, [], [(w.shape[0], C, F32)] * 4,
                  out_roff=first_row // tm, prev_outs=prev)


def _bucket_onehots():
    out = []
    qi = jnp.arange(ATT_BLOCK)[:, None]
    kj = jnp.arange(ATT_BLOCK)[None, :]
    max_exact = REL_BUCKETS // 2
    for _, dil in ATT_GROUPS:
        parts = []
        for rel in (qi + ATT_BLOCK - kj, qi - kj):
            dist = jnp.clip(rel, 0, None) * dil
            nf = jnp.maximum(dist, 1).astype(F32)
            large = max_exact + (jnp.log(nf / max_exact) / math.log(REL_MAX_DISTANCE / max_exact)
                                 * (REL_BUCKETS - max_exact)).astype(jnp.int32)
            large = jnp.minimum(large, REL_BUCKETS - 1)
            bucket = jnp.where(dist < max_exact, dist, large)
            parts.append((bucket[:, :, None] == jnp.arange(REL_BUCKETS)[None, None, :]).astype(F32))
        out.append(jnp.stack(parts))
    return out


SHARDED = ("w_in", "w_a", "pool_w", "w_b", "ssd_conv_w", "w_c", "w_o", "ffn_w_up", "ffn_conv_w", "ffn_w_down")
MATMUL_WEIGHTS = ("w_in", "w_a", "pool_w", "w_b", "w_c", "w_o", "ffn_w_up", "ffn_w_down")
ROW_SHARDED = ("w_b", "w_c", "w_o", "ffn_w_down")
W_IN_SEGS = ((0, 1152, OFF_Q), (1152, 2304, OFF_K), (2304, 3456, OFF_V), (3456, 4480, OFF_POOL), (4480, 5504, OFF_Z),
             (5504, 7040, OFF_XBC), (7040, 7056, OFF_DT), (7056, IN_WIDTH, OFF_GATE))
COL_SHARDED = {
    "w_in": (IN_WIDTH // N_DEV, W_IN_SEGS, ((OFF_DT + SSD_HEADS, OFF_Q),), NP),
    "w_a": (D // N_DEV, ((0, D, 0),), (), D),
    "ffn_w_up": (2 * D_FF // N_DEV, ((0, 2 * D_FF, 0),), (), 2 * D_FF),
    "ssd_conv_w": (SSD_XBC // N_DEV, ((0, SSD_XBC, 0),), (), SSD_XBC),
    "ffn_conv_w": (2 * D_FF // N_DEV, ((0, 2 * D_FF, 0),), (), 2 * D_FF),
}
REPLICATED = ("rel_bias", "ln1_g", "b_gate", "pool_scale", "ssd_conv_b", "ssd_dt_bias", "ssd_a_log", "ssd_d",
              "ssd_norm_w", "ln2_g", "ffn_conv_b", "final_g")
WEIGHTS = ("rel_bias", "ln1_g", "w_in", "b_gate", "w_a", "pool_w", "pool_scale", "w_b", "ssd_conv_w", "ssd_conv_b",
           "ssd_dt_bias", "ssd_a_log", "ssd_d", "ssd_norm_w", "w_c", "w_o", "ln2_g", "ffn_w_up", "ffn_conv_w",
           "ffn_conv_b", "ffn_w_down", "final_g")


def _local_weight(name, n, blocks):
    if n in COL_SHARDED:
        c, segs, zeros, width = COL_SHARDED[n]
        return _col_assemble(name, blocks, _seg_copies(segs, c), zeros, width)
    if n in ROW_SHARDED:
        return blocks.reshape(-1, blocks.shape[-1])
    return blocks


def _device_blocks(name, n, g):
    if n in COL_SHARDED:
        c, segs, _, _ = COL_SHARDED[n]
        return _col_split(name, g, _seg_copies(segs, c), c, BF16)
    if n in ROW_SHARDED:
        return g.reshape(N_DEV, g.shape[0] // N_DEV, g.shape[1]).astype(BF16)
    return g.astype(BF16)


def _row(v, n=None):
    v = v.reshape(1, -1)
    if n is not None and v.shape[1] < n:
        v = jnp.pad(v, ((0, 0), (0, n - v.shape[1])))
    return v


RT = 256


def _row_call(name, fn, cw, ncol, rows, params, outs, accs=(), into=None):
    return _tiled(name, fn, (ncol, S // RT), RT, rows, params, [(S, cw, dt) for dt in outs], accs, into=into)


def _col_call(name, fn, tc, ncol, rows, params, outs, accs=(), into=None):
    return _tiled(name, fn, (ncol, 1), S, rows, params, [(S, tc, dt) for dt in outs], accs, into=into)


def _fwd_only(fn):
    return lambda *a: (fn(*a), ())


def _layer_fwd(i, x, W, P, bias_tabs, late=None):
    sv = {"x": x}
    (u,) = _row_call(f"ln1_f{i}", _fwd_only(_rmsnorm_fn), D, 1, [_rows(x, D)], [_p_row(P["ln1_g"], D)], [BF16])
    proj = _matmul(f"inproj_f{i}", u, W["w_in"], "nn")
    sv["u"], sv["proj"] = u, proj

    os_, ls_ = [], []
    for gi in range(len(ATT_GROUPS)):
        o, lse = _att_fwd(f"att_f{i}_{gi}", proj, gi, bias_tabs[gi][0], bias_tabs[gi][1])
        os_.append(o)
        ls_.append(lse)
    sv["att_o"], sv["att_l"] = os_, ls_
    (att,) = _row_call(f"attmerge_f{i}", _fwd_only(_att_merge_fn), ATT_GW, 1,
                       [_rows(t, ATT_GW) for t in os_ + ls_], [], [BF16])
    if late is not None:
        W2, P2 = late(att)
        W.update(W2)
        P.update(P2)
    y_a = _matmul(f"wa_f{i}", att, W["w_a"], "nn")
    sv["att"], sv["y_a"] = att, y_a

    pool_params = [(W["pool_w"], (N_DEV, None, 32, 256), lambda j, i_: (0, j, 0, 0)), _p_row(P["pool_scale"], 256)]
    (yb_pre,) = _col_call(f"pool_f{i}", _fwd_only(_pool_fn), 256, 4, [_rows(proj, 256, OFF_POOL // 256)],
                          pool_params, [BF16])
    y_b = _matmul(f"wb_f{i}", yb_pre, W["w_b"], "nn")
    sv["yb_pre"], sv["y_b"] = yb_pre, y_b

    conv_params = [_p_row(P["ssd_conv_w"][k], 128) for k in range(4)] + [_p_row(P["ssd_conv_b"], 128)]
    (xbc_c,) = _col_call(f"ssdconv_f{i}", _fwd_only(_ssd_conv_fn), 128, SSD_XBC // 128,
                         [_rows(proj, 128, OFF_XBC // 128)], conv_params, [F32])
    y_ssd, states = _ssd_fwd(f"ssd_f{i}", xbc_c, proj, P["a_row"], P["dtb_row"], P["d_exp"])
    (yc_pre,) = _row_call(f"ssdnorm_f{i}", _fwd_only(_gated_norm_fn), 512, 2,
                          [_rows(y_ssd, 512), _rows(proj, 512, OFF_Z // 512)], [_p_row(P["ssd_norm_w"], 512)], [BF16])
    y_c = _matmul(f"wc_f{i}", yc_pre, W["w_c"], "nn")
    sv["xbc_c"], sv["states"], sv["y_ssd"], sv["yc_pre"], sv["y_c"] = xbc_c, states, y_ssd, yc_pre, y_c

    gate_rows = [_rows(proj, D, k) for k in range(3)] + [_rows(t, D) for t in (y_a, y_b, y_c)]
    gate_params = [_p_row(P["b_gate"], D, k) for k in range(3)]
    (merged,) = _row_call(f"gate_f{i}", _fwd_only(_gate_merge_fn), D, 1, gate_rows, gate_params, [BF16])
    x1 = _matmul(f"wo_f{i}", merged, W["w_o"], "nn", add=x)
    sv["merged"], sv["x1"] = merged, x1

    (u2,) = _row_call(f"ln2_f{i}", _fwd_only(_rmsnorm_fn), D, 1, [_rows(x1, D)], [_p_row(P["ln2_g"], D)], [BF16])
    up = _matmul(f"up_f{i}", u2, W["ffn_w_up"], "nn")
    (act,) = _col_call(f"ffnact_f{i}", _fwd_only(_ffn_act_fn), 128, D_FF // 128,
                       [_rows(up, 128), _rows(up, 128, D_FF // 128)], _ffn_params(P), [BF16])
    x2 = _matmul(f"down_f{i}", act, W["ffn_w_down"], "nn", add=x1)
    sv["u2"], sv["up"], sv["act"] = u2, up, act
    return x2, sv


def _ffn_params(P):
    nb = D_FF // 128
    return ([_p_row(P["ffn_conv_w"][k], 128) for k in range(3)] + [_p_row(P["ffn_conv_b"], 128)]
            + [_p_row(P["ffn_conv_w"][k], 128, nb) for k in range(3)] + [_p_row(P["ffn_conv_b"], 128, nb)])


def _layer_bwd(i, dx2, sv, W, P, bias_tabs, onehots, on_sharded_grads):
    G = {}
    x, proj, x1 = sv["x"], sv["proj"], sv["x1"]

    dact = _matmul(f"down_bx{i}", dx2, W["ffn_w_down"], "nt", out_dtype=BF16)
    G["ffn_w_down"] = _matmul(f"down_bw{i}", sv["act"], dx2, "tn", out_dtype=BF16)
    nb = D_FF // 128
    up = sv["up"]
    f = _with_vjp(_ffn_act_fn, 10, (0, 1), tuple(range(2, 10)))
    accs = [_a_row(D_FF, 128)] * 8
    dua, duv, a0, a1, a2, ab, v0, v1, v2, vb = _col_call(
        f"ffnact_b{i}", f, 128, nb, [_rows(up, 128), _rows(up, 128, nb)], _ffn_params(P) + [_rows_as_param(dact, 128)],
        [BF16, BF16], accs)
    G["ffn_conv_w"] = jnp.concatenate([jnp.concatenate([a0, a1, a2], 0), jnp.concatenate([v0, v1, v2], 0)], axis=1)
    G["ffn_conv_b"] = jnp.concatenate([ab, vb], axis=1)[0]
    dup = jnp.concatenate([dua, duv], axis=1)
    du2 = _matmul(f"up_bx{i}", dup, W["ffn_w_up"], "nt")
    G["ffn_w_up"] = _matmul(f"up_bw{i}", sv["u2"], dup, "tn", out_dtype=BF16)

    def norm_bwd(x_, g_, du_, dres):
        (dxn,), (dg,) = _with_vjp(_rmsnorm_fn, 2, (0,), (1,))(x_, g_, du_)
        return (dxn + dres,), (dg,)

    (dx1,), (G["ln2_g"],) = _split_res(_row_call(
        f"ln2_b{i}", lambda x_, du_, dres, g_: norm_bwd(x_, g_, du_, dres), D, 1,
        [_rows(x1, D), _rows(du2, D), _rows(dx2, D)], [_p_row(P["ln2_g"], D)], [F32], [_a_row(D, D)]), 1)

    dmerged = _matmul(f"wo_bx{i}", dx1, W["w_o"], "nt")
    G["w_o"] = _matmul(f"wo_bw{i}", sv["merged"], dx1, "tn", out_dtype=BF16)
    def gate_bwd(g_, y_, dm, b_):
        return _with_vjp(lambda g, y, b: (jax.nn.sigmoid(g + b) * y,), 3, (0, 1), (2,))(g_, y_, b_, dm)

    dproj, dys, dbs = None, [], []
    for k, t in enumerate(("y_a", "y_b", "y_c")):
        dproj, dy_k, db_k = _row_call(
            f"gate_b{i}_{k}", gate_bwd, D, 1, [_rows(proj, D, k), _rows(sv[t], D), _rows(dmerged, D)],
            [_p_row(P["b_gate"], D, k)], [BF16, BF16], [_a_row(D, D)], into={0: (dproj, k, NP)})
        dys.append(dy_k)
        dbs.append(db_k)
    dya, dyb, dyc = dys
    G["b_gate"] = jnp.concatenate(dbs, axis=1)[0]

    dyc_pre = _matmul(f"wc_bx{i}", dyc, W["w_c"], "nt")
    G["w_c"] = _matmul(f"wc_bw{i}", sv["yc_pre"], dyc, "tn", out_dtype=BF16)

    def gnorm_bwd(y_, z_, dy_, w_):
        return _with_vjp(_gated_norm_fn, 3, (0, 1), (2,))(y_, z_, w_, dy_)

    dy_ssd, dproj, dnw = _row_call(
        f"ssdnorm_b{i}", gnorm_bwd, 512, 2,
        [_rows(sv["y_ssd"], 512), _rows(proj, 512, OFF_Z // 512), _rows(dyc_pre, 512)],
        [_p_row(P["ssd_norm_w"], 512)], [F32, BF16], [_a_row(D, 512)], into={1: (dproj, OFF_Z // 512, NP)})
    G["ssd_norm_w"] = dnw[0]
    dxbc_c, dproj, da_row, ddtb_row, dd_exp = _ssd_bwd(f"ssd_b{i}", sv["xbc_c"], proj, sv["states"], dy_ssd,
                                                       P["a_row"], P["dtb_row"], P["d_exp"], dproj)
    a_vec = P["a_row"][0, :SSD_HEADS]
    G["ssd_a_log"] = da_row[0, :SSD_HEADS] * a_vec
    G["ssd_dt_bias"] = ddtb_row[0, :SSD_HEADS]
    G["ssd_d"] = dd_exp.reshape(SSD_HEADS, HEAD_DIM).sum(axis=1)
    conv_params = [_p_row(P["ssd_conv_w"][k], 128) for k in range(4)] + [_p_row(P["ssd_conv_b"], 128)]

    def conv_bwd(x_, dy_, w0, w1, w2, w3, b_):
        return _with_vjp(_ssd_conv_fn, 6, (0,), (1, 2, 3, 4, 5))(x_, w0, w1, w2, w3, b_, dy_)

    dproj, c0, c1, c2, c3, cb = _col_call(
        f"ssdconv_b{i}", conv_bwd, 128, SSD_XBC // 128, [_rows(proj, 128, OFF_XBC // 128), _rows(dxbc_c, 128)],
        conv_params, [BF16], [_a_row(SSD_XBC, 128)] * 5, into={0: (dproj, OFF_XBC // 128, NP)})
    G["ssd_conv_w"] = jnp.concatenate([c0, c1, c2, c3], axis=0)
    G["ssd_conv_b"] = cb[0]

    dyb_pre = _matmul(f"wb_bx{i}", dyb, W["w_b"], "nt")
    G["w_b"] = _matmul(f"wb_bw{i}", sv["yb_pre"], dyb, "tn", out_dtype=BF16)
    pool_params = [(W["pool_w"], (N_DEV, None, 32, 256), lambda j, i_: (0, j, 0, 0)), _p_row(P["pool_scale"], 256)]

    def pool_bwd(x_, dy_, wg, sc):
        return _with_vjp(_pool_fn, 3, (0,), (1, 2))(x_, wg.astype(F32), sc, dy_)

    dproj, dwg, dsc = _col_call(
        f"pool_b{i}", pool_bwd, 256, 4, [_rows(proj, 256, OFF_POOL // 256), _rows(dyb_pre, 256)], pool_params, [BF16],
        [((N_DEV, 4, 32, 256), (N_DEV, None, 32, 256), lambda j, i_: (0, j, 0, 0)), _a_row(D, 256)],
        into={0: (dproj, OFF_POOL // 256, NP)})
    G["pool_w"] = dwg
    G["pool_scale"] = dsc[0]

    datt = _matmul(f"wa_bx{i}", dya, W["w_a"], "nt")
    G["w_a"] = _matmul(f"wa_bw{i}", sv["att"], dya, "tn", out_dtype=BF16)

    def merge_bwd(o0, o1, o2, l0, l1, l2, da_):
        return _with_vjp(_att_merge_fn, 6, (0, 1, 2, 3, 4, 5), ())(o0, o1, o2, l0, l1, l2, da_)

    dol = _row_call(f"attmerge_b{i}", merge_bwd, ATT_GW, 1,
                    [_rows(t, ATT_GW) for t in sv["att_o"] + sv["att_l"]] + [_rows(datt, ATT_GW)], [], [F32] * 6)
    dqs, dks, dvs = [], [], []
    g_rel = jnp.zeros((REL_BUCKETS, 18), F32)
    for gi in range(len(ATT_GROUPS)):
        dq, dk, dv, gbp, gbc = _att_bwd(f"att_b{i}_{gi}", proj, gi, bias_tabs[gi][0], bias_tabs[gi][1],
                                        dol[gi], dol[3 + gi])
        dqs.append(dq)
        dks.append(dk)
        dvs.append(dv)
        oh = onehots[gi]
        gt = (jnp.einsum("hqk,qkb->bh", gbp, oh[0], precision=lax.Precision.HIGHEST)
              + jnp.einsum("hqk,qkb->bh", gbc, oh[1], precision=lax.Precision.HIGHEST))
        g_rel = g_rel.at[:, gi * 6:(gi + 1) * 6].add(gt)
    G["rel_bias"] = g_rel

    dqkv = jnp.concatenate([t.astype(BF16) for t in dqs + dks + dvs], axis=1)
    dproj = lax.dynamic_update_slice(dproj, dqkv, (0, OFF_Q))
    du = _matmul(f"inproj_bx{i}", dproj, W["w_in"], "nt")
    G["w_in"] = _matmul(f"inproj_bw{i}", sv["u"], dproj, "tn", out_dtype=BF16)
    ln1_g = P["ln1_g"] + on_sharded_grads(G)
    (dx,), (G["ln1_g"],) = _split_res(_row_call(
        f"ln1_b{i}", lambda x_, du_, dres, g_: norm_bwd(x_, g_, du_, dres), D, 1,
        [_rows(x, D), _rows(du, D), _rows(dx1, D)], [_p_row(ln1_g, D)], [F32], [_a_row(D, D)]), 1)
    G["ln1_g"] = G["ln1_g"][0]
    G["ln2_g"] = G["ln2_g"][0]
    return dx, G


def _rows_as_param(arr, cw):
    return (arr, (arr.shape[0], cw), lambda j, i: (0, j))


def _split_res(res, n_out):
    return tuple(res[:n_out]), tuple(res[n_out:])


def kernel(x, rel_bias, ln1_g, w_in, b_gate, w_a, pool_w, pool_scale, w_b, ssd_conv_w, ssd_conv_b, ssd_dt_bias, ssd_a_log, ssd_d, ssd_norm_w, w_c, w_o, ln2_g, ffn_w_up, ffn_conv_w, ffn_conv_b, ffn_w_down, final_g, loss_target, m_rel_bias, m_ln1_g, m_w_in, m_b_gate, m_w_a, m_pool_w, m_pool_scale, m_w_b, m_ssd_conv_w, m_ssd_conv_b, m_ssd_dt_bias, m_ssd_a_log, m_ssd_d, m_ssd_norm_w, m_w_c, m_w_o, m_ln2_g, m_ffn_w_up, m_ffn_conv_w, m_ffn_conv_b, m_ffn_w_down, m_final_g, v_rel_bias, v_ln1_g, v_w_in, v_b_gate, v_w_a, v_pool_w, v_pool_scale, v_w_b, v_ssd_conv_w, v_ssd_conv_b, v_ssd_dt_bias, v_ssd_a_log, v_ssd_d, v_ssd_norm_w, v_w_c, v_w_o, v_ln2_g, v_ffn_w_up, v_ffn_conv_w, v_ffn_conv_b, v_ffn_w_down, v_final_g):
    args = locals()
    wts = {n: args[n] for n in WEIGHTS}
    mom = {n: args["m_" + n] for n in WEIGHTS}
    var = {n: args["v_" + n] for n in WEIGHTS}
    names = list(SHARDED)

    onehots = _bucket_onehots()
    bias_tabs = []
    for gi in range(3):
        tab = rel_bias[:, gi * 6:(gi + 1) * 6]
        b = jnp.einsum("pqkb,bh->phqk", onehots[gi], tab, precision=lax.Precision.HIGHEST)
        bias_tabs.append((b[0], b[1]))

    def gather_start(tag, i, which, after):
        shards = [wts[n][i].astype(BF16) if n in MATMUL_WEIGHTS else wts[n][i] for n in which]
        return _exchange_start(f"gather_start{tag}", shards, False, after)

    def layer_params(i, which, landed):
        full = {n: _local_weight(f"local_{n}{i}", n, g) for n, g in zip(which, landed)}
        W = {n: full[n] for n in which if n in MATMUL_WEIGHTS}
        P = {}
        if "ssd_conv_w" in full:
            P["ssd_conv_w"] = [_row(full["ssd_conv_w"][k]) for k in range(4)]
            P["ffn_conv_w"] = [_row(full["ffn_conv_w"][k]) for k in range(3)]
        return W, P

    def replicated_params(i):
        return {"ln1_g": _row(ln1_g[i]), "ln2_g": _row(ln2_g[i]), "b_gate": _row(b_gate[i]),
                "pool_scale": _row(pool_scale[i]), "ssd_conv_b": _row(ssd_conv_b[i]),
                "ssd_norm_w": _row(ssd_norm_w[i]), "ffn_conv_b": _row(ffn_conv_b[i]),
                "a_row": _row(-jnp.exp(ssd_a_log[i]), 128), "dtb_row": _row(ssd_dt_bias[i], 128),
                "d_exp": _row(jnp.repeat(ssd_d[i], HEAD_DIM))}

    h = x.reshape(S, D)
    saved, Ws, Ps = [], [], []
    first, rest = ["w_in"], [n for n in names if n != "w_in"]
    state, _ = gather_start("0a", 0, first, h)
    landed_first = _exchange_wait("gather_wait0a", state, h)
    state_rest, token = gather_start("0b", 0, rest, landed_first[0])
    nxt = {}

    def late0(att):
        landed_rest = _exchange_wait("gather_wait0b", state_rest, att)
        W2, P2 = layer_params(0, rest, landed_rest)
        nxt["state"], tok = gather_start("1", 1, names, landed_rest[0])
        P2["pool_scale"] = _row(pool_scale[0]) + tok[0, 0]
        return W2, P2

    for i in range(DEPTH):
        P = replicated_params(i)
        if i == 0:
            W, P1 = layer_params(0, first, landed_first)
        else:
            W, P1 = layer_params(i, names, landed)
            if i + 1 < DEPTH:
                nxt["state"], token = gather_start(str(i + 1), i + 1, names, landed[0])
        P.update(P1)
        if i + 1 < DEPTH:
            P["ln1_g"] = P["ln1_g"] + token[0, 0]
        h, sv = _layer_fwd(i, h, W, P, bias_tabs, late0 if i == 0 else None)
        Ws.append(W)
        Ps.append(dict(P, ln1_g=_row(ln1_g[i]), pool_scale=_row(pool_scale[i])))
        saved.append(sv)
        if i + 1 < DEPTH:
            landed = _exchange_wait(f"gather_wait{i + 1}", nxt["state"], h)

    def loss_bwd(x_, t_, g_):
        lval, vjp = jax.vjp(_loss_fn, x_, t_, g_)
        dx_, _, dg_ = vjp(jnp.ones_like(lval))
        return (dx_,), (dg_, jnp.broadcast_to(lval, (1, 128)))

    dh, g_final, loss_part = _row_call("loss", loss_bwd, D, 1, [_rows(h, D), _rows(loss_target.reshape(S, D), D)],
                                       [_p_row(_row(final_g), D)], [F32], [_a_row(D, D), _a_row(128, 128)])
    loss = lax.psum(loss_part[0, 0], MESH_AXES)

    grads = {n: [None] * DEPTH for n in WEIGHTS if n not in ("rel_bias", "final_g")}
    g_rel = jnp.zeros((REL_BUCKETS, 18), F32)
    slots = [None] * DEPTH
    pending = None
    for i in reversed(range(DEPTH)):
        started = {}

        def on_sharded_grads(G, i=i, started=started):
            parts = [_device_blocks(f"blocks_{n}{i}", n, G[n]) for n in names]
            started["state"], token = _exchange_start(f"scatter_start{i}", parts, True, G["b_gate"])
            return token[0, 0]

        dh, G = _layer_bwd(i, dh, saved[i], Ws[i], Ps[i], bias_tabs, onehots, on_sharded_grads)
        if pending is not None:
            j, st = pending
            slots[j] = _exchange_wait(f"scatter_wait{j}", st, dh)
        pending = (i, started["state"])
        g_rel = g_rel + G.pop("rel_bias")
        for n, g in G.items():
            grads[n][i] = g
    grad_x = dh.reshape(1, S, D)
    local = {n: jnp.stack(grads[n]) for n in grads if n not in SHARDED}
    local["rel_bias"] = g_rel
    local["final_g"] = g_final[0]
    out = {}

    def pack(d):
        flat = jnp.concatenate([d[n].reshape(-1).astype(F32) for n in REPLICATED])
        rows = -(-flat.shape[0] // (8 * 128)) * 8
        return jnp.pad(flat, (0, rows * 128 - flat.shape[0])).reshape(rows, 128)

    (rep_slots,) = _exchange("gather_small_grads", [pack(local)], scatter=False)
    rep = _adamw("adamw_small", rep_slots, pack(wts), pack(mom), pack(var))
    off = 0
    for n in REPLICATED:
        sz = int(np.prod(wts[n].shape))
        out[n] = [t.reshape(-1)[off:off + sz].reshape(wts[n].shape) for t in rep]
        off += sz

    def flat2(n):
        shp = wts[n].shape
        r, c = int(np.prod(shp[:-1])), shp[-1]
        return r, c, wts[n].reshape(r, c), mom[n].reshape(r, c), var[n].reshape(r, c)

    chain = {}
    done = rep[0][0, 0]
    for k, n in enumerate(names):
        if n in MATMUL_WEIGHTS:
            r, c, w2, m2, v2 = flat2(n)
            res = None
            for i in (3, 2, 1):
                res = _adamw(f"adamw_{n}{i}", slots[i][k].reshape(N_DEV, r // DEPTH, c), w2, m2, v2,
                             first_row=i * (r // DEPTH), prev=res)
            chain[n] = res
            done = done + res[0][-1, 0]
    slots[0] = _exchange_wait("scatter_wait0", pending[1], done.reshape(1, 1))
    for k, n in enumerate(names):
        r, c, w2, m2, v2 = flat2(n)
        if n in MATMUL_WEIGHTS:
            res = _adamw(f"adamw_{n}0", slots[0][k].reshape(N_DEV, r // DEPTH, c), w2, m2, v2, first_row=0, prev=chain[n])
        else:
            stacked = jnp.stack([slots[i][k] for i in range(DEPTH)], axis=1)
            res = _adamw("adamw_" + n, stacked.reshape(N_DEV, r, c), w2, m2, v2)
        out[n] = [t.reshape(wts[n].shape) for t in res]

    return (loss, grad_x, *[out[n][0] for n in WEIGHTS], *[out[n][1] for n in WEIGHTS],
            *[out[n][2] for n in WEIGHTS], *[out[n][3] for n in WEIGHTS])
```

```python
import functools
import math

import numpy as np
import jax
import jax.numpy as jnp
from jax import lax
from jax.experimental import pallas as pl
from jax.experimental.pallas import tpu as pltpu

F32 = jnp.float32
BF16 = jnp.bfloat16

N_DEV = 8
MESH_AXES = ("x", "y", "c")
S = 4096
D = 1024
DEPTH = 4
HEAD_DIM = 64
ATT_W = 1152
ATT_GW = 384
ATT_GROUPS = ((128, 1), (512, 4), (2048, 16))
ATT_BLOCK = 128
REL_BUCKETS = 32
REL_MAX_DISTANCE = 2048
POOL_WINDOWS = (2, 4, 8, 16)
SSD_HEADS = 16
SSD_CHUNK = 128
SSD_XBC = 1536
D_FF = 2816
IN_WIDTH = 10128
EPS = 1e-6
NEG = -1e30

OFF_GATE, OFF_POOL, OFF_Z, OFF_XBC, OFF_DT, OFF_QKV = 0, 3072, 4096, 5120, 6656, 6912
NP = 10368
DT_PAD = 128
QKV_W = 3 * 2 * HEAD_DIM

ADAM_LR, ADAM_B1, ADAM_B2, ADAM_EPS, ADAM_WD, ADAM_STEP = 0.001, 0.9, 0.999, 1e-08, 0.01, 10

VMEM_LIMIT = 52 * 1024 * 1024


def _cparams(sem=None):
    return pltpu.CompilerParams(dimension_semantics=sem, vmem_limit_bytes=VMEM_LIMIT)


def _dot(a, b, ca, cb):
    return lax.dot_general(a.astype(BF16), b.astype(BF16), (((ca,), (cb,)), ((), ())), preferred_element_type=F32)


@jax.custom_vjp
def _mm(a, b):
    return _dot(a, b, 1, 0)


def _mm_fwd(a, b):
    return _mm(a, b), (a, b)


def _mm_bwd(res, g):
    a, b = res
    return _dot(g, b, 1, 1).astype(a.dtype), _dot(a, g, 0, 0).astype(b.dtype)


_mm.defvjp(_mm_fwd, _mm_bwd)


@jax.custom_vjp
def _mm_nt(a, b):
    return _dot(a, b, 1, 1)


def _mm_nt_fwd(a, b):
    return _mm_nt(a, b), (a, b)


def _mm_nt_bwd(res, g):
    a, b = res
    return _dot(g, b, 1, 0).astype(a.dtype), _dot(g, a, 0, 0).astype(b.dtype)


_mm_nt.defvjp(_mm_nt_fwd, _mm_nt_bwd)


@jax.custom_vjp
def _mm_tn(a, b):
    return _dot(a, b, 0, 0)


def _mm_tn_fwd(a, b):
    return _mm_tn(a, b), (a, b)


def _mm_tn_bwd(res, g):
    a, b = res
    return _dot(b, g, 1, 1).astype(a.dtype), _dot(a, g, 1, 0).astype(b.dtype)


_mm_tn.defvjp(_mm_tn_fwd, _mm_tn_bwd)


def _shift_impl(x, j):
    n = x.shape[0]
    if j == 0:
        return x
    r = pltpu.roll(x, j % n, axis=0)
    t = lax.broadcasted_iota(jnp.int32, x.shape, 0)
    mask = (t >= j) if j > 0 else (t < n + j)
    return jnp.where(mask, r, 0.0)


@functools.partial(jax.custom_vjp, nondiff_argnums=(1,))
def _shift(x, j):
    return _shift_impl(x, j)


_shift.defvjp(lambda x, j: (_shift_impl(x, j), None), lambda j, _, g: (_shift_impl(g, -j),))


def _tri(lower):
    r = lax.broadcasted_iota(jnp.int32, (SSD_CHUNK, SSD_CHUNK), 0)
    c = lax.broadcasted_iota(jnp.int32, (SSD_CHUNK, SSD_CHUNK), 1)
    return (r >= c) if lower else (r <= c)


def _dot_hi(a, b):
    return lax.dot_general(a, b, (((1,), (0,)), ((), ())), precision=lax.Precision.HIGHEST,
                           preferred_element_type=F32)


@jax.custom_vjp
def _cumsum_rows(a):
    return _dot_hi(_tri(True).astype(F32), a)


_cumsum_rows.defvjp(lambda a: (_cumsum_rows(a), None), lambda _, g: (_dot_hi(_tri(False).astype(F32), g),))


@jax.custom_vjp
def _softplus(x):
    return jnp.maximum(x, 0.0) + jnp.log(1.0 + jnp.exp(-jnp.abs(x)))


_softplus.defvjp(lambda x: (_softplus(x), x), lambda x, g: (g * jax.nn.sigmoid(x),))


def _silu(x):
    return x * jax.nn.sigmoid(x)


def _rows(arr, cw, off=0, lead=None, roff=0):
    return (arr, cw, off, lead, roff)


def _tiled(name, fn, grid, tm, rows, params, outs, accs=(), out_roff=0, prev_outs=None, into=None):
    into = into or {}
    ncol, nrow = grid
    in_specs, operands = [], []
    for arr, cw, off, lead, roff in rows:
        if lead is None:
            in_specs.append(pl.BlockSpec((tm, cw), functools.partial(lambda j, i, off, roff: (roff + i, off + j),
                                                                     off=off, roff=roff)))
        else:
            in_specs.append(pl.BlockSpec((None, tm, cw), functools.partial(
                lambda j, i, off, lead, roff: (lead, roff + i, off + j), off=off, lead=lead, roff=roff)))
        operands.append(arr)
    for arr, bs, im in params:
        in_specs.append(pl.BlockSpec(bs, im))
        operands.append(arr)
    out_specs, out_shape = [], []
    for k, (n_rows, cw, dt) in enumerate(outs):
        _, coff, total = into.get(k, (None, 0, ncol * cw))
        out_specs.append(pl.BlockSpec((tm, cw), functools.partial(lambda j, i, r, c: (r + i, c + j), r=out_roff, c=coff)))
        out_shape.append(jax.ShapeDtypeStruct((n_rows, total), dt))
    for shape, bs, im in accs:
        out_specs.append(pl.BlockSpec(bs, im))
        out_shape.append(jax.ShapeDtypeStruct(shape, F32))
    n_in, n_out = len(operands), len(outs)
    aliases = {}
    earlier = dict(enumerate(prev_outs)) if prev_outs is not None else {}
    earlier.update({k: v[0] for k, v in into.items() if v[0] is not None})
    for k, p in sorted(earlier.items()):
        aliases[len(operands)] = k
        in_specs.append(pl.BlockSpec(memory_space=pl.ANY))
        operands.append(p)

    n_all = len(operands)

    def body(*refs):
        vals = [r[...] for r in refs[:n_in]]
        o_vals, a_vals = fn(*vals)
        for r, v in zip(refs[n_all:n_all + n_out], o_vals):
            r[...] = v.astype(r.dtype)
        i = pl.program_id(1)
        for r, v in zip(refs[n_all + n_out:], a_vals):
            @pl.when(i == 0)
            def _(r=r, v=v):
                r[...] = v.astype(r.dtype)

            @pl.when(i > 0)
            def _(r=r, v=v):
                r[...] += v.astype(r.dtype)

    res = pl.pallas_call(body, grid=grid, in_specs=in_specs, out_specs=out_specs, out_shape=out_shape, name=name,
                         input_output_aliases=aliases, compiler_params=_cparams(("arbitrary", "arbitrary")))(*operands)
    return list(res)


def _with_vjp(fn, n_prim, want_out, want_acc):
    def f(*args):
        prim, g = args[:n_prim], args[n_prim:]
        outs, vjp = jax.vjp(lambda *a: fn(*a), *prim)
        d = vjp(tuple(gi.astype(o.dtype) for gi, o in zip(g, outs)))
        return tuple(d[k] for k in want_out), tuple(d[k] for k in want_acc)
    return f


def _p_row(arr, cw, off=0):
    return (arr, (1, cw), functools.partial(lambda j, i, off: (0, off + j), off=off))


def _a_row(n, cw):
    return ((1, n), (1, cw), lambda j, i: (0, j))


def _pick(n, cap, mult):
    best = None
    for t in range(mult, min(n, cap) + 1, mult):
        if n % t == 0:
            best = t
    return best if best is not None else n


def _matmul(name, a, b, mode, add=None, out_dtype=F32):
    if mode == "nn":
        (M, K), N = a.shape, b.shape[1]
    elif mode == "nt":
        (M, K), N = a.shape, b.shape[0]
    else:
        (K, M), N = a.shape, b.shape[1]
    tn = _pick(N, 1536, 128)
    k_cap = 2048 if mode == "tn" else 3456
    tk = K if K <= k_cap else _pick(K, k_cap, 128)
    nk = K // tk
    tm = _pick(M, 1408, 128) if mode == "tn" else _pick(M, 1024 if nk > 1 else 512, 8)
    a_bytes, b_bytes = a.size * a.dtype.itemsize, b.size * b.dtype.itemsize
    swap = nk == 1 and a_bytes * (N // tn) + b_bytes < b_bytes * (M // tm) + a_bytes
    ij = (lambda g0, g1: (g1, g0)) if swap else (lambda g0, g1: (g0, g1))

    def spec(block, index):
        return pl.BlockSpec(block, lambda g0, g1, k: index(*ij(g0, g1), k))

    if mode == "nn":
        a_spec = spec((tm, tk), lambda i, j, k: (i, k))
        b_spec = spec((tk, tn), lambda i, j, k: (k, j))
        ca, cb = 1, 0
    elif mode == "nt":
        a_spec = spec((tm, tk), lambda i, j, k: (i, k))
        b_spec = spec((tn, tk), lambda i, j, k: (j, k))
        ca, cb = 1, 1
    else:
        a_spec = spec((tk, tm), lambda i, j, k: (k, i))
        b_spec = spec((tk, tn), lambda i, j, k: (k, j))
        ca, cb = 0, 0
    in_specs, operands = [a_spec, b_spec], [a, b]
    if add is not None:
        in_specs.append(spec((tm, tn), lambda i, j, k: (i, j)))
        operands.append(add)

    def finish(r, refs, o_ref):
        if add is not None:
            r = r + refs[2][...]
        o_ref[...] = r.astype(o_ref.dtype)

    def body_single(*refs):
        finish(_dot(refs[0][...], refs[1][...], ca, cb), refs, refs[-1])

    def body_multi(*refs):
        o_ref, acc_ref = refs[-2], refs[-1]
        k = pl.program_id(2)
        d = _dot(refs[0][...], refs[1][...], ca, cb)

        @pl.when(k == 0)
        def _():
            acc_ref[...] = d

        @pl.when(jnp.logical_and(k > 0, k < nk - 1))
        def _():
            acc_ref[...] += d

        @pl.when(k == nk - 1)
        def _():
            finish(acc_ref[...] + d, refs, o_ref)

    grid = (N // tn, M // tm, nk) if swap else (M // tm, N // tn, nk)
    return pl.pallas_call(
        body_single if nk == 1 else body_multi, grid=grid, in_specs=in_specs,
        out_specs=spec((tm, tn), lambda i, j, k: (i, j)),
        out_shape=jax.ShapeDtypeStruct((M, N), out_dtype),
        scratch_shapes=[] if nk == 1 else [pltpu.VMEM((tm, tn), F32)], name=name,
        compiler_params=_cparams(("parallel", "parallel", "arbitrary")))(*operands)


def _seg_copies(segs, c):
    out = []
    for lo, hi, dst in segs:
        n = lo
        while n < hi:
            p = n // c
            w = min(hi, (p + 1) * c) - n
            out.append((p, n - p * c, w, dst + n - lo))
            n += w
    return out


def _col_assemble(name, blocks, copies, zeros, n_out):
    _, R, c = blocks.shape
    tm = R if R <= 128 else 128

    def body(b_ref, o_ref):
        for p, s, w, d in copies:
            o_ref[:, d:d + w] = b_ref[p, :, s:s + w]
        for lo, hi in zeros:
            o_ref[:, lo:hi] = jnp.zeros((tm, hi - lo), o_ref.dtype)

    return pl.pallas_call(
        body, grid=(R // tm,), in_specs=[pl.BlockSpec((N_DEV, tm, c), lambda i: (0, i, 0))],
        out_specs=pl.BlockSpec((tm, n_out), lambda i: (i, 0)),
        out_shape=jax.ShapeDtypeStruct((R, n_out), blocks.dtype), name=name, compiler_params=_cparams(("parallel",)))(blocks)


def _col_split(name, full, copies, c, dtype):
    R, n = full.shape
    tm = R if R <= 128 else 128

    def body(f_ref, o_ref):
        for p, s, w, d in copies:
            o_ref[p, :, s:s + w] = f_ref[:, d:d + w].astype(dtype)

    return pl.pallas_call(
        body, grid=(R // tm,), in_specs=[pl.BlockSpec((tm, n), lambda i: (i, 0))],
        out_specs=pl.BlockSpec((N_DEV, tm, c), lambda i: (0, i, 0)),
        out_shape=jax.ShapeDtypeStruct((N_DEV, R, c), dtype), name=name, compiler_params=_cparams(("parallel",)))(full)


def _rmsnorm_fn(x, g):
    x = x.astype(F32)
    return (x * lax.rsqrt(jnp.mean(x * x, axis=-1, keepdims=True) + EPS) * g,)


def _gate_merge_fn(g0, g1, g2, ya, yb, yc, b0, b1, b2):
    return (jax.nn.sigmoid(g0 + b0) * ya + jax.nn.sigmoid(g1 + b1) * yb + jax.nn.sigmoid(g2 + b2) * yc,)


def _gated_norm_fn(y, z, w):
    t = y * _silu(z)
    return (t * lax.rsqrt(jnp.mean(t * t, axis=-1, keepdims=True) + EPS) * w,)


def _att_merge_fn(o0, o1, o2, l0, l1, l2):
    m = lax.stop_gradient(jnp.maximum(jnp.maximum(l0, l1), l2))
    e0, e1, e2 = jnp.exp(l0 - m), jnp.exp(l1 - m), jnp.exp(l2 - m)
    return ((e0 * o0 + e1 * o1 + e2 * o2) / (e0 + e1 + e2),)


def _loss_fn(x, tgt, g):
    (y,) = _rmsnorm_fn(x, g)
    err = y - tgt
    return 0.5 * jnp.sum(jnp.mean(err * err, axis=-1, keepdims=True), axis=0, keepdims=True)


def _pool_fn(x, wg, scale):
    g = pl.program_id(0)
    s2 = x + _shift(x, 1)
    s4 = s2 + _shift(s2, 2)
    s8 = s4 + _shift(s4, 4)
    s16 = s8 + _shift(s8, 8)
    win = ((g == 0).astype(F32) * s2 + (g == 1).astype(F32) * s4 + (g == 2).astype(F32) * s8
           + (g == 3).astype(F32) * s16)
    t = lax.broadcasted_iota(jnp.int32, (x.shape[0], 1), 0) + 1
    cnt = jnp.minimum(t, jnp.left_shift(2, g)).astype(F32)
    d = win / cnt - x
    return (_mm(d, wg.reshape(256, 256)) * scale,)


def _dwconv(x, taps, b):
    k = len(taps)
    y = taps[k - 1] * x + b
    for i in range(k - 1):
        y = y + taps[i] * _shift(x, k - 1 - i)
    return y


def _ssd_conv_fn(x, w0, w1, w2, w3, b):
    return (_silu(_dwconv(x, (w0, w1, w2, w3), b)),)


def _ffn_act_fn(xa, xv, a0, a1, a2, ab, v0, v1, v2, vb):
    xa, xv = xa.astype(F32), xv.astype(F32)
    return (_silu(_dwconv(xa, (a0, a1, a2), ab)) * _dwconv(xv, (v0, v1, v2), vb),)


@jax.custom_vjp
def _halves(x):
    return x[:ATT_BLOCK], x[ATT_BLOCK:]


_halves.defvjp(lambda x: (_halves(x), None), lambda _, g: (jnp.concatenate([g[0], g[1]], axis=0),))


def _att_block(q, kp, kc, vp, vc, bpa, bpb, bca, bcb, prev_ok):
    n = ATT_BLOCK
    lane = lax.broadcasted_iota(jnp.int32, (1, 2 * HEAD_DIM), 1)
    ma = (lane < HEAD_DIM).astype(F32)
    mb = 1.0 - ma
    q = q.astype(F32) * (1.0 / math.sqrt(HEAD_DIM))
    q2 = jnp.concatenate([q * ma, q * mb], axis=0)
    qi = lax.broadcasted_iota(jnp.int32, (2 * n, n), 0) & (n - 1)
    kj = lax.broadcasted_iota(jnp.int32, (2 * n, n), 1)
    sp = jnp.where(jnp.logical_and(kj >= qi, prev_ok), _mm_nt(q2, kp) + jnp.concatenate([bpa, bpb], axis=0), NEG)
    sc = jnp.where(kj <= qi, _mm_nt(q2, kc) + jnp.concatenate([bca, bcb], axis=0), NEG)
    m = lax.stop_gradient(jnp.maximum(jnp.max(sp, axis=1, keepdims=True), jnp.max(sc, axis=1, keepdims=True)))
    pp = jnp.exp(sp - m)
    pc = jnp.exp(sc - m)
    l = jnp.sum(pp, axis=1, keepdims=True) + jnp.sum(pc, axis=1, keepdims=True)
    oa, ob = _halves((_mm(pp, vp) + _mm(pc, vc)) / l)
    la, lb = _halves((m + jnp.log(l)) * jnp.ones((1, 2 * HEAD_DIM), F32))
    return oa * ma + ob * mb, la * ma + lb * mb


def _att_slab(dil):
    nbk = 4 if dil == 1 else 1
    t = ATT_BLOCK * dil * nbk
    return nbk, t, S // t


def _att_in_specs(gi, t):
    def spec(which, prev):
        col = OFF_QKV // 128 + gi * 9 + which

        def index(p, j, col=col, prev=prev):
            jj = jnp.minimum(j, S // t - 1)
            return (jnp.maximum(jj - 1, 0) if prev else jj, col + 3 * p)
        return pl.BlockSpec((t, 2 * HEAD_DIM), index)
    return [spec(0, False), spec(1, False), spec(1, True), spec(2, False), spec(2, True)]


def _bias_specs():
    return [pl.BlockSpec((None, ATT_BLOCK, ATT_BLOCK), functools.partial(lambda p, j, hh: (2 * p + hh, 0, 0), hh=hh))
            for hh in (0, 1)]


def _att_units(dil, nbk, body):
    def per_residue(r, carry):
        for b in range(nbk):
            rows = pl.ds(b * ATT_BLOCK * dil + r, ATT_BLOCK, stride=dil)
            prev = pl.ds(((b - 1) % nbk) * ATT_BLOCK * dil + r, ATT_BLOCK, stride=dil)
            body(b, rows, prev, b > 0)
        return carry
    if dil == 1:
        per_residue(0, 0)
    else:
        lax.fori_loop(0, dil, per_residue, 0)


def _att_fwd(name, proj, gi, bias_p, bias_c):
    dil = ATT_GROUPS[gi][1]
    nbk, t, ns = _att_slab(dil)
    bsp = _bias_specs()
    out_spec = pl.BlockSpec((t, 2 * HEAD_DIM), lambda p, j: (j, p))

    def body(q_ref, kc_ref, kp_ref, vc_ref, vp_ref, bpa, bpb, bca, bcb, o_ref, l_ref):
        first = pl.program_id(1) == 0
        biases = (bpa[...], bpb[...], bca[...], bcb[...])

        def unit(b, rows, prev, in_slab):
            kp = kc_ref[prev, :] if in_slab else kp_ref[prev, :]
            vp = vc_ref[prev, :] if in_slab else vp_ref[prev, :]
            prev_ok = True if in_slab else jnp.logical_not(first)
            o, lse = _att_block(q_ref[rows, :], kp, kc_ref[rows, :], vp, vc_ref[rows, :], *biases, prev_ok)
            o_ref[rows, :] = o
            l_ref[rows, :] = lse

        _att_units(dil, nbk, unit)

    shp = jax.ShapeDtypeStruct((S, ATT_GW), F32)
    return pl.pallas_call(
        body, grid=(3, ns), in_specs=_att_in_specs(gi, t) + [bsp[0], bsp[1], bsp[0], bsp[1]],
        out_specs=[out_spec, out_spec], out_shape=[shp, shp], name=name,
        compiler_params=_cparams(("arbitrary",) * 2))(proj, proj, proj, proj, proj, bias_p, bias_p, bias_c, bias_c)


def _att_bwd(name, proj, gi, bias_p, bias_c, do, dl, dproj):
    dil = ATT_GROUPS[gi][1]
    nbk, t, ns = _att_slab(dil)
    bsp = _bias_specs()
    blk = (t, 2 * HEAD_DIM)
    cur = pl.BlockSpec(blk, lambda p, j: (jnp.minimum(j, ns - 1), p))
    done = pl.BlockSpec((t, QKV_W), lambda p, j: (jnp.maximum(j - 1, 0), OFF_QKV // QKV_W + gi * 3 + p))
    gsp = pl.BlockSpec((None, ATT_BLOCK, ATT_BLOCK), lambda p, j: (p, 0, 0))

    def body(q_ref, kc_ref, kp_ref, vc_ref, vp_ref, bpa, bpb, bca, bcb, do_ref, dl_ref, _,
             dqkv_ref, gpa, gpb, gca, gcb, accq, acck, accv):
        j = pl.program_id(1)
        mine, other = acck.at[j % 2], acck.at[1 - j % 2]
        mine_v, other_v = accv.at[j % 2], accv.at[1 - j % 2]
        dq_ref, other_q = accq.at[j % 2], accq.at[1 - j % 2]

        @pl.when(j == 0)
        def _():
            for g in (gpa, gpb, gca, gcb):
                g[...] = jnp.zeros_like(g)
            other[...] = jnp.zeros_like(other)
            other_v[...] = jnp.zeros_like(other_v)
            other_q[...] = jnp.zeros_like(other_q)

        @pl.when(j < ns)
        def _():
            mine[...] = jnp.zeros_like(mine)
            mine_v[...] = jnp.zeros_like(mine_v)
            biases = (bpa[...], bpb[...], bca[...], bcb[...])

            def unit(b, rows, prev, in_slab):
                kp = kc_ref[prev, :] if in_slab else kp_ref[prev, :]
                vp = vc_ref[prev, :] if in_slab else vp_ref[prev, :]
                prev_ok = True if in_slab else j > 0
                prim = (q_ref[rows, :], kp, kc_ref[rows, :], vp, vc_ref[rows, :]) + biases
                _, vjp = jax.vjp(lambda *a: _att_block(*a, prev_ok), *prim)
                dq, dkp, dkc, dvp, dvc, dpa, dpb, dca, dcb = vjp((do_ref[rows, :], dl_ref[rows, :]))
                dq_ref[rows, :] = dq
                mine[rows, :] += dkc
                mine_v[rows, :] += dvc
                tgt, tgt_v = (mine, mine_v) if in_slab else (other, other_v)
                tgt[prev, :] += dkp
                tgt_v[prev, :] += dvp
                gpa[...] += dpa
                gpb[...] += dpb
                gca[...] += dca
                gcb[...] += dcb

            _att_units(dil, nbk, unit)

        w = 2 * HEAD_DIM
        dqkv_ref[:, 0:w] = other_q[...].astype(BF16)
        dqkv_ref[:, w:2 * w] = other[...].astype(BF16)
        dqkv_ref[:, 2 * w:3 * w] = other_v[...].astype(BF16)

    gshp = jax.ShapeDtypeStruct((3, ATT_BLOCK, ATT_BLOCK), F32)
    res = pl.pallas_call(
        body, grid=(3, ns + 1),
        in_specs=_att_in_specs(gi, t) + [bsp[0], bsp[1], bsp[0], bsp[1], cur, cur, pl.BlockSpec(memory_space=pl.ANY)],
        out_specs=[done, gsp, gsp, gsp, gsp],
        out_shape=[jax.ShapeDtypeStruct((S, NP), BF16), gshp, gshp, gshp, gshp],
        input_output_aliases={11: 0},
        scratch_shapes=[pltpu.VMEM((2,) + blk, F32)] * 3, name=name,
        compiler_params=_cparams(("arbitrary",) * 2))(proj, proj, proj, proj, proj, bias_p, bias_p, bias_c, bias_c, do, dl,
                                                      dproj)
    dproj, gpa, gpb, gca, gcb = res
    heads = lambda a, b: jnp.stack([a, b], axis=1).reshape(6, ATT_BLOCK, ATT_BLOCK)
    return dproj, heads(gpa, gpb), heads(gca, gcb)


N_PAIR = SSD_HEADS // 2


def _ssd_chunk(xs, bs, cs_in, dt_raw, hs, a_row, dtb_row, ds):
    lane = lax.broadcasted_iota(jnp.int32, (1, 128), 1)
    row = lax.broadcasted_iota(jnp.int32, (128, 1), 0)
    tril = _tri(True)
    dt = _softplus(dt_raw + dtb_row)
    acs = _cumsum_rows(dt * a_row)
    acs_t = acs.T
    gmat = [_mm_nt(cs_in[g], bs[g]) for g in range(2)]
    lo = lane < HEAD_DIM
    lo_r = row < HEAD_DIM
    last = (row == SSD_CHUNK - 1).astype(F32)
    ys, hn = [], []
    for p in range(N_PAIR):
        g = p // (N_PAIR // 2)
        col, dtc, mm, clast = [], [], [], []
        for hh in range(2):
            h = 2 * p + hh
            oh = (lane == h).astype(F32)
            c_col = jnp.sum(acs * oh, axis=1, keepdims=True)
            c_row = jnp.sum(acs_t * (row == h).astype(F32), axis=0, keepdims=True)
            col.append(c_col)
            dtc.append(jnp.sum(dt * oh, axis=1, keepdims=True))
            clast.append(jnp.sum(c_col * last, axis=0, keepdims=True))
            mm.append(gmat[g] * jnp.exp(jnp.where(tril, c_col - c_row, NEG)))
        x = xs[p]
        xd = x * jnp.where(lo, dtc[0], dtc[1])
        y = jnp.where(lo, _mm(mm[0], xd), _mm(mm[1], xd))
        y = y + jnp.where(lo, jnp.exp(col[0]), jnp.exp(col[1])) * _mm_nt(cs_in[g], hs[p])
        ys.append(y + ds[p] * x)
        dec = jnp.where(lo, jnp.exp(clast[0] - col[0]), jnp.exp(clast[1] - col[1]))
        hn.append(hs[p] * jnp.where(lo_r, jnp.exp(clast[0]), jnp.exp(clast[1])) + _mm_tn(xd * dec, bs[g]))
    return tuple(ys), tuple(hn)


def _ssd_load(xbc_ref, dt_ref, a_ref, dtb_ref, d_ref):
    xs = tuple(xbc_ref[:, 128 * p:128 * (p + 1)] for p in range(N_PAIR))
    bs = tuple(xbc_ref[:, D + 128 * g:D + 128 * (g + 1)] for g in range(2))
    cs = tuple(xbc_ref[:, D + 256 + 128 * g:D + 256 + 128 * (g + 1)] for g in range(2))
    ds = tuple(d_ref[:, 128 * p:128 * (p + 1)] for p in range(N_PAIR))
    return xs, bs, cs, dt_ref[...], a_ref[...], dtb_ref[...], ds


def _ssd_fwd(name, xbc_c, proj, a_row, dtb_row, d_exp):
    nc = S // SSD_CHUNK
    prow = lambda n: pl.BlockSpec((1, n), lambda c: (0, 0))

    def body(xbc_ref, dt_ref, a_ref, dtb_ref, d_ref, y_ref, st_ref, h_ref):
        @pl.when(pl.program_id(0) == 0)
        def _():
            h_ref[...] = jnp.zeros_like(h_ref)

        xs, bs, cs, dt_raw, a, dtb, ds = _ssd_load(xbc_ref, dt_ref, a_ref, dtb_ref, d_ref)
        hs = tuple(h_ref[p] for p in range(N_PAIR))
        ys, hn = _ssd_chunk(xs, bs, cs, dt_raw, hs, a, dtb, ds)
        for p in range(N_PAIR):
            y_ref[:, 128 * p:128 * (p + 1)] = ys[p]
            st_ref[p] = hs[p]
            h_ref[p] = hn[p]

    return pl.pallas_call(
        body, grid=(nc,),
        in_specs=[pl.BlockSpec((SSD_CHUNK, SSD_XBC), lambda c: (c, 0)),
                  pl.BlockSpec((SSD_CHUNK, DT_PAD), lambda c: (c, OFF_DT // DT_PAD)),
                  prow(128), prow(128), prow(D)],
        out_specs=[pl.BlockSpec((SSD_CHUNK, D), lambda c: (c, 0)),
                   pl.BlockSpec((None, N_PAIR, 128, 128), lambda c: (c, 0, 0, 0))],
        out_shape=[jax.ShapeDtypeStruct((S, D), F32), jax.ShapeDtypeStruct((nc, N_PAIR, 128, 128), F32)],
        scratch_shapes=[pltpu.VMEM((N_PAIR, 128, 128), F32)], name=name,
        compiler_params=_cparams(("arbitrary",)))(xbc_c, proj, a_row, dtb_row, d_exp)


def _ssd_bwd(name, xbc_c, proj, states, dy, a_row, dtb_row, d_exp, dproj):
    nc = S // SSD_CHUNK
    prow = lambda n: pl.BlockSpec((1, n), lambda i: (0, 0))
    rc = lambda i: nc - 1 - i

    def body(xbc_ref, dt_ref, st_ref, dy_ref, a_ref, dtb_ref, d_ref, _, dxbc_ref, ddt_ref, da_ref, ddtb_ref, dd_ref, e_ref):
        i = pl.program_id(0)

        @pl.when(i == 0)
        def _():
            e_ref[...] = jnp.zeros_like(e_ref)
            da_ref[...] = jnp.zeros_like(da_ref)
            ddtb_ref[...] = jnp.zeros_like(ddtb_ref)
            dd_ref[...] = jnp.zeros_like(dd_ref)

        xs, bs, cs, dt_raw, a, dtb, ds = _ssd_load(xbc_ref, dt_ref, a_ref, dtb_ref, d_ref)
        hs = tuple(st_ref[p] for p in range(N_PAIR))
        _, vjp = jax.vjp(_ssd_chunk, xs, bs, cs, dt_raw, hs, a, dtb, ds)
        dys = tuple(dy_ref[:, 128 * p:128 * (p + 1)] for p in range(N_PAIR))
        es = tuple(e_ref[p] for p in range(N_PAIR))
        dxs, dbs, dcs, ddt, dhs, da, ddtb, dds = vjp((dys, es))
        for p in range(N_PAIR):
            dxbc_ref[:, 128 * p:128 * (p + 1)] = dxs[p]
            e_ref[p] = dhs[p]
            dd_ref[:, 128 * p:128 * (p + 1)] += dds[p]
        for g in range(2):
            dxbc_ref[:, D + 128 * g:D + 128 * (g + 1)] = dbs[g]
            dxbc_ref[:, D + 256 + 128 * g:D + 256 + 128 * (g + 1)] = dcs[g]
        ddt_ref[:, :DT_PAD] = ddt.astype(BF16)
        ddt_ref[:, DT_PAD:] = jnp.zeros((SSD_CHUNK, OFF_QKV - OFF_DT - DT_PAD), BF16)
        da_ref[...] += da
        ddtb_ref[...] += ddtb

    dt_w = OFF_QKV - OFF_DT
    return pl.pallas_call(
        body, grid=(nc,),
        in_specs=[pl.BlockSpec((SSD_CHUNK, SSD_XBC), lambda i: (rc(i), 0)),
                  pl.BlockSpec((SSD_CHUNK, DT_PAD), lambda i: (rc(i), OFF_DT // DT_PAD)),
                  pl.BlockSpec((None, N_PAIR, 128, 128), lambda i: (rc(i), 0, 0, 0)),
                  pl.BlockSpec((SSD_CHUNK, D), lambda i: (rc(i), 0)),
                  prow(128), prow(128), prow(D), pl.BlockSpec(memory_space=pl.ANY)],
        out_specs=[pl.BlockSpec((SSD_CHUNK, SSD_XBC), lambda i: (rc(i), 0)),
                   pl.BlockSpec((SSD_CHUNK, dt_w), lambda i: (rc(i), OFF_DT // dt_w)),
                   prow(128), prow(128), prow(D)],
        out_shape=[jax.ShapeDtypeStruct((S, SSD_XBC), F32), jax.ShapeDtypeStruct((S, NP), BF16),
                   jax.ShapeDtypeStruct((1, 128), F32), jax.ShapeDtypeStruct((1, 128), F32),
                   jax.ShapeDtypeStruct((1, D), F32)],
        input_output_aliases={7: 1},
        scratch_shapes=[pltpu.VMEM((N_PAIR, 128, 128), F32)], name=name,
        compiler_params=_cparams(("arbitrary",)))(xbc_c, proj, states, dy, a_row, dtb_row, d_exp, dproj)


def _exchange(name, arrays, scatter):
    n = len(arrays)
    flips = [(dx, dy, dc) for dx in (0, 1) for dy in (0, 1) for dc in (0, 1) if dx or dy or dc]

    def body(*refs):
        ins, outs = refs[:n], refs[n:2 * n]
        send_sems, recv_sems, loc_sems = refs[2 * n:]
        x, y, c = lax.axis_index("x"), lax.axis_index("y"), lax.axis_index("c")
        me = 4 * x + 2 * y + c
        peers = []
        for dx, dy, dc in flips:
            px, py, pc = (1 - x if dx else x), (1 - y if dy else y), (1 - c if dc else c)
            peers.append(((px, py, pc), 4 * px + 2 * py + pc))

        def remote(k, j, landed_from):
            dev, pid = peers[j]
            src = ins[k].at[pid] if scatter else ins[k]
            return pltpu.make_async_remote_copy(
                src_ref=src, dst_ref=outs[k].at[landed_from], send_sem=send_sems.at[k, j], recv_sem=recv_sems.at[k, j],
                device_id=dev, device_id_type=pl.DeviceIdType.MESH)

        local = [pltpu.make_async_copy(ins[k].at[me] if scatter else ins[k], outs[k].at[me], loc_sems.at[k])
                 for k in range(n)]
        for cp in local:
            cp.start()
        for k in range(n):
            for j in range(len(flips)):
                remote(k, j, me).start()
        for cp in local:
            cp.wait()
        for k in range(n):
            for j in range(len(flips)):
                remote(k, j, me).wait_send()
                remote(k, j, peers[j][1]).wait_recv()

    hbm = pl.BlockSpec(memory_space=pltpu.HBM)
    out_shape = [jax.ShapeDtypeStruct(a.shape if scatter else (N_DEV,) + a.shape, a.dtype) for a in arrays]
    res = pl.pallas_call(
        body, in_specs=[hbm] * n, out_specs=[hbm] * n, out_shape=out_shape, name=name,
        scratch_shapes=[pltpu.SemaphoreType.DMA((n, len(flips))), pltpu.SemaphoreType.DMA((n, len(flips))),
                        pltpu.SemaphoreType.DMA((n,))])(*arrays)
    return list(res)


def _peer_copies(ins, lands, send_sems, recv_sems, loc_sems, scatter):
    n = len(ins)
    flips = [(dx, dy, dc) for dx in (0, 1) for dy in (0, 1) for dc in (0, 1) if dx or dy or dc]
    x, y, c = lax.axis_index("x"), lax.axis_index("y"), lax.axis_index("c")
    me = 4 * x + 2 * y + c
    peers = []
    for dx, dy, dc in flips:
        px, py, pc = (1 - x if dx else x), (1 - y if dy else y), (1 - c if dc else c)
        peers.append(((px, py, pc), 4 * px + 2 * py + pc))

    def remote(k, j, slot):
        dev, pid = peers[j]
        return pltpu.make_async_remote_copy(
            src_ref=ins[k].at[pid] if scatter else ins[k], dst_ref=lands[k].at[slot],
            send_sem=send_sems.at[k * N_FLIP + j], recv_sem=recv_sems.at[k * N_FLIP + j],
            device_id=dev, device_id_type=pl.DeviceIdType.MESH)

    local = [pltpu.make_async_copy(ins[k].at[me] if scatter else ins[k], lands[k].at[me], loc_sems.at[k])
             for k in range(n)]
    pairs = [(k, j) for k in range(n) for j in range(len(flips))]
    sent = lambda k, j: remote(k, j, me)
    landed = lambda k, j: remote(k, j, peers[j][1])
    return local, pairs, sent, landed


_HBM = pl.BlockSpec(memory_space=pltpu.HBM)
_SEM = pl.BlockSpec(memory_space=pltpu.SEMAPHORE)
N_FLIP = N_DEV - 1


def _exchange_start(name, arrays, scatter, after):
    n = len(arrays)
    arrays = [pltpu.with_memory_space_constraint(a, pltpu.HBM) for a in arrays]
    lands = [pltpu.with_memory_space_constraint(
        lax.empty(a.shape if scatter else (N_DEV,) + a.shape, a.dtype), pltpu.HBM) for a in arrays]

    def body(*refs):
        ins, lnd = refs[:n], refs[n:2 * n]
        send_sems, recv_sems, loc_sems = refs[2 * n + 1:2 * n + 4]
        token = refs[-1]
        local, pairs, sent, _ = _peer_copies(ins, lnd, send_sems, recv_sems, loc_sems, scatter)
        for cp in local:
            cp.start()
        for k, j in pairs:
            sent(k, j).start()
        token[...] = jnp.zeros_like(token)

    res = pl.pallas_call(
        body, name=name,
        in_specs=[_HBM] * (2 * n) + [pl.BlockSpec(memory_space=pl.ANY)],
        out_specs=[_SEM, _SEM, _SEM] + [_HBM] * (2 * n) + [pl.BlockSpec(memory_space=pltpu.VMEM)],
        out_shape=[pltpu.SemaphoreType.DMA((n * N_FLIP,)), pltpu.SemaphoreType.DMA((n * N_FLIP,)), pltpu.SemaphoreType.DMA((n,))]
        + [pltpu.HBM(a.shape, a.dtype) for a in arrays] + [pltpu.HBM(a.shape, a.dtype) for a in lands]
        + [jax.ShapeDtypeStruct((8, 128), F32)],
        input_output_aliases={k: 3 + k for k in range(2 * n)},
        compiler_params=pltpu.CompilerParams(has_side_effects=pltpu.SideEffectType.DATAFLOW_SIDE_EFFECTING),
    )(*arrays, *lands, after)
    return (res[:3], res[3:3 + n], res[3 + n:3 + 2 * n], scatter), res[-1]


def _exchange_wait(name, state, after):
    sems, ins_thru, lands_thru, scatter = state
    n = len(ins_thru)

    def body(*refs):
        ins, lnd = refs[:n], refs[n:2 * n]
        send_sems, recv_sems, loc_sems = refs[2 * n:2 * n + 3]
        local, pairs, sent, landed = _peer_copies(ins, lnd, send_sems, recv_sems, loc_sems, scatter)
        for cp in local:
            cp.wait()
        for k, j in pairs:
            sent(k, j).wait_send()
            landed(k, j).wait_recv()

    res = pl.pallas_call(
        body, name=name,
        in_specs=[_HBM] * (2 * n) + [_SEM, _SEM, _SEM] + [pl.BlockSpec(memory_space=pl.ANY)],
        out_specs=[_HBM] * (2 * n),
        out_shape=[pltpu.HBM(a.shape, a.dtype) for a in ins_thru] + [pltpu.HBM(a.shape, a.dtype) for a in lands_thru],
        input_output_aliases={k: k for k in range(2 * n)},
        compiler_params=pltpu.CompilerParams(has_side_effects=pltpu.SideEffectType.DATAFLOW_SIDE_EFFECTING),
    )(*ins_thru, *lands_thru, *sems, after)
    return list(res[n:])


def _adamw_fn(*vals):
    slots, (w, m, v) = vals[:N_DEV], vals[N_DEV:]
    g = slots[0].astype(F32)
    for s in slots[1:]:
        g = g + s.astype(F32)
    m2 = ADAM_B1 * m + (1.0 - ADAM_B1) * g
    v2 = ADAM_B2 * v + (1.0 - ADAM_B2) * (g * g)
    m_hat = m2 / (1.0 - ADAM_B1 ** ADAM_STEP)
    v_hat = v2 / (1.0 - ADAM_B2 ** ADAM_STEP)
    delta = -ADAM_LR * (m_hat / (jnp.sqrt(v_hat) + ADAM_EPS) + ADAM_WD * w)
    return (g, delta, m2, v2), ()


def _adamw(name, slots, w, m, v, first_row=0, prev=None):
    R, C = slots.shape[1:]
    tm = R if R <= 128 else _pick(R, 128 if C > D else 256, 8)
    rows = ([_rows(slots, C, lead=s) for s in range(N_DEV)]
            + [_rows(a, C, roff=first_row // tm) for a in (w, m, v)])
    return _tiled(name, _adamw_fn, (1, R // tm), tm, rows, [], [(w.shape[0], C, F32)] * 4,
                  out_roff=first_row // tm, prev_outs=prev)


def _bucket_onehots():
    out = []
    qi = jnp.arange(ATT_BLOCK)[:, None]
    kj = jnp.arange(ATT_BLOCK)[None, :]
    max_exact = REL_BUCKETS // 2
    for _, dil in ATT_GROUPS:
        parts = []
        for rel in (qi + ATT_BLOCK - kj, qi - kj):
            dist = jnp.clip(rel, 0, None) * dil
            nf = jnp.maximum(dist, 1).astype(F32)
            large = max_exact + (jnp.log(nf / max_exact) / math.log(REL_MAX_DISTANCE / max_exact)
                                 * (REL_BUCKETS - max_exact)).astype(jnp.int32)
            large = jnp.minimum(large, REL_BUCKETS - 1)
            bucket = jnp.where(dist < max_exact, dist, large)
            parts.append((bucket[:, :, None] == jnp.arange(REL_BUCKETS)[None, None, :]).astype(F32))
        out.append(jnp.stack(parts))
    return out


SHARDED = ("w_in", "w_a", "pool_w", "w_b", "ssd_conv_w", "w_c", "w_o", "ffn_w_up", "ffn_conv_w", "ffn_w_down")
MATMUL_WEIGHTS = ("w_in", "w_a", "pool_w", "w_b", "w_c", "w_o", "ffn_w_up", "ffn_w_down")
ROW_SHARDED = ("w_b", "w_c", "w_o", "ffn_w_down")
W_IN_SEGS = tuple(
    (which * ATT_W + unit * 128, which * ATT_W + (unit + 1) * 128, OFF_QKV + unit * QKV_W + which * 128)
    for unit in range(9) for which in range(3)
) + ((3456, 4480, OFF_POOL), (4480, 5504, OFF_Z), (5504, 7040, OFF_XBC), (7040, 7056, OFF_DT), (7056, IN_WIDTH, OFF_GATE))
COL_SHARDED = {
    "w_in": (IN_WIDTH // N_DEV, W_IN_SEGS, ((OFF_DT + SSD_HEADS, OFF_QKV),), NP),
    "w_a": (D // N_DEV, ((0, D, 0),), (), D),
    "ffn_w_up": (2 * D_FF // N_DEV, ((0, 2 * D_FF, 0),), (), 2 * D_FF),
    "ssd_conv_w": (SSD_XBC // N_DEV, ((0, SSD_XBC, 0),), (), SSD_XBC),
    "ffn_conv_w": (2 * D_FF // N_DEV, ((0, 2 * D_FF, 0),), (), 2 * D_FF),
}
REPLICATED = ("rel_bias", "ln1_g", "b_gate", "pool_scale", "ssd_conv_b", "ssd_dt_bias", "ssd_a_log", "ssd_d",
              "ssd_norm_w", "ln2_g", "ffn_conv_b", "final_g")
WEIGHTS = ("rel_bias", "ln1_g", "w_in", "b_gate", "w_a", "pool_w", "pool_scale", "w_b", "ssd_conv_w", "ssd_conv_b",
           "ssd_dt_bias", "ssd_a_log", "ssd_d", "ssd_norm_w", "w_c", "w_o", "ln2_g", "ffn_w_up", "ffn_conv_w",
           "ffn_conv_b", "ffn_w_down", "final_g")


def _local_weight(name, n, blocks):
    if n in COL_SHARDED:
        c, segs, zeros, width = COL_SHARDED[n]
        return _col_assemble(name, blocks, _seg_copies(segs, c), zeros, width)
    if n in ROW_SHARDED:
        return blocks.reshape(-1, blocks.shape[-1])
    return blocks


def _device_blocks(name, n, g):
    if n in COL_SHARDED:
        c, segs, _, _ = COL_SHARDED[n]
        return _col_split(name, g, _seg_copies(segs, c), c, BF16)
    if n in ROW_SHARDED:
        return g.reshape(N_DEV, g.shape[0] // N_DEV, g.shape[1]).astype(BF16)
    return g.astype(BF16)


def _row(v, n=None):
    v = v.reshape(1, -1)
    if n is not None and v.shape[1] < n:
        v = jnp.pad(v, ((0, 0), (0, n - v.shape[1])))
    return v


RT = 256


def _row_call(name, fn, cw, ncol, rows, params, outs, accs=(), into=None):
    return _tiled(name, fn, (ncol, S // RT), RT, rows, params, [(S, cw, dt) for dt in outs], accs, into=into)


def _col_call(name, fn, tc, ncol, rows, params, outs, accs=(), into=None):
    return _tiled(name, fn, (ncol, 1), S, rows, params, [(S, tc, dt) for dt in outs], accs, into=into)


def _fwd_only(fn):
    return lambda *a: (fn(*a), ())


def _layer_fwd(i, x, W, P, bias_tabs, late=None):
    sv = {"x": x}
    (u,) = _row_call(f"ln1_f{i}", _fwd_only(_rmsnorm_fn), D, 1, [_rows(x, D)], [_p_row(P["ln1_g"], D)], [BF16])
    proj = _matmul(f"inproj_f{i}", u, W["w_in"], "nn")
    sv["u"], sv["proj"] = u, proj

    os_, ls_ = [], []
    for gi in range(len(ATT_GROUPS)):
        o, lse = _att_fwd(f"att_f{i}_{gi}", proj, gi, bias_tabs[gi][0], bias_tabs[gi][1])
        os_.append(o)
        ls_.append(lse)
    sv["att_o"], sv["att_l"] = os_, ls_
    (att,) = _row_call(f"attmerge_f{i}", _fwd_only(_att_merge_fn), ATT_GW, 1,
                       [_rows(t, ATT_GW) for t in os_ + ls_], [], [BF16])
    if late is not None:
        W2, P2 = late(att)
        W.update(W2)
        P.update(P2)
    y_a = _matmul(f"wa_f{i}", att, W["w_a"], "nn")
    sv["att"], sv["y_a"] = att, y_a

    pool_params = [(W["pool_w"], (N_DEV, None, 32, 256), lambda j, i_: (0, j, 0, 0)), _p_row(P["pool_scale"], 256)]
    (yb_pre,) = _col_call(f"pool_f{i}", _fwd_only(_pool_fn), 256, 4, [_rows(proj, 256, OFF_POOL // 256)],
                          pool_params, [BF16])
    y_b = _matmul(f"wb_f{i}", yb_pre, W["w_b"], "nn")
    sv["yb_pre"], sv["y_b"] = yb_pre, y_b

    conv_params = [_p_row(P["ssd_conv_w"][k], 128) for k in range(4)] + [_p_row(P["ssd_conv_b"], 128)]
    (xbc_c,) = _col_call(f"ssdconv_f{i}", _fwd_only(_ssd_conv_fn), 128, SSD_XBC // 128,
                         [_rows(proj, 128, OFF_XBC // 128)], conv_params, [F32])
    y_ssd, states = _ssd_fwd(f"ssd_f{i}", xbc_c, proj, P["a_row"], P["dtb_row"], P["d_exp"])
    (yc_pre,) = _row_call(f"ssdnorm_f{i}", _fwd_only(_gated_norm_fn), 512, 2,
                          [_rows(y_ssd, 512), _rows(proj, 512, OFF_Z // 512)], [_p_row(P["ssd_norm_w"], 512)], [BF16])
    y_c = _matmul(f"wc_f{i}", yc_pre, W["w_c"], "nn")
    sv["xbc_c"], sv["states"], sv["y_ssd"], sv["yc_pre"], sv["y_c"] = xbc_c, states, y_ssd, yc_pre, y_c

    gate_rows = [_rows(proj, D, k) for k in range(3)] + [_rows(t, D) for t in (y_a, y_b, y_c)]
    gate_params = [_p_row(P["b_gate"], D, k) for k in range(3)]
    (merged,) = _row_call(f"gate_f{i}", _fwd_only(_gate_merge_fn), D, 1, gate_rows, gate_params, [BF16])
    x1 = _matmul(f"wo_f{i}", merged, W["w_o"], "nn", add=x)
    sv["merged"], sv["x1"] = merged, x1

    (u2,) = _row_call(f"ln2_f{i}", _fwd_only(_rmsnorm_fn), D, 1, [_rows(x1, D)], [_p_row(P["ln2_g"], D)], [BF16])
    up = _matmul(f"up_f{i}", u2, W["ffn_w_up"], "nn", out_dtype=BF16)
    (act,) = _col_call(f"ffnact_f{i}", _fwd_only(_ffn_act_fn), 128, D_FF // 128,
                       [_rows(up, 128), _rows(up, 128, D_FF // 128)], _ffn_params(P), [BF16])
    x2 = _matmul(f"down_f{i}", act, W["ffn_w_down"], "nn", add=x1)
    sv["u2"], sv["up"], sv["act"] = u2, up, act
    return x2, sv


def _ffn_params(P):
    nb = D_FF // 128
    return ([_p_row(P["ffn_conv_w"][k], 128) for k in range(3)] + [_p_row(P["ffn_conv_b"], 128)]
            + [_p_row(P["ffn_conv_w"][k], 128, nb) for k in range(3)] + [_p_row(P["ffn_conv_b"], 128, nb)])


def _layer_bwd(i, dx2, sv, W, P, bias_tabs, onehots, on_sharded_grads):
    G = {}
    x, proj, x1 = sv["x"], sv["proj"], sv["x1"]

    dact = _matmul(f"down_bx{i}", dx2, W["ffn_w_down"], "nt", out_dtype=BF16)
    G["ffn_w_down"] = _matmul(f"down_bw{i}", sv["act"], dx2, "tn", out_dtype=BF16)
    nb = D_FF // 128
    up = sv["up"]
    f = _with_vjp(_ffn_act_fn, 10, (0, 1), tuple(range(2, 10)))
    accs = [_a_row(D_FF, 128)] * 8
    dua, duv, a0, a1, a2, ab, v0, v1, v2, vb = _col_call(
        f"ffnact_b{i}", f, 128, nb, [_rows(up, 128), _rows(up, 128, nb)], _ffn_params(P) + [_rows_as_param(dact, 128)],
        [BF16, BF16], accs)
    G["ffn_conv_w"] = jnp.concatenate([jnp.concatenate([a0, a1, a2], 0), jnp.concatenate([v0, v1, v2], 0)], axis=1)
    G["ffn_conv_b"] = jnp.concatenate([ab, vb], axis=1)[0]
    dup = jnp.concatenate([dua, duv], axis=1)
    du2 = _matmul(f"up_bx{i}", dup, W["ffn_w_up"], "nt")
    G["ffn_w_up"] = _matmul(f"up_bw{i}", sv["u2"], dup, "tn", out_dtype=BF16)

    def norm_bwd(x_, g_, du_, dres):
        (dxn,), (dg,) = _with_vjp(_rmsnorm_fn, 2, (0,), (1,))(x_, g_, du_)
        return (dxn + dres,), (dg,)

    (dx1,), (G["ln2_g"],) = _split_res(_row_call(
        f"ln2_b{i}", lambda x_, du_, dres, g_: norm_bwd(x_, g_, du_, dres), D, 1,
        [_rows(x1, D), _rows(du2, D), _rows(dx2, D)], [_p_row(P["ln2_g"], D)], [F32], [_a_row(D, D)]), 1)

    dmerged = _matmul(f"wo_bx{i}", dx1, W["w_o"], "nt")
    G["w_o"] = _matmul(f"wo_bw{i}", sv["merged"], dx1, "tn", out_dtype=BF16)
    def gate_bwd(g_, y_, dm, b_):
        return _with_vjp(lambda g, y, b: (jax.nn.sigmoid(g + b) * y,), 3, (0, 1), (2,))(g_, y_, b_, dm)

    dproj, dys, dbs = None, [], []
    for k, t in enumerate(("y_a", "y_b", "y_c")):
        dproj, dy_k, db_k = _row_call(
            f"gate_b{i}_{k}", gate_bwd, D, 1, [_rows(proj, D, k), _rows(sv[t], D), _rows(dmerged, D)],
            [_p_row(P["b_gate"], D, k)], [BF16, BF16], [_a_row(D, D)], into={0: (dproj, k, NP)})
        dys.append(dy_k)
        dbs.append(db_k)
    dya, dyb, dyc = dys
    G["b_gate"] = jnp.concatenate(dbs, axis=1)[0]

    dyc_pre = _matmul(f"wc_bx{i}", dyc, W["w_c"], "nt")
    G["w_c"] = _matmul(f"wc_bw{i}", sv["yc_pre"], dyc, "tn", out_dtype=BF16)

    def gnorm_bwd(y_, z_, dy_, w_):
        return _with_vjp(_gated_norm_fn, 3, (0, 1), (2,))(y_, z_, w_, dy_)

    dy_ssd, dproj, dnw = _row_call(
        f"ssdnorm_b{i}", gnorm_bwd, 512, 2,
        [_rows(sv["y_ssd"], 512), _rows(proj, 512, OFF_Z // 512), _rows(dyc_pre, 512)],
        [_p_row(P["ssd_norm_w"], 512)], [F32, BF16], [_a_row(D, 512)], into={1: (dproj, OFF_Z // 512, NP)})
    G["ssd_norm_w"] = dnw[0]
    dxbc_c, dproj, da_row, ddtb_row, dd_exp = _ssd_bwd(f"ssd_b{i}", sv["xbc_c"], proj, sv["states"], dy_ssd,
                                                       P["a_row"], P["dtb_row"], P["d_exp"], dproj)
    a_vec = P["a_row"][0, :SSD_HEADS]
    G["ssd_a_log"] = da_row[0, :SSD_HEADS] * a_vec
    G["ssd_dt_bias"] = ddtb_row[0, :SSD_HEADS]
    G["ssd_d"] = dd_exp.reshape(SSD_HEADS, HEAD_DIM).sum(axis=1)
    conv_params = [_p_row(P["ssd_conv_w"][k], 128) for k in range(4)] + [_p_row(P["ssd_conv_b"], 128)]

    def conv_bwd(x_, dy_, w0, w1, w2, w3, b_):
        return _with_vjp(_ssd_conv_fn, 6, (0,), (1, 2, 3, 4, 5))(x_, w0, w1, w2, w3, b_, dy_)

    dproj, c0, c1, c2, c3, cb = _col_call(
        f"ssdconv_b{i}", conv_bwd, 128, SSD_XBC // 128, [_rows(proj, 128, OFF_XBC // 128), _rows(dxbc_c, 128)],
        conv_params, [BF16], [_a_row(SSD_XBC, 128)] * 5, into={0: (dproj, OFF_XBC // 128, NP)})
    G["ssd_conv_w"] = jnp.concatenate([c0, c1, c2, c3], axis=0)
    G["ssd_conv_b"] = cb[0]

    dyb_pre = _matmul(f"wb_bx{i}", dyb, W["w_b"], "nt")
    G["w_b"] = _matmul(f"wb_bw{i}", sv["yb_pre"], dyb, "tn", out_dtype=BF16)
    pool_params = [(W["pool_w"], (N_DEV, None, 32, 256), lambda j, i_: (0, j, 0, 0)), _p_row(P["pool_scale"], 256)]

    def pool_bwd(x_, dy_, wg, sc):
        return _with_vjp(_pool_fn, 3, (0,), (1, 2))(x_, wg.astype(F32), sc, dy_)

    dproj, dwg, dsc = _col_call(
        f"pool_b{i}", pool_bwd, 256, 4, [_rows(proj, 256, OFF_POOL // 256), _rows(dyb_pre, 256)], pool_params, [BF16],
        [((N_DEV, 4, 32, 256), (N_DEV, None, 32, 256), lambda j, i_: (0, j, 0, 0)), _a_row(D, 256)],
        into={0: (dproj, OFF_POOL // 256, NP)})
    G["pool_w"] = dwg
    G["pool_scale"] = dsc[0]

    datt = _matmul(f"wa_bx{i}", dya, W["w_a"], "nt")
    G["w_a"] = _matmul(f"wa_bw{i}", sv["att"], dya, "tn", out_dtype=BF16)

    def merge_bwd(o0, o1, o2, l0, l1, l2, da_):
        return _with_vjp(_att_merge_fn, 6, (0, 1, 2, 3, 4, 5), ())(o0, o1, o2, l0, l1, l2, da_)

    dol = _row_call(f"attmerge_b{i}", merge_bwd, ATT_GW, 1,
                    [_rows(t, ATT_GW) for t in sv["att_o"] + sv["att_l"]] + [_rows(datt, ATT_GW)], [], [F32] * 6)
    g_rel = jnp.zeros((REL_BUCKETS, 18), F32)
    for gi in range(len(ATT_GROUPS)):
        dproj, gbp, gbc = _att_bwd(f"att_b{i}_{gi}", proj, gi, bias_tabs[gi][0], bias_tabs[gi][1],
                                   dol[gi], dol[3 + gi], dproj)
        oh = onehots[gi]
        gt = (jnp.einsum("hqk,qkb->bh", gbp, oh[0], precision=lax.Precision.HIGHEST)
              + jnp.einsum("hqk,qkb->bh", gbc, oh[1], precision=lax.Precision.HIGHEST))
        g_rel = g_rel.at[:, gi * 6:(gi + 1) * 6].add(gt)
    G["rel_bias"] = g_rel

    du = _matmul(f"inproj_bx{i}", dproj, W["w_in"], "nt")
    G["w_in"] = _matmul(f"inproj_bw{i}", sv["u"], dproj, "tn", out_dtype=BF16)
    ln1_g = P["ln1_g"] + on_sharded_grads(G)
    (dx,), (G["ln1_g"],) = _split_res(_row_call(
        f"ln1_b{i}", lambda x_, du_, dres, g_: norm_bwd(x_, g_, du_, dres), D, 1,
        [_rows(x, D), _rows(du, D), _rows(dx1, D)], [_p_row(ln1_g, D)], [F32], [_a_row(D, D)]), 1)
    G["ln1_g"] = G["ln1_g"][0]
    G["ln2_g"] = G["ln2_g"][0]
    return dx, G


def _rows_as_param(arr, cw):
    return (arr, (arr.shape[0], cw), lambda j, i: (0, j))


def _split_res(res, n_out):
    return tuple(res[:n_out]), tuple(res[n_out:])


def kernel(x, rel_bias, ln1_g, w_in, b_gate, w_a, pool_w, pool_scale, w_b, ssd_conv_w, ssd_conv_b, ssd_dt_bias, ssd_a_log, ssd_d, ssd_norm_w, w_c, w_o, ln2_g, ffn_w_up, ffn_conv_w, ffn_conv_b, ffn_w_down, final_g, loss_target, m_rel_bias, m_ln1_g, m_w_in, m_b_gate, m_w_a, m_pool_w, m_pool_scale, m_w_b, m_ssd_conv_w, m_ssd_conv_b, m_ssd_dt_bias, m_ssd_a_log, m_ssd_d, m_ssd_norm_w, m_w_c, m_w_o, m_ln2_g, m_ffn_w_up, m_ffn_conv_w, m_ffn_conv_b, m_ffn_w_down, m_final_g, v_rel_bias, v_ln1_g, v_w_in, v_b_gate, v_w_a, v_pool_w, v_pool_scale, v_w_b, v_ssd_conv_w, v_ssd_conv_b, v_ssd_dt_bias, v_ssd_a_log, v_ssd_d, v_ssd_norm_w, v_w_c, v_w_o, v_ln2_g, v_ffn_w_up, v_ffn_conv_w, v_ffn_conv_b, v_ffn_w_down, v_final_g):
    args = locals()
    wts = {n: args[n] for n in WEIGHTS}
    mom = {n: args["m_" + n] for n in WEIGHTS}
    var = {n: args["v_" + n] for n in WEIGHTS}
    names = list(SHARDED)

    onehots = _bucket_onehots()
    bias_tabs = []
    for gi in range(3):
        tab = rel_bias[:, gi * 6:(gi + 1) * 6]
        b = jnp.einsum("pqkb,bh->phqk", onehots[gi], tab, precision=lax.Precision.HIGHEST)
        bias_tabs.append((b[0], b[1]))

    def gather_start(tag, i, which, after):
        shards = [wts[n][i].astype(BF16) if n in MATMUL_WEIGHTS else wts[n][i] for n in which]
        return _exchange_start(f"gather_start{tag}", shards, False, after)

    def layer_params(i, which, landed):
        full = {n: _local_weight(f"local_{n}{i}", n, g) for n, g in zip(which, landed)}
        W = {n: full[n] for n in which if n in MATMUL_WEIGHTS}
        P = {}
        if "ssd_conv_w" in full:
            P["ssd_conv_w"] = [_row(full["ssd_conv_w"][k]) for k in range(4)]
            P["ffn_conv_w"] = [_row(full["ffn_conv_w"][k]) for k in range(3)]
        return W, P

    def replicated_params(i):
        return {"ln1_g": _row(ln1_g[i]), "ln2_g": _row(ln2_g[i]), "b_gate": _row(b_gate[i]),
                "pool_scale": _row(pool_scale[i]), "ssd_conv_b": _row(ssd_conv_b[i]),
                "ssd_norm_w": _row(ssd_norm_w[i]), "ffn_conv_b": _row(ffn_conv_b[i]),
                "a_row": _row(-jnp.exp(ssd_a_log[i]), 128), "dtb_row": _row(ssd_dt_bias[i], 128),
                "d_exp": _row(jnp.repeat(ssd_d[i], HEAD_DIM))}

    h = x.reshape(S, D)
    saved, Ws, Ps = [], [], []
    first, rest = ["w_in"], [n for n in names if n != "w_in"]
    state, _ = gather_start("0a", 0, first, h)
    landed_first = _exchange_wait("gather_wait0a", state, h)
    state_rest, token = gather_start("0b", 0, rest, landed_first[0])
    nxt = {}

    def late0(att):
        landed_rest = _exchange_wait("gather_wait0b", state_rest, att)
        W2, P2 = layer_params(0, rest, landed_rest)
        nxt["state"], tok = gather_start("1", 1, names, landed_rest[0])
        P2["pool_scale"] = _row(pool_scale[0]) + tok[0, 0]
        return W2, P2

    for i in range(DEPTH):
        P = replicated_params(i)
        if i == 0:
            W, P1 = layer_params(0, first, landed_first)
        else:
            W, P1 = layer_params(i, names, landed)
            if i + 1 < DEPTH:
                nxt["state"], token = gather_start(str(i + 1), i + 1, names, landed[0])
        P.update(P1)
        if i + 1 < DEPTH:
            P["ln1_g"] = P["ln1_g"] + token[0, 0]
        h, sv = _layer_fwd(i, h, W, P, bias_tabs, late0 if i == 0 else None)
        Ws.append(W)
        Ps.append(dict(P, ln1_g=_row(ln1_g[i]), pool_scale=_row(pool_scale[i])))
        saved.append(sv)
        if i + 1 < DEPTH:
            landed = _exchange_wait(f"gather_wait{i + 1}", nxt["state"], h)

    def loss_bwd(x_, t_, g_):
        lval, vjp = jax.vjp(_loss_fn, x_, t_, g_)
        dx_, _, dg_ = vjp(jnp.ones_like(lval))
        return (dx_,), (dg_, jnp.broadcast_to(lval, (1, 128)))

    dh, g_final, loss_part = _row_call("loss", loss_bwd, D, 1, [_rows(h, D), _rows(loss_target.reshape(S, D), D)],
                                       [_p_row(_row(final_g), D)], [F32], [_a_row(D, D), _a_row(128, 128)])
    loss = lax.psum(loss_part[0, 0], MESH_AXES)

    grads = {n: [None] * DEPTH for n in WEIGHTS if n not in ("rel_bias", "final_g")}
    g_rel = jnp.zeros((REL_BUCKETS, 18), F32)
    slots = [None] * DEPTH
    pending = None
    for i in reversed(range(DEPTH)):
        started = {}

        def on_sharded_grads(G, i=i, started=started):
            parts = [_device_blocks(f"blocks_{n}{i}", n, G[n]) for n in names]
            started["state"], token = _exchange_start(f"scatter_start{i}", parts, True, G["b_gate"])
            return token[0, 0]

        dh, G = _layer_bwd(i, dh, saved[i], Ws[i], Ps[i], bias_tabs, onehots, on_sharded_grads)
        if pending is not None:
            j, st = pending
            slots[j] = _exchange_wait(f"scatter_wait{j}", st, dh)
        pending = (i, started["state"])
        g_rel = g_rel + G.pop("rel_bias")
        for n, g in G.items():
            grads[n][i] = g
    grad_x = dh.reshape(1, S, D)
    local = {n: jnp.stack(grads[n]) for n in grads if n not in SHARDED}
    local["rel_bias"] = g_rel
    local["final_g"] = g_final[0]
    out = {}

    def pack(d):
        flat = jnp.concatenate([d[n].reshape(-1).astype(F32) for n in REPLICATED])
        rows = -(-flat.shape[0] // (8 * 128)) * 8
        return jnp.pad(flat, (0, rows * 128 - flat.shape[0])).reshape(rows, 128)

    (rep_slots,) = _exchange("gather_small_grads", [pack(local)], scatter=False)
    rep = _adamw("adamw_small", rep_slots, pack(wts), pack(mom), pack(var))
    off = 0
    for n in REPLICATED:
        sz = int(np.prod(wts[n].shape))
        out[n] = [t.reshape(-1)[off:off + sz].reshape(wts[n].shape) for t in rep]
        off += sz

    def flat2(n):
        shp = wts[n].shape
        r, c = int(np.prod(shp[:-1])), shp[-1]
        return r, c, wts[n].reshape(r, c), mom[n].reshape(r, c), var[n].reshape(r, c)

    chain = {}
    done = rep[0][0, 0]
    for k, n in enumerate(names):
        if n in MATMUL_WEIGHTS:
            r, c, w2, m2, v2 = flat2(n)
            res = None
            for i in (3, 2, 1):
                res = _adamw(f"adamw_{n}{i}", slots[i][k].reshape(N_DEV, r // DEPTH, c), w2, m2, v2,
                             first_row=i * (r // DEPTH), prev=res)
            chain[n] = res
            done = done + res[0][-1, 0]
    slots[0] = _exchange_wait("scatter_wait0", pending[1], done.reshape(1, 1))
    for k, n in enumerate(names):
        r, c, w2, m2, v2 = flat2(n)
        if n in MATMUL_WEIGHTS:
            res = _adamw(f"adamw_{n}0", slots[0][k].reshape(N_DEV, r // DEPTH, c), w2, m2, v2, first_row=0, prev=chain[n])
        else:
            stacked = jnp.stack([slots[i][k] for i in range(DEPTH)], axis=1)
            res = _adamw("adamw_" + n, stacked.reshape(N_DEV, r, c), w2, m2, v2)
        out[n] = [t.reshape(wts[n].shape) for t in res]

    return (loss, grad_x, *[out[n][0] for n in WEIGHTS], *[out[n][1] for n in WEIGHTS],
            *[out[n][2] for n in WEIGHTS], *[out[n][3] for n in WEIGHTS])
```

```python
import functools
import math

import numpy as np
import jax
import jax.numpy as jnp
from jax import lax
from jax.experimental import pallas as pl
from jax.experimental.pallas import tpu as pltpu

F32 = jnp.float32
BF16 = jnp.bfloat16

N_DEV = 8
MESH_AXES = ("x", "y", "c")
S = 4096
D = 1024
DEPTH = 4
HEAD_DIM = 64
ATT_W = 1152
ATT_GW = 384
ATT_GROUPS = ((128, 1), (512, 4), (2048, 16))
ATT_BLOCK = 128
REL_BUCKETS = 32
REL_MAX_DISTANCE = 2048
POOL_WINDOWS = (2, 4, 8, 16)
SSD_HEADS = 16
SSD_CHUNK = 128
SSD_XBC = 1536
D_FF = 2816
IN_WIDTH = 10128
EPS = 1e-6
NEG = -1e30

OFF_GATE, OFF_POOL, OFF_Z, OFF_XBC, OFF_DT, OFF_QKV = 0, 3072, 4096, 5120, 6656, 6912
NP = 10368
DT_PAD = 128
QKV_W = 3 * 2 * HEAD_DIM

ADAM_LR, ADAM_B1, ADAM_B2, ADAM_EPS, ADAM_WD, ADAM_STEP = 0.001, 0.9, 0.999, 1e-08, 0.01, 10

VMEM_LIMIT = 52 * 1024 * 1024


def _cparams(sem=None):
    return pltpu.CompilerParams(dimension_semantics=sem, vmem_limit_bytes=VMEM_LIMIT)


def _dot(a, b, ca, cb):
    return lax.dot_general(a.astype(BF16), b.astype(BF16), (((ca,), (cb,)), ((), ())), preferred_element_type=F32)


@jax.custom_vjp
def _mm(a, b):
    return _dot(a, b, 1, 0)


def _mm_fwd(a, b):
    return _mm(a, b), (a, b)


def _mm_bwd(res, g):
    a, b = res
    return _dot(g, b, 1, 1).astype(a.dtype), _dot(a, g, 0, 0).astype(b.dtype)


_mm.defvjp(_mm_fwd, _mm_bwd)


@jax.custom_vjp
def _mm_nt(a, b):
    return _dot(a, b, 1, 1)


def _mm_nt_fwd(a, b):
    return _mm_nt(a, b), (a, b)


def _mm_nt_bwd(res, g):
    a, b = res
    return _dot(g, b, 1, 0).astype(a.dtype), _dot(g, a, 0, 0).astype(b.dtype)


_mm_nt.defvjp(_mm_nt_fwd, _mm_nt_bwd)


@jax.custom_vjp
def _mm_tn(a, b):
    return _dot(a, b, 0, 0)


def _mm_tn_fwd(a, b):
    return _mm_tn(a, b), (a, b)


def _mm_tn_bwd(res, g):
    a, b = res
    return _dot(b, g, 1, 1).astype(a.dtype), _dot(a, g, 1, 0).astype(b.dtype)


_mm_tn.defvjp(_mm_tn_fwd, _mm_tn_bwd)


def _shift_impl(x, j):
    n = x.shape[0]
    if j == 0:
        return x
    r = pltpu.roll(x, j % n, axis=0)
    t = lax.broadcasted_iota(jnp.int32, x.shape, 0)
    mask = (t >= j) if j > 0 else (t < n + j)
    return jnp.where(mask, r, 0.0)


@functools.partial(jax.custom_vjp, nondiff_argnums=(1,))
def _shift(x, j):
    return _shift_impl(x, j)


_shift.defvjp(lambda x, j: (_shift_impl(x, j), None), lambda j, _, g: (_shift_impl(g, -j),))


def _tri(lower):
    r = lax.broadcasted_iota(jnp.int32, (SSD_CHUNK, SSD_CHUNK), 0)
    c = lax.broadcasted_iota(jnp.int32, (SSD_CHUNK, SSD_CHUNK), 1)
    return (r >= c) if lower else (r <= c)


def _dot_hi(a, b):
    return lax.dot_general(a, b, (((1,), (0,)), ((), ())), precision=lax.Precision.HIGHEST,
                           preferred_element_type=F32)


@jax.custom_vjp
def _cumsum_rows(a):
    return _dot_hi(_tri(True).astype(F32), a)


_cumsum_rows.defvjp(lambda a: (_cumsum_rows(a), None), lambda _, g: (_dot_hi(_tri(False).astype(F32), g),))


@jax.custom_vjp
def _softplus(x):
    return jnp.maximum(x, 0.0) + jnp.log(1.0 + jnp.exp(-jnp.abs(x)))


_softplus.defvjp(lambda x: (_softplus(x), x), lambda x, g: (g * jax.nn.sigmoid(x),))


def _silu(x):
    return x * jax.nn.sigmoid(x)


def _rows(arr, cw, off=0, lead=None, roff=0):
    return (arr, cw, off, lead, roff)


def _tiled(name, fn, grid, tm, rows, params, outs, accs=(), out_roff=0, prev_outs=None, into=None):
    into = into or {}
    ncol, nrow = grid
    in_specs, operands = [], []
    for arr, cw, off, lead, roff in rows:
        if lead is None:
            in_specs.append(pl.BlockSpec((tm, cw), functools.partial(lambda j, i, off, roff: (roff + i, off + j),
                                                                     off=off, roff=roff)))
        else:
            in_specs.append(pl.BlockSpec((None, tm, cw), functools.partial(
                lambda j, i, off, lead, roff: (lead, roff + i, off + j), off=off, lead=lead, roff=roff)))
        operands.append(arr)
    for arr, bs, im in params:
        in_specs.append(pl.BlockSpec(bs, im))
        operands.append(arr)
    out_specs, out_shape = [], []
    for k, (n_rows, cw, dt) in enumerate(outs):
        _, coff, total = into.get(k, (None, 0, ncol * cw))
        out_specs.append(pl.BlockSpec((tm, cw), functools.partial(lambda j, i, r, c: (r + i, c + j), r=out_roff, c=coff)))
        out_shape.append(jax.ShapeDtypeStruct((n_rows, total), dt))
    for shape, bs, im in accs:
        out_specs.append(pl.BlockSpec(bs, im))
        out_shape.append(jax.ShapeDtypeStruct(shape, F32))
    n_in, n_out = len(operands), len(outs)
    aliases = {}
    earlier = dict(enumerate(prev_outs)) if prev_outs is not None else {}
    earlier.update({k: v[0] for k, v in into.items() if v[0] is not None})
    for k, p in sorted(earlier.items()):
        aliases[len(operands)] = k
        in_specs.append(pl.BlockSpec(memory_space=pl.ANY))
        operands.append(p)

    n_all = len(operands)

    def body(*refs):
        vals = [r[...] for r in refs[:n_in]]
        o_vals, a_vals = fn(*vals)
        for r, v in zip(refs[n_all:n_all + n_out], o_vals):
            r[...] = v.astype(r.dtype)
        i = pl.program_id(1)
        for r, v in zip(refs[n_all + n_out:], a_vals):
            @pl.when(i == 0)
            def _(r=r, v=v):
                r[...] = v.astype(r.dtype)

            @pl.when(i > 0)
            def _(r=r, v=v):
                r[...] += v.astype(r.dtype)

    res = pl.pallas_call(body, grid=grid, in_specs=in_specs, out_specs=out_specs, out_shape=out_shape, name=name,
                         input_output_aliases=aliases, compiler_params=_cparams(("arbitrary", "arbitrary")))(*operands)
    return list(res)


def _with_vjp(fn, n_prim, want_out, want_acc):
    def f(*args):
        prim, g = args[:n_prim], args[n_prim:]
        outs, vjp = jax.vjp(lambda *a: fn(*a), *prim)
        d = vjp(tuple(gi.astype(o.dtype) for gi, o in zip(g, outs)))
        return tuple(d[k] for k in want_out), tuple(d[k] for k in want_acc)
    return f


def _p_row(arr, cw, off=0):
    return (arr, (1, cw), functools.partial(lambda j, i, off: (0, off + j), off=off))


def _a_row(n, cw):
    return ((1, n), (1, cw), lambda j, i: (0, j))


def _pick(n, cap, mult):
    best = None
    for t in range(mult, min(n, cap) + 1, mult):
        if n % t == 0:
            best = t
    return best if best is not None else n


def _matmul(name, a, b, mode, add=None, out_dtype=F32):
    if mode == "nn":
        (M, K), N = a.shape, b.shape[1]
    elif mode == "nt":
        (M, K), N = a.shape, b.shape[0]
    else:
        (K, M), N = a.shape, b.shape[1]
    tn = _pick(N, 1536, 128)
    k_cap = 2048 if mode == "tn" else 3456
    tk = K if K <= k_cap else _pick(K, k_cap, 128)
    nk = K // tk
    tm = _pick(M, 1408, 128) if mode == "tn" else _pick(M, 1024 if nk > 1 else 512, 8)
    a_bytes, b_bytes = a.size * a.dtype.itemsize, b.size * b.dtype.itemsize
    swap = nk == 1 and a_bytes * (N // tn) + b_bytes < b_bytes * (M // tm) + a_bytes
    ij = (lambda g0, g1: (g1, g0)) if swap else (lambda g0, g1: (g0, g1))

    def spec(block, index):
        return pl.BlockSpec(block, lambda g0, g1, k: index(*ij(g0, g1), k))

    if mode == "nn":
        a_spec = spec((tm, tk), lambda i, j, k: (i, k))
        b_spec = spec((tk, tn), lambda i, j, k: (k, j))
        ca, cb = 1, 0
    elif mode == "nt":
        a_spec = spec((tm, tk), lambda i, j, k: (i, k))
        b_spec = spec((tn, tk), lambda i, j, k: (j, k))
        ca, cb = 1, 1
    else:
        a_spec = spec((tk, tm), lambda i, j, k: (k, i))
        b_spec = spec((tk, tn), lambda i, j, k: (k, j))
        ca, cb = 0, 0
    in_specs, operands = [a_spec, b_spec], [a, b]
    if add is not None:
        in_specs.append(spec((tm, tn), lambda i, j, k: (i, j)))
        operands.append(add)

    def finish(r, refs, o_ref):
        if add is not None:
            r = r + refs[2][...]
        o_ref[...] = r.astype(o_ref.dtype)

    def body_single(*refs):
        finish(_dot(refs[0][...], refs[1][...], ca, cb), refs, refs[-1])

    def body_multi(*refs):
        o_ref, acc_ref = refs[-2], refs[-1]
        k = pl.program_id(2)
        d = _dot(refs[0][...], refs[1][...], ca, cb)

        @pl.when(k == 0)
        def _():
            acc_ref[...] = d

        @pl.when(jnp.logical_and(k > 0, k < nk - 1))
        def _():
            acc_ref[...] += d

        @pl.when(k == nk - 1)
        def _():
            finish(acc_ref[...] + d, refs, o_ref)

    grid = (N // tn, M // tm, nk) if swap else (M // tm, N // tn, nk)
    return pl.pallas_call(
        body_single if nk == 1 else body_multi, grid=grid, in_specs=in_specs,
        out_specs=spec((tm, tn), lambda i, j, k: (i, j)),
        out_shape=jax.ShapeDtypeStruct((M, N), out_dtype),
        scratch_shapes=[] if nk == 1 else [pltpu.VMEM((tm, tn), F32)], name=name,
        compiler_params=_cparams(("parallel", "parallel", "arbitrary")))(*operands)


def _seg_copies(segs, c):
    out = []
    for lo, hi, dst in segs:
        n = lo
        while n < hi:
            p = n // c
            w = min(hi, (p + 1) * c) - n
            out.append((p, n - p * c, w, dst + n - lo))
            n += w
    return out


def _col_assemble(name, blocks, copies, zeros, n_out):
    _, R, c = blocks.shape
    tm = R if R <= 128 else 128

    def body(b_ref, o_ref):
        for p, s, w, d in copies:
            o_ref[:, d:d + w] = b_ref[p, :, s:s + w]
        for lo, hi in zeros:
            o_ref[:, lo:hi] = jnp.zeros((tm, hi - lo), o_ref.dtype)

    return pl.pallas_call(
        body, grid=(R // tm,), in_specs=[pl.BlockSpec((N_DEV, tm, c), lambda i: (0, i, 0))],
        out_specs=pl.BlockSpec((tm, n_out), lambda i: (i, 0)),
        out_shape=jax.ShapeDtypeStruct((R, n_out), blocks.dtype), name=name, compiler_params=_cparams(("parallel",)))(blocks)


def _col_split(name, full, copies, c, dtype):
    R, n = full.shape
    tm = R if R <= 128 else 128

    def body(f_ref, o_ref):
        for p, s, w, d in copies:
            o_ref[p, :, s:s + w] = f_ref[:, d:d + w].astype(dtype)

    return pl.pallas_call(
        body, grid=(R // tm,), in_specs=[pl.BlockSpec((tm, n), lambda i: (i, 0))],
        out_specs=pl.BlockSpec((N_DEV, tm, c), lambda i: (0, i, 0)),
        out_shape=jax.ShapeDtypeStruct((N_DEV, R, c), dtype), name=name, compiler_params=_cparams(("parallel",)))(full)


def _rmsnorm_fn(x, g):
    x = x.astype(F32)
    return (x * lax.rsqrt(jnp.mean(x * x, axis=-1, keepdims=True) + EPS) * g,)


def _gate_merge_fn(g0, g1, g2, ya, yb, yc, b0, b1, b2):
    return (jax.nn.sigmoid(g0 + b0) * ya + jax.nn.sigmoid(g1 + b1) * yb + jax.nn.sigmoid(g2 + b2) * yc,)


def _gated_norm_fn(y, z, w):
    t = y * _silu(z)
    return (t * lax.rsqrt(jnp.mean(t * t, axis=-1, keepdims=True) + EPS) * w,)


def _att_merge_fn(o0, o1, o2, l0, l1, l2):
    m = lax.stop_gradient(jnp.maximum(jnp.maximum(l0, l1), l2))
    e0, e1, e2 = jnp.exp(l0 - m), jnp.exp(l1 - m), jnp.exp(l2 - m)
    return ((e0 * o0 + e1 * o1 + e2 * o2) / (e0 + e1 + e2),)


def _loss_fn(x, tgt, g):
    (y,) = _rmsnorm_fn(x, g)
    err = y - tgt
    return 0.5 * jnp.sum(jnp.mean(err * err, axis=-1, keepdims=True), axis=0, keepdims=True)


def _pool_fn(x, wg, scale):
    g = pl.program_id(0)
    s2 = x + _shift(x, 1)
    s4 = s2 + _shift(s2, 2)
    s8 = s4 + _shift(s4, 4)
    s16 = s8 + _shift(s8, 8)
    win = ((g == 0).astype(F32) * s2 + (g == 1).astype(F32) * s4 + (g == 2).astype(F32) * s8
           + (g == 3).astype(F32) * s16)
    t = lax.broadcasted_iota(jnp.int32, (x.shape[0], 1), 0) + 1
    cnt = jnp.minimum(t, jnp.left_shift(2, g)).astype(F32)
    d = win / cnt - x
    return (_mm(d, wg.reshape(256, 256)) * scale,)


def _dwconv(x, taps, b):
    k = len(taps)
    y = taps[k - 1] * x + b
    for i in range(k - 1):
        y = y + taps[i] * _shift(x, k - 1 - i)
    return y


def _ssd_conv_fn(x, w0, w1, w2, w3, b):
    return (_silu(_dwconv(x, (w0, w1, w2, w3), b)),)


def _ffn_act_fn(xa, xv, a0, a1, a2, ab, v0, v1, v2, vb):
    xa, xv = xa.astype(F32), xv.astype(F32)
    return (_silu(_dwconv(xa, (a0, a1, a2), ab)) * _dwconv(xv, (v0, v1, v2), vb),)


@jax.custom_vjp
def _halves(x):
    return x[:ATT_BLOCK], x[ATT_BLOCK:]


_halves.defvjp(lambda x: (_halves(x), None), lambda _, g: (jnp.concatenate([g[0], g[1]], axis=0),))


def _att_block(q, kp, kc, vp, vc, bpa, bpb, bca, bcb, prev_ok):
    n = ATT_BLOCK
    lane = lax.broadcasted_iota(jnp.int32, (1, 2 * HEAD_DIM), 1)
    ma = (lane < HEAD_DIM).astype(F32)
    mb = 1.0 - ma
    q = q.astype(F32) * (1.0 / math.sqrt(HEAD_DIM))
    q2 = jnp.concatenate([q * ma, q * mb], axis=0)
    qi = lax.broadcasted_iota(jnp.int32, (2 * n, n), 0) & (n - 1)
    kj = lax.broadcasted_iota(jnp.int32, (2 * n, n), 1)
    sp = jnp.where(jnp.logical_and(kj >= qi, prev_ok), _mm_nt(q2, kp) + jnp.concatenate([bpa, bpb], axis=0), NEG)
    sc = jnp.where(kj <= qi, _mm_nt(q2, kc) + jnp.concatenate([bca, bcb], axis=0), NEG)
    m = lax.stop_gradient(jnp.maximum(jnp.max(sp, axis=1, keepdims=True), jnp.max(sc, axis=1, keepdims=True)))
    pp = jnp.exp(sp - m)
    pc = jnp.exp(sc - m)
    l = jnp.sum(pp, axis=1, keepdims=True) + jnp.sum(pc, axis=1, keepdims=True)
    oa, ob = _halves((_mm(pp, vp) + _mm(pc, vc)) / l)
    la, lb = _halves((m + jnp.log(l)) * jnp.ones((1, 2 * HEAD_DIM), F32))
    return oa * ma + ob * mb, la * ma + lb * mb


def _att_slab(dil):
    nbk = 4 if dil == 1 else 1
    t = ATT_BLOCK * dil * nbk
    return nbk, t, S // t


def _att_in_specs(gi, t):
    def spec(which, prev):
        col = OFF_QKV // 128 + gi * 9 + which

        def index(p, j, col=col, prev=prev):
            jj = jnp.minimum(j, S // t - 1)
            return (jnp.maximum(jj - 1, 0) if prev else jj, col + 3 * p)
        return pl.BlockSpec((t, 2 * HEAD_DIM), index)
    return [spec(0, False), spec(1, False), spec(1, True), spec(2, False), spec(2, True)]


def _bias_specs():
    return [pl.BlockSpec((None, ATT_BLOCK, ATT_BLOCK), functools.partial(lambda p, j, hh: (2 * p + hh, 0, 0), hh=hh))
            for hh in (0, 1)]


def _att_units(dil, nbk, body):
    def per_residue(r, carry):
        for b in range(nbk):
            rows = pl.ds(b * ATT_BLOCK * dil + r, ATT_BLOCK, stride=dil)
            prev = pl.ds(((b - 1) % nbk) * ATT_BLOCK * dil + r, ATT_BLOCK, stride=dil)
            body(b, rows, prev, b > 0)
        return carry
    if dil == 1:
        per_residue(0, 0)
    else:
        lax.fori_loop(0, dil, per_residue, 0)


def _att_fwd(name, proj, gi, bias_p, bias_c):
    dil = ATT_GROUPS[gi][1]
    nbk, t, ns = _att_slab(dil)
    bsp = _bias_specs()
    out_spec = pl.BlockSpec((t, 2 * HEAD_DIM), lambda p, j: (j, p))

    def body(q_ref, kc_ref, kp_ref, vc_ref, vp_ref, bpa, bpb, bca, bcb, o_ref, l_ref):
        first = pl.program_id(1) == 0
        biases = (bpa[...], bpb[...], bca[...], bcb[...])

        def unit(b, rows, prev, in_slab):
            kp = kc_ref[prev, :] if in_slab else kp_ref[prev, :]
            vp = vc_ref[prev, :] if in_slab else vp_ref[prev, :]
            prev_ok = True if in_slab else jnp.logical_not(first)
            o, lse = _att_block(q_ref[rows, :], kp, kc_ref[rows, :], vp, vc_ref[rows, :], *biases, prev_ok)
            o_ref[rows, :] = o
            l_ref[rows, :] = lse

        _att_units(dil, nbk, unit)

    shp = jax.ShapeDtypeStruct((S, ATT_GW), F32)
    return pl.pallas_call(
        body, grid=(3, ns), in_specs=_att_in_specs(gi, t) + [bsp[0], bsp[1], bsp[0], bsp[1]],
        out_specs=[out_spec, out_spec], out_shape=[shp, shp], name=name,
        compiler_params=_cparams(("arbitrary",) * 2))(proj, proj, proj, proj, proj, bias_p, bias_p, bias_c, bias_c)


def _att_bwd(name, proj, gi, bias_p, bias_c, do, dl, dproj):
    dil = ATT_GROUPS[gi][1]
    nbk, t, ns = _att_slab(dil)
    bsp = _bias_specs()
    blk = (t, 2 * HEAD_DIM)
    cur = pl.BlockSpec(blk, lambda p, j: (jnp.minimum(j, ns - 1), p))
    done = pl.BlockSpec((t, QKV_W), lambda p, j: (jnp.maximum(j - 1, 0), OFF_QKV // QKV_W + gi * 3 + p))
    gsp = pl.BlockSpec((None, ATT_BLOCK, ATT_BLOCK), lambda p, j: (p, 0, 0))

    def body(q_ref, kc_ref, kp_ref, vc_ref, vp_ref, bpa, bpb, bca, bcb, do_ref, dl_ref, _,
             dqkv_ref, gpa, gpb, gca, gcb, accq, acck, accv):
        j = pl.program_id(1)
        mine, other = acck.at[j % 2], acck.at[1 - j % 2]
        mine_v, other_v = accv.at[j % 2], accv.at[1 - j % 2]
        dq_ref, other_q = accq.at[j % 2], accq.at[1 - j % 2]

        @pl.when(j == 0)
        def _():
            for g in (gpa, gpb, gca, gcb):
                g[...] = jnp.zeros_like(g)
            other[...] = jnp.zeros_like(other)
            other_v[...] = jnp.zeros_like(other_v)
            other_q[...] = jnp.zeros_like(other_q)

        @pl.when(j < ns)
        def _():
            mine[...] = jnp.zeros_like(mine)
            mine_v[...] = jnp.zeros_like(mine_v)
            biases = (bpa[...], bpb[...], bca[...], bcb[...])

            def unit(b, rows, prev, in_slab):
                kp = kc_ref[prev, :] if in_slab else kp_ref[prev, :]
                vp = vc_ref[prev, :] if in_slab else vp_ref[prev, :]
                prev_ok = True if in_slab else j > 0
                prim = (q_ref[rows, :], kp, kc_ref[rows, :], vp, vc_ref[rows, :]) + biases
                _, vjp = jax.vjp(lambda *a: _att_block(*a, prev_ok), *prim)
                dq, dkp, dkc, dvp, dvc, dpa, dpb, dca, dcb = vjp((do_ref[rows, :], dl_ref[rows, :]))
                dq_ref[rows, :] = dq
                mine[rows, :] += dkc
                mine_v[rows, :] += dvc
                tgt, tgt_v = (mine, mine_v) if in_slab else (other, other_v)
                tgt[prev, :] += dkp
                tgt_v[prev, :] += dvp
                gpa[...] += dpa
                gpb[...] += dpb
                gca[...] += dca
                gcb[...] += dcb

            _att_units(dil, nbk, unit)

        w = 2 * HEAD_DIM
        dqkv_ref[:, 0:w] = other_q[...].astype(BF16)
        dqkv_ref[:, w:2 * w] = other[...].astype(BF16)
        dqkv_ref[:, 2 * w:3 * w] = other_v[...].astype(BF16)

    gshp = jax.ShapeDtypeStruct((3, ATT_BLOCK, ATT_BLOCK), F32)
    res = pl.pallas_call(
        body, grid=(3, ns + 1),
        in_specs=_att_in_specs(gi, t) + [bsp[0], bsp[1], bsp[0], bsp[1], cur, cur, pl.BlockSpec(memory_space=pl.ANY)],
        out_specs=[done, gsp, gsp, gsp, gsp],
        out_shape=[jax.ShapeDtypeStruct((S, NP), BF16), gshp, gshp, gshp, gshp],
        input_output_aliases={11: 0},
        scratch_shapes=[pltpu.VMEM((2,) + blk, F32)] * 3, name=name,
        compiler_params=_cparams(("arbitrary",) * 2))(proj, proj, proj, proj, proj, bias_p, bias_p, bias_c, bias_c, do, dl,
                                                      dproj)
    dproj, gpa, gpb, gca, gcb = res
    heads = lambda a, b: jnp.stack([a, b], axis=1).reshape(6, ATT_BLOCK, ATT_BLOCK)
    return dproj, heads(gpa, gpb), heads(gca, gcb)


N_PAIR = SSD_HEADS // 2


def _ssd_chunk(xs, bs, cs_in, dt_raw, hs, a_row, dtb_row, ds):
    lane = lax.broadcasted_iota(jnp.int32, (1, 128), 1)
    row = lax.broadcasted_iota(jnp.int32, (128, 1), 0)
    tril = _tri(True)
    dt = _softplus(dt_raw + dtb_row)
    acs = _cumsum_rows(dt * a_row)
    acs_t = acs.T
    gmat = [_mm_nt(cs_in[g], bs[g]) for g in range(2)]
    lo = lane < HEAD_DIM
    lo_r = row < HEAD_DIM
    last = (row == SSD_CHUNK - 1).astype(F32)
    ys, hn = [], []
    for p in range(N_PAIR):
        g = p // (N_PAIR // 2)
        col, dtc, mm, clast = [], [], [], []
        for hh in range(2):
            h = 2 * p + hh
            oh = (lane == h).astype(F32)
            c_col = jnp.sum(acs * oh, axis=1, keepdims=True)
            c_row = jnp.sum(acs_t * (row == h).astype(F32), axis=0, keepdims=True)
            col.append(c_col)
            dtc.append(jnp.sum(dt * oh, axis=1, keepdims=True))
            clast.append(jnp.sum(c_col * last, axis=0, keepdims=True))
            mm.append(gmat[g] * jnp.exp(jnp.where(tril, c_col - c_row, NEG)))
        x = xs[p]
        xd = x * jnp.where(lo, dtc[0], dtc[1])
        y = jnp.where(lo, _mm(mm[0], xd), _mm(mm[1], xd))
        y = y + jnp.where(lo, jnp.exp(col[0]), jnp.exp(col[1])) * _mm_nt(cs_in[g], hs[p])
        ys.append(y + ds[p] * x)
        dec = jnp.where(lo, jnp.exp(clast[0] - col[0]), jnp.exp(clast[1] - col[1]))
        hn.append(hs[p] * jnp.where(lo_r, jnp.exp(clast[0]), jnp.exp(clast[1])) + _mm_tn(xd * dec, bs[g]))
    return tuple(ys), tuple(hn)


def _ssd_load(xbc_ref, dt_ref, a_ref, dtb_ref, d_ref):
    xs = tuple(xbc_ref[:, 128 * p:128 * (p + 1)] for p in range(N_PAIR))
    bs = tuple(xbc_ref[:, D + 128 * g:D + 128 * (g + 1)] for g in range(2))
    cs = tuple(xbc_ref[:, D + 256 + 128 * g:D + 256 + 128 * (g + 1)] for g in range(2))
    ds = tuple(d_ref[:, 128 * p:128 * (p + 1)] for p in range(N_PAIR))
    return xs, bs, cs, dt_ref[...], a_ref[...], dtb_ref[...], ds


def _ssd_fwd(name, xbc_c, proj, a_row, dtb_row, d_exp):
    nc = S // SSD_CHUNK
    prow = lambda n: pl.BlockSpec((1, n), lambda c: (0, 0))

    def body(xbc_ref, dt_ref, a_ref, dtb_ref, d_ref, y_ref, st_ref, h_ref):
        @pl.when(pl.program_id(0) == 0)
        def _():
            h_ref[...] = jnp.zeros_like(h_ref)

        xs, bs, cs, dt_raw, a, dtb, ds = _ssd_load(xbc_ref, dt_ref, a_ref, dtb_ref, d_ref)
        hs = tuple(h_ref[p] for p in range(N_PAIR))
        ys, hn = _ssd_chunk(xs, bs, cs, dt_raw, hs, a, dtb, ds)
        for p in range(N_PAIR):
            y_ref[:, 128 * p:128 * (p + 1)] = ys[p]
            st_ref[p] = hs[p]
            h_ref[p] = hn[p]

    return pl.pallas_call(
        body, grid=(nc,),
        in_specs=[pl.BlockSpec((SSD_CHUNK, SSD_XBC), lambda c: (c, 0)),
                  pl.BlockSpec((SSD_CHUNK, DT_PAD), lambda c: (c, OFF_DT // DT_PAD)),
                  prow(128), prow(128), prow(D)],
        out_specs=[pl.BlockSpec((SSD_CHUNK, D), lambda c: (c, 0)),
                   pl.BlockSpec((None, N_PAIR, 128, 128), lambda c: (c, 0, 0, 0))],
        out_shape=[jax.ShapeDtypeStruct((S, D), F32), jax.ShapeDtypeStruct((nc, N_PAIR, 128, 128), F32)],
        scratch_shapes=[pltpu.VMEM((N_PAIR, 128, 128), F32)], name=name,
        compiler_params=_cparams(("arbitrary",)))(xbc_c, proj, a_row, dtb_row, d_exp)


def _ssd_bwd(name, xbc_c, proj, states, dy, a_row, dtb_row, d_exp, dproj):
    nc = S // SSD_CHUNK
    prow = lambda n: pl.BlockSpec((1, n), lambda i: (0, 0))
    rc = lambda i: nc - 1 - i

    def body(xbc_ref, dt_ref, st_ref, dy_ref, a_ref, dtb_ref, d_ref, _, dxbc_ref, ddt_ref, da_ref, ddtb_ref, dd_ref, e_ref):
        i = pl.program_id(0)

        @pl.when(i == 0)
        def _():
            e_ref[...] = jnp.zeros_like(e_ref)
            da_ref[...] = jnp.zeros_like(da_ref)
            ddtb_ref[...] = jnp.zeros_like(ddtb_ref)
            dd_ref[...] = jnp.zeros_like(dd_ref)

        xs, bs, cs, dt_raw, a, dtb, ds = _ssd_load(xbc_ref, dt_ref, a_ref, dtb_ref, d_ref)
        hs = tuple(st_ref[p] for p in range(N_PAIR))
        _, vjp = jax.vjp(_ssd_chunk, xs, bs, cs, dt_raw, hs, a, dtb, ds)
        dys = tuple(dy_ref[:, 128 * p:128 * (p + 1)] for p in range(N_PAIR))
        es = tuple(e_ref[p] for p in range(N_PAIR))
        dxs, dbs, dcs, ddt, dhs, da, ddtb, dds = vjp((dys, es))
        for p in range(N_PAIR):
            dxbc_ref[:, 128 * p:128 * (p + 1)] = dxs[p]
            e_ref[p] = dhs[p]
            dd_ref[:, 128 * p:128 * (p + 1)] += dds[p]
        for g in range(2):
            dxbc_ref[:, D + 128 * g:D + 128 * (g + 1)] = dbs[g]
            dxbc_ref[:, D + 256 + 128 * g:D + 256 + 128 * (g + 1)] = dcs[g]
        ddt_ref[:, :DT_PAD] = ddt.astype(BF16)
        ddt_ref[:, DT_PAD:] = jnp.zeros((SSD_CHUNK, OFF_QKV - OFF_DT - DT_PAD), BF16)
        da_ref[...] += da
        ddtb_ref[...] += ddtb

    dt_w = OFF_QKV - OFF_DT
    return pl.pallas_call(
        body, grid=(nc,),
        in_specs=[pl.BlockSpec((SSD_CHUNK, SSD_XBC), lambda i: (rc(i), 0)),
                  pl.BlockSpec((SSD_CHUNK, DT_PAD), lambda i: (rc(i), OFF_DT // DT_PAD)),
                  pl.BlockSpec((None, N_PAIR, 128, 128), lambda i: (rc(i), 0, 0, 0)),
                  pl.BlockSpec((SSD_CHUNK, D), lambda i: (rc(i), 0)),
                  prow(128), prow(128), prow(D), pl.BlockSpec(memory_space=pl.ANY)],
        out_specs=[pl.BlockSpec((SSD_CHUNK, SSD_XBC), lambda i: (rc(i), 0)),
                   pl.BlockSpec((SSD_CHUNK, dt_w), lambda i: (rc(i), OFF_DT // dt_w)),
                   prow(128), prow(128), prow(D)],
        out_shape=[jax.ShapeDtypeStruct((S, SSD_XBC), F32), jax.ShapeDtypeStruct((S, NP), BF16),
                   jax.ShapeDtypeStruct((1, 128), F32), jax.ShapeDtypeStruct((1, 128), F32),
                   jax.ShapeDtypeStruct((1, D), F32)],
        input_output_aliases={7: 1},
        scratch_shapes=[pltpu.VMEM((N_PAIR, 128, 128), F32)], name=name,
        compiler_params=_cparams(("arbitrary",)))(xbc_c, proj, states, dy, a_row, dtb_row, d_exp, dproj)


def _exchange(name, arrays, scatter):
    n = len(arrays)
    flips = [(dx, dy, dc) for dx in (0, 1) for dy in (0, 1) for dc in (0, 1) if dx or dy or dc]

    def body(*refs):
        ins, outs = refs[:n], refs[n:2 * n]
        send_sems, recv_sems, loc_sems = refs[2 * n:]
        x, y, c = lax.axis_index("x"), lax.axis_index("y"), lax.axis_index("c")
        me = 4 * x + 2 * y + c
        peers = []
        for dx, dy, dc in flips:
            px, py, pc = (1 - x if dx else x), (1 - y if dy else y), (1 - c if dc else c)
            peers.append(((px, py, pc), 4 * px + 2 * py + pc))

        def remote(k, j, landed_from):
            dev, pid = peers[j]
            src = ins[k].at[pid] if scatter else ins[k]
            return pltpu.make_async_remote_copy(
                src_ref=src, dst_ref=outs[k].at[landed_from], send_sem=send_sems.at[k, j], recv_sem=recv_sems.at[k, j],
                device_id=dev, device_id_type=pl.DeviceIdType.MESH)

        local = [pltpu.make_async_copy(ins[k].at[me] if scatter else ins[k], outs[k].at[me], loc_sems.at[k])
                 for k in range(n)]
        for cp in local:
            cp.start()
        for k in range(n):
            for j in range(len(flips)):
                remote(k, j, me).start()
        for cp in local:
            cp.wait()
        for k in range(n):
            for j in range(len(flips)):
                remote(k, j, me).wait_send()
                remote(k, j, peers[j][1]).wait_recv()

    hbm = pl.BlockSpec(memory_space=pltpu.HBM)
    out_shape = [jax.ShapeDtypeStruct(a.shape if scatter else (N_DEV,) + a.shape, a.dtype) for a in arrays]
    res = pl.pallas_call(
        body, in_specs=[hbm] * n, out_specs=[hbm] * n, out_shape=out_shape, name=name,
        scratch_shapes=[pltpu.SemaphoreType.DMA((n, len(flips))), pltpu.SemaphoreType.DMA((n, len(flips))),
                        pltpu.SemaphoreType.DMA((n,))])(*arrays)
    return list(res)


def _gather_chip_once(name, block):
    def body(x_ref, out_ref, send_sems, recv_sems, loc_sem):
        x, y, c = lax.axis_index("x"), lax.axis_index("y"), lax.axis_index("c")
        me, sibling = (x, y, c), (x, y, 1 - c)
        chips = [(1 - x, y), (x, 1 - y), (1 - x, 1 - y)]

        def slot(px, py, pc):
            return out_ref.at[4 * px + 2 * py + pc]

        def copy(k, blk, to, src=None):
            return pltpu.make_async_remote_copy(
                src_ref=slot(*blk) if src is None else src, dst_ref=slot(*blk), send_sem=send_sems.at[k],
                recv_sem=recv_sems.at[k], device_id=to, device_id_type=pl.DeviceIdType.MESH)

        mine = pltpu.make_async_copy(x_ref, slot(*me), loc_sem)
        mine.start()
        first = [copy(0, me, sibling, src=x_ref)] + [copy(1 + j, me, (*chip, c), src=x_ref) for j, chip in enumerate(chips)]
        for cp in first:
            cp.start()
        passed = [copy(4 + j, (*chip, c), sibling) for j, chip in enumerate(chips)]
        for j, chip in enumerate(chips):
            copy(1 + j, (*chip, c), me).wait_recv()
            passed[j].start()
        copy(0, sibling, me).wait_recv()
        for j, chip in enumerate(chips):
            copy(4 + j, (*chip, 1 - c), me).wait_recv()
        for cp in first + passed:
            cp.wait_send()
        mine.wait()

    hbm = pl.BlockSpec(memory_space=pltpu.HBM)
    return pl.pallas_call(
        body, in_specs=[hbm], out_specs=hbm, out_shape=jax.ShapeDtypeStruct((N_DEV,) + block.shape, block.dtype), name=name,
        scratch_shapes=[pltpu.SemaphoreType.DMA((N_DEV - 1,)), pltpu.SemaphoreType.DMA((N_DEV - 1,)),
                        pltpu.SemaphoreType.DMA(())])(block)


def _peer_copies(ins, lands, send_sems, recv_sems, loc_sems, scatter):
    n = len(ins)
    flips = [(dx, dy, dc) for dx in (0, 1) for dy in (0, 1) for dc in (0, 1) if dx or dy or dc]
    x, y, c = lax.axis_index("x"), lax.axis_index("y"), lax.axis_index("c")
    me = 4 * x + 2 * y + c
    peers = []
    for dx, dy, dc in flips:
        px, py, pc = (1 - x if dx else x), (1 - y if dy else y), (1 - c if dc else c)
        peers.append(((px, py, pc), 4 * px + 2 * py + pc))

    def remote(k, j, slot):
        dev, pid = peers[j]
        return pltpu.make_async_remote_copy(
            src_ref=ins[k].at[pid] if scatter else ins[k], dst_ref=lands[k].at[slot],
            send_sem=send_sems.at[k * N_FLIP + j], recv_sem=recv_sems.at[k * N_FLIP + j],
            device_id=dev, device_id_type=pl.DeviceIdType.MESH)

    local = [pltpu.make_async_copy(ins[k].at[me] if scatter else ins[k], lands[k].at[me], loc_sems.at[k])
             for k in range(n)]
    pairs = [(k, j) for k in range(n) for j in range(len(flips))]
    sent = lambda k, j: remote(k, j, me)
    landed = lambda k, j: remote(k, j, peers[j][1])
    return local, pairs, sent, landed


_HBM = pl.BlockSpec(memory_space=pltpu.HBM)
_SEM = pl.BlockSpec(memory_space=pltpu.SEMAPHORE)
N_FLIP = N_DEV - 1


def _exchange_start(name, arrays, scatter, after):
    n = len(arrays)
    arrays = [pltpu.with_memory_space_constraint(a, pltpu.HBM) for a in arrays]
    lands = [pltpu.with_memory_space_constraint(
        lax.empty(a.shape if scatter else (N_DEV,) + a.shape, a.dtype), pltpu.HBM) for a in arrays]

    def body(*refs):
        ins, lnd = refs[:n], refs[n:2 * n]
        send_sems, recv_sems, loc_sems = refs[2 * n + 1:2 * n + 4]
        token = refs[-1]
        local, pairs, sent, _ = _peer_copies(ins, lnd, send_sems, recv_sems, loc_sems, scatter)
        for cp in local:
            cp.start()
        for k, j in pairs:
            sent(k, j).start()
        token[...] = jnp.zeros_like(token)

    res = pl.pallas_call(
        body, name=name,
        in_specs=[_HBM] * (2 * n) + [pl.BlockSpec(memory_space=pl.ANY)],
        out_specs=[_SEM, _SEM, _SEM] + [_HBM] * (2 * n) + [pl.BlockSpec(memory_space=pltpu.VMEM)],
        out_shape=[pltpu.SemaphoreType.DMA((n * N_FLIP,)), pltpu.SemaphoreType.DMA((n * N_FLIP,)), pltpu.SemaphoreType.DMA((n,))]
        + [pltpu.HBM(a.shape, a.dtype) for a in arrays] + [pltpu.HBM(a.shape, a.dtype) for a in lands]
        + [jax.ShapeDtypeStruct((8, 128), F32)],
        input_output_aliases={k: 3 + k for k in range(2 * n)},
        compiler_params=pltpu.CompilerParams(has_side_effects=pltpu.SideEffectType.DATAFLOW_SIDE_EFFECTING),
    )(*arrays, *lands, after)
    return (res[:3], res[3:3 + n], res[3 + n:3 + 2 * n], scatter), res[-1]


def _exchange_wait(name, state, after):
    sems, ins_thru, lands_thru, scatter = state
    n = len(ins_thru)

    def body(*refs):
        ins, lnd = refs[:n], refs[n:2 * n]
        send_sems, recv_sems, loc_sems = refs[2 * n:2 * n + 3]
        local, pairs, sent, landed = _peer_copies(ins, lnd, send_sems, recv_sems, loc_sems, scatter)
        for cp in local:
            cp.wait()
        for k, j in pairs:
            sent(k, j).wait_send()
            landed(k, j).wait_recv()

    res = pl.pallas_call(
        body, name=name,
        in_specs=[_HBM] * (2 * n) + [_SEM, _SEM, _SEM] + [pl.BlockSpec(memory_space=pl.ANY)],
        out_specs=[_HBM] * (2 * n),
        out_shape=[pltpu.HBM(a.shape, a.dtype) for a in ins_thru] + [pltpu.HBM(a.shape, a.dtype) for a in lands_thru],
        input_output_aliases={k: k for k in range(2 * n)},
        compiler_params=pltpu.CompilerParams(has_side_effects=pltpu.SideEffectType.DATAFLOW_SIDE_EFFECTING),
    )(*ins_thru, *lands_thru, *sems, after)
    return list(res[n:])


def _adamw_fn(*vals):
    slots, (w, m, v) = vals[:N_DEV], vals[N_DEV:]
    g = slots[0].astype(F32)
    for s in slots[1:]:
        g = g + s.astype(F32)
    m2 = ADAM_B1 * m + (1.0 - ADAM_B1) * g
    v2 = ADAM_B2 * v + (1.0 - ADAM_B2) * (g * g)
    m_hat = m2 / (1.0 - ADAM_B1 ** ADAM_STEP)
    v_hat = v2 / (1.0 - ADAM_B2 ** ADAM_STEP)
    delta = -ADAM_LR * (m_hat / (jnp.sqrt(v_hat) + ADAM_EPS) + ADAM_WD * w)
    return (g, delta, m2, v2), ()


def _adamw(name, slots, w, m, v, first_row=0, prev=None):
    R, C = slots.shape[1:]
    tm = R if R <= 128 else _pick(R, 128 if C > D else 256, 8)
    rows = ([_rows(slots, C, lead=s) for s in range(N_DEV)]
            + [_rows(a, C, roff=first_row // tm) for a in (w, m, v)])
    return _tiled(name, _adamw_fn, (1, R // tm), tm, rows, [], [(w.shape[0], C, F32)] * 4,
                  out_roff=first_row // tm, prev_outs=prev)


def _bucket_onehots():
    out = []
    qi = jnp.arange(ATT_BLOCK)[:, None]
    kj = jnp.arange(ATT_BLOCK)[None, :]
    max_exact = REL_BUCKETS // 2
    for _, dil in ATT_GROUPS:
        parts = []
        for rel in (qi + ATT_BLOCK - kj, qi - kj):
            dist = jnp.clip(rel, 0, None) * dil
            nf = jnp.maximum(dist, 1).astype(F32)
            large = max_exact + (jnp.log(nf / max_exact) / math.log(REL_MAX_DISTANCE / max_exact)
                                 * (REL_BUCKETS - max_exact)).astype(jnp.int32)
            large = jnp.minimum(large, REL_BUCKETS - 1)
            bucket = jnp.where(dist < max_exact, dist, large)
            parts.append((bucket[:, :, None] == jnp.arange(REL_BUCKETS)[None, None, :]).astype(F32))
        out.append(jnp.stack(parts))
    return out


SHARDED = ("w_in", "w_a", "pool_w", "w_b", "ssd_conv_w", "w_c", "w_o", "ffn_w_up", "ffn_conv_w", "ffn_w_down")
MATMUL_WEIGHTS = ("w_in", "w_a", "pool_w", "w_b", "w_c", "w_o", "ffn_w_up", "ffn_w_down")
ROW_SHARDED = ("w_b", "w_c", "w_o", "ffn_w_down")
W_IN_SEGS = tuple(
    (which * ATT_W + unit * 128, which * ATT_W + (unit + 1) * 128, OFF_QKV + unit * QKV_W + which * 128)
    for unit in range(9) for which in range(3)
) + ((3456, 4480, OFF_POOL), (4480, 5504, OFF_Z), (5504, 7040, OFF_XBC), (7040, 7056, OFF_DT), (7056, IN_WIDTH, OFF_GATE))
COL_SHARDED = {
    "w_in": (IN_WIDTH // N_DEV, W_IN_SEGS, ((OFF_DT + SSD_HEADS, OFF_QKV),), NP),
    "w_a": (D // N_DEV, ((0, D, 0),), (), D),
    "ffn_w_up": (2 * D_FF // N_DEV, ((0, 2 * D_FF, 0),), (), 2 * D_FF),
    "ssd_conv_w": (SSD_XBC // N_DEV, ((0, SSD_XBC, 0),), (), SSD_XBC),
    "ffn_conv_w": (2 * D_FF // N_DEV, ((0, 2 * D_FF, 0),), (), 2 * D_FF),
}
REPLICATED = ("rel_bias", "ln1_g", "b_gate", "pool_scale", "ssd_conv_b", "ssd_dt_bias", "ssd_a_log", "ssd_d",
              "ssd_norm_w", "ln2_g", "ffn_conv_b", "final_g")
WEIGHTS = ("rel_bias", "ln1_g", "w_in", "b_gate", "w_a", "pool_w", "pool_scale", "w_b", "ssd_conv_w", "ssd_conv_b",
           "ssd_dt_bias", "ssd_a_log", "ssd_d", "ssd_norm_w", "w_c", "w_o", "ln2_g", "ffn_w_up", "ffn_conv_w",
           "ffn_conv_b", "ffn_w_down", "final_g")


def _local_weight(name, n, blocks):
    if n in COL_SHARDED:
        c, segs, zeros, width = COL_SHARDED[n]
        return _col_assemble(name, blocks, _seg_copies(segs, c), zeros, width)
    if n in ROW_SHARDED:
        return blocks.reshape(-1, blocks.shape[-1])
    return blocks


def _device_blocks(name, n, g):
    if n in COL_SHARDED:
        c, segs, _, _ = COL_SHARDED[n]
        return _col_split(name, g, _seg_copies(segs, c), c, BF16)
    if n in ROW_SHARDED:
        return g.reshape(N_DEV, g.shape[0] // N_DEV, g.shape[1]).astype(BF16)
    return g.astype(BF16)


def _row(v, n=None):
    v = v.reshape(1, -1)
    if n is not None and v.shape[1] < n:
        v = jnp.pad(v, ((0, 0), (0, n - v.shape[1])))
    return v


RT = 512


def _row_call(name, fn, cw, ncol, rows, params, outs, accs=(), into=None):
    return _tiled(name, fn, (ncol, S // RT), RT, rows, params, [(S, cw, dt) for dt in outs], accs, into=into)


def _col_call(name, fn, tc, ncol, rows, params, outs, accs=(), into=None):
    return _tiled(name, fn, (ncol, 1), S, rows, params, [(S, tc, dt) for dt in outs], accs, into=into)


def _fwd_only(fn):
    return lambda *a: (fn(*a), ())


def _layer_fwd(i, x, W, P, bias_tabs, late=None):
    sv = {"x": x}
    (u,) = _row_call(f"ln1_f{i}", _fwd_only(_rmsnorm_fn), D, 1, [_rows(x, D)], [_p_row(P["ln1_g"], D)], [BF16])
    proj = _matmul(f"inproj_f{i}", u, W["w_in"], "nn")
    sv["u"], sv["proj"] = u, proj

    os_, ls_ = [], []
    for gi in range(len(ATT_GROUPS)):
        o, lse = _att_fwd(f"att_f{i}_{gi}", proj, gi, bias_tabs[gi][0], bias_tabs[gi][1])
        os_.append(o)
        ls_.append(lse)
    sv["att_o"], sv["att_l"] = os_, ls_
    (att,) = _row_call(f"attmerge_f{i}", _fwd_only(_att_merge_fn), ATT_GW, 1,
                       [_rows(t, ATT_GW) for t in os_ + ls_], [], [BF16])
    if late is not None:
        W2, P2 = late(att)
        W.update(W2)
        P.update(P2)
    y_a = _matmul(f"wa_f{i}", att, W["w_a"], "nn", out_dtype=BF16)
    sv["att"], sv["y_a"] = att, y_a

    pool_params = [(W["pool_w"], (N_DEV, None, 32, 256), lambda j, i_: (0, j, 0, 0)), _p_row(P["pool_scale"], 256)]
    (yb_pre,) = _col_call(f"pool_f{i}", _fwd_only(_pool_fn), 256, 4, [_rows(proj, 256, OFF_POOL // 256)],
                          pool_params, [BF16])
    y_b = _matmul(f"wb_f{i}", yb_pre, W["w_b"], "nn", out_dtype=BF16)
    sv["yb_pre"], sv["y_b"] = yb_pre, y_b

    conv_params = [_p_row(P["ssd_conv_w"][k], 128) for k in range(4)] + [_p_row(P["ssd_conv_b"], 128)]
    (xbc_c,) = _col_call(f"ssdconv_f{i}", _fwd_only(_ssd_conv_fn), 128, SSD_XBC // 128,
                         [_rows(proj, 128, OFF_XBC // 128)], conv_params, [F32])
    y_ssd, states = _ssd_fwd(f"ssd_f{i}", xbc_c, proj, P["a_row"], P["dtb_row"], P["d_exp"])
    (yc_pre,) = _row_call(f"ssdnorm_f{i}", _fwd_only(_gated_norm_fn), 512, 2,
                          [_rows(y_ssd, 512), _rows(proj, 512, OFF_Z // 512)], [_p_row(P["ssd_norm_w"], 512)], [BF16])
    y_c = _matmul(f"wc_f{i}", yc_pre, W["w_c"], "nn", out_dtype=BF16)
    sv["xbc_c"], sv["states"], sv["y_ssd"], sv["yc_pre"], sv["y_c"] = xbc_c, states, y_ssd, yc_pre, y_c

    gate_rows = [_rows(proj, D, k) for k in range(3)] + [_rows(t, D) for t in (y_a, y_b, y_c)]
    gate_params = [_p_row(P["b_gate"], D, k) for k in range(3)]
    (merged,) = _row_call(f"gate_f{i}", _fwd_only(_gate_merge_fn), D, 1, gate_rows, gate_params, [BF16])
    x1 = _matmul(f"wo_f{i}", merged, W["w_o"], "nn", add=x)
    sv["merged"], sv["x1"] = merged, x1

    (u2,) = _row_call(f"ln2_f{i}", _fwd_only(_rmsnorm_fn), D, 1, [_rows(x1, D)], [_p_row(P["ln2_g"], D)], [BF16])
    up = _matmul(f"up_f{i}", u2, W["ffn_w_up"], "nn", out_dtype=BF16)
    (act,) = _col_call(f"ffnact_f{i}", _fwd_only(_ffn_act_fn), 128, D_FF // 128,
                       [_rows(up, 128), _rows(up, 128, D_FF // 128)], _ffn_params(P), [BF16])
    x2 = _matmul(f"down_f{i}", act, W["ffn_w_down"], "nn", add=x1)
    sv["u2"], sv["up"], sv["act"] = u2, up, act
    return x2, sv


def _ffn_params(P):
    nb = D_FF // 128
    return ([_p_row(P["ffn_conv_w"][k], 128) for k in range(3)] + [_p_row(P["ffn_conv_b"], 128)]
            + [_p_row(P["ffn_conv_w"][k], 128, nb) for k in range(3)] + [_p_row(P["ffn_conv_b"], 128, nb)])


def _layer_bwd(i, dx2, sv, W, P, bias_tabs, onehots, on_sharded_grads):
    G = {}
    x, proj, x1 = sv["x"], sv["proj"], sv["x1"]

    dact = _matmul(f"down_bx{i}", dx2, W["ffn_w_down"], "nt", out_dtype=BF16)
    G["ffn_w_down"] = _matmul(f"down_bw{i}", sv["act"], dx2, "tn", out_dtype=BF16)
    nb = D_FF // 128
    up = sv["up"]
    f = _with_vjp(_ffn_act_fn, 10, (0, 1), tuple(range(2, 10)))
    accs = [_a_row(D_FF, 128)] * 8
    dua, duv, a0, a1, a2, ab, v0, v1, v2, vb = _col_call(
        f"ffnact_b{i}", f, 128, nb, [_rows(up, 128), _rows(up, 128, nb)], _ffn_params(P) + [_rows_as_param(dact, 128)],
        [BF16, BF16], accs)
    G["ffn_conv_w"] = jnp.concatenate([jnp.concatenate([a0, a1, a2], 0), jnp.concatenate([v0, v1, v2], 0)], axis=1)
    G["ffn_conv_b"] = jnp.concatenate([ab, vb], axis=1)[0]
    dup = jnp.concatenate([dua, duv], axis=1)
    du2 = _matmul(f"up_bx{i}", dup, W["ffn_w_up"], "nt")
    G["ffn_w_up"] = _matmul(f"up_bw{i}", sv["u2"], dup, "tn", out_dtype=BF16)

    def norm_bwd(x_, g_, du_, dres):
        (dxn,), (dg,) = _with_vjp(_rmsnorm_fn, 2, (0,), (1,))(x_, g_, du_)
        return (dxn + dres,), (dg,)

    (dx1,), (G["ln2_g"],) = _split_res(_row_call(
        f"ln2_b{i}", lambda x_, du_, dres, g_: norm_bwd(x_, g_, du_, dres), D, 1,
        [_rows(x1, D), _rows(du2, D), _rows(dx2, D)], [_p_row(P["ln2_g"], D)], [F32], [_a_row(D, D)]), 1)

    dmerged = _matmul(f"wo_bx{i}", dx1, W["w_o"], "nt", out_dtype=BF16)
    G["w_o"] = _matmul(f"wo_bw{i}", sv["merged"], dx1, "tn", out_dtype=BF16)
    def gate_bwd(g_, y_, dm, b_):
        return _with_vjp(lambda g, y, b: (jax.nn.sigmoid(g + b) * y,), 3, (0, 1), (2,))(g_, y_, b_, dm)

    dproj, dys, dbs = None, [], []
    for k, t in enumerate(("y_a", "y_b", "y_c")):
        dproj, dy_k, db_k = _row_call(
            f"gate_b{i}_{k}", gate_bwd, D, 1, [_rows(proj, D, k), _rows(sv[t], D), _rows(dmerged, D)],
            [_p_row(P["b_gate"], D, k)], [BF16, BF16], [_a_row(D, D)], into={0: (dproj, k, NP)})
        dys.append(dy_k)
        dbs.append(db_k)
    dya, dyb, dyc = dys
    G["b_gate"] = jnp.concatenate(dbs, axis=1)[0]

    dyc_pre = _matmul(f"wc_bx{i}", dyc, W["w_c"], "nt", out_dtype=BF16)
    G["w_c"] = _matmul(f"wc_bw{i}", sv["yc_pre"], dyc, "tn", out_dtype=BF16)

    def gnorm_bwd(y_, z_, dy_, w_):
        return _with_vjp(_gated_norm_fn, 3, (0, 1), (2,))(y_, z_, w_, dy_)

    dy_ssd, dproj, dnw = _row_call(
        f"ssdnorm_b{i}", gnorm_bwd, 512, 2,
        [_rows(sv["y_ssd"], 512), _rows(proj, 512, OFF_Z // 512), _rows(dyc_pre, 512)],
        [_p_row(P["ssd_norm_w"], 512)], [F32, BF16], [_a_row(D, 512)], into={1: (dproj, OFF_Z // 512, NP)})
    G["ssd_norm_w"] = dnw[0]
    dxbc_c, dproj, da_row, ddtb_row, dd_exp = _ssd_bwd(f"ssd_b{i}", sv["xbc_c"], proj, sv["states"], dy_ssd,
                                                       P["a_row"], P["dtb_row"], P["d_exp"], dproj)
    a_vec = P["a_row"][0, :SSD_HEADS]
    G["ssd_a_log"] = da_row[0, :SSD_HEADS] * a_vec
    G["ssd_dt_bias"] = ddtb_row[0, :SSD_HEADS]
    G["ssd_d"] = dd_exp.reshape(SSD_HEADS, HEAD_DIM).sum(axis=1)
    conv_params = [_p_row(P["ssd_conv_w"][k], 128) for k in range(4)] + [_p_row(P["ssd_conv_b"], 128)]

    def conv_bwd(x_, dy_, w0, w1, w2, w3, b_):
        return _with_vjp(_ssd_conv_fn, 6, (0,), (1, 2, 3, 4, 5))(x_, w0, w1, w2, w3, b_, dy_)

    dproj, c0, c1, c2, c3, cb = _col_call(
        f"ssdconv_b{i}", conv_bwd, 128, SSD_XBC // 128, [_rows(proj, 128, OFF_XBC // 128), _rows(dxbc_c, 128)],
        conv_params, [BF16], [_a_row(SSD_XBC, 128)] * 5, into={0: (dproj, OFF_XBC // 128, NP)})
    G["ssd_conv_w"] = jnp.concatenate([c0, c1, c2, c3], axis=0)
    G["ssd_conv_b"] = cb[0]

    dyb_pre = _matmul(f"wb_bx{i}", dyb, W["w_b"], "nt", out_dtype=BF16)
    G["w_b"] = _matmul(f"wb_bw{i}", sv["yb_pre"], dyb, "tn", out_dtype=BF16)
    pool_params = [(W["pool_w"], (N_DEV, None, 32, 256), lambda j, i_: (0, j, 0, 0)), _p_row(P["pool_scale"], 256)]

    def pool_bwd(x_, dy_, wg, sc):
        return _with_vjp(_pool_fn, 3, (0,), (1, 2))(x_, wg.astype(F32), sc, dy_)

    dproj, dwg, dsc = _col_call(
        f"pool_b{i}", pool_bwd, 256, 4, [_rows(proj, 256, OFF_POOL // 256), _rows(dyb_pre, 256)], pool_params, [BF16],
        [((N_DEV, 4, 32, 256), (N_DEV, None, 32, 256), lambda j, i_: (0, j, 0, 0)), _a_row(D, 256)],
        into={0: (dproj, OFF_POOL // 256, NP)})
    G["pool_w"] = dwg
    G["pool_scale"] = dsc[0]

    datt = _matmul(f"wa_bx{i}", dya, W["w_a"], "nt", out_dtype=BF16)
    G["w_a"] = _matmul(f"wa_bw{i}", sv["att"], dya, "tn", out_dtype=BF16)

    def merge_bwd(o0, o1, o2, l0, l1, l2, da_):
        return _with_vjp(_att_merge_fn, 6, (0, 1, 2, 3, 4, 5), ())(o0, o1, o2, l0, l1, l2, da_)

    dol = _row_call(f"attmerge_b{i}", merge_bwd, ATT_GW, 1,
                    [_rows(t, ATT_GW) for t in sv["att_o"] + sv["att_l"]] + [_rows(datt, ATT_GW)], [], [F32] * 6)
    g_rel = jnp.zeros((REL_BUCKETS, 18), F32)
    for gi in range(len(ATT_GROUPS)):
        dproj, gbp, gbc = _att_bwd(f"att_b{i}_{gi}", proj, gi, bias_tabs[gi][0], bias_tabs[gi][1],
                                   dol[gi], dol[3 + gi], dproj)
        oh = onehots[gi]
        gt = (jnp.einsum("hqk,qkb->bh", gbp, oh[0], precision=lax.Precision.HIGHEST)
              + jnp.einsum("hqk,qkb->bh", gbc, oh[1], precision=lax.Precision.HIGHEST))
        g_rel = g_rel.at[:, gi * 6:(gi + 1) * 6].add(gt)
    G["rel_bias"] = g_rel

    du = _matmul(f"inproj_bx{i}", dproj, W["w_in"], "nt")
    G["w_in"] = _matmul(f"inproj_bw{i}", sv["u"], dproj, "tn", out_dtype=BF16)
    ln1_g = P["ln1_g"] + on_sharded_grads(G)
    (dx,), (G["ln1_g"],) = _split_res(_row_call(
        f"ln1_b{i}", lambda x_, du_, dres, g_: norm_bwd(x_, g_, du_, dres), D, 1,
        [_rows(x, D), _rows(du, D), _rows(dx1, D)], [_p_row(ln1_g, D)], [F32], [_a_row(D, D)]), 1)
    G["ln1_g"] = G["ln1_g"][0]
    G["ln2_g"] = G["ln2_g"][0]
    return dx, G


def _rows_as_param(arr, cw):
    return (arr, (arr.shape[0], cw), lambda j, i: (0, j))


def _split_res(res, n_out):
    return tuple(res[:n_out]), tuple(res[n_out:])


def kernel(x, rel_bias, ln1_g, w_in, b_gate, w_a, pool_w, pool_scale, w_b, ssd_conv_w, ssd_conv_b, ssd_dt_bias, ssd_a_log, ssd_d, ssd_norm_w, w_c, w_o, ln2_g, ffn_w_up, ffn_conv_w, ffn_conv_b, ffn_w_down, final_g, loss_target, m_rel_bias, m_ln1_g, m_w_in, m_b_gate, m_w_a, m_pool_w, m_pool_scale, m_w_b, m_ssd_conv_w, m_ssd_conv_b, m_ssd_dt_bias, m_ssd_a_log, m_ssd_d, m_ssd_norm_w, m_w_c, m_w_o, m_ln2_g, m_ffn_w_up, m_ffn_conv_w, m_ffn_conv_b, m_ffn_w_down, m_final_g, v_rel_bias, v_ln1_g, v_w_in, v_b_gate, v_w_a, v_pool_w, v_pool_scale, v_w_b, v_ssd_conv_w, v_ssd_conv_b, v_ssd_dt_bias, v_ssd_a_log, v_ssd_d, v_ssd_norm_w, v_w_c, v_w_o, v_ln2_g, v_ffn_w_up, v_ffn_conv_w, v_ffn_conv_b, v_ffn_w_down, v_final_g):
    args = locals()
    wts = {n: args[n] for n in WEIGHTS}
    mom = {n: args["m_" + n] for n in WEIGHTS}
    var = {n: args["v_" + n] for n in WEIGHTS}
    names = list(SHARDED)

    onehots = _bucket_onehots()
    bias_tabs = []
    for gi in range(3):
        tab = rel_bias[:, gi * 6:(gi + 1) * 6]
        b = jnp.einsum("pqkb,bh->phqk", onehots[gi], tab, precision=lax.Precision.HIGHEST)
        bias_tabs.append((b[0], b[1]))

    def gather_start(tag, i, which, after):
        shards = [wts[n][i].astype(BF16) if n in MATMUL_WEIGHTS else wts[n][i] for n in which]
        return _exchange_start(f"gather_start{tag}", shards, False, after)

    def layer_params(i, which, landed):
        full = {n: _local_weight(f"local_{n}{i}", n, g) for n, g in zip(which, landed)}
        W = {n: full[n] for n in which if n in MATMUL_WEIGHTS}
        P = {}
        if "ssd_conv_w" in full:
            P["ssd_conv_w"] = [_row(full["ssd_conv_w"][k]) for k in range(4)]
            P["ffn_conv_w"] = [_row(full["ffn_conv_w"][k]) for k in range(3)]
        return W, P

    def replicated_params(i):
        return {"ln1_g": _row(ln1_g[i]), "ln2_g": _row(ln2_g[i]), "b_gate": _row(b_gate[i]),
                "pool_scale": _row(pool_scale[i]), "ssd_conv_b": _row(ssd_conv_b[i]),
                "ssd_norm_w": _row(ssd_norm_w[i]), "ffn_conv_b": _row(ffn_conv_b[i]),
                "a_row": _row(-jnp.exp(ssd_a_log[i]), 128), "dtb_row": _row(ssd_dt_bias[i], 128),
                "d_exp": _row(jnp.repeat(ssd_d[i], HEAD_DIM))}

    h = x.reshape(S, D)
    saved, Ws, Ps = [], [], []
    first, rest = ["w_in"], [n for n in names if n != "w_in"]
    landed_first = [_gather_chip_once("gather_w_in0", w_in[0].astype(BF16))]
    state_rest, token = gather_start("0b", 0, rest, landed_first[0])
    nxt = {}

    def late0(att):
        landed_rest = _exchange_wait("gather_wait0b", state_rest, att)
        W2, P2 = layer_params(0, rest, landed_rest)
        nxt["state"], tok = gather_start("1", 1, names, landed_rest[0])
        P2["pool_scale"] = _row(pool_scale[0]) + tok[0, 0]
        return W2, P2

    for i in range(DEPTH):
        P = replicated_params(i)
        if i == 0:
            W, P1 = layer_params(0, first, landed_first)
        else:
            W, P1 = layer_params(i, names, landed)
            if i + 1 < DEPTH:
                nxt["state"], token = gather_start(str(i + 1), i + 1, names, landed[0])
        P.update(P1)
        if i + 1 < DEPTH:
            P["ln1_g"] = P["ln1_g"] + token[0, 0]
        h, sv = _layer_fwd(i, h, W, P, bias_tabs, late0 if i == 0 else None)
        Ws.append(W)
        Ps.append(dict(P, ln1_g=_row(ln1_g[i]), pool_scale=_row(pool_scale[i])))
        saved.append(sv)
        if i + 1 < DEPTH:
            landed = _exchange_wait(f"gather_wait{i + 1}", nxt["state"], h)

    def loss_bwd(x_, t_, g_):
        lval, vjp = jax.vjp(_loss_fn, x_, t_, g_)
        dx_, _, dg_ = vjp(jnp.ones_like(lval))
        return (dx_,), (dg_, jnp.broadcast_to(lval, (1, 128)))

    dh, g_final, loss_part = _row_call("loss", loss_bwd, D, 1, [_rows(h, D), _rows(loss_target.reshape(S, D), D)],
                                       [_p_row(_row(final_g), D)], [F32], [_a_row(D, D), _a_row(128, 128)])
    loss = lax.psum(loss_part[0, 0], MESH_AXES)

    grads = {n: [None] * DEPTH for n in WEIGHTS if n not in ("rel_bias", "final_g")}
    g_rel = jnp.zeros((REL_BUCKETS, 18), F32)
    slots = [None] * DEPTH
    pending = None
    for i in reversed(range(DEPTH)):
        started = {}

        def on_sharded_grads(G, i=i, started=started):
            parts = [_device_blocks(f"blocks_{n}{i}", n, G[n]) for n in names]
            started["state"], token = _exchange_start(f"scatter_start{i}", parts, True, G["b_gate"])
            return token[0, 0]

        dh, G = _layer_bwd(i, dh, saved[i], Ws[i], Ps[i], bias_tabs, onehots, on_sharded_grads)
        if pending is not None:
            j, st = pending
            slots[j] = _exchange_wait(f"scatter_wait{j}", st, dh)
        pending = (i, started["state"])
        g_rel = g_rel + G.pop("rel_bias")
        for n, g in G.items():
            grads[n][i] = g
    grad_x = dh.reshape(1, S, D)
    local = {n: jnp.stack(grads[n]) for n in grads if n not in SHARDED}
    local["rel_bias"] = g_rel
    local["final_g"] = g_final[0]
    out = {}

    def pack(d):
        flat = jnp.concatenate([d[n].reshape(-1).astype(F32) for n in REPLICATED])
        rows = -(-flat.shape[0] // (8 * 128)) * 8
        return jnp.pad(flat, (0, rows * 128 - flat.shape[0])).reshape(rows, 128)

    (rep_slots,) = _exchange("gather_small_grads", [pack(local)], scatter=False)
    rep = _adamw("adamw_small", rep_slots, pack(wts), pack(mom), pack(var))
    off = 0
    for n in REPLICATED:
        sz = int(np.prod(wts[n].shape))
        out[n] = [t.reshape(-1)[off:off + sz].reshape(wts[n].shape) for t in rep]
        off += sz

    def flat2(n):
        shp = wts[n].shape
        r, c = int(np.prod(shp[:-1])), shp[-1]
        return r, c, wts[n].reshape(r, c), mom[n].reshape(r, c), var[n].reshape(r, c)

    chain = {}
    done = rep[0][0, 0]
    for k, n in enumerate(names):
        if n in MATMUL_WEIGHTS:
            r, c, w2, m2, v2 = flat2(n)
            res = None
            for i in (3, 2, 1):
                res = _adamw(f"adamw_{n}{i}", slots[i][k].reshape(N_DEV, r // DEPTH, c), w2, m2, v2,
                             first_row=i * (r // DEPTH), prev=res)
            chain[n] = res
            done = done + res[0][-1, 0]
    slots[0] = _exchange_wait("scatter_wait0", pending[1], done.reshape(1, 1))
    for k, n in enumerate(names):
        r, c, w2, m2, v2 = flat2(n)
        if n in MATMUL_WEIGHTS:
            res = _adamw(f"adamw_{n}0", slots[0][k].reshape(N_DEV, r // DEPTH, c), w2, m2, v2, first_row=0, prev=chain[n])
        else:
            stacked = jnp.stack([slots[i][k] for i in range(DEPTH)], axis=1)
            res = _adamw("adamw_" + n, stacked.reshape(N_DEV, r, c), w2, m2, v2)
        out[n] = [t.reshape(wts[n].shape) for t in res]

    return (loss, grad_x, *[out[n][0] for n in WEIGHTS], *[out[n][1] for n in WEIGHTS],
            *[out[n][2] for n in WEIGHTS], *[out[n][3] for n in WEIGHTS])
```

```python
import functools
import math

import numpy as np
import jax
import jax.numpy as jnp
from jax import lax
from jax.experimental import pallas as pl
from jax.experimental.pallas import tpu as pltpu

F32 = jnp.float32
BF16 = jnp.bfloat16

N_DEV = 8
MESH_AXES = ("x", "y", "c")
S = 4096
D = 1024
DEPTH = 4
HEAD_DIM = 64
ATT_W = 1152
ATT_GW = 384
ATT_GROUPS = ((128, 1), (512, 4), (2048, 16))
ATT_BLOCK = 128
REL_BUCKETS = 32
REL_MAX_DISTANCE = 2048
POOL_WINDOWS = (2, 4, 8, 16)
SSD_HEADS = 16
SSD_CHUNK = 128
SSD_XBC = 1536
D_FF = 2816
IN_WIDTH = 10128
EPS = 1e-6
NEG = -1e30

OFF_GATE, OFF_POOL, OFF_Z, OFF_XBC, OFF_DT, OFF_QKV = 0, 3072, 4096, 5120, 6656, 6912
NP = 10368
DT_PAD = 128
QKV_W = 3 * 2 * HEAD_DIM

ADAM_LR, ADAM_B1, ADAM_B2, ADAM_EPS, ADAM_WD, ADAM_STEP = 0.001, 0.9, 0.999, 1e-08, 0.01, 10

VMEM_LIMIT = 52 * 1024 * 1024


def _cparams(sem=None):
    return pltpu.CompilerParams(dimension_semantics=sem, vmem_limit_bytes=VMEM_LIMIT)


def _dot(a, b, ca, cb):
    return lax.dot_general(a.astype(BF16), b.astype(BF16), (((ca,), (cb,)), ((), ())), preferred_element_type=F32)


@jax.custom_vjp
def _mm(a, b):
    return _dot(a, b, 1, 0)


def _mm_fwd(a, b):
    return _mm(a, b), (a, b)


def _mm_bwd(res, g):
    a, b = res
    return _dot(g, b, 1, 1).astype(a.dtype), _dot(a, g, 0, 0).astype(b.dtype)


_mm.defvjp(_mm_fwd, _mm_bwd)


@jax.custom_vjp
def _mm_nt(a, b):
    return _dot(a, b, 1, 1)


def _mm_nt_fwd(a, b):
    return _mm_nt(a, b), (a, b)


def _mm_nt_bwd(res, g):
    a, b = res
    return _dot(g, b, 1, 0).astype(a.dtype), _dot(g, a, 0, 0).astype(b.dtype)


_mm_nt.defvjp(_mm_nt_fwd, _mm_nt_bwd)


@jax.custom_vjp
def _mm_tn(a, b):
    return _dot(a, b, 0, 0)


def _mm_tn_fwd(a, b):
    return _mm_tn(a, b), (a, b)


def _mm_tn_bwd(res, g):
    a, b = res
    return _dot(b, g, 1, 1).astype(a.dtype), _dot(a, g, 1, 0).astype(b.dtype)


_mm_tn.defvjp(_mm_tn_fwd, _mm_tn_bwd)


def _shift_impl(x, j):
    n = x.shape[0]
    if j == 0:
        return x
    r = pltpu.roll(x, j % n, axis=0)
    t = lax.broadcasted_iota(jnp.int32, x.shape, 0)
    mask = (t >= j) if j > 0 else (t < n + j)
    return jnp.where(mask, r, 0.0)


@functools.partial(jax.custom_vjp, nondiff_argnums=(1,))
def _shift(x, j):
    return _shift_impl(x, j)


_shift.defvjp(lambda x, j: (_shift_impl(x, j), None), lambda j, _, g: (_shift_impl(g, -j),))


def _tri(lower):
    r = lax.broadcasted_iota(jnp.int32, (SSD_CHUNK, SSD_CHUNK), 0)
    c = lax.broadcasted_iota(jnp.int32, (SSD_CHUNK, SSD_CHUNK), 1)
    return (r >= c) if lower else (r <= c)


def _dot_hi(a, b):
    return lax.dot_general(a, b, (((1,), (0,)), ((), ())), precision=lax.Precision.HIGHEST,
                           preferred_element_type=F32)


@jax.custom_vjp
def _cumsum_rows(a):
    return _dot_hi(_tri(True).astype(F32), a)


_cumsum_rows.defvjp(lambda a: (_cumsum_rows(a), None), lambda _, g: (_dot_hi(_tri(False).astype(F32), g),))


@jax.custom_vjp
def _softplus(x):
    return jnp.maximum(x, 0.0) + jnp.log(1.0 + jnp.exp(-jnp.abs(x)))


_softplus.defvjp(lambda x: (_softplus(x), x), lambda x, g: (g * jax.nn.sigmoid(x),))


def _silu(x):
    return x * jax.nn.sigmoid(x)


def _rows(arr, cw, off=0, lead=None, roff=0):
    return (arr, cw, off, lead, roff)


def _tiled(name, fn, grid, tm, rows, params, outs, accs=(), out_roff=0, prev_outs=None, into=None):
    into = into or {}
    ncol, nrow = grid
    in_specs, operands = [], []
    for arr, cw, off, lead, roff in rows:
        if lead is None:
            in_specs.append(pl.BlockSpec((tm, cw), functools.partial(lambda j, i, off, roff: (roff + i, off + j),
                                                                     off=off, roff=roff)))
        else:
            in_specs.append(pl.BlockSpec((None, tm, cw), functools.partial(
                lambda j, i, off, lead, roff: (lead, roff + i, off + j), off=off, lead=lead, roff=roff)))
        operands.append(arr)
    for arr, bs, im in params:
        in_specs.append(pl.BlockSpec(bs, im))
        operands.append(arr)
    out_specs, out_shape = [], []
    for k, (n_rows, cw, dt) in enumerate(outs):
        _, coff, total = into.get(k, (None, 0, ncol * cw))
        out_specs.append(pl.BlockSpec((tm, cw), functools.partial(lambda j, i, r, c: (r + i, c + j), r=out_roff, c=coff)))
        out_shape.append(jax.ShapeDtypeStruct((n_rows, total), dt))
    for shape, bs, im in accs:
        out_specs.append(pl.BlockSpec(bs, im))
        out_shape.append(jax.ShapeDtypeStruct(shape, F32))
    n_in, n_out = len(operands), len(outs)
    aliases = {}
    earlier = dict(enumerate(prev_outs)) if prev_outs is not None else {}
    earlier.update({k: v[0] for k, v in into.items() if v[0] is not None})
    for k, p in sorted(earlier.items()):
        aliases[len(operands)] = k
        in_specs.append(pl.BlockSpec(memory_space=pl.ANY))
        operands.append(p)

    n_all = len(operands)

    def body(*refs):
        vals = [r[...] for r in refs[:n_in]]
        o_vals, a_vals = fn(*vals)
        for r, v in zip(refs[n_all:n_all + n_out], o_vals):
            r[...] = v.astype(r.dtype)
        i = pl.program_id(1)
        for r, v in zip(refs[n_all + n_out:], a_vals):
            @pl.when(i == 0)
            def _(r=r, v=v):
                r[...] = v.astype(r.dtype)

            @pl.when(i > 0)
            def _(r=r, v=v):
                r[...] += v.astype(r.dtype)

    res = pl.pallas_call(body, grid=grid, in_specs=in_specs, out_specs=out_specs, out_shape=out_shape, name=name,
                         input_output_aliases=aliases, compiler_params=_cparams(("arbitrary", "arbitrary")))(*operands)
    return list(res)


def _with_vjp(fn, n_prim, want_out, want_acc):
    def f(*args):
        prim, g = args[:n_prim], args[n_prim:]
        outs, vjp = jax.vjp(lambda *a: fn(*a), *prim)
        d = vjp(tuple(gi.astype(o.dtype) for gi, o in zip(g, outs)))
        return tuple(d[k] for k in want_out), tuple(d[k] for k in want_acc)
    return f


def _p_row(arr, cw, off=0):
    return (arr, (1, cw), functools.partial(lambda j, i, off: (0, off + j), off=off))


def _a_row(n, cw):
    return ((1, n), (1, cw), lambda j, i: (0, j))


def _pick(n, cap, mult):
    best = None
    for t in range(mult, min(n, cap) + 1, mult):
        if n % t == 0:
            best = t
    return best if best is not None else n


def _matmul(name, a, b, mode, add=None, out_dtype=F32):
    if mode == "nn":
        (M, K), N = a.shape, b.shape[1]
    elif mode == "nt":
        (M, K), N = a.shape, b.shape[0]
    else:
        (K, M), N = a.shape, b.shape[1]
    tn = _pick(N, 1536, 128)
    k_cap = 2048 if mode == "tn" else 3456
    tk = K if K <= k_cap else _pick(K, k_cap, 128)
    nk = K // tk
    tm = _pick(M, 1408, 128) if mode == "tn" else _pick(M, 1024, 8)
    a_bytes, b_bytes = a.size * a.dtype.itemsize, b.size * b.dtype.itemsize
    swap = nk == 1 and a_bytes * (N // tn) + b_bytes < b_bytes * (M // tm) + a_bytes
    ij = (lambda g0, g1: (g1, g0)) if swap else (lambda g0, g1: (g0, g1))

    def spec(block, index):
        return pl.BlockSpec(block, lambda g0, g1, k: index(*ij(g0, g1), k))

    if mode == "nn":
        a_spec = spec((tm, tk), lambda i, j, k: (i, k))
        b_spec = spec((tk, tn), lambda i, j, k: (k, j))
        ca, cb = 1, 0
    elif mode == "nt":
        a_spec = spec((tm, tk), lambda i, j, k: (i, k))
        b_spec = spec((tn, tk), lambda i, j, k: (j, k))
        ca, cb = 1, 1
    else:
        a_spec = spec((tk, tm), lambda i, j, k: (k, i))
        b_spec = spec((tk, tn), lambda i, j, k: (k, j))
        ca, cb = 0, 0
    in_specs, operands = [a_spec, b_spec], [a, b]
    if add is not None:
        in_specs.append(spec((tm, tn), lambda i, j, k: (i, j)))
        operands.append(add)

    def finish(r, refs, o_ref):
        if add is not None:
            r = r + refs[2][...]
        o_ref[...] = r.astype(o_ref.dtype)

    def body_single(*refs):
        finish(_dot(refs[0][...], refs[1][...], ca, cb), refs, refs[-1])

    def body_multi(*refs):
        o_ref, acc_ref = refs[-2], refs[-1]
        k = pl.program_id(2)
        d = _dot(refs[0][...], refs[1][...], ca, cb)

        @pl.when(k == 0)
        def _():
            acc_ref[...] = d

        @pl.when(jnp.logical_and(k > 0, k < nk - 1))
        def _():
            acc_ref[...] += d

        @pl.when(k == nk - 1)
        def _():
            finish(acc_ref[...] + d, refs, o_ref)

    grid = (N // tn, M // tm, nk) if swap else (M // tm, N // tn, nk)
    return pl.pallas_call(
        body_single if nk == 1 else body_multi, grid=grid, in_specs=in_specs,
        out_specs=spec((tm, tn), lambda i, j, k: (i, j)),
        out_shape=jax.ShapeDtypeStruct((M, N), out_dtype),
        scratch_shapes=[] if nk == 1 else [pltpu.VMEM((tm, tn), F32)], name=name,
        compiler_params=_cparams(("parallel", "parallel", "arbitrary")))(*operands)


def _seg_copies(segs, c):
    out = []
    for lo, hi, dst in segs:
        n = lo
        while n < hi:
            p = n // c
            w = min(hi, (p + 1) * c) - n
            out.append((p, n - p * c, w, dst + n - lo))
            n += w
    return out


def _col_assemble(name, blocks, copies, zeros, n_out):
    _, R, c = blocks.shape
    tm = R if R <= 128 else 128

    def body(b_ref, o_ref):
        for p, s, w, d in copies:
            o_ref[:, d:d + w] = b_ref[p, :, s:s + w]
        for lo, hi in zeros:
            o_ref[:, lo:hi] = jnp.zeros((tm, hi - lo), o_ref.dtype)

    return pl.pallas_call(
        body, grid=(R // tm,), in_specs=[pl.BlockSpec((N_DEV, tm, c), lambda i: (0, i, 0))],
        out_specs=pl.BlockSpec((tm, n_out), lambda i: (i, 0)),
        out_shape=jax.ShapeDtypeStruct((R, n_out), blocks.dtype), name=name, compiler_params=_cparams(("parallel",)))(blocks)


def _col_split(name, full, copies, c, dtype):
    R, n = full.shape
    tm = R if R <= 128 else 128

    def body(f_ref, o_ref):
        for p, s, w, d in copies:
            o_ref[p, :, s:s + w] = f_ref[:, d:d + w].astype(dtype)

    return pl.pallas_call(
        body, grid=(R // tm,), in_specs=[pl.BlockSpec((tm, n), lambda i: (i, 0))],
        out_specs=pl.BlockSpec((N_DEV, tm, c), lambda i: (0, i, 0)),
        out_shape=jax.ShapeDtypeStruct((N_DEV, R, c), dtype), name=name, compiler_params=_cparams(("parallel",)))(full)


def _rmsnorm_fn(x, g):
    x = x.astype(F32)
    return (x * lax.rsqrt(jnp.mean(x * x, axis=-1, keepdims=True) + EPS) * g,)


def _gate_merge_fn(g0, g1, g2, ya, yb, yc, b0, b1, b2):
    return (jax.nn.sigmoid(g0 + b0) * ya + jax.nn.sigmoid(g1 + b1) * yb + jax.nn.sigmoid(g2 + b2) * yc,)


def _gated_norm_fn(y, z, w):
    t = y * _silu(z)
    return (t * lax.rsqrt(jnp.mean(t * t, axis=-1, keepdims=True) + EPS) * w,)


def _att_merge_fn(o0, o1, o2, l0, l1, l2):
    m = lax.stop_gradient(jnp.maximum(jnp.maximum(l0, l1), l2))
    e0, e1, e2 = jnp.exp(l0 - m), jnp.exp(l1 - m), jnp.exp(l2 - m)
    return ((e0 * o0 + e1 * o1 + e2 * o2) / (e0 + e1 + e2),)


def _loss_fn(x, tgt, g):
    (y,) = _rmsnorm_fn(x, g)
    err = y - tgt
    return 0.5 * jnp.sum(jnp.mean(err * err, axis=-1, keepdims=True), axis=0, keepdims=True)


def _pool_fn(x, wg, scale):
    g = pl.program_id(0)
    s2 = x + _shift(x, 1)
    s4 = s2 + _shift(s2, 2)
    s8 = s4 + _shift(s4, 4)
    s16 = s8 + _shift(s8, 8)
    win = ((g == 0).astype(F32) * s2 + (g == 1).astype(F32) * s4 + (g == 2).astype(F32) * s8
           + (g == 3).astype(F32) * s16)
    t = lax.broadcasted_iota(jnp.int32, (x.shape[0], 1), 0) + 1
    cnt = jnp.minimum(t, jnp.left_shift(2, g)).astype(F32)
    d = win / cnt - x
    return (_mm(d, wg.reshape(256, 256)) * scale,)


def _dwconv(x, taps, b):
    k = len(taps)
    y = taps[k - 1] * x + b
    for i in range(k - 1):
        y = y + taps[i] * _shift(x, k - 1 - i)
    return y


def _ssd_conv_fn(x, w0, w1, w2, w3, b):
    return (_silu(_dwconv(x, (w0, w1, w2, w3), b)),)


def _ffn_act_fn(xa, xv, a0, a1, a2, ab, v0, v1, v2, vb):
    xa, xv = xa.astype(F32), xv.astype(F32)
    return (_silu(_dwconv(xa, (a0, a1, a2), ab)) * _dwconv(xv, (v0, v1, v2), vb),)


@jax.custom_vjp
def _halves(x):
    return x[:ATT_BLOCK], x[ATT_BLOCK:]


_halves.defvjp(lambda x: (_halves(x), None), lambda _, g: (jnp.concatenate([g[0], g[1]], axis=0),))


def _att_block(q, kp, kc, vp, vc, bpa, bpb, bca, bcb, prev_ok):
    n = ATT_BLOCK
    lane = lax.broadcasted_iota(jnp.int32, (1, 2 * HEAD_DIM), 1)
    ma = (lane < HEAD_DIM).astype(F32)
    mb = 1.0 - ma
    q = q.astype(F32) * (1.0 / math.sqrt(HEAD_DIM))
    q2 = jnp.concatenate([q * ma, q * mb], axis=0)
    qi = lax.broadcasted_iota(jnp.int32, (2 * n, n), 0) & (n - 1)
    kj = lax.broadcasted_iota(jnp.int32, (2 * n, n), 1)
    sp = jnp.where(jnp.logical_and(kj >= qi, prev_ok), _mm_nt(q2, kp) + jnp.concatenate([bpa, bpb], axis=0), NEG)
    sc = jnp.where(kj <= qi, _mm_nt(q2, kc) + jnp.concatenate([bca, bcb], axis=0), NEG)
    m = lax.stop_gradient(jnp.maximum(jnp.max(sp, axis=1, keepdims=True), jnp.max(sc, axis=1, keepdims=True)))
    pp = jnp.exp(sp - m)
    pc = jnp.exp(sc - m)
    l = jnp.sum(pp, axis=1, keepdims=True) + jnp.sum(pc, axis=1, keepdims=True)
    oa, ob = _halves((_mm(pp, vp) + _mm(pc, vc)) / l)
    la, lb = _halves((m + jnp.log(l)) * jnp.ones((1, 2 * HEAD_DIM), F32))
    return oa * ma + ob * mb, la * ma + lb * mb


def _att_slab(dil):
    nbk = 4 if dil == 1 else 1
    t = ATT_BLOCK * dil * nbk
    return nbk, t, S // t


def _att_in_specs(gi, t):
    def spec(which, prev):
        col = OFF_QKV // 128 + gi * 9 + which

        def index(p, j, col=col, prev=prev):
            jj = jnp.minimum(j, S // t - 1)
            return (jnp.maximum(jj - 1, 0) if prev else jj, col + 3 * p)
        return pl.BlockSpec((t, 2 * HEAD_DIM), index)
    return [spec(0, False), spec(1, False), spec(1, True), spec(2, False), spec(2, True)]


def _bias_specs():
    return [pl.BlockSpec((None, ATT_BLOCK, ATT_BLOCK), functools.partial(lambda p, j, hh: (2 * p + hh, 0, 0), hh=hh))
            for hh in (0, 1)]


def _att_units(dil, nbk, body):
    def per_residue(r, carry):
        for b in range(nbk):
            rows = pl.ds(b * ATT_BLOCK * dil + r, ATT_BLOCK, stride=dil)
            prev = pl.ds(((b - 1) % nbk) * ATT_BLOCK * dil + r, ATT_BLOCK, stride=dil)
            body(b, rows, prev, b > 0)
        return carry
    if dil == 1:
        per_residue(0, 0)
    else:
        lax.fori_loop(0, dil, per_residue, 0)


def _att_fwd(name, proj, gi, bias_p, bias_c):
    dil = ATT_GROUPS[gi][1]
    nbk, t, ns = _att_slab(dil)
    bsp = _bias_specs()
    out_spec = pl.BlockSpec((t, 2 * HEAD_DIM), lambda p, j: (j, p))

    def body(q_ref, kc_ref, kp_ref, vc_ref, vp_ref, bpa, bpb, bca, bcb, o_ref, l_ref):
        first = pl.program_id(1) == 0
        biases = (bpa[...], bpb[...], bca[...], bcb[...])

        def unit(b, rows, prev, in_slab):
            kp = kc_ref[prev, :] if in_slab else kp_ref[prev, :]
            vp = vc_ref[prev, :] if in_slab else vp_ref[prev, :]
            prev_ok = True if in_slab else jnp.logical_not(first)
            o, lse = _att_block(q_ref[rows, :], kp, kc_ref[rows, :], vp, vc_ref[rows, :], *biases, prev_ok)
            o_ref[rows, :] = o
            l_ref[rows, :] = lse

        _att_units(dil, nbk, unit)

    shp = jax.ShapeDtypeStruct((S, ATT_GW), F32)
    return pl.pallas_call(
        body, grid=(3, ns), in_specs=_att_in_specs(gi, t) + [bsp[0], bsp[1], bsp[0], bsp[1]],
        out_specs=[out_spec, out_spec], out_shape=[shp, shp], name=name,
        compiler_params=_cparams(("arbitrary",) * 2))(proj, proj, proj, proj, proj, bias_p, bias_p, bias_c, bias_c)


def _att_bwd(name, proj, gi, bias_p, bias_c, do, dl, dproj):
    dil = ATT_GROUPS[gi][1]
    nbk, t, ns = _att_slab(dil)
    bsp = _bias_specs()
    blk = (t, 2 * HEAD_DIM)
    cur = pl.BlockSpec(blk, lambda p, j: (jnp.minimum(j, ns - 1), p))
    done = pl.BlockSpec((t, QKV_W), lambda p, j: (jnp.maximum(j - 1, 0), OFF_QKV // QKV_W + gi * 3 + p))
    gsp = pl.BlockSpec((None, ATT_BLOCK, ATT_BLOCK), lambda p, j: (p, 0, 0))

    def body(q_ref, kc_ref, kp_ref, vc_ref, vp_ref, bpa, bpb, bca, bcb, do_ref, dl_ref, _,
             dqkv_ref, gpa, gpb, gca, gcb, accq, acck, accv):
        j = pl.program_id(1)
        mine, other = acck.at[j % 2], acck.at[1 - j % 2]
        mine_v, other_v = accv.at[j % 2], accv.at[1 - j % 2]
        dq_ref, other_q = accq.at[j % 2], accq.at[1 - j % 2]

        @pl.when(j == 0)
        def _():
            for g in (gpa, gpb, gca, gcb):
                g[...] = jnp.zeros_like(g)
            other[...] = jnp.zeros_like(other)
            other_v[...] = jnp.zeros_like(other_v)
            other_q[...] = jnp.zeros_like(other_q)

        @pl.when(j < ns)
        def _():
            mine[...] = jnp.zeros_like(mine)
            mine_v[...] = jnp.zeros_like(mine_v)
            biases = (bpa[...], bpb[...], bca[...], bcb[...])

            def unit(b, rows, prev, in_slab):
                kp = kc_ref[prev, :] if in_slab else kp_ref[prev, :]
                vp = vc_ref[prev, :] if in_slab else vp_ref[prev, :]
                prev_ok = True if in_slab else j > 0
                prim = (q_ref[rows, :], kp, kc_ref[rows, :], vp, vc_ref[rows, :]) + biases
                _, vjp = jax.vjp(lambda *a: _att_block(*a, prev_ok), *prim)
                dq, dkp, dkc, dvp, dvc, dpa, dpb, dca, dcb = vjp((do_ref[rows, :], dl_ref[rows, :]))
                dq_ref[rows, :] = dq
                mine[rows, :] += dkc
                mine_v[rows, :] += dvc
                tgt, tgt_v = (mine, mine_v) if in_slab else (other, other_v)
                tgt[prev, :] += dkp
                tgt_v[prev, :] += dvp
                gpa[...] += dpa
                gpb[...] += dpb
                gca[...] += dca
                gcb[...] += dcb

            _att_units(dil, nbk, unit)

        w = 2 * HEAD_DIM
        dqkv_ref[:, 0:w] = other_q[...].astype(BF16)
        dqkv_ref[:, w:2 * w] = other[...].astype(BF16)
        dqkv_ref[:, 2 * w:3 * w] = other_v[...].astype(BF16)

    gshp = jax.ShapeDtypeStruct((3, ATT_BLOCK, ATT_BLOCK), F32)
    res = pl.pallas_call(
        body, grid=(3, ns + 1),
        in_specs=_att_in_specs(gi, t) + [bsp[0], bsp[1], bsp[0], bsp[1], cur, cur, pl.BlockSpec(memory_space=pl.ANY)],
        out_specs=[done, gsp, gsp, gsp, gsp],
        out_shape=[jax.ShapeDtypeStruct((S, NP), BF16), gshp, gshp, gshp, gshp],
        input_output_aliases={11: 0},
        scratch_shapes=[pltpu.VMEM((2,) + blk, F32)] * 3, name=name,
        compiler_params=_cparams(("arbitrary",) * 2))(proj, proj, proj, proj, proj, bias_p, bias_p, bias_c, bias_c, do, dl,
                                                      dproj)
    dproj, gpa, gpb, gca, gcb = res
    heads = lambda a, b: jnp.stack([a, b], axis=1).reshape(6, ATT_BLOCK, ATT_BLOCK)
    return dproj, heads(gpa, gpb), heads(gca, gcb)


N_PAIR = SSD_HEADS // 2


def _ssd_chunk(xs, bs, cs_in, dt_raw, hs, a_row, dtb_row, ds):
    lane = lax.broadcasted_iota(jnp.int32, (1, 128), 1)
    row = lax.broadcasted_iota(jnp.int32, (128, 1), 0)
    tril = _tri(True)
    dt = _softplus(dt_raw + dtb_row)
    acs = _cumsum_rows(dt * a_row)
    acs_t = acs.T
    gmat = [_mm_nt(cs_in[g], bs[g]) for g in range(2)]
    lo = lane < HEAD_DIM
    lo_r = row < HEAD_DIM
    last = (row == SSD_CHUNK - 1).astype(F32)
    ys, hn = [], []
    for p in range(N_PAIR):
        g = p // (N_PAIR // 2)
        col, dtc, mm, clast = [], [], [], []
        for hh in range(2):
            h = 2 * p + hh
            oh = (lane == h).astype(F32)
            c_col = jnp.sum(acs * oh, axis=1, keepdims=True)
            c_row = jnp.sum(acs_t * (row == h).astype(F32), axis=0, keepdims=True)
            col.append(c_col)
            dtc.append(jnp.sum(dt * oh, axis=1, keepdims=True))
            clast.append(jnp.sum(c_col * last, axis=0, keepdims=True))
            mm.append(gmat[g] * jnp.exp(jnp.where(tril, c_col - c_row, NEG)))
        x = xs[p]
        xd = x * jnp.where(lo, dtc[0], dtc[1])
        y = jnp.where(lo, _mm(mm[0], xd), _mm(mm[1], xd))
        y = y + jnp.where(lo, jnp.exp(col[0]), jnp.exp(col[1])) * _mm_nt(cs_in[g], hs[p])
        ys.append(y + ds[p] * x)
        dec = jnp.where(lo, jnp.exp(clast[0] - col[0]), jnp.exp(clast[1] - col[1]))
        hn.append(hs[p] * jnp.where(lo_r, jnp.exp(clast[0]), jnp.exp(clast[1])) + _mm_tn(xd * dec, bs[g]))
    return tuple(ys), tuple(hn)


def _ssd_load(xbc_ref, dt_ref, a_ref, dtb_ref, d_ref):
    xs = tuple(xbc_ref[:, 128 * p:128 * (p + 1)] for p in range(N_PAIR))
    bs = tuple(xbc_ref[:, D + 128 * g:D + 128 * (g + 1)] for g in range(2))
    cs = tuple(xbc_ref[:, D + 256 + 128 * g:D + 256 + 128 * (g + 1)] for g in range(2))
    ds = tuple(d_ref[:, 128 * p:128 * (p + 1)] for p in range(N_PAIR))
    return xs, bs, cs, dt_ref[...], a_ref[...], dtb_ref[...], ds


def _ssd_fwd(name, xbc_c, proj, a_row, dtb_row, d_exp):
    nc = S // SSD_CHUNK
    prow = lambda n: pl.BlockSpec((1, n), lambda c: (0, 0))

    def body(xbc_ref, dt_ref, a_ref, dtb_ref, d_ref, y_ref, st_ref, h_ref):
        @pl.when(pl.program_id(0) == 0)
        def _():
            h_ref[...] = jnp.zeros_like(h_ref)

        xs, bs, cs, dt_raw, a, dtb, ds = _ssd_load(xbc_ref, dt_ref, a_ref, dtb_ref, d_ref)
        hs = tuple(h_ref[p] for p in range(N_PAIR))
        ys, hn = _ssd_chunk(xs, bs, cs, dt_raw, hs, a, dtb, ds)
        for p in range(N_PAIR):
            y_ref[:, 128 * p:128 * (p + 1)] = ys[p]
            st_ref[p] = hs[p]
            h_ref[p] = hn[p]

    return pl.pallas_call(
        body, grid=(nc,),
        in_specs=[pl.BlockSpec((SSD_CHUNK, SSD_XBC), lambda c: (c, 0)),
                  pl.BlockSpec((SSD_CHUNK, DT_PAD), lambda c: (c, OFF_DT // DT_PAD)),
                  prow(128), prow(128), prow(D)],
        out_specs=[pl.BlockSpec((SSD_CHUNK, D), lambda c: (c, 0)),
                   pl.BlockSpec((None, N_PAIR, 128, 128), lambda c: (c, 0, 0, 0))],
        out_shape=[jax.ShapeDtypeStruct((S, D), F32), jax.ShapeDtypeStruct((nc, N_PAIR, 128, 128), F32)],
        scratch_shapes=[pltpu.VMEM((N_PAIR, 128, 128), F32)], name=name,
        compiler_params=_cparams(("arbitrary",)))(xbc_c, proj, a_row, dtb_row, d_exp)


def _ssd_bwd(name, xbc_c, proj, states, dy, a_row, dtb_row, d_exp, dproj):
    nc = S // SSD_CHUNK
    prow = lambda n: pl.BlockSpec((1, n), lambda i: (0, 0))
    rc = lambda i: nc - 1 - i

    def body(xbc_ref, dt_ref, st_ref, dy_ref, a_ref, dtb_ref, d_ref, _, dxbc_ref, ddt_ref, da_ref, ddtb_ref, dd_ref, e_ref):
        i = pl.program_id(0)

        @pl.when(i == 0)
        def _():
            e_ref[...] = jnp.zeros_like(e_ref)
            da_ref[...] = jnp.zeros_like(da_ref)
            ddtb_ref[...] = jnp.zeros_like(ddtb_ref)
            dd_ref[...] = jnp.zeros_like(dd_ref)

        xs, bs, cs, dt_raw, a, dtb, ds = _ssd_load(xbc_ref, dt_ref, a_ref, dtb_ref, d_ref)
        hs = tuple(st_ref[p] for p in range(N_PAIR))
        _, vjp = jax.vjp(_ssd_chunk, xs, bs, cs, dt_raw, hs, a, dtb, ds)
        dys = tuple(dy_ref[:, 128 * p:128 * (p + 1)] for p in range(N_PAIR))
        es = tuple(e_ref[p] for p in range(N_PAIR))
        dxs, dbs, dcs, ddt, dhs, da, ddtb, dds = vjp((dys, es))
        for p in range(N_PAIR):
            dxbc_ref[:, 128 * p:128 * (p + 1)] = dxs[p]
            e_ref[p] = dhs[p]
            dd_ref[:, 128 * p:128 * (p + 1)] += dds[p]
        for g in range(2):
            dxbc_ref[:, D + 128 * g:D + 128 * (g + 1)] = dbs[g]
            dxbc_ref[:, D + 256 + 128 * g:D + 256 + 128 * (g + 1)] = dcs[g]
        ddt_ref[:, :DT_PAD] = ddt.astype(BF16)
        ddt_ref[:, DT_PAD:] = jnp.zeros((SSD_CHUNK, OFF_QKV - OFF_DT - DT_PAD), BF16)
        da_ref[...] += da
        ddtb_ref[...] += ddtb

    dt_w = OFF_QKV - OFF_DT
    return pl.pallas_call(
        body, grid=(nc,),
        in_specs=[pl.BlockSpec((SSD_CHUNK, SSD_XBC), lambda i: (rc(i), 0)),
                  pl.BlockSpec((SSD_CHUNK, DT_PAD), lambda i: (rc(i), OFF_DT // DT_PAD)),
                  pl.BlockSpec((None, N_PAIR, 128, 128), lambda i: (rc(i), 0, 0, 0)),
                  pl.BlockSpec((SSD_CHUNK, D), lambda i: (rc(i), 0)),
                  prow(128), prow(128), prow(D), pl.BlockSpec(memory_space=pl.ANY)],
        out_specs=[pl.BlockSpec((SSD_CHUNK, SSD_XBC), lambda i: (rc(i), 0)),
                   pl.BlockSpec((SSD_CHUNK, dt_w), lambda i: (rc(i), OFF_DT // dt_w)),
                   prow(128), prow(128), prow(D)],
        out_shape=[jax.ShapeDtypeStruct((S, SSD_XBC), F32), jax.ShapeDtypeStruct((S, NP), BF16),
                   jax.ShapeDtypeStruct((1, 128), F32), jax.ShapeDtypeStruct((1, 128), F32),
                   jax.ShapeDtypeStruct((1, D), F32)],
        input_output_aliases={7: 1},
        scratch_shapes=[pltpu.VMEM((N_PAIR, 128, 128), F32)], name=name,
        compiler_params=_cparams(("arbitrary",)))(xbc_c, proj, states, dy, a_row, dtb_row, d_exp, dproj)


def _exchange(name, arrays, scatter):
    n = len(arrays)
    flips = [(dx, dy, dc) for dx in (0, 1) for dy in (0, 1) for dc in (0, 1) if dx or dy or dc]

    def body(*refs):
        ins, outs = refs[:n], refs[n:2 * n]
        send_sems, recv_sems, loc_sems = refs[2 * n:]
        x, y, c = lax.axis_index("x"), lax.axis_index("y"), lax.axis_index("c")
        me = 4 * x + 2 * y + c
        peers = []
        for dx, dy, dc in flips:
            px, py, pc = (1 - x if dx else x), (1 - y if dy else y), (1 - c if dc else c)
            peers.append(((px, py, pc), 4 * px + 2 * py + pc))

        def remote(k, j, landed_from):
            dev, pid = peers[j]
            src = ins[k].at[pid] if scatter else ins[k]
            return pltpu.make_async_remote_copy(
                src_ref=src, dst_ref=outs[k].at[landed_from], send_sem=send_sems.at[k, j], recv_sem=recv_sems.at[k, j],
                device_id=dev, device_id_type=pl.DeviceIdType.MESH)

        local = [pltpu.make_async_copy(ins[k].at[me] if scatter else ins[k], outs[k].at[me], loc_sems.at[k])
                 for k in range(n)]
        for cp in local:
            cp.start()
        for k in range(n):
            for j in range(len(flips)):
                remote(k, j, me).start()
        for cp in local:
            cp.wait()
        for k in range(n):
            for j in range(len(flips)):
                remote(k, j, me).wait_send()
                remote(k, j, peers[j][1]).wait_recv()

    hbm = pl.BlockSpec(memory_space=pltpu.HBM)
    out_shape = [jax.ShapeDtypeStruct(a.shape if scatter else (N_DEV,) + a.shape, a.dtype) for a in arrays]
    res = pl.pallas_call(
        body, in_specs=[hbm] * n, out_specs=[hbm] * n, out_shape=out_shape, name=name,
        scratch_shapes=[pltpu.SemaphoreType.DMA((n, len(flips))), pltpu.SemaphoreType.DMA((n, len(flips))),
                        pltpu.SemaphoreType.DMA((n,))])(*arrays)
    return list(res)


def _gather_chip_once(name, block):
    def body(x_ref, out_ref, send_sems, recv_sems, loc_sem):
        x, y, c = lax.axis_index("x"), lax.axis_index("y"), lax.axis_index("c")
        me, sibling = (x, y, c), (x, y, 1 - c)
        chips = [(1 - x, y), (x, 1 - y), (1 - x, 1 - y)]

        def slot(px, py, pc):
            return out_ref.at[4 * px + 2 * py + pc]

        def copy(k, blk, to, src=None):
            return pltpu.make_async_remote_copy(
                src_ref=slot(*blk) if src is None else src, dst_ref=slot(*blk), send_sem=send_sems.at[k],
                recv_sem=recv_sems.at[k], device_id=to, device_id_type=pl.DeviceIdType.MESH)

        mine = pltpu.make_async_copy(x_ref, slot(*me), loc_sem)
        mine.start()
        first = [copy(0, me, sibling, src=x_ref)] + [copy(1 + j, me, (*chip, c), src=x_ref) for j, chip in enumerate(chips)]
        for cp in first:
            cp.start()
        passed = [copy(4 + j, (*chip, c), sibling) for j, chip in enumerate(chips)]
        for j, chip in enumerate(chips):
            copy(1 + j, (*chip, c), me).wait_recv()
            passed[j].start()
        copy(0, sibling, me).wait_recv()
        for j, chip in enumerate(chips):
            copy(4 + j, (*chip, 1 - c), me).wait_recv()
        for cp in first + passed:
            cp.wait_send()
        mine.wait()

    hbm = pl.BlockSpec(memory_space=pltpu.HBM)
    return pl.pallas_call(
        body, in_specs=[hbm], out_specs=hbm, out_shape=jax.ShapeDtypeStruct((N_DEV,) + block.shape, block.dtype), name=name,
        scratch_shapes=[pltpu.SemaphoreType.DMA((N_DEV - 1,)), pltpu.SemaphoreType.DMA((N_DEV - 1,)),
                        pltpu.SemaphoreType.DMA(())])(block)


def _peer_copies(ins, lands, send_sems, recv_sems, loc_sems, scatter):
    n = len(ins)
    flips = [(dx, dy, dc) for dx in (0, 1) for dy in (0, 1) for dc in (0, 1) if dx or dy or dc]
    x, y, c = lax.axis_index("x"), lax.axis_index("y"), lax.axis_index("c")
    me = 4 * x + 2 * y + c
    peers = []
    for dx, dy, dc in flips:
        px, py, pc = (1 - x if dx else x), (1 - y if dy else y), (1 - c if dc else c)
        peers.append(((px, py, pc), 4 * px + 2 * py + pc))

    def remote(k, j, slot):
        dev, pid = peers[j]
        return pltpu.make_async_remote_copy(
            src_ref=ins[k].at[pid] if scatter else ins[k], dst_ref=lands[k].at[slot],
            send_sem=send_sems.at[k * N_FLIP + j], recv_sem=recv_sems.at[k * N_FLIP + j],
            device_id=dev, device_id_type=pl.DeviceIdType.MESH)

    local = [pltpu.make_async_copy(ins[k].at[me] if scatter else ins[k], lands[k].at[me], loc_sems.at[k])
             for k in range(n)]
    pairs = [(k, j) for k in range(n) for j in range(len(flips))]
    sent = lambda k, j: remote(k, j, me)
    landed = lambda k, j: remote(k, j, peers[j][1])
    return local, pairs, sent, landed


_HBM = pl.BlockSpec(memory_space=pltpu.HBM)
_SEM = pl.BlockSpec(memory_space=pltpu.SEMAPHORE)
N_FLIP = N_DEV - 1


def _exchange_start(name, arrays, scatter, after):
    n = len(arrays)
    arrays = [pltpu.with_memory_space_constraint(a, pltpu.HBM) for a in arrays]
    lands = [pltpu.with_memory_space_constraint(
        lax.empty(a.shape if scatter else (N_DEV,) + a.shape, a.dtype), pltpu.HBM) for a in arrays]

    def body(*refs):
        ins, lnd = refs[:n], refs[n:2 * n]
        send_sems, recv_sems, loc_sems = refs[2 * n + 1:2 * n + 4]
        token = refs[-1]
        local, pairs, sent, _ = _peer_copies(ins, lnd, send_sems, recv_sems, loc_sems, scatter)
        for cp in local:
            cp.start()
        for k, j in pairs:
            sent(k, j).start()
        token[...] = jnp.zeros_like(token)

    res = pl.pallas_call(
        body, name=name,
        in_specs=[_HBM] * (2 * n) + [pl.BlockSpec(memory_space=pl.ANY)],
        out_specs=[_SEM, _SEM, _SEM] + [_HBM] * (2 * n) + [pl.BlockSpec(memory_space=pltpu.VMEM)],
        out_shape=[pltpu.SemaphoreType.DMA((n * N_FLIP,)), pltpu.SemaphoreType.DMA((n * N_FLIP,)), pltpu.SemaphoreType.DMA((n,))]
        + [pltpu.HBM(a.shape, a.dtype) for a in arrays] + [pltpu.HBM(a.shape, a.dtype) for a in lands]
        + [jax.ShapeDtypeStruct((8, 128), F32)],
        input_output_aliases={k: 3 + k for k in range(2 * n)},
        compiler_params=pltpu.CompilerParams(has_side_effects=pltpu.SideEffectType.DATAFLOW_SIDE_EFFECTING),
    )(*arrays, *lands, after)
    return (res[:3], res[3:3 + n], res[3 + n:3 + 2 * n], scatter), res[-1]


def _exchange_wait(name, state, after):
    sems, ins_thru, lands_thru, scatter = state
    n = len(ins_thru)

    def body(*refs):
        ins, lnd = refs[:n], refs[n:2 * n]
        send_sems, recv_sems, loc_sems = refs[2 * n:2 * n + 3]
        local, pairs, sent, landed = _peer_copies(ins, lnd, send_sems, recv_sems, loc_sems, scatter)
        for cp in local:
            cp.wait()
        for k, j in pairs:
            sent(k, j).wait_send()
            landed(k, j).wait_recv()

    res = pl.pallas_call(
        body, name=name,
        in_specs=[_HBM] * (2 * n) + [_SEM, _SEM, _SEM] + [pl.BlockSpec(memory_space=pl.ANY)],
        out_specs=[_HBM] * (2 * n),
        out_shape=[pltpu.HBM(a.shape, a.dtype) for a in ins_thru] + [pltpu.HBM(a.shape, a.dtype) for a in lands_thru],
        input_output_aliases={k: k for k in range(2 * n)},
        compiler_params=pltpu.CompilerParams(has_side_effects=pltpu.SideEffectType.DATAFLOW_SIDE_EFFECTING),
    )(*ins_thru, *lands_thru, *sems, after)
    return list(res[n:])


def _adamw_fn(*vals):
    slots, (w, m, v) = vals[:N_DEV], vals[N_DEV:]
    g = slots[0].astype(F32)
    for s in slots[1:]:
        g = g + s.astype(F32)
    m2 = ADAM_B1 * m + (1.0 - ADAM_B1) * g
    v2 = ADAM_B2 * v + (1.0 - ADAM_B2) * (g * g)
    m_hat = m2 / (1.0 - ADAM_B1 ** ADAM_STEP)
    v_hat = v2 / (1.0 - ADAM_B2 ** ADAM_STEP)
    delta = -ADAM_LR * (m_hat / (jnp.sqrt(v_hat) + ADAM_EPS) + ADAM_WD * w)
    return (g, delta, m2, v2), ()


def _adamw(name, slots, w, m, v, first_row=0, prev=None):
    R, C = slots.shape[1:]
    tm = R if R <= 128 else _pick(R, 128 if C > D else 256, 8)
    rows = ([_rows(slots, C, lead=s) for s in range(N_DEV)]
            + [_rows(a, C, roff=first_row // tm) for a in (w, m, v)])
    return _tiled(name, _adamw_fn, (1, R // tm), tm, rows, [], [(w.shape[0], C, F32)] * 4,
                  out_roff=first_row // tm, prev_outs=prev)


def _bucket_onehots():
    out = []
    qi = jnp.arange(ATT_BLOCK)[:, None]
    kj = jnp.arange(ATT_BLOCK)[None, :]
    max_exact = REL_BUCKETS // 2
    for _, dil in ATT_GROUPS:
        parts = []
        for rel in (qi + ATT_BLOCK - kj, qi - kj):
            dist = jnp.clip(rel, 0, None) * dil
            nf = jnp.maximum(dist, 1).astype(F32)
            large = max_exact + (jnp.log(nf / max_exact) / math.log(REL_MAX_DISTANCE / max_exact)
                                 * (REL_BUCKETS - max_exact)).astype(jnp.int32)
            large = jnp.minimum(large, REL_BUCKETS - 1)
            bucket = jnp.where(dist < max_exact, dist, large)
            parts.append((bucket[:, :, None] == jnp.arange(REL_BUCKETS)[None, None, :]).astype(F32))
        out.append(jnp.stack(parts))
    return out


SHARDED = ("w_in", "w_a", "pool_w", "w_b", "ssd_conv_w", "w_c", "w_o", "ffn_w_up", "ffn_conv_w", "ffn_w_down")
MATMUL_WEIGHTS = ("w_in", "w_a", "pool_w", "w_b", "w_c", "w_o", "ffn_w_up", "ffn_w_down")
ROW_SHARDED = ("w_b", "w_c", "w_o", "ffn_w_down")
W_IN_SEGS = tuple(
    (which * ATT_W + unit * 128, which * ATT_W + (unit + 1) * 128, OFF_QKV + unit * QKV_W + which * 128)
    for unit in range(9) for which in range(3)
) + ((3456, 4480, OFF_POOL), (4480, 5504, OFF_Z), (5504, 7040, OFF_XBC), (7040, 7056, OFF_DT), (7056, IN_WIDTH, OFF_GATE))
COL_SHARDED = {
    "w_in": (IN_WIDTH // N_DEV, W_IN_SEGS, ((OFF_DT + SSD_HEADS, OFF_QKV),), NP),
    "w_a": (D // N_DEV, ((0, D, 0),), (), D),
    "ffn_w_up": (2 * D_FF // N_DEV, ((0, 2 * D_FF, 0),), (), 2 * D_FF),
    "ssd_conv_w": (SSD_XBC // N_DEV, ((0, SSD_XBC, 0),), (), SSD_XBC),
    "ffn_conv_w": (2 * D_FF // N_DEV, ((0, 2 * D_FF, 0),), (), 2 * D_FF),
}
REPLICATED = ("rel_bias", "ln1_g", "b_gate", "pool_scale", "ssd_conv_b", "ssd_dt_bias", "ssd_a_log", "ssd_d",
              "ssd_norm_w", "ln2_g", "ffn_conv_b", "final_g")
WEIGHTS = ("rel_bias", "ln1_g", "w_in", "b_gate", "w_a", "pool_w", "pool_scale", "w_b", "ssd_conv_w", "ssd_conv_b",
           "ssd_dt_bias", "ssd_a_log", "ssd_d", "ssd_norm_w", "w_c", "w_o", "ln2_g", "ffn_w_up", "ffn_conv_w",
           "ffn_conv_b", "ffn_w_down", "final_g")


def _local_weight(name, n, blocks):
    if n in COL_SHARDED:
        c, segs, zeros, width = COL_SHARDED[n]
        return _col_assemble(name, blocks, _seg_copies(segs, c), zeros, width)
    if n in ROW_SHARDED:
        return blocks.reshape(-1, blocks.shape[-1])
    return blocks


def _device_blocks(name, n, g):
    if n in COL_SHARDED:
        c, segs, _, _ = COL_SHARDED[n]
        return _col_split(name, g, _seg_copies(segs, c), c, BF16)
    if n in ROW_SHARDED:
        return g.reshape(N_DEV, g.shape[0] // N_DEV, g.shape[1]).astype(BF16)
    return g.astype(BF16)


def _row(v, n=None):
    v = v.reshape(1, -1)
    if n is not None and v.shape[1] < n:
        v = jnp.pad(v, ((0, 0), (0, n - v.shape[1])))
    return v


RT = 512


def _row_call(name, fn, cw, ncol, rows, params, outs, accs=(), into=None):
    return _tiled(name, fn, (ncol, S // RT), RT, rows, params, [(S, cw, dt) for dt in outs], accs, into=into)


def _col_call(name, fn, tc, ncol, rows, params, outs, accs=(), into=None):
    return _tiled(name, fn, (ncol, 1), S, rows, params, [(S, tc, dt) for dt in outs], accs, into=into)


def _fwd_only(fn):
    return lambda *a: (fn(*a), ())


def _layer_fwd(i, x, W, P, bias_tabs, late=None):
    sv = {"x": x}
    (u,) = _row_call(f"ln1_f{i}", _fwd_only(_rmsnorm_fn), D, 1, [_rows(x, D)], [_p_row(P["ln1_g"], D)], [BF16])
    proj = _matmul(f"inproj_f{i}", u, W["w_in"], "nn")
    sv["u"], sv["proj"] = u, proj

    os_, ls_ = [], []
    for gi in range(len(ATT_GROUPS)):
        o, lse = _att_fwd(f"att_f{i}_{gi}", proj, gi, bias_tabs[gi][0], bias_tabs[gi][1])
        os_.append(o)
        ls_.append(lse)
    sv["att_o"], sv["att_l"] = os_, ls_
    (att,) = _row_call(f"attmerge_f{i}", _fwd_only(_att_merge_fn), ATT_GW, 1,
                       [_rows(t, ATT_GW) for t in os_ + ls_], [], [BF16])
    if late is not None:
        W2, P2 = late(att)
        W.update(W2)
        P.update(P2)
    y_a = _matmul(f"wa_f{i}", att, W["w_a"], "nn", out_dtype=BF16)
    sv["att"], sv["y_a"] = att, y_a

    pool_params = [(W["pool_w"], (N_DEV, None, 32, 256), lambda j, i_: (0, j, 0, 0)), _p_row(P["pool_scale"], 256)]
    (yb_pre,) = _col_call(f"pool_f{i}", _fwd_only(_pool_fn), 256, 4, [_rows(proj, 256, OFF_POOL // 256)],
                          pool_params, [BF16])
    y_b = _matmul(f"wb_f{i}", yb_pre, W["w_b"], "nn", out_dtype=BF16)
    sv["yb_pre"], sv["y_b"] = yb_pre, y_b

    conv_params = [_p_row(P["ssd_conv_w"][k], 128) for k in range(4)] + [_p_row(P["ssd_conv_b"], 128)]
    (xbc_c,) = _col_call(f"ssdconv_f{i}", _fwd_only(_ssd_conv_fn), 128, SSD_XBC // 128,
                         [_rows(proj, 128, OFF_XBC // 128)], conv_params, [F32])
    y_ssd, states = _ssd_fwd(f"ssd_f{i}", xbc_c, proj, P["a_row"], P["dtb_row"], P["d_exp"])
    (yc_pre,) = _row_call(f"ssdnorm_f{i}", _fwd_only(_gated_norm_fn), 512, 2,
                          [_rows(y_ssd, 512), _rows(proj, 512, OFF_Z // 512)], [_p_row(P["ssd_norm_w"], 512)], [BF16])
    y_c = _matmul(f"wc_f{i}", yc_pre, W["w_c"], "nn", out_dtype=BF16)
    sv["xbc_c"], sv["states"], sv["y_ssd"], sv["yc_pre"], sv["y_c"] = xbc_c, states, y_ssd, yc_pre, y_c

    gate_rows = [_rows(proj, D, k) for k in range(3)] + [_rows(t, D) for t in (y_a, y_b, y_c)]
    gate_params = [_p_row(P["b_gate"], D, k) for k in range(3)]
    (merged,) = _row_call(f"gate_f{i}", _fwd_only(_gate_merge_fn), D, 1, gate_rows, gate_params, [BF16])
    x1 = _matmul(f"wo_f{i}", merged, W["w_o"], "nn", add=x)
    sv["merged"], sv["x1"] = merged, x1

    (u2,) = _row_call(f"ln2_f{i}", _fwd_only(_rmsnorm_fn), D, 1, [_rows(x1, D)], [_p_row(P["ln2_g"], D)], [BF16])
    up = _matmul(f"up_f{i}", u2, W["ffn_w_up"], "nn", out_dtype=BF16)
    (act,) = _col_call(f"ffnact_f{i}", _fwd_only(_ffn_act_fn), 128, D_FF // 128,
                       [_rows(up, 128), _rows(up, 128, D_FF // 128)], _ffn_params(P), [BF16])
    x2 = _matmul(f"down_f{i}", act, W["ffn_w_down"], "nn", add=x1)
    sv["u2"], sv["up"], sv["act"] = u2, up, act
    return x2, sv


def _ffn_params(P):
    nb = D_FF // 128
    return ([_p_row(P["ffn_conv_w"][k], 128) for k in range(3)] + [_p_row(P["ffn_conv_b"], 128)]
            + [_p_row(P["ffn_conv_w"][k], 128, nb) for k in range(3)] + [_p_row(P["ffn_conv_b"], 128, nb)])


def _layer_bwd(i, dx2, sv, W, P, bias_tabs, onehots, on_sharded_grads):
    G = {}
    x, proj, x1 = sv["x"], sv["proj"], sv["x1"]

    dact = _matmul(f"down_bx{i}", dx2, W["ffn_w_down"], "nt", out_dtype=BF16)
    G["ffn_w_down"] = _matmul(f"down_bw{i}", sv["act"], dx2, "tn", out_dtype=BF16)
    nb = D_FF // 128
    up = sv["up"]
    f = _with_vjp(_ffn_act_fn, 10, (0, 1), tuple(range(2, 10)))
    accs = [_a_row(D_FF, 128)] * 8
    dua, duv, a0, a1, a2, ab, v0, v1, v2, vb = _col_call(
        f"ffnact_b{i}", f, 128, nb, [_rows(up, 128), _rows(up, 128, nb)], _ffn_params(P) + [_rows_as_param(dact, 128)],
        [BF16, BF16], accs)
    G["ffn_conv_w"] = jnp.concatenate([jnp.concatenate([a0, a1, a2], 0), jnp.concatenate([v0, v1, v2], 0)], axis=1)
    G["ffn_conv_b"] = jnp.concatenate([ab, vb], axis=1)[0]
    dup = jnp.concatenate([dua, duv], axis=1)
    du2 = _matmul(f"up_bx{i}", dup, W["ffn_w_up"], "nt")
    G["ffn_w_up"] = _matmul(f"up_bw{i}", sv["u2"], dup, "tn", out_dtype=BF16)

    def norm_bwd(x_, g_, du_, dres):
        (dxn,), (dg,) = _with_vjp(_rmsnorm_fn, 2, (0,), (1,))(x_, g_, du_)
        return (dxn + dres,), (dg,)

    (dx1,), (G["ln2_g"],) = _split_res(_row_call(
        f"ln2_b{i}", lambda x_, du_, dres, g_: norm_bwd(x_, g_, du_, dres), D, 1,
        [_rows(x1, D), _rows(du2, D), _rows(dx2, D)], [_p_row(P["ln2_g"], D)], [F32], [_a_row(D, D)]), 1)

    dmerged = _matmul(f"wo_bx{i}", dx1, W["w_o"], "nt", out_dtype=BF16)
    G["w_o"] = _matmul(f"wo_bw{i}", sv["merged"], dx1, "tn", out_dtype=BF16)
    def gate_bwd(g_, y_, dm, b_):
        return _with_vjp(lambda g, y, b: (jax.nn.sigmoid(g + b) * y,), 3, (0, 1), (2,))(g_, y_, b_, dm)

    dproj, dys, dbs = None, [], []
    for k, t in enumerate(("y_a", "y_b", "y_c")):
        dproj, dy_k, db_k = _row_call(
            f"gate_b{i}_{k}", gate_bwd, D, 1, [_rows(proj, D, k), _rows(sv[t], D), _rows(dmerged, D)],
            [_p_row(P["b_gate"], D, k)], [BF16, BF16], [_a_row(D, D)], into={0: (dproj, k, NP)})
        dys.append(dy_k)
        dbs.append(db_k)
    dya, dyb, dyc = dys
    G["b_gate"] = jnp.concatenate(dbs, axis=1)[0]

    dyc_pre = _matmul(f"wc_bx{i}", dyc, W["w_c"], "nt", out_dtype=BF16)
    G["w_c"] = _matmul(f"wc_bw{i}", sv["yc_pre"], dyc, "tn", out_dtype=BF16)

    def gnorm_bwd(y_, z_, dy_, w_):
        return _with_vjp(_gated_norm_fn, 3, (0, 1), (2,))(y_, z_, w_, dy_)

    dy_ssd, dproj, dnw = _row_call(
        f"ssdnorm_b{i}", gnorm_bwd, 512, 2,
        [_rows(sv["y_ssd"], 512), _rows(proj, 512, OFF_Z // 512), _rows(dyc_pre, 512)],
        [_p_row(P["ssd_norm_w"], 512)], [F32, BF16], [_a_row(D, 512)], into={1: (dproj, OFF_Z // 512, NP)})
    G["ssd_norm_w"] = dnw[0]
    dxbc_c, dproj, da_row, ddtb_row, dd_exp = _ssd_bwd(f"ssd_b{i}", sv["xbc_c"], proj, sv["states"], dy_ssd,
                                                       P["a_row"], P["dtb_row"], P["d_exp"], dproj)
    a_vec = P["a_row"][0, :SSD_HEADS]
    G["ssd_a_log"] = da_row[0, :SSD_HEADS] * a_vec
    G["ssd_dt_bias"] = ddtb_row[0, :SSD_HEADS]
    G["ssd_d"] = dd_exp.reshape(SSD_HEADS, HEAD_DIM).sum(axis=1)
    conv_params = [_p_row(P["ssd_conv_w"][k], 128) for k in range(4)] + [_p_row(P["ssd_conv_b"], 128)]

    def conv_bwd(x_, dy_, w0, w1, w2, w3, b_):
        return _with_vjp(_ssd_conv_fn, 6, (0,), (1, 2, 3, 4, 5))(x_, w0, w1, w2, w3, b_, dy_)

    dproj, c0, c1, c2, c3, cb = _col_call(
        f"ssdconv_b{i}", conv_bwd, 128, SSD_XBC // 128, [_rows(proj, 128, OFF_XBC // 128), _rows(dxbc_c, 128)],
        conv_params, [BF16], [_a_row(SSD_XBC, 128)] * 5, into={0: (dproj, OFF_XBC // 128, NP)})
    G["ssd_conv_w"] = jnp.concatenate([c0, c1, c2, c3], axis=0)
    G["ssd_conv_b"] = cb[0]

    dyb_pre = _matmul(f"wb_bx{i}", dyb, W["w_b"], "nt", out_dtype=BF16)
    G["w_b"] = _matmul(f"wb_bw{i}", sv["yb_pre"], dyb, "tn", out_dtype=BF16)
    pool_params = [(W["pool_w"], (N_DEV, None, 32, 256), lambda j, i_: (0, j, 0, 0)), _p_row(P["pool_scale"], 256)]

    def pool_bwd(x_, dy_, wg, sc):
        return _with_vjp(_pool_fn, 3, (0,), (1, 2))(x_, wg.astype(F32), sc, dy_)

    dproj, dwg, dsc = _col_call(
        f"pool_b{i}", pool_bwd, 256, 4, [_rows(proj, 256, OFF_POOL // 256), _rows(dyb_pre, 256)], pool_params, [BF16],
        [((N_DEV, 4, 32, 256), (N_DEV, None, 32, 256), lambda j, i_: (0, j, 0, 0)), _a_row(D, 256)],
        into={0: (dproj, OFF_POOL // 256, NP)})
    G["pool_w"] = dwg
    G["pool_scale"] = dsc[0]

    datt = _matmul(f"wa_bx{i}", dya, W["w_a"], "nt", out_dtype=BF16)
    G["w_a"] = _matmul(f"wa_bw{i}", sv["att"], dya, "tn", out_dtype=BF16)

    def merge_bwd(o0, o1, o2, l0, l1, l2, da_):
        return _with_vjp(_att_merge_fn, 6, (0, 1, 2, 3, 4, 5), ())(o0, o1, o2, l0, l1, l2, da_)

    dol = _row_call(f"attmerge_b{i}", merge_bwd, ATT_GW, 1,
                    [_rows(t, ATT_GW) for t in sv["att_o"] + sv["att_l"]] + [_rows(datt, ATT_GW)], [], [F32] * 6)
    g_rel = jnp.zeros((REL_BUCKETS, 18), F32)
    for gi in range(len(ATT_GROUPS)):
        dproj, gbp, gbc = _att_bwd(f"att_b{i}_{gi}", proj, gi, bias_tabs[gi][0], bias_tabs[gi][1],
                                   dol[gi], dol[3 + gi], dproj)
        oh = onehots[gi]
        gt = (jnp.einsum("hqk,qkb->bh", gbp, oh[0], precision=lax.Precision.HIGHEST)
              + jnp.einsum("hqk,qkb->bh", gbc, oh[1], precision=lax.Precision.HIGHEST))
        g_rel = g_rel.at[:, gi * 6:(gi + 1) * 6].add(gt)
    G["rel_bias"] = g_rel

    du = _matmul(f"inproj_bx{i}", dproj, W["w_in"], "nt")
    G["w_in"] = _matmul(f"inproj_bw{i}", sv["u"], dproj, "tn", out_dtype=BF16)
    ln1_g = P["ln1_g"] + on_sharded_grads(G)
    (dx,), (G["ln1_g"],) = _split_res(_row_call(
        f"ln1_b{i}", lambda x_, du_, dres, g_: norm_bwd(x_, g_, du_, dres), D, 1,
        [_rows(x, D), _rows(du, D), _rows(dx1, D)], [_p_row(ln1_g, D)], [F32], [_a_row(D, D)]), 1)
    G["ln1_g"] = G["ln1_g"][0]
    G["ln2_g"] = G["ln2_g"][0]
    return dx, G


def _rows_as_param(arr, cw):
    return (arr, (arr.shape[0], cw), lambda j, i: (0, j))


def _split_res(res, n_out):
    return tuple(res[:n_out]), tuple(res[n_out:])


def kernel(x, rel_bias, ln1_g, w_in, b_gate, w_a, pool_w, pool_scale, w_b, ssd_conv_w, ssd_conv_b, ssd_dt_bias, ssd_a_log, ssd_d, ssd_norm_w, w_c, w_o, ln2_g, ffn_w_up, ffn_conv_w, ffn_conv_b, ffn_w_down, final_g, loss_target, m_rel_bias, m_ln1_g, m_w_in, m_b_gate, m_w_a, m_pool_w, m_pool_scale, m_w_b, m_ssd_conv_w, m_ssd_conv_b, m_ssd_dt_bias, m_ssd_a_log, m_ssd_d, m_ssd_norm_w, m_w_c, m_w_o, m_ln2_g, m_ffn_w_up, m_ffn_conv_w, m_ffn_conv_b, m_ffn_w_down, m_final_g, v_rel_bias, v_ln1_g, v_w_in, v_b_gate, v_w_a, v_pool_w, v_pool_scale, v_w_b, v_ssd_conv_w, v_ssd_conv_b, v_ssd_dt_bias, v_ssd_a_log, v_ssd_d, v_ssd_norm_w, v_w_c, v_w_o, v_ln2_g, v_ffn_w_up, v_ffn_conv_w, v_ffn_conv_b, v_ffn_w_down, v_final_g):
    args = locals()
    wts = {n: args[n] for n in WEIGHTS}
    mom = {n: args["m_" + n] for n in WEIGHTS}
    var = {n: args["v_" + n] for n in WEIGHTS}
    names = list(SHARDED)

    onehots = _bucket_onehots()
    bias_tabs = []
    for gi in range(3):
        tab = rel_bias[:, gi * 6:(gi + 1) * 6]
        b = jnp.einsum("pqkb,bh->phqk", onehots[gi], tab, precision=lax.Precision.HIGHEST)
        bias_tabs.append((b[0], b[1]))

    def gather_start(tag, i, which, after):
        shards = [wts[n][i].astype(BF16) if n in MATMUL_WEIGHTS else wts[n][i] for n in which]
        return _exchange_start(f"gather_start{tag}", shards, False, after)

    def layer_params(i, which, landed):
        full = {n: _local_weight(f"local_{n}{i}", n, g) for n, g in zip(which, landed)}
        W = {n: full[n] for n in which if n in MATMUL_WEIGHTS}
        P = {}
        if "ssd_conv_w" in full:
            P["ssd_conv_w"] = [_row(full["ssd_conv_w"][k]) for k in range(4)]
            P["ffn_conv_w"] = [_row(full["ffn_conv_w"][k]) for k in range(3)]
        return W, P

    def replicated_params(i):
        return {"ln1_g": _row(ln1_g[i]), "ln2_g": _row(ln2_g[i]), "b_gate": _row(b_gate[i]),
                "pool_scale": _row(pool_scale[i]), "ssd_conv_b": _row(ssd_conv_b[i]),
                "ssd_norm_w": _row(ssd_norm_w[i]), "ffn_conv_b": _row(ffn_conv_b[i]),
                "a_row": _row(-jnp.exp(ssd_a_log[i]), 128), "dtb_row": _row(ssd_dt_bias[i], 128),
                "d_exp": _row(jnp.repeat(ssd_d[i], HEAD_DIM))}

    h = x.reshape(S, D)
    saved, Ws, Ps = [], [], []
    first, rest = ["w_in"], [n for n in names if n != "w_in"]
    landed_first = [_gather_chip_once("gather_w_in0", w_in[0].astype(BF16))]
    state_rest, token = gather_start("0b", 0, rest, landed_first[0])
    nxt = {}

    def late0(att):
        landed_rest = _exchange_wait("gather_wait0b", state_rest, att)
        W2, P2 = layer_params(0, rest, landed_rest)
        nxt["state"], tok = gather_start("1", 1, names, landed_rest[0])
        P2["pool_scale"] = _row(pool_scale[0]) + tok[0, 0]
        return W2, P2

    for i in range(DEPTH):
        P = replicated_params(i)
        if i == 0:
            W, P1 = layer_params(0, first, landed_first)
        else:
            W, P1 = layer_params(i, names, landed)
            if i + 1 < DEPTH:
                nxt["state"], token = gather_start(str(i + 1), i + 1, names, landed[0])
        P.update(P1)
        if i + 1 < DEPTH:
            P["ln1_g"] = P["ln1_g"] + token[0, 0]
        h, sv = _layer_fwd(i, h, W, P, bias_tabs, late0 if i == 0 else None)
        Ws.append(W)
        Ps.append(dict(P, ln1_g=_row(ln1_g[i]), pool_scale=_row(pool_scale[i])))
        saved.append(sv)
        if i + 1 < DEPTH:
            landed = _exchange_wait(f"gather_wait{i + 1}", nxt["state"], h)

    def loss_bwd(x_, t_, g_):
        lval, vjp = jax.vjp(_loss_fn, x_, t_, g_)
        dx_, _, dg_ = vjp(jnp.ones_like(lval))
        return (dx_,), (dg_, jnp.broadcast_to(lval, (1, 128)))

    dh, g_final, loss_part = _row_call("loss", loss_bwd, D, 1, [_rows(h, D), _rows(loss_target.reshape(S, D), D)],
                                       [_p_row(_row(final_g), D)], [F32], [_a_row(D, D), _a_row(128, 128)])
    loss = lax.psum(loss_part[0, 0], MESH_AXES)

    grads = {n: [None] * DEPTH for n in WEIGHTS if n not in ("rel_bias", "final_g")}
    g_rel = jnp.zeros((REL_BUCKETS, 18), F32)
    slots = [None] * DEPTH
    pending = None
    for i in reversed(range(DEPTH)):
        started = {}

        def on_sharded_grads(G, i=i, started=started):
            parts = [_device_blocks(f"blocks_{n}{i}", n, G[n]) for n in names]
            started["state"], token = _exchange_start(f"scatter_start{i}", parts, True, G["b_gate"])
            return token[0, 0]

        dh, G = _layer_bwd(i, dh, saved[i], Ws[i], Ps[i], bias_tabs, onehots, on_sharded_grads)
        if pending is not None:
            j, st = pending
            slots[j] = _exchange_wait(f"scatter_wait{j}", st, dh)
        pending = (i, started["state"])
        g_rel = g_rel + G.pop("rel_bias")
        for n, g in G.items():
            grads[n][i] = g
    grad_x = dh.reshape(1, S, D)
    local = {n: jnp.stack(grads[n]) for n in grads if n not in SHARDED}
    local["rel_bias"] = g_rel
    local["final_g"] = g_final[0]
    out = {}

    def pack(d):
        flat = jnp.concatenate([d[n].reshape(-1).astype(F32) for n in REPLICATED])
        rows = -(-flat.shape[0] // (8 * 128)) * 8
        return jnp.pad(flat, (0, rows * 128 - flat.shape[0])).reshape(rows, 128)

    (rep_slots,) = _exchange("gather_small_grads", [pack(local)], scatter=False)
    rep = _adamw("adamw_small", rep_slots, pack(wts), pack(mom), pack(var))
    off = 0
    for n in REPLICATED:
        sz = int(np.prod(wts[n].shape))
        out[n] = [t.reshape(-1)[off:off + sz].reshape(wts[n].shape) for t in rep]
        off += sz

    def flat2(n):
        shp = wts[n].shape
        r, c = int(np.prod(shp[:-1])), shp[-1]
        return r, c, wts[n].reshape(r, c), mom[n].reshape(r, c), var[n].reshape(r, c)

    chain = {}
    done = rep[0][0, 0]
    for k, n in enumerate(names):
        if n in MATMUL_WEIGHTS:
            r, c, w2, m2, v2 = flat2(n)
            res = None
            for i in (3, 2, 1):
                res = _adamw(f"adamw_{n}{i}", slots[i][k].reshape(N_DEV, r // DEPTH, c), w2, m2, v2,
                             first_row=i * (r // DEPTH), prev=res)
            chain[n] = res
            done = done + res[0][-1, 0]
    slots[0] = _exchange_wait("scatter_wait0", pending[1], done.reshape(1, 1))
    for k, n in enumerate(names):
        r, c, w2, m2, v2 = flat2(n)
        if n in MATMUL_WEIGHTS:
            res = _adamw(f"adamw_{n}0", slots[0][k].reshape(N_DEV, r // DEPTH, c), w2, m2, v2, first_row=0, prev=chain[n])
        else:
            stacked = jnp.stack([slots[i][k] for i in range(DEPTH)], axis=1)
            res = _adamw("adamw_" + n, stacked.reshape(N_DEV, r, c), w2, m2, v2)
        out[n] = [t.reshape(wts[n].shape) for t in res]

    return (loss, grad_x, *[out[n][0] for n in WEIGHTS], *[out[n][1] for n in WEIGHTS],
            *[out[n][2] for n in WEIGHTS], *[out[n][3] for n in WEIGHTS])
```

```python
import functools
import math

import numpy as np
import jax
import jax.numpy as jnp
from jax import lax
from jax.experimental import pallas as pl
from jax.experimental.pallas import tpu as pltpu

F32 = jnp.float32
BF16 = jnp.bfloat16

N_DEV = 8
MESH_AXES = ("x", "y", "c")
S = 4096
D = 1024
DEPTH = 4
HEAD_DIM = 64
ATT_W = 1152
ATT_GW = 384
ATT_GROUPS = ((128, 1), (512, 4), (2048, 16))
ATT_BLOCK = 128
REL_BUCKETS = 32
REL_MAX_DISTANCE = 2048
POOL_WINDOWS = (2, 4, 8, 16)
SSD_HEADS = 16
SSD_CHUNK = 128
SSD_XBC = 1536
D_FF = 2816
IN_WIDTH = 10128
EPS = 1e-6
NEG = -1e30

OFF_GATE, OFF_POOL, OFF_Z, OFF_XBC, OFF_DT, OFF_QKV = 0, 3072, 4096, 5120, 6656, 6912
NP = 10368
DT_PAD = 128
QKV_W = 3 * 2 * HEAD_DIM

ADAM_LR, ADAM_B1, ADAM_B2, ADAM_EPS, ADAM_WD, ADAM_STEP = 0.001, 0.9, 0.999, 1e-08, 0.01, 10

VMEM_LIMIT = 52 * 1024 * 1024


def _cparams(sem=None):
    return pltpu.CompilerParams(dimension_semantics=sem, vmem_limit_bytes=VMEM_LIMIT)


def _dot(a, b, ca, cb):
    return lax.dot_general(a.astype(BF16), b.astype(BF16), (((ca,), (cb,)), ((), ())), preferred_element_type=F32)


@jax.custom_vjp
def _mm(a, b):
    return _dot(a, b, 1, 0)


def _mm_fwd(a, b):
    return _mm(a, b), (a, b)


def _mm_bwd(res, g):
    a, b = res
    return _dot(g, b, 1, 1).astype(a.dtype), _dot(a, g, 0, 0).astype(b.dtype)


_mm.defvjp(_mm_fwd, _mm_bwd)


@jax.custom_vjp
def _mm_nt(a, b):
    return _dot(a, b, 1, 1)


def _mm_nt_fwd(a, b):
    return _mm_nt(a, b), (a, b)


def _mm_nt_bwd(res, g):
    a, b = res
    return _dot(g, b, 1, 0).astype(a.dtype), _dot(g, a, 0, 0).astype(b.dtype)


_mm_nt.defvjp(_mm_nt_fwd, _mm_nt_bwd)


@jax.custom_vjp
def _mm_tn(a, b):
    return _dot(a, b, 0, 0)


def _mm_tn_fwd(a, b):
    return _mm_tn(a, b), (a, b)


def _mm_tn_bwd(res, g):
    a, b = res
    return _dot(b, g, 1, 1).astype(a.dtype), _dot(a, g, 1, 0).astype(b.dtype)


_mm_tn.defvjp(_mm_tn_fwd, _mm_tn_bwd)


def _shift_impl(x, j):
    n = x.shape[0]
    if j == 0:
        return x
    r = pltpu.roll(x, j % n, axis=0)
    t = lax.broadcasted_iota(jnp.int32, x.shape, 0)
    mask = (t >= j) if j > 0 else (t < n + j)
    return jnp.where(mask, r, 0.0)


@functools.partial(jax.custom_vjp, nondiff_argnums=(1,))
def _shift(x, j):
    return _shift_impl(x, j)


_shift.defvjp(lambda x, j: (_shift_impl(x, j), None), lambda j, _, g: (_shift_impl(g, -j),))


def _tri(lower):
    r = lax.broadcasted_iota(jnp.int32, (SSD_CHUNK, SSD_CHUNK), 0)
    c = lax.broadcasted_iota(jnp.int32, (SSD_CHUNK, SSD_CHUNK), 1)
    return (r >= c) if lower else (r <= c)


def _dot_hi(a, b):
    return lax.dot_general(a, b, (((1,), (0,)), ((), ())), precision=lax.Precision.HIGHEST,
                           preferred_element_type=F32)


@jax.custom_vjp
def _cumsum_rows(a):
    return _dot_hi(_tri(True).astype(F32), a)


_cumsum_rows.defvjp(lambda a: (_cumsum_rows(a), None), lambda _, g: (_dot_hi(_tri(False).astype(F32), g),))


@jax.custom_vjp
def _softplus(x):
    return jnp.maximum(x, 0.0) + jnp.log(1.0 + jnp.exp(-jnp.abs(x)))


_softplus.defvjp(lambda x: (_softplus(x), x), lambda x, g: (g * jax.nn.sigmoid(x),))


def _silu(x):
    return x * jax.nn.sigmoid(x)


def _rows(arr, cw, off=0, lead=None, roff=0):
    return (arr, cw, off, lead, roff)


def _tiled(name, fn, grid, tm, rows, params, outs, accs=(), out_roff=0, prev_outs=None, into=None):
    into = into or {}
    ncol, nrow = grid
    in_specs, operands = [], []
    for arr, cw, off, lead, roff in rows:
        if lead is None:
            in_specs.append(pl.BlockSpec((tm, cw), functools.partial(lambda j, i, off, roff: (roff + i, off + j),
                                                                     off=off, roff=roff)))
        else:
            in_specs.append(pl.BlockSpec((None, tm, cw), functools.partial(
                lambda j, i, off, lead, roff: (lead, roff + i, off + j), off=off, lead=lead, roff=roff)))
        operands.append(arr)
    for arr, bs, im in params:
        in_specs.append(pl.BlockSpec(bs, im))
        operands.append(arr)
    out_specs, out_shape = [], []
    for k, (n_rows, cw, dt) in enumerate(outs):
        _, coff, total = into.get(k, (None, 0, ncol * cw))
        out_specs.append(pl.BlockSpec((tm, cw), functools.partial(lambda j, i, r, c: (r + i, c + j), r=out_roff, c=coff)))
        out_shape.append(jax.ShapeDtypeStruct((n_rows, total), dt))
    for shape, bs, im in accs:
        out_specs.append(pl.BlockSpec(bs, im))
        out_shape.append(jax.ShapeDtypeStruct(shape, F32))
    n_in, n_out = len(operands), len(outs)
    aliases = {}
    earlier = dict(enumerate(prev_outs)) if prev_outs is not None else {}
    earlier.update({k: v[0] for k, v in into.items() if v[0] is not None})
    for k, p in sorted(earlier.items()):
        aliases[len(operands)] = k
        in_specs.append(pl.BlockSpec(memory_space=pl.ANY))
        operands.append(p)

    n_all = len(operands)

    def body(*refs):
        vals = [r[...] for r in refs[:n_in]]
        o_vals, a_vals = fn(*vals)
        for r, v in zip(refs[n_all:n_all + n_out], o_vals):
            r[...] = v.astype(r.dtype)
        i = pl.program_id(1)
        for r, v in zip(refs[n_all + n_out:], a_vals):
            @pl.when(i == 0)
            def _(r=r, v=v):
                r[...] = v.astype(r.dtype)

            @pl.when(i > 0)
            def _(r=r, v=v):
                r[...] += v.astype(r.dtype)

    res = pl.pallas_call(body, grid=grid, in_specs=in_specs, out_specs=out_specs, out_shape=out_shape, name=name,
                         input_output_aliases=aliases, compiler_params=_cparams(("arbitrary", "arbitrary")))(*operands)
    return list(res)


def _with_vjp(fn, n_prim, want_out, want_acc):
    def f(*args):
        prim, g = args[:n_prim], args[n_prim:]
        outs, vjp = jax.vjp(lambda *a: fn(*a), *prim)
        d = vjp(tuple(gi.astype(o.dtype) for gi, o in zip(g, outs)))
        return tuple(d[k] for k in want_out), tuple(d[k] for k in want_acc)
    return f


def _p_row(arr, cw, off=0):
    return (arr, (1, cw), functools.partial(lambda j, i, off: (0, off + j), off=off))


def _a_row(n, cw):
    return ((1, n), (1, cw), lambda j, i: (0, j))


def _pick(n, cap, mult):
    best = None
    for t in range(mult, min(n, cap) + 1, mult):
        if n % t == 0:
            best = t
    return best if best is not None else n


def _matmul(name, a, b, mode, add=None, out_dtype=F32):
    if mode == "nn":
        (M, K), N = a.shape, b.shape[1]
    elif mode == "nt":
        (M, K), N = a.shape, b.shape[0]
    else:
        (K, M), N = a.shape, b.shape[1]
    tn = _pick(N, 1536, 128)
    k_cap = 2048 if mode == "tn" else 3456
    tk = K if K <= k_cap else _pick(K, k_cap, 128)
    nk = K // tk
    tm = _pick(M, 1408, 128) if mode == "tn" else _pick(M, 1024, 8)
    a_bytes, b_bytes = a.size * a.dtype.itemsize, b.size * b.dtype.itemsize
    swap = nk == 1 and a_bytes * (N // tn) + b_bytes < b_bytes * (M // tm) + a_bytes
    ij = (lambda g0, g1: (g1, g0)) if swap else (lambda g0, g1: (g0, g1))

    def spec(block, index):
        return pl.BlockSpec(block, lambda g0, g1, k: index(*ij(g0, g1), k))

    if mode == "nn":
        a_spec = spec((tm, tk), lambda i, j, k: (i, k))
        b_spec = spec((tk, tn), lambda i, j, k: (k, j))
        ca, cb = 1, 0
    elif mode == "nt":
        a_spec = spec((tm, tk), lambda i, j, k: (i, k))
        b_spec = spec((tn, tk), lambda i, j, k: (j, k))
        ca, cb = 1, 1
    else:
        a_spec = spec((tk, tm), lambda i, j, k: (k, i))
        b_spec = spec((tk, tn), lambda i, j, k: (k, j))
        ca, cb = 0, 0
    in_specs, operands = [a_spec, b_spec], [a, b]
    if add is not None:
        in_specs.append(spec((tm, tn), lambda i, j, k: (i, j)))
        operands.append(add)

    def finish(r, refs, o_ref):
        if add is not None:
            r = r + refs[2][...]
        o_ref[...] = r.astype(o_ref.dtype)

    def body_single(*refs):
        finish(_dot(refs[0][...], refs[1][...], ca, cb), refs, refs[-1])

    def body_multi(*refs):
        o_ref, acc_ref = refs[-2], refs[-1]
        k = pl.program_id(2)
        d = _dot(refs[0][...], refs[1][...], ca, cb)

        @pl.when(k == 0)
        def _():
            acc_ref[...] = d

        @pl.when(jnp.logical_and(k > 0, k < nk - 1))
        def _():
            acc_ref[...] += d

        @pl.when(k == nk - 1)
        def _():
            finish(acc_ref[...] + d, refs, o_ref)

    grid = (N // tn, M // tm, nk) if swap else (M // tm, N // tn, nk)
    return pl.pallas_call(
        body_single if nk == 1 else body_multi, grid=grid, in_specs=in_specs,
        out_specs=spec((tm, tn), lambda i, j, k: (i, j)),
        out_shape=jax.ShapeDtypeStruct((M, N), out_dtype),
        scratch_shapes=[] if nk == 1 else [pltpu.VMEM((tm, tn), F32)], name=name,
        compiler_params=_cparams(("parallel", "parallel", "arbitrary")))(*operands)


def _seg_copies(segs, c):
    out = []
    for lo, hi, dst in segs:
        n = lo
        while n < hi:
            p = n // c
            w = min(hi, (p + 1) * c) - n
            out.append((p, n - p * c, w, dst + n - lo))
            n += w
    return out


def _col_assemble(name, blocks, copies, zeros, n_out):
    _, R, c = blocks.shape
    tm = R if R <= 128 else 128

    def body(b_ref, o_ref):
        for p, s, w, d in copies:
            o_ref[:, d:d + w] = b_ref[p, :, s:s + w]
        for lo, hi in zeros:
            o_ref[:, lo:hi] = jnp.zeros((tm, hi - lo), o_ref.dtype)

    return pl.pallas_call(
        body, grid=(R // tm,), in_specs=[pl.BlockSpec((N_DEV, tm, c), lambda i: (0, i, 0))],
        out_specs=pl.BlockSpec((tm, n_out), lambda i: (i, 0)),
        out_shape=jax.ShapeDtypeStruct((R, n_out), blocks.dtype), name=name, compiler_params=_cparams(("parallel",)))(blocks)


def _col_split(name, full, copies, c, dtype):
    R, n = full.shape
    tm = R if R <= 128 else 128

    def body(f_ref, o_ref):
        for p, s, w, d in copies:
            o_ref[p, :, s:s + w] = f_ref[:, d:d + w].astype(dtype)

    return pl.pallas_call(
        body, grid=(R // tm,), in_specs=[pl.BlockSpec((tm, n), lambda i: (i, 0))],
        out_specs=pl.BlockSpec((N_DEV, tm, c), lambda i: (0, i, 0)),
        out_shape=jax.ShapeDtypeStruct((N_DEV, R, c), dtype), name=name, compiler_params=_cparams(("parallel",)))(full)


def _rmsnorm_fn(x, g):
    x = x.astype(F32)
    return (x * lax.rsqrt(jnp.mean(x * x, axis=-1, keepdims=True) + EPS) * g,)


def _gate_merge_fn(g0, g1, g2, ya, yb, yc, b0, b1, b2):
    return (jax.nn.sigmoid(g0 + b0) * ya + jax.nn.sigmoid(g1 + b1) * yb + jax.nn.sigmoid(g2 + b2) * yc,)


def _gated_norm_fn(y, z, w):
    t = y * _silu(z)
    return (t * lax.rsqrt(jnp.mean(t * t, axis=-1, keepdims=True) + EPS) * w,)


def _att_merge_fn(o0, o1, o2, l0, l1, l2):
    m = lax.stop_gradient(jnp.maximum(jnp.maximum(l0, l1), l2))
    e0, e1, e2 = jnp.exp(l0 - m), jnp.exp(l1 - m), jnp.exp(l2 - m)
    return ((e0 * o0 + e1 * o1 + e2 * o2) / (e0 + e1 + e2),)


def _loss_fn(x, tgt, g):
    (y,) = _rmsnorm_fn(x, g)
    err = y - tgt
    return 0.5 * jnp.sum(jnp.mean(err * err, axis=-1, keepdims=True), axis=0, keepdims=True)


def _pool_fn(x, wg, scale):
    g = pl.program_id(0)
    s2 = x + _shift(x, 1)
    s4 = s2 + _shift(s2, 2)
    s8 = s4 + _shift(s4, 4)
    s16 = s8 + _shift(s8, 8)
    win = ((g == 0).astype(F32) * s2 + (g == 1).astype(F32) * s4 + (g == 2).astype(F32) * s8
           + (g == 3).astype(F32) * s16)
    t = lax.broadcasted_iota(jnp.int32, (x.shape[0], 1), 0) + 1
    cnt = jnp.minimum(t, jnp.left_shift(2, g)).astype(F32)
    d = win / cnt - x
    return (_mm(d, wg.reshape(256, 256)) * scale,)


def _dwconv(x, taps, b):
    k = len(taps)
    y = taps[k - 1] * x + b
    for i in range(k - 1):
        y = y + taps[i] * _shift(x, k - 1 - i)
    return y


def _ssd_conv_fn(x, w0, w1, w2, w3, b):
    return (_silu(_dwconv(x, (w0, w1, w2, w3), b)),)


def _ffn_act_fn(xa, xv, a0, a1, a2, ab, v0, v1, v2, vb):
    xa, xv = xa.astype(F32), xv.astype(F32)
    return (_silu(_dwconv(xa, (a0, a1, a2), ab)) * _dwconv(xv, (v0, v1, v2), vb),)


@jax.custom_vjp
def _halves(x):
    return x[:ATT_BLOCK], x[ATT_BLOCK:]


_halves.defvjp(lambda x: (_halves(x), None), lambda _, g: (jnp.concatenate([g[0], g[1]], axis=0),))


def _att_block(q, kp, kc, vp, vc, bpa, bpb, bca, bcb, prev_ok):
    n = ATT_BLOCK
    lane = lax.broadcasted_iota(jnp.int32, (1, 2 * HEAD_DIM), 1)
    ma = (lane < HEAD_DIM).astype(F32)
    mb = 1.0 - ma
    q = q.astype(F32) * (1.0 / math.sqrt(HEAD_DIM))
    q2 = jnp.concatenate([q * ma, q * mb], axis=0)
    qi = lax.broadcasted_iota(jnp.int32, (2 * n, n), 0) & (n - 1)
    kj = lax.broadcasted_iota(jnp.int32, (2 * n, n), 1)
    sp = jnp.where(jnp.logical_and(kj >= qi, prev_ok), _mm_nt(q2, kp) + jnp.concatenate([bpa, bpb], axis=0), NEG)
    sc = jnp.where(kj <= qi, _mm_nt(q2, kc) + jnp.concatenate([bca, bcb], axis=0), NEG)
    m = lax.stop_gradient(jnp.maximum(jnp.max(sp, axis=1, keepdims=True), jnp.max(sc, axis=1, keepdims=True)))
    pp = jnp.exp(sp - m)
    pc = jnp.exp(sc - m)
    l = jnp.sum(pp, axis=1, keepdims=True) + jnp.sum(pc, axis=1, keepdims=True)
    oa, ob = _halves((_mm(pp, vp) + _mm(pc, vc)) / l)
    la, lb = _halves((m + jnp.log(l)) * jnp.ones((1, 2 * HEAD_DIM), F32))
    return oa * ma + ob * mb, la * ma + lb * mb


def _att_slab(dil):
    nbk = 8 if dil == 1 else 1
    t = ATT_BLOCK * dil * nbk
    return nbk, t, S // t


def _att_in_specs(gi, t):
    def spec(which, prev):
        col = OFF_QKV // 128 + gi * 9 + which

        def index(p, j, col=col, prev=prev):
            jj = jnp.minimum(j, S // t - 1)
            return (jnp.maximum(jj - 1, 0) if prev else jj, col + 3 * p)
        return pl.BlockSpec((t, 2 * HEAD_DIM), index)
    return [spec(0, False), spec(1, False), spec(1, True), spec(2, False), spec(2, True)]


def _bias_specs():
    return [pl.BlockSpec((None, ATT_BLOCK, ATT_BLOCK), functools.partial(lambda p, j, hh: (2 * p + hh, 0, 0), hh=hh))
            for hh in (0, 1)]


def _att_units(dil, nbk, body):
    def per_residue(r, carry):
        for b in range(nbk):
            rows = pl.ds(b * ATT_BLOCK * dil + r, ATT_BLOCK, stride=dil)
            prev = pl.ds(((b - 1) % nbk) * ATT_BLOCK * dil + r, ATT_BLOCK, stride=dil)
            body(b, rows, prev, b > 0)
        return carry
    if dil == 1:
        per_residue(0, 0)
    else:
        lax.fori_loop(0, dil, per_residue, 0, unroll=min(dil, 8))


def _att_fwd(name, proj, gi, bias_p, bias_c):
    dil = ATT_GROUPS[gi][1]
    nbk, t, ns = _att_slab(dil)
    bsp = _bias_specs()
    out_spec = pl.BlockSpec((t, 2 * HEAD_DIM), lambda p, j: (j, p))

    def body(q_ref, kc_ref, kp_ref, vc_ref, vp_ref, bpa, bpb, bca, bcb, o_ref, l_ref):
        first = pl.program_id(1) == 0
        biases = (bpa[...], bpb[...], bca[...], bcb[...])

        def unit(b, rows, prev, in_slab):
            kp = kc_ref[prev, :] if in_slab else kp_ref[prev, :]
            vp = vc_ref[prev, :] if in_slab else vp_ref[prev, :]
            prev_ok = True if in_slab else jnp.logical_not(first)
            o, lse = _att_block(q_ref[rows, :], kp, kc_ref[rows, :], vp, vc_ref[rows, :], *biases, prev_ok)
            o_ref[rows, :] = o
            l_ref[rows, :] = lse

        _att_units(dil, nbk, unit)

    shp = jax.ShapeDtypeStruct((S, ATT_GW), F32)
    return pl.pallas_call(
        body, grid=(3, ns), in_specs=_att_in_specs(gi, t) + [bsp[0], bsp[1], bsp[0], bsp[1]],
        out_specs=[out_spec, out_spec], out_shape=[shp, shp], name=name,
        compiler_params=_cparams(("arbitrary",) * 2))(proj, proj, proj, proj, proj, bias_p, bias_p, bias_c, bias_c)


def _att_bwd(name, proj, gi, bias_p, bias_c, do, dl, dproj):
    dil = ATT_GROUPS[gi][1]
    nbk, t, ns = _att_slab(dil)
    bsp = _bias_specs()
    blk = (t, 2 * HEAD_DIM)
    cur = pl.BlockSpec(blk, lambda p, j: (jnp.minimum(j, ns - 1), p))
    done = pl.BlockSpec((t, QKV_W), lambda p, j: (jnp.maximum(j - 1, 0), OFF_QKV // QKV_W + gi * 3 + p))
    gsp = pl.BlockSpec((None, ATT_BLOCK, ATT_BLOCK), lambda p, j: (p, 0, 0))

    def body(q_ref, kc_ref, kp_ref, vc_ref, vp_ref, bpa, bpb, bca, bcb, do_ref, dl_ref, _,
             dqkv_ref, gpa, gpb, gca, gcb, accq, acck, accv):
        j = pl.program_id(1)
        mine, other = acck.at[j % 2], acck.at[1 - j % 2]
        mine_v, other_v = accv.at[j % 2], accv.at[1 - j % 2]
        dq_ref, other_q = accq.at[j % 2], accq.at[1 - j % 2]

        @pl.when(j == 0)
        def _():
            for g in (gpa, gpb, gca, gcb):
                g[...] = jnp.zeros_like(g)
            other[...] = jnp.zeros_like(other)
            other_v[...] = jnp.zeros_like(other_v)
            other_q[...] = jnp.zeros_like(other_q)

        @pl.when(j < ns)
        def _():
            mine[...] = jnp.zeros_like(mine)
            mine_v[...] = jnp.zeros_like(mine_v)
            biases = (bpa[...], bpb[...], bca[...], bcb[...])

            def unit(b, rows, prev, in_slab):
                kp = kc_ref[prev, :] if in_slab else kp_ref[prev, :]
                vp = vc_ref[prev, :] if in_slab else vp_ref[prev, :]
                prev_ok = True if in_slab else j > 0
                prim = (q_ref[rows, :], kp, kc_ref[rows, :], vp, vc_ref[rows, :]) + biases
                _, vjp = jax.vjp(lambda *a: _att_block(*a, prev_ok), *prim)
                dq, dkp, dkc, dvp, dvc, dpa, dpb, dca, dcb = vjp((do_ref[rows, :], dl_ref[rows, :]))
                dq_ref[rows, :] = dq
                mine[rows, :] += dkc
                mine_v[rows, :] += dvc
                tgt, tgt_v = (mine, mine_v) if in_slab else (other, other_v)
                tgt[prev, :] += dkp
                tgt_v[prev, :] += dvp
                gpa[...] += dpa
                gpb[...] += dpb
                gca[...] += dca
                gcb[...] += dcb

            _att_units(dil, nbk, unit)

        w = 2 * HEAD_DIM
        dqkv_ref[:, 0:w] = other_q[...].astype(BF16)
        dqkv_ref[:, w:2 * w] = other[...].astype(BF16)
        dqkv_ref[:, 2 * w:3 * w] = other_v[...].astype(BF16)

    gshp = jax.ShapeDtypeStruct((3, ATT_BLOCK, ATT_BLOCK), F32)
    res = pl.pallas_call(
        body, grid=(3, ns + 1),
        in_specs=_att_in_specs(gi, t) + [bsp[0], bsp[1], bsp[0], bsp[1], cur, cur, pl.BlockSpec(memory_space=pl.ANY)],
        out_specs=[done, gsp, gsp, gsp, gsp],
        out_shape=[jax.ShapeDtypeStruct((S, NP), BF16), gshp, gshp, gshp, gshp],
        input_output_aliases={11: 0},
        scratch_shapes=[pltpu.VMEM((2,) + blk, F32)] * 3, name=name,
        compiler_params=_cparams(("arbitrary",) * 2))(proj, proj, proj, proj, proj, bias_p, bias_p, bias_c, bias_c, do, dl,
                                                      dproj)
    dproj, gpa, gpb, gca, gcb = res
    heads = lambda a, b: jnp.stack([a, b], axis=1).reshape(6, ATT_BLOCK, ATT_BLOCK)
    return dproj, heads(gpa, gpb), heads(gca, gcb)


N_PAIR = SSD_HEADS // 2


def _ssd_chunk(xs, bs, cs_in, dt_raw, hs, a_row, dtb_row, ds):
    lane = lax.broadcasted_iota(jnp.int32, (1, 128), 1)
    row = lax.broadcasted_iota(jnp.int32, (128, 1), 0)
    tril = _tri(True)
    dt = _softplus(dt_raw + dtb_row)
    acs = _cumsum_rows(dt * a_row)
    acs_t = acs.T
    gmat = [_mm_nt(cs_in[g], bs[g]) for g in range(2)]
    lo = lane < HEAD_DIM
    lo_r = row < HEAD_DIM
    last = (row == SSD_CHUNK - 1).astype(F32)
    ys, hn = [], []
    for p in range(N_PAIR):
        g = p // (N_PAIR // 2)
        col, dtc, mm, clast = [], [], [], []
        for hh in range(2):
            h = 2 * p + hh
            oh = (lane == h).astype(F32)
            c_col = jnp.sum(acs * oh, axis=1, keepdims=True)
            c_row = jnp.sum(acs_t * (row == h).astype(F32), axis=0, keepdims=True)
            col.append(c_col)
            dtc.append(jnp.sum(dt * oh, axis=1, keepdims=True))
            clast.append(jnp.sum(c_col * last, axis=0, keepdims=True))
            mm.append(gmat[g] * jnp.exp(jnp.where(tril, c_col - c_row, NEG)))
        x = xs[p]
        xd = x * jnp.where(lo, dtc[0], dtc[1])
        y = jnp.where(lo, _mm(mm[0], xd), _mm(mm[1], xd))
        y = y + jnp.where(lo, jnp.exp(col[0]), jnp.exp(col[1])) * _mm_nt(cs_in[g], hs[p])
        ys.append(y + ds[p] * x)
        dec = jnp.where(lo, jnp.exp(clast[0] - col[0]), jnp.exp(clast[1] - col[1]))
        hn.append(hs[p] * jnp.where(lo_r, jnp.exp(clast[0]), jnp.exp(clast[1])) + _mm_tn(xd * dec, bs[g]))
    return tuple(ys), tuple(hn)


def _ssd_load(xbc_ref, dt_ref, a_ref, dtb_ref, d_ref):
    xs = tuple(xbc_ref[:, 128 * p:128 * (p + 1)] for p in range(N_PAIR))
    bs = tuple(xbc_ref[:, D + 128 * g:D + 128 * (g + 1)] for g in range(2))
    cs = tuple(xbc_ref[:, D + 256 + 128 * g:D + 256 + 128 * (g + 1)] for g in range(2))
    ds = tuple(d_ref[:, 128 * p:128 * (p + 1)] for p in range(N_PAIR))
    return xs, bs, cs, dt_ref[...], a_ref[...], dtb_ref[...], ds


def _ssd_fwd(name, xbc_c, proj, a_row, dtb_row, d_exp):
    nc = S // SSD_CHUNK
    prow = lambda n: pl.BlockSpec((1, n), lambda c: (0, 0))

    def body(xbc_ref, dt_ref, a_ref, dtb_ref, d_ref, y_ref, st_ref, h_ref):
        @pl.when(pl.program_id(0) == 0)
        def _():
            h_ref[...] = jnp.zeros_like(h_ref)

        xs, bs, cs, dt_raw, a, dtb, ds = _ssd_load(xbc_ref, dt_ref, a_ref, dtb_ref, d_ref)
        hs = tuple(h_ref[p] for p in range(N_PAIR))
        ys, hn = _ssd_chunk(xs, bs, cs, dt_raw, hs, a, dtb, ds)
        for p in range(N_PAIR):
            y_ref[:, 128 * p:128 * (p + 1)] = ys[p]
            st_ref[p] = hs[p]
            h_ref[p] = hn[p]

    return pl.pallas_call(
        body, grid=(nc,),
        in_specs=[pl.BlockSpec((SSD_CHUNK, SSD_XBC), lambda c: (c, 0)),
                  pl.BlockSpec((SSD_CHUNK, DT_PAD), lambda c: (c, OFF_DT // DT_PAD)),
                  prow(128), prow(128), prow(D)],
        out_specs=[pl.BlockSpec((SSD_CHUNK, D), lambda c: (c, 0)),
                   pl.BlockSpec((None, N_PAIR, 128, 128), lambda c: (c, 0, 0, 0))],
        out_shape=[jax.ShapeDtypeStruct((S, D), F32), jax.ShapeDtypeStruct((nc, N_PAIR, 128, 128), F32)],
        scratch_shapes=[pltpu.VMEM((N_PAIR, 128, 128), F32)], name=name,
        compiler_params=_cparams(("arbitrary",)))(xbc_c, proj, a_row, dtb_row, d_exp)


def _ssd_bwd(name, xbc_c, proj, states, dy, a_row, dtb_row, d_exp, dproj):
    nc = S // SSD_CHUNK
    prow = lambda n: pl.BlockSpec((1, n), lambda i: (0, 0))
    rc = lambda i: nc - 1 - i

    def body(xbc_ref, dt_ref, st_ref, dy_ref, a_ref, dtb_ref, d_ref, _, dxbc_ref, ddt_ref, da_ref, ddtb_ref, dd_ref, e_ref):
        i = pl.program_id(0)

        @pl.when(i == 0)
        def _():
            e_ref[...] = jnp.zeros_like(e_ref)
            da_ref[...] = jnp.zeros_like(da_ref)
            ddtb_ref[...] = jnp.zeros_like(ddtb_ref)
            dd_ref[...] = jnp.zeros_like(dd_ref)

        xs, bs, cs, dt_raw, a, dtb, ds = _ssd_load(xbc_ref, dt_ref, a_ref, dtb_ref, d_ref)
        hs = tuple(st_ref[p] for p in range(N_PAIR))
        _, vjp = jax.vjp(_ssd_chunk, xs, bs, cs, dt_raw, hs, a, dtb, ds)
        dys = tuple(dy_ref[:, 128 * p:128 * (p + 1)] for p in range(N_PAIR))
        es = tuple(e_ref[p] for p in range(N_PAIR))
        dxs, dbs, dcs, ddt, dhs, da, ddtb, dds = vjp((dys, es))
        for p in range(N_PAIR):
            dxbc_ref[:, 128 * p:128 * (p + 1)] = dxs[p]
            e_ref[p] = dhs[p]
            dd_ref[:, 128 * p:128 * (p + 1)] += dds[p]
        for g in range(2):
            dxbc_ref[:, D + 128 * g:D + 128 * (g + 1)] = dbs[g]
            dxbc_ref[:, D + 256 + 128 * g:D + 256 + 128 * (g + 1)] = dcs[g]
        ddt_ref[:, :DT_PAD] = ddt.astype(BF16)
        ddt_ref[:, DT_PAD:] = jnp.zeros((SSD_CHUNK, OFF_QKV - OFF_DT - DT_PAD), BF16)
        da_ref[...] += da
        ddtb_ref[...] += ddtb

    dt_w = OFF_QKV - OFF_DT
    return pl.pallas_call(
        body, grid=(nc,),
        in_specs=[pl.BlockSpec((SSD_CHUNK, SSD_XBC), lambda i: (rc(i), 0)),
                  pl.BlockSpec((SSD_CHUNK, DT_PAD), lambda i: (rc(i), OFF_DT // DT_PAD)),
                  pl.BlockSpec((None, N_PAIR, 128, 128), lambda i: (rc(i), 0, 0, 0)),
                  pl.BlockSpec((SSD_CHUNK, D), lambda i: (rc(i), 0)),
                  prow(128), prow(128), prow(D), pl.BlockSpec(memory_space=pl.ANY)],
        out_specs=[pl.BlockSpec((SSD_CHUNK, SSD_XBC), lambda i: (rc(i), 0)),
                   pl.BlockSpec((SSD_CHUNK, dt_w), lambda i: (rc(i), OFF_DT // dt_w)),
                   prow(128), prow(128), prow(D)],
        out_shape=[jax.ShapeDtypeStruct((S, SSD_XBC), F32), jax.ShapeDtypeStruct((S, NP), BF16),
                   jax.ShapeDtypeStruct((1, 128), F32), jax.ShapeDtypeStruct((1, 128), F32),
                   jax.ShapeDtypeStruct((1, D), F32)],
        input_output_aliases={7: 1},
        scratch_shapes=[pltpu.VMEM((N_PAIR, 128, 128), F32)], name=name,
        compiler_params=_cparams(("arbitrary",)))(xbc_c, proj, states, dy, a_row, dtb_row, d_exp, dproj)


def _exchange(name, arrays, scatter):
    n = len(arrays)
    flips = [(dx, dy, dc) for dx in (0, 1) for dy in (0, 1) for dc in (0, 1) if dx or dy or dc]

    def body(*refs):
        ins, outs = refs[:n], refs[n:2 * n]
        send_sems, recv_sems, loc_sems = refs[2 * n:]
        x, y, c = lax.axis_index("x"), lax.axis_index("y"), lax.axis_index("c")
        me = 4 * x + 2 * y + c
        peers = []
        for dx, dy, dc in flips:
            px, py, pc = (1 - x if dx else x), (1 - y if dy else y), (1 - c if dc else c)
            peers.append(((px, py, pc), 4 * px + 2 * py + pc))

        def remote(k, j, landed_from):
            dev, pid = peers[j]
            src = ins[k].at[pid] if scatter else ins[k]
            return pltpu.make_async_remote_copy(
                src_ref=src, dst_ref=outs[k].at[landed_from], send_sem=send_sems.at[k, j], recv_sem=recv_sems.at[k, j],
                device_id=dev, device_id_type=pl.DeviceIdType.MESH)

        local = [pltpu.make_async_copy(ins[k].at[me] if scatter else ins[k], outs[k].at[me], loc_sems.at[k])
                 for k in range(n)]
        for cp in local:
            cp.start()
        for k in range(n):
            for j in range(len(flips)):
                remote(k, j, me).start()
        for cp in local:
            cp.wait()
        for k in range(n):
            for j in range(len(flips)):
                remote(k, j, me).wait_send()
                remote(k, j, peers[j][1]).wait_recv()

    hbm = pl.BlockSpec(memory_space=pltpu.HBM)
    out_shape = [jax.ShapeDtypeStruct(a.shape if scatter else (N_DEV,) + a.shape, a.dtype) for a in arrays]
    res = pl.pallas_call(
        body, in_specs=[hbm] * n, out_specs=[hbm] * n, out_shape=out_shape, name=name,
        scratch_shapes=[pltpu.SemaphoreType.DMA((n, len(flips))), pltpu.SemaphoreType.DMA((n, len(flips))),
                        pltpu.SemaphoreType.DMA((n,))])(*arrays)
    return list(res)


def _gather_chip_once(name, block):
    def body(x_ref, out_ref, send_sems, recv_sems, loc_sem):
        x, y, c = lax.axis_index("x"), lax.axis_index("y"), lax.axis_index("c")
        me, sibling = (x, y, c), (x, y, 1 - c)
        chips = [(1 - x, y), (x, 1 - y), (1 - x, 1 - y)]

        def slot(px, py, pc):
            return out_ref.at[4 * px + 2 * py + pc]

        def copy(k, blk, to, src=None):
            return pltpu.make_async_remote_copy(
                src_ref=slot(*blk) if src is None else src, dst_ref=slot(*blk), send_sem=send_sems.at[k],
                recv_sem=recv_sems.at[k], device_id=to, device_id_type=pl.DeviceIdType.MESH)

        mine = pltpu.make_async_copy(x_ref, slot(*me), loc_sem)
        mine.start()
        first = [copy(0, me, sibling, src=x_ref)] + [copy(1 + j, me, (*chip, c), src=x_ref) for j, chip in enumerate(chips)]
        for cp in first:
            cp.start()
        passed = [copy(4 + j, (*chip, c), sibling) for j, chip in enumerate(chips)]
        for j, chip in enumerate(chips):
            copy(1 + j, (*chip, c), me).wait_recv()
            passed[j].start()
        copy(0, sibling, me).wait_recv()
        for j, chip in enumerate(chips):
            copy(4 + j, (*chip, 1 - c), me).wait_recv()
        for cp in first + passed:
            cp.wait_send()
        mine.wait()

    hbm = pl.BlockSpec(memory_space=pltpu.HBM)
    return pl.pallas_call(
        body, in_specs=[hbm], out_specs=hbm, out_shape=jax.ShapeDtypeStruct((N_DEV,) + block.shape, block.dtype), name=name,
        scratch_shapes=[pltpu.SemaphoreType.DMA((N_DEV - 1,)), pltpu.SemaphoreType.DMA((N_DEV - 1,)),
                        pltpu.SemaphoreType.DMA(())])(block)


def _peer_copies(ins, lands, send_sems, recv_sems, loc_sems, scatter):
    n = len(ins)
    flips = [(dx, dy, dc) for dx in (0, 1) for dy in (0, 1) for dc in (0, 1) if dx or dy or dc]
    x, y, c = lax.axis_index("x"), lax.axis_index("y"), lax.axis_index("c")
    me = 4 * x + 2 * y + c
    peers = []
    for dx, dy, dc in flips:
        px, py, pc = (1 - x if dx else x), (1 - y if dy else y), (1 - c if dc else c)
        peers.append(((px, py, pc), 4 * px + 2 * py + pc))

    def remote(k, j, slot):
        dev, pid = peers[j]
        return pltpu.make_async_remote_copy(
            src_ref=ins[k].at[pid] if scatter else ins[k], dst_ref=lands[k].at[slot],
            send_sem=send_sems.at[k * N_FLIP + j], recv_sem=recv_sems.at[k * N_FLIP + j],
            device_id=dev, device_id_type=pl.DeviceIdType.MESH)

    local = [pltpu.make_async_copy(ins[k].at[me] if scatter else ins[k], lands[k].at[me], loc_sems.at[k])
             for k in range(n)]
    pairs = [(k, j) for k in range(n) for j in range(len(flips))]
    sent = lambda k, j: remote(k, j, me)
    landed = lambda k, j: remote(k, j, peers[j][1])
    return local, pairs, sent, landed


_HBM = pl.BlockSpec(memory_space=pltpu.HBM)
_SEM = pl.BlockSpec(memory_space=pltpu.SEMAPHORE)
N_FLIP = N_DEV - 1


def _exchange_start(name, arrays, scatter, after):
    n = len(arrays)
    arrays = [pltpu.with_memory_space_constraint(a, pltpu.HBM) for a in arrays]
    lands = [pltpu.with_memory_space_constraint(
        lax.empty(a.shape if scatter else (N_DEV,) + a.shape, a.dtype), pltpu.HBM) for a in arrays]

    def body(*refs):
        ins, lnd = refs[:n], refs[n:2 * n]
        send_sems, recv_sems, loc_sems = refs[2 * n + 1:2 * n + 4]
        token = refs[-1]
        local, pairs, sent, _ = _peer_copies(ins, lnd, send_sems, recv_sems, loc_sems, scatter)
        for cp in local:
            cp.start()
        for k, j in pairs:
            sent(k, j).start()
        token[...] = jnp.zeros_like(token)

    res = pl.pallas_call(
        body, name=name,
        in_specs=[_HBM] * (2 * n) + [pl.BlockSpec(memory_space=pl.ANY)],
        out_specs=[_SEM, _SEM, _SEM] + [_HBM] * (2 * n) + [pl.BlockSpec(memory_space=pltpu.VMEM)],
        out_shape=[pltpu.SemaphoreType.DMA((n * N_FLIP,)), pltpu.SemaphoreType.DMA((n * N_FLIP,)), pltpu.SemaphoreType.DMA((n,))]
        + [pltpu.HBM(a.shape, a.dtype) for a in arrays] + [pltpu.HBM(a.shape, a.dtype) for a in lands]
        + [jax.ShapeDtypeStruct((8, 128), F32)],
        input_output_aliases={k: 3 + k for k in range(2 * n)},
        compiler_params=pltpu.CompilerParams(has_side_effects=pltpu.SideEffectType.DATAFLOW_SIDE_EFFECTING),
    )(*arrays, *lands, after)
    return (res[:3], res[3:3 + n], res[3 + n:3 + 2 * n], scatter), res[-1]


def _exchange_wait(name, state, after):
    sems, ins_thru, lands_thru, scatter = state
    n = len(ins_thru)

    def body(*refs):
        ins, lnd = refs[:n], refs[n:2 * n]
        send_sems, recv_sems, loc_sems = refs[2 * n:2 * n + 3]
        local, pairs, sent, landed = _peer_copies(ins, lnd, send_sems, recv_sems, loc_sems, scatter)
        for cp in local:
            cp.wait()
        for k, j in pairs:
            sent(k, j).wait_send()
            landed(k, j).wait_recv()

    res = pl.pallas_call(
        body, name=name,
        in_specs=[_HBM] * (2 * n) + [_SEM, _SEM, _SEM] + [pl.BlockSpec(memory_space=pl.ANY)],
        out_specs=[_HBM] * (2 * n),
        out_shape=[pltpu.HBM(a.shape, a.dtype) for a in ins_thru] + [pltpu.HBM(a.shape, a.dtype) for a in lands_thru],
        input_output_aliases={k: k for k in range(2 * n)},
        compiler_params=pltpu.CompilerParams(has_side_effects=pltpu.SideEffectType.DATAFLOW_SIDE_EFFECTING),
    )(*ins_thru, *lands_thru, *sems, after)
    return list(res[n:])


def _adamw_fn(*vals):
    slots, (w, m, v) = vals[:N_DEV], vals[N_DEV:]
    g = slots[0].astype(F32)
    for s in slots[1:]:
        g = g + s.astype(F32)
    m2 = ADAM_B1 * m + (1.0 - ADAM_B1) * g
    v2 = ADAM_B2 * v + (1.0 - ADAM_B2) * (g * g)
    m_hat = m2 / (1.0 - ADAM_B1 ** ADAM_STEP)
    v_hat = v2 / (1.0 - ADAM_B2 ** ADAM_STEP)
    delta = -ADAM_LR * (m_hat / (jnp.sqrt(v_hat) + ADAM_EPS) + ADAM_WD * w)
    return (g, delta, m2, v2), ()


def _adamw(name, slots, w, m, v, first_row=0, prev=None):
    R, C = slots.shape[1:]
    tm = R if R <= 128 else _pick(R, 128 if C > D else 256, 8)
    rows = ([_rows(slots, C, lead=s) for s in range(N_DEV)]
            + [_rows(a, C, roff=first_row // tm) for a in (w, m, v)])
    return _tiled(name, _adamw_fn, (1, R // tm), tm, rows, [], [(w.shape[0], C, F32)] * 4,
                  out_roff=first_row // tm, prev_outs=prev)


def _bucket_onehots():
    out = []
    qi = jnp.arange(ATT_BLOCK)[:, None]
    kj = jnp.arange(ATT_BLOCK)[None, :]
    max_exact = REL_BUCKETS // 2
    for _, dil in ATT_GROUPS:
        parts = []
        for rel in (qi + ATT_BLOCK - kj, qi - kj):
            dist = jnp.clip(rel, 0, None) * dil
            nf = jnp.maximum(dist, 1).astype(F32)
            large = max_exact + (jnp.log(nf / max_exact) / math.log(REL_MAX_DISTANCE / max_exact)
                                 * (REL_BUCKETS - max_exact)).astype(jnp.int32)
            large = jnp.minimum(large, REL_BUCKETS - 1)
            bucket = jnp.where(dist < max_exact, dist, large)
            parts.append((bucket[:, :, None] == jnp.arange(REL_BUCKETS)[None, None, :]).astype(F32))
        out.append(jnp.stack(parts))
    return out


SHARDED = ("w_in", "w_a", "pool_w", "w_b", "ssd_conv_w", "w_c", "w_o", "ffn_w_up", "ffn_conv_w", "ffn_w_down")
MATMUL_WEIGHTS = ("w_in", "w_a", "pool_w", "w_b", "w_c", "w_o", "ffn_w_up", "ffn_w_down")
ROW_SHARDED = ("w_b", "w_c", "w_o", "ffn_w_down")
W_IN_SEGS = tuple(
    (which * ATT_W + unit * 128, which * ATT_W + (unit + 1) * 128, OFF_QKV + unit * QKV_W + which * 128)
    for unit in range(9) for which in range(3)
) + ((3456, 4480, OFF_POOL), (4480, 5504, OFF_Z), (5504, 7040, OFF_XBC), (7040, 7056, OFF_DT), (7056, IN_WIDTH, OFF_GATE))
COL_SHARDED = {
    "w_in": (IN_WIDTH // N_DEV, W_IN_SEGS, ((OFF_DT + SSD_HEADS, OFF_QKV),), NP),
    "w_a": (D // N_DEV, ((0, D, 0),), (), D),
    "ffn_w_up": (2 * D_FF // N_DEV, ((0, 2 * D_FF, 0),), (), 2 * D_FF),
    "ssd_conv_w": (SSD_XBC // N_DEV, ((0, SSD_XBC, 0),), (), SSD_XBC),
    "ffn_conv_w": (2 * D_FF // N_DEV, ((0, 2 * D_FF, 0),), (), 2 * D_FF),
}
REPLICATED = ("rel_bias", "ln1_g", "b_gate", "pool_scale", "ssd_conv_b", "ssd_dt_bias", "ssd_a_log", "ssd_d",
              "ssd_norm_w", "ln2_g", "ffn_conv_b", "final_g")
WEIGHTS = ("rel_bias", "ln1_g", "w_in", "b_gate", "w_a", "pool_w", "pool_scale", "w_b", "ssd_conv_w", "ssd_conv_b",
           "ssd_dt_bias", "ssd_a_log", "ssd_d", "ssd_norm_w", "w_c", "w_o", "ln2_g", "ffn_w_up", "ffn_conv_w",
           "ffn_conv_b", "ffn_w_down", "final_g")


def _local_weight(name, n, blocks):
    if n in COL_SHARDED:
        c, segs, zeros, width = COL_SHARDED[n]
        return _col_assemble(name, blocks, _seg_copies(segs, c), zeros, width)
    if n in ROW_SHARDED:
        return blocks.reshape(-1, blocks.shape[-1])
    return blocks


def _device_blocks(name, n, g):
    if n in COL_SHARDED:
        c, segs, _, _ = COL_SHARDED[n]
        return _col_split(name, g, _seg_copies(segs, c), c, BF16)
    if n in ROW_SHARDED:
        return g.reshape(N_DEV, g.shape[0] // N_DEV, g.shape[1]).astype(BF16)
    return g.astype(BF16)


def _row(v, n=None):
    v = v.reshape(1, -1)
    if n is not None and v.shape[1] < n:
        v = jnp.pad(v, ((0, 0), (0, n - v.shape[1])))
    return v


RT = 512


def _row_call(name, fn, cw, ncol, rows, params, outs, accs=(), into=None):
    return _tiled(name, fn, (ncol, S // RT), RT, rows, params, [(S, cw, dt) for dt in outs], accs, into=into)


def _col_call(name, fn, tc, ncol, rows, params, outs, accs=(), into=None):
    return _tiled(name, fn, (ncol, 1), S, rows, params, [(S, tc, dt) for dt in outs], accs, into=into)


def _fwd_only(fn):
    return lambda *a: (fn(*a), ())


def _layer_fwd(i, x, W, P, bias_tabs, late=None):
    sv = {"x": x}
    (u,) = _row_call(f"ln1_f{i}", _fwd_only(_rmsnorm_fn), D, 1, [_rows(x, D)], [_p_row(P["ln1_g"], D)], [BF16])
    proj = _matmul(f"inproj_f{i}", u, W["w_in"], "nn")
    sv["u"], sv["proj"] = u, proj

    os_, ls_ = [], []
    for gi in range(len(ATT_GROUPS)):
        o, lse = _att_fwd(f"att_f{i}_{gi}", proj, gi, bias_tabs[gi][0], bias_tabs[gi][1])
        os_.append(o)
        ls_.append(lse)
    sv["att_o"], sv["att_l"] = os_, ls_
    (att,) = _row_call(f"attmerge_f{i}", _fwd_only(_att_merge_fn), ATT_GW, 1,
                       [_rows(t, ATT_GW) for t in os_ + ls_], [], [BF16])
    if late is not None:
        W2, P2 = late(att)
        W.update(W2)
        P.update(P2)
    y_a = _matmul(f"wa_f{i}", att, W["w_a"], "nn", out_dtype=BF16)
    sv["att"], sv["y_a"] = att, y_a

    pool_params = [(W["pool_w"], (N_DEV, None, 32, 256), lambda j, i_: (0, j, 0, 0)), _p_row(P["pool_scale"], 256)]
    (yb_pre,) = _col_call(f"pool_f{i}", _fwd_only(_pool_fn), 256, 4, [_rows(proj, 256, OFF_POOL // 256)],
                          pool_params, [BF16])
    y_b = _matmul(f"wb_f{i}", yb_pre, W["w_b"], "nn", out_dtype=BF16)
    sv["yb_pre"], sv["y_b"] = yb_pre, y_b

    conv_params = [_p_row(P["ssd_conv_w"][k], 128) for k in range(4)] + [_p_row(P["ssd_conv_b"], 128)]
    (xbc_c,) = _col_call(f"ssdconv_f{i}", _fwd_only(_ssd_conv_fn), 128, SSD_XBC // 128,
                         [_rows(proj, 128, OFF_XBC // 128)], conv_params, [F32])
    y_ssd, states = _ssd_fwd(f"ssd_f{i}", xbc_c, proj, P["a_row"], P["dtb_row"], P["d_exp"])
    (yc_pre,) = _row_call(f"ssdnorm_f{i}", _fwd_only(_gated_norm_fn), 512, 2,
                          [_rows(y_ssd, 512), _rows(proj, 512, OFF_Z // 512)], [_p_row(P["ssd_norm_w"], 512)], [BF16])
    y_c = _matmul(f"wc_f{i}", yc_pre, W["w_c"], "nn", out_dtype=BF16)
    sv["xbc_c"], sv["states"], sv["y_ssd"], sv["yc_pre"], sv["y_c"] = xbc_c, states, y_ssd, yc_pre, y_c

    gate_rows = [_rows(proj, D, k) for k in range(3)] + [_rows(t, D) for t in (y_a, y_b, y_c)]
    gate_params = [_p_row(P["b_gate"], D, k) for k in range(3)]
    (merged,) = _row_call(f"gate_f{i}", _fwd_only(_gate_merge_fn), D, 1, gate_rows, gate_params, [BF16])
    x1 = _matmul(f"wo_f{i}", merged, W["w_o"], "nn", add=x)
    sv["merged"], sv["x1"] = merged, x1

    (u2,) = _row_call(f"ln2_f{i}", _fwd_only(_rmsnorm_fn), D, 1, [_rows(x1, D)], [_p_row(P["ln2_g"], D)], [BF16])
    up = _matmul(f"up_f{i}", u2, W["ffn_w_up"], "nn", out_dtype=BF16)
    (act,) = _col_call(f"ffnact_f{i}", _fwd_only(_ffn_act_fn), 128, D_FF // 128,
                       [_rows(up, 128), _rows(up, 128, D_FF // 128)], _ffn_params(P), [BF16])
    x2 = _matmul(f"down_f{i}", act, W["ffn_w_down"], "nn", add=x1)
    sv["u2"], sv["up"], sv["act"] = u2, up, act
    return x2, sv


def _ffn_params(P):
    nb = D_FF // 128
    return ([_p_row(P["ffn_conv_w"][k], 128) for k in range(3)] + [_p_row(P["ffn_conv_b"], 128)]
            + [_p_row(P["ffn_conv_w"][k], 128, nb) for k in range(3)] + [_p_row(P["ffn_conv_b"], 128, nb)])


def _layer_bwd(i, dx2, sv, W, P, bias_tabs, onehots, on_sharded_grads):
    G = {}
    x, proj, x1 = sv["x"], sv["proj"], sv["x1"]

    dact = _matmul(f"down_bx{i}", dx2, W["ffn_w_down"], "nt", out_dtype=BF16)
    G["ffn_w_down"] = _matmul(f"down_bw{i}", sv["act"], dx2, "tn", out_dtype=BF16)
    nb = D_FF // 128
    up = sv["up"]
    f = _with_vjp(_ffn_act_fn, 10, (0, 1), tuple(range(2, 10)))
    accs = [_a_row(D_FF, 128)] * 8
    dua, duv, a0, a1, a2, ab, v0, v1, v2, vb = _col_call(
        f"ffnact_b{i}", f, 128, nb, [_rows(up, 128), _rows(up, 128, nb)], _ffn_params(P) + [_rows_as_param(dact, 128)],
        [BF16, BF16], accs)
    G["ffn_conv_w"] = jnp.concatenate([jnp.concatenate([a0, a1, a2], 0), jnp.concatenate([v0, v1, v2], 0)], axis=1)
    G["ffn_conv_b"] = jnp.concatenate([ab, vb], axis=1)[0]
    dup = jnp.concatenate([dua, duv], axis=1)
    du2 = _matmul(f"up_bx{i}", dup, W["ffn_w_up"], "nt")
    G["ffn_w_up"] = _matmul(f"up_bw{i}", sv["u2"], dup, "tn", out_dtype=BF16)

    def norm_bwd(x_, g_, du_, dres):
        (dxn,), (dg,) = _with_vjp(_rmsnorm_fn, 2, (0,), (1,))(x_, g_, du_)
        return (dxn + dres,), (dg,)

    (dx1,), (G["ln2_g"],) = _split_res(_row_call(
        f"ln2_b{i}", lambda x_, du_, dres, g_: norm_bwd(x_, g_, du_, dres), D, 1,
        [_rows(x1, D), _rows(du2, D), _rows(dx2, D)], [_p_row(P["ln2_g"], D)], [F32], [_a_row(D, D)]), 1)

    dmerged = _matmul(f"wo_bx{i}", dx1, W["w_o"], "nt", out_dtype=BF16)
    G["w_o"] = _matmul(f"wo_bw{i}", sv["merged"], dx1, "tn", out_dtype=BF16)
    def gate_bwd(g_, y_, dm, b_):
        return _with_vjp(lambda g, y, b: (jax.nn.sigmoid(g + b) * y,), 3, (0, 1), (2,))(g_, y_, b_, dm)

    dproj, dys, dbs = None, [], []
    for k, t in enumerate(("y_a", "y_b", "y_c")):
        dproj, dy_k, db_k = _row_call(
            f"gate_b{i}_{k}", gate_bwd, D, 1, [_rows(proj, D, k), _rows(sv[t], D), _rows(dmerged, D)],
            [_p_row(P["b_gate"], D, k)], [BF16, BF16], [_a_row(D, D)], into={0: (dproj, k, NP)})
        dys.append(dy_k)
        dbs.append(db_k)
    dya, dyb, dyc = dys
    G["b_gate"] = jnp.concatenate(dbs, axis=1)[0]

    dyc_pre = _matmul(f"wc_bx{i}", dyc, W["w_c"], "nt", out_dtype=BF16)
    G["w_c"] = _matmul(f"wc_bw{i}", sv["yc_pre"], dyc, "tn", out_dtype=BF16)

    def gnorm_bwd(y_, z_, dy_, w_):
        return _with_vjp(_gated_norm_fn, 3, (0, 1), (2,))(y_, z_, w_, dy_)

    dy_ssd, dproj, dnw = _row_call(
        f"ssdnorm_b{i}", gnorm_bwd, 512, 2,
        [_rows(sv["y_ssd"], 512), _rows(proj, 512, OFF_Z // 512), _rows(dyc_pre, 512)],
        [_p_row(P["ssd_norm_w"], 512)], [F32, BF16], [_a_row(D, 512)], into={1: (dproj, OFF_Z // 512, NP)})
    G["ssd_norm_w"] = dnw[0]
    dxbc_c, dproj, da_row, ddtb_row, dd_exp = _ssd_bwd(f"ssd_b{i}", sv["xbc_c"], proj, sv["states"], dy_ssd,
                                                       P["a_row"], P["dtb_row"], P["d_exp"], dproj)
    a_vec = P["a_row"][0, :SSD_HEADS]
    G["ssd_a_log"] = da_row[0, :SSD_HEADS] * a_vec
    G["ssd_dt_bias"] = ddtb_row[0, :SSD_HEADS]
    G["ssd_d"] = dd_exp.reshape(SSD_HEADS, HEAD_DIM).sum(axis=1)
    conv_params = [_p_row(P["ssd_conv_w"][k], 128) for k in range(4)] + [_p_row(P["ssd_conv_b"], 128)]

    def conv_bwd(x_, dy_, w0, w1, w2, w3, b_):
        return _with_vjp(_ssd_conv_fn, 6, (0,), (1, 2, 3, 4, 5))(x_, w0, w1, w2, w3, b_, dy_)

    dproj, c0, c1, c2, c3, cb = _col_call(
        f"ssdconv_b{i}", conv_bwd, 128, SSD_XBC // 128, [_rows(proj, 128, OFF_XBC // 128), _rows(dxbc_c, 128)],
        conv_params, [BF16], [_a_row(SSD_XBC, 128)] * 5, into={0: (dproj, OFF_XBC // 128, NP)})
    G["ssd_conv_w"] = jnp.concatenate([c0, c1, c2, c3], axis=0)
    G["ssd_conv_b"] = cb[0]

    dyb_pre = _matmul(f"wb_bx{i}", dyb, W["w_b"], "nt", out_dtype=BF16)
    G["w_b"] = _matmul(f"wb_bw{i}", sv["yb_pre"], dyb, "tn", out_dtype=BF16)
    pool_params = [(W["pool_w"], (N_DEV, None, 32, 256), lambda j, i_: (0, j, 0, 0)), _p_row(P["pool_scale"], 256)]

    def pool_bwd(x_, dy_, wg, sc):
        return _with_vjp(_pool_fn, 3, (0,), (1, 2))(x_, wg.astype(F32), sc, dy_)

    dproj, dwg, dsc = _col_call(
        f"pool_b{i}", pool_bwd, 256, 4, [_rows(proj, 256, OFF_POOL // 256), _rows(dyb_pre, 256)], pool_params, [BF16],
        [((N_DEV, 4, 32, 256), (N_DEV, None, 32, 256), lambda j, i_: (0, j, 0, 0)), _a_row(D, 256)],
        into={0: (dproj, OFF_POOL // 256, NP)})
    G["pool_w"] = dwg
    G["pool_scale"] = dsc[0]

    datt = _matmul(f"wa_bx{i}", dya, W["w_a"], "nt", out_dtype=BF16)
    G["w_a"] = _matmul(f"wa_bw{i}", sv["att"], dya, "tn", out_dtype=BF16)

    def merge_bwd(o0, o1, o2, l0, l1, l2, da_):
        return _with_vjp(_att_merge_fn, 6, (0, 1, 2, 3, 4, 5), ())(o0, o1, o2, l0, l1, l2, da_)

    dol = _row_call(f"attmerge_b{i}", merge_bwd, ATT_GW, 1,
                    [_rows(t, ATT_GW) for t in sv["att_o"] + sv["att_l"]] + [_rows(datt, ATT_GW)], [], [F32] * 6)
    g_rel = jnp.zeros((REL_BUCKETS, 18), F32)
    for gi in range(len(ATT_GROUPS)):
        dproj, gbp, gbc = _att_bwd(f"att_b{i}_{gi}", proj, gi, bias_tabs[gi][0], bias_tabs[gi][1],
                                   dol[gi], dol[3 + gi], dproj)
        oh = onehots[gi]
        gt = (jnp.einsum("hqk,qkb->bh", gbp, oh[0], precision=lax.Precision.HIGHEST)
              + jnp.einsum("hqk,qkb->bh", gbc, oh[1], precision=lax.Precision.HIGHEST))
        g_rel = g_rel.at[:, gi * 6:(gi + 1) * 6].add(gt)
    G["rel_bias"] = g_rel

    du = _matmul(f"inproj_bx{i}", dproj, W["w_in"], "nt")
    G["w_in"] = _matmul(f"inproj_bw{i}", sv["u"], dproj, "tn", out_dtype=BF16)
    ln1_g = P["ln1_g"] + on_sharded_grads(G)
    (dx,), (G["ln1_g"],) = _split_res(_row_call(
        f"ln1_b{i}", lambda x_, du_, dres, g_: norm_bwd(x_, g_, du_, dres), D, 1,
        [_rows(x, D), _rows(du, D), _rows(dx1, D)], [_p_row(ln1_g, D)], [F32], [_a_row(D, D)]), 1)
    G["ln1_g"] = G["ln1_g"][0]
    G["ln2_g"] = G["ln2_g"][0]
    return dx, G


def _rows_as_param(arr, cw):
    return (arr, (arr.shape[0], cw), lambda j, i: (0, j))


def _split_res(res, n_out):
    return tuple(res[:n_out]), tuple(res[n_out:])


def kernel(x, rel_bias, ln1_g, w_in, b_gate, w_a, pool_w, pool_scale, w_b, ssd_conv_w, ssd_conv_b, ssd_dt_bias, ssd_a_log, ssd_d, ssd_norm_w, w_c, w_o, ln2_g, ffn_w_up, ffn_conv_w, ffn_conv_b, ffn_w_down, final_g, loss_target, m_rel_bias, m_ln1_g, m_w_in, m_b_gate, m_w_a, m_pool_w, m_pool_scale, m_w_b, m_ssd_conv_w, m_ssd_conv_b, m_ssd_dt_bias, m_ssd_a_log, m_ssd_d, m_ssd_norm_w, m_w_c, m_w_o, m_ln2_g, m_ffn_w_up, m_ffn_conv_w, m_ffn_conv_b, m_ffn_w_down, m_final_g, v_rel_bias, v_ln1_g, v_w_in, v_b_gate, v_w_a, v_pool_w, v_pool_scale, v_w_b, v_ssd_conv_w, v_ssd_conv_b, v_ssd_dt_bias, v_ssd_a_log, v_ssd_d, v_ssd_norm_w, v_w_c, v_w_o, v_ln2_g, v_ffn_w_up, v_ffn_conv_w, v_ffn_conv_b, v_ffn_w_down, v_final_g):
    args = locals()
    wts = {n: args[n] for n in WEIGHTS}
    mom = {n: args["m_" + n] for n in WEIGHTS}
    var = {n: args["v_" + n] for n in WEIGHTS}
    names = list(SHARDED)

    onehots = _bucket_onehots()
    bias_tabs = []
    for gi in range(3):
        tab = rel_bias[:, gi * 6:(gi + 1) * 6]
        b = jnp.einsum("pqkb,bh->phqk", onehots[gi], tab, precision=lax.Precision.HIGHEST)
        bias_tabs.append((b[0], b[1]))

    def gather_start(tag, i, which, after):
        shards = [wts[n][i].astype(BF16) if n in MATMUL_WEIGHTS else wts[n][i] for n in which]
        return _exchange_start(f"gather_start{tag}", shards, False, after)

    def layer_params(i, which, landed):
        full = {n: _local_weight(f"local_{n}{i}", n, g) for n, g in zip(which, landed)}
        W = {n: full[n] for n in which if n in MATMUL_WEIGHTS}
        P = {}
        if "ssd_conv_w" in full:
            P["ssd_conv_w"] = [_row(full["ssd_conv_w"][k]) for k in range(4)]
            P["ffn_conv_w"] = [_row(full["ffn_conv_w"][k]) for k in range(3)]
        return W, P

    def replicated_params(i):
        return {"ln1_g": _row(ln1_g[i]), "ln2_g": _row(ln2_g[i]), "b_gate": _row(b_gate[i]),
                "pool_scale": _row(pool_scale[i]), "ssd_conv_b": _row(ssd_conv_b[i]),
                "ssd_norm_w": _row(ssd_norm_w[i]), "ffn_conv_b": _row(ffn_conv_b[i]),
                "a_row": _row(-jnp.exp(ssd_a_log[i]), 128), "dtb_row": _row(ssd_dt_bias[i], 128),
                "d_exp": _row(jnp.repeat(ssd_d[i], HEAD_DIM))}

    h = x.reshape(S, D)
    saved, Ws, Ps = [], [], []
    first, rest = ["w_in"], [n for n in names if n != "w_in"]
    landed_first = [_gather_chip_once("gather_w_in0", w_in[0].astype(BF16))]
    state_rest, token = gather_start("0b", 0, rest, landed_first[0])
    nxt = {}

    def late0(att):
        landed_rest = _exchange_wait("gather_wait0b", state_rest, att)
        W2, P2 = layer_params(0, rest, landed_rest)
        nxt["state"], tok = gather_start("1", 1, names, landed_rest[0])
        P2["pool_scale"] = _row(pool_scale[0]) + tok[0, 0]
        return W2, P2

    for i in range(DEPTH):
        P = replicated_params(i)
        if i == 0:
            W, P1 = layer_params(0, first, landed_first)
        else:
            W, P1 = layer_params(i, names, landed)
            if i + 1 < DEPTH:
                nxt["state"], token = gather_start(str(i + 1), i + 1, names, landed[0])
        P.update(P1)
        if i + 1 < DEPTH:
            P["ln1_g"] = P["ln1_g"] + token[0, 0]
        h, sv = _layer_fwd(i, h, W, P, bias_tabs, late0 if i == 0 else None)
        Ws.append(W)
        Ps.append(dict(P, ln1_g=_row(ln1_g[i]), pool_scale=_row(pool_scale[i])))
        saved.append(sv)
        if i + 1 < DEPTH:
            landed = _exchange_wait(f"gather_wait{i + 1}", nxt["state"], h)

    def loss_bwd(x_, t_, g_):
        lval, vjp = jax.vjp(_loss_fn, x_, t_, g_)
        dx_, _, dg_ = vjp(jnp.ones_like(lval))
        return (dx_,), (dg_, jnp.broadcast_to(lval, (1, 128)))

    dh, g_final, loss_part = _row_call("loss", loss_bwd, D, 1, [_rows(h, D), _rows(loss_target.reshape(S, D), D)],
                                       [_p_row(_row(final_g), D)], [F32], [_a_row(D, D), _a_row(128, 128)])
    loss = lax.psum(loss_part[0, 0], MESH_AXES)

    grads = {n: [None] * DEPTH for n in WEIGHTS if n not in ("rel_bias", "final_g")}
    g_rel = jnp.zeros((REL_BUCKETS, 18), F32)
    slots = [None] * DEPTH
    pending = None
    for i in reversed(range(DEPTH)):
        started = {}

        def on_sharded_grads(G, i=i, started=started):
            parts = [_device_blocks(f"blocks_{n}{i}", n, G[n]) for n in names]
            started["state"], token = _exchange_start(f"scatter_start{i}", parts, True, G["b_gate"])
            return token[0, 0]

        dh, G = _layer_bwd(i, dh, saved[i], Ws[i], Ps[i], bias_tabs, onehots, on_sharded_grads)
        if pending is not None:
            j, st = pending
            slots[j] = _exchange_wait(f"scatter_wait{j}", st, dh)
        pending = (i, started["state"])
        g_rel = g_rel + G.pop("rel_bias")
        for n, g in G.items():
            grads[n][i] = g
    grad_x = dh.reshape(1, S, D)
    local = {n: jnp.stack(grads[n]) for n in grads if n not in SHARDED}
    local["rel_bias"] = g_rel
    local["final_g"] = g_final[0]
    out = {}

    def pack(d):
        flat = jnp.concatenate([d[n].reshape(-1).astype(F32) for n in REPLICATED])
        rows = -(-flat.shape[0] // (8 * 128)) * 8
        return jnp.pad(flat, (0, rows * 128 - flat.shape[0])).reshape(rows, 128)

    (rep_slots,) = _exchange("gather_small_grads", [pack(local)], scatter=False)
    rep = _adamw("adamw_small", rep_slots, pack(wts), pack(mom), pack(var))
    off = 0
    for n in REPLICATED:
        sz = int(np.prod(wts[n].shape))
        out[n] = [t.reshape(-1)[off:off + sz].reshape(wts[n].shape) for t in rep]
        off += sz

    def flat2(n):
        shp = wts[n].shape
        r, c = int(np.prod(shp[:-1])), shp[-1]
        return r, c, wts[n].reshape(r, c), mom[n].reshape(r, c), var[n].reshape(r, c)

    chain = {}
    done = rep[0][0, 0]
    for k, n in enumerate(names):
        if n in MATMUL_WEIGHTS:
            r, c, w2, m2, v2 = flat2(n)
            res = None
            for i in (3, 2, 1):
                res = _adamw(f"adamw_{n}{i}", slots[i][k].reshape(N_DEV, r // DEPTH, c), w2, m2, v2,
                             first_row=i * (r // DEPTH), prev=res)
            chain[n] = res
            done = done + res[0][-1, 0]
    slots[0] = _exchange_wait("scatter_wait0", pending[1], done.reshape(1, 1))
    for k, n in enumerate(names):
        r, c, w2, m2, v2 = flat2(n)
        if n in MATMUL_WEIGHTS:
            res = _adamw(f"adamw_{n}0", slots[0][k].reshape(N_DEV, r // DEPTH, c), w2, m2, v2, first_row=0, prev=chain[n])
        else:
            stacked = jnp.stack([slots[i][k] for i in range(DEPTH)], axis=1)
            res = _adamw("adamw_" + n, stacked.reshape(N_DEV, r, c), w2, m2, v2)
        out[n] = [t.reshape(wts[n].shape) for t in res]

    return (loss, grad_x, *[out[n][0] for n in WEIGHTS], *[out[n][1] for n in WEIGHTS],
            *[out[n][2] for n in WEIGHTS], *[out[n][3] for n in WEIGHTS])
```

```python
import functools
import math

import numpy as np
import jax
import jax.numpy as jnp
from jax import lax
from jax.experimental import pallas as pl
from jax.experimental.pallas import tpu as pltpu

F32 = jnp.float32
BF16 = jnp.bfloat16

N_DEV = 8
MESH_AXES = ("x", "y", "c")
S = 4096
D = 1024
DEPTH = 4
HEAD_DIM = 64
ATT_W = 1152
ATT_GW = 384
ATT_GROUPS = ((128, 1), (512, 4), (2048, 16))
ATT_BLOCK = 128
REL_BUCKETS = 32
REL_MAX_DISTANCE = 2048
POOL_WINDOWS = (2, 4, 8, 16)
SSD_HEADS = 16
SSD_CHUNK = 128
SSD_XBC = 1536
D_FF = 2816
IN_WIDTH = 10128
EPS = 1e-6
NEG = -1e30

OFF_GATE, OFF_POOL, OFF_Z, OFF_XBC, OFF_DT, OFF_QKV = 0, 3072, 4096, 5120, 6656, 6912
NP = 10368
DT_PAD = 128
QKV_W = 3 * 2 * HEAD_DIM

ADAM_LR, ADAM_B1, ADAM_B2, ADAM_EPS, ADAM_WD, ADAM_STEP = 0.001, 0.9, 0.999, 1e-08, 0.01, 10

VMEM_LIMIT = 52 * 1024 * 1024


def _cparams(sem=None):
    return pltpu.CompilerParams(dimension_semantics=sem, vmem_limit_bytes=VMEM_LIMIT)


def _dot(a, b, ca, cb):
    return lax.dot_general(a.astype(BF16), b.astype(BF16), (((ca,), (cb,)), ((), ())), preferred_element_type=F32)


@jax.custom_vjp
def _mm(a, b):
    return _dot(a, b, 1, 0)


def _mm_fwd(a, b):
    return _mm(a, b), (a, b)


def _mm_bwd(res, g):
    a, b = res
    return _dot(g, b, 1, 1).astype(a.dtype), _dot(a, g, 0, 0).astype(b.dtype)


_mm.defvjp(_mm_fwd, _mm_bwd)


@jax.custom_vjp
def _mm_nt(a, b):
    return _dot(a, b, 1, 1)


def _mm_nt_fwd(a, b):
    return _mm_nt(a, b), (a, b)


def _mm_nt_bwd(res, g):
    a, b = res
    return _dot(g, b, 1, 0).astype(a.dtype), _dot(g, a, 0, 0).astype(b.dtype)


_mm_nt.defvjp(_mm_nt_fwd, _mm_nt_bwd)


@jax.custom_vjp
def _mm_tn(a, b):
    return _dot(a, b, 0, 0)


def _mm_tn_fwd(a, b):
    return _mm_tn(a, b), (a, b)


def _mm_tn_bwd(res, g):
    a, b = res
    return _dot(b, g, 1, 1).astype(a.dtype), _dot(a, g, 1, 0).astype(b.dtype)


_mm_tn.defvjp(_mm_tn_fwd, _mm_tn_bwd)


def _shift_impl(x, j):
    n = x.shape[0]
    if j == 0:
        return x
    r = pltpu.roll(x, j % n, axis=0)
    t = lax.broadcasted_iota(jnp.int32, x.shape, 0)
    mask = (t >= j) if j > 0 else (t < n + j)
    return jnp.where(mask, r, 0.0)


@functools.partial(jax.custom_vjp, nondiff_argnums=(1,))
def _shift(x, j):
    return _shift_impl(x, j)


_shift.defvjp(lambda x, j: (_shift_impl(x, j), None), lambda j, _, g: (_shift_impl(g, -j),))


def _tri(lower):
    r = lax.broadcasted_iota(jnp.int32, (SSD_CHUNK, SSD_CHUNK), 0)
    c = lax.broadcasted_iota(jnp.int32, (SSD_CHUNK, SSD_CHUNK), 1)
    return (r >= c) if lower else (r <= c)


def _dot_hi(a, b):
    return lax.dot_general(a, b, (((1,), (0,)), ((), ())), precision=lax.Precision.HIGHEST,
                           preferred_element_type=F32)


@jax.custom_vjp
def _cumsum_rows(a):
    return _dot_hi(_tri(True).astype(F32), a)


_cumsum_rows.defvjp(lambda a: (_cumsum_rows(a), None), lambda _, g: (_dot_hi(_tri(False).astype(F32), g),))


@jax.custom_vjp
def _softplus(x):
    return jnp.maximum(x, 0.0) + jnp.log(1.0 + jnp.exp(-jnp.abs(x)))


_softplus.defvjp(lambda x: (_softplus(x), x), lambda x, g: (g * jax.nn.sigmoid(x),))


def _silu(x):
    return x * jax.nn.sigmoid(x)


def _rows(arr, cw, off=0, lead=None, roff=0):
    return (arr, cw, off, lead, roff)


def _tiled(name, fn, grid, tm, rows, params, outs, accs=(), out_roff=0, prev_outs=None, into=None):
    into = into or {}
    ncol, nrow = grid
    in_specs, operands = [], []
    for arr, cw, off, lead, roff in rows:
        if lead is None:
            in_specs.append(pl.BlockSpec((tm, cw), functools.partial(lambda j, i, off, roff: (roff + i, off + j),
                                                                     off=off, roff=roff)))
        else:
            in_specs.append(pl.BlockSpec((None, tm, cw), functools.partial(
                lambda j, i, off, lead, roff: (lead, roff + i, off + j), off=off, lead=lead, roff=roff)))
        operands.append(arr)
    for arr, bs, im in params:
        in_specs.append(pl.BlockSpec(bs, im))
        operands.append(arr)
    out_specs, out_shape = [], []
    for k, (n_rows, cw, dt) in enumerate(outs):
        _, coff, total = into.get(k, (None, 0, ncol * cw))
        out_specs.append(pl.BlockSpec((tm, cw), functools.partial(lambda j, i, r, c: (r + i, c + j), r=out_roff, c=coff)))
        out_shape.append(jax.ShapeDtypeStruct((n_rows, total), dt))
    for shape, bs, im in accs:
        out_specs.append(pl.BlockSpec(bs, im))
        out_shape.append(jax.ShapeDtypeStruct(shape, F32))
    n_in, n_out = len(operands), len(outs)
    aliases = {}
    earlier = dict(enumerate(prev_outs)) if prev_outs is not None else {}
    earlier.update({k: v[0] for k, v in into.items() if v[0] is not None})
    for k, p in sorted(earlier.items()):
        aliases[len(operands)] = k
        in_specs.append(pl.BlockSpec(memory_space=pl.ANY))
        operands.append(p)

    n_all = len(operands)

    def body(*refs):
        vals = [r[...] for r in refs[:n_in]]
        o_vals, a_vals = fn(*vals)
        for r, v in zip(refs[n_all:n_all + n_out], o_vals):
            r[...] = v.astype(r.dtype)
        i = pl.program_id(1)
        for r, v in zip(refs[n_all + n_out:], a_vals):
            @pl.when(i == 0)
            def _(r=r, v=v):
                r[...] = v.astype(r.dtype)

            @pl.when(i > 0)
            def _(r=r, v=v):
                r[...] += v.astype(r.dtype)

    res = pl.pallas_call(body, grid=grid, in_specs=in_specs, out_specs=out_specs, out_shape=out_shape, name=name,
                         input_output_aliases=aliases, compiler_params=_cparams(("arbitrary", "arbitrary")))(*operands)
    return list(res)


def _with_vjp(fn, n_prim, want_out, want_acc):
    def f(*args):
        prim, g = args[:n_prim], args[n_prim:]
        outs, vjp = jax.vjp(lambda *a: fn(*a), *prim)
        d = vjp(tuple(gi.astype(o.dtype) for gi, o in zip(g, outs)))
        return tuple(d[k] for k in want_out), tuple(d[k] for k in want_acc)
    return f


def _p_row(arr, cw, off=0):
    return (arr, (1, cw), functools.partial(lambda j, i, off: (0, off + j), off=off))


def _a_row(n, cw):
    return ((1, n), (1, cw), lambda j, i: (0, j))


def _pick(n, cap, mult):
    best = None
    for t in range(mult, min(n, cap) + 1, mult):
        if n % t == 0:
            best = t
    return best if best is not None else n


def _matmul(name, a, b, mode, add=None, out_dtype=F32):
    if mode == "nn":
        (M, K), N = a.shape, b.shape[1]
    elif mode == "nt":
        (M, K), N = a.shape, b.shape[0]
    else:
        (K, M), N = a.shape, b.shape[1]
    tn = _pick(N, 1536, 128)
    k_cap = 2048 if mode == "tn" else 3456
    tk = K if K <= k_cap else _pick(K, k_cap, 128)
    nk = K // tk
    tm = _pick(M, 1408, 128) if mode == "tn" else _pick(M, 1024, 8)
    a_bytes, b_bytes = a.size * a.dtype.itemsize, b.size * b.dtype.itemsize
    swap = nk == 1 and a_bytes * (N // tn) + b_bytes < b_bytes * (M // tm) + a_bytes
    ij = (lambda g0, g1: (g1, g0)) if swap else (lambda g0, g1: (g0, g1))

    def spec(block, index):
        return pl.BlockSpec(block, lambda g0, g1, k: index(*ij(g0, g1), k))

    if mode == "nn":
        a_spec = spec((tm, tk), lambda i, j, k: (i, k))
        b_spec = spec((tk, tn), lambda i, j, k: (k, j))
        ca, cb = 1, 0
    elif mode == "nt":
        a_spec = spec((tm, tk), lambda i, j, k: (i, k))
        b_spec = spec((tn, tk), lambda i, j, k: (j, k))
        ca, cb = 1, 1
    else:
        a_spec = spec((tk, tm), lambda i, j, k: (k, i))
        b_spec = spec((tk, tn), lambda i, j, k: (k, j))
        ca, cb = 0, 0
    in_specs, operands = [a_spec, b_spec], [a, b]
    if add is not None:
        in_specs.append(spec((tm, tn), lambda i, j, k: (i, j)))
        operands.append(add)

    def finish(r, refs, o_ref):
        if add is not None:
            r = r + refs[2][...]
        o_ref[...] = r.astype(o_ref.dtype)

    def body_single(*refs):
        finish(_dot(refs[0][...], refs[1][...], ca, cb), refs, refs[-1])

    def body_multi(*refs):
        o_ref, acc_ref = refs[-2], refs[-1]
        k = pl.program_id(2)
        d = _dot(refs[0][...], refs[1][...], ca, cb)

        @pl.when(k == 0)
        def _():
            acc_ref[...] = d

        @pl.when(jnp.logical_and(k > 0, k < nk - 1))
        def _():
            acc_ref[...] += d

        @pl.when(k == nk - 1)
        def _():
            finish(acc_ref[...] + d, refs, o_ref)

    grid = (N // tn, M // tm, nk) if swap else (M // tm, N // tn, nk)
    return pl.pallas_call(
        body_single if nk == 1 else body_multi, grid=grid, in_specs=in_specs,
        out_specs=spec((tm, tn), lambda i, j, k: (i, j)),
        out_shape=jax.ShapeDtypeStruct((M, N), out_dtype),
        scratch_shapes=[] if nk == 1 else [pltpu.VMEM((tm, tn), F32)], name=name,
        compiler_params=_cparams(("parallel", "parallel", "arbitrary")))(*operands)


def _seg_copies(segs, c):
    out = []
    for lo, hi, dst in segs:
        n = lo
        while n < hi:
            p = n // c
            w = min(hi, (p + 1) * c) - n
            out.append((p, n - p * c, w, dst + n - lo))
            n += w
    return out


def _col_assemble(name, blocks, copies, zeros, n_out):
    _, R, c = blocks.shape
    tm = R if R <= 128 else 128

    def body(b_ref, o_ref):
        for p, s, w, d in copies:
            o_ref[:, d:d + w] = b_ref[p, :, s:s + w]
        for lo, hi in zeros:
            o_ref[:, lo:hi] = jnp.zeros((tm, hi - lo), o_ref.dtype)

    return pl.pallas_call(
        body, grid=(R // tm,), in_specs=[pl.BlockSpec((N_DEV, tm, c), lambda i: (0, i, 0))],
        out_specs=pl.BlockSpec((tm, n_out), lambda i: (i, 0)),
        out_shape=jax.ShapeDtypeStruct((R, n_out), blocks.dtype), name=name, compiler_params=_cparams(("parallel",)))(blocks)


def _col_split(name, full, copies, c, dtype):
    R, n = full.shape
    tm = R if R <= 128 else 128

    def body(f_ref, o_ref):
        for p, s, w, d in copies:
            o_ref[p, :, s:s + w] = f_ref[:, d:d + w].astype(dtype)

    return pl.pallas_call(
        body, grid=(R // tm,), in_specs=[pl.BlockSpec((tm, n), lambda i: (i, 0))],
        out_specs=pl.BlockSpec((N_DEV, tm, c), lambda i: (0, i, 0)),
        out_shape=jax.ShapeDtypeStruct((N_DEV, R, c), dtype), name=name, compiler_params=_cparams(("parallel",)))(full)


def _rmsnorm_fn(x, g):
    x = x.astype(F32)
    return (x * lax.rsqrt(jnp.mean(x * x, axis=-1, keepdims=True) + EPS) * g,)


def _gate_merge_fn(g0, g1, g2, ya, yb, yc, b0, b1, b2):
    return (jax.nn.sigmoid(g0 + b0) * ya + jax.nn.sigmoid(g1 + b1) * yb + jax.nn.sigmoid(g2 + b2) * yc,)


def _gated_norm_fn(y, z, w):
    t = y * _silu(z)
    return (t * lax.rsqrt(jnp.mean(t * t, axis=-1, keepdims=True) + EPS) * w,)


def _att_merge_fn(o0, o1, o2, l0, l1, l2):
    m = lax.stop_gradient(jnp.maximum(jnp.maximum(l0, l1), l2))
    e0, e1, e2 = jnp.exp(l0 - m), jnp.exp(l1 - m), jnp.exp(l2 - m)
    return ((e0 * o0 + e1 * o1 + e2 * o2) / (e0 + e1 + e2),)


def _loss_fn(x, tgt, g):
    (y,) = _rmsnorm_fn(x, g)
    err = y - tgt
    return 0.5 * jnp.sum(jnp.mean(err * err, axis=-1, keepdims=True), axis=0, keepdims=True)


def _pool_fn(x, wg, scale):
    g = pl.program_id(0)
    s2 = x + _shift(x, 1)
    s4 = s2 + _shift(s2, 2)
    s8 = s4 + _shift(s4, 4)
    s16 = s8 + _shift(s8, 8)
    win = ((g == 0).astype(F32) * s2 + (g == 1).astype(F32) * s4 + (g == 2).astype(F32) * s8
           + (g == 3).astype(F32) * s16)
    t = lax.broadcasted_iota(jnp.int32, (x.shape[0], 1), 0) + 1
    cnt = jnp.minimum(t, jnp.left_shift(2, g)).astype(F32)
    d = win / cnt - x
    return (_mm(d, wg.reshape(256, 256)) * scale,)


def _dwconv(x, taps, b):
    k = len(taps)
    y = taps[k - 1] * x + b
    for i in range(k - 1):
        y = y + taps[i] * _shift(x, k - 1 - i)
    return y


def _ssd_conv_fn(x, w0, w1, w2, w3, b):
    return (_silu(_dwconv(x, (w0, w1, w2, w3), b)),)


def _ffn_act_fn(xa, xv, a0, a1, a2, ab, v0, v1, v2, vb):
    xa, xv = xa.astype(F32), xv.astype(F32)
    return (_silu(_dwconv(xa, (a0, a1, a2), ab)) * _dwconv(xv, (v0, v1, v2), vb),)


@jax.custom_vjp
def _halves(x):
    return x[:ATT_BLOCK], x[ATT_BLOCK:]


_halves.defvjp(lambda x: (_halves(x), None), lambda _, g: (jnp.concatenate([g[0], g[1]], axis=0),))


def _att_block(q, kp, kc, vp, vc, bpa, bpb, bca, bcb, prev_ok):
    n = ATT_BLOCK
    lane = lax.broadcasted_iota(jnp.int32, (1, 2 * HEAD_DIM), 1)
    ma = (lane < HEAD_DIM).astype(F32)
    mb = 1.0 - ma
    q = q.astype(F32) * (1.0 / math.sqrt(HEAD_DIM))
    q2 = jnp.concatenate([q * ma, q * mb], axis=0)
    qi = lax.broadcasted_iota(jnp.int32, (2 * n, n), 0) & (n - 1)
    kj = lax.broadcasted_iota(jnp.int32, (2 * n, n), 1)
    sp = jnp.where(jnp.logical_and(kj >= qi, prev_ok), _mm_nt(q2, kp) + jnp.concatenate([bpa, bpb], axis=0), NEG)
    sc = jnp.where(kj <= qi, _mm_nt(q2, kc) + jnp.concatenate([bca, bcb], axis=0), NEG)
    m = lax.stop_gradient(jnp.maximum(jnp.max(sp, axis=1, keepdims=True), jnp.max(sc, axis=1, keepdims=True)))
    pp = jnp.exp(sp - m)
    pc = jnp.exp(sc - m)
    l = jnp.sum(pp, axis=1, keepdims=True) + jnp.sum(pc, axis=1, keepdims=True)
    oa, ob = _halves((_mm(pp, vp) + _mm(pc, vc)) / l)
    la, lb = _halves((m + jnp.log(l)) * jnp.ones((1, 2 * HEAD_DIM), F32))
    return oa * ma + ob * mb, la * ma + lb * mb


def _att_slab(dil):
    nbk = 8 if dil == 1 else 1
    t = ATT_BLOCK * dil * nbk
    return nbk, t, S // t


def _att_in_specs(gi, t):
    def spec(which, prev):
        col = OFF_QKV // 128 + gi * 9 + which

        def index(p, j, col=col, prev=prev):
            jj = jnp.minimum(j, S // t - 1)
            return (jnp.maximum(jj - 1, 0) if prev else jj, col + 3 * p)
        return pl.BlockSpec((t, 2 * HEAD_DIM), index)
    return [spec(0, False), spec(1, False), spec(1, True), spec(2, False), spec(2, True)]


def _bias_specs():
    return [pl.BlockSpec((None, ATT_BLOCK, ATT_BLOCK), functools.partial(lambda p, j, hh: (2 * p + hh, 0, 0), hh=hh))
            for hh in (0, 1)]


def _att_units(dil, nbk, body):
    def per_residue(r, carry):
        for b in range(nbk):
            rows = pl.ds(b * ATT_BLOCK * dil + r, ATT_BLOCK, stride=dil)
            prev = pl.ds(((b - 1) % nbk) * ATT_BLOCK * dil + r, ATT_BLOCK, stride=dil)
            body(b, rows, prev, b > 0)
        return carry
    if dil == 1:
        per_residue(0, 0)
    else:
        lax.fori_loop(0, dil, per_residue, 0, unroll=min(dil, 8))


def _att_fwd(name, proj, gi, bias_p, bias_c):
    dil = ATT_GROUPS[gi][1]
    nbk, t, ns = _att_slab(dil)
    bsp = _bias_specs()
    out_spec = pl.BlockSpec((t, 2 * HEAD_DIM), lambda p, j: (j, p))

    def body(q_ref, kc_ref, kp_ref, vc_ref, vp_ref, bpa, bpb, bca, bcb, o_ref, l_ref):
        first = pl.program_id(1) == 0
        biases = (bpa[...], bpb[...], bca[...], bcb[...])

        def unit(b, rows, prev, in_slab):
            kp = kc_ref[prev, :] if in_slab else kp_ref[prev, :]
            vp = vc_ref[prev, :] if in_slab else vp_ref[prev, :]
            prev_ok = True if in_slab else jnp.logical_not(first)
            o, lse = _att_block(q_ref[rows, :], kp, kc_ref[rows, :], vp, vc_ref[rows, :], *biases, prev_ok)
            o_ref[rows, :] = o
            l_ref[rows, :] = lse

        _att_units(dil, nbk, unit)

    shp = jax.ShapeDtypeStruct((S, ATT_GW), F32)
    return pl.pallas_call(
        body, grid=(3, ns), in_specs=_att_in_specs(gi, t) + [bsp[0], bsp[1], bsp[0], bsp[1]],
        out_specs=[out_spec, out_spec], out_shape=[shp, shp], name=name,
        compiler_params=_cparams(("arbitrary",) * 2))(proj, proj, proj, proj, proj, bias_p, bias_p, bias_c, bias_c)


def _att_bwd(name, proj, gi, bias_p, bias_c, do, dl, dproj):
    dil = ATT_GROUPS[gi][1]
    nbk, t, ns = _att_slab(dil)
    bsp = _bias_specs()
    blk = (t, 2 * HEAD_DIM)
    cur = pl.BlockSpec(blk, lambda p, j: (jnp.minimum(j, ns - 1), p))
    done = pl.BlockSpec((t, QKV_W), lambda p, j: (jnp.maximum(j - 1, 0), OFF_QKV // QKV_W + gi * 3 + p))
    gsp = pl.BlockSpec((None, ATT_BLOCK, ATT_BLOCK), lambda p, j: (p, 0, 0))

    def body(q_ref, kc_ref, kp_ref, vc_ref, vp_ref, bpa, bpb, bca, bcb, do_ref, dl_ref, _,
             dqkv_ref, gpa, gpb, gca, gcb, accq, acck, accv):
        j = pl.program_id(1)
        mine, other = acck.at[j % 2], acck.at[1 - j % 2]
        mine_v, other_v = accv.at[j % 2], accv.at[1 - j % 2]
        dq_ref, other_q = accq.at[j % 2], accq.at[1 - j % 2]

        @pl.when(j == 0)
        def _():
            for g in (gpa, gpb, gca, gcb):
                g[...] = jnp.zeros_like(g)
            other[...] = jnp.zeros_like(other)
            other_v[...] = jnp.zeros_like(other_v)
            other_q[...] = jnp.zeros_like(other_q)

        @pl.when(j < ns)
        def _():
            mine[...] = jnp.zeros_like(mine)
            mine_v[...] = jnp.zeros_like(mine_v)
            biases = (bpa[...], bpb[...], bca[...], bcb[...])

            def unit(b, rows, prev, in_slab):
                kp = kc_ref[prev, :] if in_slab else kp_ref[prev, :]
                vp = vc_ref[prev, :] if in_slab else vp_ref[prev, :]
                prev_ok = True if in_slab else j > 0
                prim = (q_ref[rows, :], kp, kc_ref[rows, :], vp, vc_ref[rows, :]) + biases
                _, vjp = jax.vjp(lambda *a: _att_block(*a, prev_ok), *prim)
                dq, dkp, dkc, dvp, dvc, dpa, dpb, dca, dcb = vjp((do_ref[rows, :], dl_ref[rows, :]))
                dq_ref[rows, :] = dq
                mine[rows, :] += dkc
                mine_v[rows, :] += dvc
                tgt, tgt_v = (mine, mine_v) if in_slab else (other, other_v)
                tgt[prev, :] += dkp
                tgt_v[prev, :] += dvp
                gpa[...] += dpa
                gpb[...] += dpb
                gca[...] += dca
                gcb[...] += dcb

            _att_units(dil, nbk, unit)

        w = 2 * HEAD_DIM
        dqkv_ref[:, 0:w] = other_q[...].astype(BF16)
        dqkv_ref[:, w:2 * w] = other[...].astype(BF16)
        dqkv_ref[:, 2 * w:3 * w] = other_v[...].astype(BF16)

    gshp = jax.ShapeDtypeStruct((3, ATT_BLOCK, ATT_BLOCK), F32)
    res = pl.pallas_call(
        body, grid=(3, ns + 1),
        in_specs=_att_in_specs(gi, t) + [bsp[0], bsp[1], bsp[0], bsp[1], cur, cur, pl.BlockSpec(memory_space=pl.ANY)],
        out_specs=[done, gsp, gsp, gsp, gsp],
        out_shape=[jax.ShapeDtypeStruct((S, NP), BF16), gshp, gshp, gshp, gshp],
        input_output_aliases={11: 0},
        scratch_shapes=[pltpu.VMEM((2,) + blk, F32)] * 3, name=name,
        compiler_params=_cparams(("arbitrary",) * 2))(proj, proj, proj, proj, proj, bias_p, bias_p, bias_c, bias_c, do, dl,
                                                      dproj)
    dproj, gpa, gpb, gca, gcb = res
    heads = lambda a, b: jnp.stack([a, b], axis=1).reshape(6, ATT_BLOCK, ATT_BLOCK)
    return dproj, heads(gpa, gpb), heads(gca, gcb)


N_PAIR = SSD_HEADS // 2


def _ssd_chunk(xs, bs, cs_in, dt_raw, hs, a_row, dtb_row, ds):
    lane = lax.broadcasted_iota(jnp.int32, (1, 128), 1)
    row = lax.broadcasted_iota(jnp.int32, (128, 1), 0)
    tril = _tri(True)
    dt = _softplus(dt_raw + dtb_row)
    acs = _cumsum_rows(dt * a_row)
    acs_t = acs.T
    gmat = [_mm_nt(cs_in[g], bs[g]) for g in range(2)]
    lo = lane < HEAD_DIM
    lo_r = row < HEAD_DIM
    last = (row == SSD_CHUNK - 1).astype(F32)
    ys, hn = [], []
    for p in range(N_PAIR):
        g = p // (N_PAIR // 2)
        col, dtc, mm, clast = [], [], [], []
        for hh in range(2):
            h = 2 * p + hh
            oh = (lane == h).astype(F32)
            c_col = jnp.sum(acs * oh, axis=1, keepdims=True)
            c_row = jnp.sum(acs_t * (row == h).astype(F32), axis=0, keepdims=True)
            col.append(c_col)
            dtc.append(jnp.sum(dt * oh, axis=1, keepdims=True))
            clast.append(jnp.sum(c_col * last, axis=0, keepdims=True))
            mm.append(gmat[g] * jnp.exp(jnp.where(tril, c_col - c_row, NEG)))
        x = xs[p]
        xd = x * jnp.where(lo, dtc[0], dtc[1])
        y = jnp.where(lo, _mm(mm[0], xd), _mm(mm[1], xd))
        y = y + jnp.where(lo, jnp.exp(col[0]), jnp.exp(col[1])) * _mm_nt(cs_in[g], hs[p])
        ys.append(y + ds[p] * x)
        dec = jnp.where(lo, jnp.exp(clast[0] - col[0]), jnp.exp(clast[1] - col[1]))
        hn.append(hs[p] * jnp.where(lo_r, jnp.exp(clast[0]), jnp.exp(clast[1])) + _mm_tn(xd * dec, bs[g]))
    return tuple(ys), tuple(hn)


def _ssd_load(xbc_ref, dt_ref, a_ref, dtb_ref, d_ref):
    xs = tuple(xbc_ref[:, 128 * p:128 * (p + 1)] for p in range(N_PAIR))
    bs = tuple(xbc_ref[:, D + 128 * g:D + 128 * (g + 1)] for g in range(2))
    cs = tuple(xbc_ref[:, D + 256 + 128 * g:D + 256 + 128 * (g + 1)] for g in range(2))
    ds = tuple(d_ref[:, 128 * p:128 * (p + 1)] for p in range(N_PAIR))
    return xs, bs, cs, dt_ref[...], a_ref[...], dtb_ref[...], ds


def _ssd_fwd(name, xbc_c, proj, a_row, dtb_row, d_exp):
    nc = S // SSD_CHUNK
    prow = lambda n: pl.BlockSpec((1, n), lambda c: (0, 0))

    def body(xbc_ref, dt_ref, a_ref, dtb_ref, d_ref, y_ref, st_ref, h_ref):
        @pl.when(pl.program_id(0) == 0)
        def _():
            h_ref[...] = jnp.zeros_like(h_ref)

        xs, bs, cs, dt_raw, a, dtb, ds = _ssd_load(xbc_ref, dt_ref, a_ref, dtb_ref, d_ref)
        hs = tuple(h_ref[p] for p in range(N_PAIR))
        ys, hn = _ssd_chunk(xs, bs, cs, dt_raw, hs, a, dtb, ds)
        for p in range(N_PAIR):
            y_ref[:, 128 * p:128 * (p + 1)] = ys[p]
            st_ref[p] = hs[p]
            h_ref[p] = hn[p]

    return pl.pallas_call(
        body, grid=(nc,),
        in_specs=[pl.BlockSpec((SSD_CHUNK, SSD_XBC), lambda c: (c, 0)),
                  pl.BlockSpec((SSD_CHUNK, DT_PAD), lambda c: (c, OFF_DT // DT_PAD)),
                  prow(128), prow(128), prow(D)],
        out_specs=[pl.BlockSpec((SSD_CHUNK, D), lambda c: (c, 0)),
                   pl.BlockSpec((None, N_PAIR, 128, 128), lambda c: (c, 0, 0, 0))],
        out_shape=[jax.ShapeDtypeStruct((S, D), F32), jax.ShapeDtypeStruct((nc, N_PAIR, 128, 128), F32)],
        scratch_shapes=[pltpu.VMEM((N_PAIR, 128, 128), F32)], name=name,
        compiler_params=_cparams(("arbitrary",)))(xbc_c, proj, a_row, dtb_row, d_exp)


def _ssd_bwd(name, xbc_c, proj, states, dy, a_row, dtb_row, d_exp, dproj):
    nc = S // SSD_CHUNK
    prow = lambda n: pl.BlockSpec((1, n), lambda i: (0, 0))
    rc = lambda i: nc - 1 - i

    def body(xbc_ref, dt_ref, st_ref, dy_ref, a_ref, dtb_ref, d_ref, _, dxbc_ref, ddt_ref, da_ref, ddtb_ref, dd_ref, e_ref):
        i = pl.program_id(0)

        @pl.when(i == 0)
        def _():
            e_ref[...] = jnp.zeros_like(e_ref)
            da_ref[...] = jnp.zeros_like(da_ref)
            ddtb_ref[...] = jnp.zeros_like(ddtb_ref)
            dd_ref[...] = jnp.zeros_like(dd_ref)

        xs, bs, cs, dt_raw, a, dtb, ds = _ssd_load(xbc_ref, dt_ref, a_ref, dtb_ref, d_ref)
        hs = tuple(st_ref[p] for p in range(N_PAIR))
        _, vjp = jax.vjp(_ssd_chunk, xs, bs, cs, dt_raw, hs, a, dtb, ds)
        dys = tuple(dy_ref[:, 128 * p:128 * (p + 1)] for p in range(N_PAIR))
        es = tuple(e_ref[p] for p in range(N_PAIR))
        dxs, dbs, dcs, ddt, dhs, da, ddtb, dds = vjp((dys, es))
        for p in range(N_PAIR):
            dxbc_ref[:, 128 * p:128 * (p + 1)] = dxs[p]
            e_ref[p] = dhs[p]
            dd_ref[:, 128 * p:128 * (p + 1)] += dds[p]
        for g in range(2):
            dxbc_ref[:, D + 128 * g:D + 128 * (g + 1)] = dbs[g]
            dxbc_ref[:, D + 256 + 128 * g:D + 256 + 128 * (g + 1)] = dcs[g]
        ddt_ref[:, :DT_PAD] = ddt.astype(BF16)
        ddt_ref[:, DT_PAD:] = jnp.zeros((SSD_CHUNK, OFF_QKV - OFF_DT - DT_PAD), BF16)
        da_ref[...] += da
        ddtb_ref[...] += ddtb

    dt_w = OFF_QKV - OFF_DT
    return pl.pallas_call(
        body, grid=(nc,),
        in_specs=[pl.BlockSpec((SSD_CHUNK, SSD_XBC), lambda i: (rc(i), 0)),
                  pl.BlockSpec((SSD_CHUNK, DT_PAD), lambda i: (rc(i), OFF_DT // DT_PAD)),
                  pl.BlockSpec((None, N_PAIR, 128, 128), lambda i: (rc(i), 0, 0, 0)),
                  pl.BlockSpec((SSD_CHUNK, D), lambda i: (rc(i), 0)),
                  prow(128), prow(128), prow(D), pl.BlockSpec(memory_space=pl.ANY)],
        out_specs=[pl.BlockSpec((SSD_CHUNK, SSD_XBC), lambda i: (rc(i), 0)),
                   pl.BlockSpec((SSD_CHUNK, dt_w), lambda i: (rc(i), OFF_DT // dt_w)),
                   prow(128), prow(128), prow(D)],
        out_shape=[jax.ShapeDtypeStruct((S, SSD_XBC), F32), jax.ShapeDtypeStruct((S, NP), BF16),
                   jax.ShapeDtypeStruct((1, 128), F32), jax.ShapeDtypeStruct((1, 128), F32),
                   jax.ShapeDtypeStruct((1, D), F32)],
        input_output_aliases={7: 1},
        scratch_shapes=[pltpu.VMEM((N_PAIR, 128, 128), F32)], name=name,
        compiler_params=_cparams(("arbitrary",)))(xbc_c, proj, states, dy, a_row, dtb_row, d_exp, dproj)


def _exchange(name, arrays, scatter):
    n = len(arrays)
    flips = [(dx, dy, dc) for dx in (0, 1) for dy in (0, 1) for dc in (0, 1) if dx or dy or dc]

    def body(*refs):
        ins, outs = refs[:n], refs[n:2 * n]
        send_sems, recv_sems, loc_sems = refs[2 * n:]
        x, y, c = lax.axis_index("x"), lax.axis_index("y"), lax.axis_index("c")
        me = 4 * x + 2 * y + c
        peers = []
        for dx, dy, dc in flips:
            px, py, pc = (1 - x if dx else x), (1 - y if dy else y), (1 - c if dc else c)
            peers.append(((px, py, pc), 4 * px + 2 * py + pc))

        def remote(k, j, landed_from):
            dev, pid = peers[j]
            src = ins[k].at[pid] if scatter else ins[k]
            return pltpu.make_async_remote_copy(
                src_ref=src, dst_ref=outs[k].at[landed_from], send_sem=send_sems.at[k, j], recv_sem=recv_sems.at[k, j],
                device_id=dev, device_id_type=pl.DeviceIdType.MESH)

        local = [pltpu.make_async_copy(ins[k].at[me] if scatter else ins[k], outs[k].at[me], loc_sems.at[k])
                 for k in range(n)]
        for cp in local:
            cp.start()
        for k in range(n):
            for j in range(len(flips)):
                remote(k, j, me).start()
        for cp in local:
            cp.wait()
        for k in range(n):
            for j in range(len(flips)):
                remote(k, j, me).wait_send()
                remote(k, j, peers[j][1]).wait_recv()

    hbm = pl.BlockSpec(memory_space=pltpu.HBM)
    out_shape = [jax.ShapeDtypeStruct(a.shape if scatter else (N_DEV,) + a.shape, a.dtype) for a in arrays]
    res = pl.pallas_call(
        body, in_specs=[hbm] * n, out_specs=[hbm] * n, out_shape=out_shape, name=name,
        scratch_shapes=[pltpu.SemaphoreType.DMA((n, len(flips))), pltpu.SemaphoreType.DMA((n, len(flips))),
                        pltpu.SemaphoreType.DMA((n,))])(*arrays)
    return list(res)


def _gather_chip_once(name, block):
    def body(x_ref, out_ref, send_sems, recv_sems, loc_sem):
        x, y, c = lax.axis_index("x"), lax.axis_index("y"), lax.axis_index("c")
        me, sibling = (x, y, c), (x, y, 1 - c)
        chips = [(1 - x, y), (x, 1 - y), (1 - x, 1 - y)]

        def slot(px, py, pc):
            return out_ref.at[4 * px + 2 * py + pc]

        def copy(k, blk, to, src=None):
            return pltpu.make_async_remote_copy(
                src_ref=slot(*blk) if src is None else src, dst_ref=slot(*blk), send_sem=send_sems.at[k],
                recv_sem=recv_sems.at[k], device_id=to, device_id_type=pl.DeviceIdType.MESH)

        mine = pltpu.make_async_copy(x_ref, slot(*me), loc_sem)
        mine.start()
        first = [copy(0, me, sibling, src=x_ref)] + [copy(1 + j, me, (*chip, c), src=x_ref) for j, chip in enumerate(chips)]
        for cp in first:
            cp.start()
        passed = [copy(4 + j, (*chip, c), sibling) for j, chip in enumerate(chips)]
        for j, chip in enumerate(chips):
            copy(1 + j, (*chip, c), me).wait_recv()
            passed[j].start()
        copy(0, sibling, me).wait_recv()
        for j, chip in enumerate(chips):
            copy(4 + j, (*chip, 1 - c), me).wait_recv()
        for cp in first + passed:
            cp.wait_send()
        mine.wait()

    hbm = pl.BlockSpec(memory_space=pltpu.HBM)
    return pl.pallas_call(
        body, in_specs=[hbm], out_specs=hbm, out_shape=jax.ShapeDtypeStruct((N_DEV,) + block.shape, block.dtype), name=name,
        scratch_shapes=[pltpu.SemaphoreType.DMA((N_DEV - 1,)), pltpu.SemaphoreType.DMA((N_DEV - 1,)),
                        pltpu.SemaphoreType.DMA(())])(block)


def _peer_copies(ins, lands, send_sems, recv_sems, loc_sems, scatter):
    n = len(ins)
    flips = [(dx, dy, dc) for dx in (0, 1) for dy in (0, 1) for dc in (0, 1) if dx or dy or dc]
    x, y, c = lax.axis_index("x"), lax.axis_index("y"), lax.axis_index("c")
    me = 4 * x + 2 * y + c
    peers = []
    for dx, dy, dc in flips:
        px, py, pc = (1 - x if dx else x), (1 - y if dy else y), (1 - c if dc else c)
        peers.append(((px, py, pc), 4 * px + 2 * py + pc))

    def remote(k, j, slot):
        dev, pid = peers[j]
        return pltpu.make_async_remote_copy(
            src_ref=ins[k].at[pid] if scatter else ins[k], dst_ref=lands[k].at[slot],
            send_sem=send_sems.at[k * N_FLIP + j], recv_sem=recv_sems.at[k * N_FLIP + j],
            device_id=dev, device_id_type=pl.DeviceIdType.MESH)

    local = [pltpu.make_async_copy(ins[k].at[me] if scatter else ins[k], lands[k].at[me], loc_sems.at[k])
             for k in range(n)]
    pairs = [(k, j) for k in range(n) for j in range(len(flips))]
    sent = lambda k, j: remote(k, j, me)
    landed = lambda k, j: remote(k, j, peers[j][1])
    return local, pairs, sent, landed


_HBM = pl.BlockSpec(memory_space=pltpu.HBM)
_SEM = pl.BlockSpec(memory_space=pltpu.SEMAPHORE)
N_FLIP = N_DEV - 1


def _exchange_start(name, arrays, scatter, after):
    n = len(arrays)
    arrays = [pltpu.with_memory_space_constraint(a, pltpu.HBM) for a in arrays]
    lands = [pltpu.with_memory_space_constraint(
        lax.empty(a.shape if scatter else (N_DEV,) + a.shape, a.dtype), pltpu.HBM) for a in arrays]

    def body(*refs):
        ins, lnd = refs[:n], refs[n:2 * n]
        send_sems, recv_sems, loc_sems = refs[2 * n + 1:2 * n + 4]
        token = refs[-1]
        local, pairs, sent, _ = _peer_copies(ins, lnd, send_sems, recv_sems, loc_sems, scatter)
        for cp in local:
            cp.start()
        for k, j in pairs:
            sent(k, j).start()
        token[...] = jnp.zeros_like(token)

    res = pl.pallas_call(
        body, name=name,
        in_specs=[_HBM] * (2 * n) + [pl.BlockSpec(memory_space=pl.ANY)],
        out_specs=[_SEM, _SEM, _SEM] + [_HBM] * (2 * n) + [pl.BlockSpec(memory_space=pltpu.VMEM)],
        out_shape=[pltpu.SemaphoreType.DMA((n * N_FLIP,)), pltpu.SemaphoreType.DMA((n * N_FLIP,)), pltpu.SemaphoreType.DMA((n,))]
        + [pltpu.HBM(a.shape, a.dtype) for a in arrays] + [pltpu.HBM(a.shape, a.dtype) for a in lands]
        + [jax.ShapeDtypeStruct((8, 128), F32)],
        input_output_aliases={k: 3 + k for k in range(2 * n)},
        compiler_params=pltpu.CompilerParams(has_side_effects=pltpu.SideEffectType.DATAFLOW_SIDE_EFFECTING),
    )(*arrays, *lands, after)
    return (res[:3], res[3:3 + n], res[3 + n:3 + 2 * n], scatter), res[-1]


def _exchange_wait(name, state, after):
    sems, ins_thru, lands_thru, scatter = state
    n = len(ins_thru)

    def body(*refs):
        ins, lnd = refs[:n], refs[n:2 * n]
        send_sems, recv_sems, loc_sems = refs[2 * n:2 * n + 3]
        local, pairs, sent, landed = _peer_copies(ins, lnd, send_sems, recv_sems, loc_sems, scatter)
        for cp in local:
            cp.wait()
        for k, j in pairs:
            sent(k, j).wait_send()
            landed(k, j).wait_recv()

    res = pl.pallas_call(
        body, name=name,
        in_specs=[_HBM] * (2 * n) + [_SEM, _SEM, _SEM] + [pl.BlockSpec(memory_space=pl.ANY)],
        out_specs=[_HBM] * (2 * n),
        out_shape=[pltpu.HBM(a.shape, a.dtype) for a in ins_thru] + [pltpu.HBM(a.shape, a.dtype) for a in lands_thru],
        input_output_aliases={k: k for k in range(2 * n)},
        compiler_params=pltpu.CompilerParams(has_side_effects=pltpu.SideEffectType.DATAFLOW_SIDE_EFFECTING),
    )(*ins_thru, *lands_thru, *sems, after)
    return list(res[n:])


def _adamw_fn(*vals):
    slots, (w, m, v) = vals[:N_DEV], vals[N_DEV:]
    g = slots[0].astype(F32)
    for s in slots[1:]:
        g = g + s.astype(F32)
    m2 = ADAM_B1 * m + (1.0 - ADAM_B1) * g
    v2 = ADAM_B2 * v + (1.0 - ADAM_B2) * (g * g)
    m_hat = m2 / (1.0 - ADAM_B1 ** ADAM_STEP)
    v_hat = v2 / (1.0 - ADAM_B2 ** ADAM_STEP)
    delta = -ADAM_LR * (m_hat / (jnp.sqrt(v_hat) + ADAM_EPS) + ADAM_WD * w)
    return (g, delta, m2, v2), ()


def _adamw(name, slots, w, m, v, first_row=0, prev=None):
    R, C = slots.shape[1:]
    tm = R if R <= 128 else _pick(R, 128 if C > D else 256, 8)
    rows = ([_rows(slots, C, lead=s) for s in range(N_DEV)]
            + [_rows(a, C, roff=first_row // tm) for a in (w, m, v)])
    return _tiled(name, _adamw_fn, (1, R // tm), tm, rows, [], [(w.shape[0], C, F32)] * 4,
                  out_roff=first_row // tm, prev_outs=prev)


def _bucket_onehots():
    out = []
    qi = jnp.arange(ATT_BLOCK)[:, None]
    kj = jnp.arange(ATT_BLOCK)[None, :]
    max_exact = REL_BUCKETS // 2
    for _, dil in ATT_GROUPS:
        parts = []
        for rel in (qi + ATT_BLOCK - kj, qi - kj):
            dist = jnp.clip(rel, 0, None) * dil
            nf = jnp.maximum(dist, 1).astype(F32)
            large = max_exact + (jnp.log(nf / max_exact) / math.log(REL_MAX_DISTANCE / max_exact)
                                 * (REL_BUCKETS - max_exact)).astype(jnp.int32)
            large = jnp.minimum(large, REL_BUCKETS - 1)
            bucket = jnp.where(dist < max_exact, dist, large)
            parts.append((bucket[:, :, None] == jnp.arange(REL_BUCKETS)[None, None, :]).astype(F32))
        out.append(jnp.stack(parts))
    return out


SHARDED = ("w_in", "w_a", "pool_w", "w_b", "ssd_conv_w", "w_c", "w_o", "ffn_w_up", "ffn_conv_w", "ffn_w_down")
MATMUL_WEIGHTS = ("w_in", "w_a", "pool_w", "w_b", "w_c", "w_o", "ffn_w_up", "ffn_w_down")
ROW_SHARDED = ("w_b", "w_c", "w_o", "ffn_w_down")
W_IN_SEGS = tuple(
    (which * ATT_W + unit * 128, which * ATT_W + (unit + 1) * 128, OFF_QKV + unit * QKV_W + which * 128)
    for unit in range(9) for which in range(3)
) + ((3456, 4480, OFF_POOL), (4480, 5504, OFF_Z), (5504, 7040, OFF_XBC), (7040, 7056, OFF_DT), (7056, IN_WIDTH, OFF_GATE))
FFN_SEGS = tuple((h * D_FF + j * 128, h * D_FF + (j + 1) * 128, j * 256 + h * 128)
                 for j in range(D_FF // 128) for h in range(2))
COL_SHARDED = {
    "w_in": (IN_WIDTH // N_DEV, W_IN_SEGS, ((OFF_DT + SSD_HEADS, OFF_QKV),), NP),
    "w_a": (D // N_DEV, ((0, D, 0),), (), D),
    "ffn_w_up": (2 * D_FF // N_DEV, FFN_SEGS, (), 2 * D_FF),
    "ssd_conv_w": (SSD_XBC // N_DEV, ((0, SSD_XBC, 0),), (), SSD_XBC),
    "ffn_conv_w": (2 * D_FF // N_DEV, FFN_SEGS, (), 2 * D_FF),
}
REPLICATED = ("rel_bias", "ln1_g", "b_gate", "pool_scale", "ssd_conv_b", "ssd_dt_bias", "ssd_a_log", "ssd_d",
              "ssd_norm_w", "ln2_g", "ffn_conv_b", "final_g")
WEIGHTS = ("rel_bias", "ln1_g", "w_in", "b_gate", "w_a", "pool_w", "pool_scale", "w_b", "ssd_conv_w", "ssd_conv_b",
           "ssd_dt_bias", "ssd_a_log", "ssd_d", "ssd_norm_w", "w_c", "w_o", "ln2_g", "ffn_w_up", "ffn_conv_w",
           "ffn_conv_b", "ffn_w_down", "final_g")


def _local_weight(name, n, blocks):
    if n in COL_SHARDED:
        c, segs, zeros, width = COL_SHARDED[n]
        return _col_assemble(name, blocks, _seg_copies(segs, c), zeros, width)
    if n in ROW_SHARDED:
        return blocks.reshape(-1, blocks.shape[-1])
    return blocks


def _device_blocks(name, n, g):
    if n in COL_SHARDED:
        c, segs, _, _ = COL_SHARDED[n]
        return _col_split(name, g, _seg_copies(segs, c), c, BF16)
    if n in ROW_SHARDED:
        return g.reshape(N_DEV, g.shape[0] // N_DEV, g.shape[1]).astype(BF16)
    return g.astype(BF16)


def _row(v, n=None):
    v = v.reshape(1, -1)
    if n is not None and v.shape[1] < n:
        v = jnp.pad(v, ((0, 0), (0, n - v.shape[1])))
    return v


RT = 512


def _row_call(name, fn, cw, ncol, rows, params, outs, accs=(), into=None):
    return _tiled(name, fn, (ncol, S // RT), RT, rows, params, [(S, cw, dt) for dt in outs], accs, into=into)


def _col_call(name, fn, tc, ncol, rows, params, outs, accs=(), into=None):
    return _tiled(name, fn, (ncol, 1), S, rows, params, [(S, tc, dt) for dt in outs], accs, into=into)


def _fwd_only(fn):
    return lambda *a: (fn(*a), ())


def _layer_fwd(i, x, W, P, bias_tabs, late=None):
    sv = {"x": x}
    (u,) = _row_call(f"ln1_f{i}", _fwd_only(_rmsnorm_fn), D, 1, [_rows(x, D)], [_p_row(P["ln1_g"], D)], [BF16])
    proj = _matmul(f"inproj_f{i}", u, W["w_in"], "nn")
    sv["u"], sv["proj"] = u, proj

    os_, ls_ = [], []
    for gi in range(len(ATT_GROUPS)):
        o, lse = _att_fwd(f"att_f{i}_{gi}", proj, gi, bias_tabs[gi][0], bias_tabs[gi][1])
        os_.append(o)
        ls_.append(lse)
    sv["att_o"], sv["att_l"] = os_, ls_
    (att,) = _row_call(f"attmerge_f{i}", _fwd_only(_att_merge_fn), ATT_GW, 1,
                       [_rows(t, ATT_GW) for t in os_ + ls_], [], [BF16])
    if late is not None:
        W2, P2 = late(att)
        W.update(W2)
        P.update(P2)
    y_a = _matmul(f"wa_f{i}", att, W["w_a"], "nn", out_dtype=BF16)
    sv["att"], sv["y_a"] = att, y_a

    pool_params = [(W["pool_w"], (N_DEV, None, 32, 256), lambda j, i_: (0, j, 0, 0)), _p_row(P["pool_scale"], 256)]
    (yb_pre,) = _col_call(f"pool_f{i}", _fwd_only(_pool_fn), 256, 4, [_rows(proj, 256, OFF_POOL // 256)],
                          pool_params, [BF16])
    y_b = _matmul(f"wb_f{i}", yb_pre, W["w_b"], "nn", out_dtype=BF16)
    sv["yb_pre"], sv["y_b"] = yb_pre, y_b

    conv_params = [_p_row(P["ssd_conv_w"][k], 128) for k in range(4)] + [_p_row(P["ssd_conv_b"], 128)]
    (xbc_c,) = _col_call(f"ssdconv_f{i}", _fwd_only(_ssd_conv_fn), 128, SSD_XBC // 128,
                         [_rows(proj, 128, OFF_XBC // 128)], conv_params, [F32])
    y_ssd, states = _ssd_fwd(f"ssd_f{i}", xbc_c, proj, P["a_row"], P["dtb_row"], P["d_exp"])
    (yc_pre,) = _row_call(f"ssdnorm_f{i}", _fwd_only(_gated_norm_fn), 512, 2,
                          [_rows(y_ssd, 512), _rows(proj, 512, OFF_Z // 512)], [_p_row(P["ssd_norm_w"], 512)], [BF16])
    y_c = _matmul(f"wc_f{i}", yc_pre, W["w_c"], "nn", out_dtype=BF16)
    sv["xbc_c"], sv["states"], sv["y_ssd"], sv["yc_pre"], sv["y_c"] = xbc_c, states, y_ssd, yc_pre, y_c

    gate_rows = [_rows(proj, D, k) for k in range(3)] + [_rows(t, D) for t in (y_a, y_b, y_c)]
    gate_params = [_p_row(P["b_gate"], D, k) for k in range(3)]
    (merged,) = _row_call(f"gate_f{i}", _fwd_only(_gate_merge_fn), D, 1, gate_rows, gate_params, [BF16])
    x1 = _matmul(f"wo_f{i}", merged, W["w_o"], "nn", add=x)
    sv["merged"], sv["x1"] = merged, x1

    (u2,) = _row_call(f"ln2_f{i}", _fwd_only(_rmsnorm_fn), D, 1, [_rows(x1, D)], [_p_row(P["ln2_g"], D)], [BF16])
    up = _matmul(f"up_f{i}", u2, W["ffn_w_up"], "nn", out_dtype=BF16)
    (act,) = _col_call(f"ffnact_f{i}", lambda *a: (_ffn_act_fn(*_ffn_halves(a)), ()), 128, D_FF // 128,
                       [_rows(up, 256)], _ffn_params(P), [BF16])
    x2 = _matmul(f"down_f{i}", act, W["ffn_w_down"], "nn", add=x1)
    sv["u2"], sv["up"], sv["act"] = u2, up, act
    return x2, sv


def _ffn_params(P):
    return [_p_row(P["ffn_conv_w"][k], 256) for k in range(3)] + [_p_row(_interleave_ffn(P["ffn_conv_b"]), 256)]


def _ffn_halves(vals):
    (xa, xv), (a0, v0), (a1, v1), (a2, v2), (ab, vb) = ((t[:, :128], t[:, 128:]) for t in vals)
    return xa, xv, a0, a1, a2, ab, v0, v1, v2, vb


def _interleave_ffn(row):
    return row.reshape(2, D_FF // 128, 128).transpose(1, 0, 2).reshape(1, 2 * D_FF)


def _deinterleave_ffn(row):
    return row.reshape(D_FF // 128, 2, 128).transpose(1, 0, 2).reshape(1, 2 * D_FF)


def _layer_bwd(i, dx2, sv, W, P, bias_tabs, onehots, on_sharded_grads):
    G = {}
    x, proj, x1 = sv["x"], sv["proj"], sv["x1"]

    dact = _matmul(f"down_bx{i}", dx2, W["ffn_w_down"], "nt", out_dtype=BF16)
    G["ffn_w_down"] = _matmul(f"down_bw{i}", sv["act"], dx2, "tn", out_dtype=BF16)
    nb = D_FF // 128
    up = sv["up"]
    def ffn_bwd(x2, t0, t1, t2, b2, dact_):
        (dxa, dxv), (a0, a1, a2, ab, v0, v1, v2, vb) = _with_vjp(_ffn_act_fn, 10, (0, 1), tuple(range(2, 10)))(
            *_ffn_halves((x2, t0, t1, t2, b2)), dact_)
        pair = lambda a, v: jnp.concatenate([a, v], axis=1)
        return (pair(dxa, dxv),), (pair(a0, v0), pair(a1, v1), pair(a2, v2), pair(ab, vb))

    dup, t0, t1, t2, tb = _col_call(
        f"ffnact_b{i}", ffn_bwd, 256, nb, [_rows(up, 256)], _ffn_params(P) + [_rows_as_param(dact, 128)],
        [BF16], [_a_row(2 * D_FF, 256)] * 4)
    G["ffn_conv_w"] = jnp.concatenate([t0, t1, t2], axis=0)
    G["ffn_conv_b"] = _deinterleave_ffn(tb)[0]
    du2 = _matmul(f"up_bx{i}", dup, W["ffn_w_up"], "nt")
    G["ffn_w_up"] = _matmul(f"up_bw{i}", sv["u2"], dup, "tn", out_dtype=BF16)

    def norm_bwd(x_, g_, du_, dres):
        (dxn,), (dg,) = _with_vjp(_rmsnorm_fn, 2, (0,), (1,))(x_, g_, du_)
        return (dxn + dres,), (dg,)

    (dx1,), (G["ln2_g"],) = _split_res(_row_call(
        f"ln2_b{i}", lambda x_, du_, dres, g_: norm_bwd(x_, g_, du_, dres), D, 1,
        [_rows(x1, D), _rows(du2, D), _rows(dx2, D)], [_p_row(P["ln2_g"], D)], [F32], [_a_row(D, D)]), 1)

    dmerged = _matmul(f"wo_bx{i}", dx1, W["w_o"], "nt", out_dtype=BF16)
    G["w_o"] = _matmul(f"wo_bw{i}", sv["merged"], dx1, "tn", out_dtype=BF16)
    def gate_bwd(g_, y_, dm, b_):
        return _with_vjp(lambda g, y, b: (jax.nn.sigmoid(g + b) * y,), 3, (0, 1), (2,))(g_, y_, b_, dm)

    dproj, dys, dbs = None, [], []
    for k, t in enumerate(("y_a", "y_b", "y_c")):
        dproj, dy_k, db_k = _row_call(
            f"gate_b{i}_{k}", gate_bwd, D, 1, [_rows(proj, D, k), _rows(sv[t], D), _rows(dmerged, D)],
            [_p_row(P["b_gate"], D, k)], [BF16, BF16], [_a_row(D, D)], into={0: (dproj, k, NP)})
        dys.append(dy_k)
        dbs.append(db_k)
    dya, dyb, dyc = dys
    G["b_gate"] = jnp.concatenate(dbs, axis=1)[0]

    dyc_pre = _matmul(f"wc_bx{i}", dyc, W["w_c"], "nt", out_dtype=BF16)
    G["w_c"] = _matmul(f"wc_bw{i}", sv["yc_pre"], dyc, "tn", out_dtype=BF16)

    def gnorm_bwd(y_, z_, dy_, w_):
        return _with_vjp(_gated_norm_fn, 3, (0, 1), (2,))(y_, z_, w_, dy_)

    dy_ssd, dproj, dnw = _row_call(
        f"ssdnorm_b{i}", gnorm_bwd, 512, 2,
        [_rows(sv["y_ssd"], 512), _rows(proj, 512, OFF_Z // 512), _rows(dyc_pre, 512)],
        [_p_row(P["ssd_norm_w"], 512)], [F32, BF16], [_a_row(D, 512)], into={1: (dproj, OFF_Z // 512, NP)})
    G["ssd_norm_w"] = dnw[0]
    dxbc_c, dproj, da_row, ddtb_row, dd_exp = _ssd_bwd(f"ssd_b{i}", sv["xbc_c"], proj, sv["states"], dy_ssd,
                                                       P["a_row"], P["dtb_row"], P["d_exp"], dproj)
    a_vec = P["a_row"][0, :SSD_HEADS]
    G["ssd_a_log"] = da_row[0, :SSD_HEADS] * a_vec
    G["ssd_dt_bias"] = ddtb_row[0, :SSD_HEADS]
    G["ssd_d"] = dd_exp.reshape(SSD_HEADS, HEAD_DIM).sum(axis=1)
    conv_params = [_p_row(P["ssd_conv_w"][k], 128) for k in range(4)] + [_p_row(P["ssd_conv_b"], 128)]

    def conv_bwd(x_, dy_, w0, w1, w2, w3, b_):
        return _with_vjp(_ssd_conv_fn, 6, (0,), (1, 2, 3, 4, 5))(x_, w0, w1, w2, w3, b_, dy_)

    dproj, c0, c1, c2, c3, cb = _col_call(
        f"ssdconv_b{i}", conv_bwd, 128, SSD_XBC // 128, [_rows(proj, 128, OFF_XBC // 128), _rows(dxbc_c, 128)],
        conv_params, [BF16], [_a_row(SSD_XBC, 128)] * 5, into={0: (dproj, OFF_XBC // 128, NP)})
    G["ssd_conv_w"] = jnp.concatenate([c0, c1, c2, c3], axis=0)
    G["ssd_conv_b"] = cb[0]

    dyb_pre = _matmul(f"wb_bx{i}", dyb, W["w_b"], "nt", out_dtype=BF16)
    G["w_b"] = _matmul(f"wb_bw{i}", sv["yb_pre"], dyb, "tn", out_dtype=BF16)
    pool_params = [(W["pool_w"], (N_DEV, None, 32, 256), lambda j, i_: (0, j, 0, 0)), _p_row(P["pool_scale"], 256)]

    def pool_bwd(x_, dy_, wg, sc):
        return _with_vjp(_pool_fn, 3, (0,), (1, 2))(x_, wg.astype(F32), sc, dy_)

    dproj, dwg, dsc = _col_call(
        f"pool_b{i}", pool_bwd, 256, 4, [_rows(proj, 256, OFF_POOL // 256), _rows(dyb_pre, 256)], pool_params, [BF16],
        [((N_DEV, 4, 32, 256), (N_DEV, None, 32, 256), lambda j, i_: (0, j, 0, 0)), _a_row(D, 256)],
        into={0: (dproj, OFF_POOL // 256, NP)})
    G["pool_w"] = dwg
    G["pool_scale"] = dsc[0]

    datt = _matmul(f"wa_bx{i}", dya, W["w_a"], "nt", out_dtype=BF16)
    G["w_a"] = _matmul(f"wa_bw{i}", sv["att"], dya, "tn", out_dtype=BF16)

    def merge_bwd(o0, o1, o2, l0, l1, l2, da_):
        return _with_vjp(_att_merge_fn, 6, (0, 1, 2, 3, 4, 5), ())(o0, o1, o2, l0, l1, l2, da_)

    dol = _row_call(f"attmerge_b{i}", merge_bwd, ATT_GW, 1,
                    [_rows(t, ATT_GW) for t in sv["att_o"] + sv["att_l"]] + [_rows(datt, ATT_GW)], [], [F32] * 6)
    g_rel = jnp.zeros((REL_BUCKETS, 18), F32)
    for gi in range(len(ATT_GROUPS)):
        dproj, gbp, gbc = _att_bwd(f"att_b{i}_{gi}", proj, gi, bias_tabs[gi][0], bias_tabs[gi][1],
                                   dol[gi], dol[3 + gi], dproj)
        oh = onehots[gi]
        gt = (jnp.einsum("hqk,qkb->bh", gbp, oh[0], precision=lax.Precision.HIGHEST)
              + jnp.einsum("hqk,qkb->bh", gbc, oh[1], precision=lax.Precision.HIGHEST))
        g_rel = g_rel.at[:, gi * 6:(gi + 1) * 6].add(gt)
    G["rel_bias"] = g_rel

    du = _matmul(f"inproj_bx{i}", dproj, W["w_in"], "nt")
    G["w_in"] = _matmul(f"inproj_bw{i}", sv["u"], dproj, "tn", out_dtype=BF16)
    ln1_g = P["ln1_g"] + on_sharded_grads(G)
    (dx,), (G["ln1_g"],) = _split_res(_row_call(
        f"ln1_b{i}", lambda x_, du_, dres, g_: norm_bwd(x_, g_, du_, dres), D, 1,
        [_rows(x, D), _rows(du, D), _rows(dx1, D)], [_p_row(ln1_g, D)], [F32], [_a_row(D, D)]), 1)
    G["ln1_g"] = G["ln1_g"][0]
    G["ln2_g"] = G["ln2_g"][0]
    return dx, G


def _rows_as_param(arr, cw):
    return (arr, (arr.shape[0], cw), lambda j, i: (0, j))


def _split_res(res, n_out):
    return tuple(res[:n_out]), tuple(res[n_out:])


def kernel(x, rel_bias, ln1_g, w_in, b_gate, w_a, pool_w, pool_scale, w_b, ssd_conv_w, ssd_conv_b, ssd_dt_bias, ssd_a_log, ssd_d, ssd_norm_w, w_c, w_o, ln2_g, ffn_w_up, ffn_conv_w, ffn_conv_b, ffn_w_down, final_g, loss_target, m_rel_bias, m_ln1_g, m_w_in, m_b_gate, m_w_a, m_pool_w, m_pool_scale, m_w_b, m_ssd_conv_w, m_ssd_conv_b, m_ssd_dt_bias, m_ssd_a_log, m_ssd_d, m_ssd_norm_w, m_w_c, m_w_o, m_ln2_g, m_ffn_w_up, m_ffn_conv_w, m_ffn_conv_b, m_ffn_w_down, m_final_g, v_rel_bias, v_ln1_g, v_w_in, v_b_gate, v_w_a, v_pool_w, v_pool_scale, v_w_b, v_ssd_conv_w, v_ssd_conv_b, v_ssd_dt_bias, v_ssd_a_log, v_ssd_d, v_ssd_norm_w, v_w_c, v_w_o, v_ln2_g, v_ffn_w_up, v_ffn_conv_w, v_ffn_conv_b, v_ffn_w_down, v_final_g):
    args = locals()
    wts = {n: args[n] for n in WEIGHTS}
    mom = {n: args["m_" + n] for n in WEIGHTS}
    var = {n: args["v_" + n] for n in WEIGHTS}
    names = list(SHARDED)

    onehots = _bucket_onehots()
    bias_tabs = []
    for gi in range(3):
        tab = rel_bias[:, gi * 6:(gi + 1) * 6]
        b = jnp.einsum("pqkb,bh->phqk", onehots[gi], tab, precision=lax.Precision.HIGHEST)
        bias_tabs.append((b[0], b[1]))

    def gather_start(tag, i, which, after):
        shards = [wts[n][i].astype(BF16) if n in MATMUL_WEIGHTS else wts[n][i] for n in which]
        return _exchange_start(f"gather_start{tag}", shards, False, after)

    def layer_params(i, which, landed):
        full = {n: _local_weight(f"local_{n}{i}", n, g) for n, g in zip(which, landed)}
        W = {n: full[n] for n in which if n in MATMUL_WEIGHTS}
        P = {}
        if "ssd_conv_w" in full:
            P["ssd_conv_w"] = [_row(full["ssd_conv_w"][k]) for k in range(4)]
            P["ffn_conv_w"] = [_row(full["ffn_conv_w"][k]) for k in range(3)]
        return W, P

    def replicated_params(i):
        return {"ln1_g": _row(ln1_g[i]), "ln2_g": _row(ln2_g[i]), "b_gate": _row(b_gate[i]),
                "pool_scale": _row(pool_scale[i]), "ssd_conv_b": _row(ssd_conv_b[i]),
                "ssd_norm_w": _row(ssd_norm_w[i]), "ffn_conv_b": _row(ffn_conv_b[i]),
                "a_row": _row(-jnp.exp(ssd_a_log[i]), 128), "dtb_row": _row(ssd_dt_bias[i], 128),
                "d_exp": _row(jnp.repeat(ssd_d[i], HEAD_DIM))}

    h = x.reshape(S, D)
    saved, Ws, Ps = [], [], []
    first, rest = ["w_in"], [n for n in names if n != "w_in"]
    landed_first = [_gather_chip_once("gather_w_in0", w_in[0].astype(BF16))]
    state_rest, token0 = gather_start("0b", 0, rest, landed_first[0])
    nxt = {}
    nxt["state"], token1 = gather_start("1", 1, names, token0)
    token = token0 + token1

    def late0(att):
        landed_rest = _exchange_wait("gather_wait0b", state_rest, att)
        return layer_params(0, rest, landed_rest)

    for i in range(DEPTH):
        P = replicated_params(i)
        if i == 0:
            W, P1 = layer_params(0, first, landed_first)
        else:
            W, P1 = layer_params(i, names, landed)
            if i + 1 < DEPTH:
                nxt["state"], token = gather_start(str(i + 1), i + 1, names, landed[0])
        P.update(P1)
        if i + 1 < DEPTH:
            P["ln1_g"] = P["ln1_g"] + token[0, 0]
        h, sv = _layer_fwd(i, h, W, P, bias_tabs, late0 if i == 0 else None)
        Ws.append(W)
        Ps.append(dict(P, ln1_g=_row(ln1_g[i]), pool_scale=_row(pool_scale[i])))
        saved.append(sv)
        if i + 1 < DEPTH:
            landed = _exchange_wait(f"gather_wait{i + 1}", nxt["state"], h)

    def loss_bwd(x_, t_, g_):
        lval, vjp = jax.vjp(_loss_fn, x_, t_, g_)
        dx_, _, dg_ = vjp(jnp.ones_like(lval))
        return (dx_,), (dg_, jnp.broadcast_to(lval, (1, 128)))

    dh, g_final, loss_part = _row_call("loss", loss_bwd, D, 1, [_rows(h, D), _rows(loss_target.reshape(S, D), D)],
                                       [_p_row(_row(final_g), D)], [F32], [_a_row(D, D), _a_row(128, 128)])
    loss = lax.psum(loss_part[0, 0], MESH_AXES)

    grads = {n: [None] * DEPTH for n in WEIGHTS if n not in ("rel_bias", "final_g")}
    g_rel = jnp.zeros((REL_BUCKETS, 18), F32)
    slots = [None] * DEPTH
    pending = None
    for i in reversed(range(DEPTH)):
        started = {}

        def on_sharded_grads(G, i=i, started=started):
            parts = [_device_blocks(f"blocks_{n}{i}", n, G[n]) for n in names]
            started["state"], token = _exchange_start(f"scatter_start{i}", parts, True, G["b_gate"])
            return token[0, 0]

        dh, G = _layer_bwd(i, dh, saved[i], Ws[i], Ps[i], bias_tabs, onehots, on_sharded_grads)
        if pending is not None:
            j, st = pending
            slots[j] = _exchange_wait(f"scatter_wait{j}", st, dh)
        pending = (i, started["state"])
        g_rel = g_rel + G.pop("rel_bias")
        for n, g in G.items():
            grads[n][i] = g
    grad_x = dh.reshape(1, S, D)
    local = {n: jnp.stack(grads[n]) for n in grads if n not in SHARDED}
    local["rel_bias"] = g_rel
    local["final_g"] = g_final[0]
    out = {}

    def pack(d):
        flat = jnp.concatenate([d[n].reshape(-1).astype(F32) for n in REPLICATED])
        rows = -(-flat.shape[0] // (8 * 128)) * 8
        return jnp.pad(flat, (0, rows * 128 - flat.shape[0])).reshape(rows, 128)

    (rep_slots,) = _exchange("gather_small_grads", [pack(local)], scatter=False)
    rep = _adamw("adamw_small", rep_slots, pack(wts), pack(mom), pack(var))
    off = 0
    for n in REPLICATED:
        sz = int(np.prod(wts[n].shape))
        out[n] = [t.reshape(-1)[off:off + sz].reshape(wts[n].shape) for t in rep]
        off += sz

    def flat2(n):
        shp = wts[n].shape
        r, c = int(np.prod(shp[:-1])), shp[-1]
        return r, c, wts[n].reshape(r, c), mom[n].reshape(r, c), var[n].reshape(r, c)

    chain = {}
    done = rep[0][0, 0]
    for k, n in enumerate(names):
        if n in MATMUL_WEIGHTS:
            r, c, w2, m2, v2 = flat2(n)
            res = None
            for i in (3, 2, 1):
                res = _adamw(f"adamw_{n}{i}", slots[i][k].reshape(N_DEV, r // DEPTH, c), w2, m2, v2,
                             first_row=i * (r // DEPTH), prev=res)
            chain[n] = res
            done = done + res[0][-1, 0]
    slots[0] = _exchange_wait("scatter_wait0", pending[1], done.reshape(1, 1))
    for k, n in enumerate(names):
        r, c, w2, m2, v2 = flat2(n)
        if n in MATMUL_WEIGHTS:
            res = _adamw(f"adamw_{n}0", slots[0][k].reshape(N_DEV, r // DEPTH, c), w2, m2, v2, first_row=0, prev=chain[n])
        else:
            stacked = jnp.stack([slots[i][k] for i in range(DEPTH)], axis=1)
            res = _adamw("adamw_" + n, stacked.reshape(N_DEV, r, c), w2, m2, v2)
        out[n] = [t.reshape(wts[n].shape) for t in res]

    return (loss, grad_x, *[out[n][0] for n in WEIGHTS], *[out[n][1] for n in WEIGHTS],
            *[out[n][2] for n in WEIGHTS], *[out[n][3] for n in WEIGHTS])
```

```python
import functools
import math

import numpy as np
import jax
import jax.numpy as jnp
from jax import lax
from jax.experimental import pallas as pl
from jax.experimental.pallas import tpu as pltpu

F32 = jnp.float32
BF16 = jnp.bfloat16

N_DEV = 8
MESH_AXES = ("x", "y", "c")
S = 4096
D = 1024
DEPTH = 4
HEAD_DIM = 64
ATT_W = 1152
ATT_GW = 384
ATT_GROUPS = ((128, 1), (512, 4), (2048, 16))
ATT_BLOCK = 128
REL_BUCKETS = 32
REL_MAX_DISTANCE = 2048
POOL_WINDOWS = (2, 4, 8, 16)
SSD_HEADS = 16
SSD_CHUNK = 128
SSD_XBC = 1536
D_FF = 2816
IN_WIDTH = 10128
EPS = 1e-6
NEG = -1e30

OFF_GATE, OFF_POOL, OFF_Z, OFF_XBC, OFF_DT, OFF_QKV = 0, 3072, 4096, 5120, 6656, 6912
NP = 10368
DT_PAD = 128
QKV_W = 3 * 2 * HEAD_DIM

ADAM_LR, ADAM_B1, ADAM_B2, ADAM_EPS, ADAM_WD, ADAM_STEP = 0.001, 0.9, 0.999, 1e-08, 0.01, 10

VMEM_LIMIT = 52 * 1024 * 1024


def _cparams(sem=None):
    return pltpu.CompilerParams(dimension_semantics=sem, vmem_limit_bytes=VMEM_LIMIT)


def _dot(a, b, ca, cb):
    return lax.dot_general(a.astype(BF16), b.astype(BF16), (((ca,), (cb,)), ((), ())), preferred_element_type=F32)


@jax.custom_vjp
def _mm(a, b):
    return _dot(a, b, 1, 0)


def _mm_fwd(a, b):
    return _mm(a, b), (a, b)


def _mm_bwd(res, g):
    a, b = res
    return _dot(g, b, 1, 1).astype(a.dtype), _dot(a, g, 0, 0).astype(b.dtype)


_mm.defvjp(_mm_fwd, _mm_bwd)


@jax.custom_vjp
def _mm_nt(a, b):
    return _dot(a, b, 1, 1)


def _mm_nt_fwd(a, b):
    return _mm_nt(a, b), (a, b)


def _mm_nt_bwd(res, g):
    a, b = res
    return _dot(g, b, 1, 0).astype(a.dtype), _dot(g, a, 0, 0).astype(b.dtype)


_mm_nt.defvjp(_mm_nt_fwd, _mm_nt_bwd)


@jax.custom_vjp
def _mm_tn(a, b):
    return _dot(a, b, 0, 0)


def _mm_tn_fwd(a, b):
    return _mm_tn(a, b), (a, b)


def _mm_tn_bwd(res, g):
    a, b = res
    return _dot(b, g, 1, 1).astype(a.dtype), _dot(a, g, 1, 0).astype(b.dtype)


_mm_tn.defvjp(_mm_tn_fwd, _mm_tn_bwd)


def _shift_impl(x, j):
    n = x.shape[0]
    if j == 0:
        return x
    r = pltpu.roll(x, j % n, axis=0)
    t = lax.broadcasted_iota(jnp.int32, x.shape, 0)
    mask = (t >= j) if j > 0 else (t < n + j)
    return jnp.where(mask, r, 0.0)


@functools.partial(jax.custom_vjp, nondiff_argnums=(1,))
def _shift(x, j):
    return _shift_impl(x, j)


_shift.defvjp(lambda x, j: (_shift_impl(x, j), None), lambda j, _, g: (_shift_impl(g, -j),))


def _tri(lower):
    r = lax.broadcasted_iota(jnp.int32, (SSD_CHUNK, SSD_CHUNK), 0)
    c = lax.broadcasted_iota(jnp.int32, (SSD_CHUNK, SSD_CHUNK), 1)
    return (r >= c) if lower else (r <= c)


def _dot_hi(a, b):
    return lax.dot_general(a, b, (((1,), (0,)), ((), ())), precision=lax.Precision.HIGHEST,
                           preferred_element_type=F32)


@jax.custom_vjp
def _cumsum_rows(a):
    return _dot_hi(_tri(True).astype(F32), a)


_cumsum_rows.defvjp(lambda a: (_cumsum_rows(a), None), lambda _, g: (_dot_hi(_tri(False).astype(F32), g),))


@jax.custom_vjp
def _softplus(x):
    return jnp.maximum(x, 0.0) + jnp.log(1.0 + jnp.exp(-jnp.abs(x)))


_softplus.defvjp(lambda x: (_softplus(x), x), lambda x, g: (g * jax.nn.sigmoid(x),))


def _silu(x):
    return x * jax.nn.sigmoid(x)


def _rows(arr, cw, off=0, lead=None, roff=0):
    return (arr, cw, off, lead, roff)


def _tiled(name, fn, grid, tm, rows, params, outs, accs=(), out_roff=0, prev_outs=None, into=None):
    into = into or {}
    ncol, nrow = grid
    in_specs, operands = [], []
    for arr, cw, off, lead, roff in rows:
        if lead is None:
            in_specs.append(pl.BlockSpec((tm, cw), functools.partial(lambda j, i, off, roff: (roff + i, off + j),
                                                                     off=off, roff=roff)))
        else:
            in_specs.append(pl.BlockSpec((None, tm, cw), functools.partial(
                lambda j, i, off, lead, roff: (lead, roff + i, off + j), off=off, lead=lead, roff=roff)))
        operands.append(arr)
    for arr, bs, im in params:
        in_specs.append(pl.BlockSpec(bs, im))
        operands.append(arr)
    out_specs, out_shape = [], []
    for k, (n_rows, cw, dt) in enumerate(outs):
        _, coff, total = into.get(k, (None, 0, ncol * cw))
        out_specs.append(pl.BlockSpec((tm, cw), functools.partial(lambda j, i, r, c: (r + i, c + j), r=out_roff, c=coff)))
        out_shape.append(jax.ShapeDtypeStruct((n_rows, total), dt))
    for shape, bs, im in accs:
        out_specs.append(pl.BlockSpec(bs, im))
        out_shape.append(jax.ShapeDtypeStruct(shape, F32))
    n_in, n_out = len(operands), len(outs)
    aliases = {}
    earlier = dict(enumerate(prev_outs)) if prev_outs is not None else {}
    earlier.update({k: v[0] for k, v in into.items() if v[0] is not None})
    for k, p in sorted(earlier.items()):
        aliases[len(operands)] = k
        in_specs.append(pl.BlockSpec(memory_space=pl.ANY))
        operands.append(p)

    n_all = len(operands)

    def body(*refs):
        vals = [r[...] for r in refs[:n_in]]
        o_vals, a_vals = fn(*vals)
        for r, v in zip(refs[n_all:n_all + n_out], o_vals):
            r[...] = v.astype(r.dtype)
        i = pl.program_id(1)
        for r, v in zip(refs[n_all + n_out:], a_vals):
            @pl.when(i == 0)
            def _(r=r, v=v):
                r[...] = v.astype(r.dtype)

            @pl.when(i > 0)
            def _(r=r, v=v):
                r[...] += v.astype(r.dtype)

    res = pl.pallas_call(body, grid=grid, in_specs=in_specs, out_specs=out_specs, out_shape=out_shape, name=name,
                         input_output_aliases=aliases, compiler_params=_cparams(("arbitrary", "arbitrary")))(*operands)
    return list(res)


def _with_vjp(fn, n_prim, want_out, want_acc):
    def f(*args):
        prim, g = args[:n_prim], args[n_prim:]
        outs, vjp = jax.vjp(lambda *a: fn(*a), *prim)
        d = vjp(tuple(gi.astype(o.dtype) for gi, o in zip(g, outs)))
        return tuple(d[k] for k in want_out), tuple(d[k] for k in want_acc)
    return f


def _p_row(arr, cw, off=0):
    return (arr, (1, cw), functools.partial(lambda j, i, off: (0, off + j), off=off))


def _a_row(n, cw):
    return ((1, n), (1, cw), lambda j, i: (0, j))


def _pick(n, cap, mult):
    best = None
    for t in range(mult, min(n, cap) + 1, mult):
        if n % t == 0:
            best = t
    return best if best is not None else n


def _matmul(name, a, b, mode, add=None, out_dtype=F32):
    if mode == "nn":
        (M, K), N = a.shape, b.shape[1]
    elif mode == "nt":
        (M, K), N = a.shape, b.shape[0]
    else:
        (K, M), N = a.shape, b.shape[1]
    tn = _pick(N, 1536, 128)
    k_cap = 2048 if mode == "tn" else 3456
    tk = K if K <= k_cap else _pick(K, k_cap, 128)
    nk = K // tk
    tm = _pick(M, 1408, 128) if mode == "tn" else _pick(M, 1024, 8)
    a_bytes, b_bytes = a.size * a.dtype.itemsize, b.size * b.dtype.itemsize
    swap = nk == 1 and a_bytes * (N // tn) + b_bytes < b_bytes * (M // tm) + a_bytes
    ij = (lambda g0, g1: (g1, g0)) if swap else (lambda g0, g1: (g0, g1))

    def spec(block, index):
        return pl.BlockSpec(block, lambda g0, g1, k: index(*ij(g0, g1), k))

    if mode == "nn":
        a_spec = spec((tm, tk), lambda i, j, k: (i, k))
        b_spec = spec((tk, tn), lambda i, j, k: (k, j))
        ca, cb = 1, 0
    elif mode == "nt":
        a_spec = spec((tm, tk), lambda i, j, k: (i, k))
        b_spec = spec((tn, tk), lambda i, j, k: (j, k))
        ca, cb = 1, 1
    else:
        a_spec = spec((tk, tm), lambda i, j, k: (k, i))
        b_spec = spec((tk, tn), lambda i, j, k: (k, j))
        ca, cb = 0, 0
    in_specs, operands = [a_spec, b_spec], [a, b]
    if add is not None:
        in_specs.append(spec((tm, tn), lambda i, j, k: (i, j)))
        operands.append(add)

    def finish(r, refs, o_ref):
        if add is not None:
            r = r + refs[2][...]
        o_ref[...] = r.astype(o_ref.dtype)

    def body_single(*refs):
        finish(_dot(refs[0][...], refs[1][...], ca, cb), refs, refs[-1])

    def body_multi(*refs):
        o_ref, acc_ref = refs[-2], refs[-1]
        k = pl.program_id(2)
        d = _dot(refs[0][...], refs[1][...], ca, cb)

        @pl.when(k == 0)
        def _():
            acc_ref[...] = d

        @pl.when(jnp.logical_and(k > 0, k < nk - 1))
        def _():
            acc_ref[...] += d

        @pl.when(k == nk - 1)
        def _():
            finish(acc_ref[...] + d, refs, o_ref)

    grid = (N // tn, M // tm, nk) if swap else (M // tm, N // tn, nk)
    return pl.pallas_call(
        body_single if nk == 1 else body_multi, grid=grid, in_specs=in_specs,
        out_specs=spec((tm, tn), lambda i, j, k: (i, j)),
        out_shape=jax.ShapeDtypeStruct((M, N), out_dtype),
        scratch_shapes=[] if nk == 1 else [pltpu.VMEM((tm, tn), F32)], name=name,
        compiler_params=_cparams(("parallel", "parallel", "arbitrary")))(*operands)


def _seg_copies(segs, c):
    out = []
    for lo, hi, dst in segs:
        n = lo
        while n < hi:
            p = n // c
            w = min(hi, (p + 1) * c) - n
            out.append((p, n - p * c, w, dst + n - lo))
            n += w
    return out


def _col_assemble(name, blocks, copies, zeros, n_out):
    _, R, c = blocks.shape
    tm = R if R <= 128 else 128

    def body(b_ref, o_ref):
        for p, s, w, d in copies:
            o_ref[:, d:d + w] = b_ref[p, :, s:s + w]
        for lo, hi in zeros:
            o_ref[:, lo:hi] = jnp.zeros((tm, hi - lo), o_ref.dtype)

    return pl.pallas_call(
        body, grid=(R // tm,), in_specs=[pl.BlockSpec((N_DEV, tm, c), lambda i: (0, i, 0))],
        out_specs=pl.BlockSpec((tm, n_out), lambda i: (i, 0)),
        out_shape=jax.ShapeDtypeStruct((R, n_out), blocks.dtype), name=name, compiler_params=_cparams(("parallel",)))(blocks)


def _col_split(name, full, copies, c, dtype):
    R, n = full.shape
    tm = R if R <= 128 else 128

    def body(f_ref, o_ref):
        for p, s, w, d in copies:
            o_ref[p, :, s:s + w] = f_ref[:, d:d + w].astype(dtype)

    return pl.pallas_call(
        body, grid=(R // tm,), in_specs=[pl.BlockSpec((tm, n), lambda i: (i, 0))],
        out_specs=pl.BlockSpec((N_DEV, tm, c), lambda i: (0, i, 0)),
        out_shape=jax.ShapeDtypeStruct((N_DEV, R, c), dtype), name=name, compiler_params=_cparams(("parallel",)))(full)


def _rmsnorm_fn(x, g):
    x = x.astype(F32)
    return (x * lax.rsqrt(jnp.mean(x * x, axis=-1, keepdims=True) + EPS) * g,)


def _gate_merge_fn(g0, g1, g2, ya, yb, yc, b0, b1, b2):
    return (jax.nn.sigmoid(g0 + b0) * ya + jax.nn.sigmoid(g1 + b1) * yb + jax.nn.sigmoid(g2 + b2) * yc,)


def _gated_norm_fn(y, z, w):
    t = y * _silu(z)
    return (t * lax.rsqrt(jnp.mean(t * t, axis=-1, keepdims=True) + EPS) * w,)


def _att_merge_fn(o0, o1, o2, l0, l1, l2):
    m = lax.stop_gradient(jnp.maximum(jnp.maximum(l0, l1), l2))
    e0, e1, e2 = jnp.exp(l0 - m), jnp.exp(l1 - m), jnp.exp(l2 - m)
    return ((e0 * o0 + e1 * o1 + e2 * o2) / (e0 + e1 + e2),)


def _loss_fn(x, tgt, g):
    (y,) = _rmsnorm_fn(x, g)
    err = y - tgt
    return 0.5 * jnp.sum(jnp.mean(err * err, axis=-1, keepdims=True), axis=0, keepdims=True)


def _pool_fn(x, wg, scale):
    g = pl.program_id(0)
    s2 = x + _shift(x, 1)
    s4 = s2 + _shift(s2, 2)
    s8 = s4 + _shift(s4, 4)
    s16 = s8 + _shift(s8, 8)
    win = ((g == 0).astype(F32) * s2 + (g == 1).astype(F32) * s4 + (g == 2).astype(F32) * s8
           + (g == 3).astype(F32) * s16)
    t = lax.broadcasted_iota(jnp.int32, (x.shape[0], 1), 0) + 1
    cnt = jnp.minimum(t, jnp.left_shift(2, g)).astype(F32)
    d = win / cnt - x
    return (_mm(d, wg.reshape(256, 256)) * scale,)


def _dwconv(x, taps, b):
    k = len(taps)
    y = taps[k - 1] * x + b
    for i in range(k - 1):
        y = y + taps[i] * _shift(x, k - 1 - i)
    return y


def _ssd_conv_fn(x, w0, w1, w2, w3, b):
    return (_silu(_dwconv(x, (w0, w1, w2, w3), b)),)


def _ffn_act_fn(xa, xv, a0, a1, a2, ab, v0, v1, v2, vb):
    xa, xv = xa.astype(F32), xv.astype(F32)
    return (_silu(_dwconv(xa, (a0, a1, a2), ab)) * _dwconv(xv, (v0, v1, v2), vb),)


@jax.custom_vjp
def _halves(x):
    return x[:ATT_BLOCK], x[ATT_BLOCK:]


_halves.defvjp(lambda x: (_halves(x), None), lambda _, g: (jnp.concatenate([g[0], g[1]], axis=0),))


def _att_block(q, kp, kc, vp, vc, bpa, bpb, bca, bcb, prev_ok):
    n = ATT_BLOCK
    lane = lax.broadcasted_iota(jnp.int32, (1, 2 * HEAD_DIM), 1)
    ma = (lane < HEAD_DIM).astype(F32)
    mb = 1.0 - ma
    q = q.astype(F32) * (1.0 / math.sqrt(HEAD_DIM))
    q2 = jnp.concatenate([q * ma, q * mb], axis=0)
    qi = lax.broadcasted_iota(jnp.int32, (2 * n, n), 0) & (n - 1)
    kj = lax.broadcasted_iota(jnp.int32, (2 * n, n), 1)
    sp = jnp.where(jnp.logical_and(kj >= qi, prev_ok), _mm_nt(q2, kp) + jnp.concatenate([bpa, bpb], axis=0), NEG)
    sc = jnp.where(kj <= qi, _mm_nt(q2, kc) + jnp.concatenate([bca, bcb], axis=0), NEG)
    m = lax.stop_gradient(jnp.maximum(jnp.max(sp, axis=1, keepdims=True), jnp.max(sc, axis=1, keepdims=True)))
    pp = jnp.exp(sp - m)
    pc = jnp.exp(sc - m)
    l = jnp.sum(pp, axis=1, keepdims=True) + jnp.sum(pc, axis=1, keepdims=True)
    oa, ob = _halves((_mm(pp, vp) + _mm(pc, vc)) / l)
    la, lb = _halves((m + jnp.log(l)) * jnp.ones((1, 2 * HEAD_DIM), F32))
    return oa * ma + ob * mb, la * ma + lb * mb


def _att_slab(dil):
    nbk = 8 if dil == 1 else 1
    t = ATT_BLOCK * dil * nbk
    return nbk, t, S // t


def _att_in_specs(gi, t):
    def spec(which, prev):
        col = OFF_QKV // 128 + gi * 9 + which

        def index(p, j, col=col, prev=prev):
            jj = jnp.minimum(j, S // t - 1)
            return (jnp.maximum(jj - 1, 0) if prev else jj, col + 3 * p)
        return pl.BlockSpec((t, 2 * HEAD_DIM), index)
    return [spec(0, False), spec(1, False), spec(1, True), spec(2, False), spec(2, True)]


def _bias_specs():
    return [pl.BlockSpec((None, ATT_BLOCK, ATT_BLOCK), functools.partial(lambda p, j, hh: (2 * p + hh, 0, 0), hh=hh))
            for hh in (0, 1)]


def _att_units(dil, nbk, body):
    def per_residue(r, carry):
        for b in range(nbk):
            rows = pl.ds(b * ATT_BLOCK * dil + r, ATT_BLOCK, stride=dil)
            prev = pl.ds(((b - 1) % nbk) * ATT_BLOCK * dil + r, ATT_BLOCK, stride=dil)
            body(b, rows, prev, b > 0)
        return carry
    if dil == 1:
        per_residue(0, 0)
    else:
        lax.fori_loop(0, dil, per_residue, 0, unroll=min(dil, 8))


def _att_fwd(name, proj, gi, bias_p, bias_c):
    dil = ATT_GROUPS[gi][1]
    nbk, t, ns = _att_slab(dil)
    bsp = _bias_specs()
    out_spec = pl.BlockSpec((t, 2 * HEAD_DIM), lambda p, j: (j, p))

    def body(q_ref, kc_ref, kp_ref, vc_ref, vp_ref, bpa, bpb, bca, bcb, o_ref, l_ref):
        first = pl.program_id(1) == 0
        biases = (bpa[...], bpb[...], bca[...], bcb[...])

        def unit(b, rows, prev, in_slab):
            kp = kc_ref[prev, :] if in_slab else kp_ref[prev, :]
            vp = vc_ref[prev, :] if in_slab else vp_ref[prev, :]
            prev_ok = True if in_slab else jnp.logical_not(first)
            o, lse = _att_block(q_ref[rows, :], kp, kc_ref[rows, :], vp, vc_ref[rows, :], *biases, prev_ok)
            o_ref[rows, :] = o
            l_ref[rows, :] = lse

        _att_units(dil, nbk, unit)

    shp = jax.ShapeDtypeStruct((S, ATT_GW), F32)
    return pl.pallas_call(
        body, grid=(3, ns), in_specs=_att_in_specs(gi, t) + [bsp[0], bsp[1], bsp[0], bsp[1]],
        out_specs=[out_spec, out_spec], out_shape=[shp, shp], name=name,
        compiler_params=_cparams(("arbitrary",) * 2))(proj, proj, proj, proj, proj, bias_p, bias_p, bias_c, bias_c)


def _att_bwd(name, proj, gi, bias_p, bias_c, do, dl, dproj):
    dil = ATT_GROUPS[gi][1]
    nbk, t, ns = _att_slab(dil)
    bsp = _bias_specs()
    blk = (t, 2 * HEAD_DIM)
    cur = pl.BlockSpec(blk, lambda p, j: (jnp.minimum(j, ns - 1), p))
    done = pl.BlockSpec((t, QKV_W), lambda p, j: (jnp.maximum(j - 1, 0), OFF_QKV // QKV_W + gi * 3 + p))
    gsp = pl.BlockSpec((None, ATT_BLOCK, ATT_BLOCK), lambda p, j: (p, 0, 0))

    def body(q_ref, kc_ref, kp_ref, vc_ref, vp_ref, bpa, bpb, bca, bcb, do_ref, dl_ref, _,
             dqkv_ref, gpa, gpb, gca, gcb, accq, acck, accv):
        j = pl.program_id(1)
        mine, other = acck.at[j % 2], acck.at[1 - j % 2]
        mine_v, other_v = accv.at[j % 2], accv.at[1 - j % 2]
        dq_ref, other_q = accq.at[j % 2], accq.at[1 - j % 2]

        @pl.when(j == 0)
        def _():
            for g in (gpa, gpb, gca, gcb):
                g[...] = jnp.zeros_like(g)
            other[...] = jnp.zeros_like(other)
            other_v[...] = jnp.zeros_like(other_v)
            other_q[...] = jnp.zeros_like(other_q)

        @pl.when(j < ns)
        def _():
            mine[...] = jnp.zeros_like(mine)
            mine_v[...] = jnp.zeros_like(mine_v)
            biases = (bpa[...], bpb[...], bca[...], bcb[...])

            def unit(b, rows, prev, in_slab):
                kp = kc_ref[prev, :] if in_slab else kp_ref[prev, :]
                vp = vc_ref[prev, :] if in_slab else vp_ref[prev, :]
                prev_ok = True if in_slab else j > 0
                prim = (q_ref[rows, :], kp, kc_ref[rows, :], vp, vc_ref[rows, :]) + biases
                _, vjp = jax.vjp(lambda *a: _att_block(*a, prev_ok), *prim)
                dq, dkp, dkc, dvp, dvc, dpa, dpb, dca, dcb = vjp((do_ref[rows, :], dl_ref[rows, :]))
                dq_ref[rows, :] = dq
                mine[rows, :] += dkc
                mine_v[rows, :] += dvc
                tgt, tgt_v = (mine, mine_v) if in_slab else (other, other_v)
                tgt[prev, :] += dkp
                tgt_v[prev, :] += dvp
                gpa[...] += dpa
                gpb[...] += dpb
                gca[...] += dca
                gcb[...] += dcb

            _att_units(dil, nbk, unit)

        w = 2 * HEAD_DIM
        dqkv_ref[:, 0:w] = other_q[...].astype(BF16)
        dqkv_ref[:, w:2 * w] = other[...].astype(BF16)
        dqkv_ref[:, 2 * w:3 * w] = other_v[...].astype(BF16)

    gshp = jax.ShapeDtypeStruct((3, ATT_BLOCK, ATT_BLOCK), F32)
    res = pl.pallas_call(
        body, grid=(3, ns + 1),
        in_specs=_att_in_specs(gi, t) + [bsp[0], bsp[1], bsp[0], bsp[1], cur, cur, pl.BlockSpec(memory_space=pl.ANY)],
        out_specs=[done, gsp, gsp, gsp, gsp],
        out_shape=[jax.ShapeDtypeStruct((S, NP), BF16), gshp, gshp, gshp, gshp],
        input_output_aliases={11: 0},
        scratch_shapes=[pltpu.VMEM((2,) + blk, F32)] * 3, name=name,
        compiler_params=_cparams(("arbitrary",) * 2))(proj, proj, proj, proj, proj, bias_p, bias_p, bias_c, bias_c, do, dl,
                                                      dproj)
    dproj, gpa, gpb, gca, gcb = res
    heads = lambda a, b: jnp.stack([a, b], axis=1).reshape(6, ATT_BLOCK, ATT_BLOCK)
    return dproj, heads(gpa, gpb), heads(gca, gcb)


N_PAIR = SSD_HEADS // 2


def _ssd_chunk(xs, bs, cs_in, dt_raw, hs, a_row, dtb_row, ds):
    lane = lax.broadcasted_iota(jnp.int32, (1, 128), 1)
    row = lax.broadcasted_iota(jnp.int32, (128, 1), 0)
    tril = _tri(True)
    dt = _softplus(dt_raw + dtb_row)
    acs = _cumsum_rows(dt * a_row)
    acs_t = acs.T
    gmat = [_mm_nt(cs_in[g], bs[g]) for g in range(2)]
    lo = lane < HEAD_DIM
    lo_r = row < HEAD_DIM
    last = (row == SSD_CHUNK - 1).astype(F32)
    ys, hn = [], []
    for p in range(N_PAIR):
        g = p // (N_PAIR // 2)
        col, dtc, mm, clast = [], [], [], []
        for hh in range(2):
            h = 2 * p + hh
            oh = (lane == h).astype(F32)
            c_col = jnp.sum(acs * oh, axis=1, keepdims=True)
            c_row = jnp.sum(acs_t * (row == h).astype(F32), axis=0, keepdims=True)
            col.append(c_col)
            dtc.append(jnp.sum(dt * oh, axis=1, keepdims=True))
            clast.append(jnp.sum(c_col * last, axis=0, keepdims=True))
            mm.append(gmat[g] * jnp.exp(jnp.where(tril, c_col - c_row, NEG)))
        x = xs[p]
        xd = x * jnp.where(lo, dtc[0], dtc[1])
        y = jnp.where(lo, _mm(mm[0], xd), _mm(mm[1], xd))
        y = y + jnp.where(lo, jnp.exp(col[0]), jnp.exp(col[1])) * _mm_nt(cs_in[g], hs[p])
        ys.append(y + ds[p] * x)
        dec = jnp.where(lo, jnp.exp(clast[0] - col[0]), jnp.exp(clast[1] - col[1]))
        hn.append(hs[p] * jnp.where(lo_r, jnp.exp(clast[0]), jnp.exp(clast[1])) + _mm_tn(xd * dec, bs[g]))
    return tuple(ys), tuple(hn)


def _ssd_load(xbc_ref, dt_ref, a_ref, dtb_ref, d_ref):
    xs = tuple(xbc_ref[:, 128 * p:128 * (p + 1)] for p in range(N_PAIR))
    bs = tuple(xbc_ref[:, D + 128 * g:D + 128 * (g + 1)] for g in range(2))
    cs = tuple(xbc_ref[:, D + 256 + 128 * g:D + 256 + 128 * (g + 1)] for g in range(2))
    ds = tuple(d_ref[:, 128 * p:128 * (p + 1)] for p in range(N_PAIR))
    return xs, bs, cs, dt_ref[...], a_ref[...], dtb_ref[...], ds


def _ssd_fwd(name, xbc_c, proj, a_row, dtb_row, d_exp):
    nc = S // SSD_CHUNK
    prow = lambda n: pl.BlockSpec((1, n), lambda c: (0, 0))

    def body(xbc_ref, dt_ref, a_ref, dtb_ref, d_ref, y_ref, st_ref, h_ref):
        @pl.when(pl.program_id(0) == 0)
        def _():
            h_ref[...] = jnp.zeros_like(h_ref)

        xs, bs, cs, dt_raw, a, dtb, ds = _ssd_load(xbc_ref, dt_ref, a_ref, dtb_ref, d_ref)
        hs = tuple(h_ref[p] for p in range(N_PAIR))
        ys, hn = _ssd_chunk(xs, bs, cs, dt_raw, hs, a, dtb, ds)
        for p in range(N_PAIR):
            y_ref[:, 128 * p:128 * (p + 1)] = ys[p]
            st_ref[p] = hs[p]
            h_ref[p] = hn[p]

    return pl.pallas_call(
        body, grid=(nc,),
        in_specs=[pl.BlockSpec((SSD_CHUNK, SSD_XBC), lambda c: (c, 0)),
                  pl.BlockSpec((SSD_CHUNK, DT_PAD), lambda c: (c, OFF_DT // DT_PAD)),
                  prow(128), prow(128), prow(D)],
        out_specs=[pl.BlockSpec((SSD_CHUNK, D), lambda c: (c, 0)),
                   pl.BlockSpec((None, N_PAIR, 128, 128), lambda c: (c, 0, 0, 0))],
        out_shape=[jax.ShapeDtypeStruct((S, D), F32), jax.ShapeDtypeStruct((nc, N_PAIR, 128, 128), F32)],
        scratch_shapes=[pltpu.VMEM((N_PAIR, 128, 128), F32)], name=name,
        compiler_params=_cparams(("arbitrary",)))(xbc_c, proj, a_row, dtb_row, d_exp)


def _ssd_bwd(name, xbc_c, proj, states, dy, a_row, dtb_row, d_exp, dproj):
    nc = S // SSD_CHUNK
    prow = lambda n: pl.BlockSpec((1, n), lambda i: (0, 0))
    rc = lambda i: nc - 1 - i

    def body(xbc_ref, dt_ref, st_ref, dy_ref, a_ref, dtb_ref, d_ref, _, dxbc_ref, ddt_ref, da_ref, ddtb_ref, dd_ref, e_ref):
        i = pl.program_id(0)

        @pl.when(i == 0)
        def _():
            e_ref[...] = jnp.zeros_like(e_ref)
            da_ref[...] = jnp.zeros_like(da_ref)
            ddtb_ref[...] = jnp.zeros_like(ddtb_ref)
            dd_ref[...] = jnp.zeros_like(dd_ref)

        xs, bs, cs, dt_raw, a, dtb, ds = _ssd_load(xbc_ref, dt_ref, a_ref, dtb_ref, d_ref)
        hs = tuple(st_ref[p] for p in range(N_PAIR))
        _, vjp = jax.vjp(_ssd_chunk, xs, bs, cs, dt_raw, hs, a, dtb, ds)
        dys = tuple(dy_ref[:, 128 * p:128 * (p + 1)] for p in range(N_PAIR))
        es = tuple(e_ref[p] for p in range(N_PAIR))
        dxs, dbs, dcs, ddt, dhs, da, ddtb, dds = vjp((dys, es))
        for p in range(N_PAIR):
            dxbc_ref[:, 128 * p:128 * (p + 1)] = dxs[p]
            e_ref[p] = dhs[p]
            dd_ref[:, 128 * p:128 * (p + 1)] += dds[p]
        for g in range(2):
            dxbc_ref[:, D + 128 * g:D + 128 * (g + 1)] = dbs[g]
            dxbc_ref[:, D + 256 + 128 * g:D + 256 + 128 * (g + 1)] = dcs[g]
        ddt_ref[:, :DT_PAD] = ddt.astype(BF16)
        ddt_ref[:, DT_PAD:] = jnp.zeros((SSD_CHUNK, OFF_QKV - OFF_DT - DT_PAD), BF16)
        da_ref[...] += da
        ddtb_ref[...] += ddtb

    dt_w = OFF_QKV - OFF_DT
    return pl.pallas_call(
        body, grid=(nc,),
        in_specs=[pl.BlockSpec((SSD_CHUNK, SSD_XBC), lambda i: (rc(i), 0)),
                  pl.BlockSpec((SSD_CHUNK, DT_PAD), lambda i: (rc(i), OFF_DT // DT_PAD)),
                  pl.BlockSpec((None, N_PAIR, 128, 128), lambda i: (rc(i), 0, 0, 0)),
                  pl.BlockSpec((SSD_CHUNK, D), lambda i: (rc(i), 0)),
                  prow(128), prow(128), prow(D), pl.BlockSpec(memory_space=pl.ANY)],
        out_specs=[pl.BlockSpec((SSD_CHUNK, SSD_XBC), lambda i: (rc(i), 0)),
                   pl.BlockSpec((SSD_CHUNK, dt_w), lambda i: (rc(i), OFF_DT // dt_w)),
                   prow(128), prow(128), prow(D)],
        out_shape=[jax.ShapeDtypeStruct((S, SSD_XBC), F32), jax.ShapeDtypeStruct((S, NP), BF16),
                   jax.ShapeDtypeStruct((1, 128), F32), jax.ShapeDtypeStruct((1, 128), F32),
                   jax.ShapeDtypeStruct((1, D), F32)],
        input_output_aliases={7: 1},
        scratch_shapes=[pltpu.VMEM((N_PAIR, 128, 128), F32)], name=name,
        compiler_params=_cparams(("arbitrary",)))(xbc_c, proj, states, dy, a_row, dtb_row, d_exp, dproj)


def _exchange(name, arrays, scatter):
    n = len(arrays)
    flips = [(dx, dy, dc) for dx in (0, 1) for dy in (0, 1) for dc in (0, 1) if dx or dy or dc]

    def body(*refs):
        ins, outs = refs[:n], refs[n:2 * n]
        send_sems, recv_sems, loc_sems = refs[2 * n:]
        x, y, c = lax.axis_index("x"), lax.axis_index("y"), lax.axis_index("c")
        me = 4 * x + 2 * y + c
        peers = []
        for dx, dy, dc in flips:
            px, py, pc = (1 - x if dx else x), (1 - y if dy else y), (1 - c if dc else c)
            peers.append(((px, py, pc), 4 * px + 2 * py + pc))

        def remote(k, j, landed_from):
            dev, pid = peers[j]
            src = ins[k].at[pid] if scatter else ins[k]
            return pltpu.make_async_remote_copy(
                src_ref=src, dst_ref=outs[k].at[landed_from], send_sem=send_sems.at[k, j], recv_sem=recv_sems.at[k, j],
                device_id=dev, device_id_type=pl.DeviceIdType.MESH)

        local = [pltpu.make_async_copy(ins[k].at[me] if scatter else ins[k], outs[k].at[me], loc_sems.at[k])
                 for k in range(n)]
        for cp in local:
            cp.start()
        for k in range(n):
            for j in range(len(flips)):
                remote(k, j, me).start()
        for cp in local:
            cp.wait()
        for k in range(n):
            for j in range(len(flips)):
                remote(k, j, me).wait_send()
                remote(k, j, peers[j][1]).wait_recv()

    hbm = pl.BlockSpec(memory_space=pltpu.HBM)
    out_shape = [jax.ShapeDtypeStruct(a.shape if scatter else (N_DEV,) + a.shape, a.dtype) for a in arrays]
    res = pl.pallas_call(
        body, in_specs=[hbm] * n, out_specs=[hbm] * n, out_shape=out_shape, name=name,
        scratch_shapes=[pltpu.SemaphoreType.DMA((n, len(flips))), pltpu.SemaphoreType.DMA((n, len(flips))),
                        pltpu.SemaphoreType.DMA((n,))])(*arrays)
    return list(res)


def _gather_chip_once(name, block):
    def body(x_ref, out_ref, send_sems, recv_sems, loc_sem):
        x, y, c = lax.axis_index("x"), lax.axis_index("y"), lax.axis_index("c")
        me, sibling = (x, y, c), (x, y, 1 - c)
        chips = [(1 - x, y), (x, 1 - y), (1 - x, 1 - y)]

        def slot(px, py, pc):
            return out_ref.at[4 * px + 2 * py + pc]

        def copy(k, blk, to, src=None):
            return pltpu.make_async_remote_copy(
                src_ref=slot(*blk) if src is None else src, dst_ref=slot(*blk), send_sem=send_sems.at[k],
                recv_sem=recv_sems.at[k], device_id=to, device_id_type=pl.DeviceIdType.MESH)

        mine = pltpu.make_async_copy(x_ref, slot(*me), loc_sem)
        mine.start()
        first = [copy(0, me, sibling, src=x_ref)] + [copy(1 + j, me, (*chip, c), src=x_ref) for j, chip in enumerate(chips)]
        for cp in first:
            cp.start()
        passed = [copy(4 + j, (*chip, c), sibling) for j, chip in enumerate(chips)]
        for j, chip in enumerate(chips):
            copy(1 + j, (*chip, c), me).wait_recv()
            passed[j].start()
        copy(0, sibling, me).wait_recv()
        for j, chip in enumerate(chips):
            copy(4 + j, (*chip, 1 - c), me).wait_recv()
        for cp in first + passed:
            cp.wait_send()
        mine.wait()

    hbm = pl.BlockSpec(memory_space=pltpu.HBM)
    return pl.pallas_call(
        body, in_specs=[hbm], out_specs=hbm, out_shape=jax.ShapeDtypeStruct((N_DEV,) + block.shape, block.dtype), name=name,
        scratch_shapes=[pltpu.SemaphoreType.DMA((N_DEV - 1,)), pltpu.SemaphoreType.DMA((N_DEV - 1,)),
                        pltpu.SemaphoreType.DMA(())])(block)


def _peer_copies(ins, lands, send_sems, recv_sems, loc_sems, scatter):
    n = len(ins)
    flips = [(dx, dy, dc) for dx in (0, 1) for dy in (0, 1) for dc in (0, 1) if dx or dy or dc]
    x, y, c = lax.axis_index("x"), lax.axis_index("y"), lax.axis_index("c")
    me = 4 * x + 2 * y + c
    peers = []
    for dx, dy, dc in flips:
        px, py, pc = (1 - x if dx else x), (1 - y if dy else y), (1 - c if dc else c)
        peers.append(((px, py, pc), 4 * px + 2 * py + pc))

    def remote(k, j, slot):
        dev, pid = peers[j]
        return pltpu.make_async_remote_copy(
            src_ref=ins[k].at[pid] if scatter else ins[k], dst_ref=lands[k].at[slot],
            send_sem=send_sems.at[k * N_FLIP + j], recv_sem=recv_sems.at[k * N_FLIP + j],
            device_id=dev, device_id_type=pl.DeviceIdType.MESH)

    local = [pltpu.make_async_copy(ins[k].at[me] if scatter else ins[k], lands[k].at[me], loc_sems.at[k])
             for k in range(n)]
    pairs = [(k, j) for k in range(n) for j in range(len(flips))]
    sent = lambda k, j: remote(k, j, me)
    landed = lambda k, j: remote(k, j, peers[j][1])
    return local, pairs, sent, landed


_HBM = pl.BlockSpec(memory_space=pltpu.HBM)
_SEM = pl.BlockSpec(memory_space=pltpu.SEMAPHORE)
N_FLIP = N_DEV - 1


def _exchange_start(name, arrays, scatter, after):
    n = len(arrays)
    arrays = [pltpu.with_memory_space_constraint(a, pltpu.HBM) for a in arrays]
    lands = [pltpu.with_memory_space_constraint(
        lax.empty(a.shape if scatter else (N_DEV,) + a.shape, a.dtype), pltpu.HBM) for a in arrays]

    def body(*refs):
        ins, lnd = refs[:n], refs[n:2 * n]
        send_sems, recv_sems, loc_sems = refs[2 * n + 1:2 * n + 4]
        token = refs[-1]
        local, pairs, sent, _ = _peer_copies(ins, lnd, send_sems, recv_sems, loc_sems, scatter)
        for cp in local:
            cp.start()
        for k, j in pairs:
            sent(k, j).start()
        token[...] = jnp.zeros_like(token)

    res = pl.pallas_call(
        body, name=name,
        in_specs=[_HBM] * (2 * n) + [pl.BlockSpec(memory_space=pl.ANY)],
        out_specs=[_SEM, _SEM, _SEM] + [_HBM] * (2 * n) + [pl.BlockSpec(memory_space=pltpu.VMEM)],
        out_shape=[pltpu.SemaphoreType.DMA((n * N_FLIP,)), pltpu.SemaphoreType.DMA((n * N_FLIP,)), pltpu.SemaphoreType.DMA((n,))]
        + [pltpu.HBM(a.shape, a.dtype) for a in arrays] + [pltpu.HBM(a.shape, a.dtype) for a in lands]
        + [jax.ShapeDtypeStruct((8, 128), F32)],
        input_output_aliases={k: 3 + k for k in range(2 * n)},
        compiler_params=pltpu.CompilerParams(has_side_effects=pltpu.SideEffectType.DATAFLOW_SIDE_EFFECTING),
    )(*arrays, *lands, after)
    return (res[:3], res[3:3 + n], res[3 + n:3 + 2 * n], scatter), res[-1]


def _exchange_wait(name, state, after):
    sems, ins_thru, lands_thru, scatter = state
    n = len(ins_thru)

    def body(*refs):
        ins, lnd = refs[:n], refs[n:2 * n]
        send_sems, recv_sems, loc_sems = refs[2 * n:2 * n + 3]
        local, pairs, sent, landed = _peer_copies(ins, lnd, send_sems, recv_sems, loc_sems, scatter)
        for cp in local:
            cp.wait()
        for k, j in pairs:
            sent(k, j).wait_send()
            landed(k, j).wait_recv()

    res = pl.pallas_call(
        body, name=name,
        in_specs=[_HBM] * (2 * n) + [_SEM, _SEM, _SEM] + [pl.BlockSpec(memory_space=pl.ANY)],
        out_specs=[_HBM] * (2 * n),
        out_shape=[pltpu.HBM(a.shape, a.dtype) for a in ins_thru] + [pltpu.HBM(a.shape, a.dtype) for a in lands_thru],
        input_output_aliases={k: k for k in range(2 * n)},
        compiler_params=pltpu.CompilerParams(has_side_effects=pltpu.SideEffectType.DATAFLOW_SIDE_EFFECTING),
    )(*ins_thru, *lands_thru, *sems, after)
    return list(res[n:])


def _adamw_fn(*vals):
    slots, (w, m, v) = vals[:N_DEV], vals[N_DEV:]
    g = slots[0].astype(F32)
    for s in slots[1:]:
        g = g + s.astype(F32)
    m2 = ADAM_B1 * m + (1.0 - ADAM_B1) * g
    v2 = ADAM_B2 * v + (1.0 - ADAM_B2) * (g * g)
    m_hat = m2 / (1.0 - ADAM_B1 ** ADAM_STEP)
    v_hat = v2 / (1.0 - ADAM_B2 ** ADAM_STEP)
    delta = -ADAM_LR * (m_hat / (jnp.sqrt(v_hat) + ADAM_EPS) + ADAM_WD * w)
    return (g, delta, m2, v2), ()


def _adamw(name, slots, w, m, v, first_row=0, prev=None):
    R, C = slots.shape[1:]
    tm = R if R <= 128 else _pick(R, 128 if C > D else 256, 8)
    rows = ([_rows(slots, C, lead=s) for s in range(N_DEV)]
            + [_rows(a, C, roff=first_row // tm) for a in (w, m, v)])
    return _tiled(name, _adamw_fn, (1, R // tm), tm, rows, [], [(w.shape[0], C, F32)] * 4,
                  out_roff=first_row // tm, prev_outs=prev)


def _bucket_onehots():
    out = []
    qi = jnp.arange(ATT_BLOCK)[:, None]
    kj = jnp.arange(ATT_BLOCK)[None, :]
    max_exact = REL_BUCKETS // 2
    for _, dil in ATT_GROUPS:
        parts = []
        for rel in (qi + ATT_BLOCK - kj, qi - kj):
            dist = jnp.clip(rel, 0, None) * dil
            nf = jnp.maximum(dist, 1).astype(F32)
            large = max_exact + (jnp.log(nf / max_exact) / math.log(REL_MAX_DISTANCE / max_exact)
                                 * (REL_BUCKETS - max_exact)).astype(jnp.int32)
            large = jnp.minimum(large, REL_BUCKETS - 1)
            bucket = jnp.where(dist < max_exact, dist, large)
            parts.append((bucket[:, :, None] == jnp.arange(REL_BUCKETS)[None, None, :]).astype(F32))
        out.append(jnp.stack(parts))
    return out


SHARDED = ("w_in", "w_a", "pool_w", "w_b", "ssd_conv_w", "w_c", "w_o", "ffn_w_up", "ffn_conv_w", "ffn_w_down")
MATMUL_WEIGHTS = ("w_in", "w_a", "pool_w", "w_b", "w_c", "w_o", "ffn_w_up", "ffn_w_down")
ROW_SHARDED = ("w_b", "w_c", "w_o", "ffn_w_down")
W_IN_SEGS = tuple(
    (which * ATT_W + unit * 128, which * ATT_W + (unit + 1) * 128, OFF_QKV + unit * QKV_W + which * 128)
    for unit in range(9) for which in range(3)
) + ((3456, 4480, OFF_POOL), (4480, 5504, OFF_Z), (5504, 7040, OFF_XBC), (7040, 7056, OFF_DT), (7056, IN_WIDTH, OFF_GATE))
FFN_SEGS = tuple((h * D_FF + j * 128, h * D_FF + (j + 1) * 128, j * 256 + h * 128)
                 for j in range(D_FF // 128) for h in range(2))
COL_SHARDED = {
    "w_in": (IN_WIDTH // N_DEV, W_IN_SEGS, ((OFF_DT + SSD_HEADS, OFF_QKV),), NP),
    "w_a": (D // N_DEV, ((0, D, 0),), (), D),
    "ffn_w_up": (2 * D_FF // N_DEV, FFN_SEGS, (), 2 * D_FF),
    "ssd_conv_w": (SSD_XBC // N_DEV, ((0, SSD_XBC, 0),), (), SSD_XBC),
    "ffn_conv_w": (2 * D_FF // N_DEV, FFN_SEGS, (), 2 * D_FF),
}
REPLICATED = ("rel_bias", "ln1_g", "b_gate", "pool_scale", "ssd_conv_b", "ssd_dt_bias", "ssd_a_log", "ssd_d",
              "ssd_norm_w", "ln2_g", "ffn_conv_b", "final_g")
WEIGHTS = ("rel_bias", "ln1_g", "w_in", "b_gate", "w_a", "pool_w", "pool_scale", "w_b", "ssd_conv_w", "ssd_conv_b",
           "ssd_dt_bias", "ssd_a_log", "ssd_d", "ssd_norm_w", "w_c", "w_o", "ln2_g", "ffn_w_up", "ffn_conv_w",
           "ffn_conv_b", "ffn_w_down", "final_g")


def _local_weight(name, n, blocks):
    if n in COL_SHARDED:
        c, segs, zeros, width = COL_SHARDED[n]
        return _col_assemble(name, blocks, _seg_copies(segs, c), zeros, width)
    if n in ROW_SHARDED:
        return blocks.reshape(-1, blocks.shape[-1])
    return blocks


def _device_blocks(name, n, g):
    if n in COL_SHARDED:
        c, segs, _, _ = COL_SHARDED[n]
        return _col_split(name, g, _seg_copies(segs, c), c, BF16)
    if n in ROW_SHARDED:
        return g.reshape(N_DEV, g.shape[0] // N_DEV, g.shape[1]).astype(BF16)
    return g.astype(BF16)


def _row(v, n=None):
    v = v.reshape(1, -1)
    if n is not None and v.shape[1] < n:
        v = jnp.pad(v, ((0, 0), (0, n - v.shape[1])))
    return v


RT = 512


def _row_call(name, fn, cw, ncol, rows, params, outs, accs=(), into=None):
    return _tiled(name, fn, (ncol, S // RT), RT, rows, params, [(S, cw, dt) for dt in outs], accs, into=into)


def _col_call(name, fn, tc, ncol, rows, params, outs, accs=(), into=None):
    return _tiled(name, fn, (ncol, 1), S, rows, params, [(S, tc, dt) for dt in outs], accs, into=into)


def _fwd_only(fn):
    return lambda *a: (fn(*a), ())


def _layer_fwd(i, x, W, P, bias_tabs, late=None, late_ffn=None):
    sv = {"x": x}
    (u,) = _row_call(f"ln1_f{i}", _fwd_only(_rmsnorm_fn), D, 1, [_rows(x, D)], [_p_row(P["ln1_g"], D)], [BF16])
    proj = _matmul(f"inproj_f{i}", u, W["w_in"], "nn")
    sv["u"], sv["proj"] = u, proj

    os_, ls_ = [], []
    for gi in range(len(ATT_GROUPS)):
        o, lse = _att_fwd(f"att_f{i}_{gi}", proj, gi, bias_tabs[gi][0], bias_tabs[gi][1])
        os_.append(o)
        ls_.append(lse)
    sv["att_o"], sv["att_l"] = os_, ls_
    (att,) = _row_call(f"attmerge_f{i}", _fwd_only(_att_merge_fn), ATT_GW, 1,
                       [_rows(t, ATT_GW) for t in os_ + ls_], [], [BF16])
    if late is not None:
        W2, P2 = late(att)
        W.update(W2)
        P.update(P2)
    y_a = _matmul(f"wa_f{i}", att, W["w_a"], "nn", out_dtype=BF16)
    sv["att"], sv["y_a"] = att, y_a

    pool_params = [(W["pool_w"], (N_DEV, None, 32, 256), lambda j, i_: (0, j, 0, 0)), _p_row(P["pool_scale"], 256)]
    (yb_pre,) = _col_call(f"pool_f{i}", _fwd_only(_pool_fn), 256, 4, [_rows(proj, 256, OFF_POOL // 256)],
                          pool_params, [BF16])
    y_b = _matmul(f"wb_f{i}", yb_pre, W["w_b"], "nn", out_dtype=BF16)
    sv["yb_pre"], sv["y_b"] = yb_pre, y_b

    conv_params = [_p_row(P["ssd_conv_w"][k], 128) for k in range(4)] + [_p_row(P["ssd_conv_b"], 128)]
    (xbc_c,) = _col_call(f"ssdconv_f{i}", _fwd_only(_ssd_conv_fn), 128, SSD_XBC // 128,
                         [_rows(proj, 128, OFF_XBC // 128)], conv_params, [F32])
    y_ssd, states = _ssd_fwd(f"ssd_f{i}", xbc_c, proj, P["a_row"], P["dtb_row"], P["d_exp"])
    (yc_pre,) = _row_call(f"ssdnorm_f{i}", _fwd_only(_gated_norm_fn), 512, 2,
                          [_rows(y_ssd, 512), _rows(proj, 512, OFF_Z // 512)], [_p_row(P["ssd_norm_w"], 512)], [BF16])
    y_c = _matmul(f"wc_f{i}", yc_pre, W["w_c"], "nn", out_dtype=BF16)
    sv["xbc_c"], sv["states"], sv["y_ssd"], sv["yc_pre"], sv["y_c"] = xbc_c, states, y_ssd, yc_pre, y_c

    gate_rows = [_rows(proj, D, k) for k in range(3)] + [_rows(t, D) for t in (y_a, y_b, y_c)]
    gate_params = [_p_row(P["b_gate"], D, k) for k in range(3)]
    (merged,) = _row_call(f"gate_f{i}", _fwd_only(_gate_merge_fn), D, 1, gate_rows, gate_params, [BF16])
    x1 = _matmul(f"wo_f{i}", merged, W["w_o"], "nn", add=x)
    sv["merged"], sv["x1"] = merged, x1

    if late_ffn is not None:
        W2, P2 = late_ffn(x1)
        W.update(W2)
        P.update(P2)
    (u2,) = _row_call(f"ln2_f{i}", _fwd_only(_rmsnorm_fn), D, 1, [_rows(x1, D)], [_p_row(P["ln2_g"], D)], [BF16])
    up = _matmul(f"up_f{i}", u2, W["ffn_w_up"], "nn", out_dtype=BF16)
    (act,) = _col_call(f"ffnact_f{i}", lambda *a: (_ffn_act_fn(*_ffn_halves(a)), ()), 128, D_FF // 128,
                       [_rows(up, 256)], _ffn_params(P), [BF16])
    x2 = _matmul(f"down_f{i}", act, W["ffn_w_down"], "nn", add=x1)
    sv["u2"], sv["up"], sv["act"] = u2, up, act
    return x2, sv


def _ffn_params(P):
    return [_p_row(P["ffn_conv_w"][k], 256) for k in range(3)] + [_p_row(_interleave_ffn(P["ffn_conv_b"]), 256)]


def _ffn_halves(vals):
    (xa, xv), (a0, v0), (a1, v1), (a2, v2), (ab, vb) = ((t[:, :128], t[:, 128:]) for t in vals)
    return xa, xv, a0, a1, a2, ab, v0, v1, v2, vb


def _interleave_ffn(row):
    return row.reshape(2, D_FF // 128, 128).transpose(1, 0, 2).reshape(1, 2 * D_FF)


def _deinterleave_ffn(row):
    return row.reshape(D_FF // 128, 2, 128).transpose(1, 0, 2).reshape(1, 2 * D_FF)


def _layer_bwd(i, dx2, sv, W, P, bias_tabs, onehots, on_sharded_grads):
    G = {}
    x, proj, x1 = sv["x"], sv["proj"], sv["x1"]

    dact = _matmul(f"down_bx{i}", dx2, W["ffn_w_down"], "nt", out_dtype=BF16)
    G["ffn_w_down"] = _matmul(f"down_bw{i}", sv["act"], dx2, "tn", out_dtype=BF16)
    nb = D_FF // 128
    up = sv["up"]
    def ffn_bwd(x2, t0, t1, t2, b2, dact_):
        (dxa, dxv), (a0, a1, a2, ab, v0, v1, v2, vb) = _with_vjp(_ffn_act_fn, 10, (0, 1), tuple(range(2, 10)))(
            *_ffn_halves((x2, t0, t1, t2, b2)), dact_)
        pair = lambda a, v: jnp.concatenate([a, v], axis=1)
        return (pair(dxa, dxv),), (pair(a0, v0), pair(a1, v1), pair(a2, v2), pair(ab, vb))

    dup, t0, t1, t2, tb = _col_call(
        f"ffnact_b{i}", ffn_bwd, 256, nb, [_rows(up, 256)], _ffn_params(P) + [_rows_as_param(dact, 128)],
        [BF16], [_a_row(2 * D_FF, 256)] * 4)
    G["ffn_conv_w"] = jnp.concatenate([t0, t1, t2], axis=0)
    G["ffn_conv_b"] = _deinterleave_ffn(tb)[0]
    du2 = _matmul(f"up_bx{i}", dup, W["ffn_w_up"], "nt")
    G["ffn_w_up"] = _matmul(f"up_bw{i}", sv["u2"], dup, "tn", out_dtype=BF16)

    def norm_bwd(x_, g_, du_, dres):
        (dxn,), (dg,) = _with_vjp(_rmsnorm_fn, 2, (0,), (1,))(x_, g_, du_)
        return (dxn + dres,), (dg,)

    (dx1,), (G["ln2_g"],) = _split_res(_row_call(
        f"ln2_b{i}", lambda x_, du_, dres, g_: norm_bwd(x_, g_, du_, dres), D, 1,
        [_rows(x1, D), _rows(du2, D), _rows(dx2, D)], [_p_row(P["ln2_g"], D)], [F32], [_a_row(D, D)]), 1)

    dmerged = _matmul(f"wo_bx{i}", dx1, W["w_o"], "nt", out_dtype=BF16)
    G["w_o"] = _matmul(f"wo_bw{i}", sv["merged"], dx1, "tn", out_dtype=BF16)
    def gate_bwd(g_, y_, dm, b_):
        return _with_vjp(lambda g, y, b: (jax.nn.sigmoid(g + b) * y,), 3, (0, 1), (2,))(g_, y_, b_, dm)

    dproj, dys, dbs = None, [], []
    for k, t in enumerate(("y_a", "y_b", "y_c")):
        dproj, dy_k, db_k = _row_call(
            f"gate_b{i}_{k}", gate_bwd, D, 1, [_rows(proj, D, k), _rows(sv[t], D), _rows(dmerged, D)],
            [_p_row(P["b_gate"], D, k)], [BF16, BF16], [_a_row(D, D)], into={0: (dproj, k, NP)})
        dys.append(dy_k)
        dbs.append(db_k)
    dya, dyb, dyc = dys
    G["b_gate"] = jnp.concatenate(dbs, axis=1)[0]

    dyc_pre = _matmul(f"wc_bx{i}", dyc, W["w_c"], "nt", out_dtype=BF16)
    G["w_c"] = _matmul(f"wc_bw{i}", sv["yc_pre"], dyc, "tn", out_dtype=BF16)

    def gnorm_bwd(y_, z_, dy_, w_):
        return _with_vjp(_gated_norm_fn, 3, (0, 1), (2,))(y_, z_, w_, dy_)

    dy_ssd, dproj, dnw = _row_call(
        f"ssdnorm_b{i}", gnorm_bwd, 512, 2,
        [_rows(sv["y_ssd"], 512), _rows(proj, 512, OFF_Z // 512), _rows(dyc_pre, 512)],
        [_p_row(P["ssd_norm_w"], 512)], [F32, BF16], [_a_row(D, 512)], into={1: (dproj, OFF_Z // 512, NP)})
    G["ssd_norm_w"] = dnw[0]
    dxbc_c, dproj, da_row, ddtb_row, dd_exp = _ssd_bwd(f"ssd_b{i}", sv["xbc_c"], proj, sv["states"], dy_ssd,
                                                       P["a_row"], P["dtb_row"], P["d_exp"], dproj)
    a_vec = P["a_row"][0, :SSD_HEADS]
    G["ssd_a_log"] = da_row[0, :SSD_HEADS] * a_vec
    G["ssd_dt_bias"] = ddtb_row[0, :SSD_HEADS]
    G["ssd_d"] = dd_exp.reshape(SSD_HEADS, HEAD_DIM).sum(axis=1)
    conv_params = [_p_row(P["ssd_conv_w"][k], 128) for k in range(4)] + [_p_row(P["ssd_conv_b"], 128)]

    def conv_bwd(x_, dy_, w0, w1, w2, w3, b_):
        return _with_vjp(_ssd_conv_fn, 6, (0,), (1, 2, 3, 4, 5))(x_, w0, w1, w2, w3, b_, dy_)

    dproj, c0, c1, c2, c3, cb = _col_call(
        f"ssdconv_b{i}", conv_bwd, 128, SSD_XBC // 128, [_rows(proj, 128, OFF_XBC // 128), _rows(dxbc_c, 128)],
        conv_params, [BF16], [_a_row(SSD_XBC, 128)] * 5, into={0: (dproj, OFF_XBC // 128, NP)})
    G["ssd_conv_w"] = jnp.concatenate([c0, c1, c2, c3], axis=0)
    G["ssd_conv_b"] = cb[0]

    dyb_pre = _matmul(f"wb_bx{i}", dyb, W["w_b"], "nt", out_dtype=BF16)
    G["w_b"] = _matmul(f"wb_bw{i}", sv["yb_pre"], dyb, "tn", out_dtype=BF16)
    pool_params = [(W["pool_w"], (N_DEV, None, 32, 256), lambda j, i_: (0, j, 0, 0)), _p_row(P["pool_scale"], 256)]

    def pool_bwd(x_, dy_, wg, sc):
        return _with_vjp(_pool_fn, 3, (0,), (1, 2))(x_, wg.astype(F32), sc, dy_)

    dproj, dwg, dsc = _col_call(
        f"pool_b{i}", pool_bwd, 256, 4, [_rows(proj, 256, OFF_POOL // 256), _rows(dyb_pre, 256)], pool_params, [BF16],
        [((N_DEV, 4, 32, 256), (N_DEV, None, 32, 256), lambda j, i_: (0, j, 0, 0)), _a_row(D, 256)],
        into={0: (dproj, OFF_POOL // 256, NP)})
    G["pool_w"] = dwg
    G["pool_scale"] = dsc[0]

    datt = _matmul(f"wa_bx{i}", dya, W["w_a"], "nt", out_dtype=BF16)
    G["w_a"] = _matmul(f"wa_bw{i}", sv["att"], dya, "tn", out_dtype=BF16)

    def merge_bwd(o0, o1, o2, l0, l1, l2, da_):
        return _with_vjp(_att_merge_fn, 6, (0, 1, 2, 3, 4, 5), ())(o0, o1, o2, l0, l1, l2, da_)

    dol = _row_call(f"attmerge_b{i}", merge_bwd, ATT_GW, 1,
                    [_rows(t, ATT_GW) for t in sv["att_o"] + sv["att_l"]] + [_rows(datt, ATT_GW)], [], [F32] * 6)
    g_rel = jnp.zeros((REL_BUCKETS, 18), F32)
    for gi in range(len(ATT_GROUPS)):
        dproj, gbp, gbc = _att_bwd(f"att_b{i}_{gi}", proj, gi, bias_tabs[gi][0], bias_tabs[gi][1],
                                   dol[gi], dol[3 + gi], dproj)
        oh = onehots[gi]
        gt = (jnp.einsum("hqk,qkb->bh", gbp, oh[0], precision=lax.Precision.HIGHEST)
              + jnp.einsum("hqk,qkb->bh", gbc, oh[1], precision=lax.Precision.HIGHEST))
        g_rel = g_rel.at[:, gi * 6:(gi + 1) * 6].add(gt)
    G["rel_bias"] = g_rel

    du = _matmul(f"inproj_bx{i}", dproj, W["w_in"], "nt")
    G["w_in"] = _matmul(f"inproj_bw{i}", sv["u"], dproj, "tn", out_dtype=BF16)
    ln1_g = P["ln1_g"] + on_sharded_grads(G)
    (dx,), (G["ln1_g"],) = _split_res(_row_call(
        f"ln1_b{i}", lambda x_, du_, dres, g_: norm_bwd(x_, g_, du_, dres), D, 1,
        [_rows(x, D), _rows(du, D), _rows(dx1, D)], [_p_row(ln1_g, D)], [F32], [_a_row(D, D)]), 1)
    G["ln1_g"] = G["ln1_g"][0]
    G["ln2_g"] = G["ln2_g"][0]
    return dx, G


def _rows_as_param(arr, cw):
    return (arr, (arr.shape[0], cw), lambda j, i: (0, j))


def _split_res(res, n_out):
    return tuple(res[:n_out]), tuple(res[n_out:])


def kernel(x, rel_bias, ln1_g, w_in, b_gate, w_a, pool_w, pool_scale, w_b, ssd_conv_w, ssd_conv_b, ssd_dt_bias, ssd_a_log, ssd_d, ssd_norm_w, w_c, w_o, ln2_g, ffn_w_up, ffn_conv_w, ffn_conv_b, ffn_w_down, final_g, loss_target, m_rel_bias, m_ln1_g, m_w_in, m_b_gate, m_w_a, m_pool_w, m_pool_scale, m_w_b, m_ssd_conv_w, m_ssd_conv_b, m_ssd_dt_bias, m_ssd_a_log, m_ssd_d, m_ssd_norm_w, m_w_c, m_w_o, m_ln2_g, m_ffn_w_up, m_ffn_conv_w, m_ffn_conv_b, m_ffn_w_down, m_final_g, v_rel_bias, v_ln1_g, v_w_in, v_b_gate, v_w_a, v_pool_w, v_pool_scale, v_w_b, v_ssd_conv_w, v_ssd_conv_b, v_ssd_dt_bias, v_ssd_a_log, v_ssd_d, v_ssd_norm_w, v_w_c, v_w_o, v_ln2_g, v_ffn_w_up, v_ffn_conv_w, v_ffn_conv_b, v_ffn_w_down, v_final_g):
    args = locals()
    wts = {n: args[n] for n in WEIGHTS}
    mom = {n: args["m_" + n] for n in WEIGHTS}
    var = {n: args["v_" + n] for n in WEIGHTS}
    names = list(SHARDED)

    onehots = _bucket_onehots()
    bias_tabs = []
    for gi in range(3):
        tab = rel_bias[:, gi * 6:(gi + 1) * 6]
        b = jnp.einsum("pqkb,bh->phqk", onehots[gi], tab, precision=lax.Precision.HIGHEST)
        bias_tabs.append((b[0], b[1]))

    def gather_start(tag, i, which, after):
        shards = [wts[n][i].astype(BF16) if n in MATMUL_WEIGHTS else wts[n][i] for n in which]
        return _exchange_start(f"gather_start{tag}", shards, False, after)

    def layer_params(i, which, landed):
        full = {n: _local_weight(f"local_{n}{i}", n, g) for n, g in zip(which, landed)}
        W = {n: full[n] for n in which if n in MATMUL_WEIGHTS}
        P = {n: [_row(full[n][k]) for k in range(full[n].shape[0])] for n in ("ssd_conv_w", "ffn_conv_w") if n in full}
        return W, P

    def replicated_params(i):
        return {"ln1_g": _row(ln1_g[i]), "ln2_g": _row(ln2_g[i]), "b_gate": _row(b_gate[i]),
                "pool_scale": _row(pool_scale[i]), "ssd_conv_b": _row(ssd_conv_b[i]),
                "ssd_norm_w": _row(ssd_norm_w[i]), "ffn_conv_b": _row(ffn_conv_b[i]),
                "a_row": _row(-jnp.exp(ssd_a_log[i]), 128), "dtb_row": _row(ssd_dt_bias[i], 128),
                "d_exp": _row(jnp.repeat(ssd_d[i], HEAD_DIM))}

    h = x.reshape(S, D)
    saved, Ws, Ps = [], [], []
    ffn = ["ffn_w_up", "ffn_conv_w", "ffn_w_down"]
    core = [n for n in names if n not in ffn]
    first, rest = ["w_in"], [n for n in core if n != "w_in"]
    landed_first = [_gather_chip_once("gather_w_in0", w_in[0].astype(BF16))]
    state_rest, token = gather_start("0r", 0, rest, landed_first[0])
    landed = None

    def late0(att):
        return layer_params(0, rest, _exchange_wait("gather_wait0r", state_rest, att))

    for i in range(DEPTH):
        P = replicated_params(i)
        W, P1 = layer_params(0, first, landed_first) if i == 0 else layer_params(i, core, landed)
        P.update(P1)
        state_ffn, tok = gather_start(f"{i}f", i, ffn, token if i == 0 else landed[0])
        token = tok if i > 0 else token + tok
        if i + 1 < DEPTH:
            state_next, tok = gather_start(f"{i + 1}c", i + 1, core, tok)
            token = token + tok
        P["ln1_g"] = P["ln1_g"] + token[0, 0]

        def late_ffn(x1, i=i, state_ffn=state_ffn):
            return layer_params(i, ffn, _exchange_wait(f"gather_wait{i}f", state_ffn, x1))

        h, sv = _layer_fwd(i, h, W, P, bias_tabs, late0 if i == 0 else None, late_ffn)
        Ws.append(W)
        Ps.append(dict(P, ln1_g=_row(ln1_g[i])))
        saved.append(sv)
        if i + 1 < DEPTH:
            landed = _exchange_wait(f"gather_wait{i + 1}c", state_next, h)

    def loss_bwd(x_, t_, g_):
        lval, vjp = jax.vjp(_loss_fn, x_, t_, g_)
        dx_, _, dg_ = vjp(jnp.ones_like(lval))
        return (dx_,), (dg_, jnp.broadcast_to(lval, (1, 128)))

    dh, g_final, loss_part = _row_call("loss", loss_bwd, D, 1, [_rows(h, D), _rows(loss_target.reshape(S, D), D)],
                                       [_p_row(_row(final_g), D)], [F32], [_a_row(D, D), _a_row(128, 128)])
    loss = lax.psum(loss_part[0, 0], MESH_AXES)

    grads = {n: [None] * DEPTH for n in WEIGHTS if n not in ("rel_bias", "final_g")}
    g_rel = jnp.zeros((REL_BUCKETS, 18), F32)
    slots = [None] * DEPTH
    pending = None
    for i in reversed(range(DEPTH)):
        started = {}

        def on_sharded_grads(G, i=i, started=started):
            parts = [_device_blocks(f"blocks_{n}{i}", n, G[n]) for n in names]
            started["state"], token = _exchange_start(f"scatter_start{i}", parts, True, G["b_gate"])
            return token[0, 0]

        dh, G = _layer_bwd(i, dh, saved[i], Ws[i], Ps[i], bias_tabs, onehots, on_sharded_grads)
        if pending is not None:
            j, st = pending
            slots[j] = _exchange_wait(f"scatter_wait{j}", st, dh)
        pending = (i, started["state"])
        g_rel = g_rel + G.pop("rel_bias")
        for n, g in G.items():
            grads[n][i] = g
    grad_x = dh.reshape(1, S, D)
    local = {n: jnp.stack(grads[n]) for n in grads if n not in SHARDED}
    local["rel_bias"] = g_rel
    local["final_g"] = g_final[0]
    out = {}

    def pack(d):
        flat = jnp.concatenate([d[n].reshape(-1).astype(F32) for n in REPLICATED])
        rows = -(-flat.shape[0] // (8 * 128)) * 8
        return jnp.pad(flat, (0, rows * 128 - flat.shape[0])).reshape(rows, 128)

    (rep_slots,) = _exchange("gather_small_grads", [pack(local)], scatter=False)
    rep = _adamw("adamw_small", rep_slots, pack(wts), pack(mom), pack(var))
    off = 0
    for n in REPLICATED:
        sz = int(np.prod(wts[n].shape))
        out[n] = [t.reshape(-1)[off:off + sz].reshape(wts[n].shape) for t in rep]
        off += sz

    def flat2(n):
        shp = wts[n].shape
        r, c = int(np.prod(shp[:-1])), shp[-1]
        return r, c, wts[n].reshape(r, c), mom[n].reshape(r, c), var[n].reshape(r, c)

    chain = {}
    done = rep[0][0, 0]
    for k, n in enumerate(names):
        if n in MATMUL_WEIGHTS:
            r, c, w2, m2, v2 = flat2(n)
            res = None
            for i in (3, 2, 1):
                res = _adamw(f"adamw_{n}{i}", slots[i][k].reshape(N_DEV, r // DEPTH, c), w2, m2, v2,
                             first_row=i * (r // DEPTH), prev=res)
            chain[n] = res
            done = done + res[0][-1, 0]
    slots[0] = _exchange_wait("scatter_wait0", pending[1], done.reshape(1, 1))
    for k, n in enumerate(names):
        r, c, w2, m2, v2 = flat2(n)
        if n in MATMUL_WEIGHTS:
            res = _adamw(f"adamw_{n}0", slots[0][k].reshape(N_DEV, r // DEPTH, c), w2, m2, v2, first_row=0, prev=chain[n])
        else:
            stacked = jnp.stack([slots[i][k] for i in range(DEPTH)], axis=1)
            res = _adamw("adamw_" + n, stacked.reshape(N_DEV, r, c), w2, m2, v2)
        out[n] = [t.reshape(wts[n].shape) for t in res]

    return (loss, grad_x, *[out[n][0] for n in WEIGHTS], *[out[n][1] for n in WEIGHTS],
            *[out[n][2] for n in WEIGHTS], *[out[n][3] for n in WEIGHTS])
```

```python
import functools
import math

import numpy as np
import jax
import jax.numpy as jnp
from jax import lax
from jax.experimental import pallas as pl
from jax.experimental.pallas import tpu as pltpu

F32 = jnp.float32
BF16 = jnp.bfloat16

N_DEV = 8
MESH_AXES = ("x", "y", "c")
S = 4096
D = 1024
DEPTH = 4
HEAD_DIM = 64
ATT_W = 1152
ATT_GW = 384
ATT_GROUPS = ((128, 1), (512, 4), (2048, 16))
ATT_BLOCK = 128
REL_BUCKETS = 32
REL_MAX_DISTANCE = 2048
POOL_WINDOWS = (2, 4, 8, 16)
SSD_HEADS = 16
SSD_CHUNK = 128
SSD_XBC = 1536
D_FF = 2816
IN_WIDTH = 10128
EPS = 1e-6
NEG = -1e30

OFF_GATE, OFF_POOL, OFF_Z, OFF_XBC, OFF_DT, OFF_QKV = 0, 3072, 4096, 5120, 6656, 6912
NP = 10368
DT_PAD = 128
QKV_W = 3 * 2 * HEAD_DIM

ADAM_LR, ADAM_B1, ADAM_B2, ADAM_EPS, ADAM_WD, ADAM_STEP = 0.001, 0.9, 0.999, 1e-08, 0.01, 10

VMEM_LIMIT = 52 * 1024 * 1024


def _cparams(sem=None):
    return pltpu.CompilerParams(dimension_semantics=sem, vmem_limit_bytes=VMEM_LIMIT)


def _dot(a, b, ca, cb):
    return lax.dot_general(a.astype(BF16), b.astype(BF16), (((ca,), (cb,)), ((), ())), preferred_element_type=F32)


@jax.custom_vjp
def _mm(a, b):
    return _dot(a, b, 1, 0)


def _mm_fwd(a, b):
    return _mm(a, b), (a, b)


def _mm_bwd(res, g):
    a, b = res
    return _dot(g, b, 1, 1).astype(a.dtype), _dot(a, g, 0, 0).astype(b.dtype)


_mm.defvjp(_mm_fwd, _mm_bwd)


@jax.custom_vjp
def _mm_nt(a, b):
    return _dot(a, b, 1, 1)


def _mm_nt_fwd(a, b):
    return _mm_nt(a, b), (a, b)


def _mm_nt_bwd(res, g):
    a, b = res
    return _dot(g, b, 1, 0).astype(a.dtype), _dot(g, a, 0, 0).astype(b.dtype)


_mm_nt.defvjp(_mm_nt_fwd, _mm_nt_bwd)


@jax.custom_vjp
def _mm_tn(a, b):
    return _dot(a, b, 0, 0)


def _mm_tn_fwd(a, b):
    return _mm_tn(a, b), (a, b)


def _mm_tn_bwd(res, g):
    a, b = res
    return _dot(b, g, 1, 1).astype(a.dtype), _dot(a, g, 1, 0).astype(b.dtype)


_mm_tn.defvjp(_mm_tn_fwd, _mm_tn_bwd)


def _shift_impl(x, j):
    n = x.shape[0]
    if j == 0:
        return x
    r = pltpu.roll(x, j % n, axis=0)
    t = lax.broadcasted_iota(jnp.int32, x.shape, 0)
    mask = (t >= j) if j > 0 else (t < n + j)
    return jnp.where(mask, r, 0.0)


@functools.partial(jax.custom_vjp, nondiff_argnums=(1,))
def _shift(x, j):
    return _shift_impl(x, j)


_shift.defvjp(lambda x, j: (_shift_impl(x, j), None), lambda j, _, g: (_shift_impl(g, -j),))


def _tri(lower):
    r = lax.broadcasted_iota(jnp.int32, (SSD_CHUNK, SSD_CHUNK), 0)
    c = lax.broadcasted_iota(jnp.int32, (SSD_CHUNK, SSD_CHUNK), 1)
    return (r >= c) if lower else (r <= c)


def _dot_hi(a, b):
    return lax.dot_general(a, b, (((1,), (0,)), ((), ())), precision=lax.Precision.HIGHEST,
                           preferred_element_type=F32)


@jax.custom_vjp
def _cumsum_rows(a):
    return _dot_hi(_tri(True).astype(F32), a)


_cumsum_rows.defvjp(lambda a: (_cumsum_rows(a), None), lambda _, g: (_dot_hi(_tri(False).astype(F32), g),))


@jax.custom_vjp
def _softplus(x):
    return jnp.maximum(x, 0.0) + jnp.log(1.0 + jnp.exp(-jnp.abs(x)))


_softplus.defvjp(lambda x: (_softplus(x), x), lambda x, g: (g * jax.nn.sigmoid(x),))


def _silu(x):
    return x * jax.nn.sigmoid(x)


def _rows(arr, cw, off=0, lead=None, roff=0):
    return (arr, cw, off, lead, roff)


def _tiled(name, fn, grid, tm, rows, params, outs, accs=(), out_roff=0, prev_outs=None, into=None):
    into = into or {}
    ncol, nrow = grid
    in_specs, operands = [], []
    for arr, cw, off, lead, roff in rows:
        if lead is None:
            in_specs.append(pl.BlockSpec((tm, cw), functools.partial(lambda j, i, off, roff: (roff + i, off + j),
                                                                     off=off, roff=roff)))
        else:
            in_specs.append(pl.BlockSpec((None, tm, cw), functools.partial(
                lambda j, i, off, lead, roff: (lead, roff + i, off + j), off=off, lead=lead, roff=roff)))
        operands.append(arr)
    for arr, bs, im in params:
        in_specs.append(pl.BlockSpec(bs, im))
        operands.append(arr)
    out_specs, out_shape = [], []
    for k, (n_rows, cw, dt) in enumerate(outs):
        _, coff, total = into.get(k, (None, 0, ncol * cw))
        out_specs.append(pl.BlockSpec((tm, cw), functools.partial(lambda j, i, r, c: (r + i, c + j), r=out_roff, c=coff)))
        out_shape.append(jax.ShapeDtypeStruct((n_rows, total), dt))
    for shape, bs, im in accs:
        out_specs.append(pl.BlockSpec(bs, im))
        out_shape.append(jax.ShapeDtypeStruct(shape, F32))
    n_in, n_out = len(operands), len(outs)
    aliases = {}
    earlier = dict(enumerate(prev_outs)) if prev_outs is not None else {}
    earlier.update({k: v[0] for k, v in into.items() if v[0] is not None})
    for k, p in sorted(earlier.items()):
        aliases[len(operands)] = k
        in_specs.append(pl.BlockSpec(memory_space=pl.ANY))
        operands.append(p)

    n_all = len(operands)

    def body(*refs):
        vals = [r[...] for r in refs[:n_in]]
        o_vals, a_vals = fn(*vals)
        for r, v in zip(refs[n_all:n_all + n_out], o_vals):
            r[...] = v.astype(r.dtype)
        i = pl.program_id(1)
        for r, v in zip(refs[n_all + n_out:], a_vals):
            @pl.when(i == 0)
            def _(r=r, v=v):
                r[...] = v.astype(r.dtype)

            @pl.when(i > 0)
            def _(r=r, v=v):
                r[...] += v.astype(r.dtype)

    res = pl.pallas_call(body, grid=grid, in_specs=in_specs, out_specs=out_specs, out_shape=out_shape, name=name,
                         input_output_aliases=aliases, compiler_params=_cparams(("arbitrary", "arbitrary")))(*operands)
    return list(res)


def _with_vjp(fn, n_prim, want_out, want_acc):
    def f(*args):
        prim, g = args[:n_prim], args[n_prim:]
        outs, vjp = jax.vjp(lambda *a: fn(*a), *prim)
        d = vjp(tuple(gi.astype(o.dtype) for gi, o in zip(g, outs)))
        return tuple(d[k] for k in want_out), tuple(d[k] for k in want_acc)
    return f


def _p_row(arr, cw, off=0):
    return (arr, (1, cw), functools.partial(lambda j, i, off: (0, off + j), off=off))


def _a_row(n, cw):
    return ((1, n), (1, cw), lambda j, i: (0, j))


def _pick(n, cap, mult):
    best = None
    for t in range(mult, min(n, cap) + 1, mult):
        if n % t == 0:
            best = t
    return best if best is not None else n


def _matmul(name, a, b, mode, add=None, out_dtype=F32):
    if mode == "nn":
        (M, K), N = a.shape, b.shape[1]
    elif mode == "nt":
        (M, K), N = a.shape, b.shape[0]
    else:
        (K, M), N = a.shape, b.shape[1]
    tn = _pick(N, 1536, 128)
    k_cap = 2048 if mode == "tn" else 3456
    tk = K if K <= k_cap else _pick(K, k_cap, 128)
    nk = K // tk
    tm = _pick(M, 1408, 128) if mode == "tn" else _pick(M, 1024, 8)
    a_bytes, b_bytes = a.size * a.dtype.itemsize, b.size * b.dtype.itemsize
    swap = nk == 1 and a_bytes * (N // tn) + b_bytes < b_bytes * (M // tm) + a_bytes
    ij = (lambda g0, g1: (g1, g0)) if swap else (lambda g0, g1: (g0, g1))

    def spec(block, index):
        return pl.BlockSpec(block, lambda g0, g1, k: index(*ij(g0, g1), k))

    if mode == "nn":
        a_spec = spec((tm, tk), lambda i, j, k: (i, k))
        b_spec = spec((tk, tn), lambda i, j, k: (k, j))
        ca, cb = 1, 0
    elif mode == "nt":
        a_spec = spec((tm, tk), lambda i, j, k: (i, k))
        b_spec = spec((tn, tk), lambda i, j, k: (j, k))
        ca, cb = 1, 1
    else:
        a_spec = spec((tk, tm), lambda i, j, k: (k, i))
        b_spec = spec((tk, tn), lambda i, j, k: (k, j))
        ca, cb = 0, 0
    in_specs, operands = [a_spec, b_spec], [a, b]
    if add is not None:
        in_specs.append(spec((tm, tn), lambda i, j, k: (i, j)))
        operands.append(add)

    def finish(r, refs, o_ref):
        if add is not None:
            r = r + refs[2][...]
        o_ref[...] = r.astype(o_ref.dtype)

    def body_single(*refs):
        finish(_dot(refs[0][...], refs[1][...], ca, cb), refs, refs[-1])

    def body_multi(*refs):
        o_ref, acc_ref = refs[-2], refs[-1]
        k = pl.program_id(2)
        d = _dot(refs[0][...], refs[1][...], ca, cb)

        @pl.when(k == 0)
        def _():
            acc_ref[...] = d

        @pl.when(jnp.logical_and(k > 0, k < nk - 1))
        def _():
            acc_ref[...] += d

        @pl.when(k == nk - 1)
        def _():
            finish(acc_ref[...] + d, refs, o_ref)

    grid = (N // tn, M // tm, nk) if swap else (M // tm, N // tn, nk)
    return pl.pallas_call(
        body_single if nk == 1 else body_multi, grid=grid, in_specs=in_specs,
        out_specs=spec((tm, tn), lambda i, j, k: (i, j)),
        out_shape=jax.ShapeDtypeStruct((M, N), out_dtype),
        scratch_shapes=[] if nk == 1 else [pltpu.VMEM((tm, tn), F32)], name=name,
        compiler_params=_cparams(("parallel", "parallel", "arbitrary")))(*operands)


def _seg_copies(segs, c):
    out = []
    for lo, hi, dst in segs:
        n = lo
        while n < hi:
            p = n // c
            w = min(hi, (p + 1) * c) - n
            out.append((p, n - p * c, w, dst + n - lo))
            n += w
    return out


def _col_assemble(name, blocks, copies, zeros, n_out):
    _, R, c = blocks.shape
    tm = R if R <= 128 else 128

    def body(b_ref, o_ref):
        for p, s, w, d in copies:
            o_ref[:, d:d + w] = b_ref[p, :, s:s + w]
        for lo, hi in zeros:
            o_ref[:, lo:hi] = jnp.zeros((tm, hi - lo), o_ref.dtype)

    return pl.pallas_call(
        body, grid=(R // tm,), in_specs=[pl.BlockSpec((N_DEV, tm, c), lambda i: (0, i, 0))],
        out_specs=pl.BlockSpec((tm, n_out), lambda i: (i, 0)),
        out_shape=jax.ShapeDtypeStruct((R, n_out), blocks.dtype), name=name, compiler_params=_cparams(("parallel",)))(blocks)


def _col_split(name, full, copies, c, dtype):
    R, n = full.shape
    tm = R if R <= 128 else 128

    def body(f_ref, o_ref):
        for p, s, w, d in copies:
            o_ref[p, :, s:s + w] = f_ref[:, d:d + w].astype(dtype)

    return pl.pallas_call(
        body, grid=(R // tm,), in_specs=[pl.BlockSpec((tm, n), lambda i: (i, 0))],
        out_specs=pl.BlockSpec((N_DEV, tm, c), lambda i: (0, i, 0)),
        out_shape=jax.ShapeDtypeStruct((N_DEV, R, c), dtype), name=name, compiler_params=_cparams(("parallel",)))(full)


def _rmsnorm_fn(x, g):
    x = x.astype(F32)
    return (x * lax.rsqrt(jnp.mean(x * x, axis=-1, keepdims=True) + EPS) * g,)


def _gate_merge_fn(g0, g1, g2, ya, yb, yc, b0, b1, b2):
    return (jax.nn.sigmoid(g0 + b0) * ya + jax.nn.sigmoid(g1 + b1) * yb + jax.nn.sigmoid(g2 + b2) * yc,)


def _gated_norm_fn(y, z, w):
    t = y * _silu(z)
    return (t * lax.rsqrt(jnp.mean(t * t, axis=-1, keepdims=True) + EPS) * w,)


def _att_merge_fn(o0, o1, o2, l0, l1, l2):
    m = lax.stop_gradient(jnp.maximum(jnp.maximum(l0, l1), l2))
    e0, e1, e2 = jnp.exp(l0 - m), jnp.exp(l1 - m), jnp.exp(l2 - m)
    return ((e0 * o0 + e1 * o1 + e2 * o2) / (e0 + e1 + e2),)


def _loss_fn(x, tgt, g):
    (y,) = _rmsnorm_fn(x, g)
    err = y - tgt
    return 0.5 * jnp.sum(jnp.mean(err * err, axis=-1, keepdims=True), axis=0, keepdims=True)


def _pool_fn(x, wg, scale):
    g = pl.program_id(0)
    s2 = x + _shift(x, 1)
    s4 = s2 + _shift(s2, 2)
    s8 = s4 + _shift(s4, 4)
    s16 = s8 + _shift(s8, 8)
    win = ((g == 0).astype(F32) * s2 + (g == 1).astype(F32) * s4 + (g == 2).astype(F32) * s8
           + (g == 3).astype(F32) * s16)
    t = lax.broadcasted_iota(jnp.int32, (x.shape[0], 1), 0) + 1
    cnt = jnp.minimum(t, jnp.left_shift(2, g)).astype(F32)
    d = win / cnt - x
    return (_mm(d, wg.reshape(256, 256)) * scale,)


def _dwconv(x, taps, b):
    k = len(taps)
    y = taps[k - 1] * x + b
    for i in range(k - 1):
        y = y + taps[i] * _shift(x, k - 1 - i)
    return y


def _ssd_conv_fn(x, w0, w1, w2, w3, b):
    return (_silu(_dwconv(x, (w0, w1, w2, w3), b)),)


def _ffn_act_fn(xa, xv, a0, a1, a2, ab, v0, v1, v2, vb):
    xa, xv = xa.astype(F32), xv.astype(F32)
    return (_silu(_dwconv(xa, (a0, a1, a2), ab)) * _dwconv(xv, (v0, v1, v2), vb),)


@jax.custom_vjp
def _halves(x):
    return x[:ATT_BLOCK], x[ATT_BLOCK:]


_halves.defvjp(lambda x: (_halves(x), None), lambda _, g: (jnp.concatenate([g[0], g[1]], axis=0),))


def _att_block(q, kp, kc, vp, vc, bpa, bpb, bca, bcb, prev_ok):
    n = ATT_BLOCK
    lane = lax.broadcasted_iota(jnp.int32, (1, 2 * HEAD_DIM), 1)
    ma = (lane < HEAD_DIM).astype(F32)
    mb = 1.0 - ma
    q = q.astype(F32) * (1.0 / math.sqrt(HEAD_DIM))
    q2 = jnp.concatenate([q * ma, q * mb], axis=0)
    qi = lax.broadcasted_iota(jnp.int32, (2 * n, n), 0) & (n - 1)
    kj = lax.broadcasted_iota(jnp.int32, (2 * n, n), 1)
    sp = jnp.where(jnp.logical_and(kj >= qi, prev_ok), _mm_nt(q2, kp) + jnp.concatenate([bpa, bpb], axis=0), NEG)
    sc = jnp.where(kj <= qi, _mm_nt(q2, kc) + jnp.concatenate([bca, bcb], axis=0), NEG)
    m = lax.stop_gradient(jnp.maximum(jnp.max(sp, axis=1, keepdims=True), jnp.max(sc, axis=1, keepdims=True)))
    pp = jnp.exp(sp - m)
    pc = jnp.exp(sc - m)
    l = jnp.sum(pp, axis=1, keepdims=True) + jnp.sum(pc, axis=1, keepdims=True)
    oa, ob = _halves((_mm(pp, vp) + _mm(pc, vc)) / l)
    la, lb = _halves((m + jnp.log(l)) * jnp.ones((1, 2 * HEAD_DIM), F32))
    return oa * ma + ob * mb, la * ma + lb * mb


def _att_slab(dil):
    nbk = 8 if dil == 1 else 1
    t = ATT_BLOCK * dil * nbk
    return nbk, t, S // t


def _att_in_specs(gi, t):
    def spec(which, prev):
        col = OFF_QKV // 128 + gi * 9 + which

        def index(p, j, col=col, prev=prev):
            jj = jnp.minimum(j, S // t - 1)
            return (jnp.maximum(jj - 1, 0) if prev else jj, col + 3 * p)
        return pl.BlockSpec((t, 2 * HEAD_DIM), index)
    return [spec(0, False), spec(1, False), spec(1, True), spec(2, False), spec(2, True)]


def _bias_specs():
    return [pl.BlockSpec((None, ATT_BLOCK, ATT_BLOCK), functools.partial(lambda p, j, hh: (2 * p + hh, 0, 0), hh=hh))
            for hh in (0, 1)]


def _att_units(dil, nbk, body):
    def per_residue(r, carry):
        for b in range(nbk):
            rows = pl.ds(b * ATT_BLOCK * dil + r, ATT_BLOCK, stride=dil)
            prev = pl.ds(((b - 1) % nbk) * ATT_BLOCK * dil + r, ATT_BLOCK, stride=dil)
            body(b, rows, prev, b > 0)
        return carry
    if dil == 1:
        per_residue(0, 0)
    else:
        lax.fori_loop(0, dil, per_residue, 0, unroll=min(dil, 8))


def _att_fwd(name, proj, gi, bias_p, bias_c):
    dil = ATT_GROUPS[gi][1]
    nbk, t, ns = _att_slab(dil)
    bsp = _bias_specs()
    out_spec = pl.BlockSpec((t, 2 * HEAD_DIM), lambda p, j: (j, p))

    def body(q_ref, kc_ref, kp_ref, vc_ref, vp_ref, bpa, bpb, bca, bcb, o_ref, l_ref):
        first = pl.program_id(1) == 0
        biases = (bpa[...], bpb[...], bca[...], bcb[...])

        def unit(b, rows, prev, in_slab):
            kp = kc_ref[prev, :] if in_slab else kp_ref[prev, :]
            vp = vc_ref[prev, :] if in_slab else vp_ref[prev, :]
            prev_ok = True if in_slab else jnp.logical_not(first)
            o, lse = _att_block(q_ref[rows, :], kp, kc_ref[rows, :], vp, vc_ref[rows, :], *biases, prev_ok)
            o_ref[rows, :] = o
            l_ref[rows, :] = lse

        _att_units(dil, nbk, unit)

    shp = jax.ShapeDtypeStruct((S, ATT_GW), F32)
    return pl.pallas_call(
        body, grid=(3, ns), in_specs=_att_in_specs(gi, t) + [bsp[0], bsp[1], bsp[0], bsp[1]],
        out_specs=[out_spec, out_spec], out_shape=[shp, shp], name=name,
        compiler_params=_cparams(("arbitrary",) * 2))(proj, proj, proj, proj, proj, bias_p, bias_p, bias_c, bias_c)


def _att_bwd(name, proj, gi, bias_p, bias_c, do, dl, dproj):
    dil = ATT_GROUPS[gi][1]
    nbk, t, ns = _att_slab(dil)
    bsp = _bias_specs()
    blk = (t, 2 * HEAD_DIM)
    cur = pl.BlockSpec(blk, lambda p, j: (jnp.minimum(j, ns - 1), p))
    done = pl.BlockSpec((t, QKV_W), lambda p, j: (jnp.maximum(j - 1, 0), OFF_QKV // QKV_W + gi * 3 + p))
    gsp = pl.BlockSpec((None, ATT_BLOCK, ATT_BLOCK), lambda p, j: (p, 0, 0))

    def body(q_ref, kc_ref, kp_ref, vc_ref, vp_ref, bpa, bpb, bca, bcb, do_ref, dl_ref, _,
             dqkv_ref, gpa, gpb, gca, gcb, accq, acck, accv):
        j = pl.program_id(1)
        mine, other = acck.at[j % 2], acck.at[1 - j % 2]
        mine_v, other_v = accv.at[j % 2], accv.at[1 - j % 2]
        dq_ref, other_q = accq.at[j % 2], accq.at[1 - j % 2]

        @pl.when(j == 0)
        def _():
            for g in (gpa, gpb, gca, gcb):
                g[...] = jnp.zeros_like(g)
            other[...] = jnp.zeros_like(other)
            other_v[...] = jnp.zeros_like(other_v)
            other_q[...] = jnp.zeros_like(other_q)

        @pl.when(j < ns)
        def _():
            mine[...] = jnp.zeros_like(mine)
            mine_v[...] = jnp.zeros_like(mine_v)
            biases = (bpa[...], bpb[...], bca[...], bcb[...])

            def unit(b, rows, prev, in_slab):
                kp = kc_ref[prev, :] if in_slab else kp_ref[prev, :]
                vp = vc_ref[prev, :] if in_slab else vp_ref[prev, :]
                prev_ok = True if in_slab else j > 0
                prim = (q_ref[rows, :], kp, kc_ref[rows, :], vp, vc_ref[rows, :]) + biases
                _, vjp = jax.vjp(lambda *a: _att_block(*a, prev_ok), *prim)
                dq, dkp, dkc, dvp, dvc, dpa, dpb, dca, dcb = vjp((do_ref[rows, :], dl_ref[rows, :]))
                dq_ref[rows, :] = dq
                mine[rows, :] += dkc
                mine_v[rows, :] += dvc
                tgt, tgt_v = (mine, mine_v) if in_slab else (other, other_v)
                tgt[prev, :] += dkp
                tgt_v[prev, :] += dvp
                gpa[...] += dpa
                gpb[...] += dpb
                gca[...] += dca
                gcb[...] += dcb

            _att_units(dil, nbk, unit)

        w = 2 * HEAD_DIM
        dqkv_ref[:, 0:w] = other_q[...].astype(BF16)
        dqkv_ref[:, w:2 * w] = other[...].astype(BF16)
        dqkv_ref[:, 2 * w:3 * w] = other_v[...].astype(BF16)

    gshp = jax.ShapeDtypeStruct((3, ATT_BLOCK, ATT_BLOCK), F32)
    res = pl.pallas_call(
        body, grid=(3, ns + 1),
        in_specs=_att_in_specs(gi, t) + [bsp[0], bsp[1], bsp[0], bsp[1], cur, cur, pl.BlockSpec(memory_space=pl.ANY)],
        out_specs=[done, gsp, gsp, gsp, gsp],
        out_shape=[jax.ShapeDtypeStruct((S, NP), BF16), gshp, gshp, gshp, gshp],
        input_output_aliases={11: 0},
        scratch_shapes=[pltpu.VMEM((2,) + blk, F32)] * 3, name=name,
        compiler_params=_cparams(("arbitrary",) * 2))(proj, proj, proj, proj, proj, bias_p, bias_p, bias_c, bias_c, do, dl,
                                                      dproj)
    dproj, gpa, gpb, gca, gcb = res
    heads = lambda a, b: jnp.stack([a, b], axis=1).reshape(6, ATT_BLOCK, ATT_BLOCK)
    return dproj, heads(gpa, gpb), heads(gca, gcb)


N_PAIR = SSD_HEADS // 2


def _ssd_chunk(xs, bs, cs_in, dt_raw, hs, a_row, dtb_row, ds):
    lane = lax.broadcasted_iota(jnp.int32, (1, 128), 1)
    row = lax.broadcasted_iota(jnp.int32, (128, 1), 0)
    tril = _tri(True)
    dt = _softplus(dt_raw + dtb_row)
    acs = _cumsum_rows(dt * a_row)
    acs_t = acs.T
    gmat = [_mm_nt(cs_in[g], bs[g]) for g in range(2)]
    lo = lane < HEAD_DIM
    lo_r = row < HEAD_DIM
    last = (row == SSD_CHUNK - 1).astype(F32)
    ys, hn = [], []
    for p in range(N_PAIR):
        g = p // (N_PAIR // 2)
        col, dtc, mm, clast = [], [], [], []
        for hh in range(2):
            h = 2 * p + hh
            oh = (lane == h).astype(F32)
            c_col = jnp.sum(acs * oh, axis=1, keepdims=True)
            c_row = jnp.sum(acs_t * (row == h).astype(F32), axis=0, keepdims=True)
            col.append(c_col)
            dtc.append(jnp.sum(dt * oh, axis=1, keepdims=True))
            clast.append(jnp.sum(c_col * last, axis=0, keepdims=True))
            mm.append(gmat[g] * jnp.exp(jnp.where(tril, c_col - c_row, NEG)))
        x = xs[p]
        xd = x * jnp.where(lo, dtc[0], dtc[1])
        y = jnp.where(lo, _mm(mm[0], xd), _mm(mm[1], xd))
        y = y + jnp.where(lo, jnp.exp(col[0]), jnp.exp(col[1])) * _mm_nt(cs_in[g], hs[p])
        ys.append(y + ds[p] * x)
        dec = jnp.where(lo, jnp.exp(clast[0] - col[0]), jnp.exp(clast[1] - col[1]))
        hn.append(hs[p] * jnp.where(lo_r, jnp.exp(clast[0]), jnp.exp(clast[1])) + _mm_tn(xd * dec, bs[g]))
    return tuple(ys), tuple(hn)


def _ssd_load(xbc_ref, dt_ref, a_ref, dtb_ref, d_ref):
    xs = tuple(xbc_ref[:, 128 * p:128 * (p + 1)] for p in range(N_PAIR))
    bs = tuple(xbc_ref[:, D + 128 * g:D + 128 * (g + 1)] for g in range(2))
    cs = tuple(xbc_ref[:, D + 256 + 128 * g:D + 256 + 128 * (g + 1)] for g in range(2))
    ds = tuple(d_ref[:, 128 * p:128 * (p + 1)] for p in range(N_PAIR))
    return xs, bs, cs, dt_ref[...], a_ref[...], dtb_ref[...], ds


def _ssd_fwd(name, xbc_c, proj, a_row, dtb_row, d_exp):
    nc = S // SSD_CHUNK
    prow = lambda n: pl.BlockSpec((1, n), lambda c: (0, 0))

    def body(xbc_ref, dt_ref, a_ref, dtb_ref, d_ref, y_ref, st_ref, h_ref):
        @pl.when(pl.program_id(0) == 0)
        def _():
            h_ref[...] = jnp.zeros_like(h_ref)

        xs, bs, cs, dt_raw, a, dtb, ds = _ssd_load(xbc_ref, dt_ref, a_ref, dtb_ref, d_ref)
        hs = tuple(h_ref[p] for p in range(N_PAIR))
        ys, hn = _ssd_chunk(xs, bs, cs, dt_raw, hs, a, dtb, ds)
        for p in range(N_PAIR):
            y_ref[:, 128 * p:128 * (p + 1)] = ys[p]
            st_ref[p] = hs[p]
            h_ref[p] = hn[p]

    return pl.pallas_call(
        body, grid=(nc,),
        in_specs=[pl.BlockSpec((SSD_CHUNK, SSD_XBC), lambda c: (c, 0)),
                  pl.BlockSpec((SSD_CHUNK, DT_PAD), lambda c: (c, OFF_DT // DT_PAD)),
                  prow(128), prow(128), prow(D)],
        out_specs=[pl.BlockSpec((SSD_CHUNK, D), lambda c: (c, 0)),
                   pl.BlockSpec((None, N_PAIR, 128, 128), lambda c: (c, 0, 0, 0))],
        out_shape=[jax.ShapeDtypeStruct((S, D), F32), jax.ShapeDtypeStruct((nc, N_PAIR, 128, 128), F32)],
        scratch_shapes=[pltpu.VMEM((N_PAIR, 128, 128), F32)], name=name,
        compiler_params=_cparams(("arbitrary",)))(xbc_c, proj, a_row, dtb_row, d_exp)


def _ssd_bwd(name, xbc_c, proj, states, dy, a_row, dtb_row, d_exp, dproj):
    nc = S // SSD_CHUNK
    prow = lambda n: pl.BlockSpec((1, n), lambda i: (0, 0))
    rc = lambda i: nc - 1 - i

    def body(xbc_ref, dt_ref, st_ref, dy_ref, a_ref, dtb_ref, d_ref, _, dxbc_ref, ddt_ref, da_ref, ddtb_ref, dd_ref, e_ref):
        i = pl.program_id(0)

        @pl.when(i == 0)
        def _():
            e_ref[...] = jnp.zeros_like(e_ref)
            da_ref[...] = jnp.zeros_like(da_ref)
            ddtb_ref[...] = jnp.zeros_like(ddtb_ref)
            dd_ref[...] = jnp.zeros_like(dd_ref)

        xs, bs, cs, dt_raw, a, dtb, ds = _ssd_load(xbc_ref, dt_ref, a_ref, dtb_ref, d_ref)
        hs = tuple(st_ref[p] for p in range(N_PAIR))
        _, vjp = jax.vjp(_ssd_chunk, xs, bs, cs, dt_raw, hs, a, dtb, ds)
        dys = tuple(dy_ref[:, 128 * p:128 * (p + 1)] for p in range(N_PAIR))
        es = tuple(e_ref[p] for p in range(N_PAIR))
        dxs, dbs, dcs, ddt, dhs, da, ddtb, dds = vjp((dys, es))
        for p in range(N_PAIR):
            dxbc_ref[:, 128 * p:128 * (p + 1)] = dxs[p]
            e_ref[p] = dhs[p]
            dd_ref[:, 128 * p:128 * (p + 1)] += dds[p]
        for g in range(2):
            dxbc_ref[:, D + 128 * g:D + 128 * (g + 1)] = dbs[g]
            dxbc_ref[:, D + 256 + 128 * g:D + 256 + 128 * (g + 1)] = dcs[g]
        ddt_ref[:, :DT_PAD] = ddt.astype(BF16)
        ddt_ref[:, DT_PAD:] = jnp.zeros((SSD_CHUNK, OFF_QKV - OFF_DT - DT_PAD), BF16)
        da_ref[...] += da
        ddtb_ref[...] += ddtb

    dt_w = OFF_QKV - OFF_DT
    return pl.pallas_call(
        body, grid=(nc,),
        in_specs=[pl.BlockSpec((SSD_CHUNK, SSD_XBC), lambda i: (rc(i), 0)),
                  pl.BlockSpec((SSD_CHUNK, DT_PAD), lambda i: (rc(i), OFF_DT // DT_PAD)),
                  pl.BlockSpec((None, N_PAIR, 128, 128), lambda i: (rc(i), 0, 0, 0)),
                  pl.BlockSpec((SSD_CHUNK, D), lambda i: (rc(i), 0)),
                  prow(128), prow(128), prow(D), pl.BlockSpec(memory_space=pl.ANY)],
        out_specs=[pl.BlockSpec((SSD_CHUNK, SSD_XBC), lambda i: (rc(i), 0)),
                   pl.BlockSpec((SSD_CHUNK, dt_w), lambda i: (rc(i), OFF_DT // dt_w)),
                   prow(128), prow(128), prow(D)],
        out_shape=[jax.ShapeDtypeStruct((S, SSD_XBC), F32), jax.ShapeDtypeStruct((S, NP), BF16),
                   jax.ShapeDtypeStruct((1, 128), F32), jax.ShapeDtypeStruct((1, 128), F32),
                   jax.ShapeDtypeStruct((1, D), F32)],
        input_output_aliases={7: 1},
        scratch_shapes=[pltpu.VMEM((N_PAIR, 128, 128), F32)], name=name,
        compiler_params=_cparams(("arbitrary",)))(xbc_c, proj, states, dy, a_row, dtb_row, d_exp, dproj)


def _exchange(name, arrays, scatter):
    n = len(arrays)
    flips = [(dx, dy, dc) for dx in (0, 1) for dy in (0, 1) for dc in (0, 1) if dx or dy or dc]

    def body(*refs):
        ins, outs = refs[:n], refs[n:2 * n]
        send_sems, recv_sems, loc_sems = refs[2 * n:]
        x, y, c = lax.axis_index("x"), lax.axis_index("y"), lax.axis_index("c")
        me = 4 * x + 2 * y + c
        peers = []
        for dx, dy, dc in flips:
            px, py, pc = (1 - x if dx else x), (1 - y if dy else y), (1 - c if dc else c)
            peers.append(((px, py, pc), 4 * px + 2 * py + pc))

        def remote(k, j, landed_from):
            dev, pid = peers[j]
            src = ins[k].at[pid] if scatter else ins[k]
            return pltpu.make_async_remote_copy(
                src_ref=src, dst_ref=outs[k].at[landed_from], send_sem=send_sems.at[k, j], recv_sem=recv_sems.at[k, j],
                device_id=dev, device_id_type=pl.DeviceIdType.MESH)

        local = [pltpu.make_async_copy(ins[k].at[me] if scatter else ins[k], outs[k].at[me], loc_sems.at[k])
                 for k in range(n)]
        for cp in local:
            cp.start()
        for k in range(n):
            for j in range(len(flips)):
                remote(k, j, me).start()
        for cp in local:
            cp.wait()
        for k in range(n):
            for j in range(len(flips)):
                remote(k, j, me).wait_send()
                remote(k, j, peers[j][1]).wait_recv()

    hbm = pl.BlockSpec(memory_space=pltpu.HBM)
    out_shape = [jax.ShapeDtypeStruct(a.shape if scatter else (N_DEV,) + a.shape, a.dtype) for a in arrays]
    res = pl.pallas_call(
        body, in_specs=[hbm] * n, out_specs=[hbm] * n, out_shape=out_shape, name=name,
        scratch_shapes=[pltpu.SemaphoreType.DMA((n, len(flips))), pltpu.SemaphoreType.DMA((n, len(flips))),
                        pltpu.SemaphoreType.DMA((n,))])(*arrays)
    return list(res)


def _gather_chip_once(name, block):
    def body(x_ref, out_ref, send_sems, recv_sems, loc_sem):
        x, y, c = lax.axis_index("x"), lax.axis_index("y"), lax.axis_index("c")
        me, sibling = (x, y, c), (x, y, 1 - c)
        chips = [(1 - x, y), (x, 1 - y), (1 - x, 1 - y)]

        def slot(px, py, pc):
            return out_ref.at[4 * px + 2 * py + pc]

        def copy(k, blk, to, src=None):
            return pltpu.make_async_remote_copy(
                src_ref=slot(*blk) if src is None else src, dst_ref=slot(*blk), send_sem=send_sems.at[k],
                recv_sem=recv_sems.at[k], device_id=to, device_id_type=pl.DeviceIdType.MESH)

        mine = pltpu.make_async_copy(x_ref, slot(*me), loc_sem)
        mine.start()
        first = [copy(0, me, sibling, src=x_ref)] + [copy(1 + j, me, (*chip, c), src=x_ref) for j, chip in enumerate(chips)]
        for cp in first:
            cp.start()
        passed = [copy(4 + j, (*chip, c), sibling) for j, chip in enumerate(chips)]
        for j, chip in enumerate(chips):
            copy(1 + j, (*chip, c), me).wait_recv()
            passed[j].start()
        copy(0, sibling, me).wait_recv()
        for j, chip in enumerate(chips):
            copy(4 + j, (*chip, 1 - c), me).wait_recv()
        for cp in first + passed:
            cp.wait_send()
        mine.wait()

    hbm = pl.BlockSpec(memory_space=pltpu.HBM)
    return pl.pallas_call(
        body, in_specs=[hbm], out_specs=hbm, out_shape=jax.ShapeDtypeStruct((N_DEV,) + block.shape, block.dtype), name=name,
        scratch_shapes=[pltpu.SemaphoreType.DMA((N_DEV - 1,)), pltpu.SemaphoreType.DMA((N_DEV - 1,)),
                        pltpu.SemaphoreType.DMA(())])(block)


def _peer_copies(ins, lands, send_sems, recv_sems, loc_sems, scatter):
    n = len(ins)
    flips = [(dx, dy, dc) for dx in (0, 1) for dy in (0, 1) for dc in (0, 1) if dx or dy or dc]
    x, y, c = lax.axis_index("x"), lax.axis_index("y"), lax.axis_index("c")
    me = 4 * x + 2 * y + c
    peers = []
    for dx, dy, dc in flips:
        px, py, pc = (1 - x if dx else x), (1 - y if dy else y), (1 - c if dc else c)
        peers.append(((px, py, pc), 4 * px + 2 * py + pc))

    def remote(k, j, slot):
        dev, pid = peers[j]
        return pltpu.make_async_remote_copy(
            src_ref=ins[k].at[pid] if scatter else ins[k], dst_ref=lands[k].at[slot],
            send_sem=send_sems.at[k * N_FLIP + j], recv_sem=recv_sems.at[k * N_FLIP + j],
            device_id=dev, device_id_type=pl.DeviceIdType.MESH)

    local = [pltpu.make_async_copy(ins[k].at[me] if scatter else ins[k], lands[k].at[me], loc_sems.at[k])
             for k in range(n)]
    pairs = [(k, j) for k in range(n) for j in range(len(flips))]
    sent = lambda k, j: remote(k, j, me)
    landed = lambda k, j: remote(k, j, peers[j][1])
    return local, pairs, sent, landed


_HBM = pl.BlockSpec(memory_space=pltpu.HBM)
_SEM = pl.BlockSpec(memory_space=pltpu.SEMAPHORE)
N_FLIP = N_DEV - 1


def _exchange_start(name, arrays, scatter, after):
    n = len(arrays)
    arrays = [pltpu.with_memory_space_constraint(a, pltpu.HBM) for a in arrays]
    lands = [pltpu.with_memory_space_constraint(
        lax.empty(a.shape if scatter else (N_DEV,) + a.shape, a.dtype), pltpu.HBM) for a in arrays]

    def body(*refs):
        ins, lnd = refs[:n], refs[n:2 * n]
        send_sems, recv_sems, loc_sems = refs[2 * n + 1:2 * n + 4]
        token = refs[-1]
        local, pairs, sent, _ = _peer_copies(ins, lnd, send_sems, recv_sems, loc_sems, scatter)
        for cp in local:
            cp.start()
        for k, j in pairs:
            sent(k, j).start()
        token[...] = jnp.zeros_like(token)

    res = pl.pallas_call(
        body, name=name,
        in_specs=[_HBM] * (2 * n) + [pl.BlockSpec(memory_space=pl.ANY)],
        out_specs=[_SEM, _SEM, _SEM] + [_HBM] * (2 * n) + [pl.BlockSpec(memory_space=pltpu.VMEM)],
        out_shape=[pltpu.SemaphoreType.DMA((n * N_FLIP,)), pltpu.SemaphoreType.DMA((n * N_FLIP,)), pltpu.SemaphoreType.DMA((n,))]
        + [pltpu.HBM(a.shape, a.dtype) for a in arrays] + [pltpu.HBM(a.shape, a.dtype) for a in lands]
        + [jax.ShapeDtypeStruct((8, 128), F32)],
        input_output_aliases={k: 3 + k for k in range(2 * n)},
        compiler_params=pltpu.CompilerParams(has_side_effects=pltpu.SideEffectType.DATAFLOW_SIDE_EFFECTING),
    )(*arrays, *lands, after)
    return (res[:3], res[3:3 + n], res[3 + n:3 + 2 * n], scatter), res[-1]


def _exchange_wait(name, state, after):
    sems, ins_thru, lands_thru, scatter = state
    n = len(ins_thru)

    def body(*refs):
        ins, lnd = refs[:n], refs[n:2 * n]
        send_sems, recv_sems, loc_sems = refs[2 * n:2 * n + 3]
        local, pairs, sent, landed = _peer_copies(ins, lnd, send_sems, recv_sems, loc_sems, scatter)
        for cp in local:
            cp.wait()
        for k, j in pairs:
            sent(k, j).wait_send()
            landed(k, j).wait_recv()

    res = pl.pallas_call(
        body, name=name,
        in_specs=[_HBM] * (2 * n) + [_SEM, _SEM, _SEM] + [pl.BlockSpec(memory_space=pl.ANY)],
        out_specs=[_HBM] * (2 * n),
        out_shape=[pltpu.HBM(a.shape, a.dtype) for a in ins_thru] + [pltpu.HBM(a.shape, a.dtype) for a in lands_thru],
        input_output_aliases={k: k for k in range(2 * n)},
        compiler_params=pltpu.CompilerParams(has_side_effects=pltpu.SideEffectType.DATAFLOW_SIDE_EFFECTING),
    )(*ins_thru, *lands_thru, *sems, after)
    return list(res[n:])


def _adamw_fn(*vals):
    slots, (w, m, v) = vals[:N_DEV], vals[N_DEV:]
    g = slots[0].astype(F32)
    for s in slots[1:]:
        g = g + s.astype(F32)
    m2 = ADAM_B1 * m + (1.0 - ADAM_B1) * g
    v2 = ADAM_B2 * v + (1.0 - ADAM_B2) * (g * g)
    m_hat = m2 / (1.0 - ADAM_B1 ** ADAM_STEP)
    v_hat = v2 / (1.0 - ADAM_B2 ** ADAM_STEP)
    delta = -ADAM_LR * (m_hat / (jnp.sqrt(v_hat) + ADAM_EPS) + ADAM_WD * w)
    return (g, delta, m2, v2), ()


def _adamw(name, slots, w, m, v, first_row=0, prev=None):
    R, C = slots.shape[1:]
    tm = R if R <= 128 else _pick(R, 128 if C > D else 256, 8)
    rows = ([_rows(slots, C, lead=s) for s in range(N_DEV)]
            + [_rows(a, C, roff=first_row // tm) for a in (w, m, v)])
    return _tiled(name, _adamw_fn, (1, R // tm), tm, rows, [], [(w.shape[0], C, F32)] * 4,
                  out_roff=first_row // tm, prev_outs=prev)


def _bucket_onehots():
    out = []
    qi = jnp.arange(ATT_BLOCK)[:, None]
    kj = jnp.arange(ATT_BLOCK)[None, :]
    max_exact = REL_BUCKETS // 2
    for _, dil in ATT_GROUPS:
        parts = []
        for rel in (qi + ATT_BLOCK - kj, qi - kj):
            dist = jnp.clip(rel, 0, None) * dil
            nf = jnp.maximum(dist, 1).astype(F32)
            large = max_exact + (jnp.log(nf / max_exact) / math.log(REL_MAX_DISTANCE / max_exact)
                                 * (REL_BUCKETS - max_exact)).astype(jnp.int32)
            large = jnp.minimum(large, REL_BUCKETS - 1)
            bucket = jnp.where(dist < max_exact, dist, large)
            parts.append((bucket[:, :, None] == jnp.arange(REL_BUCKETS)[None, None, :]).astype(F32))
        out.append(jnp.stack(parts))
    return out


SHARDED = ("w_in", "w_a", "pool_w", "w_b", "ssd_conv_w", "w_c", "w_o", "ffn_w_up", "ffn_conv_w", "ffn_w_down")
MATMUL_WEIGHTS = ("w_in", "w_a", "pool_w", "w_b", "w_c", "w_o", "ffn_w_up", "ffn_w_down")
ROW_SHARDED = ("w_b", "w_c", "w_o", "ffn_w_down")
W_IN_SEGS = tuple(
    (which * ATT_W + unit * 128, which * ATT_W + (unit + 1) * 128, OFF_QKV + unit * QKV_W + which * 128)
    for unit in range(9) for which in range(3)
) + ((3456, 4480, OFF_POOL), (4480, 5504, OFF_Z), (5504, 7040, OFF_XBC), (7040, 7056, OFF_DT), (7056, IN_WIDTH, OFF_GATE))
FFN_SEGS = tuple((h * D_FF + j * 128, h * D_FF + (j + 1) * 128, j * 256 + h * 128)
                 for j in range(D_FF // 128) for h in range(2))
COL_SHARDED = {
    "w_in": (IN_WIDTH // N_DEV, W_IN_SEGS, ((OFF_DT + SSD_HEADS, OFF_QKV),), NP),
    "w_a": (D // N_DEV, ((0, D, 0),), (), D),
    "ffn_w_up": (2 * D_FF // N_DEV, FFN_SEGS, (), 2 * D_FF),
    "ssd_conv_w": (SSD_XBC // N_DEV, ((0, SSD_XBC, 0),), (), SSD_XBC),
    "ffn_conv_w": (2 * D_FF // N_DEV, FFN_SEGS, (), 2 * D_FF),
}
REPLICATED = ("rel_bias", "ln1_g", "b_gate", "pool_scale", "ssd_conv_b", "ssd_dt_bias", "ssd_a_log", "ssd_d",
              "ssd_norm_w", "ln2_g", "ffn_conv_b", "final_g")
WEIGHTS = ("rel_bias", "ln1_g", "w_in", "b_gate", "w_a", "pool_w", "pool_scale", "w_b", "ssd_conv_w", "ssd_conv_b",
           "ssd_dt_bias", "ssd_a_log", "ssd_d", "ssd_norm_w", "w_c", "w_o", "ln2_g", "ffn_w_up", "ffn_conv_w",
           "ffn_conv_b", "ffn_w_down", "final_g")


def _local_weight(name, n, blocks):
    if n in COL_SHARDED:
        c, segs, zeros, width = COL_SHARDED[n]
        return _col_assemble(name, blocks, _seg_copies(segs, c), zeros, width)
    if n in ROW_SHARDED:
        return blocks.reshape(-1, blocks.shape[-1])
    return blocks


def _device_blocks(name, n, g):
    if n in COL_SHARDED:
        c, segs, _, _ = COL_SHARDED[n]
        return _col_split(name, g, _seg_copies(segs, c), c, BF16)
    if n in ROW_SHARDED:
        return g.reshape(N_DEV, g.shape[0] // N_DEV, g.shape[1]).astype(BF16)
    return g.astype(BF16)


def _row(v, n=None):
    v = v.reshape(1, -1)
    if n is not None and v.shape[1] < n:
        v = jnp.pad(v, ((0, 0), (0, n - v.shape[1])))
    return v


RT = 512


def _row_call(name, fn, cw, ncol, rows, params, outs, accs=(), into=None):
    return _tiled(name, fn, (ncol, S // RT), RT, rows, params, [(S, cw, dt) for dt in outs], accs, into=into)


def _col_call(name, fn, tc, ncol, rows, params, outs, accs=(), into=None):
    return _tiled(name, fn, (ncol, 1), S, rows, params, [(S, tc, dt) for dt in outs], accs, into=into)


def _fwd_only(fn):
    return lambda *a: (fn(*a), ())


def _layer_fwd(i, x, W, P, bias_tabs, late=None, late_ffn=None):
    sv = {"x": x}
    (u,) = _row_call(f"ln1_f{i}", _fwd_only(_rmsnorm_fn), D, 1, [_rows(x, D)], [_p_row(P["ln1_g"], D)], [BF16])
    proj = _matmul(f"inproj_f{i}", u, W["w_in"], "nn")
    sv["u"], sv["proj"] = u, proj

    os_, ls_ = [], []
    for gi in range(len(ATT_GROUPS)):
        o, lse = _att_fwd(f"att_f{i}_{gi}", proj, gi, bias_tabs[gi][0], bias_tabs[gi][1])
        os_.append(o)
        ls_.append(lse)
    sv["att_o"], sv["att_l"] = os_, ls_
    (att,) = _row_call(f"attmerge_f{i}", _fwd_only(_att_merge_fn), ATT_GW, 1,
                       [_rows(t, ATT_GW) for t in os_ + ls_], [], [BF16])
    if late is not None:
        W2, P2 = late(att)
        W.update(W2)
        P.update(P2)
    y_a = _matmul(f"wa_f{i}", att, W["w_a"], "nn", out_dtype=BF16)
    sv["att"], sv["y_a"] = att, y_a

    pool_params = [(W["pool_w"], (N_DEV, None, 32, 256), lambda j, i_: (0, j, 0, 0)), _p_row(P["pool_scale"], 256)]
    (yb_pre,) = _col_call(f"pool_f{i}", _fwd_only(_pool_fn), 256, 4, [_rows(proj, 256, OFF_POOL // 256)],
                          pool_params, [BF16])
    y_b = _matmul(f"wb_f{i}", yb_pre, W["w_b"], "nn", out_dtype=BF16)
    sv["yb_pre"], sv["y_b"] = yb_pre, y_b

    conv_params = [_p_row(P["ssd_conv_w"][k], 128) for k in range(4)] + [_p_row(P["ssd_conv_b"], 128)]
    (xbc_c,) = _col_call(f"ssdconv_f{i}", _fwd_only(_ssd_conv_fn), 128, SSD_XBC // 128,
                         [_rows(proj, 128, OFF_XBC // 128)], conv_params, [F32])
    y_ssd, states = _ssd_fwd(f"ssd_f{i}", xbc_c, proj, P["a_row"], P["dtb_row"], P["d_exp"])
    (yc_pre,) = _row_call(f"ssdnorm_f{i}", _fwd_only(_gated_norm_fn), 512, 2,
                          [_rows(y_ssd, 512), _rows(proj, 512, OFF_Z // 512)], [_p_row(P["ssd_norm_w"], 512)], [BF16])
    y_c = _matmul(f"wc_f{i}", yc_pre, W["w_c"], "nn", out_dtype=BF16)
    sv["xbc_c"], sv["states"], sv["y_ssd"], sv["yc_pre"], sv["y_c"] = xbc_c, states, y_ssd, yc_pre, y_c

    gate_rows = [_rows(proj, D, k) for k in range(3)] + [_rows(t, D) for t in (y_a, y_b, y_c)]
    gate_params = [_p_row(P["b_gate"], D, k) for k in range(3)]
    (merged,) = _row_call(f"gate_f{i}", _fwd_only(_gate_merge_fn), D, 1, gate_rows, gate_params, [BF16])
    x1 = _matmul(f"wo_f{i}", merged, W["w_o"], "nn", add=x)
    sv["merged"], sv["x1"] = merged, x1

    if late_ffn is not None:
        W2, P2 = late_ffn(x1)
        W.update(W2)
        P.update(P2)
    (u2,) = _row_call(f"ln2_f{i}", _fwd_only(_rmsnorm_fn), D, 1, [_rows(x1, D)], [_p_row(P["ln2_g"], D)], [BF16])
    up = _matmul(f"up_f{i}", u2, W["ffn_w_up"], "nn", out_dtype=BF16)
    (act,) = _col_call(f"ffnact_f{i}", lambda *a: (_ffn_act_fn(*_ffn_halves(a)), ()), 128, D_FF // 128,
                       [_rows(up, 256)], _ffn_params(P), [BF16])
    x2 = _matmul(f"down_f{i}", act, W["ffn_w_down"], "nn", add=x1)
    sv["u2"], sv["up"], sv["act"] = u2, up, act
    return x2, sv


def _ffn_params(P):
    return [_p_row(P["ffn_conv_w"][k], 256) for k in range(3)] + [_p_row(_interleave_ffn(P["ffn_conv_b"]), 256)]


def _ffn_halves(vals):
    (xa, xv), (a0, v0), (a1, v1), (a2, v2), (ab, vb) = ((t[:, :128], t[:, 128:]) for t in vals)
    return xa, xv, a0, a1, a2, ab, v0, v1, v2, vb


def _interleave_ffn(row):
    return row.reshape(2, D_FF // 128, 128).transpose(1, 0, 2).reshape(1, 2 * D_FF)


def _deinterleave_ffn(row):
    return row.reshape(D_FF // 128, 2, 128).transpose(1, 0, 2).reshape(1, 2 * D_FF)


def _layer_bwd(i, dx2, sv, W, P, bias_tabs, onehots, on_ffn_grads, on_sharded_grads):
    G = {}
    x, proj, x1 = sv["x"], sv["proj"], sv["x1"]

    dact = _matmul(f"down_bx{i}", dx2, W["ffn_w_down"], "nt", out_dtype=BF16)
    G["ffn_w_down"] = _matmul(f"down_bw{i}", sv["act"], dx2, "tn", out_dtype=BF16)
    nb = D_FF // 128
    up = sv["up"]
    def ffn_bwd(x2, t0, t1, t2, b2, dact_):
        (dxa, dxv), (a0, a1, a2, ab, v0, v1, v2, vb) = _with_vjp(_ffn_act_fn, 10, (0, 1), tuple(range(2, 10)))(
            *_ffn_halves((x2, t0, t1, t2, b2)), dact_)
        pair = lambda a, v: jnp.concatenate([a, v], axis=1)
        return (pair(dxa, dxv),), (pair(a0, v0), pair(a1, v1), pair(a2, v2), pair(ab, vb))

    dup, t0, t1, t2, tb = _col_call(
        f"ffnact_b{i}", ffn_bwd, 256, nb, [_rows(up, 256)], _ffn_params(P) + [_rows_as_param(dact, 128)],
        [BF16], [_a_row(2 * D_FF, 256)] * 4)
    G["ffn_conv_w"] = jnp.concatenate([t0, t1, t2], axis=0)
    G["ffn_conv_b"] = _deinterleave_ffn(tb)[0]
    du2 = _matmul(f"up_bx{i}", dup, W["ffn_w_up"], "nt")
    G["ffn_w_up"] = _matmul(f"up_bw{i}", sv["u2"], dup, "tn", out_dtype=BF16)

    def norm_bwd(x_, g_, du_, dres):
        (dxn,), (dg,) = _with_vjp(_rmsnorm_fn, 2, (0,), (1,))(x_, g_, du_)
        return (dxn + dres,), (dg,)

    ln2_g = P["ln2_g"] + on_ffn_grads(G)
    (dx1,), (G["ln2_g"],) = _split_res(_row_call(
        f"ln2_b{i}", lambda x_, du_, dres, g_: norm_bwd(x_, g_, du_, dres), D, 1,
        [_rows(x1, D), _rows(du2, D), _rows(dx2, D)], [_p_row(ln2_g, D)], [F32], [_a_row(D, D)]), 1)

    dmerged = _matmul(f"wo_bx{i}", dx1, W["w_o"], "nt", out_dtype=BF16)
    G["w_o"] = _matmul(f"wo_bw{i}", sv["merged"], dx1, "tn", out_dtype=BF16)
    def gate_bwd(g_, y_, dm, b_):
        return _with_vjp(lambda g, y, b: (jax.nn.sigmoid(g + b) * y,), 3, (0, 1), (2,))(g_, y_, b_, dm)

    dproj, dys, dbs = None, [], []
    for k, t in enumerate(("y_a", "y_b", "y_c")):
        dproj, dy_k, db_k = _row_call(
            f"gate_b{i}_{k}", gate_bwd, D, 1, [_rows(proj, D, k), _rows(sv[t], D), _rows(dmerged, D)],
            [_p_row(P["b_gate"], D, k)], [BF16, BF16], [_a_row(D, D)], into={0: (dproj, k, NP)})
        dys.append(dy_k)
        dbs.append(db_k)
    dya, dyb, dyc = dys
    G["b_gate"] = jnp.concatenate(dbs, axis=1)[0]

    dyc_pre = _matmul(f"wc_bx{i}", dyc, W["w_c"], "nt", out_dtype=BF16)
    G["w_c"] = _matmul(f"wc_bw{i}", sv["yc_pre"], dyc, "tn", out_dtype=BF16)

    def gnorm_bwd(y_, z_, dy_, w_):
        return _with_vjp(_gated_norm_fn, 3, (0, 1), (2,))(y_, z_, w_, dy_)

    dy_ssd, dproj, dnw = _row_call(
        f"ssdnorm_b{i}", gnorm_bwd, 512, 2,
        [_rows(sv["y_ssd"], 512), _rows(proj, 512, OFF_Z // 512), _rows(dyc_pre, 512)],
        [_p_row(P["ssd_norm_w"], 512)], [F32, BF16], [_a_row(D, 512)], into={1: (dproj, OFF_Z // 512, NP)})
    G["ssd_norm_w"] = dnw[0]
    dxbc_c, dproj, da_row, ddtb_row, dd_exp = _ssd_bwd(f"ssd_b{i}", sv["xbc_c"], proj, sv["states"], dy_ssd,
                                                       P["a_row"], P["dtb_row"], P["d_exp"], dproj)
    a_vec = P["a_row"][0, :SSD_HEADS]
    G["ssd_a_log"] = da_row[0, :SSD_HEADS] * a_vec
    G["ssd_dt_bias"] = ddtb_row[0, :SSD_HEADS]
    G["ssd_d"] = dd_exp.reshape(SSD_HEADS, HEAD_DIM).sum(axis=1)
    conv_params = [_p_row(P["ssd_conv_w"][k], 128) for k in range(4)] + [_p_row(P["ssd_conv_b"], 128)]

    def conv_bwd(x_, dy_, w0, w1, w2, w3, b_):
        return _with_vjp(_ssd_conv_fn, 6, (0,), (1, 2, 3, 4, 5))(x_, w0, w1, w2, w3, b_, dy_)

    dproj, c0, c1, c2, c3, cb = _col_call(
        f"ssdconv_b{i}", conv_bwd, 128, SSD_XBC // 128, [_rows(proj, 128, OFF_XBC // 128), _rows(dxbc_c, 128)],
        conv_params, [BF16], [_a_row(SSD_XBC, 128)] * 5, into={0: (dproj, OFF_XBC // 128, NP)})
    G["ssd_conv_w"] = jnp.concatenate([c0, c1, c2, c3], axis=0)
    G["ssd_conv_b"] = cb[0]

    dyb_pre = _matmul(f"wb_bx{i}", dyb, W["w_b"], "nt", out_dtype=BF16)
    G["w_b"] = _matmul(f"wb_bw{i}", sv["yb_pre"], dyb, "tn", out_dtype=BF16)
    pool_params = [(W["pool_w"], (N_DEV, None, 32, 256), lambda j, i_: (0, j, 0, 0)), _p_row(P["pool_scale"], 256)]

    def pool_bwd(x_, dy_, wg, sc):
        return _with_vjp(_pool_fn, 3, (0,), (1, 2))(x_, wg.astype(F32), sc, dy_)

    dproj, dwg, dsc = _col_call(
        f"pool_b{i}", pool_bwd, 256, 4, [_rows(proj, 256, OFF_POOL // 256), _rows(dyb_pre, 256)], pool_params, [BF16],
        [((N_DEV, 4, 32, 256), (N_DEV, None, 32, 256), lambda j, i_: (0, j, 0, 0)), _a_row(D, 256)],
        into={0: (dproj, OFF_POOL // 256, NP)})
    G["pool_w"] = dwg
    G["pool_scale"] = dsc[0]

    datt = _matmul(f"wa_bx{i}", dya, W["w_a"], "nt", out_dtype=BF16)
    G["w_a"] = _matmul(f"wa_bw{i}", sv["att"], dya, "tn", out_dtype=BF16)

    def merge_bwd(o0, o1, o2, l0, l1, l2, da_):
        return _with_vjp(_att_merge_fn, 6, (0, 1, 2, 3, 4, 5), ())(o0, o1, o2, l0, l1, l2, da_)

    dol = _row_call(f"attmerge_b{i}", merge_bwd, ATT_GW, 1,
                    [_rows(t, ATT_GW) for t in sv["att_o"] + sv["att_l"]] + [_rows(datt, ATT_GW)], [], [F32] * 6)
    g_rel = jnp.zeros((REL_BUCKETS, 18), F32)
    for gi in range(len(ATT_GROUPS)):
        dproj, gbp, gbc = _att_bwd(f"att_b{i}_{gi}", proj, gi, bias_tabs[gi][0], bias_tabs[gi][1],
                                   dol[gi], dol[3 + gi], dproj)
        oh = onehots[gi]
        gt = (jnp.einsum("hqk,qkb->bh", gbp, oh[0], precision=lax.Precision.HIGHEST)
              + jnp.einsum("hqk,qkb->bh", gbc, oh[1], precision=lax.Precision.HIGHEST))
        g_rel = g_rel.at[:, gi * 6:(gi + 1) * 6].add(gt)
    G["rel_bias"] = g_rel

    du = _matmul(f"inproj_bx{i}", dproj, W["w_in"], "nt")
    G["w_in"] = _matmul(f"inproj_bw{i}", sv["u"], dproj, "tn", out_dtype=BF16)
    ln1_g = P["ln1_g"] + on_sharded_grads(G)
    (dx,), (G["ln1_g"],) = _split_res(_row_call(
        f"ln1_b{i}", lambda x_, du_, dres, g_: norm_bwd(x_, g_, du_, dres), D, 1,
        [_rows(x, D), _rows(du, D), _rows(dx1, D)], [_p_row(ln1_g, D)], [F32], [_a_row(D, D)]), 1)
    G["ln1_g"] = G["ln1_g"][0]
    G["ln2_g"] = G["ln2_g"][0]
    return dx, G


def _rows_as_param(arr, cw):
    return (arr, (arr.shape[0], cw), lambda j, i: (0, j))


def _split_res(res, n_out):
    return tuple(res[:n_out]), tuple(res[n_out:])


def kernel(x, rel_bias, ln1_g, w_in, b_gate, w_a, pool_w, pool_scale, w_b, ssd_conv_w, ssd_conv_b, ssd_dt_bias, ssd_a_log, ssd_d, ssd_norm_w, w_c, w_o, ln2_g, ffn_w_up, ffn_conv_w, ffn_conv_b, ffn_w_down, final_g, loss_target, m_rel_bias, m_ln1_g, m_w_in, m_b_gate, m_w_a, m_pool_w, m_pool_scale, m_w_b, m_ssd_conv_w, m_ssd_conv_b, m_ssd_dt_bias, m_ssd_a_log, m_ssd_d, m_ssd_norm_w, m_w_c, m_w_o, m_ln2_g, m_ffn_w_up, m_ffn_conv_w, m_ffn_conv_b, m_ffn_w_down, m_final_g, v_rel_bias, v_ln1_g, v_w_in, v_b_gate, v_w_a, v_pool_w, v_pool_scale, v_w_b, v_ssd_conv_w, v_ssd_conv_b, v_ssd_dt_bias, v_ssd_a_log, v_ssd_d, v_ssd_norm_w, v_w_c, v_w_o, v_ln2_g, v_ffn_w_up, v_ffn_conv_w, v_ffn_conv_b, v_ffn_w_down, v_final_g):
    args = locals()
    wts = {n: args[n] for n in WEIGHTS}
    mom = {n: args["m_" + n] for n in WEIGHTS}
    var = {n: args["v_" + n] for n in WEIGHTS}
    names = list(SHARDED)

    onehots = _bucket_onehots()
    bias_tabs = []
    for gi in range(3):
        tab = rel_bias[:, gi * 6:(gi + 1) * 6]
        b = jnp.einsum("pqkb,bh->phqk", onehots[gi], tab, precision=lax.Precision.HIGHEST)
        bias_tabs.append((b[0], b[1]))

    def gather_start(tag, i, which, after):
        shards = [wts[n][i].astype(BF16) if n in MATMUL_WEIGHTS else wts[n][i] for n in which]
        return _exchange_start(f"gather_start{tag}", shards, False, after)

    def layer_params(i, which, landed):
        full = {n: _local_weight(f"local_{n}{i}", n, g) for n, g in zip(which, landed)}
        W = {n: full[n] for n in which if n in MATMUL_WEIGHTS}
        P = {n: [_row(full[n][k]) for k in range(full[n].shape[0])] for n in ("ssd_conv_w", "ffn_conv_w") if n in full}
        return W, P

    def replicated_params(i):
        return {"ln1_g": _row(ln1_g[i]), "ln2_g": _row(ln2_g[i]), "b_gate": _row(b_gate[i]),
                "pool_scale": _row(pool_scale[i]), "ssd_conv_b": _row(ssd_conv_b[i]),
                "ssd_norm_w": _row(ssd_norm_w[i]), "ffn_conv_b": _row(ffn_conv_b[i]),
                "a_row": _row(-jnp.exp(ssd_a_log[i]), 128), "dtb_row": _row(ssd_dt_bias[i], 128),
                "d_exp": _row(jnp.repeat(ssd_d[i], HEAD_DIM))}

    h = x.reshape(S, D)
    saved, Ws, Ps = [], [], []
    ffn = ["ffn_w_up", "ffn_conv_w", "ffn_w_down"]
    core = [n for n in names if n not in ffn]
    first, rest = ["w_in"], [n for n in core if n != "w_in"]
    landed_first = [_gather_chip_once("gather_w_in0", w_in[0].astype(BF16))]
    state_rest, token = gather_start("0r", 0, rest, landed_first[0])
    landed = None

    def late0(att):
        return layer_params(0, rest, _exchange_wait("gather_wait0r", state_rest, att))

    for i in range(DEPTH):
        P = replicated_params(i)
        W, P1 = layer_params(0, first, landed_first) if i == 0 else layer_params(i, core, landed)
        P.update(P1)
        state_ffn, tok = gather_start(f"{i}f", i, ffn, token if i == 0 else landed[0])
        token = tok if i > 0 else token + tok
        if i + 1 < DEPTH:
            state_next, tok = gather_start(f"{i + 1}c", i + 1, core, tok)
            token = token + tok
        P["ln1_g"] = P["ln1_g"] + token[0, 0]

        def late_ffn(x1, i=i, state_ffn=state_ffn):
            return layer_params(i, ffn, _exchange_wait(f"gather_wait{i}f", state_ffn, x1))

        h, sv = _layer_fwd(i, h, W, P, bias_tabs, late0 if i == 0 else None, late_ffn)
        Ws.append(W)
        Ps.append(dict(P, ln1_g=_row(ln1_g[i])))
        saved.append(sv)
        if i + 1 < DEPTH:
            landed = _exchange_wait(f"gather_wait{i + 1}c", state_next, h)

    def loss_bwd(x_, t_, g_):
        lval, vjp = jax.vjp(_loss_fn, x_, t_, g_)
        dx_, _, dg_ = vjp(jnp.ones_like(lval))
        return (dx_,), (dg_, jnp.broadcast_to(lval, (1, 128)))

    dh, g_final, loss_part = _row_call("loss", loss_bwd, D, 1, [_rows(h, D), _rows(loss_target.reshape(S, D), D)],
                                       [_p_row(_row(final_g), D)], [F32], [_a_row(D, D), _a_row(128, 128)])
    loss = lax.psum(loss_part[0, 0], MESH_AXES)

    grads = {n: [None] * DEPTH for n in WEIGHTS if n not in ("rel_bias", "final_g")}
    g_rel = jnp.zeros((REL_BUCKETS, 18), F32)
    slots = [dict() for _ in range(DEPTH)]
    pending = []
    for i in reversed(range(DEPTH)):
        started = []

        def start_group(tag, group, G, after, i=i, started=started):
            parts = [_device_blocks(f"blocks_{n}{i}", n, G[n]) for n in group]
            state, token = _exchange_start(f"scatter_start{i}{tag}", parts, True, after)
            started.append((i, tag, group, state))
            return token[0, 0]

        dh, G = _layer_bwd(i, dh, saved[i], Ws[i], Ps[i], bias_tabs, onehots,
                           lambda G: start_group("f", ffn, G, G["ffn_conv_b"]),
                           lambda G: start_group("c", core, G, G["b_gate"]))
        for j, tag, group, state in pending:
            slots[j].update(zip(group, _exchange_wait(f"scatter_wait{j}{tag}", state, dh)))
        pending = started
        g_rel = g_rel + G.pop("rel_bias")
        for n, g in G.items():
            grads[n][i] = g
    grad_x = dh.reshape(1, S, D)
    local = {n: jnp.stack(grads[n]) for n in grads if n not in SHARDED}
    local["rel_bias"] = g_rel
    local["final_g"] = g_final[0]
    out = {}

    def pack(d):
        flat = jnp.concatenate([d[n].reshape(-1).astype(F32) for n in REPLICATED])
        rows = -(-flat.shape[0] // (8 * 128)) * 8
        return jnp.pad(flat, (0, rows * 128 - flat.shape[0])).reshape(rows, 128)

    (rep_slots,) = _exchange("gather_small_grads", [pack(local)], scatter=False)
    rep = _adamw("adamw_small", rep_slots, pack(wts), pack(mom), pack(var))
    off = 0
    for n in REPLICATED:
        sz = int(np.prod(wts[n].shape))
        out[n] = [t.reshape(-1)[off:off + sz].reshape(wts[n].shape) for t in rep]
        off += sz

    def flat2(n):
        shp = wts[n].shape
        r, c = int(np.prod(shp[:-1])), shp[-1]
        return r, c, wts[n].reshape(r, c), mom[n].reshape(r, c), var[n].reshape(r, c)

    chain = {}
    done = rep[0][0, 0]
    for n in names:
        if n in MATMUL_WEIGHTS:
            r, c, w2, m2, v2 = flat2(n)
            res = None
            for i in (3, 2, 1):
                res = _adamw(f"adamw_{n}{i}", slots[i][n].reshape(N_DEV, r // DEPTH, c), w2, m2, v2,
                             first_row=i * (r // DEPTH), prev=res)
            chain[n] = res
            done = done + res[0][-1, 0]
    for j, tag, group, state in pending:
        slots[j].update(zip(group, _exchange_wait(f"scatter_wait{j}{tag}", state, done.reshape(1, 1))))
    for n in names:
        r, c, w2, m2, v2 = flat2(n)
        if n in MATMUL_WEIGHTS:
            res = _adamw(f"adamw_{n}0", slots[0][n].reshape(N_DEV, r // DEPTH, c), w2, m2, v2, first_row=0, prev=chain[n])
        else:
            stacked = jnp.stack([slots[i][n] for i in range(DEPTH)], axis=1)
            res = _adamw("adamw_" + n, stacked.reshape(N_DEV, r, c), w2, m2, v2)
        out[n] = [t.reshape(wts[n].shape) for t in res]

    return (loss, grad_x, *[out[n][0] for n in WEIGHTS], *[out[n][1] for n in WEIGHTS],
            *[out[n][2] for n in WEIGHTS], *[out[n][3] for n in WEIGHTS])
```

```python
import functools
import math

import numpy as np
import jax
import jax.numpy as jnp
from jax import lax
from jax.experimental import pallas as pl
from jax.experimental.pallas import tpu as pltpu

F32 = jnp.float32
BF16 = jnp.bfloat16

N_DEV = 8
MESH_AXES = ("x", "y", "c")
S = 4096
D = 1024
DEPTH = 4
HEAD_DIM = 64
ATT_W = 1152
ATT_GW = 384
ATT_GROUPS = ((128, 1), (512, 4), (2048, 16))
ATT_BLOCK = 128
REL_BUCKETS = 32
REL_MAX_DISTANCE = 2048
POOL_WINDOWS = (2, 4, 8, 16)
SSD_HEADS = 16
SSD_CHUNK = 128
SSD_XBC = 1536
D_FF = 2816
IN_WIDTH = 10128
EPS = 1e-6
NEG = -1e30

OFF_GATE, OFF_POOL, OFF_Z, OFF_XBC, OFF_DT, OFF_QKV = 0, 3072, 4096, 5120, 6656, 6912
NP = 10368
DT_PAD = 128
QKV_W = 3 * 2 * HEAD_DIM

ADAM_LR, ADAM_B1, ADAM_B2, ADAM_EPS, ADAM_WD, ADAM_STEP = 0.001, 0.9, 0.999, 1e-08, 0.01, 10

VMEM_LIMIT = 52 * 1024 * 1024


def _cparams(sem=None):
    return pltpu.CompilerParams(dimension_semantics=sem, vmem_limit_bytes=VMEM_LIMIT)


def _dot(a, b, ca, cb):
    return lax.dot_general(a.astype(BF16), b.astype(BF16), (((ca,), (cb,)), ((), ())), preferred_element_type=F32)


@jax.custom_vjp
def _mm(a, b):
    return _dot(a, b, 1, 0)


def _mm_fwd(a, b):
    return _mm(a, b), (a, b)


def _mm_bwd(res, g):
    a, b = res
    return _dot(g, b, 1, 1).astype(a.dtype), _dot(a, g, 0, 0).astype(b.dtype)


_mm.defvjp(_mm_fwd, _mm_bwd)


@jax.custom_vjp
def _mm_nt(a, b):
    return _dot(a, b, 1, 1)


def _mm_nt_fwd(a, b):
    return _mm_nt(a, b), (a, b)


def _mm_nt_bwd(res, g):
    a, b = res
    return _dot(g, b, 1, 0).astype(a.dtype), _dot(g, a, 0, 0).astype(b.dtype)


_mm_nt.defvjp(_mm_nt_fwd, _mm_nt_bwd)


@jax.custom_vjp
def _mm_tn(a, b):
    return _dot(a, b, 0, 0)


def _mm_tn_fwd(a, b):
    return _mm_tn(a, b), (a, b)


def _mm_tn_bwd(res, g):
    a, b = res
    return _dot(b, g, 1, 1).astype(a.dtype), _dot(a, g, 1, 0).astype(b.dtype)


_mm_tn.defvjp(_mm_tn_fwd, _mm_tn_bwd)


def _shift_impl(x, j):
    n = x.shape[0]
    if j == 0:
        return x
    r = pltpu.roll(x, j % n, axis=0)
    t = lax.broadcasted_iota(jnp.int32, x.shape, 0)
    mask = (t >= j) if j > 0 else (t < n + j)
    return jnp.where(mask, r, 0.0)


@functools.partial(jax.custom_vjp, nondiff_argnums=(1,))
def _shift(x, j):
    return _shift_impl(x, j)


_shift.defvjp(lambda x, j: (_shift_impl(x, j), None), lambda j, _, g: (_shift_impl(g, -j),))


def _tri(lower):
    r = lax.broadcasted_iota(jnp.int32, (SSD_CHUNK, SSD_CHUNK), 0)
    c = lax.broadcasted_iota(jnp.int32, (SSD_CHUNK, SSD_CHUNK), 1)
    return (r >= c) if lower else (r <= c)


def _dot_hi(a, b):
    return lax.dot_general(a, b, (((1,), (0,)), ((), ())), precision=lax.Precision.HIGHEST,
                           preferred_element_type=F32)


@jax.custom_vjp
def _cumsum_rows(a):
    return _dot_hi(_tri(True).astype(F32), a)


_cumsum_rows.defvjp(lambda a: (_cumsum_rows(a), None), lambda _, g: (_dot_hi(_tri(False).astype(F32), g),))


@jax.custom_vjp
def _softplus(x):
    return jnp.maximum(x, 0.0) + jnp.log(1.0 + jnp.exp(-jnp.abs(x)))


_softplus.defvjp(lambda x: (_softplus(x), x), lambda x, g: (g * jax.nn.sigmoid(x),))


def _silu(x):
    return x * jax.nn.sigmoid(x)


def _rows(arr, cw, off=0, lead=None, roff=0):
    return (arr, cw, off, lead, roff)


def _tiled(name, fn, grid, tm, rows, params, outs, accs=(), out_roff=0, prev_outs=None, into=None):
    into = into or {}
    ncol, nrow = grid
    in_specs, operands = [], []
    for arr, cw, off, lead, roff in rows:
        if lead is None:
            in_specs.append(pl.BlockSpec((tm, cw), functools.partial(lambda j, i, off, roff: (roff + i, off + j),
                                                                     off=off, roff=roff)))
        else:
            in_specs.append(pl.BlockSpec((None, tm, cw), functools.partial(
                lambda j, i, off, lead, roff: (lead, roff + i, off + j), off=off, lead=lead, roff=roff)))
        operands.append(arr)
    for arr, bs, im in params:
        in_specs.append(pl.BlockSpec(bs, im))
        operands.append(arr)
    out_specs, out_shape = [], []
    for k, (n_rows, cw, dt) in enumerate(outs):
        _, coff, total = into.get(k, (None, 0, ncol * cw))
        out_specs.append(pl.BlockSpec((tm, cw), functools.partial(lambda j, i, r, c: (r + i, c + j), r=out_roff, c=coff)))
        out_shape.append(jax.ShapeDtypeStruct((n_rows, total), dt))
    for shape, bs, im in accs:
        out_specs.append(pl.BlockSpec(bs, im))
        out_shape.append(jax.ShapeDtypeStruct(shape, F32))
    n_in, n_out = len(operands), len(outs)
    aliases = {}
    earlier = dict(enumerate(prev_outs)) if prev_outs is not None else {}
    earlier.update({k: v[0] for k, v in into.items() if v[0] is not None})
    for k, p in sorted(earlier.items()):
        aliases[len(operands)] = k
        in_specs.append(pl.BlockSpec(memory_space=pl.ANY))
        operands.append(p)

    n_all = len(operands)

    def body(*refs):
        vals = [r[...] for r in refs[:n_in]]
        o_vals, a_vals = fn(*vals)
        for r, v in zip(refs[n_all:n_all + n_out], o_vals):
            r[...] = v.astype(r.dtype)
        i = pl.program_id(1)
        for r, v in zip(refs[n_all + n_out:], a_vals):
            @pl.when(i == 0)
            def _(r=r, v=v):
                r[...] = v.astype(r.dtype)

            @pl.when(i > 0)
            def _(r=r, v=v):
                r[...] += v.astype(r.dtype)

    res = pl.pallas_call(body, grid=grid, in_specs=in_specs, out_specs=out_specs, out_shape=out_shape, name=name,
                         input_output_aliases=aliases, compiler_params=_cparams(("arbitrary", "arbitrary")))(*operands)
    return list(res)


def _with_vjp(fn, n_prim, want_out, want_acc):
    def f(*args):
        prim, g = args[:n_prim], args[n_prim:]
        outs, vjp = jax.vjp(lambda *a: fn(*a), *prim)
        d = vjp(tuple(gi.astype(o.dtype) for gi, o in zip(g, outs)))
        return tuple(d[k] for k in want_out), tuple(d[k] for k in want_acc)
    return f


def _p_row(arr, cw, off=0):
    return (arr, (1, cw), functools.partial(lambda j, i, off: (0, off + j), off=off))


def _a_row(n, cw):
    return ((1, n), (1, cw), lambda j, i: (0, j))


def _pick(n, cap, mult):
    best = None
    for t in range(mult, min(n, cap) + 1, mult):
        if n % t == 0:
            best = t
    return best if best is not None else n


def _matmul(name, a, b, mode, add=None, out_dtype=F32):
    if mode == "nn":
        (M, K), N = a.shape, b.shape[1]
    elif mode == "nt":
        (M, K), N = a.shape, b.shape[0]
    else:
        (K, M), N = a.shape, b.shape[1]
    tn = _pick(N, 1536, 128)
    k_cap = 2048 if mode == "tn" else 3456
    tk = K if K <= k_cap else _pick(K, k_cap, 128)
    nk = K // tk
    tm = _pick(M, 1408, 128) if mode == "tn" else _pick(M, 1024, 8)
    a_bytes, b_bytes = a.size * a.dtype.itemsize, b.size * b.dtype.itemsize
    swap = nk == 1 and a_bytes * (N // tn) + b_bytes < b_bytes * (M // tm) + a_bytes
    ij = (lambda g0, g1: (g1, g0)) if swap else (lambda g0, g1: (g0, g1))

    def spec(block, index):
        return pl.BlockSpec(block, lambda g0, g1, k: index(*ij(g0, g1), k))

    if mode == "nn":
        a_spec = spec((tm, tk), lambda i, j, k: (i, k))
        b_spec = spec((tk, tn), lambda i, j, k: (k, j))
        ca, cb = 1, 0
    elif mode == "nt":
        a_spec = spec((tm, tk), lambda i, j, k: (i, k))
        b_spec = spec((tn, tk), lambda i, j, k: (j, k))
        ca, cb = 1, 1
    else:
        a_spec = spec((tk, tm), lambda i, j, k: (k, i))
        b_spec = spec((tk, tn), lambda i, j, k: (k, j))
        ca, cb = 0, 0
    in_specs, operands = [a_spec, b_spec], [a, b]
    if add is not None:
        in_specs.append(spec((tm, tn), lambda i, j, k: (i, j)))
        operands.append(add)

    def finish(r, refs, o_ref):
        if add is not None:
            r = r + refs[2][...]
        o_ref[...] = r.astype(o_ref.dtype)

    def body_single(*refs):
        finish(_dot(refs[0][...], refs[1][...], ca, cb), refs, refs[-1])

    def body_multi(*refs):
        o_ref, acc_ref = refs[-2], refs[-1]
        k = pl.program_id(2)
        d = _dot(refs[0][...], refs[1][...], ca, cb)

        @pl.when(k == 0)
        def _():
            acc_ref[...] = d

        @pl.when(jnp.logical_and(k > 0, k < nk - 1))
        def _():
            acc_ref[...] += d

        @pl.when(k == nk - 1)
        def _():
            finish(acc_ref[...] + d, refs, o_ref)

    grid = (N // tn, M // tm, nk) if swap else (M // tm, N // tn, nk)
    return pl.pallas_call(
        body_single if nk == 1 else body_multi, grid=grid, in_specs=in_specs,
        out_specs=spec((tm, tn), lambda i, j, k: (i, j)),
        out_shape=jax.ShapeDtypeStruct((M, N), out_dtype),
        scratch_shapes=[] if nk == 1 else [pltpu.VMEM((tm, tn), F32)], name=name,
        compiler_params=_cparams(("parallel", "parallel", "arbitrary")))(*operands)


def _seg_copies(segs, c):
    out = []
    for lo, hi, dst in segs:
        n = lo
        while n < hi:
            p = n // c
            w = min(hi, (p + 1) * c) - n
            out.append((p, n - p * c, w, dst + n - lo))
            n += w
    return out


def _col_assemble(name, blocks, copies, zeros, n_out):
    _, R, c = blocks.shape
    tm = R if R <= 128 else 128

    def body(b_ref, o_ref):
        for p, s, w, d in copies:
            o_ref[:, d:d + w] = b_ref[p, :, s:s + w]
        for lo, hi in zeros:
            o_ref[:, lo:hi] = jnp.zeros((tm, hi - lo), o_ref.dtype)

    return pl.pallas_call(
        body, grid=(R // tm,), in_specs=[pl.BlockSpec((N_DEV, tm, c), lambda i: (0, i, 0))],
        out_specs=pl.BlockSpec((tm, n_out), lambda i: (i, 0)),
        out_shape=jax.ShapeDtypeStruct((R, n_out), blocks.dtype), name=name, compiler_params=_cparams(("parallel",)))(blocks)


def _col_split(name, full, copies, c, dtype):
    R, n = full.shape
    tm = R if R <= 128 else 128

    def body(f_ref, o_ref):
        for p, s, w, d in copies:
            o_ref[p, :, s:s + w] = f_ref[:, d:d + w].astype(dtype)

    return pl.pallas_call(
        body, grid=(R // tm,), in_specs=[pl.BlockSpec((tm, n), lambda i: (i, 0))],
        out_specs=pl.BlockSpec((N_DEV, tm, c), lambda i: (0, i, 0)),
        out_shape=jax.ShapeDtypeStruct((N_DEV, R, c), dtype), name=name, compiler_params=_cparams(("parallel",)))(full)


def _rmsnorm_fn(x, g):
    x = x.astype(F32)
    return (x * lax.rsqrt(jnp.mean(x * x, axis=-1, keepdims=True) + EPS) * g,)


def _gate_merge_fn(g0, g1, g2, ya, yb, yc, b0, b1, b2):
    return (jax.nn.sigmoid(g0 + b0) * ya + jax.nn.sigmoid(g1 + b1) * yb + jax.nn.sigmoid(g2 + b2) * yc,)


def _gated_norm_fn(y, z, w):
    t = y * _silu(z)
    return (t * lax.rsqrt(jnp.mean(t * t, axis=-1, keepdims=True) + EPS) * w,)


def _att_merge_fn(o0, o1, o2, l0, l1, l2):
    m = lax.stop_gradient(jnp.maximum(jnp.maximum(l0, l1), l2))
    e0, e1, e2 = jnp.exp(l0 - m), jnp.exp(l1 - m), jnp.exp(l2 - m)
    return ((e0 * o0 + e1 * o1 + e2 * o2) / (e0 + e1 + e2),)


def _loss_fn(x, tgt, g):
    (y,) = _rmsnorm_fn(x, g)
    err = y - tgt
    return 0.5 * jnp.sum(jnp.mean(err * err, axis=-1, keepdims=True), axis=0, keepdims=True)


def _pool_fn(x, wg, scale):
    g = pl.program_id(0)
    s2 = x + _shift(x, 1)
    s4 = s2 + _shift(s2, 2)
    s8 = s4 + _shift(s4, 4)
    s16 = s8 + _shift(s8, 8)
    win = ((g == 0).astype(F32) * s2 + (g == 1).astype(F32) * s4 + (g == 2).astype(F32) * s8
           + (g == 3).astype(F32) * s16)
    t = lax.broadcasted_iota(jnp.int32, (x.shape[0], 1), 0) + 1
    cnt = jnp.minimum(t, jnp.left_shift(2, g)).astype(F32)
    d = win / cnt - x
    return (_mm(d, wg.reshape(256, 256)) * scale,)


def _dwconv(x, taps, b):
    k = len(taps)
    y = taps[k - 1] * x + b
    for i in range(k - 1):
        y = y + taps[i] * _shift(x, k - 1 - i)
    return y


def _ssd_conv_fn(x, w0, w1, w2, w3, b):
    return (_silu(_dwconv(x, (w0, w1, w2, w3), b)),)


def _ffn_act_fn(xa, xv, a0, a1, a2, ab, v0, v1, v2, vb):
    xa, xv = xa.astype(F32), xv.astype(F32)
    return (_silu(_dwconv(xa, (a0, a1, a2), ab)) * _dwconv(xv, (v0, v1, v2), vb),)


@jax.custom_vjp
def _halves(x):
    return x[:ATT_BLOCK], x[ATT_BLOCK:]


_halves.defvjp(lambda x: (_halves(x), None), lambda _, g: (jnp.concatenate([g[0], g[1]], axis=0),))


def _att_block(q, kp, kc, vp, vc, bpa, bpb, bca, bcb, prev_ok):
    n = ATT_BLOCK
    lane = lax.broadcasted_iota(jnp.int32, (1, 2 * HEAD_DIM), 1)
    ma = (lane < HEAD_DIM).astype(F32)
    mb = 1.0 - ma
    q = q.astype(F32) * (1.0 / math.sqrt(HEAD_DIM))
    q2 = jnp.concatenate([q * ma, q * mb], axis=0)
    qi = lax.broadcasted_iota(jnp.int32, (2 * n, n), 0) & (n - 1)
    kj = lax.broadcasted_iota(jnp.int32, (2 * n, n), 1)
    sp = jnp.where(jnp.logical_and(kj >= qi, prev_ok), _mm_nt(q2, kp) + jnp.concatenate([bpa, bpb], axis=0), NEG)
    sc = jnp.where(kj <= qi, _mm_nt(q2, kc) + jnp.concatenate([bca, bcb], axis=0), NEG)
    m = lax.stop_gradient(jnp.maximum(jnp.max(sp, axis=1, keepdims=True), jnp.max(sc, axis=1, keepdims=True)))
    pp = jnp.exp(sp - m)
    pc = jnp.exp(sc - m)
    l = jnp.sum(pp, axis=1, keepdims=True) + jnp.sum(pc, axis=1, keepdims=True)
    oa, ob = _halves((_mm(pp, vp) + _mm(pc, vc)) / l)
    la, lb = _halves((m + jnp.log(l)) * jnp.ones((1, 2 * HEAD_DIM), F32))
    return oa * ma + ob * mb, la * ma + lb * mb


def _att_slab(dil):
    nbk = 8 if dil == 1 else 1
    t = ATT_BLOCK * dil * nbk
    return nbk, t, S // t


def _att_in_specs(gi, t):
    def spec(which, prev):
        col = OFF_QKV // 128 + gi * 9 + which

        def index(p, j, col=col, prev=prev):
            jj = jnp.minimum(j, S // t - 1)
            return (jnp.maximum(jj - 1, 0) if prev else jj, col + 3 * p)
        return pl.BlockSpec((t, 2 * HEAD_DIM), index)
    return [spec(0, False), spec(1, False), spec(1, True), spec(2, False), spec(2, True)]


def _bias_specs():
    return [pl.BlockSpec((None, ATT_BLOCK, ATT_BLOCK), functools.partial(lambda p, j, hh: (2 * p + hh, 0, 0), hh=hh))
            for hh in (0, 1)]


def _att_units(dil, nbk, body):
    def per_residue(r, carry):
        for b in range(nbk):
            rows = pl.ds(b * ATT_BLOCK * dil + r, ATT_BLOCK, stride=dil)
            prev = pl.ds(((b - 1) % nbk) * ATT_BLOCK * dil + r, ATT_BLOCK, stride=dil)
            body(b, rows, prev, b > 0)
        return carry
    if dil == 1:
        per_residue(0, 0)
    else:
        lax.fori_loop(0, dil, per_residue, 0, unroll=min(dil, 8))


def _att_fwd(name, proj, gi, bias_p, bias_c):
    dil = ATT_GROUPS[gi][1]
    nbk, t, ns = _att_slab(dil)
    bsp = _bias_specs()
    out_spec = pl.BlockSpec((t, 2 * HEAD_DIM), lambda p, j: (j, p))

    def body(q_ref, kc_ref, kp_ref, vc_ref, vp_ref, bpa, bpb, bca, bcb, o_ref, l_ref):
        first = pl.program_id(1) == 0
        biases = (bpa[...], bpb[...], bca[...], bcb[...])

        def unit(b, rows, prev, in_slab):
            kp = kc_ref[prev, :] if in_slab else kp_ref[prev, :]
            vp = vc_ref[prev, :] if in_slab else vp_ref[prev, :]
            prev_ok = True if in_slab else jnp.logical_not(first)
            o, lse = _att_block(q_ref[rows, :], kp, kc_ref[rows, :], vp, vc_ref[rows, :], *biases, prev_ok)
            o_ref[rows, :] = o
            l_ref[rows, :] = lse

        _att_units(dil, nbk, unit)

    shp = jax.ShapeDtypeStruct((S, ATT_GW), F32)
    return pl.pallas_call(
        body, grid=(3, ns), in_specs=_att_in_specs(gi, t) + [bsp[0], bsp[1], bsp[0], bsp[1]],
        out_specs=[out_spec, out_spec], out_shape=[shp, shp], name=name,
        compiler_params=_cparams(("arbitrary",) * 2))(proj, proj, proj, proj, proj, bias_p, bias_p, bias_c, bias_c)


def _att_bwd(name, proj, gi, bias_p, bias_c, do, dl, dproj):
    dil = ATT_GROUPS[gi][1]
    nbk, t, ns = _att_slab(dil)
    bsp = _bias_specs()
    blk = (t, 2 * HEAD_DIM)
    cur = pl.BlockSpec(blk, lambda p, j: (jnp.minimum(j, ns - 1), p))
    done = pl.BlockSpec((t, QKV_W), lambda p, j: (jnp.maximum(j - 1, 0), OFF_QKV // QKV_W + gi * 3 + p))
    gsp = pl.BlockSpec((None, ATT_BLOCK, ATT_BLOCK), lambda p, j: (p, 0, 0))

    def body(q_ref, kc_ref, kp_ref, vc_ref, vp_ref, bpa, bpb, bca, bcb, do_ref, dl_ref, _,
             dqkv_ref, gpa, gpb, gca, gcb, accq, acck, accv):
        j = pl.program_id(1)
        mine, other = acck.at[j % 2], acck.at[1 - j % 2]
        mine_v, other_v = accv.at[j % 2], accv.at[1 - j % 2]
        dq_ref, other_q = accq.at[j % 2], accq.at[1 - j % 2]

        @pl.when(j == 0)
        def _():
            for g in (gpa, gpb, gca, gcb):
                g[...] = jnp.zeros_like(g)
            other[...] = jnp.zeros_like(other)
            other_v[...] = jnp.zeros_like(other_v)
            other_q[...] = jnp.zeros_like(other_q)

        @pl.when(j < ns)
        def _():
            mine[...] = jnp.zeros_like(mine)
            mine_v[...] = jnp.zeros_like(mine_v)
            biases = (bpa[...], bpb[...], bca[...], bcb[...])

            def unit(b, rows, prev, in_slab):
                kp = kc_ref[prev, :] if in_slab else kp_ref[prev, :]
                vp = vc_ref[prev, :] if in_slab else vp_ref[prev, :]
                prev_ok = True if in_slab else j > 0
                prim = (q_ref[rows, :], kp, kc_ref[rows, :], vp, vc_ref[rows, :]) + biases
                _, vjp = jax.vjp(lambda *a: _att_block(*a, prev_ok), *prim)
                dq, dkp, dkc, dvp, dvc, dpa, dpb, dca, dcb = vjp((do_ref[rows, :], dl_ref[rows, :]))
                dq_ref[rows, :] = dq
                mine[rows, :] += dkc
                mine_v[rows, :] += dvc
                tgt, tgt_v = (mine, mine_v) if in_slab else (other, other_v)
                tgt[prev, :] += dkp
                tgt_v[prev, :] += dvp
                gpa[...] += dpa
                gpb[...] += dpb
                gca[...] += dca
                gcb[...] += dcb

            _att_units(dil, nbk, unit)

        w = 2 * HEAD_DIM
        dqkv_ref[:, 0:w] = other_q[...].astype(BF16)
        dqkv_ref[:, w:2 * w] = other[...].astype(BF16)
        dqkv_ref[:, 2 * w:3 * w] = other_v[...].astype(BF16)

    gshp = jax.ShapeDtypeStruct((3, ATT_BLOCK, ATT_BLOCK), F32)
    res = pl.pallas_call(
        body, grid=(3, ns + 1),
        in_specs=_att_in_specs(gi, t) + [bsp[0], bsp[1], bsp[0], bsp[1], cur, cur, pl.BlockSpec(memory_space=pl.ANY)],
        out_specs=[done, gsp, gsp, gsp, gsp],
        out_shape=[jax.ShapeDtypeStruct((S, NP), BF16), gshp, gshp, gshp, gshp],
        input_output_aliases={11: 0},
        scratch_shapes=[pltpu.VMEM((2,) + blk, F32)] * 3, name=name,
        compiler_params=_cparams(("arbitrary",) * 2))(proj, proj, proj, proj, proj, bias_p, bias_p, bias_c, bias_c, do, dl,
                                                      dproj)
    dproj, gpa, gpb, gca, gcb = res
    heads = lambda a, b: jnp.stack([a, b], axis=1).reshape(6, ATT_BLOCK, ATT_BLOCK)
    return dproj, heads(gpa, gpb), heads(gca, gcb)


N_PAIR = SSD_HEADS // 2


def _ssd_chunk(xs, bs, cs_in, dt_raw, hs, a_row, dtb_row, ds):
    lane = lax.broadcasted_iota(jnp.int32, (1, 128), 1)
    row = lax.broadcasted_iota(jnp.int32, (128, 1), 0)
    tril = _tri(True)
    dt = _softplus(dt_raw + dtb_row)
    acs = _cumsum_rows(dt * a_row)
    acs_t = acs.T
    gmat = [_mm_nt(cs_in[g], bs[g]) for g in range(2)]
    lo = lane < HEAD_DIM
    lo_r = row < HEAD_DIM
    last = (row == SSD_CHUNK - 1).astype(F32)
    ys, hn = [], []
    for p in range(N_PAIR):
        g = p // (N_PAIR // 2)
        col, dtc, mm, clast = [], [], [], []
        for hh in range(2):
            h = 2 * p + hh
            oh = (lane == h).astype(F32)
            c_col = jnp.sum(acs * oh, axis=1, keepdims=True)
            c_row = jnp.sum(acs_t * (row == h).astype(F32), axis=0, keepdims=True)
            col.append(c_col)
            dtc.append(jnp.sum(dt * oh, axis=1, keepdims=True))
            clast.append(jnp.sum(c_col * last, axis=0, keepdims=True))
            mm.append(gmat[g] * jnp.exp(jnp.where(tril, c_col - c_row, NEG)))
        x = xs[p]
        xd = x * jnp.where(lo, dtc[0], dtc[1])
        y = jnp.where(lo, _mm(mm[0], xd), _mm(mm[1], xd))
        y = y + jnp.where(lo, jnp.exp(col[0]), jnp.exp(col[1])) * _mm_nt(cs_in[g], hs[p])
        ys.append(y + ds[p] * x)
        dec = jnp.where(lo, jnp.exp(clast[0] - col[0]), jnp.exp(clast[1] - col[1]))
        hn.append(hs[p] * jnp.where(lo_r, jnp.exp(clast[0]), jnp.exp(clast[1])) + _mm_tn(xd * dec, bs[g]))
    return tuple(ys), tuple(hn)


def _ssd_load(xbc_ref, dt_ref, a_ref, dtb_ref, d_ref):
    xs = tuple(xbc_ref[:, 128 * p:128 * (p + 1)] for p in range(N_PAIR))
    bs = tuple(xbc_ref[:, D + 128 * g:D + 128 * (g + 1)] for g in range(2))
    cs = tuple(xbc_ref[:, D + 256 + 128 * g:D + 256 + 128 * (g + 1)] for g in range(2))
    ds = tuple(d_ref[:, 128 * p:128 * (p + 1)] for p in range(N_PAIR))
    return xs, bs, cs, dt_ref[...], a_ref[...], dtb_ref[...], ds


def _ssd_fwd(name, xbc_c, proj, a_row, dtb_row, d_exp):
    nc = S // SSD_CHUNK
    prow = lambda n: pl.BlockSpec((1, n), lambda c: (0, 0))

    def body(xbc_ref, dt_ref, a_ref, dtb_ref, d_ref, y_ref, st_ref, h_ref):
        @pl.when(pl.program_id(0) == 0)
        def _():
            h_ref[...] = jnp.zeros_like(h_ref)

        xs, bs, cs, dt_raw, a, dtb, ds = _ssd_load(xbc_ref, dt_ref, a_ref, dtb_ref, d_ref)
        hs = tuple(h_ref[p] for p in range(N_PAIR))
        ys, hn = _ssd_chunk(xs, bs, cs, dt_raw, hs, a, dtb, ds)
        for p in range(N_PAIR):
            y_ref[:, 128 * p:128 * (p + 1)] = ys[p]
            st_ref[p] = hs[p]
            h_ref[p] = hn[p]

    return pl.pallas_call(
        body, grid=(nc,),
        in_specs=[pl.BlockSpec((SSD_CHUNK, SSD_XBC), lambda c: (c, 0)),
                  pl.BlockSpec((SSD_CHUNK, DT_PAD), lambda c: (c, OFF_DT // DT_PAD)),
                  prow(128), prow(128), prow(D)],
        out_specs=[pl.BlockSpec((SSD_CHUNK, D), lambda c: (c, 0)),
                   pl.BlockSpec((None, N_PAIR, 128, 128), lambda c: (c, 0, 0, 0))],
        out_shape=[jax.ShapeDtypeStruct((S, D), F32), jax.ShapeDtypeStruct((nc, N_PAIR, 128, 128), F32)],
        scratch_shapes=[pltpu.VMEM((N_PAIR, 128, 128), F32)], name=name,
        compiler_params=_cparams(("arbitrary",)))(xbc_c, proj, a_row, dtb_row, d_exp)


def _ssd_bwd(name, xbc_c, proj, states, dy, a_row, dtb_row, d_exp, dproj):
    nc = S // SSD_CHUNK
    prow = lambda n: pl.BlockSpec((1, n), lambda i: (0, 0))
    rc = lambda i: nc - 1 - i

    def body(xbc_ref, dt_ref, st_ref, dy_ref, a_ref, dtb_ref, d_ref, _, dxbc_ref, ddt_ref, da_ref, ddtb_ref, dd_ref, e_ref):
        i = pl.program_id(0)

        @pl.when(i == 0)
        def _():
            e_ref[...] = jnp.zeros_like(e_ref)
            da_ref[...] = jnp.zeros_like(da_ref)
            ddtb_ref[...] = jnp.zeros_like(ddtb_ref)
            dd_ref[...] = jnp.zeros_like(dd_ref)

        xs, bs, cs, dt_raw, a, dtb, ds = _ssd_load(xbc_ref, dt_ref, a_ref, dtb_ref, d_ref)
        hs = tuple(st_ref[p] for p in range(N_PAIR))
        _, vjp = jax.vjp(_ssd_chunk, xs, bs, cs, dt_raw, hs, a, dtb, ds)
        dys = tuple(dy_ref[:, 128 * p:128 * (p + 1)] for p in range(N_PAIR))
        es = tuple(e_ref[p] for p in range(N_PAIR))
        dxs, dbs, dcs, ddt, dhs, da, ddtb, dds = vjp((dys, es))
        for p in range(N_PAIR):
            dxbc_ref[:, 128 * p:128 * (p + 1)] = dxs[p]
            e_ref[p] = dhs[p]
            dd_ref[:, 128 * p:128 * (p + 1)] += dds[p]
        for g in range(2):
            dxbc_ref[:, D + 128 * g:D + 128 * (g + 1)] = dbs[g]
            dxbc_ref[:, D + 256 + 128 * g:D + 256 + 128 * (g + 1)] = dcs[g]
        ddt_ref[:, :DT_PAD] = ddt.astype(BF16)
        ddt_ref[:, DT_PAD:] = jnp.zeros((SSD_CHUNK, OFF_QKV - OFF_DT - DT_PAD), BF16)
        da_ref[...] += da
        ddtb_ref[...] += ddtb

    dt_w = OFF_QKV - OFF_DT
    return pl.pallas_call(
        body, grid=(nc,),
        in_specs=[pl.BlockSpec((SSD_CHUNK, SSD_XBC), lambda i: (rc(i), 0)),
                  pl.BlockSpec((SSD_CHUNK, DT_PAD), lambda i: (rc(i), OFF_DT // DT_PAD)),
                  pl.BlockSpec((None, N_PAIR, 128, 128), lambda i: (rc(i), 0, 0, 0)),
                  pl.BlockSpec((SSD_CHUNK, D), lambda i: (rc(i), 0)),
                  prow(128), prow(128), prow(D), pl.BlockSpec(memory_space=pl.ANY)],
        out_specs=[pl.BlockSpec((SSD_CHUNK, SSD_XBC), lambda i: (rc(i), 0)),
                   pl.BlockSpec((SSD_CHUNK, dt_w), lambda i: (rc(i), OFF_DT // dt_w)),
                   prow(128), prow(128), prow(D)],
        out_shape=[jax.ShapeDtypeStruct((S, SSD_XBC), F32), jax.ShapeDtypeStruct((S, NP), BF16),
                   jax.ShapeDtypeStruct((1, 128), F32), jax.ShapeDtypeStruct((1, 128), F32),
                   jax.ShapeDtypeStruct((1, D), F32)],
        input_output_aliases={7: 1},
        scratch_shapes=[pltpu.VMEM((N_PAIR, 128, 128), F32)], name=name,
        compiler_params=_cparams(("arbitrary",)))(xbc_c, proj, states, dy, a_row, dtb_row, d_exp, dproj)


def _exchange(name, arrays, scatter):
    n = len(arrays)
    flips = [(dx, dy, dc) for dx in (0, 1) for dy in (0, 1) for dc in (0, 1) if dx or dy or dc]

    def body(*refs):
        ins, outs = refs[:n], refs[n:2 * n]
        send_sems, recv_sems, loc_sems = refs[2 * n:]
        x, y, c = lax.axis_index("x"), lax.axis_index("y"), lax.axis_index("c")
        me = 4 * x + 2 * y + c
        peers = []
        for dx, dy, dc in flips:
            px, py, pc = (1 - x if dx else x), (1 - y if dy else y), (1 - c if dc else c)
            peers.append(((px, py, pc), 4 * px + 2 * py + pc))

        def remote(k, j, landed_from):
            dev, pid = peers[j]
            src = ins[k].at[pid] if scatter else ins[k]
            return pltpu.make_async_remote_copy(
                src_ref=src, dst_ref=outs[k].at[landed_from], send_sem=send_sems.at[k, j], recv_sem=recv_sems.at[k, j],
                device_id=dev, device_id_type=pl.DeviceIdType.MESH)

        local = [pltpu.make_async_copy(ins[k].at[me] if scatter else ins[k], outs[k].at[me], loc_sems.at[k])
                 for k in range(n)]
        for cp in local:
            cp.start()
        for k in range(n):
            for j in range(len(flips)):
                remote(k, j, me).start()
        for cp in local:
            cp.wait()
        for k in range(n):
            for j in range(len(flips)):
                remote(k, j, me).wait_send()
                remote(k, j, peers[j][1]).wait_recv()

    hbm = pl.BlockSpec(memory_space=pltpu.HBM)
    out_shape = [jax.ShapeDtypeStruct(a.shape if scatter else (N_DEV,) + a.shape, a.dtype) for a in arrays]
    res = pl.pallas_call(
        body, in_specs=[hbm] * n, out_specs=[hbm] * n, out_shape=out_shape, name=name,
        scratch_shapes=[pltpu.SemaphoreType.DMA((n, len(flips))), pltpu.SemaphoreType.DMA((n, len(flips))),
                        pltpu.SemaphoreType.DMA((n,))])(*arrays)
    return list(res)


def _gather_chip_once(name, block):
    def body(x_ref, out_ref, send_sems, recv_sems, loc_sem):
        x, y, c = lax.axis_index("x"), lax.axis_index("y"), lax.axis_index("c")
        me, sibling = (x, y, c), (x, y, 1 - c)
        chips = [(1 - x, y), (x, 1 - y), (1 - x, 1 - y)]

        def slot(px, py, pc):
            return out_ref.at[4 * px + 2 * py + pc]

        def copy(k, blk, to, src=None):
            return pltpu.make_async_remote_copy(
                src_ref=slot(*blk) if src is None else src, dst_ref=slot(*blk), send_sem=send_sems.at[k],
                recv_sem=recv_sems.at[k], device_id=to, device_id_type=pl.DeviceIdType.MESH)

        mine = pltpu.make_async_copy(x_ref, slot(*me), loc_sem)
        mine.start()
        first = [copy(0, me, sibling, src=x_ref)] + [copy(1 + j, me, (*chip, c), src=x_ref) for j, chip in enumerate(chips)]
        for cp in first:
            cp.start()
        passed = [copy(4 + j, (*chip, c), sibling) for j, chip in enumerate(chips)]
        for j, chip in enumerate(chips):
            copy(1 + j, (*chip, c), me).wait_recv()
            passed[j].start()
        copy(0, sibling, me).wait_recv()
        for j, chip in enumerate(chips):
            copy(4 + j, (*chip, 1 - c), me).wait_recv()
        for cp in first + passed:
            cp.wait_send()
        mine.wait()

    hbm = pl.BlockSpec(memory_space=pltpu.HBM)
    return pl.pallas_call(
        body, in_specs=[hbm], out_specs=hbm, out_shape=jax.ShapeDtypeStruct((N_DEV,) + block.shape, block.dtype), name=name,
        scratch_shapes=[pltpu.SemaphoreType.DMA((N_DEV - 1,)), pltpu.SemaphoreType.DMA((N_DEV - 1,)),
                        pltpu.SemaphoreType.DMA(())])(block)


def _peer_copies(ins, lands, send_sems, recv_sems, loc_sems, scatter):
    n = len(ins)
    flips = [(dx, dy, dc) for dx in (0, 1) for dy in (0, 1) for dc in (0, 1) if dx or dy or dc]
    x, y, c = lax.axis_index("x"), lax.axis_index("y"), lax.axis_index("c")
    me = 4 * x + 2 * y + c
    peers = []
    for dx, dy, dc in flips:
        px, py, pc = (1 - x if dx else x), (1 - y if dy else y), (1 - c if dc else c)
        peers.append(((px, py, pc), 4 * px + 2 * py + pc))

    def remote(k, j, slot):
        dev, pid = peers[j]
        return pltpu.make_async_remote_copy(
            src_ref=ins[k].at[pid] if scatter else ins[k], dst_ref=lands[k].at[slot],
            send_sem=send_sems.at[k * N_FLIP + j], recv_sem=recv_sems.at[k * N_FLIP + j],
            device_id=dev, device_id_type=pl.DeviceIdType.MESH)

    local = [pltpu.make_async_copy(ins[k].at[me] if scatter else ins[k], lands[k].at[me], loc_sems.at[k])
             for k in range(n)]
    pairs = [(k, j) for k in range(n) for j in range(len(flips))]
    sent = lambda k, j: remote(k, j, me)
    landed = lambda k, j: remote(k, j, peers[j][1])
    return local, pairs, sent, landed


_HBM = pl.BlockSpec(memory_space=pltpu.HBM)
_SEM = pl.BlockSpec(memory_space=pltpu.SEMAPHORE)
N_FLIP = N_DEV - 1


def _exchange_start(name, arrays, scatter, after):
    n = len(arrays)
    arrays = [pltpu.with_memory_space_constraint(a, pltpu.HBM) for a in arrays]
    lands = [pltpu.with_memory_space_constraint(
        lax.empty(a.shape if scatter else (N_DEV,) + a.shape, a.dtype), pltpu.HBM) for a in arrays]

    def body(*refs):
        ins, lnd = refs[:n], refs[n:2 * n]
        send_sems, recv_sems, loc_sems = refs[2 * n + 1:2 * n + 4]
        token = refs[-1]
        local, pairs, sent, _ = _peer_copies(ins, lnd, send_sems, recv_sems, loc_sems, scatter)
        for cp in local:
            cp.start()
        for k, j in pairs:
            sent(k, j).start()
        token[...] = jnp.zeros_like(token)

    res = pl.pallas_call(
        body, name=name,
        in_specs=[_HBM] * (2 * n) + [pl.BlockSpec(memory_space=pl.ANY)],
        out_specs=[_SEM, _SEM, _SEM] + [_HBM] * (2 * n) + [pl.BlockSpec(memory_space=pltpu.VMEM)],
        out_shape=[pltpu.SemaphoreType.DMA((n * N_FLIP,)), pltpu.SemaphoreType.DMA((n * N_FLIP,)), pltpu.SemaphoreType.DMA((n,))]
        + [pltpu.HBM(a.shape, a.dtype) for a in arrays] + [pltpu.HBM(a.shape, a.dtype) for a in lands]
        + [jax.ShapeDtypeStruct((8, 128), F32)],
        input_output_aliases={k: 3 + k for k in range(2 * n)},
        compiler_params=pltpu.CompilerParams(has_side_effects=pltpu.SideEffectType.DATAFLOW_SIDE_EFFECTING),
    )(*arrays, *lands, after)
    return (res[:3], res[3:3 + n], res[3 + n:3 + 2 * n], scatter), res[-1]


def _exchange_wait(name, state, after):
    sems, ins_thru, lands_thru, scatter = state
    n = len(ins_thru)

    def body(*refs):
        ins, lnd = refs[:n], refs[n:2 * n]
        send_sems, recv_sems, loc_sems = refs[2 * n:2 * n + 3]
        local, pairs, sent, landed = _peer_copies(ins, lnd, send_sems, recv_sems, loc_sems, scatter)
        for cp in local:
            cp.wait()
        for k, j in pairs:
            sent(k, j).wait_send()
            landed(k, j).wait_recv()

    res = pl.pallas_call(
        body, name=name,
        in_specs=[_HBM] * (2 * n) + [_SEM, _SEM, _SEM] + [pl.BlockSpec(memory_space=pl.ANY)],
        out_specs=[_HBM] * (2 * n),
        out_shape=[pltpu.HBM(a.shape, a.dtype) for a in ins_thru] + [pltpu.HBM(a.shape, a.dtype) for a in lands_thru],
        input_output_aliases={k: k for k in range(2 * n)},
        compiler_params=pltpu.CompilerParams(has_side_effects=pltpu.SideEffectType.DATAFLOW_SIDE_EFFECTING),
    )(*ins_thru, *lands_thru, *sems, after)
    return list(res[n:])


def _adamw_fn(*vals):
    slots, (w, m, v) = vals[:N_DEV], vals[N_DEV:]
    g = slots[0].astype(F32)
    for s in slots[1:]:
        g = g + s.astype(F32)
    m2 = ADAM_B1 * m + (1.0 - ADAM_B1) * g
    v2 = ADAM_B2 * v + (1.0 - ADAM_B2) * (g * g)
    m_hat = m2 / (1.0 - ADAM_B1 ** ADAM_STEP)
    v_hat = v2 / (1.0 - ADAM_B2 ** ADAM_STEP)
    delta = -ADAM_LR * (m_hat / (jnp.sqrt(v_hat) + ADAM_EPS) + ADAM_WD * w)
    return (g, delta, m2, v2), ()


def _adamw(name, slots, w, m, v, first_row=0, prev=None):
    R, C = slots.shape[1:]
    tm = R if R <= 128 else _pick(R, 256, 8)
    rows = ([_rows(slots, C, lead=s) for s in range(N_DEV)]
            + [_rows(a, C, roff=first_row // tm) for a in (w, m, v)])
    return _tiled(name, _adamw_fn, (1, R // tm), tm, rows, [], [(w.shape[0], C, F32)] * 4,
                  out_roff=first_row // tm, prev_outs=prev)


def _bucket_onehots():
    out = []
    qi = jnp.arange(ATT_BLOCK)[:, None]
    kj = jnp.arange(ATT_BLOCK)[None, :]
    max_exact = REL_BUCKETS // 2
    for _, dil in ATT_GROUPS:
        parts = []
        for rel in (qi + ATT_BLOCK - kj, qi - kj):
            dist = jnp.clip(rel, 0, None) * dil
            nf = jnp.maximum(dist, 1).astype(F32)
            large = max_exact + (jnp.log(nf / max_exact) / math.log(REL_MAX_DISTANCE / max_exact)
                                 * (REL_BUCKETS - max_exact)).astype(jnp.int32)
            large = jnp.minimum(large, REL_BUCKETS - 1)
            bucket = jnp.where(dist < max_exact, dist, large)
            parts.append((bucket[:, :, None] == jnp.arange(REL_BUCKETS)[None, None, :]).astype(F32))
        out.append(jnp.stack(parts))
    return out


SHARDED = ("w_in", "w_a", "pool_w", "w_b", "ssd_conv_w", "w_c", "w_o", "ffn_w_up", "ffn_conv_w", "ffn_w_down")
MATMUL_WEIGHTS = ("w_in", "w_a", "pool_w", "w_b", "w_c", "w_o", "ffn_w_up", "ffn_w_down")
ROW_SHARDED = ("w_b", "w_c", "w_o", "ffn_w_down")
W_IN_SEGS = tuple(
    (which * ATT_W + unit * 128, which * ATT_W + (unit + 1) * 128, OFF_QKV + unit * QKV_W + which * 128)
    for unit in range(9) for which in range(3)
) + ((3456, 4480, OFF_POOL), (4480, 5504, OFF_Z), (5504, 7040, OFF_XBC), (7040, 7056, OFF_DT), (7056, IN_WIDTH, OFF_GATE))
FFN_SEGS = tuple((h * D_FF + j * 128, h * D_FF + (j + 1) * 128, j * 256 + h * 128)
                 for j in range(D_FF // 128) for h in range(2))
COL_SHARDED = {
    "w_in": (IN_WIDTH // N_DEV, W_IN_SEGS, ((OFF_DT + SSD_HEADS, OFF_QKV),), NP),
    "w_a": (D // N_DEV, ((0, D, 0),), (), D),
    "ffn_w_up": (2 * D_FF // N_DEV, FFN_SEGS, (), 2 * D_FF),
    "ssd_conv_w": (SSD_XBC // N_DEV, ((0, SSD_XBC, 0),), (), SSD_XBC),
    "ffn_conv_w": (2 * D_FF // N_DEV, FFN_SEGS, (), 2 * D_FF),
}
REPLICATED = ("rel_bias", "ln1_g", "b_gate", "pool_scale", "ssd_conv_b", "ssd_dt_bias", "ssd_a_log", "ssd_d",
              "ssd_norm_w", "ln2_g", "ffn_conv_b", "final_g")
WEIGHTS = ("rel_bias", "ln1_g", "w_in", "b_gate", "w_a", "pool_w", "pool_scale", "w_b", "ssd_conv_w", "ssd_conv_b",
           "ssd_dt_bias", "ssd_a_log", "ssd_d", "ssd_norm_w", "w_c", "w_o", "ln2_g", "ffn_w_up", "ffn_conv_w",
           "ffn_conv_b", "ffn_w_down", "final_g")


def _local_weight(name, n, blocks):
    if n in COL_SHARDED:
        c, segs, zeros, width = COL_SHARDED[n]
        return _col_assemble(name, blocks, _seg_copies(segs, c), zeros, width)
    if n in ROW_SHARDED:
        return blocks.reshape(-1, blocks.shape[-1])
    return blocks


def _device_blocks(name, n, g):
    if n in COL_SHARDED:
        c, segs, _, _ = COL_SHARDED[n]
        return _col_split(name, g, _seg_copies(segs, c), c, BF16)
    if n in ROW_SHARDED:
        return g.reshape(N_DEV, g.shape[0] // N_DEV, g.shape[1]).astype(BF16)
    return g.astype(BF16)


def _row(v, n=None):
    v = v.reshape(1, -1)
    if n is not None and v.shape[1] < n:
        v = jnp.pad(v, ((0, 0), (0, n - v.shape[1])))
    return v


RT = 512


def _row_call(name, fn, cw, ncol, rows, params, outs, accs=(), into=None):
    return _tiled(name, fn, (ncol, S // RT), RT, rows, params, [(S, cw, dt) for dt in outs], accs, into=into)


def _col_call(name, fn, tc, ncol, rows, params, outs, accs=(), into=None):
    return _tiled(name, fn, (ncol, 1), S, rows, params, [(S, tc, dt) for dt in outs], accs, into=into)


def _fwd_only(fn):
    return lambda *a: (fn(*a), ())


def _layer_fwd(i, x, W, P, bias_tabs, late=None, late_ffn=None):
    sv = {"x": x}
    (u,) = _row_call(f"ln1_f{i}", _fwd_only(_rmsnorm_fn), D, 1, [_rows(x, D)], [_p_row(P["ln1_g"], D)], [BF16])
    proj = _matmul(f"inproj_f{i}", u, W["w_in"], "nn")
    sv["u"], sv["proj"] = u, proj

    os_, ls_ = [], []
    for gi in range(len(ATT_GROUPS)):
        o, lse = _att_fwd(f"att_f{i}_{gi}", proj, gi, bias_tabs[gi][0], bias_tabs[gi][1])
        os_.append(o)
        ls_.append(lse)
    sv["att_o"], sv["att_l"] = os_, ls_
    (att,) = _row_call(f"attmerge_f{i}", _fwd_only(_att_merge_fn), ATT_GW, 1,
                       [_rows(t, ATT_GW) for t in os_ + ls_], [], [BF16])
    if late is not None:
        W2, P2 = late(att)
        W.update(W2)
        P.update(P2)
    y_a = _matmul(f"wa_f{i}", att, W["w_a"], "nn", out_dtype=BF16)
    sv["att"], sv["y_a"] = att, y_a

    pool_params = [(W["pool_w"], (N_DEV, None, 32, 256), lambda j, i_: (0, j, 0, 0)), _p_row(P["pool_scale"], 256)]
    (yb_pre,) = _col_call(f"pool_f{i}", _fwd_only(_pool_fn), 256, 4, [_rows(proj, 256, OFF_POOL // 256)],
                          pool_params, [BF16])
    y_b = _matmul(f"wb_f{i}", yb_pre, W["w_b"], "nn", out_dtype=BF16)
    sv["yb_pre"], sv["y_b"] = yb_pre, y_b

    conv_params = [_p_row(P["ssd_conv_w"][k], 128) for k in range(4)] + [_p_row(P["ssd_conv_b"], 128)]
    (xbc_c,) = _col_call(f"ssdconv_f{i}", _fwd_only(_ssd_conv_fn), 128, SSD_XBC // 128,
                         [_rows(proj, 128, OFF_XBC // 128)], conv_params, [F32])
    y_ssd, states = _ssd_fwd(f"ssd_f{i}", xbc_c, proj, P["a_row"], P["dtb_row"], P["d_exp"])
    (yc_pre,) = _row_call(f"ssdnorm_f{i}", _fwd_only(_gated_norm_fn), 512, 2,
                          [_rows(y_ssd, 512), _rows(proj, 512, OFF_Z // 512)], [_p_row(P["ssd_norm_w"], 512)], [BF16])
    y_c = _matmul(f"wc_f{i}", yc_pre, W["w_c"], "nn", out_dtype=BF16)
    sv["xbc_c"], sv["states"], sv["y_ssd"], sv["yc_pre"], sv["y_c"] = xbc_c, states, y_ssd, yc_pre, y_c

    gate_rows = [_rows(proj, D, k) for k in range(3)] + [_rows(t, D) for t in (y_a, y_b, y_c)]
    gate_params = [_p_row(P["b_gate"], D, k) for k in range(3)]
    (merged,) = _row_call(f"gate_f{i}", _fwd_only(_gate_merge_fn), D, 1, gate_rows, gate_params, [BF16])
    x1 = _matmul(f"wo_f{i}", merged, W["w_o"], "nn", add=x)
    sv["merged"], sv["x1"] = merged, x1

    if late_ffn is not None:
        W2, P2 = late_ffn(x1)
        W.update(W2)
        P.update(P2)
    (u2,) = _row_call(f"ln2_f{i}", _fwd_only(_rmsnorm_fn), D, 1, [_rows(x1, D)], [_p_row(P["ln2_g"], D)], [BF16])
    up = _matmul(f"up_f{i}", u2, W["ffn_w_up"], "nn", out_dtype=BF16)
    (act,) = _col_call(f"ffnact_f{i}", lambda *a: (_ffn_act_fn(*_ffn_halves(a)), ()), 128, D_FF // 128,
                       [_rows(up, 256)], _ffn_params(P), [BF16])
    x2 = _matmul(f"down_f{i}", act, W["ffn_w_down"], "nn", add=x1)
    sv["u2"], sv["up"], sv["act"] = u2, up, act
    return x2, sv


def _ffn_params(P):
    return [_p_row(P["ffn_conv_w"][k], 256) for k in range(3)] + [_p_row(_interleave_ffn(P["ffn_conv_b"]), 256)]


def _ffn_halves(vals):
    (xa, xv), (a0, v0), (a1, v1), (a2, v2), (ab, vb) = ((t[:, :128], t[:, 128:]) for t in vals)
    return xa, xv, a0, a1, a2, ab, v0, v1, v2, vb


def _interleave_ffn(row):
    return row.reshape(2, D_FF // 128, 128).transpose(1, 0, 2).reshape(1, 2 * D_FF)


def _deinterleave_ffn(row):
    return row.reshape(D_FF // 128, 2, 128).transpose(1, 0, 2).reshape(1, 2 * D_FF)


def _layer_bwd(i, dx2, sv, W, P, bias_tabs, onehots, on_ffn_grads, on_sharded_grads):
    G = {}
    x, proj, x1 = sv["x"], sv["proj"], sv["x1"]

    dact = _matmul(f"down_bx{i}", dx2, W["ffn_w_down"], "nt", out_dtype=BF16)
    G["ffn_w_down"] = _matmul(f"down_bw{i}", sv["act"], dx2, "tn", out_dtype=BF16)
    nb = D_FF // 128
    up = sv["up"]
    def ffn_bwd(x2, t0, t1, t2, b2, dact_):
        (dxa, dxv), (a0, a1, a2, ab, v0, v1, v2, vb) = _with_vjp(_ffn_act_fn, 10, (0, 1), tuple(range(2, 10)))(
            *_ffn_halves((x2, t0, t1, t2, b2)), dact_)
        pair = lambda a, v: jnp.concatenate([a, v], axis=1)
        return (pair(dxa, dxv),), (pair(a0, v0), pair(a1, v1), pair(a2, v2), pair(ab, vb))

    dup, t0, t1, t2, tb = _col_call(
        f"ffnact_b{i}", ffn_bwd, 256, nb, [_rows(up, 256)], _ffn_params(P) + [_rows_as_param(dact, 128)],
        [BF16], [_a_row(2 * D_FF, 256)] * 4)
    G["ffn_conv_w"] = jnp.concatenate([t0, t1, t2], axis=0)
    G["ffn_conv_b"] = _deinterleave_ffn(tb)[0]
    du2 = _matmul(f"up_bx{i}", dup, W["ffn_w_up"], "nt")
    G["ffn_w_up"] = _matmul(f"up_bw{i}", sv["u2"], dup, "tn", out_dtype=BF16)

    def norm_bwd(x_, g_, du_, dres):
        (dxn,), (dg,) = _with_vjp(_rmsnorm_fn, 2, (0,), (1,))(x_, g_, du_)
        return (dxn + dres,), (dg,)

    ln2_g = P["ln2_g"] + on_ffn_grads(G)
    (dx1,), (G["ln2_g"],) = _split_res(_row_call(
        f"ln2_b{i}", lambda x_, du_, dres, g_: norm_bwd(x_, g_, du_, dres), D, 1,
        [_rows(x1, D), _rows(du2, D), _rows(dx2, D)], [_p_row(ln2_g, D)], [F32], [_a_row(D, D)]), 1)

    dmerged = _matmul(f"wo_bx{i}", dx1, W["w_o"], "nt", out_dtype=BF16)
    G["w_o"] = _matmul(f"wo_bw{i}", sv["merged"], dx1, "tn", out_dtype=BF16)
    def gate_bwd(g0, g1, g2, ya, yb, yc, dm, b0, b1, b2):
        (d0, d1, d2, da, db, dc), dbias = _with_vjp(_gate_merge_fn, 9, (0, 1, 2, 3, 4, 5), (6, 7, 8))(
            g0, g1, g2, ya, yb, yc, b0, b1, b2, dm)
        dgate = jnp.concatenate([t.astype(BF16) for t in (d0, d1, d2)], axis=1)
        return (dgate, da, db, dc), dbias

    gate_rows = [_rows(proj, D, k) for k in range(3)] + [_rows(sv[t], D) for t in ("y_a", "y_b", "y_c")]
    dproj, dya, dyb, dyc, db0, db1, db2 = _tiled(
        f"gate_b{i}", gate_bwd, (1, S // RT), RT, gate_rows + [_rows(dmerged, D)],
        [_p_row(P["b_gate"], D, k) for k in range(3)], [(S, 3 * D, BF16)] + [(S, D, BF16)] * 3, [_a_row(D, D)] * 3,
        into={0: (None, 0, NP)})
    G["b_gate"] = jnp.concatenate([db0, db1, db2], axis=1)[0]

    dyc_pre = _matmul(f"wc_bx{i}", dyc, W["w_c"], "nt", out_dtype=BF16)
    G["w_c"] = _matmul(f"wc_bw{i}", sv["yc_pre"], dyc, "tn", out_dtype=BF16)

    def gnorm_bwd(y_, z_, dy_, w_):
        return _with_vjp(_gated_norm_fn, 3, (0, 1), (2,))(y_, z_, w_, dy_)

    dy_ssd, dproj, dnw = _row_call(
        f"ssdnorm_b{i}", gnorm_bwd, 512, 2,
        [_rows(sv["y_ssd"], 512), _rows(proj, 512, OFF_Z // 512), _rows(dyc_pre, 512)],
        [_p_row(P["ssd_norm_w"], 512)], [F32, BF16], [_a_row(D, 512)], into={1: (dproj, OFF_Z // 512, NP)})
    G["ssd_norm_w"] = dnw[0]
    dxbc_c, dproj, da_row, ddtb_row, dd_exp = _ssd_bwd(f"ssd_b{i}", sv["xbc_c"], proj, sv["states"], dy_ssd,
                                                       P["a_row"], P["dtb_row"], P["d_exp"], dproj)
    a_vec = P["a_row"][0, :SSD_HEADS]
    G["ssd_a_log"] = da_row[0, :SSD_HEADS] * a_vec
    G["ssd_dt_bias"] = ddtb_row[0, :SSD_HEADS]
    G["ssd_d"] = dd_exp.reshape(SSD_HEADS, HEAD_DIM).sum(axis=1)
    conv_params = [_p_row(P["ssd_conv_w"][k], 128) for k in range(4)] + [_p_row(P["ssd_conv_b"], 128)]

    def conv_bwd(x_, dy_, w0, w1, w2, w3, b_):
        return _with_vjp(_ssd_conv_fn, 6, (0,), (1, 2, 3, 4, 5))(x_, w0, w1, w2, w3, b_, dy_)

    dproj, c0, c1, c2, c3, cb = _col_call(
        f"ssdconv_b{i}", conv_bwd, 128, SSD_XBC // 128, [_rows(proj, 128, OFF_XBC // 128), _rows(dxbc_c, 128)],
        conv_params, [BF16], [_a_row(SSD_XBC, 128)] * 5, into={0: (dproj, OFF_XBC // 128, NP)})
    G["ssd_conv_w"] = jnp.concatenate([c0, c1, c2, c3], axis=0)
    G["ssd_conv_b"] = cb[0]

    dyb_pre = _matmul(f"wb_bx{i}", dyb, W["w_b"], "nt", out_dtype=BF16)
    G["w_b"] = _matmul(f"wb_bw{i}", sv["yb_pre"], dyb, "tn", out_dtype=BF16)
    pool_params = [(W["pool_w"], (N_DEV, None, 32, 256), lambda j, i_: (0, j, 0, 0)), _p_row(P["pool_scale"], 256)]

    def pool_bwd(x_, dy_, wg, sc):
        return _with_vjp(_pool_fn, 3, (0,), (1, 2))(x_, wg.astype(F32), sc, dy_)

    dproj, dwg, dsc = _col_call(
        f"pool_b{i}", pool_bwd, 256, 4, [_rows(proj, 256, OFF_POOL // 256), _rows(dyb_pre, 256)], pool_params, [BF16],
        [((N_DEV, 4, 32, 256), (N_DEV, None, 32, 256), lambda j, i_: (0, j, 0, 0)), _a_row(D, 256)],
        into={0: (dproj, OFF_POOL // 256, NP)})
    G["pool_w"] = dwg
    G["pool_scale"] = dsc[0]

    datt = _matmul(f"wa_bx{i}", dya, W["w_a"], "nt", out_dtype=BF16)
    G["w_a"] = _matmul(f"wa_bw{i}", sv["att"], dya, "tn", out_dtype=BF16)

    def merge_bwd(o0, o1, o2, l0, l1, l2, da_):
        return _with_vjp(_att_merge_fn, 6, (0, 1, 2, 3, 4, 5), ())(o0, o1, o2, l0, l1, l2, da_)

    dol = _row_call(f"attmerge_b{i}", merge_bwd, ATT_GW, 1,
                    [_rows(t, ATT_GW) for t in sv["att_o"] + sv["att_l"]] + [_rows(datt, ATT_GW)], [], [F32] * 6)
    g_rel = jnp.zeros((REL_BUCKETS, 18), F32)
    for gi in range(len(ATT_GROUPS)):
        dproj, gbp, gbc = _att_bwd(f"att_b{i}_{gi}", proj, gi, bias_tabs[gi][0], bias_tabs[gi][1],
                                   dol[gi], dol[3 + gi], dproj)
        oh = onehots[gi]
        gt = (jnp.einsum("hqk,qkb->bh", gbp, oh[0], precision=lax.Precision.HIGHEST)
              + jnp.einsum("hqk,qkb->bh", gbc, oh[1], precision=lax.Precision.HIGHEST))
        g_rel = g_rel.at[:, gi * 6:(gi + 1) * 6].add(gt)
    G["rel_bias"] = g_rel

    du = _matmul(f"inproj_bx{i}", dproj, W["w_in"], "nt")
    G["w_in"] = _matmul(f"inproj_bw{i}", sv["u"], dproj, "tn", out_dtype=BF16)
    ln1_g = P["ln1_g"] + on_sharded_grads(G)
    (dx,), (G["ln1_g"],) = _split_res(_row_call(
        f"ln1_b{i}", lambda x_, du_, dres, g_: norm_bwd(x_, g_, du_, dres), D, 1,
        [_rows(x, D), _rows(du, D), _rows(dx1, D)], [_p_row(ln1_g, D)], [F32], [_a_row(D, D)]), 1)
    G["ln1_g"] = G["ln1_g"][0]
    G["ln2_g"] = G["ln2_g"][0]
    return dx, G


def _rows_as_param(arr, cw):
    return (arr, (arr.shape[0], cw), lambda j, i: (0, j))


def _split_res(res, n_out):
    return tuple(res[:n_out]), tuple(res[n_out:])


def kernel(x, rel_bias, ln1_g, w_in, b_gate, w_a, pool_w, pool_scale, w_b, ssd_conv_w, ssd_conv_b, ssd_dt_bias, ssd_a_log, ssd_d, ssd_norm_w, w_c, w_o, ln2_g, ffn_w_up, ffn_conv_w, ffn_conv_b, ffn_w_down, final_g, loss_target, m_rel_bias, m_ln1_g, m_w_in, m_b_gate, m_w_a, m_pool_w, m_pool_scale, m_w_b, m_ssd_conv_w, m_ssd_conv_b, m_ssd_dt_bias, m_ssd_a_log, m_ssd_d, m_ssd_norm_w, m_w_c, m_w_o, m_ln2_g, m_ffn_w_up, m_ffn_conv_w, m_ffn_conv_b, m_ffn_w_down, m_final_g, v_rel_bias, v_ln1_g, v_w_in, v_b_gate, v_w_a, v_pool_w, v_pool_scale, v_w_b, v_ssd_conv_w, v_ssd_conv_b, v_ssd_dt_bias, v_ssd_a_log, v_ssd_d, v_ssd_norm_w, v_w_c, v_w_o, v_ln2_g, v_ffn_w_up, v_ffn_conv_w, v_ffn_conv_b, v_ffn_w_down, v_final_g):
    args = locals()
    wts = {n: args[n] for n in WEIGHTS}
    mom = {n: args["m_" + n] for n in WEIGHTS}
    var = {n: args["v_" + n] for n in WEIGHTS}
    names = list(SHARDED)

    onehots = _bucket_onehots()
    bias_tabs = []
    for gi in range(3):
        tab = rel_bias[:, gi * 6:(gi + 1) * 6]
        b = jnp.einsum("pqkb,bh->phqk", onehots[gi], tab, precision=lax.Precision.HIGHEST)
        bias_tabs.append((b[0], b[1]))

    def gather_start(tag, i, which, after):
        shards = [wts[n][i].astype(BF16) if n in MATMUL_WEIGHTS else wts[n][i] for n in which]
        return _exchange_start(f"gather_start{tag}", shards, False, after)

    def layer_params(i, which, landed):
        full = {n: _local_weight(f"local_{n}{i}", n, g) for n, g in zip(which, landed)}
        W = {n: full[n] for n in which if n in MATMUL_WEIGHTS}
        P = {n: [_row(full[n][k]) for k in range(full[n].shape[0])] for n in ("ssd_conv_w", "ffn_conv_w") if n in full}
        return W, P

    def replicated_params(i):
        return {"ln1_g": _row(ln1_g[i]), "ln2_g": _row(ln2_g[i]), "b_gate": _row(b_gate[i]),
                "pool_scale": _row(pool_scale[i]), "ssd_conv_b": _row(ssd_conv_b[i]),
                "ssd_norm_w": _row(ssd_norm_w[i]), "ffn_conv_b": _row(ffn_conv_b[i]),
                "a_row": _row(-jnp.exp(ssd_a_log[i]), 128), "dtb_row": _row(ssd_dt_bias[i], 128),
                "d_exp": _row(jnp.repeat(ssd_d[i], HEAD_DIM))}

    h = x.reshape(S, D)
    saved, Ws, Ps = [], [], []
    ffn = ["ffn_w_up", "ffn_conv_w", "ffn_w_down"]
    core = [n for n in names if n not in ffn]
    first, rest = ["w_in"], [n for n in core if n != "w_in"]
    landed_first = [_gather_chip_once("gather_w_in0", w_in[0].astype(BF16))]
    state_rest, token = gather_start("0r", 0, rest, landed_first[0])
    landed = None

    def late0(att):
        return layer_params(0, rest, _exchange_wait("gather_wait0r", state_rest, att))

    for i in range(DEPTH):
        P = replicated_params(i)
        W, P1 = layer_params(0, first, landed_first) if i == 0 else layer_params(i, core, landed)
        P.update(P1)
        state_ffn, tok = gather_start(f"{i}f", i, ffn, token if i == 0 else landed[0])
        token = tok if i > 0 else token + tok
        if i + 1 < DEPTH:
            state_next, tok = gather_start(f"{i + 1}c", i + 1, core, tok)
            token = token + tok
        P["ln1_g"] = P["ln1_g"] + token[0, 0]

        def late_ffn(x1, i=i, state_ffn=state_ffn):
            return layer_params(i, ffn, _exchange_wait(f"gather_wait{i}f", state_ffn, x1))

        h, sv = _layer_fwd(i, h, W, P, bias_tabs, late0 if i == 0 else None, late_ffn)
        Ws.append(W)
        Ps.append(dict(P, ln1_g=_row(ln1_g[i])))
        saved.append(sv)
        if i + 1 < DEPTH:
            landed = _exchange_wait(f"gather_wait{i + 1}c", state_next, h)

    def loss_bwd(x_, t_, g_):
        lval, vjp = jax.vjp(_loss_fn, x_, t_, g_)
        dx_, _, dg_ = vjp(jnp.ones_like(lval))
        return (dx_,), (dg_, jnp.broadcast_to(lval, (1, 128)))

    dh, g_final, loss_part = _row_call("loss", loss_bwd, D, 1, [_rows(h, D), _rows(loss_target.reshape(S, D), D)],
                                       [_p_row(_row(final_g), D)], [F32], [_a_row(D, D), _a_row(128, 128)])
    loss = lax.psum(loss_part[0, 0], MESH_AXES)

    grads = {n: [None] * DEPTH for n in WEIGHTS if n not in ("rel_bias", "final_g")}
    g_rel = jnp.zeros((REL_BUCKETS, 18), F32)
    slots = [dict() for _ in range(DEPTH)]
    pending = []
    for i in reversed(range(DEPTH)):
        started = []

        def start_group(tag, group, G, after, i=i, started=started):
            parts = [_device_blocks(f"blocks_{n}{i}", n, G[n]) for n in group]
            state, token = _exchange_start(f"scatter_start{i}{tag}", parts, True, after)
            started.append((i, tag, group, state))
            return token[0, 0]

        dh, G = _layer_bwd(i, dh, saved[i], Ws[i], Ps[i], bias_tabs, onehots,
                           lambda G: start_group("f", ffn, G, G["ffn_conv_b"]),
                           lambda G: start_group("c", core, G, G["b_gate"]))
        for j, tag, group, state in pending:
            slots[j].update(zip(group, _exchange_wait(f"scatter_wait{j}{tag}", state, dh)))
        pending = started
        g_rel = g_rel + G.pop("rel_bias")
        for n, g in G.items():
            grads[n][i] = g
    grad_x = dh.reshape(1, S, D)
    local = {n: jnp.stack(grads[n]) for n in grads if n not in SHARDED}
    local["rel_bias"] = g_rel
    local["final_g"] = g_final[0]
    out = {}

    def pack(d):
        flat = jnp.concatenate([d[n].reshape(-1).astype(F32) for n in REPLICATED])
        rows = -(-flat.shape[0] // (8 * 128)) * 8
        return jnp.pad(flat, (0, rows * 128 - flat.shape[0])).reshape(rows, 128)

    (rep_slots,) = _exchange("gather_small_grads", [pack(local)], scatter=False)
    rep = _adamw("adamw_small", rep_slots, pack(wts), pack(mom), pack(var))
    off = 0
    for n in REPLICATED:
        sz = int(np.prod(wts[n].shape))
        out[n] = [t.reshape(-1)[off:off + sz].reshape(wts[n].shape) for t in rep]
        off += sz

    def flat2(n):
        shp = wts[n].shape
        r, c = int(np.prod(shp[:-1])), shp[-1]
        return r, c, wts[n].reshape(r, c), mom[n].reshape(r, c), var[n].reshape(r, c)

    chain = {}
    done = rep[0][0, 0]
    for n in names:
        if n in MATMUL_WEIGHTS:
            r, c, w2, m2, v2 = flat2(n)
            res = None
            for i in (3, 2, 1):
                res = _adamw(f"adamw_{n}{i}", slots[i][n].reshape(N_DEV, r // DEPTH, c), w2, m2, v2,
                             first_row=i * (r // DEPTH), prev=res)
            chain[n] = res
            done = done + res[0][-1, 0]
    for j, tag, group, state in pending:
        slots[j].update(zip(group, _exchange_wait(f"scatter_wait{j}{tag}", state, done.reshape(1, 1))))
    for n in names:
        r, c, w2, m2, v2 = flat2(n)
        if n in MATMUL_WEIGHTS:
            res = _adamw(f"adamw_{n}0", slots[0][n].reshape(N_DEV, r // DEPTH, c), w2, m2, v2, first_row=0, prev=chain[n])
        else:
            stacked = jnp.stack([slots[i][n] for i in range(DEPTH)], axis=1)
            res = _adamw("adamw_" + n, stacked.reshape(N_DEV, r, c), w2, m2, v2)
        out[n] = [t.reshape(wts[n].shape) for t in res]

    return (loss, grad_x, *[out[n][0] for n in WEIGHTS], *[out[n][1] for n in WEIGHTS],
            *[out[n][2] for n in WEIGHTS], *[out[n][3] for n in WEIGHTS])
```

```python
import functools
import math

import numpy as np
import jax
import jax.numpy as jnp
from jax import lax
from jax.experimental import pallas as pl
from jax.experimental.pallas import tpu as pltpu

F32 = jnp.float32
BF16 = jnp.bfloat16

N_DEV = 8
MESH_AXES = ("x", "y", "c")
S = 4096
D = 1024
DEPTH = 4
HEAD_DIM = 64
ATT_W = 1152
ATT_GW = 384
ATT_GROUPS = ((128, 1), (512, 4), (2048, 16))
ATT_BLOCK = 128
REL_BUCKETS = 32
REL_MAX_DISTANCE = 2048
POOL_WINDOWS = (2, 4, 8, 16)
SSD_HEADS = 16
SSD_CHUNK = 128
SSD_XBC = 1536
D_FF = 2816
IN_WIDTH = 10128
EPS = 1e-6
NEG = -1e30

OFF_GATE, OFF_POOL, OFF_Z, OFF_XBC, OFF_DT, OFF_QKV = 0, 3072, 4096, 5120, 6656, 6912
NP = 10368
DT_PAD = 128
QKV_W = 3 * 2 * HEAD_DIM

ADAM_LR, ADAM_B1, ADAM_B2, ADAM_EPS, ADAM_WD, ADAM_STEP = 0.001, 0.9, 0.999, 1e-08, 0.01, 10

VMEM_LIMIT = 52 * 1024 * 1024


def _cparams(sem=None):
    return pltpu.CompilerParams(dimension_semantics=sem, vmem_limit_bytes=VMEM_LIMIT)


def _dot(a, b, ca, cb):
    return lax.dot_general(a.astype(BF16), b.astype(BF16), (((ca,), (cb,)), ((), ())), preferred_element_type=F32)


@jax.custom_vjp
def _mm(a, b):
    return _dot(a, b, 1, 0)


def _mm_fwd(a, b):
    return _mm(a, b), (a, b)


def _mm_bwd(res, g):
    a, b = res
    return _dot(g, b, 1, 1).astype(a.dtype), _dot(a, g, 0, 0).astype(b.dtype)


_mm.defvjp(_mm_fwd, _mm_bwd)


@jax.custom_vjp
def _mm_nt(a, b):
    return _dot(a, b, 1, 1)


def _mm_nt_fwd(a, b):
    return _mm_nt(a, b), (a, b)


def _mm_nt_bwd(res, g):
    a, b = res
    return _dot(g, b, 1, 0).astype(a.dtype), _dot(g, a, 0, 0).astype(b.dtype)


_mm_nt.defvjp(_mm_nt_fwd, _mm_nt_bwd)


@jax.custom_vjp
def _mm_tn(a, b):
    return _dot(a, b, 0, 0)


def _mm_tn_fwd(a, b):
    return _mm_tn(a, b), (a, b)


def _mm_tn_bwd(res, g):
    a, b = res
    return _dot(b, g, 1, 1).astype(a.dtype), _dot(a, g, 1, 0).astype(b.dtype)


_mm_tn.defvjp(_mm_tn_fwd, _mm_tn_bwd)


def _shift_impl(x, j):
    n = x.shape[0]
    if j == 0:
        return x
    r = pltpu.roll(x, j % n, axis=0)
    t = lax.broadcasted_iota(jnp.int32, x.shape, 0)
    mask = (t >= j) if j > 0 else (t < n + j)
    return jnp.where(mask, r, 0.0)


@functools.partial(jax.custom_vjp, nondiff_argnums=(1,))
def _shift(x, j):
    return _shift_impl(x, j)


_shift.defvjp(lambda x, j: (_shift_impl(x, j), None), lambda j, _, g: (_shift_impl(g, -j),))


def _tri(lower):
    r = lax.broadcasted_iota(jnp.int32, (SSD_CHUNK, SSD_CHUNK), 0)
    c = lax.broadcasted_iota(jnp.int32, (SSD_CHUNK, SSD_CHUNK), 1)
    return (r >= c) if lower else (r <= c)


def _dot_hi(a, b):
    return lax.dot_general(a, b, (((1,), (0,)), ((), ())), precision=lax.Precision.HIGHEST,
                           preferred_element_type=F32)


@jax.custom_vjp
def _cumsum_rows(a):
    return _dot_hi(_tri(True).astype(F32), a)


_cumsum_rows.defvjp(lambda a: (_cumsum_rows(a), None), lambda _, g: (_dot_hi(_tri(False).astype(F32), g),))


@jax.custom_vjp
def _softplus(x):
    return jnp.maximum(x, 0.0) + jnp.log(1.0 + jnp.exp(-jnp.abs(x)))


_softplus.defvjp(lambda x: (_softplus(x), x), lambda x, g: (g * jax.nn.sigmoid(x),))


def _silu(x):
    return x * jax.nn.sigmoid(x)


def _rows(arr, cw, off=0, lead=None, roff=0):
    return (arr, cw, off, lead, roff)


def _tiled(name, fn, grid, tm, rows, params, outs, accs=(), out_roff=0, prev_outs=None, into=None):
    into = into or {}
    ncol, nrow = grid
    in_specs, operands = [], []
    for arr, cw, off, lead, roff in rows:
        if lead is None:
            in_specs.append(pl.BlockSpec((tm, cw), functools.partial(lambda j, i, off, roff: (roff + i, off + j),
                                                                     off=off, roff=roff)))
        else:
            in_specs.append(pl.BlockSpec((None, tm, cw), functools.partial(
                lambda j, i, off, lead, roff: (lead, roff + i, off + j), off=off, lead=lead, roff=roff)))
        operands.append(arr)
    for arr, bs, im in params:
        in_specs.append(pl.BlockSpec(bs, im))
        operands.append(arr)
    out_specs, out_shape = [], []
    for k, (n_rows, cw, dt) in enumerate(outs):
        _, coff, total = into.get(k, (None, 0, ncol * cw))
        out_specs.append(pl.BlockSpec((tm, cw), functools.partial(lambda j, i, r, c: (r + i, c + j), r=out_roff, c=coff)))
        out_shape.append(jax.ShapeDtypeStruct((n_rows, total), dt))
    for shape, bs, im in accs:
        out_specs.append(pl.BlockSpec(bs, im))
        out_shape.append(jax.ShapeDtypeStruct(shape, F32))
    n_in, n_out = len(operands), len(outs)
    aliases = {}
    earlier = dict(enumerate(prev_outs)) if prev_outs is not None else {}
    earlier.update({k: v[0] for k, v in into.items() if v[0] is not None})
    for k, p in sorted(earlier.items()):
        aliases[len(operands)] = k
        in_specs.append(pl.BlockSpec(memory_space=pl.ANY))
        operands.append(p)

    n_all = len(operands)

    def body(*refs):
        vals = [r[...] for r in refs[:n_in]]
        o_vals, a_vals = fn(*vals)
        for r, v in zip(refs[n_all:n_all + n_out], o_vals):
            r[...] = v.astype(r.dtype)
        i = pl.program_id(1)
        for r, v in zip(refs[n_all + n_out:], a_vals):
            @pl.when(i == 0)
            def _(r=r, v=v):
                r[...] = v.astype(r.dtype)

            @pl.when(i > 0)
            def _(r=r, v=v):
                r[...] += v.astype(r.dtype)

    res = pl.pallas_call(body, grid=grid, in_specs=in_specs, out_specs=out_specs, out_shape=out_shape, name=name,
                         input_output_aliases=aliases, compiler_params=_cparams(("arbitrary", "arbitrary")))(*operands)
    return list(res)


def _with_vjp(fn, n_prim, want_out, want_acc):
    def f(*args):
        prim, g = args[:n_prim], args[n_prim:]
        outs, vjp = jax.vjp(lambda *a: fn(*a), *prim)
        d = vjp(tuple(gi.astype(o.dtype) for gi, o in zip(g, outs)))
        return tuple(d[k] for k in want_out), tuple(d[k] for k in want_acc)
    return f


def _p_row(arr, cw, off=0):
    return (arr, (1, cw), functools.partial(lambda j, i, off: (0, off + j), off=off))


def _a_row(n, cw):
    return ((1, n), (1, cw), lambda j, i: (0, j))


def _pick(n, cap, mult):
    best = None
    for t in range(mult, min(n, cap) + 1, mult):
        if n % t == 0:
            best = t
    return best if best is not None else n


def _matmul(name, a, b, mode, add=None, out_dtype=F32):
    if mode == "nn":
        (M, K), N = a.shape, b.shape[1]
    elif mode == "nt":
        (M, K), N = a.shape, b.shape[0]
    else:
        (K, M), N = a.shape, b.shape[1]
    tn = _pick(N, 1536, 128)
    k_cap = 2048 if mode == "tn" else 3456
    tk = K if K <= k_cap else _pick(K, k_cap, 128)
    nk = K // tk
    tm = _pick(M, 1408, 128) if mode == "tn" else _pick(M, 1024, 8)
    a_bytes, b_bytes = a.size * a.dtype.itemsize, b.size * b.dtype.itemsize
    swap = nk == 1 and a_bytes * (N // tn) + b_bytes < b_bytes * (M // tm) + a_bytes
    ij = (lambda g0, g1: (g1, g0)) if swap else (lambda g0, g1: (g0, g1))

    def spec(block, index):
        return pl.BlockSpec(block, lambda g0, g1, k: index(*ij(g0, g1), k))

    if mode == "nn":
        a_spec = spec((tm, tk), lambda i, j, k: (i, k))
        b_spec = spec((tk, tn), lambda i, j, k: (k, j))
        ca, cb = 1, 0
    elif mode == "nt":
        a_spec = spec((tm, tk), lambda i, j, k: (i, k))
        b_spec = spec((tn, tk), lambda i, j, k: (j, k))
        ca, cb = 1, 1
    else:
        a_spec = spec((tk, tm), lambda i, j, k: (k, i))
        b_spec = spec((tk, tn), lambda i, j, k: (k, j))
        ca, cb = 0, 0
    in_specs, operands = [a_spec, b_spec], [a, b]
    if add is not None:
        in_specs.append(spec((tm, tn), lambda i, j, k: (i, j)))
        operands.append(add)

    def finish(r, refs, o_ref):
        if add is not None:
            r = r + refs[2][...]
        o_ref[...] = r.astype(o_ref.dtype)

    def body_single(*refs):
        finish(_dot(refs[0][...], refs[1][...], ca, cb), refs, refs[-1])

    def body_multi(*refs):
        o_ref, acc_ref = refs[-2], refs[-1]
        k = pl.program_id(2)
        d = _dot(refs[0][...], refs[1][...], ca, cb)

        @pl.when(k == 0)
        def _():
            acc_ref[...] = d

        @pl.when(jnp.logical_and(k > 0, k < nk - 1))
        def _():
            acc_ref[...] += d

        @pl.when(k == nk - 1)
        def _():
            finish(acc_ref[...] + d, refs, o_ref)

    grid = (N // tn, M // tm, nk) if swap else (M // tm, N // tn, nk)
    return pl.pallas_call(
        body_single if nk == 1 else body_multi, grid=grid, in_specs=in_specs,
        out_specs=spec((tm, tn), lambda i, j, k: (i, j)),
        out_shape=jax.ShapeDtypeStruct((M, N), out_dtype),
        scratch_shapes=[] if nk == 1 else [pltpu.VMEM((tm, tn), F32)], name=name,
        compiler_params=_cparams(("parallel", "parallel", "arbitrary")))(*operands)


def _seg_copies(segs, c):
    out = []
    for lo, hi, dst in segs:
        n = lo
        while n < hi:
            p = n // c
            w = min(hi, (p + 1) * c) - n
            out.append((p, n - p * c, w, dst + n - lo))
            n += w
    return out


def _col_assemble(name, blocks, copies, zeros, n_out):
    _, R, c = blocks.shape
    tm = R if R <= 128 else 128

    def body(b_ref, o_ref):
        for p, s, w, d in copies:
            o_ref[:, d:d + w] = b_ref[p, :, s:s + w]
        for lo, hi in zeros:
            o_ref[:, lo:hi] = jnp.zeros((tm, hi - lo), o_ref.dtype)

    return pl.pallas_call(
        body, grid=(R // tm,), in_specs=[pl.BlockSpec((N_DEV, tm, c), lambda i: (0, i, 0))],
        out_specs=pl.BlockSpec((tm, n_out), lambda i: (i, 0)),
        out_shape=jax.ShapeDtypeStruct((R, n_out), blocks.dtype), name=name, compiler_params=_cparams(("parallel",)))(blocks)


def _col_split(name, full, copies, c, dtype):
    R, n = full.shape
    tm = R if R <= 128 else 128

    def body(f_ref, o_ref):
        for p, s, w, d in copies:
            o_ref[p, :, s:s + w] = f_ref[:, d:d + w].astype(dtype)

    return pl.pallas_call(
        body, grid=(R // tm,), in_specs=[pl.BlockSpec((tm, n), lambda i: (i, 0))],
        out_specs=pl.BlockSpec((N_DEV, tm, c), lambda i: (0, i, 0)),
        out_shape=jax.ShapeDtypeStruct((N_DEV, R, c), dtype), name=name, compiler_params=_cparams(("parallel",)))(full)


def _rmsnorm_fn(x, g):
    x = x.astype(F32)
    return (x * lax.rsqrt(jnp.mean(x * x, axis=-1, keepdims=True) + EPS) * g,)


def _gate_merge_fn(g0, g1, g2, ya, yb, yc, b0, b1, b2):
    return (jax.nn.sigmoid(g0 + b0) * ya + jax.nn.sigmoid(g1 + b1) * yb + jax.nn.sigmoid(g2 + b2) * yc,)


def _gated_norm_fn(y, z, w):
    t = y * _silu(z)
    return (t * lax.rsqrt(jnp.mean(t * t, axis=-1, keepdims=True) + EPS) * w,)


def _att_merge_fn(o0, o1, o2, l0, l1, l2):
    m = lax.stop_gradient(jnp.maximum(jnp.maximum(l0, l1), l2))
    e0, e1, e2 = jnp.exp(l0 - m), jnp.exp(l1 - m), jnp.exp(l2 - m)
    return ((e0 * o0 + e1 * o1 + e2 * o2) / (e0 + e1 + e2),)


def _loss_fn(x, tgt, g):
    (y,) = _rmsnorm_fn(x, g)
    err = y - tgt
    return 0.5 * jnp.sum(jnp.mean(err * err, axis=-1, keepdims=True), axis=0, keepdims=True)


def _pool_fn(x, wg, scale):
    g = pl.program_id(0)
    s2 = x + _shift(x, 1)
    s4 = s2 + _shift(s2, 2)
    s8 = s4 + _shift(s4, 4)
    s16 = s8 + _shift(s8, 8)
    win = ((g == 0).astype(F32) * s2 + (g == 1).astype(F32) * s4 + (g == 2).astype(F32) * s8
           + (g == 3).astype(F32) * s16)
    t = lax.broadcasted_iota(jnp.int32, (x.shape[0], 1), 0) + 1
    cnt = jnp.minimum(t, jnp.left_shift(2, g)).astype(F32)
    d = win / cnt - x
    return (_mm(d, wg.reshape(256, 256)) * scale,)


def _dwconv(x, taps, b):
    k = len(taps)
    y = taps[k - 1] * x + b
    for i in range(k - 1):
        y = y + taps[i] * _shift(x, k - 1 - i)
    return y


def _ssd_conv_fn(x, w0, w1, w2, w3, b):
    return (_silu(_dwconv(x, (w0, w1, w2, w3), b)),)


def _ffn_act_fn(xa, xv, a0, a1, a2, ab, v0, v1, v2, vb):
    xa, xv = xa.astype(F32), xv.astype(F32)
    return (_silu(_dwconv(xa, (a0, a1, a2), ab)) * _dwconv(xv, (v0, v1, v2), vb),)


@jax.custom_vjp
def _halves(x):
    return x[:ATT_BLOCK], x[ATT_BLOCK:]


_halves.defvjp(lambda x: (_halves(x), None), lambda _, g: (jnp.concatenate([g[0], g[1]], axis=0),))


def _att_block(q, kp, kc, vp, vc, bpa, bpb, bca, bcb, prev_ok):
    n = ATT_BLOCK
    lane = lax.broadcasted_iota(jnp.int32, (1, 2 * HEAD_DIM), 1)
    ma = (lane < HEAD_DIM).astype(F32)
    mb = 1.0 - ma
    q = q.astype(F32) * (1.0 / math.sqrt(HEAD_DIM))
    q2 = jnp.concatenate([q * ma, q * mb], axis=0)
    qi = lax.broadcasted_iota(jnp.int32, (2 * n, n), 0) & (n - 1)
    kj = lax.broadcasted_iota(jnp.int32, (2 * n, n), 1)
    sp = jnp.where(jnp.logical_and(kj >= qi, prev_ok), _mm_nt(q2, kp) + jnp.concatenate([bpa, bpb], axis=0), NEG)
    sc = jnp.where(kj <= qi, _mm_nt(q2, kc) + jnp.concatenate([bca, bcb], axis=0), NEG)
    m = lax.stop_gradient(jnp.maximum(jnp.max(sp, axis=1, keepdims=True), jnp.max(sc, axis=1, keepdims=True)))
    pp = jnp.exp(sp - m)
    pc = jnp.exp(sc - m)
    l = jnp.sum(pp, axis=1, keepdims=True) + jnp.sum(pc, axis=1, keepdims=True)
    oa, ob = _halves((_mm(pp, vp) + _mm(pc, vc)) / l)
    la, lb = _halves((m + jnp.log(l)) * jnp.ones((1, 2 * HEAD_DIM), F32))
    return oa * ma + ob * mb, la * ma + lb * mb


def _att_slab(dil):
    nbk = max(1, 8 // dil)
    t = ATT_BLOCK * dil * nbk
    return nbk, t, S // t


def _att_in_specs(gi, t):
    def spec(which, prev):
        col = OFF_QKV // 128 + gi * 9 + which

        def index(p, j, col=col, prev=prev):
            jj = jnp.minimum(j, S // t - 1)
            return (jnp.maximum(jj - 1, 0) if prev else jj, col + 3 * p)
        return pl.BlockSpec((t, 2 * HEAD_DIM), index)
    return [spec(0, False), spec(1, False), spec(1, True), spec(2, False), spec(2, True)]


def _bias_specs():
    return [pl.BlockSpec((None, ATT_BLOCK, ATT_BLOCK), functools.partial(lambda p, j, hh: (2 * p + hh, 0, 0), hh=hh))
            for hh in (0, 1)]


def _att_units(dil, nbk, body):
    def per_residue(r, carry):
        for b in range(nbk):
            rows = pl.ds(b * ATT_BLOCK * dil + r, ATT_BLOCK, stride=dil)
            prev = pl.ds(((b - 1) % nbk) * ATT_BLOCK * dil + r, ATT_BLOCK, stride=dil)
            body(b, rows, prev, b > 0)
        return carry
    if dil == 1:
        per_residue(0, 0)
    else:
        lax.fori_loop(0, dil, per_residue, 0, unroll=min(dil, 8))


def _att_fwd(name, proj, gi, bias_p, bias_c):
    dil = ATT_GROUPS[gi][1]
    nbk, t, ns = _att_slab(dil)
    bsp = _bias_specs()
    out_spec = pl.BlockSpec((t, 2 * HEAD_DIM), lambda p, j: (j, p))

    def body(q_ref, kc_ref, kp_ref, vc_ref, vp_ref, bpa, bpb, bca, bcb, o_ref, l_ref):
        first = pl.program_id(1) == 0
        biases = (bpa[...], bpb[...], bca[...], bcb[...])

        def unit(b, rows, prev, in_slab):
            kp = kc_ref[prev, :] if in_slab else kp_ref[prev, :]
            vp = vc_ref[prev, :] if in_slab else vp_ref[prev, :]
            prev_ok = True if in_slab else jnp.logical_not(first)
            o, lse = _att_block(q_ref[rows, :], kp, kc_ref[rows, :], vp, vc_ref[rows, :], *biases, prev_ok)
            o_ref[rows, :] = o
            l_ref[rows, :] = lse

        _att_units(dil, nbk, unit)

    shp = jax.ShapeDtypeStruct((S, ATT_GW), F32)
    return pl.pallas_call(
        body, grid=(3, ns), in_specs=_att_in_specs(gi, t) + [bsp[0], bsp[1], bsp[0], bsp[1]],
        out_specs=[out_spec, out_spec], out_shape=[shp, shp], name=name,
        compiler_params=_cparams(("arbitrary",) * 2))(proj, proj, proj, proj, proj, bias_p, bias_p, bias_c, bias_c)


def _att_bwd(name, proj, gi, bias_p, bias_c, do, dl, dproj):
    dil = ATT_GROUPS[gi][1]
    nbk, t, ns = _att_slab(dil)
    bsp = _bias_specs()
    blk = (t, 2 * HEAD_DIM)
    cur = pl.BlockSpec(blk, lambda p, j: (jnp.minimum(j, ns - 1), p))
    done = pl.BlockSpec((t, QKV_W), lambda p, j: (jnp.maximum(j - 1, 0), OFF_QKV // QKV_W + gi * 3 + p))
    gsp = pl.BlockSpec((None, ATT_BLOCK, ATT_BLOCK), lambda p, j: (p, 0, 0))

    def body(q_ref, kc_ref, kp_ref, vc_ref, vp_ref, bpa, bpb, bca, bcb, do_ref, dl_ref, _,
             dqkv_ref, gpa, gpb, gca, gcb, accq, acck, accv):
        j = pl.program_id(1)
        mine, other = acck.at[j % 2], acck.at[1 - j % 2]
        mine_v, other_v = accv.at[j % 2], accv.at[1 - j % 2]
        dq_ref, other_q = accq.at[j % 2], accq.at[1 - j % 2]

        @pl.when(j == 0)
        def _():
            for g in (gpa, gpb, gca, gcb):
                g[...] = jnp.zeros_like(g)
            other[...] = jnp.zeros_like(other)
            other_v[...] = jnp.zeros_like(other_v)
            other_q[...] = jnp.zeros_like(other_q)

        @pl.when(j < ns)
        def _():
            mine[...] = jnp.zeros_like(mine)
            mine_v[...] = jnp.zeros_like(mine_v)
            biases = (bpa[...], bpb[...], bca[...], bcb[...])

            def unit(b, rows, prev, in_slab):
                kp = kc_ref[prev, :] if in_slab else kp_ref[prev, :]
                vp = vc_ref[prev, :] if in_slab else vp_ref[prev, :]
                prev_ok = True if in_slab else j > 0
                prim = (q_ref[rows, :], kp, kc_ref[rows, :], vp, vc_ref[rows, :]) + biases
                _, vjp = jax.vjp(lambda *a: _att_block(*a, prev_ok), *prim)
                dq, dkp, dkc, dvp, dvc, dpa, dpb, dca, dcb = vjp((do_ref[rows, :], dl_ref[rows, :]))
                dq_ref[rows, :] = dq
                mine[rows, :] += dkc
                mine_v[rows, :] += dvc
                tgt, tgt_v = (mine, mine_v) if in_slab else (other, other_v)
                tgt[prev, :] += dkp
                tgt_v[prev, :] += dvp
                gpa[...] += dpa
                gpb[...] += dpb
                gca[...] += dca
                gcb[...] += dcb

            _att_units(dil, nbk, unit)

        w = 2 * HEAD_DIM
        dqkv_ref[:, 0:w] = other_q[...].astype(BF16)
        dqkv_ref[:, w:2 * w] = other[...].astype(BF16)
        dqkv_ref[:, 2 * w:3 * w] = other_v[...].astype(BF16)

    gshp = jax.ShapeDtypeStruct((3, ATT_BLOCK, ATT_BLOCK), F32)
    res = pl.pallas_call(
        body, grid=(3, ns + 1),
        in_specs=_att_in_specs(gi, t) + [bsp[0], bsp[1], bsp[0], bsp[1], cur, cur, pl.BlockSpec(memory_space=pl.ANY)],
        out_specs=[done, gsp, gsp, gsp, gsp],
        out_shape=[jax.ShapeDtypeStruct((S, NP), BF16), gshp, gshp, gshp, gshp],
        input_output_aliases={11: 0},
        scratch_shapes=[pltpu.VMEM((2,) + blk, F32)] * 3, name=name,
        compiler_params=_cparams(("arbitrary",) * 2))(proj, proj, proj, proj, proj, bias_p, bias_p, bias_c, bias_c, do, dl,
                                                      dproj)
    dproj, gpa, gpb, gca, gcb = res
    heads = lambda a, b: jnp.stack([a, b], axis=1).reshape(6, ATT_BLOCK, ATT_BLOCK)
    return dproj, heads(gpa, gpb), heads(gca, gcb)


N_PAIR = SSD_HEADS // 2


def _ssd_chunk(xs, bs, cs_in, dt_raw, hs, a_row, dtb_row, ds):
    lane = lax.broadcasted_iota(jnp.int32, (1, 128), 1)
    row = lax.broadcasted_iota(jnp.int32, (128, 1), 0)
    tril = _tri(True)
    dt = _softplus(dt_raw + dtb_row)
    acs = _cumsum_rows(dt * a_row)
    acs_t = acs.T
    gmat = [_mm_nt(cs_in[g], bs[g]) for g in range(2)]
    lo = lane < HEAD_DIM
    lo_r = row < HEAD_DIM
    last = (row == SSD_CHUNK - 1).astype(F32)
    ys, hn = [], []
    for p in range(N_PAIR):
        g = p // (N_PAIR // 2)
        col, dtc, mm, clast = [], [], [], []
        for hh in range(2):
            h = 2 * p + hh
            oh = (lane == h).astype(F32)
            c_col = jnp.sum(acs * oh, axis=1, keepdims=True)
            c_row = jnp.sum(acs_t * (row == h).astype(F32), axis=0, keepdims=True)
            col.append(c_col)
            dtc.append(jnp.sum(dt * oh, axis=1, keepdims=True))
            clast.append(jnp.sum(c_col * last, axis=0, keepdims=True))
            mm.append(gmat[g] * jnp.exp(jnp.where(tril, c_col - c_row, NEG)))
        x = xs[p]
        xd = x * jnp.where(lo, dtc[0], dtc[1])
        y = jnp.where(lo, _mm(mm[0], xd), _mm(mm[1], xd))
        y = y + jnp.where(lo, jnp.exp(col[0]), jnp.exp(col[1])) * _mm_nt(cs_in[g], hs[p])
        ys.append(y + ds[p] * x)
        dec = jnp.where(lo, jnp.exp(clast[0] - col[0]), jnp.exp(clast[1] - col[1]))
        hn.append(hs[p] * jnp.where(lo_r, jnp.exp(clast[0]), jnp.exp(clast[1])) + _mm_tn(xd * dec, bs[g]))
    return tuple(ys), tuple(hn)


def _ssd_load(xbc_ref, dt_ref, a_ref, dtb_ref, d_ref):
    xs = tuple(xbc_ref[:, 128 * p:128 * (p + 1)] for p in range(N_PAIR))
    bs = tuple(xbc_ref[:, D + 128 * g:D + 128 * (g + 1)] for g in range(2))
    cs = tuple(xbc_ref[:, D + 256 + 128 * g:D + 256 + 128 * (g + 1)] for g in range(2))
    ds = tuple(d_ref[:, 128 * p:128 * (p + 1)] for p in range(N_PAIR))
    return xs, bs, cs, dt_ref[...], a_ref[...], dtb_ref[...], ds


def _ssd_fwd(name, xbc_c, proj, a_row, dtb_row, d_exp):
    nc = S // SSD_CHUNK
    prow = lambda n: pl.BlockSpec((1, n), lambda c: (0, 0))

    def body(xbc_ref, dt_ref, a_ref, dtb_ref, d_ref, y_ref, st_ref, h_ref):
        @pl.when(pl.program_id(0) == 0)
        def _():
            h_ref[...] = jnp.zeros_like(h_ref)

        xs, bs, cs, dt_raw, a, dtb, ds = _ssd_load(xbc_ref, dt_ref, a_ref, dtb_ref, d_ref)
        hs = tuple(h_ref[p] for p in range(N_PAIR))
        ys, hn = _ssd_chunk(xs, bs, cs, dt_raw, hs, a, dtb, ds)
        for p in range(N_PAIR):
            y_ref[:, 128 * p:128 * (p + 1)] = ys[p]
            st_ref[p] = hs[p]
            h_ref[p] = hn[p]

    return pl.pallas_call(
        body, grid=(nc,),
        in_specs=[pl.BlockSpec((SSD_CHUNK, SSD_XBC), lambda c: (c, 0)),
                  pl.BlockSpec((SSD_CHUNK, DT_PAD), lambda c: (c, OFF_DT // DT_PAD)),
                  prow(128), prow(128), prow(D)],
        out_specs=[pl.BlockSpec((SSD_CHUNK, D), lambda c: (c, 0)),
                   pl.BlockSpec((None, N_PAIR, 128, 128), lambda c: (c, 0, 0, 0))],
        out_shape=[jax.ShapeDtypeStruct((S, D), F32), jax.ShapeDtypeStruct((nc, N_PAIR, 128, 128), F32)],
        scratch_shapes=[pltpu.VMEM((N_PAIR, 128, 128), F32)], name=name,
        compiler_params=_cparams(("arbitrary",)))(xbc_c, proj, a_row, dtb_row, d_exp)


def _ssd_bwd(name, xbc_c, proj, states, dy, a_row, dtb_row, d_exp, dproj):
    nc = S // SSD_CHUNK
    prow = lambda n: pl.BlockSpec((1, n), lambda i: (0, 0))
    rc = lambda i: nc - 1 - i

    def body(xbc_ref, dt_ref, st_ref, dy_ref, a_ref, dtb_ref, d_ref, _, dxbc_ref, ddt_ref, da_ref, ddtb_ref, dd_ref, e_ref):
        i = pl.program_id(0)

        @pl.when(i == 0)
        def _():
            e_ref[...] = jnp.zeros_like(e_ref)
            da_ref[...] = jnp.zeros_like(da_ref)
            ddtb_ref[...] = jnp.zeros_like(ddtb_ref)
            dd_ref[...] = jnp.zeros_like(dd_ref)

        xs, bs, cs, dt_raw, a, dtb, ds = _ssd_load(xbc_ref, dt_ref, a_ref, dtb_ref, d_ref)
        hs = tuple(st_ref[p] for p in range(N_PAIR))
        _, vjp = jax.vjp(_ssd_chunk, xs, bs, cs, dt_raw, hs, a, dtb, ds)
        dys = tuple(dy_ref[:, 128 * p:128 * (p + 1)] for p in range(N_PAIR))
        es = tuple(e_ref[p] for p in range(N_PAIR))
        dxs, dbs, dcs, ddt, dhs, da, ddtb, dds = vjp((dys, es))
        for p in range(N_PAIR):
            dxbc_ref[:, 128 * p:128 * (p + 1)] = dxs[p]
            e_ref[p] = dhs[p]
            dd_ref[:, 128 * p:128 * (p + 1)] += dds[p]
        for g in range(2):
            dxbc_ref[:, D + 128 * g:D + 128 * (g + 1)] = dbs[g]
            dxbc_ref[:, D + 256 + 128 * g:D + 256 + 128 * (g + 1)] = dcs[g]
        ddt_ref[:, :DT_PAD] = ddt.astype(BF16)
        ddt_ref[:, DT_PAD:] = jnp.zeros((SSD_CHUNK, OFF_QKV - OFF_DT - DT_PAD), BF16)
        da_ref[...] += da
        ddtb_ref[...] += ddtb

    dt_w = OFF_QKV - OFF_DT
    return pl.pallas_call(
        body, grid=(nc,),
        in_specs=[pl.BlockSpec((SSD_CHUNK, SSD_XBC), lambda i: (rc(i), 0)),
                  pl.BlockSpec((SSD_CHUNK, DT_PAD), lambda i: (rc(i), OFF_DT // DT_PAD)),
                  pl.BlockSpec((None, N_PAIR, 128, 128), lambda i: (rc(i), 0, 0, 0)),
                  pl.BlockSpec((SSD_CHUNK, D), lambda i: (rc(i), 0)),
                  prow(128), prow(128), prow(D), pl.BlockSpec(memory_space=pl.ANY)],
        out_specs=[pl.BlockSpec((SSD_CHUNK, SSD_XBC), lambda i: (rc(i), 0)),
                   pl.BlockSpec((SSD_CHUNK, dt_w), lambda i: (rc(i), OFF_DT // dt_w)),
                   prow(128), prow(128), prow(D)],
        out_shape=[jax.ShapeDtypeStruct((S, SSD_XBC), F32), jax.ShapeDtypeStruct((S, NP), BF16),
                   jax.ShapeDtypeStruct((1, 128), F32), jax.ShapeDtypeStruct((1, 128), F32),
                   jax.ShapeDtypeStruct((1, D), F32)],
        input_output_aliases={7: 1},
        scratch_shapes=[pltpu.VMEM((N_PAIR, 128, 128), F32)], name=name,
        compiler_params=_cparams(("arbitrary",)))(xbc_c, proj, states, dy, a_row, dtb_row, d_exp, dproj)


def _exchange(name, arrays, scatter):
    n = len(arrays)
    flips = [(dx, dy, dc) for dx in (0, 1) for dy in (0, 1) for dc in (0, 1) if dx or dy or dc]

    def body(*refs):
        ins, outs = refs[:n], refs[n:2 * n]
        send_sems, recv_sems, loc_sems = refs[2 * n:]
        x, y, c = lax.axis_index("x"), lax.axis_index("y"), lax.axis_index("c")
        me = 4 * x + 2 * y + c
        peers = []
        for dx, dy, dc in flips:
            px, py, pc = (1 - x if dx else x), (1 - y if dy else y), (1 - c if dc else c)
            peers.append(((px, py, pc), 4 * px + 2 * py + pc))

        def remote(k, j, landed_from):
            dev, pid = peers[j]
            src = ins[k].at[pid] if scatter else ins[k]
            return pltpu.make_async_remote_copy(
                src_ref=src, dst_ref=outs[k].at[landed_from], send_sem=send_sems.at[k, j], recv_sem=recv_sems.at[k, j],
                device_id=dev, device_id_type=pl.DeviceIdType.MESH)

        local = [pltpu.make_async_copy(ins[k].at[me] if scatter else ins[k], outs[k].at[me], loc_sems.at[k])
                 for k in range(n)]
        for cp in local:
            cp.start()
        for k in range(n):
            for j in range(len(flips)):
                remote(k, j, me).start()
        for cp in local:
            cp.wait()
        for k in range(n):
            for j in range(len(flips)):
                remote(k, j, me).wait_send()
                remote(k, j, peers[j][1]).wait_recv()

    hbm = pl.BlockSpec(memory_space=pltpu.HBM)
    out_shape = [jax.ShapeDtypeStruct(a.shape if scatter else (N_DEV,) + a.shape, a.dtype) for a in arrays]
    res = pl.pallas_call(
        body, in_specs=[hbm] * n, out_specs=[hbm] * n, out_shape=out_shape, name=name,
        scratch_shapes=[pltpu.SemaphoreType.DMA((n, len(flips))), pltpu.SemaphoreType.DMA((n, len(flips))),
                        pltpu.SemaphoreType.DMA((n,))])(*arrays)
    return list(res)


def _gather_chip_once(name, block):
    def body(x_ref, out_ref, send_sems, recv_sems, loc_sem):
        x, y, c = lax.axis_index("x"), lax.axis_index("y"), lax.axis_index("c")
        me, sibling = (x, y, c), (x, y, 1 - c)
        chips = [(1 - x, y), (x, 1 - y), (1 - x, 1 - y)]

        def slot(px, py, pc):
            return out_ref.at[4 * px + 2 * py + pc]

        def copy(k, blk, to, src=None):
            return pltpu.make_async_remote_copy(
                src_ref=slot(*blk) if src is None else src, dst_ref=slot(*blk), send_sem=send_sems.at[k],
                recv_sem=recv_sems.at[k], device_id=to, device_id_type=pl.DeviceIdType.MESH)

        mine = pltpu.make_async_copy(x_ref, slot(*me), loc_sem)
        mine.start()
        first = [copy(0, me, sibling, src=x_ref)] + [copy(1 + j, me, (*chip, c), src=x_ref) for j, chip in enumerate(chips)]
        for cp in first:
            cp.start()
        passed = [copy(4 + j, (*chip, c), sibling) for j, chip in enumerate(chips)]
        for j, chip in enumerate(chips):
            copy(1 + j, (*chip, c), me).wait_recv()
            passed[j].start()
        copy(0, sibling, me).wait_recv()
        for j, chip in enumerate(chips):
            copy(4 + j, (*chip, 1 - c), me).wait_recv()
        for cp in first + passed:
            cp.wait_send()
        mine.wait()

    hbm = pl.BlockSpec(memory_space=pltpu.HBM)
    return pl.pallas_call(
        body, in_specs=[hbm], out_specs=hbm, out_shape=jax.ShapeDtypeStruct((N_DEV,) + block.shape, block.dtype), name=name,
        scratch_shapes=[pltpu.SemaphoreType.DMA((N_DEV - 1,)), pltpu.SemaphoreType.DMA((N_DEV - 1,)),
                        pltpu.SemaphoreType.DMA(())])(block)


def _peer_copies(ins, lands, send_sems, recv_sems, loc_sems, scatter):
    n = len(ins)
    flips = [(dx, dy, dc) for dx in (0, 1) for dy in (0, 1) for dc in (0, 1) if dx or dy or dc]
    x, y, c = lax.axis_index("x"), lax.axis_index("y"), lax.axis_index("c")
    me = 4 * x + 2 * y + c
    peers = []
    for dx, dy, dc in flips:
        px, py, pc = (1 - x if dx else x), (1 - y if dy else y), (1 - c if dc else c)
        peers.append(((px, py, pc), 4 * px + 2 * py + pc))

    def remote(k, j, slot):
        dev, pid = peers[j]
        return pltpu.make_async_remote_copy(
            src_ref=ins[k].at[pid] if scatter else ins[k], dst_ref=lands[k].at[slot],
            send_sem=send_sems.at[k * N_FLIP + j], recv_sem=recv_sems.at[k * N_FLIP + j],
            device_id=dev, device_id_type=pl.DeviceIdType.MESH)

    local = [pltpu.make_async_copy(ins[k].at[me] if scatter else ins[k], lands[k].at[me], loc_sems.at[k])
             for k in range(n)]
    pairs = [(k, j) for k in range(n) for j in range(len(flips))]
    sent = lambda k, j: remote(k, j, me)
    landed = lambda k, j: remote(k, j, peers[j][1])
    return local, pairs, sent, landed


_HBM = pl.BlockSpec(memory_space=pltpu.HBM)
_SEM = pl.BlockSpec(memory_space=pltpu.SEMAPHORE)
N_FLIP = N_DEV - 1


def _exchange_start(name, arrays, scatter, after):
    n = len(arrays)
    arrays = [pltpu.with_memory_space_constraint(a, pltpu.HBM) for a in arrays]
    lands = [pltpu.with_memory_space_constraint(
        lax.empty(a.shape if scatter else (N_DEV,) + a.shape, a.dtype), pltpu.HBM) for a in arrays]

    def body(*refs):
        ins, lnd = refs[:n], refs[n:2 * n]
        send_sems, recv_sems, loc_sems = refs[2 * n + 1:2 * n + 4]
        token = refs[-1]
        local, pairs, sent, _ = _peer_copies(ins, lnd, send_sems, recv_sems, loc_sems, scatter)
        for cp in local:
            cp.start()
        for k, j in pairs:
            sent(k, j).start()
        token[...] = jnp.zeros_like(token)

    res = pl.pallas_call(
        body, name=name,
        in_specs=[_HBM] * (2 * n) + [pl.BlockSpec(memory_space=pl.ANY)],
        out_specs=[_SEM, _SEM, _SEM] + [_HBM] * (2 * n) + [pl.BlockSpec(memory_space=pltpu.VMEM)],
        out_shape=[pltpu.SemaphoreType.DMA((n * N_FLIP,)), pltpu.SemaphoreType.DMA((n * N_FLIP,)), pltpu.SemaphoreType.DMA((n,))]
        + [pltpu.HBM(a.shape, a.dtype) for a in arrays] + [pltpu.HBM(a.shape, a.dtype) for a in lands]
        + [jax.ShapeDtypeStruct((8, 128), F32)],
        input_output_aliases={k: 3 + k for k in range(2 * n)},
        compiler_params=pltpu.CompilerParams(has_side_effects=pltpu.SideEffectType.DATAFLOW_SIDE_EFFECTING),
    )(*arrays, *lands, after)
    return (res[:3], res[3:3 + n], res[3 + n:3 + 2 * n], scatter), res[-1]


def _exchange_wait(name, state, after):
    sems, ins_thru, lands_thru, scatter = state
    n = len(ins_thru)

    def body(*refs):
        ins, lnd = refs[:n], refs[n:2 * n]
        send_sems, recv_sems, loc_sems = refs[2 * n:2 * n + 3]
        local, pairs, sent, landed = _peer_copies(ins, lnd, send_sems, recv_sems, loc_sems, scatter)
        for cp in local:
            cp.wait()
        for k, j in pairs:
            sent(k, j).wait_send()
            landed(k, j).wait_recv()

    res = pl.pallas_call(
        body, name=name,
        in_specs=[_HBM] * (2 * n) + [_SEM, _SEM, _SEM] + [pl.BlockSpec(memory_space=pl.ANY)],
        out_specs=[_HBM] * (2 * n),
        out_shape=[pltpu.HBM(a.shape, a.dtype) for a in ins_thru] + [pltpu.HBM(a.shape, a.dtype) for a in lands_thru],
        input_output_aliases={k: k for k in range(2 * n)},
        compiler_params=pltpu.CompilerParams(has_side_effects=pltpu.SideEffectType.DATAFLOW_SIDE_EFFECTING),
    )(*ins_thru, *lands_thru, *sems, after)
    return list(res[n:])


def _adamw_fn(*vals):
    slots, (w, m, v) = vals[:N_DEV], vals[N_DEV:]
    g = slots[0].astype(F32)
    for s in slots[1:]:
        g = g + s.astype(F32)
    m2 = ADAM_B1 * m + (1.0 - ADAM_B1) * g
    v2 = ADAM_B2 * v + (1.0 - ADAM_B2) * (g * g)
    m_hat = m2 / (1.0 - ADAM_B1 ** ADAM_STEP)
    v_hat = v2 / (1.0 - ADAM_B2 ** ADAM_STEP)
    delta = -ADAM_LR * (m_hat / (jnp.sqrt(v_hat) + ADAM_EPS) + ADAM_WD * w)
    return (g, delta, m2, v2), ()


def _adamw(name, slots, w, m, v, first_row=0, prev=None):
    R, C = slots.shape[1:]
    tm = R if R <= 128 else _pick(R, 256, 8)
    rows = ([_rows(slots, C, lead=s) for s in range(N_DEV)]
            + [_rows(a, C, roff=first_row // tm) for a in (w, m, v)])
    return _tiled(name, _adamw_fn, (1, R // tm), tm, rows, [], [(w.shape[0], C, F32)] * 4,
                  out_roff=first_row // tm, prev_outs=prev)


def _bucket_onehots():
    out = []
    qi = jnp.arange(ATT_BLOCK)[:, None]
    kj = jnp.arange(ATT_BLOCK)[None, :]
    max_exact = REL_BUCKETS // 2
    for _, dil in ATT_GROUPS:
        parts = []
        for rel in (qi + ATT_BLOCK - kj, qi - kj):
            dist = jnp.clip(rel, 0, None) * dil
            nf = jnp.maximum(dist, 1).astype(F32)
            large = max_exact + (jnp.log(nf / max_exact) / math.log(REL_MAX_DISTANCE / max_exact)
                                 * (REL_BUCKETS - max_exact)).astype(jnp.int32)
            large = jnp.minimum(large, REL_BUCKETS - 1)
            bucket = jnp.where(dist < max_exact, dist, large)
            parts.append((bucket[:, :, None] == jnp.arange(REL_BUCKETS)[None, None, :]).astype(F32))
        out.append(jnp.stack(parts))
    return out


SHARDED = ("w_in", "w_a", "pool_w", "w_b", "ssd_conv_w", "w_c", "w_o", "ffn_w_up", "ffn_conv_w", "ffn_w_down")
MATMUL_WEIGHTS = ("w_in", "w_a", "pool_w", "w_b", "w_c", "w_o", "ffn_w_up", "ffn_w_down")
ROW_SHARDED = ("w_b", "w_c", "w_o", "ffn_w_down")
W_IN_SEGS = tuple(
    (which * ATT_W + unit * 128, which * ATT_W + (unit + 1) * 128, OFF_QKV + unit * QKV_W + which * 128)
    for unit in range(9) for which in range(3)
) + ((3456, 4480, OFF_POOL), (4480, 5504, OFF_Z), (5504, 7040, OFF_XBC), (7040, 7056, OFF_DT), (7056, IN_WIDTH, OFF_GATE))
FFN_SEGS = tuple((h * D_FF + j * 128, h * D_FF + (j + 1) * 128, j * 256 + h * 128)
                 for j in range(D_FF // 128) for h in range(2))
COL_SHARDED = {
    "w_in": (IN_WIDTH // N_DEV, W_IN_SEGS, ((OFF_DT + SSD_HEADS, OFF_QKV),), NP),
    "w_a": (D // N_DEV, ((0, D, 0),), (), D),
    "ffn_w_up": (2 * D_FF // N_DEV, FFN_SEGS, (), 2 * D_FF),
    "ssd_conv_w": (SSD_XBC // N_DEV, ((0, SSD_XBC, 0),), (), SSD_XBC),
    "ffn_conv_w": (2 * D_FF // N_DEV, FFN_SEGS, (), 2 * D_FF),
}
REPLICATED = ("rel_bias", "ln1_g", "b_gate", "pool_scale", "ssd_conv_b", "ssd_dt_bias", "ssd_a_log", "ssd_d",
              "ssd_norm_w", "ln2_g", "ffn_conv_b", "final_g")
WEIGHTS = ("rel_bias", "ln1_g", "w_in", "b_gate", "w_a", "pool_w", "pool_scale", "w_b", "ssd_conv_w", "ssd_conv_b",
           "ssd_dt_bias", "ssd_a_log", "ssd_d", "ssd_norm_w", "w_c", "w_o", "ln2_g", "ffn_w_up", "ffn_conv_w",
           "ffn_conv_b", "ffn_w_down", "final_g")


def _local_weight(name, n, blocks):
    if n in COL_SHARDED:
        c, segs, zeros, width = COL_SHARDED[n]
        return _col_assemble(name, blocks, _seg_copies(segs, c), zeros, width)
    if n in ROW_SHARDED:
        return blocks.reshape(-1, blocks.shape[-1])
    return blocks


def _device_blocks(name, n, g):
    if n in COL_SHARDED:
        c, segs, _, _ = COL_SHARDED[n]
        return _col_split(name, g, _seg_copies(segs, c), c, BF16)
    if n in ROW_SHARDED:
        return g.reshape(N_DEV, g.shape[0] // N_DEV, g.shape[1]).astype(BF16)
    return g.astype(BF16)


def _row(v, n=None):
    v = v.reshape(1, -1)
    if n is not None and v.shape[1] < n:
        v = jnp.pad(v, ((0, 0), (0, n - v.shape[1])))
    return v


RT = 512


def _row_call(name, fn, cw, ncol, rows, params, outs, accs=(), into=None):
    return _tiled(name, fn, (ncol, S // RT), RT, rows, params, [(S, cw, dt) for dt in outs], accs, into=into)


def _col_call(name, fn, tc, ncol, rows, params, outs, accs=(), into=None):
    return _tiled(name, fn, (ncol, 1), S, rows, params, [(S, tc, dt) for dt in outs], accs, into=into)


def _fwd_only(fn):
    return lambda *a: (fn(*a), ())


def _layer_fwd(i, x, W, P, bias_tabs, late=None, late_ffn=None):
    sv = {"x": x}
    (u,) = _row_call(f"ln1_f{i}", _fwd_only(_rmsnorm_fn), D, 1, [_rows(x, D)], [_p_row(P["ln1_g"], D)], [BF16])
    proj = _matmul(f"inproj_f{i}", u, W["w_in"], "nn")
    sv["u"], sv["proj"] = u, proj

    os_, ls_ = [], []
    for gi in range(len(ATT_GROUPS)):
        o, lse = _att_fwd(f"att_f{i}_{gi}", proj, gi, bias_tabs[gi][0], bias_tabs[gi][1])
        os_.append(o)
        ls_.append(lse)
    sv["att_o"], sv["att_l"] = os_, ls_
    (att,) = _row_call(f"attmerge_f{i}", _fwd_only(_att_merge_fn), ATT_GW, 1,
                       [_rows(t, ATT_GW) for t in os_ + ls_], [], [BF16])
    if late is not None:
        W2, P2 = late(att)
        W.update(W2)
        P.update(P2)
    y_a = _matmul(f"wa_f{i}", att, W["w_a"], "nn", out_dtype=BF16)
    sv["att"], sv["y_a"] = att, y_a

    pool_params = [(W["pool_w"], (N_DEV, None, 32, 256), lambda j, i_: (0, j, 0, 0)), _p_row(P["pool_scale"], 256)]
    (yb_pre,) = _col_call(f"pool_f{i}", _fwd_only(_pool_fn), 256, 4, [_rows(proj, 256, OFF_POOL // 256)],
                          pool_params, [BF16])
    y_b = _matmul(f"wb_f{i}", yb_pre, W["w_b"], "nn", out_dtype=BF16)
    sv["yb_pre"], sv["y_b"] = yb_pre, y_b

    conv_params = [_p_row(P["ssd_conv_w"][k], 128) for k in range(4)] + [_p_row(P["ssd_conv_b"], 128)]
    (xbc_c,) = _col_call(f"ssdconv_f{i}", _fwd_only(_ssd_conv_fn), 128, SSD_XBC // 128,
                         [_rows(proj, 128, OFF_XBC // 128)], conv_params, [F32])
    y_ssd, states = _ssd_fwd(f"ssd_f{i}", xbc_c, proj, P["a_row"], P["dtb_row"], P["d_exp"])
    (yc_pre,) = _row_call(f"ssdnorm_f{i}", _fwd_only(_gated_norm_fn), 512, 2,
                          [_rows(y_ssd, 512), _rows(proj, 512, OFF_Z // 512)], [_p_row(P["ssd_norm_w"], 512)], [BF16])
    y_c = _matmul(f"wc_f{i}", yc_pre, W["w_c"], "nn", out_dtype=BF16)
    sv["xbc_c"], sv["states"], sv["y_ssd"], sv["yc_pre"], sv["y_c"] = xbc_c, states, y_ssd, yc_pre, y_c

    gate_rows = [_rows(proj, D, k) for k in range(3)] + [_rows(t, D) for t in (y_a, y_b, y_c)]
    gate_params = [_p_row(P["b_gate"], D, k) for k in range(3)]
    (merged,) = _row_call(f"gate_f{i}", _fwd_only(_gate_merge_fn), D, 1, gate_rows, gate_params, [BF16])
    x1 = _matmul(f"wo_f{i}", merged, W["w_o"], "nn", add=x)
    sv["merged"], sv["x1"] = merged, x1

    if late_ffn is not None:
        W2, P2 = late_ffn(x1)
        W.update(W2)
        P.update(P2)
    (u2,) = _row_call(f"ln2_f{i}", _fwd_only(_rmsnorm_fn), D, 1, [_rows(x1, D)], [_p_row(P["ln2_g"], D)], [BF16])
    up = _matmul(f"up_f{i}", u2, W["ffn_w_up"], "nn", out_dtype=BF16)
    (act,) = _col_call(f"ffnact_f{i}", lambda *a: (_ffn_act_fn(*_ffn_halves(a)), ()), 128, D_FF // 128,
                       [_rows(up, 256)], _ffn_params(P), [BF16])
    x2 = _matmul(f"down_f{i}", act, W["ffn_w_down"], "nn", add=x1)
    sv["u2"], sv["up"], sv["act"] = u2, up, act
    return x2, sv


def _ffn_params(P):
    return [_p_row(P["ffn_conv_w"][k], 256) for k in range(3)] + [_p_row(_interleave_ffn(P["ffn_conv_b"]), 256)]


def _ffn_halves(vals):
    (xa, xv), (a0, v0), (a1, v1), (a2, v2), (ab, vb) = ((t[:, :128], t[:, 128:]) for t in vals)
    return xa, xv, a0, a1, a2, ab, v0, v1, v2, vb


def _interleave_ffn(row):
    return row.reshape(2, D_FF // 128, 128).transpose(1, 0, 2).reshape(1, 2 * D_FF)


def _deinterleave_ffn(row):
    return row.reshape(D_FF // 128, 2, 128).transpose(1, 0, 2).reshape(1, 2 * D_FF)


def _layer_bwd(i, dx2, sv, W, P, bias_tabs, onehots, on_ffn_grads, on_sharded_grads):
    G = {}
    x, proj, x1 = sv["x"], sv["proj"], sv["x1"]

    dact = _matmul(f"down_bx{i}", dx2, W["ffn_w_down"], "nt", out_dtype=BF16)
    G["ffn_w_down"] = _matmul(f"down_bw{i}", sv["act"], dx2, "tn", out_dtype=BF16)
    nb = D_FF // 128
    up = sv["up"]
    def ffn_bwd(x2, t0, t1, t2, b2, dact_):
        (dxa, dxv), (a0, a1, a2, ab, v0, v1, v2, vb) = _with_vjp(_ffn_act_fn, 10, (0, 1), tuple(range(2, 10)))(
            *_ffn_halves((x2, t0, t1, t2, b2)), dact_)
        pair = lambda a, v: jnp.concatenate([a, v], axis=1)
        return (pair(dxa, dxv),), (pair(a0, v0), pair(a1, v1), pair(a2, v2), pair(ab, vb))

    dup, t0, t1, t2, tb = _col_call(
        f"ffnact_b{i}", ffn_bwd, 256, nb, [_rows(up, 256)], _ffn_params(P) + [_rows_as_param(dact, 128)],
        [BF16], [_a_row(2 * D_FF, 256)] * 4)
    G["ffn_conv_w"] = jnp.concatenate([t0, t1, t2], axis=0)
    G["ffn_conv_b"] = _deinterleave_ffn(tb)[0]
    du2 = _matmul(f"up_bx{i}", dup, W["ffn_w_up"], "nt")
    G["ffn_w_up"] = _matmul(f"up_bw{i}", sv["u2"], dup, "tn", out_dtype=BF16)

    def norm_bwd(x_, g_, du_, dres):
        (dxn,), (dg,) = _with_vjp(_rmsnorm_fn, 2, (0,), (1,))(x_, g_, du_)
        return (dxn + dres,), (dg,)

    ln2_g = P["ln2_g"] + on_ffn_grads(G)
    (dx1,), (G["ln2_g"],) = _split_res(_row_call(
        f"ln2_b{i}", lambda x_, du_, dres, g_: norm_bwd(x_, g_, du_, dres), D, 1,
        [_rows(x1, D), _rows(du2, D), _rows(dx2, D)], [_p_row(ln2_g, D)], [F32], [_a_row(D, D)]), 1)

    dmerged = _matmul(f"wo_bx{i}", dx1, W["w_o"], "nt", out_dtype=BF16)
    G["w_o"] = _matmul(f"wo_bw{i}", sv["merged"], dx1, "tn", out_dtype=BF16)
    def gate_bwd(g0, g1, g2, ya, yb, yc, dm, b0, b1, b2):
        (d0, d1, d2, da, db, dc), dbias = _with_vjp(_gate_merge_fn, 9, (0, 1, 2, 3, 4, 5), (6, 7, 8))(
            g0, g1, g2, ya, yb, yc, b0, b1, b2, dm)
        dgate = jnp.concatenate([t.astype(BF16) for t in (d0, d1, d2)], axis=1)
        return (dgate, da, db, dc), dbias

    gate_rows = [_rows(proj, D, k) for k in range(3)] + [_rows(sv[t], D) for t in ("y_a", "y_b", "y_c")]
    dproj, dya, dyb, dyc, db0, db1, db2 = _tiled(
        f"gate_b{i}", gate_bwd, (1, S // RT), RT, gate_rows + [_rows(dmerged, D)],
        [_p_row(P["b_gate"], D, k) for k in range(3)], [(S, 3 * D, BF16)] + [(S, D, BF16)] * 3, [_a_row(D, D)] * 3,
        into={0: (None, 0, NP)})
    G["b_gate"] = jnp.concatenate([db0, db1, db2], axis=1)[0]

    dyc_pre = _matmul(f"wc_bx{i}", dyc, W["w_c"], "nt", out_dtype=BF16)
    G["w_c"] = _matmul(f"wc_bw{i}", sv["yc_pre"], dyc, "tn", out_dtype=BF16)

    def gnorm_bwd(y_, z_, dy_, w_):
        return _with_vjp(_gated_norm_fn, 3, (0, 1), (2,))(y_, z_, w_, dy_)

    dy_ssd, dproj, dnw = _row_call(
        f"ssdnorm_b{i}", gnorm_bwd, 512, 2,
        [_rows(sv["y_ssd"], 512), _rows(proj, 512, OFF_Z // 512), _rows(dyc_pre, 512)],
        [_p_row(P["ssd_norm_w"], 512)], [F32, BF16], [_a_row(D, 512)], into={1: (dproj, OFF_Z // 512, NP)})
    G["ssd_norm_w"] = dnw[0]
    dxbc_c, dproj, da_row, ddtb_row, dd_exp = _ssd_bwd(f"ssd_b{i}", sv["xbc_c"], proj, sv["states"], dy_ssd,
                                                       P["a_row"], P["dtb_row"], P["d_exp"], dproj)
    a_vec = P["a_row"][0, :SSD_HEADS]
    G["ssd_a_log"] = da_row[0, :SSD_HEADS] * a_vec
    G["ssd_dt_bias"] = ddtb_row[0, :SSD_HEADS]
    G["ssd_d"] = dd_exp.reshape(SSD_HEADS, HEAD_DIM).sum(axis=1)
    conv_params = [_p_row(P["ssd_conv_w"][k], 128) for k in range(4)] + [_p_row(P["ssd_conv_b"], 128)]

    def conv_bwd(x_, dy_, w0, w1, w2, w3, b_):
        return _with_vjp(_ssd_conv_fn, 6, (0,), (1, 2, 3, 4, 5))(x_, w0, w1, w2, w3, b_, dy_)

    dproj, c0, c1, c2, c3, cb = _col_call(
        f"ssdconv_b{i}", conv_bwd, 128, SSD_XBC // 128, [_rows(proj, 128, OFF_XBC // 128), _rows(dxbc_c, 128)],
        conv_params, [BF16], [_a_row(SSD_XBC, 128)] * 5, into={0: (dproj, OFF_XBC // 128, NP)})
    G["ssd_conv_w"] = jnp.concatenate([c0, c1, c2, c3], axis=0)
    G["ssd_conv_b"] = cb[0]

    dyb_pre = _matmul(f"wb_bx{i}", dyb, W["w_b"], "nt", out_dtype=BF16)
    G["w_b"] = _matmul(f"wb_bw{i}", sv["yb_pre"], dyb, "tn", out_dtype=BF16)
    pool_params = [(W["pool_w"], (N_DEV, None, 32, 256), lambda j, i_: (0, j, 0, 0)), _p_row(P["pool_scale"], 256)]

    def pool_bwd(x_, dy_, wg, sc):
        return _with_vjp(_pool_fn, 3, (0,), (1, 2))(x_, wg.astype(F32), sc, dy_)

    dproj, dwg, dsc = _col_call(
        f"pool_b{i}", pool_bwd, 256, 4, [_rows(proj, 256, OFF_POOL // 256), _rows(dyb_pre, 256)], pool_params, [BF16],
        [((N_DEV, 4, 32, 256), (N_DEV, None, 32, 256), lambda j, i_: (0, j, 0, 0)), _a_row(D, 256)],
        into={0: (dproj, OFF_POOL // 256, NP)})
    G["pool_w"] = dwg
    G["pool_scale"] = dsc[0]

    datt = _matmul(f"wa_bx{i}", dya, W["w_a"], "nt", out_dtype=BF16)
    G["w_a"] = _matmul(f"wa_bw{i}", sv["att"], dya, "tn", out_dtype=BF16)

    def merge_bwd(o0, o1, o2, l0, l1, l2, da_):
        return _with_vjp(_att_merge_fn, 6, (0, 1, 2, 3, 4, 5), ())(o0, o1, o2, l0, l1, l2, da_)

    dol = _row_call(f"attmerge_b{i}", merge_bwd, ATT_GW, 1,
                    [_rows(t, ATT_GW) for t in sv["att_o"] + sv["att_l"]] + [_rows(datt, ATT_GW)], [], [F32] * 6)
    g_rel = jnp.zeros((REL_BUCKETS, 18), F32)
    for gi in range(len(ATT_GROUPS)):
        dproj, gbp, gbc = _att_bwd(f"att_b{i}_{gi}", proj, gi, bias_tabs[gi][0], bias_tabs[gi][1],
                                   dol[gi], dol[3 + gi], dproj)
        oh = onehots[gi]
        gt = (jnp.einsum("hqk,qkb->bh", gbp, oh[0], precision=lax.Precision.HIGHEST)
              + jnp.einsum("hqk,qkb->bh", gbc, oh[1], precision=lax.Precision.HIGHEST))
        g_rel = g_rel.at[:, gi * 6:(gi + 1) * 6].add(gt)
    G["rel_bias"] = g_rel

    du = _matmul(f"inproj_bx{i}", dproj, W["w_in"], "nt")
    G["w_in"] = _matmul(f"inproj_bw{i}", sv["u"], dproj, "tn", out_dtype=BF16)
    ln1_g = P["ln1_g"] + on_sharded_grads(G)
    (dx,), (G["ln1_g"],) = _split_res(_row_call(
        f"ln1_b{i}", lambda x_, du_, dres, g_: norm_bwd(x_, g_, du_, dres), D, 1,
        [_rows(x, D), _rows(du, D), _rows(dx1, D)], [_p_row(ln1_g, D)], [F32], [_a_row(D, D)]), 1)
    G["ln1_g"] = G["ln1_g"][0]
    G["ln2_g"] = G["ln2_g"][0]
    return dx, G


def _rows_as_param(arr, cw):
    return (arr, (arr.shape[0], cw), lambda j, i: (0, j))


def _split_res(res, n_out):
    return tuple(res[:n_out]), tuple(res[n_out:])


def kernel(x, rel_bias, ln1_g, w_in, b_gate, w_a, pool_w, pool_scale, w_b, ssd_conv_w, ssd_conv_b, ssd_dt_bias, ssd_a_log, ssd_d, ssd_norm_w, w_c, w_o, ln2_g, ffn_w_up, ffn_conv_w, ffn_conv_b, ffn_w_down, final_g, loss_target, m_rel_bias, m_ln1_g, m_w_in, m_b_gate, m_w_a, m_pool_w, m_pool_scale, m_w_b, m_ssd_conv_w, m_ssd_conv_b, m_ssd_dt_bias, m_ssd_a_log, m_ssd_d, m_ssd_norm_w, m_w_c, m_w_o, m_ln2_g, m_ffn_w_up, m_ffn_conv_w, m_ffn_conv_b, m_ffn_w_down, m_final_g, v_rel_bias, v_ln1_g, v_w_in, v_b_gate, v_w_a, v_pool_w, v_pool_scale, v_w_b, v_ssd_conv_w, v_ssd_conv_b, v_ssd_dt_bias, v_ssd_a_log, v_ssd_d, v_ssd_norm_w, v_w_c, v_w_o, v_ln2_g, v_ffn_w_up, v_ffn_conv_w, v_ffn_conv_b, v_ffn_w_down, v_final_g):
    args = locals()
    wts = {n: args[n] for n in WEIGHTS}
    mom = {n: args["m_" + n] for n in WEIGHTS}
    var = {n: args["v_" + n] for n in WEIGHTS}
    names = list(SHARDED)

    onehots = _bucket_onehots()
    bias_tabs = []
    for gi in range(3):
        tab = rel_bias[:, gi * 6:(gi + 1) * 6]
        b = jnp.einsum("pqkb,bh->phqk", onehots[gi], tab, precision=lax.Precision.HIGHEST)
        bias_tabs.append((b[0], b[1]))

    def gather_start(tag, i, which, after):
        shards = [wts[n][i].astype(BF16) if n in MATMUL_WEIGHTS else wts[n][i] for n in which]
        return _exchange_start(f"gather_start{tag}", shards, False, after)

    def layer_params(i, which, landed):
        full = {n: _local_weight(f"local_{n}{i}", n, g) for n, g in zip(which, landed)}
        W = {n: full[n] for n in which if n in MATMUL_WEIGHTS}
        P = {n: [_row(full[n][k]) for k in range(full[n].shape[0])] for n in ("ssd_conv_w", "ffn_conv_w") if n in full}
        return W, P

    def replicated_params(i):
        return {"ln1_g": _row(ln1_g[i]), "ln2_g": _row(ln2_g[i]), "b_gate": _row(b_gate[i]),
                "pool_scale": _row(pool_scale[i]), "ssd_conv_b": _row(ssd_conv_b[i]),
                "ssd_norm_w": _row(ssd_norm_w[i]), "ffn_conv_b": _row(ffn_conv_b[i]),
                "a_row": _row(-jnp.exp(ssd_a_log[i]), 128), "dtb_row": _row(ssd_dt_bias[i], 128),
                "d_exp": _row(jnp.repeat(ssd_d[i], HEAD_DIM))}

    h = x.reshape(S, D)
    saved, Ws, Ps = [], [], []
    ffn = ["ffn_w_up", "ffn_conv_w", "ffn_w_down"]
    core = [n for n in names if n not in ffn]
    first, rest = ["w_in"], [n for n in core if n != "w_in"]
    landed_first = [_gather_chip_once("gather_w_in0", w_in[0].astype(BF16))]
    state_rest, token = gather_start("0r", 0, rest, landed_first[0])
    landed = None

    def late0(att):
        return layer_params(0, rest, _exchange_wait("gather_wait0r", state_rest, att))

    for i in range(DEPTH):
        P = replicated_params(i)
        W, P1 = layer_params(0, first, landed_first) if i == 0 else layer_params(i, core, landed)
        P.update(P1)
        state_ffn, tok = gather_start(f"{i}f", i, ffn, token if i == 0 else landed[0])
        token = tok if i > 0 else token + tok
        if i + 1 < DEPTH:
            state_next, tok = gather_start(f"{i + 1}c", i + 1, core, tok)
            token = token + tok
        P["ln1_g"] = P["ln1_g"] + token[0, 0]

        def late_ffn(x1, i=i, state_ffn=state_ffn):
            return layer_params(i, ffn, _exchange_wait(f"gather_wait{i}f", state_ffn, x1))

        h, sv = _layer_fwd(i, h, W, P, bias_tabs, late0 if i == 0 else None, late_ffn)
        Ws.append(W)
        Ps.append(dict(P, ln1_g=_row(ln1_g[i])))
        saved.append(sv)
        if i + 1 < DEPTH:
            landed = _exchange_wait(f"gather_wait{i + 1}c", state_next, h)

    def loss_bwd(x_, t_, g_):
        lval, vjp = jax.vjp(_loss_fn, x_, t_, g_)
        dx_, _, dg_ = vjp(jnp.ones_like(lval))
        return (dx_,), (dg_, jnp.broadcast_to(lval, (1, 128)))

    dh, g_final, loss_part = _row_call("loss", loss_bwd, D, 1, [_rows(h, D), _rows(loss_target.reshape(S, D), D)],
                                       [_p_row(_row(final_g), D)], [F32], [_a_row(D, D), _a_row(128, 128)])
    loss = lax.psum(loss_part[0, 0], MESH_AXES)

    grads = {n: [None] * DEPTH for n in WEIGHTS if n not in ("rel_bias", "final_g")}
    g_rel = jnp.zeros((REL_BUCKETS, 18), F32)
    slots = [dict() for _ in range(DEPTH)]
    pending = []
    for i in reversed(range(DEPTH)):
        started = []

        def start_group(tag, group, G, after, i=i, started=started):
            parts = [_device_blocks(f"blocks_{n}{i}", n, G[n]) for n in group]
            state, token = _exchange_start(f"scatter_start{i}{tag}", parts, True, after)
            started.append((i, tag, group, state))
            return token[0, 0]

        dh, G = _layer_bwd(i, dh, saved[i], Ws[i], Ps[i], bias_tabs, onehots,
                           lambda G: start_group("f", ffn, G, G["ffn_conv_b"]),
                           lambda G: start_group("c", core, G, G["b_gate"]))
        for j, tag, group, state in pending:
            slots[j].update(zip(group, _exchange_wait(f"scatter_wait{j}{tag}", state, dh)))
        pending = started
        g_rel = g_rel + G.pop("rel_bias")
        for n, g in G.items():
            grads[n][i] = g
    grad_x = dh.reshape(1, S, D)
    local = {n: jnp.stack(grads[n]) for n in grads if n not in SHARDED}
    local["rel_bias"] = g_rel
    local["final_g"] = g_final[0]
    out = {}

    def pack(d):
        flat = jnp.concatenate([d[n].reshape(-1).astype(F32) for n in REPLICATED])
        rows = -(-flat.shape[0] // (8 * 128)) * 8
        return jnp.pad(flat, (0, rows * 128 - flat.shape[0])).reshape(rows, 128)

    (rep_slots,) = _exchange("gather_small_grads", [pack(local)], scatter=False)
    rep = _adamw("adamw_small", rep_slots, pack(wts), pack(mom), pack(var))
    off = 0
    for n in REPLICATED:
        sz = int(np.prod(wts[n].shape))
        out[n] = [t.reshape(-1)[off:off + sz].reshape(wts[n].shape) for t in rep]
        off += sz

    def flat2(n):
        shp = wts[n].shape
        r, c = int(np.prod(shp[:-1])), shp[-1]
        return r, c, wts[n].reshape(r, c), mom[n].reshape(r, c), var[n].reshape(r, c)

    chain = {}
    done = rep[0][0, 0]
    for n in names:
        if n in MATMUL_WEIGHTS:
            r, c, w2, m2, v2 = flat2(n)
            res = None
            for i in (3, 2, 1):
                res = _adamw(f"adamw_{n}{i}", slots[i][n].reshape(N_DEV, r // DEPTH, c), w2, m2, v2,
                             first_row=i * (r // DEPTH), prev=res)
            chain[n] = res
            done = done + res[0][-1, 0]
    for j, tag, group, state in pending:
        slots[j].update(zip(group, _exchange_wait(f"scatter_wait{j}{tag}", state, done.reshape(1, 1))))
    for n in names:
        r, c, w2, m2, v2 = flat2(n)
        if n in MATMUL_WEIGHTS:
            res = _adamw(f"adamw_{n}0", slots[0][n].reshape(N_DEV, r // DEPTH, c), w2, m2, v2, first_row=0, prev=chain[n])
        else:
            stacked = jnp.stack([slots[i][n] for i in range(DEPTH)], axis=1)
            res = _adamw("adamw_" + n, stacked.reshape(N_DEV, r, c), w2, m2, v2)
        out[n] = [t.reshape(wts[n].shape) for t in res]

    return (loss, grad_x, *[out[n][0] for n in WEIGHTS], *[out[n][1] for n in WEIGHTS],
            *[out[n][2] for n in WEIGHTS], *[out[n][3] for n in WEIGHTS])
```

```python
import functools
import math

import numpy as np
import jax
import jax.numpy as jnp
from jax import lax
from jax.experimental import pallas as pl
from jax.experimental.pallas import tpu as pltpu

F32 = jnp.float32
BF16 = jnp.bfloat16

N_DEV = 8
MESH_AXES = ("x", "y", "c")
S = 4096
D = 1024
DEPTH = 4
HEAD_DIM = 64
ATT_W = 1152
ATT_GW = 384
ATT_GROUPS = ((128, 1), (512, 4), (2048, 16))
ATT_BLOCK = 128
REL_BUCKETS = 32
REL_MAX_DISTANCE = 2048
POOL_WINDOWS = (2, 4, 8, 16)
SSD_HEADS = 16
SSD_CHUNK = 128
SSD_XBC = 1536
D_FF = 2816
IN_WIDTH = 10128
EPS = 1e-6
NEG = -1e30

OFF_GATE, OFF_POOL, OFF_Z, OFF_XBC, OFF_DT, OFF_QKV = 0, 3072, 4096, 5120, 6656, 6912
NP = 10368
DT_PAD = 128
QKV_W = 3 * 2 * HEAD_DIM

ADAM_LR, ADAM_B1, ADAM_B2, ADAM_EPS, ADAM_WD, ADAM_STEP = 0.001, 0.9, 0.999, 1e-08, 0.01, 10

VMEM_LIMIT = 52 * 1024 * 1024


def _cparams(sem=None):
    return pltpu.CompilerParams(dimension_semantics=sem, vmem_limit_bytes=VMEM_LIMIT)


def _dot(a, b, ca, cb):
    return lax.dot_general(a.astype(BF16), b.astype(BF16), (((ca,), (cb,)), ((), ())), preferred_element_type=F32)


@jax.custom_vjp
def _mm(a, b):
    return _dot(a, b, 1, 0)


def _mm_fwd(a, b):
    return _mm(a, b), (a, b)


def _mm_bwd(res, g):
    a, b = res
    return _dot(g, b, 1, 1).astype(a.dtype), _dot(a, g, 0, 0).astype(b.dtype)


_mm.defvjp(_mm_fwd, _mm_bwd)


@jax.custom_vjp
def _mm_nt(a, b):
    return _dot(a, b, 1, 1)


def _mm_nt_fwd(a, b):
    return _mm_nt(a, b), (a, b)


def _mm_nt_bwd(res, g):
    a, b = res
    return _dot(g, b, 1, 0).astype(a.dtype), _dot(g, a, 0, 0).astype(b.dtype)


_mm_nt.defvjp(_mm_nt_fwd, _mm_nt_bwd)


@jax.custom_vjp
def _mm_tn(a, b):
    return _dot(a, b, 0, 0)


def _mm_tn_fwd(a, b):
    return _mm_tn(a, b), (a, b)


def _mm_tn_bwd(res, g):
    a, b = res
    return _dot(b, g, 1, 1).astype(a.dtype), _dot(a, g, 1, 0).astype(b.dtype)


_mm_tn.defvjp(_mm_tn_fwd, _mm_tn_bwd)


def _shift_impl(x, j):
    n = x.shape[0]
    if j == 0:
        return x
    r = pltpu.roll(x, j % n, axis=0)
    t = lax.broadcasted_iota(jnp.int32, x.shape, 0)
    mask = (t >= j) if j > 0 else (t < n + j)
    return jnp.where(mask, r, 0.0)


@functools.partial(jax.custom_vjp, nondiff_argnums=(1,))
def _shift(x, j):
    return _shift_impl(x, j)


_shift.defvjp(lambda x, j: (_shift_impl(x, j), None), lambda j, _, g: (_shift_impl(g, -j),))


def _tri(lower):
    r = lax.broadcasted_iota(jnp.int32, (SSD_CHUNK, SSD_CHUNK), 0)
    c = lax.broadcasted_iota(jnp.int32, (SSD_CHUNK, SSD_CHUNK), 1)
    return (r >= c) if lower else (r <= c)


def _dot_hi(a, b):
    return lax.dot_general(a, b, (((1,), (0,)), ((), ())), precision=lax.Precision.HIGHEST,
                           preferred_element_type=F32)


@jax.custom_vjp
def _cumsum_rows(a):
    return _dot_hi(_tri(True).astype(F32), a)


_cumsum_rows.defvjp(lambda a: (_cumsum_rows(a), None), lambda _, g: (_dot_hi(_tri(False).astype(F32), g),))


@jax.custom_vjp
def _softplus(x):
    return jnp.maximum(x, 0.0) + jnp.log(1.0 + jnp.exp(-jnp.abs(x)))


_softplus.defvjp(lambda x: (_softplus(x), x), lambda x, g: (g * jax.nn.sigmoid(x),))


def _silu(x):
    return x * jax.nn.sigmoid(x)


def _rows(arr, cw, off=0, lead=None, roff=0):
    return (arr, cw, off, lead, roff)


def _tiled(name, fn, grid, tm, rows, params, outs, accs=(), out_roff=0, prev_outs=None, into=None):
    into = into or {}
    ncol, nrow = grid
    in_specs, operands = [], []
    for arr, cw, off, lead, roff in rows:
        if lead is None:
            in_specs.append(pl.BlockSpec((tm, cw), functools.partial(lambda j, i, off, roff: (roff + i, off + j),
                                                                     off=off, roff=roff)))
        else:
            in_specs.append(pl.BlockSpec((None, tm, cw), functools.partial(
                lambda j, i, off, lead, roff: (lead, roff + i, off + j), off=off, lead=lead, roff=roff)))
        operands.append(arr)
    for arr, bs, im in params:
        in_specs.append(pl.BlockSpec(bs, im))
        operands.append(arr)
    out_specs, out_shape = [], []
    for k, (n_rows, cw, dt) in enumerate(outs):
        _, coff, total = into.get(k, (None, 0, ncol * cw))
        out_specs.append(pl.BlockSpec((tm, cw), functools.partial(lambda j, i, r, c: (r + i, c + j), r=out_roff, c=coff)))
        out_shape.append(jax.ShapeDtypeStruct((n_rows, total), dt))
    for shape, bs, im in accs:
        out_specs.append(pl.BlockSpec(bs, im))
        out_shape.append(jax.ShapeDtypeStruct(shape, F32))
    n_in, n_out = len(operands), len(outs)
    aliases = {}
    earlier = dict(enumerate(prev_outs)) if prev_outs is not None else {}
    earlier.update({k: v[0] for k, v in into.items() if v[0] is not None})
    for k, p in sorted(earlier.items()):
        aliases[len(operands)] = k
        in_specs.append(pl.BlockSpec(memory_space=pl.ANY))
        operands.append(p)

    n_all = len(operands)

    def body(*refs):
        vals = [r[...] for r in refs[:n_in]]
        o_vals, a_vals = fn(*vals)
        for r, v in zip(refs[n_all:n_all + n_out], o_vals):
            r[...] = v.astype(r.dtype)
        i = pl.program_id(1)
        for r, v in zip(refs[n_all + n_out:], a_vals):
            @pl.when(i == 0)
            def _(r=r, v=v):
                r[...] = v.astype(r.dtype)

            @pl.when(i > 0)
            def _(r=r, v=v):
                r[...] += v.astype(r.dtype)

    res = pl.pallas_call(body, grid=grid, in_specs=in_specs, out_specs=out_specs, out_shape=out_shape, name=name,
                         input_output_aliases=aliases, compiler_params=_cparams(("arbitrary", "arbitrary")))(*operands)
    return list(res)


def _with_vjp(fn, n_prim, want_out, want_acc):
    def f(*args):
        prim, g = args[:n_prim], args[n_prim:]
        outs, vjp = jax.vjp(lambda *a: fn(*a), *prim)
        d = vjp(tuple(gi.astype(o.dtype) for gi, o in zip(g, outs)))
        return tuple(d[k] for k in want_out), tuple(d[k] for k in want_acc)
    return f


def _p_row(arr, cw, off=0):
    return (arr, (1, cw), functools.partial(lambda j, i, off: (0, off + j), off=off))


def _a_row(n, cw):
    return ((1, n), (1, cw), lambda j, i: (0, j))


def _pick(n, cap, mult):
    best = None
    for t in range(mult, min(n, cap) + 1, mult):
        if n % t == 0:
            best = t
    return best if best is not None else n


def _matmul(name, a, b, mode, add=None, out_dtype=F32):
    if mode == "nn":
        (M, K), N = a.shape, b.shape[1]
    elif mode == "nt":
        (M, K), N = a.shape, b.shape[0]
    else:
        (K, M), N = a.shape, b.shape[1]
    tn = _pick(N, 1536, 128)
    k_cap = 2048 if mode == "tn" else 3456
    tk = K if K <= k_cap else _pick(K, k_cap, 128)
    nk = K // tk
    tm = _pick(M, 1408, 128) if mode == "tn" else _pick(M, 1024, 8)
    a_bytes, b_bytes = a.size * a.dtype.itemsize, b.size * b.dtype.itemsize
    swap = nk == 1 and a_bytes * (N // tn) + b_bytes < b_bytes * (M // tm) + a_bytes
    ij = (lambda g0, g1: (g1, g0)) if swap else (lambda g0, g1: (g0, g1))

    def spec(block, index):
        return pl.BlockSpec(block, lambda g0, g1, k: index(*ij(g0, g1), k))

    if mode == "nn":
        a_spec = spec((tm, tk), lambda i, j, k: (i, k))
        b_spec = spec((tk, tn), lambda i, j, k: (k, j))
        ca, cb = 1, 0
    elif mode == "nt":
        a_spec = spec((tm, tk), lambda i, j, k: (i, k))
        b_spec = spec((tn, tk), lambda i, j, k: (j, k))
        ca, cb = 1, 1
    else:
        a_spec = spec((tk, tm), lambda i, j, k: (k, i))
        b_spec = spec((tk, tn), lambda i, j, k: (k, j))
        ca, cb = 0, 0
    in_specs, operands = [a_spec, b_spec], [a, b]
    if add is not None:
        in_specs.append(spec((tm, tn), lambda i, j, k: (i, j)))
        operands.append(add)

    def finish(r, refs, o_ref):
        if add is not None:
            r = r + refs[2][...]
        o_ref[...] = r.astype(o_ref.dtype)

    def body_single(*refs):
        finish(_dot(refs[0][...], refs[1][...], ca, cb), refs, refs[-1])

    def body_multi(*refs):
        o_ref, acc_ref = refs[-2], refs[-1]
        k = pl.program_id(2)
        d = _dot(refs[0][...], refs[1][...], ca, cb)

        @pl.when(k == 0)
        def _():
            acc_ref[...] = d

        @pl.when(jnp.logical_and(k > 0, k < nk - 1))
        def _():
            acc_ref[...] += d

        @pl.when(k == nk - 1)
        def _():
            finish(acc_ref[...] + d, refs, o_ref)

    grid = (N // tn, M // tm, nk) if swap else (M // tm, N // tn, nk)
    return pl.pallas_call(
        body_single if nk == 1 else body_multi, grid=grid, in_specs=in_specs,
        out_specs=spec((tm, tn), lambda i, j, k: (i, j)),
        out_shape=jax.ShapeDtypeStruct((M, N), out_dtype),
        scratch_shapes=[] if nk == 1 else [pltpu.VMEM((tm, tn), F32)], name=name,
        compiler_params=_cparams(("parallel", "parallel", "arbitrary")))(*operands)


def _seg_copies(segs, c):
    out = []
    for lo, hi, dst in segs:
        n = lo
        while n < hi:
            p = n // c
            w = min(hi, (p + 1) * c) - n
            out.append((p, n - p * c, w, dst + n - lo))
            n += w
    return out


def _col_assemble(name, blocks, copies, zeros, n_out):
    _, R, c = blocks.shape
    tm = R if R <= 128 else 128

    def body(b_ref, o_ref):
        for p, s, w, d in copies:
            o_ref[:, d:d + w] = b_ref[p, :, s:s + w]
        for lo, hi in zeros:
            o_ref[:, lo:hi] = jnp.zeros((tm, hi - lo), o_ref.dtype)

    return pl.pallas_call(
        body, grid=(R // tm,), in_specs=[pl.BlockSpec((N_DEV, tm, c), lambda i: (0, i, 0))],
        out_specs=pl.BlockSpec((tm, n_out), lambda i: (i, 0)),
        out_shape=jax.ShapeDtypeStruct((R, n_out), blocks.dtype), name=name, compiler_params=_cparams(("parallel",)))(blocks)


def _col_split(name, full, copies, c, dtype):
    R, n = full.shape
    tm = R if R <= 128 else 128

    def body(f_ref, o_ref):
        for p, s, w, d in copies:
            o_ref[p, :, s:s + w] = f_ref[:, d:d + w].astype(dtype)

    return pl.pallas_call(
        body, grid=(R // tm,), in_specs=[pl.BlockSpec((tm, n), lambda i: (i, 0))],
        out_specs=pl.BlockSpec((N_DEV, tm, c), lambda i: (0, i, 0)),
        out_shape=jax.ShapeDtypeStruct((N_DEV, R, c), dtype), name=name, compiler_params=_cparams(("parallel",)))(full)


def _rmsnorm_fn(x, g):
    x = x.astype(F32)
    return (x * lax.rsqrt(jnp.mean(x * x, axis=-1, keepdims=True) + EPS) * g,)


def _gate_merge_fn(g0, g1, g2, ya, yb, yc, b0, b1, b2):
    return (jax.nn.sigmoid(g0 + b0) * ya + jax.nn.sigmoid(g1 + b1) * yb + jax.nn.sigmoid(g2 + b2) * yc,)


def _gated_norm_fn(y, z, w):
    t = y * _silu(z)
    return (t * lax.rsqrt(jnp.mean(t * t, axis=-1, keepdims=True) + EPS) * w,)


def _att_merge_fn(o0, o1, o2, l0, l1, l2):
    m = lax.stop_gradient(jnp.maximum(jnp.maximum(l0, l1), l2))
    e0, e1, e2 = jnp.exp(l0 - m), jnp.exp(l1 - m), jnp.exp(l2 - m)
    return ((e0 * o0 + e1 * o1 + e2 * o2) / (e0 + e1 + e2),)


def _loss_fn(x, tgt, g):
    (y,) = _rmsnorm_fn(x, g)
    err = y - tgt
    return 0.5 * jnp.sum(jnp.mean(err * err, axis=-1, keepdims=True), axis=0, keepdims=True)


def _pool_fn(x, wg, scale):
    g = pl.program_id(0)
    s2 = x + _shift(x, 1)
    s4 = s2 + _shift(s2, 2)
    s8 = s4 + _shift(s4, 4)
    s16 = s8 + _shift(s8, 8)
    win = ((g == 0).astype(F32) * s2 + (g == 1).astype(F32) * s4 + (g == 2).astype(F32) * s8
           + (g == 3).astype(F32) * s16)
    t = lax.broadcasted_iota(jnp.int32, (x.shape[0], 1), 0) + 1
    cnt = jnp.minimum(t, jnp.left_shift(2, g)).astype(F32)
    d = win / cnt - x
    return (_mm(d, wg.reshape(256, 256)) * scale,)


def _dwconv(x, taps, b):
    k = len(taps)
    y = taps[k - 1] * x + b
    for i in range(k - 1):
        y = y + taps[i] * _shift(x, k - 1 - i)
    return y


def _ssd_conv_fn(x, w0, w1, w2, w3, b):
    return (_silu(_dwconv(x, (w0, w1, w2, w3), b)),)


def _ffn_act_fn(xa, xv, a0, a1, a2, ab, v0, v1, v2, vb):
    xa, xv = xa.astype(F32), xv.astype(F32)
    return (_silu(_dwconv(xa, (a0, a1, a2), ab)) * _dwconv(xv, (v0, v1, v2), vb),)


@jax.custom_vjp
def _halves(x):
    return x[:ATT_BLOCK], x[ATT_BLOCK:]


_halves.defvjp(lambda x: (_halves(x), None), lambda _, g: (jnp.concatenate([g[0], g[1]], axis=0),))


def _att_block(q, kp, kc, vp, vc, bpa, bpb, bca, bcb, prev_ok):
    n = ATT_BLOCK
    lane = lax.broadcasted_iota(jnp.int32, (1, 2 * HEAD_DIM), 1)
    ma = (lane < HEAD_DIM).astype(F32)
    mb = 1.0 - ma
    q = q.astype(F32) * (1.0 / math.sqrt(HEAD_DIM))
    q2 = jnp.concatenate([q * ma, q * mb], axis=0)
    qi = lax.broadcasted_iota(jnp.int32, (2 * n, n), 0) & (n - 1)
    kj = lax.broadcasted_iota(jnp.int32, (2 * n, n), 1)
    sp = jnp.where(jnp.logical_and(kj >= qi, prev_ok), _mm_nt(q2, kp) + jnp.concatenate([bpa, bpb], axis=0), NEG)
    sc = jnp.where(kj <= qi, _mm_nt(q2, kc) + jnp.concatenate([bca, bcb], axis=0), NEG)
    m = lax.stop_gradient(jnp.maximum(jnp.max(sp, axis=1, keepdims=True), jnp.max(sc, axis=1, keepdims=True)))
    pp = jnp.exp(sp - m)
    pc = jnp.exp(sc - m)
    l = jnp.sum(pp, axis=1, keepdims=True) + jnp.sum(pc, axis=1, keepdims=True)
    oa, ob = _halves((_mm(pp, vp) + _mm(pc, vc)) / l)
    la, lb = _halves((m + jnp.log(l)) * jnp.ones((1, 2 * HEAD_DIM), F32))
    return oa * ma + ob * mb, la * ma + lb * mb


def _att_slab(dil):
    nbk = max(1, 8 // dil)
    t = ATT_BLOCK * dil * nbk
    return nbk, t, S // t


def _att_in_specs(gi, t):
    def spec(which, prev):
        col = OFF_QKV // 128 + gi * 9 + which

        def index(p, j, col=col, prev=prev):
            jj = jnp.minimum(j, S // t - 1)
            return (jnp.maximum(jj - 1, 0) if prev else jj, col + 3 * p)
        return pl.BlockSpec((t, 2 * HEAD_DIM), index)
    return [spec(0, False), spec(1, False), spec(1, True), spec(2, False), spec(2, True)]


def _bias_specs():
    return [pl.BlockSpec((None, ATT_BLOCK, ATT_BLOCK), functools.partial(lambda p, j, hh: (2 * p + hh, 0, 0), hh=hh))
            for hh in (0, 1)]


def _att_units(dil, nbk, body):
    def per_residue(r, carry):
        for b in range(nbk):
            rows = pl.ds(b * ATT_BLOCK * dil + r, ATT_BLOCK, stride=dil)
            prev = pl.ds(((b - 1) % nbk) * ATT_BLOCK * dil + r, ATT_BLOCK, stride=dil)
            body(b, rows, prev, b > 0)
        return carry
    if dil == 1:
        per_residue(0, 0)
    else:
        lax.fori_loop(0, dil, per_residue, 0, unroll=min(dil, 8))


def _att_fwd(name, proj, gi, bias_p, bias_c):
    dil = ATT_GROUPS[gi][1]
    nbk, t, ns = _att_slab(dil)
    bsp = _bias_specs()
    out_spec = pl.BlockSpec((t, 2 * HEAD_DIM), lambda p, j: (j, p))

    def body(q_ref, kc_ref, kp_ref, vc_ref, vp_ref, bpa, bpb, bca, bcb, o_ref, l_ref):
        first = pl.program_id(1) == 0
        biases = (bpa[...], bpb[...], bca[...], bcb[...])

        def unit(b, rows, prev, in_slab):
            kp = kc_ref[prev, :] if in_slab else kp_ref[prev, :]
            vp = vc_ref[prev, :] if in_slab else vp_ref[prev, :]
            prev_ok = True if in_slab else jnp.logical_not(first)
            o, lse = _att_block(q_ref[rows, :], kp, kc_ref[rows, :], vp, vc_ref[rows, :], *biases, prev_ok)
            o_ref[rows, :] = o
            l_ref[rows, :] = lse

        _att_units(dil, nbk, unit)

    shp = jax.ShapeDtypeStruct((S, ATT_GW), F32)
    return pl.pallas_call(
        body, grid=(3, ns), in_specs=_att_in_specs(gi, t) + [bsp[0], bsp[1], bsp[0], bsp[1]],
        out_specs=[out_spec, out_spec], out_shape=[shp, shp], name=name,
        compiler_params=_cparams(("arbitrary",) * 2))(proj, proj, proj, proj, proj, bias_p, bias_p, bias_c, bias_c)


def _att_bwd(name, proj, gi, bias_p, bias_c, do, dl, dproj):
    dil = ATT_GROUPS[gi][1]
    nbk, t, ns = _att_slab(dil)
    bsp = _bias_specs()
    blk = (t, 2 * HEAD_DIM)
    cur = pl.BlockSpec(blk, lambda p, j: (jnp.minimum(j, ns - 1), p))
    done = pl.BlockSpec((t, QKV_W), lambda p, j: (jnp.maximum(j - 1, 0), OFF_QKV // QKV_W + gi * 3 + p))
    gsp = pl.BlockSpec((None, ATT_BLOCK, ATT_BLOCK), lambda p, j: (p, 0, 0))

    def body(q_ref, kc_ref, kp_ref, vc_ref, vp_ref, bpa, bpb, bca, bcb, do_ref, dl_ref, _,
             dqkv_ref, gpa, gpb, gca, gcb, accq, acck, accv):
        j = pl.program_id(1)
        mine, other = acck.at[j % 2], acck.at[1 - j % 2]
        mine_v, other_v = accv.at[j % 2], accv.at[1 - j % 2]
        dq_ref, other_q = accq.at[j % 2], accq.at[1 - j % 2]

        @pl.when(j == 0)
        def _():
            for g in (gpa, gpb, gca, gcb):
                g[...] = jnp.zeros_like(g)
            other[...] = jnp.zeros_like(other)
            other_v[...] = jnp.zeros_like(other_v)
            other_q[...] = jnp.zeros_like(other_q)

        @pl.when(j < ns)
        def _():
            mine[...] = jnp.zeros_like(mine)
            mine_v[...] = jnp.zeros_like(mine_v)
            biases = (bpa[...], bpb[...], bca[...], bcb[...])

            def unit(b, rows, prev, in_slab):
                kp = kc_ref[prev, :] if in_slab else kp_ref[prev, :]
                vp = vc_ref[prev, :] if in_slab else vp_ref[prev, :]
                prev_ok = True if in_slab else j > 0
                prim = (q_ref[rows, :], kp, kc_ref[rows, :], vp, vc_ref[rows, :]) + biases
                _, vjp = jax.vjp(lambda *a: _att_block(*a, prev_ok), *prim)
                dq, dkp, dkc, dvp, dvc, dpa, dpb, dca, dcb = vjp((do_ref[rows, :], dl_ref[rows, :]))
                dq_ref[rows, :] = dq
                mine[rows, :] += dkc
                mine_v[rows, :] += dvc
                tgt, tgt_v = (mine, mine_v) if in_slab else (other, other_v)
                tgt[prev, :] += dkp
                tgt_v[prev, :] += dvp
                gpa[...] += dpa
                gpb[...] += dpb
                gca[...] += dca
                gcb[...] += dcb

            _att_units(dil, nbk, unit)

        w = 2 * HEAD_DIM
        dqkv_ref[:, 0:w] = other_q[...].astype(BF16)
        dqkv_ref[:, w:2 * w] = other[...].astype(BF16)
        dqkv_ref[:, 2 * w:3 * w] = other_v[...].astype(BF16)

    gshp = jax.ShapeDtypeStruct((3, ATT_BLOCK, ATT_BLOCK), F32)
    res = pl.pallas_call(
        body, grid=(3, ns + 1),
        in_specs=_att_in_specs(gi, t) + [bsp[0], bsp[1], bsp[0], bsp[1], cur, cur, pl.BlockSpec(memory_space=pl.ANY)],
        out_specs=[done, gsp, gsp, gsp, gsp],
        out_shape=[jax.ShapeDtypeStruct((S, NP), BF16), gshp, gshp, gshp, gshp],
        input_output_aliases={11: 0},
        scratch_shapes=[pltpu.VMEM((2,) + blk, F32)] * 3, name=name,
        compiler_params=_cparams(("arbitrary",) * 2))(proj, proj, proj, proj, proj, bias_p, bias_p, bias_c, bias_c, do, dl,
                                                      dproj)
    dproj, gpa, gpb, gca, gcb = res
    heads = lambda a, b: jnp.stack([a, b], axis=1).reshape(6, ATT_BLOCK, ATT_BLOCK)
    return dproj, heads(gpa, gpb), heads(gca, gcb)


N_PAIR = SSD_HEADS // 2


def _ssd_chunk(xs, bs, cs_in, dt_raw, hs, a_row, dtb_row, ds):
    lane = lax.broadcasted_iota(jnp.int32, (1, 128), 1)
    row = lax.broadcasted_iota(jnp.int32, (128, 1), 0)
    tril = _tri(True)
    dt = _softplus(dt_raw + dtb_row)
    acs = _cumsum_rows(dt * a_row)
    acs_t = acs.T
    gmat = [_mm_nt(cs_in[g], bs[g]) for g in range(2)]
    lo = lane < HEAD_DIM
    lo_r = row < HEAD_DIM
    last = (row == SSD_CHUNK - 1).astype(F32)
    ys, hn = [], []
    for p in range(N_PAIR):
        g = p // (N_PAIR // 2)
        col, dtc, mm, clast = [], [], [], []
        for hh in range(2):
            h = 2 * p + hh
            oh = (lane == h).astype(F32)
            c_col = jnp.sum(acs * oh, axis=1, keepdims=True)
            c_row = jnp.sum(acs_t * (row == h).astype(F32), axis=0, keepdims=True)
            col.append(c_col)
            dtc.append(jnp.sum(dt * oh, axis=1, keepdims=True))
            clast.append(jnp.sum(c_col * last, axis=0, keepdims=True))
            mm.append(gmat[g] * jnp.exp(jnp.where(tril, c_col - c_row, NEG)))
        x = xs[p]
        xd = x * jnp.where(lo, dtc[0], dtc[1])
        y = jnp.where(lo, _mm(mm[0], xd), _mm(mm[1], xd))
        y = y + jnp.where(lo, jnp.exp(col[0]), jnp.exp(col[1])) * _mm_nt(cs_in[g], hs[p])
        ys.append(y + ds[p] * x)
        dec = jnp.where(lo, jnp.exp(clast[0] - col[0]), jnp.exp(clast[1] - col[1]))
        hn.append(hs[p] * jnp.where(lo_r, jnp.exp(clast[0]), jnp.exp(clast[1])) + _mm_tn(xd * dec, bs[g]))
    return tuple(ys), tuple(hn)


def _ssd_load(xbc_ref, dt_ref, a_ref, dtb_ref, d_ref):
    xs = tuple(xbc_ref[:, 128 * p:128 * (p + 1)] for p in range(N_PAIR))
    bs = tuple(xbc_ref[:, D + 128 * g:D + 128 * (g + 1)] for g in range(2))
    cs = tuple(xbc_ref[:, D + 256 + 128 * g:D + 256 + 128 * (g + 1)] for g in range(2))
    ds = tuple(d_ref[:, 128 * p:128 * (p + 1)] for p in range(N_PAIR))
    return xs, bs, cs, dt_ref[...], a_ref[...], dtb_ref[...], ds


def _ssd_fwd(name, xbc_c, proj, a_row, dtb_row, d_exp):
    nc = S // SSD_CHUNK
    prow = lambda n: pl.BlockSpec((1, n), lambda c: (0, 0))

    def body(xbc_ref, dt_ref, a_ref, dtb_ref, d_ref, y_ref, st_ref, h_ref):
        @pl.when(pl.program_id(0) == 0)
        def _():
            h_ref[...] = jnp.zeros_like(h_ref)

        xs, bs, cs, dt_raw, a, dtb, ds = _ssd_load(xbc_ref, dt_ref, a_ref, dtb_ref, d_ref)
        hs = tuple(h_ref[p] for p in range(N_PAIR))
        ys, hn = _ssd_chunk(xs, bs, cs, dt_raw, hs, a, dtb, ds)
        for p in range(N_PAIR):
            y_ref[:, 128 * p:128 * (p + 1)] = ys[p]
            st_ref[p] = hs[p]
            h_ref[p] = hn[p]

    return pl.pallas_call(
        body, grid=(nc,),
        in_specs=[pl.BlockSpec((SSD_CHUNK, SSD_XBC), lambda c: (c, 0)),
                  pl.BlockSpec((SSD_CHUNK, DT_PAD), lambda c: (c, OFF_DT // DT_PAD)),
                  prow(128), prow(128), prow(D)],
        out_specs=[pl.BlockSpec((SSD_CHUNK, D), lambda c: (c, 0)),
                   pl.BlockSpec((None, N_PAIR, 128, 128), lambda c: (c, 0, 0, 0))],
        out_shape=[jax.ShapeDtypeStruct((S, D), F32), jax.ShapeDtypeStruct((nc, N_PAIR, 128, 128), F32)],
        scratch_shapes=[pltpu.VMEM((N_PAIR, 128, 128), F32)], name=name,
        compiler_params=_cparams(("arbitrary",)))(xbc_c, proj, a_row, dtb_row, d_exp)


def _ssd_bwd(name, xbc_c, proj, states, dy, a_row, dtb_row, d_exp, dproj):
    nc = S // SSD_CHUNK
    prow = lambda n: pl.BlockSpec((1, n), lambda i: (0, 0))
    rc = lambda i: nc - 1 - i

    def body(xbc_ref, dt_ref, st_ref, dy_ref, a_ref, dtb_ref, d_ref, _, dxbc_ref, ddt_ref, da_ref, ddtb_ref, dd_ref, e_ref):
        i = pl.program_id(0)

        @pl.when(i == 0)
        def _():
            e_ref[...] = jnp.zeros_like(e_ref)
            da_ref[...] = jnp.zeros_like(da_ref)
            ddtb_ref[...] = jnp.zeros_like(ddtb_ref)
            dd_ref[...] = jnp.zeros_like(dd_ref)

        xs, bs, cs, dt_raw, a, dtb, ds = _ssd_load(xbc_ref, dt_ref, a_ref, dtb_ref, d_ref)
        hs = tuple(st_ref[p] for p in range(N_PAIR))
        _, vjp = jax.vjp(_ssd_chunk, xs, bs, cs, dt_raw, hs, a, dtb, ds)
        dys = tuple(dy_ref[:, 128 * p:128 * (p + 1)] for p in range(N_PAIR))
        es = tuple(e_ref[p] for p in range(N_PAIR))
        dxs, dbs, dcs, ddt, dhs, da, ddtb, dds = vjp((dys, es))
        for p in range(N_PAIR):
            dxbc_ref[:, 128 * p:128 * (p + 1)] = dxs[p]
            e_ref[p] = dhs[p]
            dd_ref[:, 128 * p:128 * (p + 1)] += dds[p]
        for g in range(2):
            dxbc_ref[:, D + 128 * g:D + 128 * (g + 1)] = dbs[g]
            dxbc_ref[:, D + 256 + 128 * g:D + 256 + 128 * (g + 1)] = dcs[g]
        ddt_ref[:, :DT_PAD] = ddt.astype(BF16)
        ddt_ref[:, DT_PAD:] = jnp.zeros((SSD_CHUNK, OFF_QKV - OFF_DT - DT_PAD), BF16)
        da_ref[...] += da
        ddtb_ref[...] += ddtb

    dt_w = OFF_QKV - OFF_DT
    return pl.pallas_call(
        body, grid=(nc,),
        in_specs=[pl.BlockSpec((SSD_CHUNK, SSD_XBC), lambda i: (rc(i), 0)),
                  pl.BlockSpec((SSD_CHUNK, DT_PAD), lambda i: (rc(i), OFF_DT // DT_PAD)),
                  pl.BlockSpec((None, N_PAIR, 128, 128), lambda i: (rc(i), 0, 0, 0)),
                  pl.BlockSpec((SSD_CHUNK, D), lambda i: (rc(i), 0)),
                  prow(128), prow(128), prow(D), pl.BlockSpec(memory_space=pl.ANY)],
        out_specs=[pl.BlockSpec((SSD_CHUNK, SSD_XBC), lambda i: (rc(i), 0)),
                   pl.BlockSpec((SSD_CHUNK, dt_w), lambda i: (rc(i), OFF_DT // dt_w)),
                   prow(128), prow(128), prow(D)],
        out_shape=[jax.ShapeDtypeStruct((S, SSD_XBC), F32), jax.ShapeDtypeStruct((S, NP), BF16),
                   jax.ShapeDtypeStruct((1, 128), F32), jax.ShapeDtypeStruct((1, 128), F32),
                   jax.ShapeDtypeStruct((1, D), F32)],
        input_output_aliases={7: 1},
        scratch_shapes=[pltpu.VMEM((N_PAIR, 128, 128), F32)], name=name,
        compiler_params=_cparams(("arbitrary",)))(xbc_c, proj, states, dy, a_row, dtb_row, d_exp, dproj)


def _exchange(name, arrays, scatter):
    n = len(arrays)
    flips = [(dx, dy, dc) for dx in (0, 1) for dy in (0, 1) for dc in (0, 1) if dx or dy or dc]

    def body(*refs):
        ins, outs = refs[:n], refs[n:2 * n]
        send_sems, recv_sems, loc_sems = refs[2 * n:]
        x, y, c = lax.axis_index("x"), lax.axis_index("y"), lax.axis_index("c")
        me = 4 * x + 2 * y + c
        peers = []
        for dx, dy, dc in flips:
            px, py, pc = (1 - x if dx else x), (1 - y if dy else y), (1 - c if dc else c)
            peers.append(((px, py, pc), 4 * px + 2 * py + pc))

        def remote(k, j, landed_from):
            dev, pid = peers[j]
            src = ins[k].at[pid] if scatter else ins[k]
            return pltpu.make_async_remote_copy(
                src_ref=src, dst_ref=outs[k].at[landed_from], send_sem=send_sems.at[k, j], recv_sem=recv_sems.at[k, j],
                device_id=dev, device_id_type=pl.DeviceIdType.MESH)

        local = [pltpu.make_async_copy(ins[k].at[me] if scatter else ins[k], outs[k].at[me], loc_sems.at[k])
                 for k in range(n)]
        for cp in local:
            cp.start()
        for k in range(n):
            for j in range(len(flips)):
                remote(k, j, me).start()
        for cp in local:
            cp.wait()
        for k in range(n):
            for j in range(len(flips)):
                remote(k, j, me).wait_send()
                remote(k, j, peers[j][1]).wait_recv()

    hbm = pl.BlockSpec(memory_space=pltpu.HBM)
    out_shape = [jax.ShapeDtypeStruct(a.shape if scatter else (N_DEV,) + a.shape, a.dtype) for a in arrays]
    res = pl.pallas_call(
        body, in_specs=[hbm] * n, out_specs=[hbm] * n, out_shape=out_shape, name=name,
        scratch_shapes=[pltpu.SemaphoreType.DMA((n, len(flips))), pltpu.SemaphoreType.DMA((n, len(flips))),
                        pltpu.SemaphoreType.DMA((n,))])(*arrays)
    return list(res)


def _gather_chip_once(name, block):
    def body(x_ref, out_ref, send_sems, recv_sems, loc_sem):
        x, y, c = lax.axis_index("x"), lax.axis_index("y"), lax.axis_index("c")
        me, sibling = (x, y, c), (x, y, 1 - c)
        chips = [(1 - x, y), (x, 1 - y), (1 - x, 1 - y)]

        def slot(px, py, pc):
            return out_ref.at[4 * px + 2 * py + pc]

        def copy(k, blk, to, src=None):
            return pltpu.make_async_remote_copy(
                src_ref=slot(*blk) if src is None else src, dst_ref=slot(*blk), send_sem=send_sems.at[k],
                recv_sem=recv_sems.at[k], device_id=to, device_id_type=pl.DeviceIdType.MESH)

        mine = pltpu.make_async_copy(x_ref, slot(*me), loc_sem)
        mine.start()
        first = [copy(0, me, sibling, src=x_ref)] + [copy(1 + j, me, (*chip, c), src=x_ref) for j, chip in enumerate(chips)]
        for cp in first:
            cp.start()
        passed = [copy(4 + j, (*chip, c), sibling) for j, chip in enumerate(chips)]
        for j, chip in enumerate(chips):
            copy(1 + j, (*chip, c), me).wait_recv()
            passed[j].start()
        copy(0, sibling, me).wait_recv()
        for j, chip in enumerate(chips):
            copy(4 + j, (*chip, 1 - c), me).wait_recv()
        for cp in first + passed:
            cp.wait_send()
        mine.wait()

    hbm = pl.BlockSpec(memory_space=pltpu.HBM)
    return pl.pallas_call(
        body, in_specs=[hbm], out_specs=hbm, out_shape=jax.ShapeDtypeStruct((N_DEV,) + block.shape, block.dtype), name=name,
        scratch_shapes=[pltpu.SemaphoreType.DMA((N_DEV - 1,)), pltpu.SemaphoreType.DMA((N_DEV - 1,)),
                        pltpu.SemaphoreType.DMA(())])(block)


def _peer_copies(ins, lands, send_sems, recv_sems, loc_sems, scatter):
    n = len(ins)
    flips = [(dx, dy, dc) for dx in (0, 1) for dy in (0, 1) for dc in (0, 1) if dx or dy or dc]
    x, y, c = lax.axis_index("x"), lax.axis_index("y"), lax.axis_index("c")
    me = 4 * x + 2 * y + c
    peers = []
    for dx, dy, dc in flips:
        px, py, pc = (1 - x if dx else x), (1 - y if dy else y), (1 - c if dc else c)
        peers.append(((px, py, pc), 4 * px + 2 * py + pc))

    def remote(k, j, slot):
        dev, pid = peers[j]
        return pltpu.make_async_remote_copy(
            src_ref=ins[k].at[pid] if scatter else ins[k], dst_ref=lands[k].at[slot],
            send_sem=send_sems.at[k * N_FLIP + j], recv_sem=recv_sems.at[k * N_FLIP + j],
            device_id=dev, device_id_type=pl.DeviceIdType.MESH)

    local = [pltpu.make_async_copy(ins[k].at[me] if scatter else ins[k], lands[k].at[me], loc_sems.at[k])
             for k in range(n)]
    pairs = [(k, j) for k in range(n) for j in range(len(flips))]
    sent = lambda k, j: remote(k, j, me)
    landed = lambda k, j: remote(k, j, peers[j][1])
    return local, pairs, sent, landed


_HBM = pl.BlockSpec(memory_space=pltpu.HBM)
_SEM = pl.BlockSpec(memory_space=pltpu.SEMAPHORE)
N_FLIP = N_DEV - 1


def _exchange_start(name, arrays, scatter, after):
    n = len(arrays)
    arrays = [pltpu.with_memory_space_constraint(a, pltpu.HBM) for a in arrays]
    lands = [pltpu.with_memory_space_constraint(
        lax.empty(a.shape if scatter else (N_DEV,) + a.shape, a.dtype), pltpu.HBM) for a in arrays]

    def body(*refs):
        ins, lnd = refs[:n], refs[n:2 * n]
        send_sems, recv_sems, loc_sems = refs[2 * n + 1:2 * n + 4]
        token = refs[-1]
        local, pairs, sent, _ = _peer_copies(ins, lnd, send_sems, recv_sems, loc_sems, scatter)
        for cp in local:
            cp.start()
        for k, j in pairs:
            sent(k, j).start()
        token[...] = jnp.zeros_like(token)

    res = pl.pallas_call(
        body, name=name,
        in_specs=[_HBM] * (2 * n) + [pl.BlockSpec(memory_space=pl.ANY)],
        out_specs=[_SEM, _SEM, _SEM] + [_HBM] * (2 * n) + [pl.BlockSpec(memory_space=pltpu.VMEM)],
        out_shape=[pltpu.SemaphoreType.DMA((n * N_FLIP,)), pltpu.SemaphoreType.DMA((n * N_FLIP,)), pltpu.SemaphoreType.DMA((n,))]
        + [pltpu.HBM(a.shape, a.dtype) for a in arrays] + [pltpu.HBM(a.shape, a.dtype) for a in lands]
        + [jax.ShapeDtypeStruct((8, 128), F32)],
        input_output_aliases={k: 3 + k for k in range(2 * n)},
        compiler_params=pltpu.CompilerParams(has_side_effects=pltpu.SideEffectType.DATAFLOW_SIDE_EFFECTING),
    )(*arrays, *lands, after)
    return (res[:3], res[3:3 + n], res[3 + n:3 + 2 * n], scatter), res[-1]


def _exchange_wait(name, state, after):
    sems, ins_thru, lands_thru, scatter = state
    n = len(ins_thru)

    def body(*refs):
        ins, lnd = refs[:n], refs[n:2 * n]
        send_sems, recv_sems, loc_sems = refs[2 * n:2 * n + 3]
        local, pairs, sent, landed = _peer_copies(ins, lnd, send_sems, recv_sems, loc_sems, scatter)
        for cp in local:
            cp.wait()
        for k, j in pairs:
            sent(k, j).wait_send()
            landed(k, j).wait_recv()

    res = pl.pallas_call(
        body, name=name,
        in_specs=[_HBM] * (2 * n) + [_SEM, _SEM, _SEM] + [pl.BlockSpec(memory_space=pl.ANY)],
        out_specs=[_HBM] * (2 * n),
        out_shape=[pltpu.HBM(a.shape, a.dtype) for a in ins_thru] + [pltpu.HBM(a.shape, a.dtype) for a in lands_thru],
        input_output_aliases={k: k for k in range(2 * n)},
        compiler_params=pltpu.CompilerParams(has_side_effects=pltpu.SideEffectType.DATAFLOW_SIDE_EFFECTING),
    )(*ins_thru, *lands_thru, *sems, after)
    return list(res[n:])


def _adamw_fn(*vals):
    slots, (w, m, v) = vals[:N_DEV], vals[N_DEV:]
    g = slots[0].astype(F32)
    for s in slots[1:]:
        g = g + s.astype(F32)
    m2 = ADAM_B1 * m + (1.0 - ADAM_B1) * g
    v2 = ADAM_B2 * v + (1.0 - ADAM_B2) * (g * g)
    m_hat = m2 / (1.0 - ADAM_B1 ** ADAM_STEP)
    v_hat = v2 / (1.0 - ADAM_B2 ** ADAM_STEP)
    delta = -ADAM_LR * (m_hat / (jnp.sqrt(v_hat) + ADAM_EPS) + ADAM_WD * w)
    return (g, delta, m2, v2), ()


def _adamw(name, slots, w, m, v, first_row=0, prev=None):
    R, C = slots.shape[1:]
    tm = R if R <= 128 else _pick(R, 256, 8)
    rows = ([_rows(slots, C, lead=s) for s in range(N_DEV)]
            + [_rows(a, C, roff=first_row // tm) for a in (w, m, v)])
    return _tiled(name, _adamw_fn, (1, R // tm), tm, rows, [], [(w.shape[0], C, F32)] * 4,
                  out_roff=first_row // tm, prev_outs=prev)


def _bucket_onehots():
    out = []
    qi = jnp.arange(ATT_BLOCK)[:, None]
    kj = jnp.arange(ATT_BLOCK)[None, :]
    max_exact = REL_BUCKETS // 2
    for _, dil in ATT_GROUPS:
        parts = []
        for rel in (qi + ATT_BLOCK - kj, qi - kj):
            dist = jnp.clip(rel, 0, None) * dil
            nf = jnp.maximum(dist, 1).astype(F32)
            large = max_exact + (jnp.log(nf / max_exact) / math.log(REL_MAX_DISTANCE / max_exact)
                                 * (REL_BUCKETS - max_exact)).astype(jnp.int32)
            large = jnp.minimum(large, REL_BUCKETS - 1)
            bucket = jnp.where(dist < max_exact, dist, large)
            parts.append((bucket[:, :, None] == jnp.arange(REL_BUCKETS)[None, None, :]).astype(F32))
        out.append(jnp.stack(parts))
    return out


SHARDED = ("w_in", "w_a", "pool_w", "w_b", "ssd_conv_w", "w_c", "w_o", "ffn_w_up", "ffn_conv_w", "ffn_w_down")
MATMUL_WEIGHTS = ("w_in", "w_a", "pool_w", "w_b", "w_c", "w_o", "ffn_w_up", "ffn_w_down")
ROW_SHARDED = ("w_b", "w_c", "w_o", "ffn_w_down")
W_IN_SEGS = tuple(
    (which * ATT_W + unit * 128, which * ATT_W + (unit + 1) * 128, OFF_QKV + unit * QKV_W + which * 128)
    for unit in range(9) for which in range(3)
) + ((3456, 4480, OFF_POOL), (4480, 5504, OFF_Z), (5504, 7040, OFF_XBC), (7040, 7056, OFF_DT), (7056, IN_WIDTH, OFF_GATE))
FFN_SEGS = tuple((h * D_FF + j * 128, h * D_FF + (j + 1) * 128, j * 256 + h * 128)
                 for j in range(D_FF // 128) for h in range(2))
COL_SHARDED = {
    "w_in": (IN_WIDTH // N_DEV, W_IN_SEGS, ((OFF_DT + SSD_HEADS, OFF_QKV),), NP),
    "w_a": (D // N_DEV, ((0, D, 0),), (), D),
    "ffn_w_up": (2 * D_FF // N_DEV, FFN_SEGS, (), 2 * D_FF),
    "ssd_conv_w": (SSD_XBC // N_DEV, ((0, SSD_XBC, 0),), (), SSD_XBC),
    "ffn_conv_w": (2 * D_FF // N_DEV, FFN_SEGS, (), 2 * D_FF),
}
REPLICATED = ("rel_bias", "ln1_g", "b_gate", "pool_scale", "ssd_conv_b", "ssd_dt_bias", "ssd_a_log", "ssd_d",
              "ssd_norm_w", "ln2_g", "ffn_conv_b", "final_g")
WEIGHTS = ("rel_bias", "ln1_g", "w_in", "b_gate", "w_a", "pool_w", "pool_scale", "w_b", "ssd_conv_w", "ssd_conv_b",
           "ssd_dt_bias", "ssd_a_log", "ssd_d", "ssd_norm_w", "w_c", "w_o", "ln2_g", "ffn_w_up", "ffn_conv_w",
           "ffn_conv_b", "ffn_w_down", "final_g")


def _local_weight(name, n, blocks):
    if n in COL_SHARDED:
        c, segs, zeros, width = COL_SHARDED[n]
        return _col_assemble(name, blocks, _seg_copies(segs, c), zeros, width)
    if n in ROW_SHARDED:
        return blocks.reshape(-1, blocks.shape[-1])
    return blocks


def _device_blocks(name, n, g):
    if n in COL_SHARDED:
        c, segs, _, _ = COL_SHARDED[n]
        return _col_split(name, g, _seg_copies(segs, c), c, BF16)
    if n in ROW_SHARDED:
        return g.reshape(N_DEV, g.shape[0] // N_DEV, g.shape[1]).astype(BF16)
    return g.astype(BF16)


def _row(v, n=None):
    v = v.reshape(1, -1)
    if n is not None and v.shape[1] < n:
        v = jnp.pad(v, ((0, 0), (0, n - v.shape[1])))
    return v


RT = 512


def _row_call(name, fn, cw, ncol, rows, params, outs, accs=(), into=None):
    return _tiled(name, fn, (ncol, S // RT), RT, rows, params, [(S, cw, dt) for dt in outs], accs, into=into)


def _col_call(name, fn, tc, ncol, rows, params, outs, accs=(), into=None):
    return _tiled(name, fn, (ncol, 1), S, rows, params, [(S, tc, dt) for dt in outs], accs, into=into)


def _fwd_only(fn):
    return lambda *a: (fn(*a), ())


def _layer_fwd(i, x, W, P, bias_tabs, late=None, late_ffn=None):
    sv = {"x": x}
    (u,) = _row_call(f"ln1_f{i}", _fwd_only(_rmsnorm_fn), D, 1, [_rows(x, D)], [_p_row(P["ln1_g"], D)], [BF16])
    proj = _matmul(f"inproj_f{i}", u, W["w_in"], "nn")
    sv["u"], sv["proj"] = u, proj

    os_, ls_ = [], []
    for gi in range(len(ATT_GROUPS)):
        o, lse = _att_fwd(f"att_f{i}_{gi}", proj, gi, bias_tabs[gi][0], bias_tabs[gi][1])
        os_.append(o)
        ls_.append(lse)
    sv["att_o"], sv["att_l"] = os_, ls_
    (att,) = _row_call(f"attmerge_f{i}", _fwd_only(_att_merge_fn), ATT_GW, 1,
                       [_rows(t, ATT_GW) for t in os_ + ls_], [], [BF16])
    if late is not None:
        W2, P2 = late(att)
        W.update(W2)
        P.update(P2)
    y_a = _matmul(f"wa_f{i}", att, W["w_a"], "nn", out_dtype=BF16)
    sv["att"], sv["y_a"] = att, y_a

    pool_params = [(W["pool_w"], (N_DEV, None, 32, 256), lambda j, i_: (0, j, 0, 0)), _p_row(P["pool_scale"], 256)]
    (yb_pre,) = _col_call(f"pool_f{i}", _fwd_only(_pool_fn), 256, 4, [_rows(proj, 256, OFF_POOL // 256)],
                          pool_params, [BF16])
    y_b = _matmul(f"wb_f{i}", yb_pre, W["w_b"], "nn", out_dtype=BF16)
    sv["yb_pre"], sv["y_b"] = yb_pre, y_b

    conv_params = [_p_row(P["ssd_conv_w"][k], 128) for k in range(4)] + [_p_row(P["ssd_conv_b"], 128)]
    (xbc_c,) = _col_call(f"ssdconv_f{i}", _fwd_only(_ssd_conv_fn), 128, SSD_XBC // 128,
                         [_rows(proj, 128, OFF_XBC // 128)], conv_params, [F32])
    y_ssd, states = _ssd_fwd(f"ssd_f{i}", xbc_c, proj, P["a_row"], P["dtb_row"], P["d_exp"])
    (yc_pre,) = _row_call(f"ssdnorm_f{i}", _fwd_only(_gated_norm_fn), 512, 2,
                          [_rows(y_ssd, 512), _rows(proj, 512, OFF_Z // 512)], [_p_row(P["ssd_norm_w"], 512)], [BF16])
    y_c = _matmul(f"wc_f{i}", yc_pre, W["w_c"], "nn", out_dtype=BF16)
    sv["xbc_c"], sv["states"], sv["y_ssd"], sv["yc_pre"], sv["y_c"] = xbc_c, states, y_ssd, yc_pre, y_c

    gate_rows = [_rows(proj, D, k) for k in range(3)] + [_rows(t, D) for t in (y_a, y_b, y_c)]
    gate_params = [_p_row(P["b_gate"], D, k) for k in range(3)]
    (merged,) = _row_call(f"gate_f{i}", _fwd_only(_gate_merge_fn), D, 1, gate_rows, gate_params, [BF16])
    x1 = _matmul(f"wo_f{i}", merged, W["w_o"], "nn", add=x)
    sv["merged"], sv["x1"] = merged, x1

    if late_ffn is not None:
        W2, P2 = late_ffn(x1)
        W.update(W2)
        P.update(P2)
    (u2,) = _row_call(f"ln2_f{i}", _fwd_only(_rmsnorm_fn), D, 1, [_rows(x1, D)], [_p_row(P["ln2_g"], D)], [BF16])
    up = _matmul(f"up_f{i}", u2, W["ffn_w_up"], "nn", out_dtype=BF16)
    (act,) = _col_call(f"ffnact_f{i}", lambda *a: (_ffn_act_fn(*_ffn_halves(a)), ()), 128, D_FF // 128,
                       [_rows(up, 256)], _ffn_params(P), [BF16])
    x2 = _matmul(f"down_f{i}", act, W["ffn_w_down"], "nn", add=x1)
    sv["u2"], sv["up"], sv["act"] = u2, up, act
    return x2, sv


def _ffn_params(P):
    return [_p_row(P["ffn_conv_w"][k], 256) for k in range(3)] + [_p_row(_interleave_ffn(P["ffn_conv_b"]), 256)]


def _ffn_halves(vals):
    (xa, xv), (a0, v0), (a1, v1), (a2, v2), (ab, vb) = ((t[:, :128], t[:, 128:]) for t in vals)
    return xa, xv, a0, a1, a2, ab, v0, v1, v2, vb


def _interleave_ffn(row):
    return row.reshape(2, D_FF // 128, 128).transpose(1, 0, 2).reshape(1, 2 * D_FF)


def _deinterleave_ffn(row):
    return row.reshape(D_FF // 128, 2, 128).transpose(1, 0, 2).reshape(1, 2 * D_FF)


def _layer_bwd(i, dx2, dx2_b, sv, W, P, bias_tabs, onehots, on_ffn_grads, on_sharded_grads):
    G = {}
    x, proj, x1 = sv["x"], sv["proj"], sv["x1"]

    dact = _matmul(f"down_bx{i}", dx2_b, W["ffn_w_down"], "nt", out_dtype=BF16)
    G["ffn_w_down"] = _matmul(f"down_bw{i}", sv["act"], dx2_b, "tn", out_dtype=BF16)
    nb = D_FF // 128
    up = sv["up"]
    def ffn_bwd(x2, t0, t1, t2, b2, dact_):
        (dxa, dxv), (a0, a1, a2, ab, v0, v1, v2, vb) = _with_vjp(_ffn_act_fn, 10, (0, 1), tuple(range(2, 10)))(
            *_ffn_halves((x2, t0, t1, t2, b2)), dact_)
        pair = lambda a, v: jnp.concatenate([a, v], axis=1)
        return (pair(dxa, dxv),), (pair(a0, v0), pair(a1, v1), pair(a2, v2), pair(ab, vb))

    dup, t0, t1, t2, tb = _col_call(
        f"ffnact_b{i}", ffn_bwd, 256, nb, [_rows(up, 256)], _ffn_params(P) + [_rows_as_param(dact, 128)],
        [BF16], [_a_row(2 * D_FF, 256)] * 4)
    G["ffn_conv_w"] = jnp.concatenate([t0, t1, t2], axis=0)
    G["ffn_conv_b"] = _deinterleave_ffn(tb)[0]
    du2 = _matmul(f"up_bx{i}", dup, W["ffn_w_up"], "nt")
    G["ffn_w_up"] = _matmul(f"up_bw{i}", sv["u2"], dup, "tn", out_dtype=BF16)

    def norm_bwd(x_, g_, du_, dres):
        (dxn,), (dg,) = _with_vjp(_rmsnorm_fn, 2, (0,), (1,))(x_, g_, du_)
        return (dxn + dres, dxn + dres), (dg,)

    ln2_g = P["ln2_g"] + on_ffn_grads(G)
    (dx1, dx1_b), (G["ln2_g"],) = _split_res(_row_call(
        f"ln2_b{i}", lambda x_, du_, dres, g_: norm_bwd(x_, g_, du_, dres), D, 1,
        [_rows(x1, D), _rows(du2, D), _rows(dx2, D)], [_p_row(ln2_g, D)], [F32, BF16], [_a_row(D, D)]), 2)

    dmerged = _matmul(f"wo_bx{i}", dx1_b, W["w_o"], "nt", out_dtype=BF16)
    G["w_o"] = _matmul(f"wo_bw{i}", sv["merged"], dx1_b, "tn", out_dtype=BF16)
    def gate_bwd(g0, g1, g2, ya, yb, yc, dm, b0, b1, b2):
        (d0, d1, d2, da, db, dc), dbias = _with_vjp(_gate_merge_fn, 9, (0, 1, 2, 3, 4, 5), (6, 7, 8))(
            g0, g1, g2, ya, yb, yc, b0, b1, b2, dm)
        dgate = jnp.concatenate([t.astype(BF16) for t in (d0, d1, d2)], axis=1)
        return (dgate, da, db, dc), dbias

    gate_rows = [_rows(proj, D, k) for k in range(3)] + [_rows(sv[t], D) for t in ("y_a", "y_b", "y_c")]
    dproj, dya, dyb, dyc, db0, db1, db2 = _tiled(
        f"gate_b{i}", gate_bwd, (1, S // RT), RT, gate_rows + [_rows(dmerged, D)],
        [_p_row(P["b_gate"], D, k) for k in range(3)], [(S, 3 * D, BF16)] + [(S, D, BF16)] * 3, [_a_row(D, D)] * 3,
        into={0: (None, 0, NP)})
    G["b_gate"] = jnp.concatenate([db0, db1, db2], axis=1)[0]

    dyc_pre = _matmul(f"wc_bx{i}", dyc, W["w_c"], "nt", out_dtype=BF16)
    G["w_c"] = _matmul(f"wc_bw{i}", sv["yc_pre"], dyc, "tn", out_dtype=BF16)

    def gnorm_bwd(y_, z_, dy_, w_):
        return _with_vjp(_gated_norm_fn, 3, (0, 1), (2,))(y_, z_, w_, dy_)

    dy_ssd, dproj, dnw = _row_call(
        f"ssdnorm_b{i}", gnorm_bwd, 512, 2,
        [_rows(sv["y_ssd"], 512), _rows(proj, 512, OFF_Z // 512), _rows(dyc_pre, 512)],
        [_p_row(P["ssd_norm_w"], 512)], [F32, BF16], [_a_row(D, 512)], into={1: (dproj, OFF_Z // 512, NP)})
    G["ssd_norm_w"] = dnw[0]
    dxbc_c, dproj, da_row, ddtb_row, dd_exp = _ssd_bwd(f"ssd_b{i}", sv["xbc_c"], proj, sv["states"], dy_ssd,
                                                       P["a_row"], P["dtb_row"], P["d_exp"], dproj)
    a_vec = P["a_row"][0, :SSD_HEADS]
    G["ssd_a_log"] = da_row[0, :SSD_HEADS] * a_vec
    G["ssd_dt_bias"] = ddtb_row[0, :SSD_HEADS]
    G["ssd_d"] = dd_exp.reshape(SSD_HEADS, HEAD_DIM).sum(axis=1)
    conv_params = [_p_row(P["ssd_conv_w"][k], 128) for k in range(4)] + [_p_row(P["ssd_conv_b"], 128)]

    def conv_bwd(x_, dy_, w0, w1, w2, w3, b_):
        return _with_vjp(_ssd_conv_fn, 6, (0,), (1, 2, 3, 4, 5))(x_, w0, w1, w2, w3, b_, dy_)

    dproj, c0, c1, c2, c3, cb = _col_call(
        f"ssdconv_b{i}", conv_bwd, 128, SSD_XBC // 128, [_rows(proj, 128, OFF_XBC // 128), _rows(dxbc_c, 128)],
        conv_params, [BF16], [_a_row(SSD_XBC, 128)] * 5, into={0: (dproj, OFF_XBC // 128, NP)})
    G["ssd_conv_w"] = jnp.concatenate([c0, c1, c2, c3], axis=0)
    G["ssd_conv_b"] = cb[0]

    dyb_pre = _matmul(f"wb_bx{i}", dyb, W["w_b"], "nt", out_dtype=BF16)
    G["w_b"] = _matmul(f"wb_bw{i}", sv["yb_pre"], dyb, "tn", out_dtype=BF16)
    pool_params = [(W["pool_w"], (N_DEV, None, 32, 256), lambda j, i_: (0, j, 0, 0)), _p_row(P["pool_scale"], 256)]

    def pool_bwd(x_, dy_, wg, sc):
        return _with_vjp(_pool_fn, 3, (0,), (1, 2))(x_, wg.astype(F32), sc, dy_)

    dproj, dwg, dsc = _col_call(
        f"pool_b{i}", pool_bwd, 256, 4, [_rows(proj, 256, OFF_POOL // 256), _rows(dyb_pre, 256)], pool_params, [BF16],
        [((N_DEV, 4, 32, 256), (N_DEV, None, 32, 256), lambda j, i_: (0, j, 0, 0)), _a_row(D, 256)],
        into={0: (dproj, OFF_POOL // 256, NP)})
    G["pool_w"] = dwg
    G["pool_scale"] = dsc[0]

    datt = _matmul(f"wa_bx{i}", dya, W["w_a"], "nt", out_dtype=BF16)
    G["w_a"] = _matmul(f"wa_bw{i}", sv["att"], dya, "tn", out_dtype=BF16)

    def merge_bwd(o0, o1, o2, l0, l1, l2, da_):
        return _with_vjp(_att_merge_fn, 6, (0, 1, 2, 3, 4, 5), ())(o0, o1, o2, l0, l1, l2, da_)

    dol = _row_call(f"attmerge_b{i}", merge_bwd, ATT_GW, 1,
                    [_rows(t, ATT_GW) for t in sv["att_o"] + sv["att_l"]] + [_rows(datt, ATT_GW)], [], [F32] * 6)
    g_rel = jnp.zeros((REL_BUCKETS, 18), F32)
    for gi in range(len(ATT_GROUPS)):
        dproj, gbp, gbc = _att_bwd(f"att_b{i}_{gi}", proj, gi, bias_tabs[gi][0], bias_tabs[gi][1],
                                   dol[gi], dol[3 + gi], dproj)
        oh = onehots[gi]
        gt = (jnp.einsum("hqk,qkb->bh", gbp, oh[0], precision=lax.Precision.HIGHEST)
              + jnp.einsum("hqk,qkb->bh", gbc, oh[1], precision=lax.Precision.HIGHEST))
        g_rel = g_rel.at[:, gi * 6:(gi + 1) * 6].add(gt)
    G["rel_bias"] = g_rel

    du = _matmul(f"inproj_bx{i}", dproj, W["w_in"], "nt")
    G["w_in"] = _matmul(f"inproj_bw{i}", sv["u"], dproj, "tn", out_dtype=BF16)
    ln1_g = P["ln1_g"] + on_sharded_grads(G)
    (dx, dx_b), (G["ln1_g"],) = _split_res(_row_call(
        f"ln1_b{i}", lambda x_, du_, dres, g_: norm_bwd(x_, g_, du_, dres), D, 1,
        [_rows(x, D), _rows(du, D), _rows(dx1, D)], [_p_row(ln1_g, D)], [F32, BF16], [_a_row(D, D)]), 2)
    G["ln1_g"] = G["ln1_g"][0]
    G["ln2_g"] = G["ln2_g"][0]
    return dx, dx_b, G


def _rows_as_param(arr, cw):
    return (arr, (arr.shape[0], cw), lambda j, i: (0, j))


def _split_res(res, n_out):
    return tuple(res[:n_out]), tuple(res[n_out:])


def kernel(x, rel_bias, ln1_g, w_in, b_gate, w_a, pool_w, pool_scale, w_b, ssd_conv_w, ssd_conv_b, ssd_dt_bias, ssd_a_log, ssd_d, ssd_norm_w, w_c, w_o, ln2_g, ffn_w_up, ffn_conv_w, ffn_conv_b, ffn_w_down, final_g, loss_target, m_rel_bias, m_ln1_g, m_w_in, m_b_gate, m_w_a, m_pool_w, m_pool_scale, m_w_b, m_ssd_conv_w, m_ssd_conv_b, m_ssd_dt_bias, m_ssd_a_log, m_ssd_d, m_ssd_norm_w, m_w_c, m_w_o, m_ln2_g, m_ffn_w_up, m_ffn_conv_w, m_ffn_conv_b, m_ffn_w_down, m_final_g, v_rel_bias, v_ln1_g, v_w_in, v_b_gate, v_w_a, v_pool_w, v_pool_scale, v_w_b, v_ssd_conv_w, v_ssd_conv_b, v_ssd_dt_bias, v_ssd_a_log, v_ssd_d, v_ssd_norm_w, v_w_c, v_w_o, v_ln2_g, v_ffn_w_up, v_ffn_conv_w, v_ffn_conv_b, v_ffn_w_down, v_final_g):
    args = locals()
    wts = {n: args[n] for n in WEIGHTS}
    mom = {n: args["m_" + n] for n in WEIGHTS}
    var = {n: args["v_" + n] for n in WEIGHTS}
    names = list(SHARDED)

    onehots = _bucket_onehots()
    bias_tabs = []
    for gi in range(3):
        tab = rel_bias[:, gi * 6:(gi + 1) * 6]
        b = jnp.einsum("pqkb,bh->phqk", onehots[gi], tab, precision=lax.Precision.HIGHEST)
        bias_tabs.append((b[0], b[1]))

    def gather_start(tag, i, which, after):
        shards = [wts[n][i].astype(BF16) if n in MATMUL_WEIGHTS else wts[n][i] for n in which]
        return _exchange_start(f"gather_start{tag}", shards, False, after)

    def layer_params(i, which, landed):
        full = {n: _local_weight(f"local_{n}{i}", n, g) for n, g in zip(which, landed)}
        W = {n: full[n] for n in which if n in MATMUL_WEIGHTS}
        P = {n: [_row(full[n][k]) for k in range(full[n].shape[0])] for n in ("ssd_conv_w", "ffn_conv_w") if n in full}
        return W, P

    def replicated_params(i):
        return {"ln1_g": _row(ln1_g[i]), "ln2_g": _row(ln2_g[i]), "b_gate": _row(b_gate[i]),
                "pool_scale": _row(pool_scale[i]), "ssd_conv_b": _row(ssd_conv_b[i]),
                "ssd_norm_w": _row(ssd_norm_w[i]), "ffn_conv_b": _row(ffn_conv_b[i]),
                "a_row": _row(-jnp.exp(ssd_a_log[i]), 128), "dtb_row": _row(ssd_dt_bias[i], 128),
                "d_exp": _row(jnp.repeat(ssd_d[i], HEAD_DIM))}

    h = x.reshape(S, D)
    saved, Ws, Ps = [], [], []
    ffn = ["ffn_w_up", "ffn_conv_w", "ffn_w_down"]
    core = [n for n in names if n not in ffn]
    first, rest = ["w_in"], [n for n in core if n != "w_in"]
    landed_first = [_gather_chip_once("gather_w_in0", w_in[0].astype(BF16))]
    state_rest, token = gather_start("0r", 0, rest, landed_first[0])
    landed = None

    def late0(att):
        return layer_params(0, rest, _exchange_wait("gather_wait0r", state_rest, att))

    for i in range(DEPTH):
        P = replicated_params(i)
        W, P1 = layer_params(0, first, landed_first) if i == 0 else layer_params(i, core, landed)
        P.update(P1)
        state_ffn, tok = gather_start(f"{i}f", i, ffn, token if i == 0 else landed[0])
        token = tok if i > 0 else token + tok
        if i + 1 < DEPTH:
            state_next, tok = gather_start(f"{i + 1}c", i + 1, core, tok)
            token = token + tok
        P["ln1_g"] = P["ln1_g"] + token[0, 0]

        def late_ffn(x1, i=i, state_ffn=state_ffn):
            return layer_params(i, ffn, _exchange_wait(f"gather_wait{i}f", state_ffn, x1))

        h, sv = _layer_fwd(i, h, W, P, bias_tabs, late0 if i == 0 else None, late_ffn)
        Ws.append(W)
        Ps.append(dict(P, ln1_g=_row(ln1_g[i])))
        saved.append(sv)
        if i + 1 < DEPTH:
            landed = _exchange_wait(f"gather_wait{i + 1}c", state_next, h)

    def loss_bwd(x_, t_, g_):
        lval, vjp = jax.vjp(_loss_fn, x_, t_, g_)
        dx_, _, dg_ = vjp(jnp.ones_like(lval))
        return (dx_, dx_), (dg_, jnp.broadcast_to(lval, (1, 128)))

    dh, dh_b, g_final, loss_part = _row_call(
        "loss", loss_bwd, D, 1, [_rows(h, D), _rows(loss_target.reshape(S, D), D)],
        [_p_row(_row(final_g), D)], [F32, BF16], [_a_row(D, D), _a_row(128, 128)])
    loss = lax.psum(loss_part[0, 0], MESH_AXES)

    grads = {n: [None] * DEPTH for n in WEIGHTS if n not in ("rel_bias", "final_g")}
    g_rel = jnp.zeros((REL_BUCKETS, 18), F32)
    slots = [dict() for _ in range(DEPTH)]
    pending = []
    for i in reversed(range(DEPTH)):
        started = []

        def start_group(tag, group, G, after, i=i, started=started):
            parts = [_device_blocks(f"blocks_{n}{i}", n, G[n]) for n in group]
            state, token = _exchange_start(f"scatter_start{i}{tag}", parts, True, after)
            started.append((i, tag, group, state))
            return token[0, 0]

        dh, dh_b, G = _layer_bwd(i, dh, dh_b, saved[i], Ws[i], Ps[i], bias_tabs, onehots,
                           lambda G: start_group("f", ffn, G, G["ffn_conv_b"]),
                           lambda G: start_group("c", core, G, G["b_gate"]))
        for j, tag, group, state in pending:
            slots[j].update(zip(group, _exchange_wait(f"scatter_wait{j}{tag}", state, dh)))
        pending = started
        g_rel = g_rel + G.pop("rel_bias")
        for n, g in G.items():
            grads[n][i] = g
    grad_x = dh.reshape(1, S, D)
    local = {n: jnp.stack(grads[n]) for n in grads if n not in SHARDED}
    local["rel_bias"] = g_rel
    local["final_g"] = g_final[0]
    out = {}

    def pack(d):
        flat = jnp.concatenate([d[n].reshape(-1).astype(F32) for n in REPLICATED])
        rows = -(-flat.shape[0] // (8 * 128)) * 8
        return jnp.pad(flat, (0, rows * 128 - flat.shape[0])).reshape(rows, 128)

    (rep_slots,) = _exchange("gather_small_grads", [pack(local)], scatter=False)
    rep = _adamw("adamw_small", rep_slots, pack(wts), pack(mom), pack(var))
    off = 0
    for n in REPLICATED:
        sz = int(np.prod(wts[n].shape))
        out[n] = [t.reshape(-1)[off:off + sz].reshape(wts[n].shape) for t in rep]
        off += sz

    def flat2(n):
        shp = wts[n].shape
        r, c = int(np.prod(shp[:-1])), shp[-1]
        return r, c, wts[n].reshape(r, c), mom[n].reshape(r, c), var[n].reshape(r, c)

    chain = {}
    done = rep[0][0, 0]
    for n in names:
        if n in MATMUL_WEIGHTS:
            r, c, w2, m2, v2 = flat2(n)
            res = None
            for i in (3, 2, 1):
                res = _adamw(f"adamw_{n}{i}", slots[i][n].reshape(N_DEV, r // DEPTH, c), w2, m2, v2,
                             first_row=i * (r // DEPTH), prev=res)
            chain[n] = res
            done = done + res[0][-1, 0]
    for j, tag, group, state in pending:
        slots[j].update(zip(group, _exchange_wait(f"scatter_wait{j}{tag}", state, done.reshape(1, 1))))
    for n in names:
        r, c, w2, m2, v2 = flat2(n)
        if n in MATMUL_WEIGHTS:
            res = _adamw(f"adamw_{n}0", slots[0][n].reshape(N_DEV, r // DEPTH, c), w2, m2, v2, first_row=0, prev=chain[n])
        else:
            stacked = jnp.stack([slots[i][n] for i in range(DEPTH)], axis=1)
            res = _adamw("adamw_" + n, stacked.reshape(N_DEV, r, c), w2, m2, v2)
        out[n] = [t.reshape(wts[n].shape) for t in res]

    return (loss, grad_x, *[out[n][0] for n in WEIGHTS], *[out[n][1] for n in WEIGHTS],
            *[out[n][2] for n in WEIGHTS], *[out[n][3] for n in WEIGHTS])
```
